```python
import math
import jax, jax.numpy as jnp
from jax import lax
import numpy as np

D_MODEL = 1024
BATCH = 4
SEQ = 8192
DEPTH = 2

N_BRANCH = 4
BRANCH_WIDTH = 256
NORM_EPS = 1e-6
NEG_INF = -1e30

GLA_HEADS = 4
GLA_DK = 32
GLA_DV = 64
GLA_LOWRANK = 16
GLA_TAU = 16.0
GLA_CHUNK = 64

SWA_HEADS = 4
SWA_KV_HEADS = 2
SWA_HD = 64
SWA_WINDOW = 128

MOBA_HEADS = 4
MOBA_HD = 64
MOBA_BLOCK = 256
MOBA_TOPK = 3
MOBA_QCHUNK = 64

RET_HEADS = 4
RET_DK = 32
RET_DV = 64
RET_CHUNK = 128
RET_ROPE_BASE = 10000.0

N_GROUPS = 4
EXPERTS_PER_GROUP = 8
N_EXPERTS = N_GROUPS * EXPERTS_PER_GROUP
EXPERT_TOPK = 2
EXPERT_HIDDEN = 256
MOE_BLOCK = 256

IN_LAYOUT = (
    ('gla_q', GLA_HEADS * GLA_DK), ('gla_k', GLA_HEADS * GLA_DK), ('gla_v', GLA_HEADS * GLA_DV),
    ('gla_r', GLA_HEADS * GLA_DV), ('gla_a', GLA_LOWRANK),
    ('swa_q', SWA_HEADS * SWA_HD), ('swa_k', SWA_KV_HEADS * SWA_HD), ('swa_v', SWA_KV_HEADS * SWA_HD),
    ('moba_q', MOBA_HEADS * MOBA_HD), ('moba_k', MOBA_HEADS * MOBA_HD), ('moba_v', MOBA_HEADS * MOBA_HD),
    ('ret_q', RET_HEADS * RET_DK), ('ret_k', RET_HEADS * RET_DK), ('ret_v', RET_HEADS * RET_DV),
    ('ret_g', RET_HEADS * RET_DV),
    ('gates', N_BRANCH * D_MODEL),
)
IN_WIDTH = sum(w for _, w in IN_LAYOUT)

kernel_name = 'hybrid_gla_swa_moba_retention_hmoe'


def rms_norm(x, g):
    xf = x.astype(jnp.float32)
    y = xf * lax.rsqrt(jnp.mean(xf * xf, axis=-1, keepdims=True) + NORM_EPS)
    return (y * g.astype(jnp.float32)).astype(x.dtype)


def split_columns(z):
    parts, off = {}, 0
    for name, width in IN_LAYOUT:
        parts[name] = z[..., off:off + width]
        off += width
    return parts


def gla_mixer(q, k, v, r, a_lr, w_a2, b_a, norm_g):
    B, S, _ = q.shape
    H, DK, DV, C = GLA_HEADS, GLA_DK, GLA_DV, GLA_CHUNK
    N = S // C
    f32 = jnp.float32

    def chunked(t, d):
        return t.astype(f32).reshape(B, N, C, H, d).transpose(0, 3, 1, 2, 4)

    log_a = jax.nn.log_sigmoid((a_lr @ w_a2 + b_a).astype(f32)) / GLA_TAU
    b = jnp.cumsum(chunked(log_a, DK), axis=3)
    b_last = b[:, :, :, -1:, :]
    qc = chunked(q, DK) * (DK ** -0.5)
    kc = chunked(k, DK)
    vc = chunked(v, DV)
    q_dec = qc * jnp.exp(b)
    k_dec = kc * jnp.exp(-b)
    causal = jnp.tril(jnp.ones((C, C), dtype=bool))
    att = jnp.where(causal, jnp.einsum('bhnid,bhnjd->bhnij', q_dec, k_dec), 0.0)
    o = jnp.einsum('bhnij,bhnje->bhnie', att, vc)
    kv = jnp.einsum('bhnjd,bhnje->bhnde', kc * jnp.exp(b_last - b), vc)
    decay = jnp.exp(b_last[:, :, :, 0, :])

    def step(state, inp):
        dec, kv_n = inp
        return dec[..., None] * state + kv_n, state

    _, prev = lax.scan(step, jnp.zeros((B, H, DK, DV), f32),
                       (jnp.moveaxis(decay, 2, 0), jnp.moveaxis(kv, 2, 0)))
    o = o + jnp.einsum('bhnid,bhnde->bhnie', q_dec, jnp.moveaxis(prev, 0, 2))
    o = o.transpose(0, 2, 3, 1, 4).reshape(B, S, H, DV)
    o = rms_norm(o, norm_g) * jax.nn.silu(r.astype(f32).reshape(B, S, H, DV))
    return o.reshape(B, S, H * DV).astype(q.dtype)


def swa_mixer(q, k, v, sinks):
    B, S, _ = q.shape
    H, KV, HD, W = SWA_HEADS, SWA_KV_HEADS, SWA_HD, SWA_WINDOW
    G = H // KV
    NB = S // W
    f32 = jnp.float32
    qb = q.reshape(B, NB, W, KV, G, HD).transpose(0, 3, 4, 1, 2, 5)

    def band(t):
        t = t.reshape(B, NB, W, KV, HD).transpose(0, 3, 1, 2, 4)
        prev = jnp.pad(t, ((0, 0), (0, 0), (1, 0), (0, 0), (0, 0)))[:, :, :-1]
        return jnp.concatenate([prev, t], axis=3)

    kb, vb = band(k), band(v)
    s = jnp.einsum('bkgnqd,bknsd->bkgnqs', qb, kb).astype(f32) * (HD ** -0.5)
    qi = jnp.arange(W)[:, None]
    sj = jnp.arange(2 * W)[None, :]
    in_window = (sj > qi) & (sj <= qi + W)
    started = (jnp.arange(NB)[:, None, None] > 0) | (sj[None] >= W)
    s = jnp.where(in_window[None] & started, s, NEG_INF)
    sink = sinks.astype(f32).reshape(KV, G)[None, :, :, None, None]
    m = jnp.maximum(jnp.max(s, axis=-1), sink)
    p = jnp.exp(s - m[..., None])
    denom = jnp.sum(p, axis=-1) + jnp.exp(sink - m)
    o = jnp.einsum('bkgnqs,bknsd->bkgnqd', p, vb.astype(f32)) / denom[..., None]
    o = o.transpose(0, 3, 4, 1, 2, 5).reshape(B, S, H * HD)
    return o.astype(q.dtype)


def moba_mixer(q, k, v):
    B, S, _ = q.shape
    H, HD, BLK, QC = MOBA_HEADS, MOBA_HD, MOBA_BLOCK, MOBA_QCHUNK
    NBK = -(-S // BLK)
    SP = NBK * BLK
    NQ = SP // QC
    f32 = jnp.float32
    scale = HD ** -0.5

    def heads(t):
        t = t.reshape(B, S, H, HD).transpose(0, 2, 1, 3)
        return jnp.pad(t, ((0, 0), (0, 0), (0, SP - S), (0, 0)))

    qh, kh, vh = heads(q), heads(k), heads(v)
    kb = kh.reshape(B, H, NBK, BLK, HD)
    vb = vh.reshape(B, H, NBK, BLK, HD)
    k_mean = jnp.mean(kb.astype(f32), axis=3)
    gate = jnp.einsum('bhsd,bhnd->bhsn', qh.astype(f32), k_mean)
    q_blk = jnp.arange(SP) // BLK
    past = jnp.arange(NBK)[None, :] < q_blk[:, None]
    gate = jnp.where(past, gate, NEG_INF)
    n_sel = min(MOBA_TOPK, max(NBK - 1, 1))
    _, sel = lax.top_k(gate, n_sel)
    sel_valid = sel < q_blk[:, None]

    def chunks(t):
        return jnp.moveaxis(t.reshape((B, H, NQ, QC) + t.shape[3:]), 2, 0)

    b_ix = jnp.arange(B)[:, None, None, None]
    h_ix = jnp.arange(H)[None, :, None, None]

    def attend(args):
        qc, selc, validc, c = args
        blk = (c * QC) // BLK
        k_own = lax.dynamic_index_in_dim(kb, blk, axis=2, keepdims=False)
        v_own = lax.dynamic_index_in_dim(vb, blk, axis=2, keepdims=False)
        q_pos = c * QC + jnp.arange(QC)
        k_pos = blk * BLK + jnp.arange(BLK)
        s_own = jnp.einsum('bhqd,bhkd->bhqk', qc, k_own).astype(f32)
        s_own = jnp.where(k_pos[None, :] <= q_pos[:, None], s_own, NEG_INF)
        k_sel = kb[b_ix, h_ix, selc]
        v_sel = vb[b_ix, h_ix, selc]
        s_sel = jnp.einsum('bhqd,bhqnkd->bhqnk', qc, k_sel).astype(f32)
        s_sel = jnp.where(validc[..., None], s_sel, NEG_INF).reshape(B, H, QC, n_sel * BLK)
        p = jax.nn.softmax(jnp.concatenate([s_own, s_sel], axis=-1), axis=-1)
        o = jnp.einsum('bhqk,bhkd->bhqd', p[..., :BLK], v_own.astype(f32))
        o = o + jnp.einsum('bhqnk,bhqnkd->bhqd', p[..., BLK:].reshape(B, H, QC, n_sel, BLK),
                           v_sel.astype(f32))
        return o.astype(qc.dtype)

    out = lax.map(attend, (chunks(qh * scale), chunks(sel), chunks(sel_valid), jnp.arange(NQ)))
    out = jnp.moveaxis(out, 0, 2).reshape(B, H, SP, HD)[:, :, :S]
    return out.transpose(0, 2, 1, 3).reshape(B, S, H * HD)


def retention_mixer(q, k, v, g, norm_g):
    B, S, _ = q.shape
    H, DK, DV, C = RET_HEADS, RET_DK, RET_DV, RET_CHUNK
    N = S // C
    half = DK // 2
    f32 = jnp.float32
    pos = jnp.arange(S, dtype=f32)
    inv_freq = RET_ROPE_BASE ** (-jnp.arange(half, dtype=f32) * 2.0 / DK)
    ang = pos[:, None] * inv_freq[None, :]
    cos = jnp.cos(ang)[:, None, :]
    sin = jnp.sin(ang)[:, None, :]

    def rotate(t):
        t1, t2 = t[..., :half], t[..., half:]
        return jnp.concatenate([t1 * cos - t2 * sin, t1 * sin + t2 * cos], axis=-1)

    def chunked(t, d):
        return t.reshape(B, N, C, H, d).transpose(0, 3, 1, 2, 4)

    qc = chunked(rotate(q.astype(f32).reshape(B, S, H, DK)), DK)
    kc = chunked(rotate(k.astype(f32).reshape(B, S, H, DK)), DK) * (DK ** -0.5)
    vc = chunked(v.astype(f32).reshape(B, S, H, DV), DV)
    log_gamma = jnp.log1p(-jnp.exp2(-5.0 - jnp.arange(H, dtype=f32)))
    idx = jnp.arange(C, dtype=f32)
    rel = idx[:, None] - idx[None, :]
    decay_mat = jnp.where(rel >= 0, jnp.exp(jnp.maximum(rel, 0.0)[None] * log_gamma[:, None, None]), 0.0)
    att = jnp.einsum('bhnid,bhnjd->bhnij', qc, kc) * decay_mat[None, :, None]
    o = jnp.einsum('bhnij,bhnje->bhnie', att, vc)
    key_decay = jnp.exp((C - 1 - idx)[None, :] * log_gamma[:, None])
    query_decay = jnp.exp((idx + 1)[None, :] * log_gamma[:, None])
    chunk_decay = jnp.exp(C * log_gamma)
    kv = jnp.einsum('bhnjd,bhnje->bhnde', kc * key_decay[None, :, None, :, None], vc)

    def step(state, kv_n):
        return chunk_decay[None, :, None, None] * state + kv_n, state

    _, prev = lax.scan(step, jnp.zeros((B, H, DK, DV), f32), jnp.moveaxis(kv, 2, 0))
    o = o + jnp.einsum('bhnid,bhnde->bhnie', qc, jnp.moveaxis(prev, 0, 2)) * query_decay[None, :, None, :, None]
    o = o.transpose(0, 2, 3, 1, 4).reshape(B, S, H, DV)
    o = rms_norm(o, norm_g) * jax.nn.silu(g.astype(f32).reshape(B, S, H, DV))
    return o.reshape(B, S, H * DV).astype(q.dtype)


def token_mixer(u, w_in, gla_w_a2, gla_b_a, gla_norm_g, swa_sinks, ret_norm_g, w_branch, w_out):
    p = split_columns(u @ w_in)
    branches = (
        gla_mixer(p['gla_q'], p['gla_k'], p['gla_v'], p['gla_r'], p['gla_a'], gla_w_a2, gla_b_a, gla_norm_g),
        swa_mixer(p['swa_q'], p['swa_k'], p['swa_v'], swa_sinks),
        moba_mixer(p['moba_q'], p['moba_k'], p['moba_v']),
        retention_mixer(p['ret_q'], p['ret_k'], p['ret_v'], p['ret_g'], ret_norm_g),
    )
    gates = p['gates']
    merged = None
    for i, o in enumerate(branches):
        term = jax.nn.sigmoid(gates[..., i * D_MODEL:(i + 1) * D_MODEL]) * (o @ w_branch[i])
        merged = term if merged is None else merged + term
    return merged @ w_out


def hier_moe(u, w_group, b_group, w_expert, b_expert, w1, w3, w2):
    B, S, D = u.shape
    T = B * S
    TK = T * EXPERT_TOPK
    f32 = jnp.float32
    xt = u.reshape(T, D)
    p_group = jax.nn.softmax((xt @ w_group + b_group).astype(f32), axis=-1)
    g_sel = jnp.argmax(p_group, axis=-1)
    g_w = jnp.take_along_axis(p_group, g_sel[:, None], axis=1)[:, 0]
    e_logits = (xt @ w_expert + b_expert).astype(f32).reshape(T, N_GROUPS, EXPERTS_PER_GROUP)
    e_logits = jnp.take_along_axis(e_logits, g_sel[:, None, None], axis=1)[:, 0]
    top_logit, top_local = lax.top_k(e_logits, EXPERT_TOPK)
    top_w = jax.nn.softmax(top_logit, axis=-1) * g_w[:, None]
    expert_id = (g_sel[:, None] * EXPERTS_PER_GROUP + top_local).reshape(TK).astype(jnp.int32)
    weight = top_w.reshape(TK).astype(u.dtype)
    token_id = jnp.repeat(jnp.arange(T, dtype=jnp.int32), EXPERT_TOPK)

    n_blocks = -(-TK // MOE_BLOCK) + N_EXPERTS
    P = n_blocks * MOE_BLOCK
    order = jnp.argsort(expert_id)
    e_sorted = expert_id[order]
    counts = jnp.bincount(expert_id, length=N_EXPERTS)
    padded = ((counts + MOE_BLOCK - 1) // MOE_BLOCK) * MOE_BLOCK
    start = jnp.cumsum(counts) - counts
    pend = jnp.cumsum(padded)
    pstart = pend - padded
    dest = pstart[e_sorted] + jnp.arange(TK, dtype=jnp.int32) - start[e_sorted]
    slot_tok = jnp.full((P,), T, dtype=jnp.int32).at[dest].set(token_id[order])
    slot_w = jnp.zeros((P,), dtype=u.dtype).at[dest].set(weight[order])
    blk_e = jnp.clip(jnp.searchsorted(pend, jnp.arange(n_blocks, dtype=pend.dtype) * MOE_BLOCK, side='right'),
                     0, N_EXPERTS - 1)
    x_pad = jnp.concatenate([xt, jnp.zeros((1, D), xt.dtype)], axis=0)
    xs = x_pad[slot_tok].reshape(n_blocks, MOE_BLOCK, D)

    def expert_block(args):
        xb, e = args
        hdn = jax.nn.silu(xb @ w1[e]) * (xb @ w3[e])
        return hdn @ w2[e]

    ys = lax.map(expert_block, (xs, blk_e)).reshape(P, D)
    out = jnp.zeros((T + 1, D), ys.dtype).at[slot_tok].add(ys * slot_w[:, None])[:T]
    return out.reshape(B, S, D)


def setup_inputs(seed: int = 0) -> dict:
    key = jax.random.key(seed)
    ks = jax.random.split(key, 19)
    f32 = jnp.float32
    L, D = DEPTH, D_MODEL

    def nrm(k, shape, scale):
        return jax.random.normal(k, shape, f32) * scale

    return {
        'x': nrm(ks[0], (BATCH, SEQ, D), 1.0),
        'ln1_g': 1.0 + nrm(ks[1], (L, D), 0.02),
        'w_in': nrm(ks[2], (L, D, IN_WIDTH), D ** -0.5),
        'gla_w_a2': nrm(ks[3], (L, GLA_LOWRANK, GLA_HEADS * GLA_DK), GLA_LOWRANK ** -0.5),
        'gla_b_a': nrm(ks[4], (L, GLA_HEADS * GLA_DK), 0.1),
        'gla_norm_g': 1.0 + nrm(ks[5], (L, GLA_DV), 0.02),
        'swa_sinks': nrm(ks[6], (L, SWA_HEADS), 0.5),
        'ret_norm_g': 1.0 + nrm(ks[7], (L, RET_DV), 0.02),
        'w_branch': nrm(ks[8], (L, N_BRANCH, BRANCH_WIDTH, D), BRANCH_WIDTH ** -0.5),
        'w_out': nrm(ks[9], (L, D, D), D ** -0.5),
        'ln2_g': 1.0 + nrm(ks[10], (L, D), 0.02),
        'w_group': nrm(ks[11], (L, D, N_GROUPS), D ** -0.5),
        'b_group': nrm(ks[12], (L, N_GROUPS), 0.01),
        'w_expert': nrm(ks[13], (L, D, N_EXPERTS), D ** -0.5),
        'b_expert': nrm(ks[14], (L, N_EXPERTS), 0.01),
        'w1': nrm(ks[15], (L, N_EXPERTS, D, EXPERT_HIDDEN), D ** -0.5),
        'w3': nrm(ks[16], (L, N_EXPERTS, D, EXPERT_HIDDEN), D ** -0.5),
        'w2': nrm(ks[17], (L, N_EXPERTS, EXPERT_HIDDEN, D), EXPERT_HIDDEN ** -0.5),
        'final_g': 1.0 + nrm(ks[18], (D,), 0.02),
    }


def reference(x, ln1_g, w_in, gla_w_a2, gla_b_a, gla_norm_g, swa_sinks, ret_norm_g, w_branch, w_out,
              ln2_g, w_group, b_group, w_expert, b_expert, w1, w3, w2, final_g):
    h = x
    for l in range(DEPTH):
        h = h + token_mixer(rms_norm(h, ln1_g[l]), w_in[l], gla_w_a2[l], gla_b_a[l], gla_norm_g[l],
                            swa_sinks[l], ret_norm_g[l], w_branch[l], w_out[l])
        h = h + hier_moe(rms_norm(h, ln2_g[l]), w_group[l], b_group[l], w_expert[l], b_expert[l],
                         w1[l], w3[l], w2[l])
    return rms_norm(h, final_g)
```

```python
import functools
import math

import numpy as np
import jax
import jax.numpy as jnp
from jax import lax
from jax.experimental import pallas as pl
from jax.experimental.pallas import tpu as pltpu

F32 = jnp.float32
BF16 = jnp.bfloat16
I32 = jnp.int32
HIGHEST = lax.Precision.HIGHEST

D_MODEL = 1024
N_BRANCH = 4
NORM_EPS = 1e-6
NEG_INF = -1e30
BELOW_NEG_INF = -3e38

GLA_HEADS, GLA_DK, GLA_DV, GLA_LOWRANK, GLA_TAU, GLA_CHUNK = 4, 32, 64, 16, 16.0, 64
SWA_HEADS, SWA_KV_HEADS, SWA_HD, SWA_WINDOW = 4, 2, 64, 128
MOBA_HEADS, MOBA_HD, MOBA_BLOCK, MOBA_TOPK = 4, 64, 256, 3
RET_HEADS, RET_DK, RET_DV, RET_CHUNK, RET_ROPE_BASE = 4, 32, 64, 128, 10000.0
N_GROUPS, EXPERTS_PER_GROUP, EXPERT_TOPK, EXPERT_HIDDEN, MOE_BLOCK = 4, 8, 2, 256, 256
N_EXPERTS = N_GROUPS * EXPERTS_PER_GROUP

GLA_W = 896
SWA_W = 512
MOBA_W = 768
RET_W = 768
GATE_W = N_BRANCH * D_MODEL

LANES = 128
EXPERT_LANE0 = 32

VMEM_LIMIT = 56 * 1024 * 1024

TOK_TILE = 512
ROW_TILE = 256


def _params(*sem):
    return pltpu.CompilerParams(dimension_semantics=sem, vmem_limit_bytes=VMEM_LIMIT)


def _dot(a, b, precision=None):
    return jnp.dot(a, b, preferred_element_type=F32, precision=precision)


def _dot_nt(a, b, precision=None):
    return lax.dot_general(a, b, (((1,), (1,)), ((), ())), preferred_element_type=F32, precision=precision)


def _dot_tn(a, b, precision=None):
    return lax.dot_general(a, b, (((0,), (0,)), ((), ())), preferred_element_type=F32, precision=precision)


def _iota(shape, dim):
    return lax.broadcasted_iota(I32, shape, dim)


def _rms(x, g):
    ms = jnp.mean(x * x, axis=-1, keepdims=True)
    return x * lax.rsqrt(ms + NORM_EPS) * g


def _full(shape):
    return pl.BlockSpec(shape, lambda *_: (0,) * len(shape))


def _in_proj_kernel(h_ref, g_ref, w0, w1, w2, w3, w4, o0, o1, o2, o3, o4):
    u = _rms(h_ref[...], g_ref[...]).astype(BF16)
    for w, o in ((w0, o0), (w1, o1), (w2, o2), (w3, o3), (w4, o4)):
        o[...] = _dot(u, w[...]).astype(BF16)


def _in_proj(h, g, ws):
    T = h.shape[0]
    tm = TOK_TILE
    widths = [w.shape[1] for w in ws]
    return pl.pallas_call(
        _in_proj_kernel,
        grid=(T // tm,),
        in_specs=[pl.BlockSpec((tm, D_MODEL), lambda i: (i, 0)), _full((1, D_MODEL))]
        + [pl.BlockSpec((D_MODEL, n), lambda i: (0, 0), pipeline_mode=pl.Buffered(1)) for n in widths],
        out_specs=[pl.BlockSpec((tm, n), lambda i: (i, 0)) for n in widths],
        out_shape=[jax.ShapeDtypeStruct((T, n), BF16) for n in widths],
        compiler_params=_params("parallel"),
        name="in_proj",
    )(h, g, *ws)


def _head_select(x, lane_head, n_heads, rows):
    out = None
    for hh in range(n_heads):
        term = jnp.where(lane_head == hh, x[hh * rows:(hh + 1) * rows], 0.0)
        out = term if out is None else out + term
    return out


def _head_norm_gate(o, gate_in, g):
    n = o.shape[1]
    same_head = (_iota((n, n), 0) >> 6) == (_iota((n, n), 1) >> 6)
    ms = _dot(o * o, same_head.astype(F32), precision=HIGHEST) * (1.0 / 64.0)
    return o * lax.rsqrt(ms + NORM_EPS) * g * (gate_in * jax.nn.sigmoid(gate_in))


def _gla_kernel(z_ref, wa_ref, ba_ref, g_ref, o_ref, st_ref, oacc_ref, *, tiles_per_seq):
    C = GLA_CHUNK
    tt = z_ref.shape[0]

    @pl.when(pl.program_id(0) % tiles_per_seq == 0)
    def _():
        st_ref[...] = jnp.zeros_like(st_ref)

    q = z_ref[:, 0:128].astype(F32)
    k = z_ref[:, 128:256].astype(F32)
    r = z_ref[:, 512:768].astype(F32)
    a = z_ref[:, 768:896].astype(F32)
    pre = _dot(a, wa_ref[...], precision=HIGHEST) + ba_ref[...]
    log_a = (jnp.minimum(pre, 0.0) - jnp.log(1.0 + jnp.exp(-jnp.abs(pre)))) * (1.0 / GLA_TAU)

    row = _iota((tt, tt), 0)
    col = _iota((tt, tt), 1)
    same_chunk = (row >> 6) == (col >> 6)
    b = _dot((same_chunk & (col <= row)).astype(F32), log_a, precision=HIGHEST)
    b_last = _dot(same_chunk.astype(F32), log_a, precision=HIGHEST)
    q_dec = (q * (GLA_DK ** -0.5) * jnp.exp(b)).astype(BF16)
    k_dec = (k * jnp.exp(-b)).astype(BF16)
    k_end = (k * jnp.exp(b_last - b)).astype(BF16)
    decay = jnp.exp(b_last)

    qk_head = _iota((C, 128), 1) >> 5
    v_head = _iota((C, 256), 1) >> 6
    causal = _iota((4 * C, C), 1) <= (_iota((4 * C, C), 0) & (C - 1))
    state_mask = (_iota((256, 128), 0) >> 6) == (_iota((256, 128), 1) >> 5)

    for c in range(tt // C):
        sl = slice(c * C, (c + 1) * C)
        qd = q_dec[sl]
        v = z_ref[sl, 256:512]
        q_stack = jnp.concatenate([jnp.where(qk_head == hh, qd, 0) for hh in range(GLA_HEADS)], axis=0)
        att = jnp.where(causal, _dot_nt(q_stack, k_dec[sl]), 0.0)
        intra = _head_select(_dot(att.astype(BF16), v), v_head, GLA_HEADS, C)
        state = st_ref[...]
        inter = _dot_nt(qd, state.astype(BF16))
        oacc_ref[sl, :] = intra + inter
        kv = _dot_tn(v, k_end[sl])
        st_ref[...] = state * decay[c * C:c * C + 1] + jnp.where(state_mask, kv, 0.0)

    o_ref[...] = _head_norm_gate(oacc_ref[...], r, g_ref[...]).astype(BF16)


def _gla(z, wa, ba, g, seq):
    T = z.shape[0]
    tt = TOK_TILE
    return pl.pallas_call(
        functools.partial(_gla_kernel, tiles_per_seq=seq // tt),
        grid=(T // tt,),
        in_specs=[pl.BlockSpec((tt, GLA_W), lambda i: (i, 0)), _full((128, 128)), _full((1, 128)),
                  _full((1, 256))],
        out_specs=pl.BlockSpec((tt, 256), lambda i: (i, 0)),
        out_shape=jax.ShapeDtypeStruct((T, 256), BF16),
        scratch_shapes=[pltpu.VMEM((256, 128), F32), pltpu.VMEM((tt, 256), F32)],
        compiler_params=_params("arbitrary"),
        name="gla_mixer",
    )(z, wa, ba, g)


_RET_LOG_GAMMA = [math.log1p(-(2.0 ** (-5.0 - hh))) for hh in range(RET_HEADS)]


def _by_head(head, values):
    out = jnp.full(head.shape, values[-1], F32)
    for hh in range(len(values) - 2, -1, -1):
        out = jnp.where(head == hh, values[hh], out)
    return out


def _ret_kernel(z_ref, cos_ref, sina_ref, sinb_ref, g_ref, o_ref, st_ref, oacc_ref, *, tiles_per_seq):
    C = RET_CHUNK
    tt = z_ref.shape[0]

    @pl.when(pl.program_id(0) % tiles_per_seq == 0)
    def _():
        st_ref[...] = jnp.zeros_like(st_ref)

    cos, sina, sinb = cos_ref[...], sina_ref[...], sinb_ref[...]

    def rotate(t):
        return t * cos + pltpu.roll(t, 112, 1) * sina + pltpu.roll(t, 16, 1) * sinb

    q_rot = rotate(z_ref[:, 0:128].astype(F32))
    k_rot = rotate(z_ref[:, 128:256].astype(F32)) * (RET_DK ** -0.5)
    gate_in = z_ref[:, 512:768].astype(F32)

    qk_head = _iota((C, 128), 1) >> 5
    v_head = _iota((C, 256), 1) >> 6
    lg_qk = _by_head(_iota((1, 128), 1) >> 5, _RET_LOG_GAMMA)
    lg_v = _by_head(_iota((1, 256), 1) >> 6, _RET_LOG_GAMMA)
    pos_qk = _iota((C, 128), 0).astype(F32)
    pos_v = _iota((C, 256), 0).astype(F32)
    key_decay = jnp.exp((C - 1.0 - pos_qk) * lg_qk)
    query_decay = jnp.exp((pos_v + 1.0) * lg_v)
    chunk_decay = jnp.exp(float(C) * lg_qk)
    srow = _iota((4 * C, C), 0)
    rel = ((srow & (C - 1)) - _iota((4 * C, C), 1)).astype(F32)
    decay_mat = jnp.where(rel >= 0, jnp.exp(jnp.maximum(rel, 0.0) * _by_head(srow >> 7, _RET_LOG_GAMMA)), 0.0)
    state_mask = (_iota((256, 128), 0) >> 6) == (_iota((256, 128), 1) >> 5)

    q_bf = q_rot.astype(BF16)
    k_bf = k_rot.astype(BF16)
    for c in range(tt // C):
        sl = slice(c * C, (c + 1) * C)
        qc = q_bf[sl]
        v = z_ref[sl, 256:512]
        q_stack = jnp.concatenate([jnp.where(qk_head == hh, qc, 0) for hh in range(RET_HEADS)], axis=0)
        att = _dot_nt(q_stack, k_bf[sl]) * decay_mat
        intra = _head_select(_dot(att.astype(BF16), v), v_head, RET_HEADS, C)
        state = st_ref[...]
        inter = _dot_nt(qc, state.astype(BF16)) * query_decay
        oacc_ref[sl, :] = intra + inter
        kv = _dot_tn(v, (k_rot[sl] * key_decay).astype(BF16))
        st_ref[...] = state * chunk_decay + jnp.where(state_mask, kv, 0.0)

    o_ref[...] = _head_norm_gate(oacc_ref[...], gate_in, g_ref[...]).astype(BF16)


def _retention(z, cos, sina, sinb, g, seq):
    T = z.shape[0]
    tt = TOK_TILE
    tps = seq // tt
    pos_spec = pl.BlockSpec((tt, 128), lambda i: (i % tps, 0))
    return pl.pallas_call(
        functools.partial(_ret_kernel, tiles_per_seq=tps),
        grid=(T // tt,),
        in_specs=[pl.BlockSpec((tt, RET_W), lambda i: (i, 0)), pos_spec, pos_spec, pos_spec, _full((1, 256))],
        out_specs=pl.BlockSpec((tt, 256), lambda i: (i, 0)),
        out_shape=jax.ShapeDtypeStruct((T, 256), BF16),
        scratch_shapes=[pltpu.VMEM((256, 128), F32), pltpu.VMEM((tt, 256), F32)],
        compiler_params=_params("arbitrary"),
        name="retention_mixer",
    )(z, cos, sina, sinb, g)


def _rope_tables(seq):
    half = RET_DK // 2
    pos = jnp.arange(seq, dtype=F32)
    inv_freq = RET_ROPE_BASE ** (-jnp.arange(half, dtype=F32) * 2.0 / RET_DK)
    ang = pos[:, None] * inv_freq[None, :]
    cos = jnp.tile(jnp.cos(ang), (1, LANES // half))
    sin = jnp.tile(jnp.sin(ang), (1, LANES // half))
    first_half = (jnp.arange(LANES) % RET_DK) < half
    return cos, jnp.where(first_half, -sin, 0.0), jnp.where(first_half, 0.0, sin)


def _swa_kernel(sink_ref, cur_ref, prev_ref, o_ref, *, tiles_per_seq):
    W = SWA_WINDOW
    HD = SWA_HD
    tt = cur_ref.shape[0]
    first_key = jnp.where(pl.program_id(0) % tiles_per_seq == 0, W, 0)
    qi = _iota((W, 2 * W), 0)
    sj = _iota((W, 2 * W), 1)
    in_window = (sj > qi) & (sj <= qi + W)
    kv0 = SWA_HEADS * HD
    for c in range(tt // W):
        rows = slice(c * W, (c + 1) * W)
        if c == 0:
            k_prev, v_prev = prev_ref[:, kv0:kv0 + 128], prev_ref[:, kv0 + 128:kv0 + 256]
            mask = in_window & (sj >= first_key)
        else:
            prows = slice((c - 1) * W, c * W)
            k_prev, v_prev = cur_ref[prows, kv0:kv0 + 128], cur_ref[prows, kv0 + 128:kv0 + 256]
            mask = in_window
        k_band = jnp.concatenate([k_prev, cur_ref[rows, kv0:kv0 + 128]], axis=0)
        v_band = jnp.concatenate([v_prev, cur_ref[rows, kv0 + 128:kv0 + 256]], axis=0)
        outs = []
        for hh in range(SWA_HEADS):
            kk = hh // (SWA_HEADS // SWA_KV_HEADS)
            s = _dot_nt(cur_ref[rows, hh * HD:(hh + 1) * HD], k_band[:, kk * HD:(kk + 1) * HD]) * (HD ** -0.5)
            s = jnp.where(mask, s, NEG_INF)
            sink = sink_ref[hh]
            m = jnp.maximum(jnp.max(s, axis=-1, keepdims=True), sink)
            p = jnp.exp(s - m)
            denom = jnp.sum(p, axis=-1, keepdims=True) + jnp.exp(sink - m)
            outs.append(_dot(p.astype(BF16), v_band[:, kk * HD:(kk + 1) * HD]) / denom)
        o_ref[rows, :] = jnp.concatenate(outs, axis=1).astype(BF16)


def _swa(z, sinks, seq):
    T = z.shape[0]
    tt = TOK_TILE
    per = tt // SWA_WINDOW
    return pl.pallas_call(
        functools.partial(_swa_kernel, tiles_per_seq=seq // tt),
        grid=(T // tt,),
        in_specs=[pl.BlockSpec(memory_space=pltpu.SMEM),
                  pl.BlockSpec((tt, SWA_W), lambda i: (i, 0)),
                  pl.BlockSpec((SWA_WINDOW, SWA_W), lambda i: (jnp.maximum(i * per - 1, 0), 0))],
        out_specs=pl.BlockSpec((tt, 256), lambda i: (i, 0)),
        out_shape=jax.ShapeDtypeStruct((T, 256), BF16),
        compiler_params=_params("parallel"),
        name="swa_mixer",
    )(sinks, z, z)


_KMEAN_BLOCKS = 8


def _kmean_kernel(k_ref, o_ref):
    k = k_ref[...].astype(F32).reshape(_KMEAN_BLOCKS, MOBA_BLOCK, 256)
    o_ref[...] = jnp.mean(k, axis=1)


def _moba_kmean(z):
    T = z.shape[0]
    rows = _KMEAN_BLOCKS * MOBA_BLOCK
    return pl.pallas_call(
        _kmean_kernel,
        grid=(T // rows,),
        in_specs=[pl.BlockSpec((rows, 256), lambda i: (i, 1))],
        out_specs=pl.BlockSpec((_KMEAN_BLOCKS, 256), lambda i: (i, 0)),
        out_shape=jax.ShapeDtypeStruct((T // MOBA_BLOCK, 256), F32),
        compiler_params=_params("parallel"),
        name="moba_kmean",
    )(z)


def _moba_kernel(qidx_ref, kidx_ref, q_ref, k_ref, v_ref, km_ref, o_ref, sel_ref, m_ref, l_ref, acc_ref):
    HD = MOBA_HD
    tq = q_ref.shape[0]
    nb = km_ref.shape[0]
    step = pl.program_id(1)
    qb = qidx_ref[step]
    kb = kidx_ref[step]
    own = kb == qb
    last = (kb == qb - 1) | (qb == 0)

    @pl.when(own)
    def _():
        blk = _iota((tq, nb), 1).astype(F32)
        for hh in range(MOBA_HEADS):
            gate = _dot_nt(q_ref[:, hh * HD:(hh + 1) * HD].astype(F32), km_ref[:, hh * HD:(hh + 1) * HD],
                           precision=HIGHEST)
            gate = jnp.where(blk < qb.astype(F32), gate, NEG_INF)
            sel = jnp.zeros((tq, nb), F32)
            for _ in range(MOBA_TOPK):
                best = jnp.max(gate, axis=-1, keepdims=True)
                first = jnp.min(jnp.where(gate == best, blk, float(nb)), axis=-1, keepdims=True)
                hit = blk == first
                sel = sel + jnp.where(hit, jnp.where(best > 0.5 * NEG_INF, 1.0, 0.0), 0.0)
                gate = jnp.where(hit, BELOW_NEG_INF, gate)
            sel_ref[hh] = sel
        m_ref[...] = jnp.full(m_ref.shape, NEG_INF, F32)
        l_ref[...] = jnp.zeros_like(l_ref)
        acc_ref[...] = jnp.zeros_like(acc_ref)

    def attend(mask_scores):
        for hh in range(MOBA_HEADS):
            cols = slice(hh * HD, (hh + 1) * HD)
            s = mask_scores(hh, _dot_nt(q_ref[:, cols] * (HD ** -0.5), k_ref[:, cols]))
            m_old = m_ref[hh]
            m_new = jnp.maximum(m_old, jnp.max(s, axis=-1, keepdims=True))
            alpha = jnp.exp(m_old - m_new)
            p = jnp.exp(s - m_new)
            l_ref[hh] = alpha * l_ref[hh] + jnp.sum(p, axis=-1, keepdims=True)
            acc_ref[:, cols] = alpha * acc_ref[:, cols] + _dot(p.astype(BF16), v_ref[:, cols])
            m_ref[hh] = m_new

    @pl.when(own)
    def _():
        causal = _iota((tq, tq), 1) <= _iota((tq, tq), 0)
        attend(lambda hh, s: jnp.where(causal, s, NEG_INF))

    @pl.when(jnp.logical_not(own))
    def _():
        this_block = _iota((tq, nb), 1) == kb

        def mask_unselected(hh, s):
            chosen = jnp.sum(jnp.where(this_block, sel_ref[hh], 0.0), axis=-1, keepdims=True)
            return s + (chosen - 1.0) * (-NEG_INF)

        attend(mask_unselected)

    @pl.when(last)
    def _():
        outs = [acc_ref[:, hh * HD:(hh + 1) * HD] / l_ref[hh] for hh in range(MOBA_HEADS)]
        o_ref[...] = jnp.concatenate(outs, axis=1).astype(BF16)


def _moba(z, seq):
    T = z.shape[0]
    nb = seq // MOBA_BLOCK
    batch = T // seq
    kmean = _moba_kmean(z)
    qidx = np.concatenate([np.full(b + 1, b) for b in range(nb)]).astype(np.int32)
    kidx = np.concatenate([np.concatenate([[b], np.arange(b)]) for b in range(nb)]).astype(np.int32)
    tq = MOBA_BLOCK
    grid_spec = pltpu.PrefetchScalarGridSpec(
        num_scalar_prefetch=2,
        grid=(batch, len(qidx)),
        in_specs=[pl.BlockSpec((tq, 256), lambda b, s, qi, ki: (b * nb + qi[s], 0)),
                  pl.BlockSpec((tq, 256), lambda b, s, qi, ki: (b * nb + ki[s], 1)),
                  pl.BlockSpec((tq, 256), lambda b, s, qi, ki: (b * nb + ki[s], 2)),
                  pl.BlockSpec((nb, 256), lambda b, s, qi, ki: (b, 0))],
        out_specs=pl.BlockSpec((tq, 256), lambda b, s, qi, ki: (b * nb + qi[s], 0)),
        scratch_shapes=[pltpu.VMEM((MOBA_HEADS, tq, nb), F32), pltpu.VMEM((MOBA_HEADS, tq, 1), F32),
                        pltpu.VMEM((MOBA_HEADS, tq, 1), F32), pltpu.VMEM((tq, 256), F32)],
    )
    return pl.pallas_call(
        _moba_kernel,
        grid_spec=grid_spec,
        out_shape=jax.ShapeDtypeStruct((T, 256), BF16),
        compiler_params=_params("parallel", "arbitrary"),
        name="moba_mixer",
    )(jnp.asarray(qidx), jnp.asarray(kidx), z, z, z, kmean)


def _merge_kernel(h_ref, o0, o1, o2, o3, gates_ref, wb_ref, wout_ref, out_ref):
    merged = None
    for i, o in enumerate((o0, o1, o2, o3)):
        gate = jax.nn.sigmoid(gates_ref[:, i * D_MODEL:(i + 1) * D_MODEL].astype(F32))
        term = gate * _dot(o[...], wb_ref[i])
        merged = term if merged is None else merged + term
    out_ref[...] = h_ref[...] + _dot(merged.astype(BF16), wout_ref[...])


def _merge(h, branches, gates, wb, wout):
    T = h.shape[0]
    tm = TOK_TILE
    return pl.pallas_call(
        _merge_kernel,
        grid=(T // tm,),
        in_specs=[pl.BlockSpec((tm, D_MODEL), lambda i: (i, 0))]
        + [pl.BlockSpec((tm, 256), lambda i: (i, 0))] * 4
        + [pl.BlockSpec((tm, GATE_W), lambda i: (i, 0)),
           pl.BlockSpec((N_BRANCH, 256, D_MODEL), lambda i: (0, 0, 0), pipeline_mode=pl.Buffered(1)),
           pl.BlockSpec((D_MODEL, D_MODEL), lambda i: (0, 0), pipeline_mode=pl.Buffered(1))],
        out_specs=pl.BlockSpec((tm, D_MODEL), lambda i: (i, 0)),
        out_shape=jax.ShapeDtypeStruct((T, D_MODEL), F32),
        compiler_params=_params("parallel"),
        name="merge_out_proj",
    )(h, *branches, gates, wb, wout)


def _lane_first(mask, lane):
    return jnp.min(jnp.where(mask, lane, float(LANES)), axis=-1, keepdims=True)


def _router_kernel(h_ref, g_ref, wr_ref, br_ref, u_ref, route_ref, count_ref, carry_ref):
    tm = h_ref.shape[0]

    @pl.when(pl.program_id(0) == 0)
    def _():
        carry_ref[...] = jnp.zeros_like(carry_ref)

    u = _rms(h_ref[...], g_ref[...])
    u_ref[...] = u
    logits = _dot(u, wr_ref[...], precision=HIGHEST) + br_ref[...]
    lane_i = _iota((tm, LANES), 1)
    lane = lane_i.astype(F32)
    lane_group = ((lane_i - EXPERT_LANE0) >> 3).astype(F32)

    grp = jnp.where(lane_i < N_GROUPS, logits, BELOW_NEG_INF)
    grp_e = jnp.exp(grp - jnp.max(grp, axis=-1, keepdims=True))
    p_group = grp_e / jnp.sum(grp_e, axis=-1, keepdims=True)
    g_w = jnp.max(p_group, axis=-1, keepdims=True)
    g_sel = _lane_first(p_group == g_w, lane)

    in_group = (lane_i >= EXPERT_LANE0) & (lane_group == g_sel)
    el = jnp.where(in_group, logits, BELOW_NEG_INF)
    top1 = jnp.max(el, axis=-1, keepdims=True)
    lane1 = _lane_first(el == top1, lane)
    el = jnp.where(lane == lane1, BELOW_NEG_INF, el)
    top2 = jnp.max(el, axis=-1, keepdims=True)
    lane2 = _lane_first(el == top2, lane)
    e2 = jnp.exp(top2 - top1)
    w1 = g_w / (1.0 + e2)
    w2 = g_w * e2 / (1.0 + e2)

    onehot = ((lane == lane1) | (lane == lane2)).astype(BF16)
    before = (_iota((tm, tm), 1) < _iota((tm, tm), 0)).astype(BF16)
    seen = _dot(before, onehot) + carry_ref[...]
    rank1 = jnp.sum(jnp.where(lane == lane1, seen, 0.0), axis=-1, keepdims=True)
    rank2 = jnp.sum(jnp.where(lane == lane2, seen, 0.0), axis=-1, keepdims=True)
    carry_ref[...] += jnp.sum(onehot.astype(F32), axis=0, keepdims=True)
    count_ref[...] = carry_ref[...]

    out = jnp.zeros((tm, LANES), F32)
    for idx, val in enumerate((lane1 - EXPERT_LANE0, lane2 - EXPERT_LANE0, w1, w2, rank1, rank2)):
        out = jnp.where(lane_i == idx, val, out)
    route_ref[...] = out


def _router(h, g, wr, br):
    T = h.shape[0]
    tm = TOK_TILE
    return pl.pallas_call(
        _router_kernel,
        grid=(T // tm,),
        in_specs=[pl.BlockSpec((tm, D_MODEL), lambda i: (i, 0)), _full((1, D_MODEL)),
                  _full((D_MODEL, LANES)), _full((1, LANES))],
        out_specs=[pl.BlockSpec((tm, D_MODEL), lambda i: (i, 0)), pl.BlockSpec((tm, LANES), lambda i: (i, 0)),
                   _full((1, LANES))],
        out_shape=[jax.ShapeDtypeStruct((T, D_MODEL), F32), jax.ShapeDtypeStruct((T, LANES), F32),
                   jax.ShapeDtypeStruct((1, LANES), F32)],
        scratch_shapes=[pltpu.VMEM((1, LANES), F32)],
        compiler_params=_params("arbitrary"),
        name="moe_router",
    )(h, g, wr, br)


def _row_copy(src_ref, src_row, dst_ref, dst_row, sem):
    return pltpu.make_async_copy(src_ref.at[pl.ds(src_row, 1)], dst_ref.at[pl.ds(dst_row, 1)], sem)


def _dispatch_kernel(dest_ref, u_ref, xs_in_ref, xs_ref, sem):
    del xs_in_ref
    tm = u_ref.shape[0]
    base = pl.program_id(0) * (2 * tm)

    def issue(r, carry):
        _row_copy(u_ref, r, xs_ref, dest_ref[base + 2 * r], sem).start()
        _row_copy(u_ref, r, xs_ref, dest_ref[base + 2 * r + 1], sem).start()
        return carry

    def drain(r, carry):
        _row_copy(u_ref, 0, xs_ref, 0, sem).wait()
        return carry

    lax.fori_loop(0, tm, issue, 0)
    lax.fori_loop(0, 2 * tm, drain, 0)


def _dispatch(dest, u, n_slots):
    T = u.shape[0]
    tm = ROW_TILE
    grid_spec = pltpu.PrefetchScalarGridSpec(
        num_scalar_prefetch=1,
        grid=(T // tm,),
        in_specs=[pl.BlockSpec((tm, D_MODEL), lambda i, d: (i, 0)), pl.BlockSpec(memory_space=pl.ANY)],
        out_specs=pl.BlockSpec(memory_space=pl.ANY),
        scratch_shapes=[pltpu.SemaphoreType.DMA(())],
    )
    return pl.pallas_call(
        _dispatch_kernel,
        grid_spec=grid_spec,
        out_shape=jax.ShapeDtypeStruct((n_slots, D_MODEL), F32),
        input_output_aliases={2: 0},
        compiler_params=_params("arbitrary"),
        name="moe_dispatch",
    )(dest, u, jnp.zeros((n_slots, D_MODEL), F32))


def _expert_kernel(blk_e_ref, n_used_ref, x_ref, w1_ref, w3_ref, w2_ref, y_ref):
    del blk_e_ref
    used = pl.program_id(0) < n_used_ref[0]

    @pl.when(used)
    def _():
        x = x_ref[...].astype(BF16)
        a = _dot(x, w1_ref[0])
        hidden = (a * jax.nn.sigmoid(a)) * _dot(x, w3_ref[0])
        y_ref[...] = _dot(hidden.astype(BF16), w2_ref[0])

    @pl.when(jnp.logical_not(used))
    def _():
        y_ref[...] = jnp.zeros_like(y_ref)


def _experts(blk_e, n_used, xs, w1, w3, w2):
    n_slots = xs.shape[0]
    bm = MOE_BLOCK
    grid_spec = pltpu.PrefetchScalarGridSpec(
        num_scalar_prefetch=2,
        grid=(n_slots // bm,),
        in_specs=[pl.BlockSpec((bm, D_MODEL), lambda i, be, nu: (i, 0)),
                  pl.BlockSpec((1, D_MODEL, EXPERT_HIDDEN), lambda i, be, nu: (be[i], 0, 0)),
                  pl.BlockSpec((1, D_MODEL, EXPERT_HIDDEN), lambda i, be, nu: (be[i], 0, 0)),
                  pl.BlockSpec((1, EXPERT_HIDDEN, D_MODEL), lambda i, be, nu: (be[i], 0, 0))],
        out_specs=pl.BlockSpec((bm, D_MODEL), lambda i, be, nu: (i, 0)),
    )
    return pl.pallas_call(
        _expert_kernel,
        grid_spec=grid_spec,
        out_shape=jax.ShapeDtypeStruct((n_slots, D_MODEL), F32),
        compiler_params=_params("parallel"),
        name="moe_experts",
    )(blk_e, n_used, xs, w1, w3, w2)


def _combine_kernel(dest_ref, h_ref, route_ref, ys_ref, out_ref, y0_ref, y1_ref, sem):
    tm = h_ref.shape[0]
    base = pl.program_id(0) * (2 * tm)

    def issue(r, carry):
        _row_copy(ys_ref, dest_ref[base + 2 * r], y0_ref, r, sem).start()
        _row_copy(ys_ref, dest_ref[base + 2 * r + 1], y1_ref, r, sem).start()
        return carry

    def drain(r, carry):
        _row_copy(ys_ref, 0, y0_ref, 0, sem).wait()
        return carry

    lax.fori_loop(0, tm, issue, 0)
    lax.fori_loop(0, 2 * tm, drain, 0)
    out_ref[...] = h_ref[...] + route_ref[:, 2:3] * y0_ref[...] + route_ref[:, 3:4] * y1_ref[...]


def _combine(dest, h, route, ys):
    T = h.shape[0]
    tm = ROW_TILE
    grid_spec = pltpu.PrefetchScalarGridSpec(
        num_scalar_prefetch=1,
        grid=(T // tm,),
        in_specs=[pl.BlockSpec((tm, D_MODEL), lambda i, d: (i, 0)),
                  pl.BlockSpec((tm, LANES), lambda i, d: (i, 0)),
                  pl.BlockSpec(memory_space=pl.ANY)],
        out_specs=pl.BlockSpec((tm, D_MODEL), lambda i, d: (i, 0)),
        scratch_shapes=[pltpu.VMEM((tm, D_MODEL), F32), pltpu.VMEM((tm, D_MODEL), F32),
                        pltpu.SemaphoreType.DMA(())],
    )
    return pl.pallas_call(
        _combine_kernel,
        grid_spec=grid_spec,
        out_shape=jax.ShapeDtypeStruct((T, D_MODEL), F32),
        compiler_params=_params("arbitrary"),
        name="moe_combine",
    )(dest, h, route, ys)


def _moe(h, g, w_group, b_group, w_expert, b_expert, w1, w3, w2):
    T = h.shape[0]
    wr = jnp.zeros((D_MODEL, LANES), F32)
    wr = wr.at[:, :N_GROUPS].set(w_group).at[:, EXPERT_LANE0:EXPERT_LANE0 + N_EXPERTS].set(w_expert)
    br = jnp.zeros((1, LANES), F32)
    br = br.at[0, :N_GROUPS].set(b_group).at[0, EXPERT_LANE0:EXPERT_LANE0 + N_EXPERTS].set(b_expert)
    u, route, count = _router(h, g, wr, br)

    n_blocks = (T * EXPERT_TOPK) // MOE_BLOCK + N_EXPERTS
    counts = count[0, EXPERT_LANE0:EXPERT_LANE0 + N_EXPERTS].astype(I32)
    padded = ((counts + MOE_BLOCK - 1) // MOE_BLOCK) * MOE_BLOCK
    pend = jnp.cumsum(padded)
    pstart = pend - padded
    expert = route[:, 0:2].astype(I32)
    dest = (pstart[expert] + route[:, 4:6].astype(I32)).reshape(T * EXPERT_TOPK)
    blk_e = jnp.clip(jnp.searchsorted(pend, jnp.arange(n_blocks, dtype=I32) * MOE_BLOCK, side="right"),
                     0, N_EXPERTS - 1).astype(I32)
    n_used = (pend[-1:] // MOE_BLOCK).astype(I32)

    xs = _dispatch(dest, u, n_blocks * MOE_BLOCK)
    ys = _experts(blk_e, n_used, xs, w1, w3, w2)
    return _combine(dest, h, route, ys)


def _final_norm_kernel(h_ref, g_ref, o_ref):
    o_ref[...] = _rms(h_ref[...], g_ref[...])


def _final_norm(h, g):
    T = h.shape[0]
    tm = TOK_TILE
    return pl.pallas_call(
        _final_norm_kernel,
        grid=(T // tm,),
        in_specs=[pl.BlockSpec((tm, D_MODEL), lambda i: (i, 0)), _full((1, D_MODEL))],
        out_specs=pl.BlockSpec((tm, D_MODEL), lambda i: (i, 0)),
        out_shape=jax.ShapeDtypeStruct((T, D_MODEL), F32),
        compiler_params=_params("parallel"),
        name="final_norm",
    )(h, g)


def _split_w_in(w_in):
    gla_n = GLA_HEADS * (2 * GLA_DK + 2 * GLA_DV) + GLA_LOWRANK
    swa_n = (SWA_HEADS + 2 * SWA_KV_HEADS) * SWA_HD
    moba_n = 3 * MOBA_HEADS * MOBA_HD
    ret_n = RET_HEADS * (2 * RET_DK + 2 * RET_DV)
    offs = np.cumsum([0, gla_n, swa_n, moba_n, ret_n, GATE_W])
    parts = [w_in[:, offs[i]:offs[i + 1]].astype(BF16) for i in range(5)]
    parts[0] = jnp.pad(parts[0], ((0, 0), (0, GLA_W - gla_n)))
    return parts


def _token_mixer(h, seq, ln1_g, w_in, gla_w_a2, gla_b_a, gla_norm_g, swa_sinks, ret_norm_g, w_branch, w_out,
                 rope):
    z_gla, z_swa, z_moba, z_ret, z_gates = _in_proj(h, ln1_g.reshape(1, D_MODEL), _split_w_in(w_in))
    wa = jnp.zeros((128, 128), F32).at[:GLA_LOWRANK].set(gla_w_a2)
    o_gla = _gla(z_gla, wa, gla_b_a.reshape(1, 128), jnp.tile(gla_norm_g, GLA_HEADS).reshape(1, 256), seq)
    o_swa = _swa(z_swa, swa_sinks, seq)
    o_moba = _moba(z_moba, seq)
    o_ret = _retention(z_ret, *rope, jnp.tile(ret_norm_g, RET_HEADS).reshape(1, 256), seq)
    return _merge(h, (o_gla, o_swa, o_moba, o_ret), z_gates, w_branch.astype(BF16), w_out.astype(BF16))


def kernel(x, ln1_g, w_in, gla_w_a2, gla_b_a, gla_norm_g, swa_sinks, ret_norm_g, w_branch, w_out, ln2_g,
           w_group, b_group, w_expert, b_expert, w1, w3, w2, final_g):
    batch, seq, _ = x.shape
    depth = w_in.shape[0]
    rope = _rope_tables(seq)
    h = x.reshape(batch * seq, D_MODEL)
    for l in range(depth):
        h = _token_mixer(h, seq, ln1_g[l], w_in[l], gla_w_a2[l], gla_b_a[l], gla_norm_g[l], swa_sinks[l],
                         ret_norm_g[l], w_branch[l], w_out[l], rope)
        h = _moe(h, ln2_g[l].reshape(1, D_MODEL), w_group[l], b_group[l], w_expert[l], b_expert[l],
                 w1[l].astype(BF16), w3[l].astype(BF16), w2[l].astype(BF16))
    return _final_norm(h, final_g.reshape(1, D_MODEL)).reshape(batch, seq, D_MODEL)
```

```python
import functools
import math

import numpy as np
import jax
import jax.numpy as jnp
from jax import lax
from jax.experimental import pallas as pl
from jax.experimental.pallas import tpu as pltpu

F32 = jnp.float32
BF16 = jnp.bfloat16
I32 = jnp.int32
HIGHEST = lax.Precision.HIGHEST

D_MODEL = 1024
N_BRANCH = 4
NORM_EPS = 1e-6
NEG_INF = -1e30
BELOW_NEG_INF = -3e38

GLA_HEADS, GLA_DK, GLA_DV, GLA_LOWRANK, GLA_TAU, GLA_CHUNK = 4, 32, 64, 16, 16.0, 64
SWA_HEADS, SWA_KV_HEADS, SWA_HD, SWA_WINDOW = 4, 2, 64, 128
MOBA_HEADS, MOBA_HD, MOBA_BLOCK, MOBA_TOPK = 4, 64, 256, 3
RET_HEADS, RET_DK, RET_DV, RET_CHUNK, RET_ROPE_BASE = 4, 32, 64, 128, 10000.0
N_GROUPS, EXPERTS_PER_GROUP, EXPERT_TOPK, EXPERT_HIDDEN, MOE_BLOCK = 4, 8, 2, 256, 256
N_EXPERTS = N_GROUPS * EXPERTS_PER_GROUP

GLA_W = 896
SWA_W = 512
MOBA_W = 768
RET_W = 768
GATE_W = N_BRANCH * D_MODEL

LANES = 128
EXPERT_LANE0 = 32

VMEM_LIMIT = 56 * 1024 * 1024

TOK_TILE = 512
ROW_TILE = 256


def _params(*sem):
    return pltpu.CompilerParams(dimension_semantics=sem, vmem_limit_bytes=VMEM_LIMIT)


def _dot(a, b, precision=None):
    return jnp.dot(a, b, preferred_element_type=F32, precision=precision)


def _dot_nt(a, b, precision=None):
    return lax.dot_general(a, b, (((1,), (1,)), ((), ())), preferred_element_type=F32, precision=precision)


def _dot_tn(a, b, precision=None):
    return lax.dot_general(a, b, (((0,), (0,)), ((), ())), preferred_element_type=F32, precision=precision)


def _iota(shape, dim):
    return lax.broadcasted_iota(I32, shape, dim)


def _rms(x, g):
    ms = jnp.mean(x * x, axis=-1, keepdims=True)
    return x * lax.rsqrt(ms + NORM_EPS) * g


def _full(shape):
    return pl.BlockSpec(shape, lambda *_: (0,) * len(shape))


def _in_proj_kernel(h_ref, g_ref, w0, w1, w2, w3, w4, o0, o1, o2, o3, o4):
    u = _rms(h_ref[...], g_ref[...]).astype(BF16)
    for w, o in ((w0, o0), (w1, o1), (w2, o2), (w3, o3), (w4, o4)):
        o[...] = _dot(u, w[...]).astype(BF16)


def _in_proj(h, g, ws):
    T = h.shape[0]
    tm = TOK_TILE
    widths = [w.shape[1] for w in ws]
    return pl.pallas_call(
        _in_proj_kernel,
        grid=(T // tm,),
        in_specs=[pl.BlockSpec((tm, D_MODEL), lambda i: (i, 0)), _full((1, D_MODEL))]
        + [pl.BlockSpec((D_MODEL, n), lambda i: (0, 0), pipeline_mode=pl.Buffered(1)) for n in widths],
        out_specs=[pl.BlockSpec((tm, n), lambda i: (i, 0)) for n in widths],
        out_shape=[jax.ShapeDtypeStruct((T, n), BF16) for n in widths],
        compiler_params=_params("parallel"),
        name="in_proj",
    )(h, g, *ws)


def _head_select(x, lane_head, n_heads, rows):
    out = None
    for hh in range(n_heads):
        term = jnp.where(lane_head == hh, x[hh * rows:(hh + 1) * rows], 0.0)
        out = term if out is None else out + term
    return out


def _head_norm_gate(o, gate_in, g):
    n = o.shape[1]
    same_head = (_iota((n, n), 0) >> 6) == (_iota((n, n), 1) >> 6)
    ms = _dot(o * o, same_head.astype(F32), precision=HIGHEST) * (1.0 / 64.0)
    return o * lax.rsqrt(ms + NORM_EPS) * g * (gate_in * jax.nn.sigmoid(gate_in))


def _gla_kernel(z_ref, wa_ref, ba_ref, g_ref, o_ref, st_ref, oacc_ref, *, tiles_per_seq):
    C = GLA_CHUNK
    tt = z_ref.shape[0]

    @pl.when(pl.program_id(0) % tiles_per_seq == 0)
    def _():
        st_ref[...] = jnp.zeros_like(st_ref)

    q = z_ref[:, 0:128].astype(F32)
    k = z_ref[:, 128:256].astype(F32)
    r = z_ref[:, 512:768].astype(F32)
    a = z_ref[:, 768:896].astype(F32)
    pre = _dot(a, wa_ref[...], precision=HIGHEST) + ba_ref[...]
    log_a = (jnp.minimum(pre, 0.0) - jnp.log(1.0 + jnp.exp(-jnp.abs(pre)))) * (1.0 / GLA_TAU)

    row = _iota((tt, tt), 0)
    col = _iota((tt, tt), 1)
    same_chunk = (row >> 6) == (col >> 6)
    b = _dot((same_chunk & (col <= row)).astype(F32), log_a, precision=HIGHEST)
    b_last = _dot(same_chunk.astype(F32), log_a, precision=HIGHEST)
    q_dec = (q * (GLA_DK ** -0.5) * jnp.exp(b)).astype(BF16)
    k_dec = (k * jnp.exp(-b)).astype(BF16)
    k_end = (k * jnp.exp(b_last - b)).astype(BF16)
    decay = jnp.exp(b_last)

    qk_head = _iota((C, 128), 1) >> 5
    v_head = _iota((C, 256), 1) >> 6
    causal = _iota((4 * C, C), 1) <= (_iota((4 * C, C), 0) & (C - 1))
    state_mask = (_iota((256, 128), 0) >> 6) == (_iota((256, 128), 1) >> 5)

    for c in range(tt // C):
        sl = slice(c * C, (c + 1) * C)
        qd = q_dec[sl]
        v = z_ref[sl, 256:512]
        q_stack = jnp.concatenate([jnp.where(qk_head == hh, qd, 0) for hh in range(GLA_HEADS)], axis=0)
        att = jnp.where(causal, _dot_nt(q_stack, k_dec[sl]), 0.0)
        intra = _head_select(_dot(att.astype(BF16), v), v_head, GLA_HEADS, C)
        state = st_ref[...]
        inter = _dot_nt(qd, state.astype(BF16))
        oacc_ref[sl, :] = intra + inter
        kv = _dot_tn(v, k_end[sl])
        st_ref[...] = state * decay[c * C:c * C + 1] + jnp.where(state_mask, kv, 0.0)

    o_ref[...] = _head_norm_gate(oacc_ref[...], r, g_ref[...]).astype(BF16)


def _gla(z, wa, ba, g, seq):
    T = z.shape[0]
    tt = TOK_TILE
    return pl.pallas_call(
        functools.partial(_gla_kernel, tiles_per_seq=seq // tt),
        grid=(T // tt,),
        in_specs=[pl.BlockSpec((tt, GLA_W), lambda i: (i, 0)), _full((128, 128)), _full((1, 128)),
                  _full((1, 256))],
        out_specs=pl.BlockSpec((tt, 256), lambda i: (i, 0)),
        out_shape=jax.ShapeDtypeStruct((T, 256), BF16),
        scratch_shapes=[pltpu.VMEM((256, 128), F32), pltpu.VMEM((tt, 256), F32)],
        compiler_params=_params("arbitrary"),
        name="gla_mixer",
    )(z, wa, ba, g)


_RET_LOG_GAMMA = [math.log1p(-(2.0 ** (-5.0 - hh))) for hh in range(RET_HEADS)]


def _by_head(head, values):
    out = jnp.full(head.shape, values[-1], F32)
    for hh in range(len(values) - 2, -1, -1):
        out = jnp.where(head == hh, values[hh], out)
    return out


def _ret_kernel(z_ref, cos_ref, sina_ref, sinb_ref, g_ref, o_ref, st_ref, oacc_ref, *, tiles_per_seq):
    C = RET_CHUNK
    tt = z_ref.shape[0]

    @pl.when(pl.program_id(0) % tiles_per_seq == 0)
    def _():
        st_ref[...] = jnp.zeros_like(st_ref)

    cos, sina, sinb = cos_ref[...], sina_ref[...], sinb_ref[...]

    def rotate(t):
        return t * cos + pltpu.roll(t, 112, 1) * sina + pltpu.roll(t, 16, 1) * sinb

    q_rot = rotate(z_ref[:, 0:128].astype(F32))
    k_rot = rotate(z_ref[:, 128:256].astype(F32)) * (RET_DK ** -0.5)
    gate_in = z_ref[:, 512:768].astype(F32)

    qk_head = _iota((C, 128), 1) >> 5
    v_head = _iota((C, 256), 1) >> 6
    lg_qk = _by_head(_iota((1, 128), 1) >> 5, _RET_LOG_GAMMA)
    lg_v = _by_head(_iota((1, 256), 1) >> 6, _RET_LOG_GAMMA)
    pos_qk = _iota((C, 128), 0).astype(F32)
    pos_v = _iota((C, 256), 0).astype(F32)
    key_decay = jnp.exp((C - 1.0 - pos_qk) * lg_qk)
    query_decay = jnp.exp((pos_v + 1.0) * lg_v)
    chunk_decay = jnp.exp(float(C) * lg_qk)
    srow = _iota((4 * C, C), 0)
    rel = ((srow & (C - 1)) - _iota((4 * C, C), 1)).astype(F32)
    decay_mat = jnp.where(rel >= 0, jnp.exp(jnp.maximum(rel, 0.0) * _by_head(srow >> 7, _RET_LOG_GAMMA)), 0.0)
    state_mask = (_iota((256, 128), 0) >> 6) == (_iota((256, 128), 1) >> 5)

    q_bf = q_rot.astype(BF16)
    k_bf = k_rot.astype(BF16)
    for c in range(tt // C):
        sl = slice(c * C, (c + 1) * C)
        qc = q_bf[sl]
        v = z_ref[sl, 256:512]
        q_stack = jnp.concatenate([jnp.where(qk_head == hh, qc, 0) for hh in range(RET_HEADS)], axis=0)
        att = _dot_nt(q_stack, k_bf[sl]) * decay_mat
        intra = _head_select(_dot(att.astype(BF16), v), v_head, RET_HEADS, C)
        state = st_ref[...]
        inter = _dot_nt(qc, state.astype(BF16)) * query_decay
        oacc_ref[sl, :] = intra + inter
        kv = _dot_tn(v, (k_rot[sl] * key_decay).astype(BF16))
        st_ref[...] = state * chunk_decay + jnp.where(state_mask, kv, 0.0)

    o_ref[...] = _head_norm_gate(oacc_ref[...], gate_in, g_ref[...]).astype(BF16)


def _retention(z, cos, sina, sinb, g, seq):
    T = z.shape[0]
    tt = TOK_TILE
    tps = seq // tt
    pos_spec = pl.BlockSpec((tt, 128), lambda i: (i % tps, 0))
    return pl.pallas_call(
        functools.partial(_ret_kernel, tiles_per_seq=tps),
        grid=(T // tt,),
        in_specs=[pl.BlockSpec((tt, RET_W), lambda i: (i, 0)), pos_spec, pos_spec, pos_spec, _full((1, 256))],
        out_specs=pl.BlockSpec((tt, 256), lambda i: (i, 0)),
        out_shape=jax.ShapeDtypeStruct((T, 256), BF16),
        scratch_shapes=[pltpu.VMEM((256, 128), F32), pltpu.VMEM((tt, 256), F32)],
        compiler_params=_params("arbitrary"),
        name="retention_mixer",
    )(z, cos, sina, sinb, g)


def _rope_tables(seq):
    half = RET_DK // 2
    pos = jnp.arange(seq, dtype=F32)
    inv_freq = RET_ROPE_BASE ** (-jnp.arange(half, dtype=F32) * 2.0 / RET_DK)
    ang = pos[:, None] * inv_freq[None, :]
    cos = jnp.tile(jnp.cos(ang), (1, LANES // half))
    sin = jnp.tile(jnp.sin(ang), (1, LANES // half))
    first_half = (jnp.arange(LANES) % RET_DK) < half
    return cos, jnp.where(first_half, -sin, 0.0), jnp.where(first_half, 0.0, sin)


def _swa_kernel(sink_ref, cur_ref, prev_ref, o_ref, *, tiles_per_seq):
    W = SWA_WINDOW
    HD = SWA_HD
    tt = cur_ref.shape[0]
    first_key = jnp.where(pl.program_id(0) % tiles_per_seq == 0, W, 0)
    qi = _iota((W, 2 * W), 0)
    sj = _iota((W, 2 * W), 1)
    in_window = (sj > qi) & (sj <= qi + W)
    kv0 = SWA_HEADS * HD
    for c in range(tt // W):
        rows = slice(c * W, (c + 1) * W)
        if c == 0:
            k_prev, v_prev = prev_ref[:, kv0:kv0 + 128], prev_ref[:, kv0 + 128:kv0 + 256]
            mask = in_window & (sj >= first_key)
        else:
            prows = slice((c - 1) * W, c * W)
            k_prev, v_prev = cur_ref[prows, kv0:kv0 + 128], cur_ref[prows, kv0 + 128:kv0 + 256]
            mask = in_window
        k_band = jnp.concatenate([k_prev, cur_ref[rows, kv0:kv0 + 128]], axis=0)
        v_band = jnp.concatenate([v_prev, cur_ref[rows, kv0 + 128:kv0 + 256]], axis=0)
        outs = []
        for hh in range(SWA_HEADS):
            kk = hh // (SWA_HEADS // SWA_KV_HEADS)
            s = _dot_nt(cur_ref[rows, hh * HD:(hh + 1) * HD], k_band[:, kk * HD:(kk + 1) * HD]) * (HD ** -0.5)
            s = jnp.where(mask, s, NEG_INF)
            sink = sink_ref[hh]
            m = jnp.maximum(jnp.max(s, axis=-1, keepdims=True), sink)
            p = jnp.exp(s - m)
            denom = jnp.sum(p, axis=-1, keepdims=True) + jnp.exp(sink - m)
            outs.append(_dot(p.astype(BF16), v_band[:, kk * HD:(kk + 1) * HD]) / denom)
        o_ref[rows, :] = jnp.concatenate(outs, axis=1).astype(BF16)


def _swa(z, sinks, seq):
    T = z.shape[0]
    tt = TOK_TILE
    per = tt // SWA_WINDOW
    return pl.pallas_call(
        functools.partial(_swa_kernel, tiles_per_seq=seq // tt),
        grid=(T // tt,),
        in_specs=[pl.BlockSpec(memory_space=pltpu.SMEM),
                  pl.BlockSpec((tt, SWA_W), lambda i: (i, 0)),
                  pl.BlockSpec((SWA_WINDOW, SWA_W), lambda i: (jnp.maximum(i * per - 1, 0), 0))],
        out_specs=pl.BlockSpec((tt, 256), lambda i: (i, 0)),
        out_shape=jax.ShapeDtypeStruct((T, 256), BF16),
        compiler_params=_params("parallel"),
        name="swa_mixer",
    )(sinks, z, z)


_KMEAN_BLOCKS = 8


def _kmean_kernel(k_ref, o_ref):
    k = k_ref[...].astype(F32).reshape(_KMEAN_BLOCKS, MOBA_BLOCK, 256)
    o_ref[...] = jnp.mean(k, axis=1)


def _moba_kmean(z):
    T = z.shape[0]
    rows = _KMEAN_BLOCKS * MOBA_BLOCK
    return pl.pallas_call(
        _kmean_kernel,
        grid=(T // rows,),
        in_specs=[pl.BlockSpec((rows, 256), lambda i: (i, 1))],
        out_specs=pl.BlockSpec((_KMEAN_BLOCKS, 256), lambda i: (i, 0)),
        out_shape=jax.ShapeDtypeStruct((T // MOBA_BLOCK, 256), F32),
        compiler_params=_params("parallel"),
        name="moba_kmean",
    )(z)


def _moba_prep_kernel(z_ref, km_ref, q_out, kt_out, v_out):
    HD = MOBA_HD
    tq = z_ref.shape[0]
    nb = km_ref.shape[0]
    qb = pl.program_id(0) % nb
    blk_i = _iota((tq, nb), 1)
    blk = blk_i.astype(F32)
    pad = LANES - HD - nb
    this_block_row = (_iota((nb, tq), 0) == qb).astype(BF16)
    ones_col = (_iota((tq, LANES - HD), 1) == 0).astype(BF16)
    for hh in range(MOBA_HEADS):
        cols = slice(hh * HD, (hh + 1) * HD)
        q = z_ref[:, cols]
        gate = _dot_nt(q.astype(F32), km_ref[:, cols], precision=HIGHEST)
        gate = jnp.where(blk_i < qb, gate, NEG_INF)
        keep = jnp.where(blk_i == qb, 1.0, 0.0)
        for _ in range(MOBA_TOPK):
            best = jnp.max(gate, axis=-1, keepdims=True)
            first = jnp.min(jnp.where(gate == best, blk, float(nb)), axis=-1, keepdims=True)
            hit = blk == first
            keep = keep + jnp.where(hit, jnp.where(best > 0.5 * NEG_INF, 1.0, 0.0), 0.0)
            gate = jnp.where(hit, BELOW_NEG_INF, gate)
        bias = ((keep - 1.0) * (-NEG_INF)).astype(BF16)
        q_out[hh] = jnp.concatenate([q * (HD ** -0.5), bias, jnp.zeros((tq, pad), BF16)], axis=1)
        k_t = z_ref[:, 256 + hh * HD:256 + (hh + 1) * HD].astype(F32).T.astype(BF16)
        kt_out[hh] = jnp.concatenate([k_t, this_block_row, jnp.zeros((pad, tq), BF16)], axis=0)
        v_out[hh] = jnp.concatenate([z_ref[:, 512 + hh * HD:512 + (hh + 1) * HD], ones_col], axis=1)


def _moba_prep(z, kmean, seq):
    T = z.shape[0]
    nb = seq // MOBA_BLOCK
    assert MOBA_HD + nb <= LANES
    tq = MOBA_BLOCK
    H = MOBA_HEADS
    return pl.pallas_call(
        _moba_prep_kernel,
        grid=(T // tq,),
        in_specs=[pl.BlockSpec((tq, MOBA_W), lambda i: (i, 0)), pl.BlockSpec((nb, 256), lambda i: (i // nb, 0))],
        out_specs=[pl.BlockSpec((H, tq, LANES), lambda i: (0, i, 0)),
                   pl.BlockSpec((H, LANES, tq), lambda i: (0, 0, i)),
                   pl.BlockSpec((H, tq, LANES), lambda i: (0, i, 0))],
        out_shape=[jax.ShapeDtypeStruct((H, T, LANES), BF16), jax.ShapeDtypeStruct((H, LANES, T), BF16),
                   jax.ShapeDtypeStruct((H, T, LANES), BF16)],
        compiler_params=_params("parallel"),
        name="moba_prep",
    )(z, kmean)


def _moba_kernel(qidx_ref, kidx_ref, q_ref, kt_ref, v_ref, o_ref, m_ref, acc_ref):
    HD = MOBA_HD
    tq = q_ref.shape[1]
    step = pl.program_id(1)
    qb = qidx_ref[step]
    kb = kidx_ref[step]
    own = kb == qb
    last = (kb == qb - 1) | (qb == 0)

    def attend(first_visit):
        for hh in range(MOBA_HEADS):
            s = _dot(q_ref[hh], kt_ref[hh])
            if first_visit:
                s = jnp.where(_iota((tq, tq), 1) <= _iota((tq, tq), 0), s, NEG_INF)
                m_new = jnp.broadcast_to(jnp.max(s, axis=-1, keepdims=True), (tq, LANES))
            else:
                m_old = m_ref[hh]
                m_new = jnp.maximum(m_old, jnp.max(s, axis=-1, keepdims=True))
            p = jnp.concatenate([jnp.exp(s[:, :LANES] - m_new), jnp.exp(s[:, LANES:] - m_new)], axis=1)
            pv = _dot(p.astype(BF16), v_ref[hh])
            acc_ref[hh] = pv if first_visit else jnp.exp(m_old - m_new) * acc_ref[hh] + pv
            m_ref[hh] = m_new

    @pl.when(own)
    def _():
        attend(True)

    @pl.when(jnp.logical_not(own))
    def _():
        attend(False)

    @pl.when(last)
    def _():
        outs = [acc_ref[hh, :, 0:HD] / acc_ref[hh, :, HD:HD + 1] for hh in range(MOBA_HEADS)]
        o_ref[...] = jnp.concatenate(outs, axis=1).astype(BF16)


def _moba(z, seq):
    T = z.shape[0]
    nb = seq // MOBA_BLOCK
    batch = T // seq
    q_aug, kt_aug, v_aug = _moba_prep(z, _moba_kmean(z), seq)
    qidx = np.concatenate([np.full(b + 1, b) for b in range(nb)]).astype(np.int32)
    kidx = np.concatenate([np.concatenate([[b], np.arange(b)]) for b in range(nb)]).astype(np.int32)
    tq = MOBA_BLOCK
    H = MOBA_HEADS
    grid_spec = pltpu.PrefetchScalarGridSpec(
        num_scalar_prefetch=2,
        grid=(batch, len(qidx)),
        in_specs=[pl.BlockSpec((H, tq, LANES), lambda b, s, qi, ki: (0, b * nb + qi[s], 0)),
                  pl.BlockSpec((H, LANES, tq), lambda b, s, qi, ki: (0, 0, b * nb + ki[s])),
                  pl.BlockSpec((H, tq, LANES), lambda b, s, qi, ki: (0, b * nb + ki[s], 0))],
        out_specs=pl.BlockSpec((tq, 256), lambda b, s, qi, ki: (b * nb + qi[s], 0)),
        scratch_shapes=[pltpu.VMEM((H, tq, LANES), F32), pltpu.VMEM((H, tq, LANES), F32)],
    )
    return pl.pallas_call(
        _moba_kernel,
        grid_spec=grid_spec,
        out_shape=jax.ShapeDtypeStruct((T, 256), BF16),
        compiler_params=_params("parallel", "arbitrary"),
        name="moba_mixer",
    )(jnp.asarray(qidx), jnp.asarray(kidx), q_aug, kt_aug, v_aug)


def _merge_kernel(h_ref, o0, o1, o2, o3, gates_ref, wb_ref, wout_ref, out_ref):
    merged = None
    for i, o in enumerate((o0, o1, o2, o3)):
        gate = jax.nn.sigmoid(gates_ref[:, i * D_MODEL:(i + 1) * D_MODEL].astype(F32))
        term = gate * _dot(o[...], wb_ref[i])
        merged = term if merged is None else merged + term
    out_ref[...] = h_ref[...] + _dot(merged.astype(BF16), wout_ref[...])


def _merge(h, branches, gates, wb, wout):
    T = h.shape[0]
    tm = TOK_TILE
    return pl.pallas_call(
        _merge_kernel,
        grid=(T // tm,),
        in_specs=[pl.BlockSpec((tm, D_MODEL), lambda i: (i, 0))]
        + [pl.BlockSpec((tm, 256), lambda i: (i, 0))] * 4
        + [pl.BlockSpec((tm, GATE_W), lambda i: (i, 0)),
           pl.BlockSpec((N_BRANCH, 256, D_MODEL), lambda i: (0, 0, 0), pipeline_mode=pl.Buffered(1)),
           pl.BlockSpec((D_MODEL, D_MODEL), lambda i: (0, 0), pipeline_mode=pl.Buffered(1))],
        out_specs=pl.BlockSpec((tm, D_MODEL), lambda i: (i, 0)),
        out_shape=jax.ShapeDtypeStruct((T, D_MODEL), F32),
        compiler_params=_params("parallel"),
        name="merge_out_proj",
    )(h, *branches, gates, wb, wout)


ROW_TILES = D_MODEL // LANES


def _row_tile_spec(rows, index_map):
    return pl.BlockSpec((rows, ROW_TILES, LANES), index_map)


def _store_row_tiles(ref, x):
    for j in range(ROW_TILES):
        ref[:, j, :] = x[:, j * LANES:(j + 1) * LANES]


def _load_row_tiles(ref):
    return jnp.concatenate([ref[:, j, :] for j in range(ROW_TILES)], axis=1)


def _lane_first(mask, lane):
    return jnp.min(jnp.where(mask, lane, float(LANES)), axis=-1, keepdims=True)


def _router_kernel(h_ref, g_ref, wr_ref, br_ref, u_ref, route_ref, count_ref, carry_ref):
    tm = h_ref.shape[0]

    @pl.when(pl.program_id(0) == 0)
    def _():
        carry_ref[...] = jnp.zeros_like(carry_ref)

    u = _rms(h_ref[...], g_ref[...])
    _store_row_tiles(u_ref, u)
    logits = _dot(u, wr_ref[...], precision=HIGHEST) + br_ref[...]
    lane_i = _iota((tm, LANES), 1)
    lane = lane_i.astype(F32)
    lane_group = ((lane_i - EXPERT_LANE0) >> 3).astype(F32)

    grp = jnp.where(lane_i < N_GROUPS, logits, BELOW_NEG_INF)
    grp_e = jnp.exp(grp - jnp.max(grp, axis=-1, keepdims=True))
    p_group = grp_e / jnp.sum(grp_e, axis=-1, keepdims=True)
    g_w = jnp.max(p_group, axis=-1, keepdims=True)
    g_sel = _lane_first(p_group == g_w, lane)

    in_group = (lane_i >= EXPERT_LANE0) & (lane_group == g_sel)
    el = jnp.where(in_group, logits, BELOW_NEG_INF)
    top1 = jnp.max(el, axis=-1, keepdims=True)
    lane1 = _lane_first(el == top1, lane)
    el = jnp.where(lane == lane1, BELOW_NEG_INF, el)
    top2 = jnp.max(el, axis=-1, keepdims=True)
    lane2 = _lane_first(el == top2, lane)
    e2 = jnp.exp(top2 - top1)
    w1 = g_w / (1.0 + e2)
    w2 = g_w * e2 / (1.0 + e2)

    onehot = ((lane == lane1) | (lane == lane2)).astype(BF16)
    before = (_iota((tm, tm), 1) < _iota((tm, tm), 0)).astype(BF16)
    seen = _dot(before, onehot) + carry_ref[...]
    rank1 = jnp.sum(jnp.where(lane == lane1, seen, 0.0), axis=-1, keepdims=True)
    rank2 = jnp.sum(jnp.where(lane == lane2, seen, 0.0), axis=-1, keepdims=True)
    carry_ref[...] += jnp.sum(onehot.astype(F32), axis=0, keepdims=True)
    count_ref[...] = carry_ref[...]

    out = jnp.zeros((tm, LANES), F32)
    for idx, val in enumerate((lane1 - EXPERT_LANE0, lane2 - EXPERT_LANE0, w1, w2, rank1, rank2)):
        out = jnp.where(lane_i == idx, val, out)
    route_ref[...] = out


def _router(h, g, wr, br):
    T = h.shape[0]
    tm = TOK_TILE
    return pl.pallas_call(
        _router_kernel,
        grid=(T // tm,),
        in_specs=[pl.BlockSpec((tm, D_MODEL), lambda i: (i, 0)), _full((1, D_MODEL)),
                  _full((D_MODEL, LANES)), _full((1, LANES))],
        out_specs=[_row_tile_spec(tm, lambda i: (i, 0, 0)), pl.BlockSpec((tm, LANES), lambda i: (i, 0)),
                   _full((1, LANES))],
        out_shape=[jax.ShapeDtypeStruct((T, ROW_TILES, LANES), F32), jax.ShapeDtypeStruct((T, LANES), F32),
                   jax.ShapeDtypeStruct((1, LANES), F32)],
        scratch_shapes=[pltpu.VMEM((1, LANES), F32)],
        compiler_params=_params("arbitrary"),
        name="moe_router",
    )(h, g, wr, br)


def _row_copy(src_ref, src_row, dst_ref, dst_row, sem):
    return pltpu.make_async_copy(src_ref.at[pl.ds(src_row, 1)], dst_ref.at[pl.ds(dst_row, 1)], sem)


def _dispatch_kernel(dest_ref, u_ref, xs_in_ref, xs_ref, sem):
    del xs_in_ref
    tm = u_ref.shape[0]
    base = pl.program_id(0) * (2 * tm)

    def issue(r, carry):
        _row_copy(u_ref, r, xs_ref, dest_ref[base + 2 * r], sem).start()
        _row_copy(u_ref, r, xs_ref, dest_ref[base + 2 * r + 1], sem).start()
        return carry

    def drain(r, carry):
        _row_copy(u_ref, 0, xs_ref, 0, sem).wait()
        return carry

    lax.fori_loop(0, tm, issue, 0)
    lax.fori_loop(0, 2 * tm, drain, 0)


def _dispatch(dest, u, n_slots):
    T = u.shape[0]
    tm = ROW_TILE
    grid_spec = pltpu.PrefetchScalarGridSpec(
        num_scalar_prefetch=1,
        grid=(T // tm,),
        in_specs=[_row_tile_spec(tm, lambda i, d: (i, 0, 0)), pl.BlockSpec(memory_space=pl.ANY)],
        out_specs=pl.BlockSpec(memory_space=pl.ANY),
        scratch_shapes=[pltpu.SemaphoreType.DMA(())],
    )
    return pl.pallas_call(
        _dispatch_kernel,
        grid_spec=grid_spec,
        out_shape=jax.ShapeDtypeStruct((n_slots, ROW_TILES, LANES), F32),
        input_output_aliases={2: 0},
        compiler_params=_params("arbitrary"),
        name="moe_dispatch",
    )(dest, u, jnp.zeros((n_slots, ROW_TILES, LANES), F32))


def _expert_kernel(blk_e_ref, n_used_ref, x_ref, w1_ref, w3_ref, w2_ref, y_ref):
    del blk_e_ref
    used = pl.program_id(0) < n_used_ref[0]

    @pl.when(used)
    def _():
        x = _load_row_tiles(x_ref).astype(BF16)
        a = _dot(x, w1_ref[0])
        hidden = (a * jax.nn.sigmoid(a)) * _dot(x, w3_ref[0])
        _store_row_tiles(y_ref, _dot(hidden.astype(BF16), w2_ref[0]))

    @pl.when(jnp.logical_not(used))
    def _():
        y_ref[...] = jnp.zeros_like(y_ref)


def _experts(blk_e, n_used, xs, w1, w3, w2):
    n_slots = xs.shape[0]
    bm = MOE_BLOCK
    grid_spec = pltpu.PrefetchScalarGridSpec(
        num_scalar_prefetch=2,
        grid=(n_slots // bm,),
        in_specs=[_row_tile_spec(bm, lambda i, be, nu: (i, 0, 0)),
                  pl.BlockSpec((1, D_MODEL, EXPERT_HIDDEN), lambda i, be, nu: (be[i], 0, 0)),
                  pl.BlockSpec((1, D_MODEL, EXPERT_HIDDEN), lambda i, be, nu: (be[i], 0, 0)),
                  pl.BlockSpec((1, EXPERT_HIDDEN, D_MODEL), lambda i, be, nu: (be[i], 0, 0))],
        out_specs=_row_tile_spec(bm, lambda i, be, nu: (i, 0, 0)),
    )
    return pl.pallas_call(
        _expert_kernel,
        grid_spec=grid_spec,
        out_shape=jax.ShapeDtypeStruct((n_slots, ROW_TILES, LANES), F32),
        compiler_params=_params("parallel"),
        name="moe_experts",
    )(blk_e, n_used, xs, w1, w3, w2)


def _combine_kernel(dest_ref, h_ref, route_ref, ys_ref, out_ref, y0_ref, y1_ref, sem):
    tm = h_ref.shape[0]
    base = pl.program_id(0) * (2 * tm)

    def issue(r, carry):
        _row_copy(ys_ref, dest_ref[base + 2 * r], y0_ref, r, sem).start()
        _row_copy(ys_ref, dest_ref[base + 2 * r + 1], y1_ref, r, sem).start()
        return carry

    def drain(r, carry):
        _row_copy(ys_ref, 0, y0_ref, 0, sem).wait()
        return carry

    lax.fori_loop(0, tm, issue, 0)
    lax.fori_loop(0, 2 * tm, drain, 0)
    w0 = route_ref[:, 2:3]
    w1 = route_ref[:, 3:4]
    for j in range(ROW_TILES):
        cols = slice(j * LANES, (j + 1) * LANES)
        out_ref[:, cols] = h_ref[:, cols] + w0 * y0_ref[:, j, :] + w1 * y1_ref[:, j, :]


def _combine(dest, h, route, ys):
    T = h.shape[0]
    tm = ROW_TILE
    grid_spec = pltpu.PrefetchScalarGridSpec(
        num_scalar_prefetch=1,
        grid=(T // tm,),
        in_specs=[pl.BlockSpec((tm, D_MODEL), lambda i, d: (i, 0)),
                  pl.BlockSpec((tm, LANES), lambda i, d: (i, 0)),
                  pl.BlockSpec(memory_space=pl.ANY)],
        out_specs=pl.BlockSpec((tm, D_MODEL), lambda i, d: (i, 0)),
        scratch_shapes=[pltpu.VMEM((tm, ROW_TILES, LANES), F32), pltpu.VMEM((tm, ROW_TILES, LANES), F32),
                        pltpu.SemaphoreType.DMA(())],
    )
    return pl.pallas_call(
        _combine_kernel,
        grid_spec=grid_spec,
        out_shape=jax.ShapeDtypeStruct((T, D_MODEL), F32),
        compiler_params=_params("arbitrary"),
        name="moe_combine",
    )(dest, h, route, ys)


def _moe(h, g, w_group, b_group, w_expert, b_expert, w1, w3, w2):
    T = h.shape[0]
    wr = jnp.zeros((D_MODEL, LANES), F32)
    wr = wr.at[:, :N_GROUPS].set(w_group).at[:, EXPERT_LANE0:EXPERT_LANE0 + N_EXPERTS].set(w_expert)
    br = jnp.zeros((1, LANES), F32)
    br = br.at[0, :N_GROUPS].set(b_group).at[0, EXPERT_LANE0:EXPERT_LANE0 + N_EXPERTS].set(b_expert)
    u, route, count = _router(h, g, wr, br)

    n_blocks = (T * EXPERT_TOPK) // MOE_BLOCK + N_EXPERTS
    counts = count[0, EXPERT_LANE0:EXPERT_LANE0 + N_EXPERTS].astype(I32)
    padded = ((counts + MOE_BLOCK - 1) // MOE_BLOCK) * MOE_BLOCK
    pend = jnp.cumsum(padded)
    pstart = pend - padded
    expert = route[:, 0:2].astype(I32)
    dest = (pstart[expert] + route[:, 4:6].astype(I32)).reshape(T * EXPERT_TOPK)
    block_row0 = jnp.arange(n_blocks, dtype=I32) * MOE_BLOCK
    blk_e = jnp.minimum(jnp.sum((pend[None, :] <= block_row0[:, None]).astype(I32), axis=1), N_EXPERTS - 1)
    n_used = (pend[-1:] // MOE_BLOCK).astype(I32)

    xs = _dispatch(dest, u, n_blocks * MOE_BLOCK)
    ys = _experts(blk_e, n_used, xs, w1, w3, w2)
    return _combine(dest, h, route, ys)


def _final_norm_kernel(h_ref, g_ref, o_ref):
    o_ref[...] = _rms(h_ref[...], g_ref[...])


def _final_norm(h, g):
    T = h.shape[0]
    tm = TOK_TILE
    return pl.pallas_call(
        _final_norm_kernel,
        grid=(T // tm,),
        in_specs=[pl.BlockSpec((tm, D_MODEL), lambda i: (i, 0)), _full((1, D_MODEL))],
        out_specs=pl.BlockSpec((tm, D_MODEL), lambda i: (i, 0)),
        out_shape=jax.ShapeDtypeStruct((T, D_MODEL), F32),
        compiler_params=_params("parallel"),
        name="final_norm",
    )(h, g)


def _split_w_in(w_in):
    gla_n = GLA_HEADS * (2 * GLA_DK + 2 * GLA_DV) + GLA_LOWRANK
    swa_n = (SWA_HEADS + 2 * SWA_KV_HEADS) * SWA_HD
    moba_n = 3 * MOBA_HEADS * MOBA_HD
    ret_n = RET_HEADS * (2 * RET_DK + 2 * RET_DV)
    offs = np.cumsum([0, gla_n, swa_n, moba_n, ret_n, GATE_W])
    parts = [w_in[:, offs[i]:offs[i + 1]].astype(BF16) for i in range(5)]
    parts[0] = jnp.pad(parts[0], ((0, 0), (0, GLA_W - gla_n)))
    return parts


def _token_mixer(h, seq, ln1_g, w_in, gla_w_a2, gla_b_a, gla_norm_g, swa_sinks, ret_norm_g, w_branch, w_out,
                 rope):
    z_gla, z_swa, z_moba, z_ret, z_gates = _in_proj(h, ln1_g.reshape(1, D_MODEL), _split_w_in(w_in))
    wa = jnp.zeros((128, 128), F32).at[:GLA_LOWRANK].set(gla_w_a2)
    o_gla = _gla(z_gla, wa, gla_b_a.reshape(1, 128), jnp.tile(gla_norm_g, GLA_HEADS).reshape(1, 256), seq)
    o_swa = _swa(z_swa, swa_sinks, seq)
    o_moba = _moba(z_moba, seq)
    o_ret = _retention(z_ret, *rope, jnp.tile(ret_norm_g, RET_HEADS).reshape(1, 256), seq)
    return _merge(h, (o_gla, o_swa, o_moba, o_ret), z_gates, w_branch.astype(BF16), w_out.astype(BF16))


def kernel(x, ln1_g, w_in, gla_w_a2, gla_b_a, gla_norm_g, swa_sinks, ret_norm_g, w_branch, w_out, ln2_g,
           w_group, b_group, w_expert, b_expert, w1, w3, w2, final_g):
    batch, seq, _ = x.shape
    depth = w_in.shape[0]
    rope = _rope_tables(seq)
    h = x.reshape(batch * seq, D_MODEL)
    for l in range(depth):
        h = _token_mixer(h, seq, ln1_g[l], w_in[l], gla_w_a2[l], gla_b_a[l], gla_norm_g[l], swa_sinks[l],
                         ret_norm_g[l], w_branch[l], w_out[l], rope)
        h = _moe(h, ln2_g[l].reshape(1, D_MODEL), w_group[l], b_group[l], w_expert[l], b_expert[l],
                 w1[l].astype(BF16), w3[l].astype(BF16), w2[l].astype(BF16))
    return _final_norm(h, final_g.reshape(1, D_MODEL)).reshape(batch, seq, D_MODEL)
```

```python
import functools
import math

import numpy as np
import jax
import jax.numpy as jnp
from jax import lax
from jax.experimental import pallas as pl
from jax.experimental.pallas import tpu as pltpu

F32 = jnp.float32
BF16 = jnp.bfloat16
I32 = jnp.int32
HIGHEST = lax.Precision.HIGHEST

D_MODEL = 1024
N_BRANCH = 4
NORM_EPS = 1e-6
NEG_INF = -1e30
BELOW_NEG_INF = -3e38

GLA_HEADS, GLA_DK, GLA_DV, GLA_LOWRANK, GLA_TAU, GLA_CHUNK = 4, 32, 64, 16, 16.0, 64
SWA_HEADS, SWA_KV_HEADS, SWA_HD, SWA_WINDOW = 4, 2, 64, 128
MOBA_HEADS, MOBA_HD, MOBA_BLOCK, MOBA_TOPK = 4, 64, 256, 3
RET_HEADS, RET_DK, RET_DV, RET_CHUNK, RET_ROPE_BASE = 4, 32, 64, 128, 10000.0
N_GROUPS, EXPERTS_PER_GROUP, EXPERT_TOPK, EXPERT_HIDDEN, MOE_BLOCK = 4, 8, 2, 256, 256
N_EXPERTS = N_GROUPS * EXPERTS_PER_GROUP

GLA_W = 896
SWA_W = 512
MOBA_W = 768
RET_W = 768
GATE_W = N_BRANCH * D_MODEL

LANES = 128
EXPERT_LANE0 = 32

VMEM_LIMIT = 56 * 1024 * 1024

TOK_TILE = 512
ROW_TILE = 256


def _params(*sem):
    return pltpu.CompilerParams(dimension_semantics=sem, vmem_limit_bytes=VMEM_LIMIT)


def _dot(a, b, precision=None):
    return jnp.dot(a, b, preferred_element_type=F32, precision=precision)


def _dot_nt(a, b, precision=None):
    return lax.dot_general(a, b, (((1,), (1,)), ((), ())), preferred_element_type=F32, precision=precision)


def _dot_tn(a, b, precision=None):
    return lax.dot_general(a, b, (((0,), (0,)), ((), ())), preferred_element_type=F32, precision=precision)


def _iota(shape, dim):
    return lax.broadcasted_iota(I32, shape, dim)


def _rms(x, g):
    ms = jnp.mean(x * x, axis=-1, keepdims=True)
    return x * lax.rsqrt(ms + NORM_EPS) * g


def _full(shape):
    return pl.BlockSpec(shape, lambda *_: (0,) * len(shape))


def _in_proj_kernel(h_ref, g_ref, w0, w1, w2, w3, w4, o0, o1, o2, o3, o4):
    u = _rms(h_ref[...], g_ref[...]).astype(BF16)
    for w, o in ((w0, o0), (w1, o1), (w2, o2), (w3, o3), (w4, o4)):
        o[...] = _dot(u, w[...]).astype(BF16)


def _in_proj(h, g, ws):
    T = h.shape[0]
    tm = TOK_TILE
    widths = [w.shape[1] for w in ws]
    return pl.pallas_call(
        _in_proj_kernel,
        grid=(T // tm,),
        in_specs=[pl.BlockSpec((tm, D_MODEL), lambda i: (i, 0)), _full((1, D_MODEL))]
        + [pl.BlockSpec((D_MODEL, n), lambda i: (0, 0), pipeline_mode=pl.Buffered(1)) for n in widths],
        out_specs=[pl.BlockSpec((tm, n), lambda i: (i, 0)) for n in widths],
        out_shape=[jax.ShapeDtypeStruct((T, n), BF16) for n in widths],
        compiler_params=_params("parallel"),
        name="in_proj",
    )(h, g, *ws)


def _head_select(x, lane_head, n_heads, rows):
    out = None
    for hh in range(n_heads):
        term = jnp.where(lane_head == hh, x[hh * rows:(hh + 1) * rows], 0.0)
        out = term if out is None else out + term
    return out


def _head_norm_gate(o, gate_in, g):
    n = o.shape[1]
    same_head = (_iota((n, n), 0) >> 6) == (_iota((n, n), 1) >> 6)
    ms = _dot(o * o, same_head.astype(F32), precision=HIGHEST) * (1.0 / 64.0)
    return o * lax.rsqrt(ms + NORM_EPS) * g * (gate_in * jax.nn.sigmoid(gate_in))


def _gla_kernel(z_ref, wa_ref, ba_ref, g_ref, o_ref, st_ref, oacc_ref, *, tiles_per_seq):
    C = GLA_CHUNK
    tt = z_ref.shape[0]

    @pl.when(pl.program_id(0) % tiles_per_seq == 0)
    def _():
        st_ref[...] = jnp.zeros_like(st_ref)

    q = z_ref[:, 0:128].astype(F32)
    k = z_ref[:, 128:256].astype(F32)
    r = z_ref[:, 512:768].astype(F32)
    a = z_ref[:, 768:896].astype(F32)
    pre = _dot(a, wa_ref[...], precision=HIGHEST) + ba_ref[...]
    log_a = (jnp.minimum(pre, 0.0) - jnp.log(1.0 + jnp.exp(-jnp.abs(pre)))) * (1.0 / GLA_TAU)

    row = _iota((tt, tt), 0)
    col = _iota((tt, tt), 1)
    same_chunk = (row >> 6) == (col >> 6)
    b = _dot((same_chunk & (col <= row)).astype(F32), log_a, precision=HIGHEST)
    b_last = _dot(same_chunk.astype(F32), log_a, precision=HIGHEST)
    q_dec = (q * (GLA_DK ** -0.5) * jnp.exp(b)).astype(BF16)
    k_dec = (k * jnp.exp(-b)).astype(BF16)
    k_end = (k * jnp.exp(b_last - b)).astype(BF16)
    decay = jnp.exp(b_last)

    qk_head = _iota((C, 128), 1) >> 5
    v_head = _iota((C, 256), 1) >> 6
    causal = _iota((4 * C, C), 1) <= (_iota((4 * C, C), 0) & (C - 1))
    state_mask = (_iota((256, 128), 0) >> 6) == (_iota((256, 128), 1) >> 5)

    for c in range(tt // C):
        sl = slice(c * C, (c + 1) * C)
        qd = q_dec[sl]
        v = z_ref[sl, 256:512]
        q_stack = jnp.concatenate([jnp.where(qk_head == hh, qd, 0) for hh in range(GLA_HEADS)], axis=0)
        att = jnp.where(causal, _dot_nt(q_stack, k_dec[sl]), 0.0)
        intra = _head_select(_dot(att.astype(BF16), v), v_head, GLA_HEADS, C)
        state = st_ref[...]
        inter = _dot_nt(qd, state.astype(BF16))
        oacc_ref[sl, :] = intra + inter
        kv = _dot_tn(v, k_end[sl])
        st_ref[...] = state * decay[c * C:c * C + 1] + jnp.where(state_mask, kv, 0.0)

    o_ref[...] = _head_norm_gate(oacc_ref[...], r, g_ref[...]).astype(BF16)


def _gla(z, wa, ba, g, seq):
    T = z.shape[0]
    tt = TOK_TILE
    return pl.pallas_call(
        functools.partial(_gla_kernel, tiles_per_seq=seq // tt),
        grid=(T // tt,),
        in_specs=[pl.BlockSpec((tt, GLA_W), lambda i: (i, 0)), _full((128, 128)), _full((1, 128)),
                  _full((1, 256))],
        out_specs=pl.BlockSpec((tt, 256), lambda i: (i, 0)),
        out_shape=jax.ShapeDtypeStruct((T, 256), BF16),
        scratch_shapes=[pltpu.VMEM((256, 128), F32), pltpu.VMEM((tt, 256), F32)],
        compiler_params=_params("arbitrary"),
        name="gla_mixer",
    )(z, wa, ba, g)


_RET_LOG_GAMMA = [math.log1p(-(2.0 ** (-5.0 - hh))) for hh in range(RET_HEADS)]


def _by_head(head, values):
    out = jnp.full(head.shape, values[-1], F32)
    for hh in range(len(values) - 2, -1, -1):
        out = jnp.where(head == hh, values[hh], out)
    return out


def _ret_kernel(z_ref, cos_ref, sina_ref, sinb_ref, g_ref, o_ref, st_ref, oacc_ref, *, tiles_per_seq):
    C = RET_CHUNK
    tt = z_ref.shape[0]

    @pl.when(pl.program_id(0) % tiles_per_seq == 0)
    def _():
        st_ref[...] = jnp.zeros_like(st_ref)

    cos, sina, sinb = cos_ref[...], sina_ref[...], sinb_ref[...]

    def rotate(t):
        return t * cos + pltpu.roll(t, 112, 1) * sina + pltpu.roll(t, 16, 1) * sinb

    q_rot = rotate(z_ref[:, 0:128].astype(F32))
    k_rot = rotate(z_ref[:, 128:256].astype(F32)) * (RET_DK ** -0.5)
    gate_in = z_ref[:, 512:768].astype(F32)

    qk_head = _iota((C, 128), 1) >> 5
    v_head = _iota((C, 256), 1) >> 6
    lg_qk = _by_head(_iota((1, 128), 1) >> 5, _RET_LOG_GAMMA)
    lg_v = _by_head(_iota((1, 256), 1) >> 6, _RET_LOG_GAMMA)
    pos_qk = _iota((C, 128), 0).astype(F32)
    pos_v = _iota((C, 256), 0).astype(F32)
    key_decay = jnp.exp((C - 1.0 - pos_qk) * lg_qk)
    query_decay = jnp.exp((pos_v + 1.0) * lg_v)
    chunk_decay = jnp.exp(float(C) * lg_qk)
    srow = _iota((4 * C, C), 0)
    rel = ((srow & (C - 1)) - _iota((4 * C, C), 1)).astype(F32)
    decay_mat = jnp.where(rel >= 0, jnp.exp(jnp.maximum(rel, 0.0) * _by_head(srow >> 7, _RET_LOG_GAMMA)), 0.0)
    state_mask = (_iota((256, 128), 0) >> 6) == (_iota((256, 128), 1) >> 5)

    q_bf = q_rot.astype(BF16)
    k_bf = k_rot.astype(BF16)
    for c in range(tt // C):
        sl = slice(c * C, (c + 1) * C)
        qc = q_bf[sl]
        v = z_ref[sl, 256:512]
        q_stack = jnp.concatenate([jnp.where(qk_head == hh, qc, 0) for hh in range(RET_HEADS)], axis=0)
        att = _dot_nt(q_stack, k_bf[sl]) * decay_mat
        intra = _head_select(_dot(att.astype(BF16), v), v_head, RET_HEADS, C)
        state = st_ref[...]
        inter = _dot_nt(qc, state.astype(BF16)) * query_decay
        oacc_ref[sl, :] = intra + inter
        kv = _dot_tn(v, (k_rot[sl] * key_decay).astype(BF16))
        st_ref[...] = state * chunk_decay + jnp.where(state_mask, kv, 0.0)

    o_ref[...] = _head_norm_gate(oacc_ref[...], gate_in, g_ref[...]).astype(BF16)


def _retention(z, cos, sina, sinb, g, seq):
    T = z.shape[0]
    tt = TOK_TILE
    tps = seq // tt
    pos_spec = pl.BlockSpec((tt, 128), lambda i: (i % tps, 0))
    return pl.pallas_call(
        functools.partial(_ret_kernel, tiles_per_seq=tps),
        grid=(T // tt,),
        in_specs=[pl.BlockSpec((tt, RET_W), lambda i: (i, 0)), pos_spec, pos_spec, pos_spec, _full((1, 256))],
        out_specs=pl.BlockSpec((tt, 256), lambda i: (i, 0)),
        out_shape=jax.ShapeDtypeStruct((T, 256), BF16),
        scratch_shapes=[pltpu.VMEM((256, 128), F32), pltpu.VMEM((tt, 256), F32)],
        compiler_params=_params("arbitrary"),
        name="retention_mixer",
    )(z, cos, sina, sinb, g)


def _rope_tables(seq):
    half = RET_DK // 2
    pos = jnp.arange(seq, dtype=F32)
    inv_freq = RET_ROPE_BASE ** (-jnp.arange(half, dtype=F32) * 2.0 / RET_DK)
    ang = pos[:, None] * inv_freq[None, :]
    cos = jnp.tile(jnp.cos(ang), (1, LANES // half))
    sin = jnp.tile(jnp.sin(ang), (1, LANES // half))
    first_half = (jnp.arange(LANES) % RET_DK) < half
    return cos, jnp.where(first_half, -sin, 0.0), jnp.where(first_half, 0.0, sin)


def _swa_kernel(sink_ref, cur_ref, prev_ref, o_ref, *, tiles_per_seq):
    W = SWA_WINDOW
    HD = SWA_HD
    tt = cur_ref.shape[0]
    first_key = jnp.where(pl.program_id(0) % tiles_per_seq == 0, W, 0)
    qi = _iota((W, 2 * W), 0)
    sj = _iota((W, 2 * W), 1)
    in_window = (sj > qi) & (sj <= qi + W)
    kv0 = SWA_HEADS * HD
    for c in range(tt // W):
        rows = slice(c * W, (c + 1) * W)
        if c == 0:
            k_prev, v_prev = prev_ref[:, kv0:kv0 + 128], prev_ref[:, kv0 + 128:kv0 + 256]
            mask = in_window & (sj >= first_key)
        else:
            prows = slice((c - 1) * W, c * W)
            k_prev, v_prev = cur_ref[prows, kv0:kv0 + 128], cur_ref[prows, kv0 + 128:kv0 + 256]
            mask = in_window
        k_band = jnp.concatenate([k_prev, cur_ref[rows, kv0:kv0 + 128]], axis=0)
        v_band = jnp.concatenate([v_prev, cur_ref[rows, kv0 + 128:kv0 + 256]], axis=0)
        outs = []
        for hh in range(SWA_HEADS):
            kk = hh // (SWA_HEADS // SWA_KV_HEADS)
            s = _dot_nt(cur_ref[rows, hh * HD:(hh + 1) * HD], k_band[:, kk * HD:(kk + 1) * HD]) * (HD ** -0.5)
            s = jnp.where(mask, s, NEG_INF)
            sink = sink_ref[hh]
            m = jnp.maximum(jnp.max(s, axis=-1, keepdims=True), sink)
            p = jnp.exp(s - m)
            denom = jnp.sum(p, axis=-1, keepdims=True) + jnp.exp(sink - m)
            outs.append(_dot(p.astype(BF16), v_band[:, kk * HD:(kk + 1) * HD]) / denom)
        o_ref[rows, :] = jnp.concatenate(outs, axis=1).astype(BF16)


def _swa(z, sinks, seq):
    T = z.shape[0]
    tt = TOK_TILE
    per = tt // SWA_WINDOW
    return pl.pallas_call(
        functools.partial(_swa_kernel, tiles_per_seq=seq // tt),
        grid=(T // tt,),
        in_specs=[pl.BlockSpec(memory_space=pltpu.SMEM),
                  pl.BlockSpec((tt, SWA_W), lambda i: (i, 0)),
                  pl.BlockSpec((SWA_WINDOW, SWA_W), lambda i: (jnp.maximum(i * per - 1, 0), 0))],
        out_specs=pl.BlockSpec((tt, 256), lambda i: (i, 0)),
        out_shape=jax.ShapeDtypeStruct((T, 256), BF16),
        compiler_params=_params("parallel"),
        name="swa_mixer",
    )(sinks, z, z)


_KMEAN_BLOCKS = 8


def _kmean_kernel(k_ref, o_ref):
    k = k_ref[...].astype(F32).reshape(_KMEAN_BLOCKS, MOBA_BLOCK, 256)
    o_ref[...] = jnp.mean(k, axis=1)


def _moba_kmean(z):
    T = z.shape[0]
    rows = _KMEAN_BLOCKS * MOBA_BLOCK
    return pl.pallas_call(
        _kmean_kernel,
        grid=(T // rows,),
        in_specs=[pl.BlockSpec((rows, 256), lambda i: (i, 1))],
        out_specs=pl.BlockSpec((_KMEAN_BLOCKS, 256), lambda i: (i, 0)),
        out_shape=jax.ShapeDtypeStruct((T // MOBA_BLOCK, 256), F32),
        compiler_params=_params("parallel"),
        name="moba_kmean",
    )(z)


def _moba_prep_kernel(z_ref, km_ref, q_out, kt_out, v_out):
    HD = MOBA_HD
    tq = z_ref.shape[0]
    nb = km_ref.shape[0]
    qb = pl.program_id(0) % nb
    blk_i = _iota((tq, nb), 1)
    blk = blk_i.astype(F32)
    pad = LANES - HD - nb
    this_block_row = (_iota((nb, tq), 0) == qb).astype(BF16)
    ones_col = (_iota((tq, LANES - HD), 1) == 0).astype(BF16)
    for hh in range(MOBA_HEADS):
        cols = slice(hh * HD, (hh + 1) * HD)
        q = z_ref[:, cols]
        gate = _dot_nt(q.astype(F32), km_ref[:, cols], precision=HIGHEST)
        gate = jnp.where(blk_i < qb, gate, NEG_INF)
        keep = jnp.where(blk_i == qb, 1.0, 0.0)
        for _ in range(MOBA_TOPK):
            best = jnp.max(gate, axis=-1, keepdims=True)
            first = jnp.min(jnp.where(gate == best, blk, float(nb)), axis=-1, keepdims=True)
            hit = blk == first
            keep = keep + jnp.where(hit, jnp.where(best > 0.5 * NEG_INF, 1.0, 0.0), 0.0)
            gate = jnp.where(hit, BELOW_NEG_INF, gate)
        bias = ((keep - 1.0) * (-NEG_INF)).astype(BF16)
        q_out[hh] = jnp.concatenate([q * (HD ** -0.5), bias, jnp.zeros((tq, pad), BF16)], axis=1)
        k_t = z_ref[:, 256 + hh * HD:256 + (hh + 1) * HD].astype(F32).T.astype(BF16)
        kt_out[hh, 0] = jnp.concatenate([k_t, this_block_row, jnp.zeros((pad, tq), BF16)], axis=0)
        v_out[hh] = jnp.concatenate([z_ref[:, 512 + hh * HD:512 + (hh + 1) * HD], ones_col], axis=1)


def _moba_prep(z, kmean, seq):
    T = z.shape[0]
    nb = seq // MOBA_BLOCK
    assert MOBA_HD + nb <= LANES
    tq = MOBA_BLOCK
    H = MOBA_HEADS
    return pl.pallas_call(
        _moba_prep_kernel,
        grid=(T // tq,),
        in_specs=[pl.BlockSpec((tq, MOBA_W), lambda i: (i, 0)), pl.BlockSpec((nb, 256), lambda i: (i // nb, 0))],
        out_specs=[pl.BlockSpec((H, tq, LANES), lambda i: (0, i, 0)),
                   pl.BlockSpec((H, 1, LANES, tq), lambda i: (0, i, 0, 0)),
                   pl.BlockSpec((H, tq, LANES), lambda i: (0, i, 0))],
        out_shape=[jax.ShapeDtypeStruct((H, T, LANES), BF16), jax.ShapeDtypeStruct((H, T // tq, LANES, tq), BF16),
                   jax.ShapeDtypeStruct((H, T, LANES), BF16)],
        compiler_params=_params("parallel"),
        name="moba_prep",
    )(z, kmean)


def _moba_kernel(qidx_ref, kidx_ref, q_ref, kt_ref, v_ref, o_ref, m_ref, acc_ref):
    HD = MOBA_HD
    tq = q_ref.shape[1]
    step = pl.program_id(1)
    qb = qidx_ref[step]
    kb = kidx_ref[step]
    own = kb == qb
    last = (kb == qb - 1) | (qb == 0)

    def attend(first_visit):
        for hh in range(MOBA_HEADS):
            s = _dot(q_ref[hh], kt_ref[hh, 0])
            if first_visit:
                s = jnp.where(_iota((tq, tq), 1) <= _iota((tq, tq), 0), s, NEG_INF)
                m_new = jnp.broadcast_to(jnp.max(s, axis=-1, keepdims=True), (tq, LANES))
            else:
                m_old = m_ref[hh]
                m_new = jnp.maximum(m_old, jnp.max(s, axis=-1, keepdims=True))
            p = jnp.concatenate([jnp.exp(s[:, :LANES] - m_new), jnp.exp(s[:, LANES:] - m_new)], axis=1)
            pv = _dot(p.astype(BF16), v_ref[hh])
            acc_ref[hh] = pv if first_visit else jnp.exp(m_old - m_new) * acc_ref[hh] + pv
            m_ref[hh] = m_new

    @pl.when(own)
    def _():
        attend(True)

    @pl.when(jnp.logical_not(own))
    def _():
        attend(False)

    @pl.when(last)
    def _():
        outs = [acc_ref[hh, :, 0:HD] / acc_ref[hh, :, HD:HD + 1] for hh in range(MOBA_HEADS)]
        o_ref[...] = jnp.concatenate(outs, axis=1).astype(BF16)


def _moba(z, seq):
    T = z.shape[0]
    nb = seq // MOBA_BLOCK
    batch = T // seq
    q_aug, kt_aug, v_aug = _moba_prep(z, _moba_kmean(z), seq)
    qidx = np.concatenate([np.full(b + 1, b) for b in range(nb)]).astype(np.int32)
    kidx = np.concatenate([np.concatenate([[b], np.arange(b)]) for b in range(nb)]).astype(np.int32)
    tq = MOBA_BLOCK
    H = MOBA_HEADS
    grid_spec = pltpu.PrefetchScalarGridSpec(
        num_scalar_prefetch=2,
        grid=(batch, len(qidx)),
        in_specs=[pl.BlockSpec((H, tq, LANES), lambda b, s, qi, ki: (0, b * nb + qi[s], 0)),
                  pl.BlockSpec((H, 1, LANES, tq), lambda b, s, qi, ki: (0, b * nb + ki[s], 0, 0)),
                  pl.BlockSpec((H, tq, LANES), lambda b, s, qi, ki: (0, b * nb + ki[s], 0))],
        out_specs=pl.BlockSpec((tq, 256), lambda b, s, qi, ki: (b * nb + qi[s], 0)),
        scratch_shapes=[pltpu.VMEM((H, tq, LANES), F32), pltpu.VMEM((H, tq, LANES), F32)],
    )
    return pl.pallas_call(
        _moba_kernel,
        grid_spec=grid_spec,
        out_shape=jax.ShapeDtypeStruct((T, 256), BF16),
        compiler_params=_params("parallel", "arbitrary"),
        name="moba_mixer",
    )(jnp.asarray(qidx), jnp.asarray(kidx), q_aug, kt_aug, v_aug)


def _merge_kernel(h_ref, o0, o1, o2, o3, gates_ref, wb_ref, wout_ref, out_ref):
    merged = None
    for i, o in enumerate((o0, o1, o2, o3)):
        gate = jax.nn.sigmoid(gates_ref[:, i * D_MODEL:(i + 1) * D_MODEL].astype(F32))
        term = gate * _dot(o[...], wb_ref[i])
        merged = term if merged is None else merged + term
    out_ref[...] = h_ref[...] + _dot(merged.astype(BF16), wout_ref[...])


def _merge(h, branches, gates, wb, wout):
    T = h.shape[0]
    tm = TOK_TILE
    return pl.pallas_call(
        _merge_kernel,
        grid=(T // tm,),
        in_specs=[pl.BlockSpec((tm, D_MODEL), lambda i: (i, 0))]
        + [pl.BlockSpec((tm, 256), lambda i: (i, 0))] * 4
        + [pl.BlockSpec((tm, GATE_W), lambda i: (i, 0)),
           pl.BlockSpec((N_BRANCH, 256, D_MODEL), lambda i: (0, 0, 0), pipeline_mode=pl.Buffered(1)),
           pl.BlockSpec((D_MODEL, D_MODEL), lambda i: (0, 0), pipeline_mode=pl.Buffered(1))],
        out_specs=pl.BlockSpec((tm, D_MODEL), lambda i: (i, 0)),
        out_shape=jax.ShapeDtypeStruct((T, D_MODEL), F32),
        compiler_params=_params("parallel"),
        name="merge_out_proj",
    )(h, *branches, gates, wb, wout)


ROW_TILES = D_MODEL // LANES


def _row_tile_spec(rows, index_map):
    return pl.BlockSpec((rows * ROW_TILES, LANES), index_map)


def _row_slab(ref, j):
    return ref[pl.ds(j, ref.shape[0] // ROW_TILES, stride=ROW_TILES), :]


def _store_row_tiles(ref, x):
    for j in range(ROW_TILES):
        ref[pl.ds(j, x.shape[0], stride=ROW_TILES), :] = x[:, j * LANES:(j + 1) * LANES]


def _load_row_tiles(ref):
    return jnp.concatenate([_row_slab(ref, j) for j in range(ROW_TILES)], axis=1)


def _lane_first(mask, lane):
    return jnp.min(jnp.where(mask, lane, float(LANES)), axis=-1, keepdims=True)


def _router_kernel(h_ref, g_ref, wr_ref, br_ref, u_ref, route_ref, count_ref, carry_ref):
    tm = h_ref.shape[0]

    @pl.when(pl.program_id(0) == 0)
    def _():
        carry_ref[...] = jnp.zeros_like(carry_ref)

    u = _rms(h_ref[...], g_ref[...])
    _store_row_tiles(u_ref, u)
    logits = _dot(u, wr_ref[...], precision=HIGHEST) + br_ref[...]
    lane_i = _iota((tm, LANES), 1)
    lane = lane_i.astype(F32)
    lane_group = ((lane_i - EXPERT_LANE0) >> 3).astype(F32)

    grp = jnp.where(lane_i < N_GROUPS, logits, BELOW_NEG_INF)
    grp_e = jnp.exp(grp - jnp.max(grp, axis=-1, keepdims=True))
    p_group = grp_e / jnp.sum(grp_e, axis=-1, keepdims=True)
    g_w = jnp.max(p_group, axis=-1, keepdims=True)
    g_sel = _lane_first(p_group == g_w, lane)

    in_group = (lane_i >= EXPERT_LANE0) & (lane_group == g_sel)
    el = jnp.where(in_group, logits, BELOW_NEG_INF)
    top1 = jnp.max(el, axis=-1, keepdims=True)
    lane1 = _lane_first(el == top1, lane)
    el = jnp.where(lane == lane1, BELOW_NEG_INF, el)
    top2 = jnp.max(el, axis=-1, keepdims=True)
    lane2 = _lane_first(el == top2, lane)
    e2 = jnp.exp(top2 - top1)
    w1 = g_w / (1.0 + e2)
    w2 = g_w * e2 / (1.0 + e2)

    onehot = ((lane == lane1) | (lane == lane2)).astype(BF16)
    before = (_iota((tm, tm), 1) < _iota((tm, tm), 0)).astype(BF16)
    seen = _dot(before, onehot) + carry_ref[...]
    rank1 = jnp.sum(jnp.where(lane == lane1, seen, 0.0), axis=-1, keepdims=True)
    rank2 = jnp.sum(jnp.where(lane == lane2, seen, 0.0), axis=-1, keepdims=True)
    carry_ref[...] += jnp.sum(onehot.astype(F32), axis=0, keepdims=True)
    count_ref[...] = carry_ref[...]

    out = jnp.zeros((tm, LANES), F32)
    for idx, val in enumerate((lane1 - EXPERT_LANE0, lane2 - EXPERT_LANE0, w1, w2, rank1, rank2)):
        out = jnp.where(lane_i == idx, val, out)
    route_ref[...] = out


def _router(h, g, wr, br):
    T = h.shape[0]
    tm = TOK_TILE
    return pl.pallas_call(
        _router_kernel,
        grid=(T // tm,),
        in_specs=[pl.BlockSpec((tm, D_MODEL), lambda i: (i, 0)), _full((1, D_MODEL)),
                  _full((D_MODEL, LANES)), _full((1, LANES))],
        out_specs=[_row_tile_spec(tm, lambda i: (i, 0)), pl.BlockSpec((tm, LANES), lambda i: (i, 0)),
                   _full((1, LANES))],
        out_shape=[jax.ShapeDtypeStruct((T * ROW_TILES, LANES), F32), jax.ShapeDtypeStruct((T, LANES), F32),
                   jax.ShapeDtypeStruct((1, LANES), F32)],
        scratch_shapes=[pltpu.VMEM((1, LANES), F32)],
        compiler_params=_params("arbitrary"),
        name="moe_router",
    )(h, g, wr, br)


ROW_UNROLL = 8


def _row_copy(src_ref, src_row, dst_ref, dst_row, sem):
    src = src_ref.at[pl.ds(pl.multiple_of(src_row * ROW_TILES, ROW_TILES), ROW_TILES)]
    dst = dst_ref.at[pl.ds(pl.multiple_of(dst_row * ROW_TILES, ROW_TILES), ROW_TILES)]
    return pltpu.make_async_copy(src, dst, sem)


def _for_each_row_pair(rows, body):
    def group(g, carry):
        for i in range(ROW_UNROLL):
            for k in range(EXPERT_TOPK):
                body(g * ROW_UNROLL + i, k)
        return carry

    lax.fori_loop(0, rows // ROW_UNROLL, group, 0)


def _dispatch_kernel(dest0_ref, dest1_ref, pstart_ref, pend_ref, u_ref, xs_ref, zero_ref, sem):
    tm = u_ref.shape[0] // ROW_TILES
    tile = pl.program_id(0)
    dests = (dest0_ref, dest1_ref)

    @pl.when(tile == 0)
    def _():
        zero_ref[...] = jnp.zeros_like(zero_ref)

        def zero_block(first_row):
            start = pl.multiple_of(first_row * ROW_TILES, ROW_TILES)
            fill = pltpu.make_async_copy(zero_ref, xs_ref.at[pl.ds(start, MOE_BLOCK * ROW_TILES)], sem)
            fill.start()
            fill.wait()

        def zero_tail(e, carry):
            @pl.when(pend_ref[e] > pstart_ref[e])
            def _():
                zero_block(pend_ref[e] - MOE_BLOCK)
            return carry

        def zero_unused(b, carry):
            zero_block(b * MOE_BLOCK)
            return carry

        lax.fori_loop(0, N_EXPERTS, zero_tail, 0)
        lax.fori_loop(pend_ref[N_EXPERTS - 1] // MOE_BLOCK, xs_ref.shape[0] // (MOE_BLOCK * ROW_TILES),
                      zero_unused, 0)

    def issue(r, k):
        _row_copy(u_ref, r, xs_ref, dests[k][tile * tm + r], sem).start(priority=k)

    _for_each_row_pair(tm, issue)
    for _ in range(EXPERT_TOPK):
        pltpu.make_async_copy(u_ref, xs_ref.at[pl.ds(0, tm * ROW_TILES)], sem).wait()


def _dispatch(dest0, dest1, pstart, pend, u, n_slots):
    T = u.shape[0] // ROW_TILES
    tm = ROW_TILE
    grid_spec = pltpu.PrefetchScalarGridSpec(
        num_scalar_prefetch=4,
        grid=(T // tm,),
        in_specs=[_row_tile_spec(tm, lambda i, *_: (i, 0))],
        out_specs=pl.BlockSpec(memory_space=pl.ANY),
        scratch_shapes=[pltpu.VMEM((MOE_BLOCK * ROW_TILES, LANES), F32), pltpu.SemaphoreType.DMA(())],
    )
    return pl.pallas_call(
        _dispatch_kernel,
        grid_spec=grid_spec,
        out_shape=jax.ShapeDtypeStruct((n_slots * ROW_TILES, LANES), F32),
        compiler_params=_params("arbitrary"),
        name="moe_dispatch",
    )(dest0, dest1, pstart, pend, u)


def _expert_kernel(blk_e_ref, n_used_ref, x_ref, w1_ref, w3_ref, w2_ref, y_ref, w1b_ref, w3b_ref, w2b_ref):
    i = pl.program_id(0)
    used = i < n_used_ref[0]

    @pl.when((i == 0) | (blk_e_ref[i] != blk_e_ref[jnp.maximum(i - 1, 0)]))
    def _():
        w1b_ref[...] = w1_ref[0, 0].astype(BF16)
        w3b_ref[...] = w3_ref[0, 0].astype(BF16)
        w2b_ref[...] = w2_ref[0, 0].astype(BF16)

    @pl.when(used)
    def _():
        x = _load_row_tiles(x_ref).astype(BF16)
        a = _dot(x, w1b_ref[...])
        hidden = (a * jax.nn.sigmoid(a)) * _dot(x, w3b_ref[...])
        _store_row_tiles(y_ref, _dot(hidden.astype(BF16), w2b_ref[...]))

    @pl.when(jnp.logical_not(used))
    def _():
        y_ref[...] = jnp.zeros_like(y_ref)


def _experts(blk_e, n_used, xs, w1, w3, w2, layer):
    n_slots = xs.shape[0] // ROW_TILES
    bm = MOE_BLOCK
    grid_spec = pltpu.PrefetchScalarGridSpec(
        num_scalar_prefetch=2,
        grid=(n_slots // bm,),
        in_specs=[_row_tile_spec(bm, lambda i, be, nu: (jnp.minimum(i, nu[0] - 1), 0)),
                  pl.BlockSpec((1, 1, D_MODEL, EXPERT_HIDDEN), lambda i, be, nu: (layer, be[i], 0, 0)),
                  pl.BlockSpec((1, 1, D_MODEL, EXPERT_HIDDEN), lambda i, be, nu: (layer, be[i], 0, 0)),
                  pl.BlockSpec((1, 1, EXPERT_HIDDEN, D_MODEL), lambda i, be, nu: (layer, be[i], 0, 0))],
        out_specs=_row_tile_spec(bm, lambda i, be, nu: (i, 0)),
        scratch_shapes=[pltpu.VMEM((D_MODEL, EXPERT_HIDDEN), BF16), pltpu.VMEM((D_MODEL, EXPERT_HIDDEN), BF16),
                        pltpu.VMEM((EXPERT_HIDDEN, D_MODEL), BF16)],
    )
    return pl.pallas_call(
        _expert_kernel,
        grid_spec=grid_spec,
        out_shape=jax.ShapeDtypeStruct((n_slots * ROW_TILES, LANES), F32),
        compiler_params=_params("arbitrary"),
        name="moe_experts",
    )(blk_e, n_used, xs, w1, w3, w2)


def _combine_kernel(dest0_ref, dest1_ref, h_ref, route_ref, ys_ref, out_ref, y_ref, sem):
    tm = h_ref.shape[0]
    step = pl.program_id(0)
    slot = step % 2

    dests = (dest0_ref, dest1_ref)

    def gather_tile(tile, buf):
        def issue(r, k):
            _row_copy(ys_ref, dests[k][tile * tm + r], y_ref.at[buf, k], r, sem.at[buf]).start(priority=k)

        _for_each_row_pair(tm, issue)

    @pl.when(step == 0)
    def _():
        gather_tile(0, 0)

    @pl.when(step + 1 < pl.num_programs(0))
    def _():
        gather_tile(step + 1, 1 - slot)

    for k in range(EXPERT_TOPK):
        pltpu.make_async_copy(ys_ref.at[pl.ds(0, tm * ROW_TILES)], y_ref.at[slot, k], sem.at[slot]).wait()
    w0 = route_ref[:, 2:3]
    w1 = route_ref[:, 3:4]
    for j in range(ROW_TILES):
        cols = slice(j * LANES, (j + 1) * LANES)
        out_ref[:, cols] = (h_ref[:, cols] + w0 * _row_slab(y_ref.at[slot, 0], j)
                            + w1 * _row_slab(y_ref.at[slot, 1], j))


def _combine(dest0, dest1, h, route, ys):
    T = h.shape[0]
    tm = ROW_TILE
    grid_spec = pltpu.PrefetchScalarGridSpec(
        num_scalar_prefetch=2,
        grid=(T // tm,),
        in_specs=[pl.BlockSpec((tm, D_MODEL), lambda i, *_: (i, 0)),
                  pl.BlockSpec((tm, LANES), lambda i, *_: (i, 0)),
                  pl.BlockSpec(memory_space=pl.ANY)],
        out_specs=pl.BlockSpec((tm, D_MODEL), lambda i, *_: (i, 0)),
        scratch_shapes=[pltpu.VMEM((2, EXPERT_TOPK, tm * ROW_TILES, LANES), F32),
                        pltpu.SemaphoreType.DMA((2,))],
    )
    return pl.pallas_call(
        _combine_kernel,
        grid_spec=grid_spec,
        out_shape=jax.ShapeDtypeStruct((T, D_MODEL), F32),
        compiler_params=_params("arbitrary"),
        name="moe_combine",
    )(dest0, dest1, h, route, ys)


def _moe(h, g, w_group, b_group, w_expert, b_expert, w1, w3, w2, layer):
    T = h.shape[0]
    wr = jnp.zeros((D_MODEL, LANES), F32)
    wr = wr.at[:, :N_GROUPS].set(w_group).at[:, EXPERT_LANE0:EXPERT_LANE0 + N_EXPERTS].set(w_expert)
    br = jnp.zeros((1, LANES), F32)
    br = br.at[0, :N_GROUPS].set(b_group).at[0, EXPERT_LANE0:EXPERT_LANE0 + N_EXPERTS].set(b_expert)
    u, route, count = _router(h, g, wr, br)

    n_blocks = (T * EXPERT_TOPK) // MOE_BLOCK + N_EXPERTS
    counts = count[0, EXPERT_LANE0:EXPERT_LANE0 + N_EXPERTS].astype(I32)
    padded = ((counts + MOE_BLOCK - 1) // MOE_BLOCK) * MOE_BLOCK
    pend = jnp.cumsum(padded)
    pstart = pend - padded
    dest0 = pstart[route[:, 0].astype(I32)] + route[:, 4].astype(I32)
    dest1 = pstart[route[:, 1].astype(I32)] + route[:, 5].astype(I32)
    block_row0 = jnp.arange(n_blocks, dtype=I32) * MOE_BLOCK
    blk_e = jnp.minimum(jnp.sum((pend[None, :] <= block_row0[:, None]).astype(I32), axis=1), N_EXPERTS - 1)
    n_used = (pend[-1:] // MOE_BLOCK).astype(I32)

    xs = _dispatch(dest0, dest1, pstart, pend, u, n_blocks * MOE_BLOCK)
    ys = _experts(blk_e, n_used, xs, w1, w3, w2, layer)
    return _combine(dest0, dest1, h, route, ys)


def _final_norm_kernel(h_ref, g_ref, o_ref):
    o_ref[...] = _rms(h_ref[...], g_ref[...])


def _final_norm(h, g):
    T = h.shape[0]
    tm = TOK_TILE
    return pl.pallas_call(
        _final_norm_kernel,
        grid=(T // tm,),
        in_specs=[pl.BlockSpec((tm, D_MODEL), lambda i: (i, 0)), _full((1, D_MODEL))],
        out_specs=pl.BlockSpec((tm, D_MODEL), lambda i: (i, 0)),
        out_shape=jax.ShapeDtypeStruct((T, D_MODEL), F32),
        compiler_params=_params("parallel"),
        name="final_norm",
    )(h, g)


def _split_w_in(w_in):
    gla_n = GLA_HEADS * (2 * GLA_DK + 2 * GLA_DV) + GLA_LOWRANK
    swa_n = (SWA_HEADS + 2 * SWA_KV_HEADS) * SWA_HD
    moba_n = 3 * MOBA_HEADS * MOBA_HD
    ret_n = RET_HEADS * (2 * RET_DK + 2 * RET_DV)
    offs = np.cumsum([0, gla_n, swa_n, moba_n, ret_n, GATE_W])
    parts = [w_in[:, offs[i]:offs[i + 1]].astype(BF16) for i in range(5)]
    parts[0] = jnp.pad(parts[0], ((0, 0), (0, GLA_W - gla_n)))
    return parts


def _token_mixer(h, seq, ln1_g, w_in, gla_w_a2, gla_b_a, gla_norm_g, swa_sinks, ret_norm_g, w_branch, w_out,
                 rope):
    z_gla, z_swa, z_moba, z_ret, z_gates = _in_proj(h, ln1_g.reshape(1, D_MODEL), _split_w_in(w_in))
    wa = jnp.zeros((128, 128), F32).at[:GLA_LOWRANK].set(gla_w_a2)
    o_gla = _gla(z_gla, wa, gla_b_a.reshape(1, 128), jnp.tile(gla_norm_g, GLA_HEADS).reshape(1, 256), seq)
    o_swa = _swa(z_swa, swa_sinks, seq)
    o_moba = _moba(z_moba, seq)
    o_ret = _retention(z_ret, *rope, jnp.tile(ret_norm_g, RET_HEADS).reshape(1, 256), seq)
    return _merge(h, (o_gla, o_swa, o_moba, o_ret), z_gates, w_branch.astype(BF16), w_out.astype(BF16))


def kernel(x, ln1_g, w_in, gla_w_a2, gla_b_a, gla_norm_g, swa_sinks, ret_norm_g, w_branch, w_out, ln2_g,
           w_group, b_group, w_expert, b_expert, w1, w3, w2, final_g):
    batch, seq, _ = x.shape
    depth = w_in.shape[0]
    rope = _rope_tables(seq)
    h = x.reshape(batch * seq, D_MODEL)
    for l in range(depth):
        h = _token_mixer(h, seq, ln1_g[l], w_in[l], gla_w_a2[l], gla_b_a[l], gla_norm_g[l], swa_sinks[l],
                         ret_norm_g[l], w_branch[l], w_out[l], rope)
        h = _moe(h, ln2_g[l].reshape(1, D_MODEL), w_group[l], b_group[l], w_expert[l], b_expert[l],
                 w1, w3, w2, l)
    return _final_norm(h, final_g.reshape(1, D_MODEL)).reshape(batch, seq, D_MODEL)
```

```python
import functools
import math

import numpy as np
import jax
import jax.numpy as jnp
from jax import lax
from jax.experimental import pallas as pl
from jax.experimental.pallas import tpu as pltpu

F32 = jnp.float32
BF16 = jnp.bfloat16
I32 = jnp.int32
HIGHEST = lax.Precision.HIGHEST

D_MODEL = 1024
N_BRANCH = 4
NORM_EPS = 1e-6
NEG_INF = -1e30
BELOW_NEG_INF = -3e38

GLA_HEADS, GLA_DK, GLA_DV, GLA_LOWRANK, GLA_TAU, GLA_CHUNK = 4, 32, 64, 16, 16.0, 64
SWA_HEADS, SWA_KV_HEADS, SWA_HD, SWA_WINDOW = 4, 2, 64, 128
MOBA_HEADS, MOBA_HD, MOBA_BLOCK, MOBA_TOPK = 4, 64, 256, 3
RET_HEADS, RET_DK, RET_DV, RET_CHUNK, RET_ROPE_BASE = 4, 32, 64, 128, 10000.0
N_GROUPS, EXPERTS_PER_GROUP, EXPERT_TOPK, EXPERT_HIDDEN, MOE_BLOCK = 4, 8, 2, 256, 256
N_EXPERTS = N_GROUPS * EXPERTS_PER_GROUP

GLA_W = 896
SWA_W = 512
MOBA_W = 768
RET_W = 768
GATE_W = N_BRANCH * D_MODEL

LANES = 128
EXPERT_LANE0 = 32

VMEM_LIMIT = 56 * 1024 * 1024

TOK_TILE = 512
ROW_TILE = 256


def _params(*sem):
    return pltpu.CompilerParams(dimension_semantics=sem, vmem_limit_bytes=VMEM_LIMIT)


def _dot(a, b, precision=None):
    return jnp.dot(a, b, preferred_element_type=F32, precision=precision)


def _dot_nt(a, b, precision=None):
    return lax.dot_general(a, b, (((1,), (1,)), ((), ())), preferred_element_type=F32, precision=precision)


def _dot_tn(a, b, precision=None):
    return lax.dot_general(a, b, (((0,), (0,)), ((), ())), preferred_element_type=F32, precision=precision)


def _split_bf16(x, parts):
    out = []
    for _ in range(parts - 1):
        piece = x.astype(BF16)
        out.append(piece)
        x = x - piece.astype(F32)
    out.append(x.astype(BF16))
    return out


def _dot_f32_by_bf16(x, m):
    return sum(_dot(piece, m) for piece in _split_bf16(x, 3))


def _dot_bf16_by_f32(m, x):
    return sum(_dot(m, piece) for piece in _split_bf16(x, 3))


def _iota(shape, dim):
    return lax.broadcasted_iota(I32, shape, dim)


def _rms(x, g):
    ms = jnp.mean(x * x, axis=-1, keepdims=True)
    return x * lax.rsqrt(ms + NORM_EPS) * g


def _full(shape):
    return pl.BlockSpec(shape, lambda *_: (0,) * len(shape))


def _in_proj_kernel(h_ref, g_ref, w0, w1, w2, w3, w4, o0, o1, o2, o3, o4):
    u = _rms(h_ref[...], g_ref[...]).astype(BF16)
    for w, o in ((w0, o0), (w1, o1), (w2, o2), (w3, o3), (w4, o4)):
        o[...] = _dot(u, w[...]).astype(BF16)


def _in_proj(h, g, ws):
    T = h.shape[0]
    tm = TOK_TILE
    widths = [w.shape[1] for w in ws]
    return pl.pallas_call(
        _in_proj_kernel,
        grid=(T // tm,),
        in_specs=[pl.BlockSpec((tm, D_MODEL), lambda i: (i, 0)), _full((1, D_MODEL))]
        + [pl.BlockSpec((D_MODEL, n), lambda i: (0, 0), pipeline_mode=pl.Buffered(1)) for n in widths],
        out_specs=[pl.BlockSpec((tm, n), lambda i: (i, 0)) for n in widths],
        out_shape=[jax.ShapeDtypeStruct((T, n), BF16) for n in widths],
        compiler_params=_params("parallel"),
        name="in_proj",
    )(h, g, *ws)


def _head_select(x, lane_head, n_heads, rows):
    out = None
    for hh in range(n_heads):
        term = jnp.where(lane_head == hh, x[hh * rows:(hh + 1) * rows], 0.0)
        out = term if out is None else out + term
    return out


def _head_norm_gate(o, gate_in, g):
    n = o.shape[1]
    same_head = (_iota((n, n), 0) >> 6) == (_iota((n, n), 1) >> 6)
    ms = _dot_f32_by_bf16(o * o, same_head.astype(BF16)) * (1.0 / 64.0)
    return o * lax.rsqrt(ms + NORM_EPS) * g * (gate_in * jax.nn.sigmoid(gate_in))


def _gla_kernel(z_ref, wa_ref, ba_ref, g_ref, o_ref, st_ref, oacc_ref, *, tiles_per_seq):
    C = GLA_CHUNK
    tt = z_ref.shape[0]

    @pl.when(pl.program_id(0) % tiles_per_seq == 0)
    def _():
        st_ref[...] = jnp.zeros_like(st_ref)

    q = z_ref[:, 0:128].astype(F32)
    k = z_ref[:, 128:256].astype(F32)
    r = z_ref[:, 512:768].astype(F32)
    pre = _dot_bf16_by_f32(z_ref[:, 768:896], wa_ref[...]) + ba_ref[...]
    log_a = (jnp.minimum(pre, 0.0) - jnp.log(1.0 + jnp.exp(-jnp.abs(pre)))) * (1.0 / GLA_TAU)

    lower = (_iota((C, C), 1) <= _iota((C, C), 0)).astype(BF16)
    b = jnp.concatenate([_dot_bf16_by_f32(lower, log_a[c * C:(c + 1) * C]) for c in range(tt // C)], axis=0)
    b_last = jnp.broadcast_to(b.reshape(tt // C, C, 128)[:, C - 1:C, :], (tt // C, C, 128)).reshape(tt, 128)
    q_dec = (q * (GLA_DK ** -0.5) * jnp.exp(b)).astype(BF16)
    k_dec = (k * jnp.exp(-b)).astype(BF16)
    k_end = (k * jnp.exp(b_last - b)).astype(BF16)
    decay = jnp.exp(b_last)

    qk_head = _iota((C, 128), 1) >> 5
    v_head = _iota((C, 256), 1) >> 6
    causal = _iota((4 * C, C), 1) <= (_iota((4 * C, C), 0) & (C - 1))
    state_mask = (_iota((256, 128), 0) >> 6) == (_iota((256, 128), 1) >> 5)

    for c in range(tt // C):
        sl = slice(c * C, (c + 1) * C)
        qd = q_dec[sl]
        v = z_ref[sl, 256:512]
        q_stack = jnp.concatenate([jnp.where(qk_head == hh, qd, 0) for hh in range(GLA_HEADS)], axis=0)
        att = jnp.where(causal, _dot_nt(q_stack, k_dec[sl]), 0.0)
        intra = _head_select(_dot(att.astype(BF16), v), v_head, GLA_HEADS, C)
        state = st_ref[...]
        inter = _dot_nt(qd, state.astype(BF16))
        oacc_ref[sl, :] = intra + inter
        kv = _dot_tn(v, k_end[sl])
        st_ref[...] = state * decay[c * C:c * C + 1] + jnp.where(state_mask, kv, 0.0)

    o_ref[...] = _head_norm_gate(oacc_ref[...], r, g_ref[...]).astype(BF16)


def _gla(z, wa, ba, g, seq):
    T = z.shape[0]
    tt = TOK_TILE
    return pl.pallas_call(
        functools.partial(_gla_kernel, tiles_per_seq=seq // tt),
        grid=(T // tt,),
        in_specs=[pl.BlockSpec((tt, GLA_W), lambda i: (i, 0)), _full((128, 128)), _full((1, 128)),
                  _full((1, 256))],
        out_specs=pl.BlockSpec((tt, 256), lambda i: (i, 0)),
        out_shape=jax.ShapeDtypeStruct((T, 256), BF16),
        scratch_shapes=[pltpu.VMEM((256, 128), F32), pltpu.VMEM((tt, 256), F32)],
        compiler_params=_params("arbitrary"),
        name="gla_mixer",
    )(z, wa, ba, g)


_RET_LOG_GAMMA = [math.log1p(-(2.0 ** (-5.0 - hh))) for hh in range(RET_HEADS)]


def _by_head(head, values):
    out = jnp.full(head.shape, values[-1], F32)
    for hh in range(len(values) - 2, -1, -1):
        out = jnp.where(head == hh, values[hh], out)
    return out


def _ret_kernel(z_ref, cos_ref, sina_ref, sinb_ref, g_ref, o_ref, st_ref, oacc_ref, *, tiles_per_seq):
    C = RET_CHUNK
    tt = z_ref.shape[0]

    @pl.when(pl.program_id(0) % tiles_per_seq == 0)
    def _():
        st_ref[...] = jnp.zeros_like(st_ref)

    cos, sina, sinb = cos_ref[...], sina_ref[...], sinb_ref[...]

    def rotate(t):
        return t * cos + pltpu.roll(t, 112, 1) * sina + pltpu.roll(t, 16, 1) * sinb

    q_rot = rotate(z_ref[:, 0:128].astype(F32))
    k_rot = rotate(z_ref[:, 128:256].astype(F32)) * (RET_DK ** -0.5)
    gate_in = z_ref[:, 512:768].astype(F32)

    qk_head = _iota((C, 128), 1) >> 5
    v_head = _iota((C, 256), 1) >> 6
    lg_qk = _by_head(_iota((1, 128), 1) >> 5, _RET_LOG_GAMMA)
    lg_v = _by_head(_iota((1, 256), 1) >> 6, _RET_LOG_GAMMA)
    pos_qk = _iota((C, 128), 0).astype(F32)
    pos_v = _iota((C, 256), 0).astype(F32)
    key_decay = jnp.exp((C - 1.0 - pos_qk) * lg_qk)
    query_decay = jnp.exp((pos_v + 1.0) * lg_v)
    chunk_decay = jnp.exp(float(C) * lg_qk)
    srow = _iota((4 * C, C), 0)
    rel = ((srow & (C - 1)) - _iota((4 * C, C), 1)).astype(F32)
    decay_mat = jnp.where(rel >= 0, jnp.exp(jnp.maximum(rel, 0.0) * _by_head(srow >> 7, _RET_LOG_GAMMA)), 0.0)
    state_mask = (_iota((256, 128), 0) >> 6) == (_iota((256, 128), 1) >> 5)

    q_bf = q_rot.astype(BF16)
    k_bf = k_rot.astype(BF16)
    for c in range(tt // C):
        sl = slice(c * C, (c + 1) * C)
        qc = q_bf[sl]
        v = z_ref[sl, 256:512]
        q_stack = jnp.concatenate([jnp.where(qk_head == hh, qc, 0) for hh in range(RET_HEADS)], axis=0)
        att = _dot_nt(q_stack, k_bf[sl]) * decay_mat
        intra = _head_select(_dot(att.astype(BF16), v), v_head, RET_HEADS, C)
        state = st_ref[...]
        inter = _dot_nt(qc, state.astype(BF16)) * query_decay
        oacc_ref[sl, :] = intra + inter
        kv = _dot_tn(v, (k_rot[sl] * key_decay).astype(BF16))
        st_ref[...] = state * chunk_decay + jnp.where(state_mask, kv, 0.0)

    o_ref[...] = _head_norm_gate(oacc_ref[...], gate_in, g_ref[...]).astype(BF16)


def _retention(z, cos, sina, sinb, g, seq):
    T = z.shape[0]
    tt = TOK_TILE
    tps = seq // tt
    pos_spec = pl.BlockSpec((tt, 128), lambda i: (i % tps, 0))
    return pl.pallas_call(
        functools.partial(_ret_kernel, tiles_per_seq=tps),
        grid=(T // tt,),
        in_specs=[pl.BlockSpec((tt, RET_W), lambda i: (i, 0)), pos_spec, pos_spec, pos_spec, _full((1, 256))],
        out_specs=pl.BlockSpec((tt, 256), lambda i: (i, 0)),
        out_shape=jax.ShapeDtypeStruct((T, 256), BF16),
        scratch_shapes=[pltpu.VMEM((256, 128), F32), pltpu.VMEM((tt, 256), F32)],
        compiler_params=_params("arbitrary"),
        name="retention_mixer",
    )(z, cos, sina, sinb, g)


def _rope_tables(seq):
    half = RET_DK // 2
    pos = jnp.arange(seq, dtype=F32)
    inv_freq = RET_ROPE_BASE ** (-jnp.arange(half, dtype=F32) * 2.0 / RET_DK)
    ang = pos[:, None] * inv_freq[None, :]
    cos = jnp.tile(jnp.cos(ang), (1, LANES // half))
    sin = jnp.tile(jnp.sin(ang), (1, LANES // half))
    first_half = (jnp.arange(LANES) % RET_DK) < half
    return cos, jnp.where(first_half, -sin, 0.0), jnp.where(first_half, 0.0, sin)


def _swa_kernel(sink_ref, cur_ref, prev_ref, o_ref, *, tiles_per_seq):
    W = SWA_WINDOW
    HD = SWA_HD
    tt = cur_ref.shape[0]
    first_key = jnp.where(pl.program_id(0) % tiles_per_seq == 0, W, 0)
    qi = _iota((W, 2 * W), 0)
    sj = _iota((W, 2 * W), 1)
    in_window = (sj > qi) & (sj <= qi + W)
    kv0 = SWA_HEADS * HD
    for c in range(tt // W):
        rows = slice(c * W, (c + 1) * W)
        if c == 0:
            k_prev, v_prev = prev_ref[:, kv0:kv0 + 128], prev_ref[:, kv0 + 128:kv0 + 256]
            mask = in_window & (sj >= first_key)
        else:
            prows = slice((c - 1) * W, c * W)
            k_prev, v_prev = cur_ref[prows, kv0:kv0 + 128], cur_ref[prows, kv0 + 128:kv0 + 256]
            mask = in_window
        k_band = jnp.concatenate([k_prev, cur_ref[rows, kv0:kv0 + 128]], axis=0)
        v_band = jnp.concatenate([v_prev, cur_ref[rows, kv0 + 128:kv0 + 256]], axis=0)
        outs = []
        for hh in range(SWA_HEADS):
            kk = hh // (SWA_HEADS // SWA_KV_HEADS)
            s = _dot_nt(cur_ref[rows, hh * HD:(hh + 1) * HD], k_band[:, kk * HD:(kk + 1) * HD]) * (HD ** -0.5)
            s = jnp.where(mask, s, NEG_INF)
            sink = sink_ref[hh]
            m = jnp.maximum(jnp.max(s, axis=-1, keepdims=True), sink)
            p = jnp.exp(s - m)
            denom = jnp.sum(p, axis=-1, keepdims=True) + jnp.exp(sink - m)
            outs.append(_dot(p.astype(BF16), v_band[:, kk * HD:(kk + 1) * HD]) / denom)
        o_ref[rows, :] = jnp.concatenate(outs, axis=1).astype(BF16)


def _swa(z, sinks, seq):
    T = z.shape[0]
    tt = TOK_TILE
    per = tt // SWA_WINDOW
    return pl.pallas_call(
        functools.partial(_swa_kernel, tiles_per_seq=seq // tt),
        grid=(T // tt,),
        in_specs=[pl.BlockSpec(memory_space=pltpu.SMEM),
                  pl.BlockSpec((tt, SWA_W), lambda i: (i, 0)),
                  pl.BlockSpec((SWA_WINDOW, SWA_W), lambda i: (jnp.maximum(i * per - 1, 0), 0))],
        out_specs=pl.BlockSpec((tt, 256), lambda i: (i, 0)),
        out_shape=jax.ShapeDtypeStruct((T, 256), BF16),
        compiler_params=_params("parallel"),
        name="swa_mixer",
    )(sinks, z, z)


_KMEAN_BLOCKS = 8


def _kmean_kernel(k_ref, o_ref):
    k = k_ref[...].astype(F32).reshape(_KMEAN_BLOCKS, MOBA_BLOCK, 256)
    o_ref[...] = jnp.mean(k, axis=1)


def _moba_kmean(z):
    T = z.shape[0]
    rows = _KMEAN_BLOCKS * MOBA_BLOCK
    return pl.pallas_call(
        _kmean_kernel,
        grid=(T // rows,),
        in_specs=[pl.BlockSpec((rows, 256), lambda i: (i, 1))],
        out_specs=pl.BlockSpec((_KMEAN_BLOCKS, 256), lambda i: (i, 0)),
        out_shape=jax.ShapeDtypeStruct((T // MOBA_BLOCK, 256), F32),
        compiler_params=_params("parallel"),
        name="moba_kmean",
    )(z)


def _moba_prep_kernel(z_ref, km_ref, qt_out, k_out, vt_out):
    HD = MOBA_HD
    tq = z_ref.shape[0]
    nb = km_ref.shape[0]
    qb = pl.program_id(0) % nb
    blk_i = _iota((nb, tq), 0)
    blk = blk_i.astype(F32)
    pad = LANES - HD - nb
    this_block_lanes = (_iota((tq, nb), 1) == qb).astype(BF16)
    ones_row = (_iota((LANES - HD, tq), 0) == 0).astype(BF16)
    for hh in range(MOBA_HEADS):
        cols = slice(hh * HD, (hh + 1) * HD)
        q_t = z_ref[:, cols].astype(F32).T
        gate = _dot(km_ref[:, cols], q_t, precision=HIGHEST)
        gate = jnp.where(blk_i < qb, gate, NEG_INF)
        keep = jnp.where(blk_i == qb, 1.0, 0.0)
        for _ in range(MOBA_TOPK):
            best = jnp.max(gate, axis=0, keepdims=True)
            first = jnp.min(jnp.where(gate == best, blk, float(nb)), axis=0, keepdims=True)
            hit = blk == first
            keep = keep + jnp.where(hit, jnp.where(best > 0.5 * NEG_INF, 1.0, 0.0), 0.0)
            gate = jnp.where(hit, BELOW_NEG_INF, gate)
        bias = ((keep - 1.0) * (-NEG_INF)).astype(BF16)
        qt_out[hh, 0] = jnp.concatenate([(q_t * (HD ** -0.5)).astype(BF16), bias, jnp.zeros((pad, tq), BF16)],
                                        axis=0)
        k_out[hh] = jnp.concatenate([z_ref[:, 256 + hh * HD:256 + (hh + 1) * HD], this_block_lanes,
                                     jnp.zeros((tq, pad), BF16)], axis=1)
        v_t = z_ref[:, 512 + hh * HD:512 + (hh + 1) * HD].astype(F32).T.astype(BF16)
        vt_out[hh, 0] = jnp.concatenate([v_t, ones_row], axis=0)


def _moba_prep(z, kmean, seq):
    T = z.shape[0]
    nb = seq // MOBA_BLOCK
    assert MOBA_HD + nb <= LANES
    tq = MOBA_BLOCK
    H = MOBA_HEADS
    per_block_t = pl.BlockSpec((H, 1, LANES, tq), lambda i: (0, i, 0, 0))
    return pl.pallas_call(
        _moba_prep_kernel,
        grid=(T // tq,),
        in_specs=[pl.BlockSpec((tq, MOBA_W), lambda i: (i, 0)), pl.BlockSpec((nb, 256), lambda i: (i // nb, 0))],
        out_specs=[per_block_t, pl.BlockSpec((H, tq, LANES), lambda i: (0, i, 0)), per_block_t],
        out_shape=[jax.ShapeDtypeStruct((H, T // tq, LANES, tq), BF16), jax.ShapeDtypeStruct((H, T, LANES), BF16),
                   jax.ShapeDtypeStruct((H, T // tq, LANES, tq), BF16)],
        compiler_params=_params("parallel"),
        name="moba_prep",
    )(z, kmean)


MOBA_PAIR = 2


def _moba_kernel(qidx_ref, kidx_ref, qt_ref, k_ref, vt_ref, o_ref, m_ref, acc_ref, s_ref):
    HD = MOBA_HD
    B = MOBA_BLOCK
    step = pl.program_id(1)
    qt = qidx_ref[step]
    kt = kidx_ref[step]
    diagonal_step = kt == qt
    last = (kt == qt - 1) | (qt == 0)

    def attend(diagonal):
        causal = _iota((B, B), 0) <= _iota((B, B), 1)
        key_blocks_of = [range(qh + 1) if diagonal else range(MOBA_PAIR) for qh in range(MOBA_PAIR)]
        for hh in range(MOBA_HEADS):
            for qh in range(MOBA_PAIR):
                for c in key_blocks_of[qh]:
                    s = _dot(k_ref[hh, c * B:(c + 1) * B, :], qt_ref[hh, qh])
                    if diagonal and c == qh:
                        s = jnp.where(causal, s, NEG_INF)
                    s_ref[hh, qh, c] = s
        for hh in range(MOBA_HEADS):
            for qh in range(MOBA_PAIR):
                key_blocks = key_blocks_of[qh]
                scores = [s_ref[hh, qh, c] for c in key_blocks]
                m_new = functools.reduce(jnp.maximum, [jnp.max(s, axis=0, keepdims=True) for s in scores])
                if not diagonal:
                    m_old = m_ref[hh, qh]
                    m_new = jnp.maximum(m_old, m_new)
                pv = None
                for c, s in zip(key_blocks, scores):
                    term = _dot(vt_ref[hh, c], jnp.exp(s - m_new).astype(BF16))
                    pv = term if pv is None else pv + term
                acc_ref[hh, qh] = pv if diagonal else jnp.exp(m_old - m_new) * acc_ref[hh, qh] + pv
                m_ref[hh, qh] = m_new

    @pl.when(diagonal_step)
    def _():
        attend(True)

    @pl.when(jnp.logical_not(diagonal_step))
    def _():
        attend(False)

    @pl.when(last)
    def _():
        for qh in range(MOBA_PAIR):
            outs = []
            for hh in range(MOBA_HEADS):
                a = acc_ref[hh, qh].T
                outs.append(a[:, 0:HD] / a[:, HD:HD + 1])
            o_ref[qh * B:(qh + 1) * B, :] = jnp.concatenate(outs, axis=1).astype(BF16)


def _moba(z, seq):
    T = z.shape[0]
    nb = seq // MOBA_BLOCK
    assert nb % MOBA_PAIR == 0
    npair = nb // MOBA_PAIR
    batch = T // seq
    qt_aug, k_aug, vt_aug = _moba_prep(z, _moba_kmean(z), seq)
    qidx = np.concatenate([np.full(t + 1, t) for t in range(npair)]).astype(np.int32)
    kidx = np.concatenate([np.concatenate([[t], np.arange(t)]) for t in range(npair)]).astype(np.int32)
    tq = MOBA_PAIR * MOBA_BLOCK
    H = MOBA_HEADS
    grid_spec = pltpu.PrefetchScalarGridSpec(
        num_scalar_prefetch=2,
        grid=(batch, len(qidx)),
        in_specs=[pl.BlockSpec((H, MOBA_PAIR, LANES, MOBA_BLOCK),
                               lambda b, s, qi, ki: (0, b * npair + qi[s], 0, 0)),
                  pl.BlockSpec((H, tq, LANES), lambda b, s, qi, ki: (0, b * npair + ki[s], 0)),
                  pl.BlockSpec((H, MOBA_PAIR, LANES, MOBA_BLOCK),
                               lambda b, s, qi, ki: (0, b * npair + ki[s], 0, 0))],
        out_specs=pl.BlockSpec((tq, 256), lambda b, s, qi, ki: (b * npair + qi[s], 0)),
        scratch_shapes=[pltpu.VMEM((H, MOBA_PAIR, 1, MOBA_BLOCK), F32),
                        pltpu.VMEM((H, MOBA_PAIR, LANES, MOBA_BLOCK), F32),
                        pltpu.VMEM((H, MOBA_PAIR, MOBA_PAIR, MOBA_BLOCK, MOBA_BLOCK), F32)],
    )
    return pl.pallas_call(
        _moba_kernel,
        grid_spec=grid_spec,
        out_shape=jax.ShapeDtypeStruct((T, 256), BF16),
        compiler_params=_params("parallel", "arbitrary"),
        name="moba_mixer",
    )(jnp.asarray(qidx), jnp.asarray(kidx), qt_aug, k_aug, vt_aug)


def _merge_kernel(h_ref, o0, o1, o2, o3, gates_ref, wb_ref, wout_ref, out_ref):
    merged = None
    for i, o in enumerate((o0, o1, o2, o3)):
        gate = jax.nn.sigmoid(gates_ref[:, i * D_MODEL:(i + 1) * D_MODEL].astype(F32))
        term = gate * _dot(o[...], wb_ref[i])
        merged = term if merged is None else merged + term
    out_ref[...] = h_ref[...] + _dot(merged.astype(BF16), wout_ref[...])


def _merge(h, branches, gates, wb, wout):
    T = h.shape[0]
    tm = TOK_TILE
    return pl.pallas_call(
        _merge_kernel,
        grid=(T // tm,),
        in_specs=[pl.BlockSpec((tm, D_MODEL), lambda i: (i, 0))]
        + [pl.BlockSpec((tm, 256), lambda i: (i, 0))] * 4
        + [pl.BlockSpec((tm, GATE_W), lambda i: (i, 0)),
           pl.BlockSpec((N_BRANCH, 256, D_MODEL), lambda i: (0, 0, 0), pipeline_mode=pl.Buffered(1)),
           pl.BlockSpec((D_MODEL, D_MODEL), lambda i: (0, 0), pipeline_mode=pl.Buffered(1))],
        out_specs=pl.BlockSpec((tm, D_MODEL), lambda i: (i, 0)),
        out_shape=jax.ShapeDtypeStruct((T, D_MODEL), F32),
        compiler_params=_params("parallel"),
        name="merge_out_proj",
    )(h, *branches, gates, wb, wout)


ROW_TILES = D_MODEL // LANES


def _row_tile_spec(rows, index_map):
    return pl.BlockSpec((rows * ROW_TILES, LANES), index_map)


def _row_slab(ref, j):
    return ref[pl.ds(j, ref.shape[0] // ROW_TILES, stride=ROW_TILES), :]


def _store_row_tiles(ref, x):
    for j in range(ROW_TILES):
        ref[pl.ds(j, x.shape[0], stride=ROW_TILES), :] = x[:, j * LANES:(j + 1) * LANES]


def _load_row_tiles(ref):
    return jnp.concatenate([_row_slab(ref, j) for j in range(ROW_TILES)], axis=1)


def _lane_first(mask, lane):
    return jnp.min(jnp.where(mask, lane, float(LANES)), axis=-1, keepdims=True)


def _router_kernel(h_ref, g_ref, wr_ref, br_ref, u_ref, route_ref, route_t_ref, count_ref, carry_ref):
    tm = h_ref.shape[0]

    @pl.when(pl.program_id(0) == 0)
    def _():
        carry_ref[...] = jnp.zeros_like(carry_ref)

    u = _rms(h_ref[...], g_ref[...])
    _store_row_tiles(u_ref, u)
    u_hi, u_lo = _split_bf16(u, 2)
    w_hi, w_lo = _split_bf16(wr_ref[...], 2)
    logits = _dot(u_hi, w_hi) + (_dot(u_lo, w_hi) + _dot(u_hi, w_lo)) + br_ref[...]
    lane_i = _iota((tm, LANES), 1)
    lane = lane_i.astype(F32)
    lane_group = ((lane_i - EXPERT_LANE0) >> 3).astype(F32)

    grp = jnp.where(lane_i < N_GROUPS, logits, BELOW_NEG_INF)
    grp_e = jnp.exp(grp - jnp.max(grp, axis=-1, keepdims=True))
    p_group = grp_e / jnp.sum(grp_e, axis=-1, keepdims=True)
    g_w = jnp.max(p_group, axis=-1, keepdims=True)
    g_sel = _lane_first(p_group == g_w, lane)

    in_group = (lane_i >= EXPERT_LANE0) & (lane_group == g_sel)
    el = jnp.where(in_group, logits, BELOW_NEG_INF)
    top1 = jnp.max(el, axis=-1, keepdims=True)
    lane1 = _lane_first(el == top1, lane)
    el = jnp.where(lane == lane1, BELOW_NEG_INF, el)
    top2 = jnp.max(el, axis=-1, keepdims=True)
    lane2 = _lane_first(el == top2, lane)
    e2 = jnp.exp(top2 - top1)
    w1 = g_w / (1.0 + e2)
    w2 = g_w * e2 / (1.0 + e2)

    onehot = ((lane == lane1) | (lane == lane2)).astype(BF16)
    before = (_iota((tm, tm), 1) < _iota((tm, tm), 0)).astype(BF16)
    seen = _dot(before, onehot) + carry_ref[...]
    rank1 = jnp.sum(jnp.where(lane == lane1, seen, 0.0), axis=-1, keepdims=True)
    rank2 = jnp.sum(jnp.where(lane == lane2, seen, 0.0), axis=-1, keepdims=True)
    carry_ref[...] += jnp.sum(onehot.astype(F32), axis=0, keepdims=True)
    count_ref[...] = carry_ref[...]

    out = jnp.zeros((tm, LANES), F32)
    for idx, val in enumerate((lane1 - EXPERT_LANE0, lane2 - EXPERT_LANE0, w1, w2, rank1, rank2)):
        out = jnp.where(lane_i == idx, val, out)
    route_ref[...] = out
    route_t_ref[...] = out.T[0:8, :]


def _router(h, g, wr, br):
    T = h.shape[0]
    tm = TOK_TILE
    return pl.pallas_call(
        _router_kernel,
        grid=(T // tm,),
        in_specs=[pl.BlockSpec((tm, D_MODEL), lambda i: (i, 0)), _full((1, D_MODEL)),
                  _full((D_MODEL, LANES)), _full((1, LANES))],
        out_specs=[_row_tile_spec(tm, lambda i: (i, 0)), pl.BlockSpec((tm, LANES), lambda i: (i, 0)),
                   pl.BlockSpec((8, tm), lambda i: (0, i)), _full((1, LANES))],
        out_shape=[jax.ShapeDtypeStruct((T * ROW_TILES, LANES), F32), jax.ShapeDtypeStruct((T, LANES), F32),
                   jax.ShapeDtypeStruct((8, T), F32), jax.ShapeDtypeStruct((1, LANES), F32)],
        scratch_shapes=[pltpu.VMEM((1, LANES), F32)],
        compiler_params=_params("arbitrary"),
        name="moe_router",
    )(h, g, wr, br)


ROW_UNROLL = 8


def _row_copy(src_ref, src_row, dst_ref, dst_row, sem):
    src = src_ref.at[pl.ds(pl.multiple_of(src_row * ROW_TILES, ROW_TILES), ROW_TILES)]
    dst = dst_ref.at[pl.ds(pl.multiple_of(dst_row * ROW_TILES, ROW_TILES), ROW_TILES)]
    return pltpu.make_async_copy(src, dst, sem)


def _for_each_row_pair(rows, body):
    def group(g, carry):
        for i in range(ROW_UNROLL):
            for k in range(EXPERT_TOPK):
                body(g * ROW_UNROLL + i, k)
        return carry

    lax.fori_loop(0, rows // ROW_UNROLL, group, 0)


def _dispatch_kernel(dest0_ref, dest1_ref, pstart_ref, pend_ref, u_ref, xs_ref, zero_ref, sem):
    tm = u_ref.shape[0] // ROW_TILES
    tile = pl.program_id(0)
    dests = (dest0_ref, dest1_ref)

    @pl.when(tile == 0)
    def _():
        zero_ref[...] = jnp.zeros_like(zero_ref)

        def zero_block(first_row):
            start = pl.multiple_of(first_row * ROW_TILES, ROW_TILES)
            fill = pltpu.make_async_copy(zero_ref, xs_ref.at[pl.ds(start, MOE_BLOCK * ROW_TILES)], sem)
            fill.start()
            fill.wait()

        def zero_tail(e, carry):
            @pl.when(pend_ref[e] > pstart_ref[e])
            def _():
                zero_block(pend_ref[e] - MOE_BLOCK)
            return carry

        def zero_unused(b, carry):
            zero_block(b * MOE_BLOCK)
            return carry

        lax.fori_loop(0, N_EXPERTS, zero_tail, 0)
        lax.fori_loop(pend_ref[N_EXPERTS - 1] // MOE_BLOCK, xs_ref.shape[0] // (MOE_BLOCK * ROW_TILES),
                      zero_unused, 0)

    def issue(r, k):
        _row_copy(u_ref, r, xs_ref, dests[k][tile * tm + r], sem).start(priority=k)

    _for_each_row_pair(tm, issue)
    for _ in range(EXPERT_TOPK):
        pltpu.make_async_copy(u_ref, xs_ref.at[pl.ds(0, tm * ROW_TILES)], sem).wait()


def _dispatch(dest0, dest1, pstart, pend, u, n_slots):
    T = u.shape[0] // ROW_TILES
    tm = ROW_TILE
    grid_spec = pltpu.PrefetchScalarGridSpec(
        num_scalar_prefetch=4,
        grid=(T // tm,),
        in_specs=[_row_tile_spec(tm, lambda i, *_: (i, 0))],
        out_specs=pl.BlockSpec(memory_space=pl.ANY),
        scratch_shapes=[pltpu.VMEM((MOE_BLOCK * ROW_TILES, LANES), F32), pltpu.SemaphoreType.DMA(())],
    )
    return pl.pallas_call(
        _dispatch_kernel,
        grid_spec=grid_spec,
        out_shape=jax.ShapeDtypeStruct((n_slots * ROW_TILES, LANES), F32),
        compiler_params=_params("arbitrary"),
        name="moe_dispatch",
    )(dest0, dest1, pstart, pend, u)


def _expert_kernel(blk_e_ref, n_used_ref, x_ref, w1_ref, w3_ref, w2_ref, y_ref, w1b_ref, w3b_ref, w2b_ref):
    i = pl.program_id(0)
    used = i < n_used_ref[0]

    @pl.when((i == 0) | (blk_e_ref[i] != blk_e_ref[jnp.maximum(i - 1, 0)]))
    def _():
        w1b_ref[...] = w1_ref[0, 0].astype(BF16)
        w3b_ref[...] = w3_ref[0, 0].astype(BF16)
        w2b_ref[...] = w2_ref[0, 0].astype(BF16)

    @pl.when(used)
    def _():
        x = _load_row_tiles(x_ref).astype(BF16)
        a = _dot(x, w1b_ref[...])
        hidden = (a * jax.nn.sigmoid(a)) * _dot(x, w3b_ref[...])
        _store_row_tiles(y_ref, _dot(hidden.astype(BF16), w2b_ref[...]))

    @pl.when(jnp.logical_not(used))
    def _():
        y_ref[...] = jnp.zeros_like(y_ref)


def _experts(blk_e, n_used, xs, w1, w3, w2, layer):
    n_slots = xs.shape[0] // ROW_TILES
    bm = MOE_BLOCK
    grid_spec = pltpu.PrefetchScalarGridSpec(
        num_scalar_prefetch=2,
        grid=(n_slots // bm,),
        in_specs=[_row_tile_spec(bm, lambda i, be, nu: (jnp.minimum(i, nu[0] - 1), 0)),
                  pl.BlockSpec((1, 1, D_MODEL, EXPERT_HIDDEN), lambda i, be, nu: (layer, be[i], 0, 0)),
                  pl.BlockSpec((1, 1, D_MODEL, EXPERT_HIDDEN), lambda i, be, nu: (layer, be[i], 0, 0)),
                  pl.BlockSpec((1, 1, EXPERT_HIDDEN, D_MODEL), lambda i, be, nu: (layer, be[i], 0, 0))],
        out_specs=_row_tile_spec(bm, lambda i, be, nu: (i, 0)),
        scratch_shapes=[pltpu.VMEM((D_MODEL, EXPERT_HIDDEN), BF16), pltpu.VMEM((D_MODEL, EXPERT_HIDDEN), BF16),
                        pltpu.VMEM((EXPERT_HIDDEN, D_MODEL), BF16)],
    )
    return pl.pallas_call(
        _expert_kernel,
        grid_spec=grid_spec,
        out_shape=jax.ShapeDtypeStruct((n_slots * ROW_TILES, LANES), F32),
        compiler_params=_params("arbitrary"),
        name="moe_experts",
    )(blk_e, n_used, xs, w1, w3, w2)


def _combine_kernel(dest0_ref, dest1_ref, h_ref, route_ref, ys_ref, out_ref, y_ref, sem):
    tm = h_ref.shape[0]
    step = pl.program_id(0)
    slot = step % 2

    dests = (dest0_ref, dest1_ref)

    def gather_tile(tile, buf):
        def issue(r, k):
            _row_copy(ys_ref, dests[k][tile * tm + r], y_ref.at[buf, k], r, sem.at[buf]).start(priority=k)

        _for_each_row_pair(tm, issue)

    @pl.when(step == 0)
    def _():
        gather_tile(0, 0)

    @pl.when(step + 1 < pl.num_programs(0))
    def _():
        gather_tile(step + 1, 1 - slot)

    for k in range(EXPERT_TOPK):
        pltpu.make_async_copy(ys_ref.at[pl.ds(0, tm * ROW_TILES)], y_ref.at[slot, k], sem.at[slot]).wait()
    w0 = route_ref[:, 2:3]
    w1 = route_ref[:, 3:4]
    for j in range(ROW_TILES):
        cols = slice(j * LANES, (j + 1) * LANES)
        out_ref[:, cols] = (h_ref[:, cols] + w0 * _row_slab(y_ref.at[slot, 0], j)
                            + w1 * _row_slab(y_ref.at[slot, 1], j))


def _combine(dest0, dest1, h, route, ys):
    T = h.shape[0]
    tm = ROW_TILE
    grid_spec = pltpu.PrefetchScalarGridSpec(
        num_scalar_prefetch=2,
        grid=(T // tm,),
        in_specs=[pl.BlockSpec((tm, D_MODEL), lambda i, *_: (i, 0)),
                  pl.BlockSpec((tm, LANES), lambda i, *_: (i, 0)),
                  pl.BlockSpec(memory_space=pl.ANY)],
        out_specs=pl.BlockSpec((tm, D_MODEL), lambda i, *_: (i, 0)),
        scratch_shapes=[pltpu.VMEM((2, EXPERT_TOPK, tm * ROW_TILES, LANES), F32),
                        pltpu.SemaphoreType.DMA((2,))],
    )
    return pl.pallas_call(
        _combine_kernel,
        grid_spec=grid_spec,
        out_shape=jax.ShapeDtypeStruct((T, D_MODEL), F32),
        compiler_params=_params("arbitrary"),
        name="moe_combine",
    )(dest0, dest1, h, route, ys)


def _moe(h, g, w_group, b_group, w_expert, b_expert, w1, w3, w2, layer):
    T = h.shape[0]
    wr = jnp.zeros((D_MODEL, LANES), F32)
    wr = wr.at[:, :N_GROUPS].set(w_group).at[:, EXPERT_LANE0:EXPERT_LANE0 + N_EXPERTS].set(w_expert)
    br = jnp.zeros((1, LANES), F32)
    br = br.at[0, :N_GROUPS].set(b_group).at[0, EXPERT_LANE0:EXPERT_LANE0 + N_EXPERTS].set(b_expert)
    u, route, route_t, count = _router(h, g, wr, br)

    n_blocks = (T * EXPERT_TOPK) // MOE_BLOCK + N_EXPERTS
    counts = count[0, EXPERT_LANE0:EXPERT_LANE0 + N_EXPERTS].astype(I32)
    padded = ((counts + MOE_BLOCK - 1) // MOE_BLOCK) * MOE_BLOCK
    pend = jnp.cumsum(padded)
    pstart = pend - padded
    expert_ids = jnp.arange(N_EXPERTS, dtype=F32)[:, None]

    def slot_of(expert_row, rank_row):
        segment_start = jnp.sum(jnp.where(expert_row[None, :] == expert_ids, pstart[:, None], 0), axis=0)
        return segment_start + rank_row.astype(I32)

    dest0 = slot_of(route_t[0], route_t[4])
    dest1 = slot_of(route_t[1], route_t[5])
    block_row0 = jnp.arange(n_blocks, dtype=I32) * MOE_BLOCK
    blk_e = jnp.minimum(jnp.sum((pend[None, :] <= block_row0[:, None]).astype(I32), axis=1), N_EXPERTS - 1)
    n_used = (pend[-1:] // MOE_BLOCK).astype(I32)

    xs = _dispatch(dest0, dest1, pstart, pend, u, n_blocks * MOE_BLOCK)
    ys = _experts(blk_e, n_used, xs, w1, w3, w2, layer)
    return _combine(dest0, dest1, h, route, ys)


def _final_norm_kernel(h_ref, g_ref, o_ref):
    o_ref[...] = _rms(h_ref[...], g_ref[...])


def _final_norm(h, g):
    T = h.shape[0]
    tm = TOK_TILE
    return pl.pallas_call(
        _final_norm_kernel,
        grid=(T // tm,),
        in_specs=[pl.BlockSpec((tm, D_MODEL), lambda i: (i, 0)), _full((1, D_MODEL))],
        out_specs=pl.BlockSpec((tm, D_MODEL), lambda i: (i, 0)),
        out_shape=jax.ShapeDtypeStruct((T, D_MODEL), F32),
        compiler_params=_params("parallel"),
        name="final_norm",
    )(h, g)


def _split_w_in(w_in):
    gla_n = GLA_HEADS * (2 * GLA_DK + 2 * GLA_DV) + GLA_LOWRANK
    swa_n = (SWA_HEADS + 2 * SWA_KV_HEADS) * SWA_HD
    moba_n = 3 * MOBA_HEADS * MOBA_HD
    ret_n = RET_HEADS * (2 * RET_DK + 2 * RET_DV)
    offs = np.cumsum([0, gla_n, swa_n, moba_n, ret_n, GATE_W])
    parts = [w_in[:, offs[i]:offs[i + 1]].astype(BF16) for i in range(5)]
    parts[0] = jnp.pad(parts[0], ((0, 0), (0, GLA_W - gla_n)))
    return parts


def _token_mixer(h, seq, ln1_g, w_in, gla_w_a2, gla_b_a, gla_norm_g, swa_sinks, ret_norm_g, w_branch, w_out,
                 rope):
    z_gla, z_swa, z_moba, z_ret, z_gates = _in_proj(h, ln1_g.reshape(1, D_MODEL), _split_w_in(w_in))
    wa = jnp.zeros((128, 128), F32).at[:GLA_LOWRANK].set(gla_w_a2)
    o_gla = _gla(z_gla, wa, gla_b_a.reshape(1, 128), jnp.tile(gla_norm_g, GLA_HEADS).reshape(1, 256), seq)
    o_swa = _swa(z_swa, swa_sinks, seq)
    o_moba = _moba(z_moba, seq)
    o_ret = _retention(z_ret, *rope, jnp.tile(ret_norm_g, RET_HEADS).reshape(1, 256), seq)
    return _merge(h, (o_gla, o_swa, o_moba, o_ret), z_gates, w_branch.astype(BF16), w_out.astype(BF16))


def kernel(x, ln1_g, w_in, gla_w_a2, gla_b_a, gla_norm_g, swa_sinks, ret_norm_g, w_branch, w_out, ln2_g,
           w_group, b_group, w_expert, b_expert, w1, w3, w2, final_g):
    batch, seq, _ = x.shape
    depth = w_in.shape[0]
    rope = _rope_tables(seq)
    h = x.reshape(batch * seq, D_MODEL)
    for l in range(depth):
        h = _token_mixer(h, seq, ln1_g[l], w_in[l], gla_w_a2[l], gla_b_a[l], gla_norm_g[l], swa_sinks[l],
                         ret_norm_g[l], w_branch[l], w_out[l], rope)
        h = _moe(h, ln2_g[l].reshape(1, D_MODEL), w_group[l], b_group[l], w_expert[l], b_expert[l],
                 w1, w3, w2, l)
    return _final_norm(h, final_g.reshape(1, D_MODEL)).reshape(batch, seq, D_MODEL)
```

```python
import functools
import math

import numpy as np
import jax
import jax.numpy as jnp
from jax import lax
from jax.experimental import pallas as pl
from jax.experimental.pallas import tpu as pltpu

F32 = jnp.float32
BF16 = jnp.bfloat16
I32 = jnp.int32
HIGHEST = lax.Precision.HIGHEST

D_MODEL = 1024
N_BRANCH = 4
NORM_EPS = 1e-6
NEG_INF = -1e30
BELOW_NEG_INF = -3e38

GLA_HEADS, GLA_DK, GLA_DV, GLA_LOWRANK, GLA_TAU, GLA_CHUNK = 4, 32, 64, 16, 16.0, 64
SWA_HEADS, SWA_KV_HEADS, SWA_HD, SWA_WINDOW = 4, 2, 64, 128
MOBA_HEADS, MOBA_HD, MOBA_BLOCK, MOBA_TOPK = 4, 64, 256, 3
RET_HEADS, RET_DK, RET_DV, RET_CHUNK, RET_ROPE_BASE = 4, 32, 64, 128, 10000.0
N_GROUPS, EXPERTS_PER_GROUP, EXPERT_TOPK, EXPERT_HIDDEN = 4, 8, 2, 256
MOE_BLOCK = 512
N_EXPERTS = N_GROUPS * EXPERTS_PER_GROUP

GLA_W = 896
SWA_W = 512
MOBA_W = 768
RET_W = 768
GATE_W = N_BRANCH * D_MODEL

LANES = 128
EXPERT_LANE0 = 32

VMEM_LIMIT = 56 * 1024 * 1024

TOK_TILE = 512
ROW_TILE = 256


def _params(*sem):
    return pltpu.CompilerParams(dimension_semantics=sem, vmem_limit_bytes=VMEM_LIMIT)


def _dot(a, b, precision=None):
    return jnp.dot(a, b, preferred_element_type=F32, precision=precision)


def _dot_nt(a, b, precision=None):
    return lax.dot_general(a, b, (((1,), (1,)), ((), ())), preferred_element_type=F32, precision=precision)


def _dot_tn(a, b, precision=None):
    return lax.dot_general(a, b, (((0,), (0,)), ((), ())), preferred_element_type=F32, precision=precision)


def _split_bf16(x, parts):
    out = []
    for _ in range(parts - 1):
        piece = x.astype(BF16)
        out.append(piece)
        x = x - piece.astype(F32)
    out.append(x.astype(BF16))
    return out


def _dot_f32_by_bf16(x, m):
    return sum(_dot(piece, m) for piece in _split_bf16(x, 3))


def _dot_bf16_by_f32(m, x):
    return sum(_dot(m, piece) for piece in _split_bf16(x, 3))


def _iota(shape, dim):
    return lax.broadcasted_iota(I32, shape, dim)


def _rms(x, g):
    ms = jnp.mean(x * x, axis=-1, keepdims=True)
    return x * lax.rsqrt(ms + NORM_EPS) * g


def _full(shape):
    return pl.BlockSpec(shape, lambda *_: (0,) * len(shape))


def _in_proj_kernel(h_ref, g_ref, w0, w1, w2, w3, w4, o0, o1, o2, o3, o4):
    u = _rms(h_ref[...], g_ref[...]).astype(BF16)
    for w, o in ((w0, o0), (w1, o1), (w2, o2), (w3, o3), (w4, o4)):
        o[...] = _dot(u, w[...]).astype(BF16)


def _in_proj(h, g, ws):
    T = h.shape[0]
    tm = TOK_TILE
    widths = [w.shape[1] for w in ws]
    return pl.pallas_call(
        _in_proj_kernel,
        grid=(T // tm,),
        in_specs=[pl.BlockSpec((tm, D_MODEL), lambda i: (i, 0)), _full((1, D_MODEL))]
        + [pl.BlockSpec((D_MODEL, n), lambda i: (0, 0), pipeline_mode=pl.Buffered(1)) for n in widths],
        out_specs=[pl.BlockSpec((tm, n), lambda i: (i, 0)) for n in widths],
        out_shape=[jax.ShapeDtypeStruct((T, n), BF16) for n in widths],
        compiler_params=_params("parallel"),
        name="in_proj",
    )(h, g, *ws)


def _head_select(x, lane_head, n_heads, rows):
    out = None
    for hh in range(n_heads):
        term = jnp.where(lane_head == hh, x[hh * rows:(hh + 1) * rows], 0.0)
        out = term if out is None else out + term
    return out


def _head_norm_gate(o, gate_in, g):
    n = o.shape[1]
    same_head = (_iota((n, n), 0) >> 6) == (_iota((n, n), 1) >> 6)
    ms = _dot_f32_by_bf16(o * o, same_head.astype(BF16)) * (1.0 / 64.0)
    return o * lax.rsqrt(ms + NORM_EPS) * g * (gate_in * jax.nn.sigmoid(gate_in))


def _gla_kernel(z_ref, wa_ref, ba_ref, g_ref, o_ref, st_ref, oacc_ref, *, tiles_per_seq):
    C = GLA_CHUNK
    tt = z_ref.shape[0]

    @pl.when(pl.program_id(0) % tiles_per_seq == 0)
    def _():
        st_ref[...] = jnp.zeros_like(st_ref)

    q = z_ref[:, 0:128].astype(F32)
    k = z_ref[:, 128:256].astype(F32)
    r = z_ref[:, 512:768].astype(F32)
    pre = _dot_bf16_by_f32(z_ref[:, 768:896], wa_ref[...]) + ba_ref[...]
    log_a = (jnp.minimum(pre, 0.0) - jnp.log(1.0 + jnp.exp(-jnp.abs(pre)))) * (1.0 / GLA_TAU)

    lower = (_iota((C, C), 1) <= _iota((C, C), 0)).astype(BF16)
    b = jnp.concatenate([_dot_bf16_by_f32(lower, log_a[c * C:(c + 1) * C]) for c in range(tt // C)], axis=0)
    b_last = jnp.broadcast_to(b.reshape(tt // C, C, 128)[:, C - 1:C, :], (tt // C, C, 128)).reshape(tt, 128)
    q_dec = (q * (GLA_DK ** -0.5) * jnp.exp(b)).astype(BF16)
    k_dec = (k * jnp.exp(-b)).astype(BF16)
    k_end = (k * jnp.exp(b_last - b)).astype(BF16)
    decay = jnp.exp(b_last)

    qk_head = _iota((C, 128), 1) >> 5
    v_head = _iota((C, 256), 1) >> 6
    causal = _iota((4 * C, C), 1) <= (_iota((4 * C, C), 0) & (C - 1))
    state_mask = (_iota((256, 128), 0) >> 6) == (_iota((256, 128), 1) >> 5)

    for c in range(tt // C):
        sl = slice(c * C, (c + 1) * C)
        qd = q_dec[sl]
        v = z_ref[sl, 256:512]
        q_stack = jnp.concatenate([jnp.where(qk_head == hh, qd, 0) for hh in range(GLA_HEADS)], axis=0)
        att = jnp.where(causal, _dot_nt(q_stack, k_dec[sl]), 0.0)
        intra = _head_select(_dot(att.astype(BF16), v), v_head, GLA_HEADS, C)
        state = st_ref[...]
        inter = _dot_nt(qd, state.astype(BF16))
        oacc_ref[sl, :] = intra + inter
        kv = _dot_tn(v, k_end[sl])
        st_ref[...] = state * decay[c * C:c * C + 1] + jnp.where(state_mask, kv, 0.0)

    o_ref[...] = _head_norm_gate(oacc_ref[...], r, g_ref[...]).astype(BF16)


def _gla(z, wa, ba, g, seq):
    T = z.shape[0]
    tt = TOK_TILE
    return pl.pallas_call(
        functools.partial(_gla_kernel, tiles_per_seq=seq // tt),
        grid=(T // tt,),
        in_specs=[pl.BlockSpec((tt, GLA_W), lambda i: (i, 0)), _full((128, 128)), _full((1, 128)),
                  _full((1, 256))],
        out_specs=pl.BlockSpec((tt, 256), lambda i: (i, 0)),
        out_shape=jax.ShapeDtypeStruct((T, 256), BF16),
        scratch_shapes=[pltpu.VMEM((256, 128), F32), pltpu.VMEM((tt, 256), F32)],
        compiler_params=_params("arbitrary"),
        name="gla_mixer",
    )(z, wa, ba, g)


_RET_LOG_GAMMA = [math.log1p(-(2.0 ** (-5.0 - hh))) for hh in range(RET_HEADS)]


def _by_head(head, values):
    out = jnp.full(head.shape, values[-1], F32)
    for hh in range(len(values) - 2, -1, -1):
        out = jnp.where(head == hh, values[hh], out)
    return out


def _ret_kernel(z_ref, cos_ref, sina_ref, sinb_ref, g_ref, o_ref, st_ref, oacc_ref, *, tiles_per_seq):
    C = RET_CHUNK
    tt = z_ref.shape[0]

    @pl.when(pl.program_id(0) % tiles_per_seq == 0)
    def _():
        st_ref[...] = jnp.zeros_like(st_ref)

    cos, sina, sinb = cos_ref[...], sina_ref[...], sinb_ref[...]

    def rotate(t):
        return t * cos + pltpu.roll(t, 112, 1) * sina + pltpu.roll(t, 16, 1) * sinb

    q_rot = rotate(z_ref[:, 0:128].astype(F32))
    k_rot = rotate(z_ref[:, 128:256].astype(F32)) * (RET_DK ** -0.5)
    gate_in = z_ref[:, 512:768].astype(F32)

    qk_head = _iota((C, 128), 1) >> 5
    v_head = _iota((C, 256), 1) >> 6
    lg_qk = _by_head(_iota((1, 128), 1) >> 5, _RET_LOG_GAMMA)
    lg_v = _by_head(_iota((1, 256), 1) >> 6, _RET_LOG_GAMMA)
    pos_qk = _iota((C, 128), 0).astype(F32)
    pos_v = _iota((C, 256), 0).astype(F32)
    key_decay = jnp.exp((C - 1.0 - pos_qk) * lg_qk)
    query_decay = jnp.exp((pos_v + 1.0) * lg_v)
    chunk_decay = jnp.exp(float(C) * lg_qk)
    srow = _iota((4 * C, C), 0)
    rel = ((srow & (C - 1)) - _iota((4 * C, C), 1)).astype(F32)
    decay_mat = jnp.where(rel >= 0, jnp.exp(jnp.maximum(rel, 0.0) * _by_head(srow >> 7, _RET_LOG_GAMMA)), 0.0)
    state_mask = (_iota((256, 128), 0) >> 6) == (_iota((256, 128), 1) >> 5)

    q_bf = q_rot.astype(BF16)
    k_bf = k_rot.astype(BF16)
    for c in range(tt // C):
        sl = slice(c * C, (c + 1) * C)
        qc = q_bf[sl]
        v = z_ref[sl, 256:512]
        q_stack = jnp.concatenate([jnp.where(qk_head == hh, qc, 0) for hh in range(RET_HEADS)], axis=0)
        att = _dot_nt(q_stack, k_bf[sl]) * decay_mat
        intra = _head_select(_dot(att.astype(BF16), v), v_head, RET_HEADS, C)
        state = st_ref[...]
        inter = _dot_nt(qc, state.astype(BF16)) * query_decay
        oacc_ref[sl, :] = intra + inter
        kv = _dot_tn(v, (k_rot[sl] * key_decay).astype(BF16))
        st_ref[...] = state * chunk_decay + jnp.where(state_mask, kv, 0.0)

    o_ref[...] = _head_norm_gate(oacc_ref[...], gate_in, g_ref[...]).astype(BF16)


def _retention(z, cos, sina, sinb, g, seq):
    T = z.shape[0]
    tt = TOK_TILE
    tps = seq // tt
    pos_spec = pl.BlockSpec((tt, 128), lambda i: (i % tps, 0))
    return pl.pallas_call(
        functools.partial(_ret_kernel, tiles_per_seq=tps),
        grid=(T // tt,),
        in_specs=[pl.BlockSpec((tt, RET_W), lambda i: (i, 0)), pos_spec, pos_spec, pos_spec, _full((1, 256))],
        out_specs=pl.BlockSpec((tt, 256), lambda i: (i, 0)),
        out_shape=jax.ShapeDtypeStruct((T, 256), BF16),
        scratch_shapes=[pltpu.VMEM((256, 128), F32), pltpu.VMEM((tt, 256), F32)],
        compiler_params=_params("arbitrary"),
        name="retention_mixer",
    )(z, cos, sina, sinb, g)


def _rope_tables(seq):
    half = RET_DK // 2
    pos = jnp.arange(seq, dtype=F32)
    inv_freq = RET_ROPE_BASE ** (-jnp.arange(half, dtype=F32) * 2.0 / RET_DK)
    ang = pos[:, None] * inv_freq[None, :]
    cos = jnp.tile(jnp.cos(ang), (1, LANES // half))
    sin = jnp.tile(jnp.sin(ang), (1, LANES // half))
    first_half = (jnp.arange(LANES) % RET_DK) < half
    return cos, jnp.where(first_half, -sin, 0.0), jnp.where(first_half, 0.0, sin)


def _swa_kernel(sink_ref, cur_ref, prev_ref, o_ref, *, tiles_per_seq):
    W = SWA_WINDOW
    HD = SWA_HD
    tt = cur_ref.shape[0]
    first_key = jnp.where(pl.program_id(0) % tiles_per_seq == 0, W, 0)
    qi = _iota((W, 2 * W), 0)
    sj = _iota((W, 2 * W), 1)
    in_window = (sj > qi) & (sj <= qi + W)
    kv0 = SWA_HEADS * HD
    for c in range(tt // W):
        rows = slice(c * W, (c + 1) * W)
        if c == 0:
            k_prev, v_prev = prev_ref[:, kv0:kv0 + 128], prev_ref[:, kv0 + 128:kv0 + 256]
            mask = in_window & (sj >= first_key)
        else:
            prows = slice((c - 1) * W, c * W)
            k_prev, v_prev = cur_ref[prows, kv0:kv0 + 128], cur_ref[prows, kv0 + 128:kv0 + 256]
            mask = in_window
        k_band = jnp.concatenate([k_prev, cur_ref[rows, kv0:kv0 + 128]], axis=0)
        v_band = jnp.concatenate([v_prev, cur_ref[rows, kv0 + 128:kv0 + 256]], axis=0)
        outs = []
        for hh in range(SWA_HEADS):
            kk = hh // (SWA_HEADS // SWA_KV_HEADS)
            s = _dot_nt(cur_ref[rows, hh * HD:(hh + 1) * HD], k_band[:, kk * HD:(kk + 1) * HD]) * (HD ** -0.5)
            s = jnp.where(mask, s, NEG_INF)
            sink = sink_ref[hh]
            m = jnp.maximum(jnp.max(s, axis=-1, keepdims=True), sink)
            p = jnp.exp(s - m)
            denom = jnp.sum(p, axis=-1, keepdims=True) + jnp.exp(sink - m)
            outs.append(_dot(p.astype(BF16), v_band[:, kk * HD:(kk + 1) * HD]) / denom)
        o_ref[rows, :] = jnp.concatenate(outs, axis=1).astype(BF16)


def _swa(z, sinks, seq):
    T = z.shape[0]
    tt = TOK_TILE
    per = tt // SWA_WINDOW
    return pl.pallas_call(
        functools.partial(_swa_kernel, tiles_per_seq=seq // tt),
        grid=(T // tt,),
        in_specs=[pl.BlockSpec(memory_space=pltpu.SMEM),
                  pl.BlockSpec((tt, SWA_W), lambda i: (i, 0)),
                  pl.BlockSpec((SWA_WINDOW, SWA_W), lambda i: (jnp.maximum(i * per - 1, 0), 0))],
        out_specs=pl.BlockSpec((tt, 256), lambda i: (i, 0)),
        out_shape=jax.ShapeDtypeStruct((T, 256), BF16),
        compiler_params=_params("parallel"),
        name="swa_mixer",
    )(sinks, z, z)


_KMEAN_BLOCKS = 8


def _kmean_kernel(k_ref, o_ref):
    k = k_ref[...].astype(F32).reshape(_KMEAN_BLOCKS, MOBA_BLOCK, 256)
    o_ref[...] = jnp.mean(k, axis=1)


def _moba_kmean(z):
    T = z.shape[0]
    rows = _KMEAN_BLOCKS * MOBA_BLOCK
    return pl.pallas_call(
        _kmean_kernel,
        grid=(T // rows,),
        in_specs=[pl.BlockSpec((rows, 256), lambda i: (i, 1))],
        out_specs=pl.BlockSpec((_KMEAN_BLOCKS, 256), lambda i: (i, 0)),
        out_shape=jax.ShapeDtypeStruct((T // MOBA_BLOCK, 256), F32),
        compiler_params=_params("parallel"),
        name="moba_kmean",
    )(z)


def _moba_prep_kernel(z_ref, km_ref, qt_out, k_out, vt_out):
    HD = MOBA_HD
    tq = z_ref.shape[0]
    nb = km_ref.shape[0]
    qb = pl.program_id(0) % nb
    blk_i = _iota((nb, tq), 0)
    blk = blk_i.astype(F32)
    pad = LANES - HD - nb
    this_block_lanes = (_iota((tq, nb), 1) == qb).astype(BF16)
    ones_row = (_iota((LANES - HD, tq), 0) == 0).astype(BF16)
    for hh in range(MOBA_HEADS):
        cols = slice(hh * HD, (hh + 1) * HD)
        q_t = z_ref[:, cols].astype(F32).T
        gate = _dot(km_ref[:, cols], q_t, precision=HIGHEST)
        gate = jnp.where(blk_i < qb, gate, NEG_INF)
        keep = jnp.where(blk_i == qb, 1.0, 0.0)
        for _ in range(MOBA_TOPK):
            best = jnp.max(gate, axis=0, keepdims=True)
            first = jnp.min(jnp.where(gate == best, blk, float(nb)), axis=0, keepdims=True)
            hit = blk == first
            keep = keep + jnp.where(hit, jnp.where(best > 0.5 * NEG_INF, 1.0, 0.0), 0.0)
            gate = jnp.where(hit, BELOW_NEG_INF, gate)
        bias = ((keep - 1.0) * (-NEG_INF)).astype(BF16)
        qt_out[hh, 0] = jnp.concatenate([(q_t * (HD ** -0.5)).astype(BF16), bias, jnp.zeros((pad, tq), BF16)],
                                        axis=0)
        k_out[hh] = jnp.concatenate([z_ref[:, 256 + hh * HD:256 + (hh + 1) * HD], this_block_lanes,
                                     jnp.zeros((tq, pad), BF16)], axis=1)
        v_t = z_ref[:, 512 + hh * HD:512 + (hh + 1) * HD].astype(F32).T.astype(BF16)
        vt_out[hh, 0] = jnp.concatenate([v_t, ones_row], axis=0)


def _moba_prep(z, kmean, seq):
    T = z.shape[0]
    nb = seq // MOBA_BLOCK
    assert MOBA_HD + nb <= LANES
    tq = MOBA_BLOCK
    H = MOBA_HEADS
    per_block_t = pl.BlockSpec((H, 1, LANES, tq), lambda i: (0, i, 0, 0))
    return pl.pallas_call(
        _moba_prep_kernel,
        grid=(T // tq,),
        in_specs=[pl.BlockSpec((tq, MOBA_W), lambda i: (i, 0)), pl.BlockSpec((nb, 256), lambda i: (i // nb, 0))],
        out_specs=[per_block_t, pl.BlockSpec((H, tq, LANES), lambda i: (0, i, 0)), per_block_t],
        out_shape=[jax.ShapeDtypeStruct((H, T // tq, LANES, tq), BF16), jax.ShapeDtypeStruct((H, T, LANES), BF16),
                   jax.ShapeDtypeStruct((H, T // tq, LANES, tq), BF16)],
        compiler_params=_params("parallel"),
        name="moba_prep",
    )(z, kmean)


MOBA_PAIR = 2


def _moba_kernel(qidx_ref, kidx_ref, qt_ref, k_ref, vt_ref, o_ref, m_ref, acc_ref, s_ref):
    HD = MOBA_HD
    B = MOBA_BLOCK
    step = pl.program_id(1)
    qt = qidx_ref[step]
    kt = kidx_ref[step]
    diagonal_step = kt == qt
    last = (kt == qt - 1) | (qt == 0)

    def attend(diagonal):
        causal = _iota((B, B), 0) <= _iota((B, B), 1)
        key_blocks_of = [range(qh + 1) if diagonal else range(MOBA_PAIR) for qh in range(MOBA_PAIR)]
        for hh in range(MOBA_HEADS):
            for qh in range(MOBA_PAIR):
                for c in key_blocks_of[qh]:
                    s = _dot(k_ref[hh, c * B:(c + 1) * B, :], qt_ref[hh, qh])
                    if diagonal and c == qh:
                        s = jnp.where(causal, s, NEG_INF)
                    s_ref[hh, qh, c] = s
        for hh in range(MOBA_HEADS):
            for qh in range(MOBA_PAIR):
                key_blocks = key_blocks_of[qh]
                scores = [s_ref[hh, qh, c] for c in key_blocks]
                m_new = functools.reduce(jnp.maximum, [jnp.max(s, axis=0, keepdims=True) for s in scores])
                if not diagonal:
                    m_old = m_ref[hh, qh]
                    m_new = jnp.maximum(m_old, m_new)
                pv = None
                for c, s in zip(key_blocks, scores):
                    term = _dot(vt_ref[hh, c], jnp.exp(s - m_new).astype(BF16))
                    pv = term if pv is None else pv + term
                acc_ref[hh, qh] = pv if diagonal else jnp.exp(m_old - m_new) * acc_ref[hh, qh] + pv
                m_ref[hh, qh] = m_new

    @pl.when(diagonal_step)
    def _():
        attend(True)

    @pl.when(jnp.logical_not(diagonal_step))
    def _():
        attend(False)

    @pl.when(last)
    def _():
        for qh in range(MOBA_PAIR):
            outs = []
            for hh in range(MOBA_HEADS):
                a = acc_ref[hh, qh].T
                outs.append(a[:, 0:HD] / a[:, HD:HD + 1])
            o_ref[qh * B:(qh + 1) * B, :] = jnp.concatenate(outs, axis=1).astype(BF16)


def _moba(z, seq):
    T = z.shape[0]
    nb = seq // MOBA_BLOCK
    assert nb % MOBA_PAIR == 0
    npair = nb // MOBA_PAIR
    batch = T // seq
    qt_aug, k_aug, vt_aug = _moba_prep(z, _moba_kmean(z), seq)
    qidx = np.concatenate([np.full(t + 1, t) for t in range(npair)]).astype(np.int32)
    kidx = np.concatenate([np.concatenate([[t], np.arange(t)]) for t in range(npair)]).astype(np.int32)
    tq = MOBA_PAIR * MOBA_BLOCK
    H = MOBA_HEADS
    grid_spec = pltpu.PrefetchScalarGridSpec(
        num_scalar_prefetch=2,
        grid=(batch, len(qidx)),
        in_specs=[pl.BlockSpec((H, MOBA_PAIR, LANES, MOBA_BLOCK),
                               lambda b, s, qi, ki: (0, b * npair + qi[s], 0, 0)),
                  pl.BlockSpec((H, tq, LANES), lambda b, s, qi, ki: (0, b * npair + ki[s], 0)),
                  pl.BlockSpec((H, MOBA_PAIR, LANES, MOBA_BLOCK),
                               lambda b, s, qi, ki: (0, b * npair + ki[s], 0, 0))],
        out_specs=pl.BlockSpec((tq, 256), lambda b, s, qi, ki: (b * npair + qi[s], 0)),
        scratch_shapes=[pltpu.VMEM((H, MOBA_PAIR, 1, MOBA_BLOCK), F32),
                        pltpu.VMEM((H, MOBA_PAIR, LANES, MOBA_BLOCK), F32),
                        pltpu.VMEM((H, MOBA_PAIR, MOBA_PAIR, MOBA_BLOCK, MOBA_BLOCK), F32)],
    )
    return pl.pallas_call(
        _moba_kernel,
        grid_spec=grid_spec,
        out_shape=jax.ShapeDtypeStruct((T, 256), BF16),
        compiler_params=_params("parallel", "arbitrary"),
        name="moba_mixer",
    )(jnp.asarray(qidx), jnp.asarray(kidx), qt_aug, k_aug, vt_aug)


def _merge_kernel(h_ref, o0, o1, o2, o3, gates_ref, wb_ref, wout_ref, out_ref):
    merged = None
    for i, o in enumerate((o0, o1, o2, o3)):
        gate = jax.nn.sigmoid(gates_ref[:, i * D_MODEL:(i + 1) * D_MODEL].astype(F32))
        term = gate * _dot(o[...], wb_ref[i])
        merged = term if merged is None else merged + term
    out_ref[...] = h_ref[...] + _dot(merged.astype(BF16), wout_ref[...])


def _merge(h, branches, gates, wb, wout):
    T = h.shape[0]
    tm = TOK_TILE
    return pl.pallas_call(
        _merge_kernel,
        grid=(T // tm,),
        in_specs=[pl.BlockSpec((tm, D_MODEL), lambda i: (i, 0))]
        + [pl.BlockSpec((tm, 256), lambda i: (i, 0))] * 4
        + [pl.BlockSpec((tm, GATE_W), lambda i: (i, 0)),
           pl.BlockSpec((N_BRANCH, 256, D_MODEL), lambda i: (0, 0, 0), pipeline_mode=pl.Buffered(1)),
           pl.BlockSpec((D_MODEL, D_MODEL), lambda i: (0, 0), pipeline_mode=pl.Buffered(1))],
        out_specs=pl.BlockSpec((tm, D_MODEL), lambda i: (i, 0)),
        out_shape=jax.ShapeDtypeStruct((T, D_MODEL), F32),
        compiler_params=_params("parallel"),
        name="merge_out_proj",
    )(h, *branches, gates, wb, wout)


ROW_TILES = D_MODEL // LANES


def _row_tile_spec(rows, index_map):
    return pl.BlockSpec((rows * ROW_TILES, LANES), index_map)


def _row_slab(ref, j):
    return ref[pl.ds(j, ref.shape[0] // ROW_TILES, stride=ROW_TILES), :]


def _store_row_tiles(ref, x):
    for j in range(ROW_TILES):
        ref[pl.ds(j, x.shape[0], stride=ROW_TILES), :] = x[:, j * LANES:(j + 1) * LANES]


def _load_row_tiles(ref):
    return jnp.concatenate([_row_slab(ref, j) for j in range(ROW_TILES)], axis=1)


def _lane_first(mask, lane):
    return jnp.min(jnp.where(mask, lane, float(LANES)), axis=-1, keepdims=True)


def _router_kernel(h_ref, g_ref, wr_ref, br_ref, u_ref, route_ref, route_t_ref, count_ref, carry_ref):
    tm = h_ref.shape[0]

    @pl.when(pl.program_id(0) == 0)
    def _():
        carry_ref[...] = jnp.zeros_like(carry_ref)

    u = _rms(h_ref[...], g_ref[...])
    _store_row_tiles(u_ref, u)
    u_hi, u_lo = _split_bf16(u, 2)
    w_hi, w_lo = _split_bf16(wr_ref[...], 2)
    logits = _dot(u_hi, w_hi) + (_dot(u_lo, w_hi) + _dot(u_hi, w_lo)) + br_ref[...]
    lane_i = _iota((tm, LANES), 1)
    lane = lane_i.astype(F32)
    lane_group = ((lane_i - EXPERT_LANE0) >> 3).astype(F32)

    grp = jnp.where(lane_i < N_GROUPS, logits, BELOW_NEG_INF)
    grp_e = jnp.exp(grp - jnp.max(grp, axis=-1, keepdims=True))
    p_group = grp_e / jnp.sum(grp_e, axis=-1, keepdims=True)
    g_w = jnp.max(p_group, axis=-1, keepdims=True)
    g_sel = _lane_first(p_group == g_w, lane)

    in_group = (lane_i >= EXPERT_LANE0) & (lane_group == g_sel)
    el = jnp.where(in_group, logits, BELOW_NEG_INF)
    top1 = jnp.max(el, axis=-1, keepdims=True)
    lane1 = _lane_first(el == top1, lane)
    el = jnp.where(lane == lane1, BELOW_NEG_INF, el)
    top2 = jnp.max(el, axis=-1, keepdims=True)
    lane2 = _lane_first(el == top2, lane)
    e2 = jnp.exp(top2 - top1)
    w1 = g_w / (1.0 + e2)
    w2 = g_w * e2 / (1.0 + e2)

    onehot = ((lane == lane1) | (lane == lane2)).astype(BF16)
    before = (_iota((tm, tm), 1) < _iota((tm, tm), 0)).astype(BF16)
    seen = _dot(before, onehot) + carry_ref[...]
    rank1 = jnp.sum(jnp.where(lane == lane1, seen, 0.0), axis=-1, keepdims=True)
    rank2 = jnp.sum(jnp.where(lane == lane2, seen, 0.0), axis=-1, keepdims=True)
    carry_ref[...] += jnp.sum(onehot.astype(F32), axis=0, keepdims=True)
    count_ref[...] = carry_ref[...]

    out = jnp.zeros((tm, LANES), F32)
    for idx, val in enumerate((lane1 - EXPERT_LANE0, lane2 - EXPERT_LANE0, w1, w2, rank1, rank2)):
        out = jnp.where(lane_i == idx, val, out)
    route_ref[...] = out
    route_t_ref[...] = out.T[0:8, :]


def _router(h, g, wr, br):
    T = h.shape[0]
    tm = TOK_TILE
    return pl.pallas_call(
        _router_kernel,
        grid=(T // tm,),
        in_specs=[pl.BlockSpec((tm, D_MODEL), lambda i: (i, 0)), _full((1, D_MODEL)),
                  _full((D_MODEL, LANES)), _full((1, LANES))],
        out_specs=[_row_tile_spec(tm, lambda i: (i, 0)), pl.BlockSpec((tm, LANES), lambda i: (i, 0)),
                   pl.BlockSpec((8, tm), lambda i: (0, i)), _full((1, LANES))],
        out_shape=[jax.ShapeDtypeStruct((T * ROW_TILES, LANES), F32), jax.ShapeDtypeStruct((T, LANES), F32),
                   jax.ShapeDtypeStruct((8, T), F32), jax.ShapeDtypeStruct((1, LANES), F32)],
        scratch_shapes=[pltpu.VMEM((1, LANES), F32)],
        compiler_params=_params("arbitrary"),
        name="moe_router",
    )(h, g, wr, br)


ROW_UNROLL = 8


def _row_copy(src_ref, src_row, dst_ref, dst_row, sem):
    src = src_ref.at[pl.ds(pl.multiple_of(src_row * ROW_TILES, ROW_TILES), ROW_TILES)]
    dst = dst_ref.at[pl.ds(pl.multiple_of(dst_row * ROW_TILES, ROW_TILES), ROW_TILES)]
    return pltpu.make_async_copy(src, dst, sem)


def _for_each_row_pair(rows, body):
    def group(g, carry):
        for i in range(ROW_UNROLL):
            for k in range(EXPERT_TOPK):
                body(g * ROW_UNROLL + i, k)
        return carry

    lax.fori_loop(0, rows // ROW_UNROLL, group, 0)


def _dispatch_kernel(dest0_ref, dest1_ref, pstart_ref, pend_ref, u_ref, xs_ref, zero_ref, sem, row_sem):
    tm = ROW_TILE
    tile = pl.program_id(0)
    slot = tile % 2
    dests = (dest0_ref, dest1_ref)

    @pl.when(tile == 0)
    def _():
        zero_ref[...] = jnp.zeros_like(zero_ref)

        def zero_block(first_row):
            start = pl.multiple_of(first_row * ROW_TILES, ROW_TILES)
            fill = pltpu.make_async_copy(zero_ref, xs_ref.at[pl.ds(start, MOE_BLOCK * ROW_TILES)], sem)
            fill.start()
            fill.wait()

        def zero_tail(e, carry):
            @pl.when(pend_ref[e] > pstart_ref[e])
            def _():
                zero_block(pend_ref[e] - MOE_BLOCK)
            return carry

        def zero_unused(b, carry):
            zero_block(b * MOE_BLOCK)
            return carry

        lax.fori_loop(0, N_EXPERTS, zero_tail, 0)
        lax.fori_loop(pend_ref[N_EXPERTS - 1] // MOE_BLOCK, xs_ref.shape[0] // (MOE_BLOCK * ROW_TILES),
                      zero_unused, 0)

    def issue(r, k):
        t = tile * tm + r
        _row_copy(u_ref, t, xs_ref, dests[k][t], row_sem.at[slot]).start(priority=k)

    def wait_tile(s):
        for _ in range(EXPERT_TOPK):
            pltpu.make_async_copy(u_ref.at[pl.ds(0, tm * ROW_TILES)], xs_ref.at[pl.ds(0, tm * ROW_TILES)],
                                  row_sem.at[s]).wait()

    _for_each_row_pair(tm, issue)

    @pl.when(tile > 0)
    def _():
        wait_tile(1 - slot)

    @pl.when(tile == pl.num_programs(0) - 1)
    def _():
        wait_tile(slot)


def _dispatch(dest0, dest1, pstart, pend, u, n_slots):
    T = u.shape[0] // ROW_TILES
    grid_spec = pltpu.PrefetchScalarGridSpec(
        num_scalar_prefetch=4,
        grid=(T // ROW_TILE,),
        in_specs=[pl.BlockSpec(memory_space=pl.ANY)],
        out_specs=pl.BlockSpec(memory_space=pl.ANY),
        scratch_shapes=[pltpu.VMEM((MOE_BLOCK * ROW_TILES, LANES), F32), pltpu.SemaphoreType.DMA(()),
                        pltpu.SemaphoreType.DMA((2,))],
    )
    return pl.pallas_call(
        _dispatch_kernel,
        grid_spec=grid_spec,
        out_shape=jax.ShapeDtypeStruct((n_slots * ROW_TILES, LANES), F32),
        compiler_params=_params("arbitrary"),
        name="moe_dispatch",
    )(dest0, dest1, pstart, pend, u)


def _expert_kernel(blk_e_ref, n_used_ref, x_ref, w1_ref, w3_ref, w2_ref, y_ref, w1b_ref, w3b_ref, w2b_ref):
    i = pl.program_id(0)
    used = i < n_used_ref[0]

    @pl.when((i == 0) | (blk_e_ref[i] != blk_e_ref[jnp.maximum(i - 1, 0)]))
    def _():
        w1b_ref[...] = w1_ref[0, 0].astype(BF16)
        w3b_ref[...] = w3_ref[0, 0].astype(BF16)
        w2b_ref[...] = w2_ref[0, 0].astype(BF16)

    @pl.when(used)
    def _():
        x = _load_row_tiles(x_ref).astype(BF16)
        a = _dot(x, w1b_ref[...])
        hidden = (a * jax.nn.sigmoid(a)) * _dot(x, w3b_ref[...])
        _store_row_tiles(y_ref, _dot(hidden.astype(BF16), w2b_ref[...]))

    @pl.when(jnp.logical_not(used))
    def _():
        y_ref[...] = jnp.zeros_like(y_ref)


def _experts(blk_e, n_used, xs, w1, w3, w2, layer):
    n_slots = xs.shape[0] // ROW_TILES
    bm = MOE_BLOCK
    grid_spec = pltpu.PrefetchScalarGridSpec(
        num_scalar_prefetch=2,
        grid=(n_slots // bm,),
        in_specs=[_row_tile_spec(bm, lambda i, be, nu: (jnp.minimum(i, nu[0] - 1), 0)),
                  pl.BlockSpec((1, 1, D_MODEL, EXPERT_HIDDEN), lambda i, be, nu: (layer, be[i], 0, 0)),
                  pl.BlockSpec((1, 1, D_MODEL, EXPERT_HIDDEN), lambda i, be, nu: (layer, be[i], 0, 0)),
                  pl.BlockSpec((1, 1, EXPERT_HIDDEN, D_MODEL), lambda i, be, nu: (layer, be[i], 0, 0))],
        out_specs=_row_tile_spec(bm, lambda i, be, nu: (i, 0)),
        scratch_shapes=[pltpu.VMEM((D_MODEL, EXPERT_HIDDEN), BF16), pltpu.VMEM((D_MODEL, EXPERT_HIDDEN), BF16),
                        pltpu.VMEM((EXPERT_HIDDEN, D_MODEL), BF16)],
    )
    return pl.pallas_call(
        _expert_kernel,
        grid_spec=grid_spec,
        out_shape=jax.ShapeDtypeStruct((n_slots * ROW_TILES, LANES), F32),
        compiler_params=_params("arbitrary"),
        name="moe_experts",
    )(blk_e, n_used, xs, w1, w3, w2)


def _combine_kernel(dest0_ref, dest1_ref, h_ref, route_ref, g_ref, ys_ref, out_ref, y_ref, sem, *, normalize):
    tm = h_ref.shape[0]
    step = pl.program_id(0)
    slot = step % 2

    dests = (dest0_ref, dest1_ref)

    def gather_tile(tile, buf):
        def issue(r, k):
            _row_copy(ys_ref, dests[k][tile * tm + r], y_ref.at[buf, k], r, sem.at[buf]).start(priority=k)

        _for_each_row_pair(tm, issue)

    @pl.when(step == 0)
    def _():
        gather_tile(0, 0)

    @pl.when(step + 1 < pl.num_programs(0))
    def _():
        gather_tile(step + 1, 1 - slot)

    for k in range(EXPERT_TOPK):
        pltpu.make_async_copy(ys_ref.at[pl.ds(0, tm * ROW_TILES)], y_ref.at[slot, k], sem.at[slot]).wait()
    w0 = route_ref[:, 2:3]
    w1 = route_ref[:, 3:4]
    for j in range(ROW_TILES):
        cols = slice(j * LANES, (j + 1) * LANES)
        out_ref[:, cols] = (h_ref[:, cols] + w0 * _row_slab(y_ref.at[slot, 0], j)
                            + w1 * _row_slab(y_ref.at[slot, 1], j))
    if normalize:
        out_ref[...] = _rms(out_ref[...], g_ref[...])


def _combine(dest0, dest1, h, route, ys, final_g, normalize):
    T = h.shape[0]
    tm = ROW_TILE
    grid_spec = pltpu.PrefetchScalarGridSpec(
        num_scalar_prefetch=2,
        grid=(T // tm,),
        in_specs=[pl.BlockSpec((tm, D_MODEL), lambda i, *_: (i, 0)),
                  pl.BlockSpec((tm, LANES), lambda i, *_: (i, 0)),
                  _full((1, D_MODEL)),
                  pl.BlockSpec(memory_space=pl.ANY)],
        out_specs=pl.BlockSpec((tm, D_MODEL), lambda i, *_: (i, 0)),
        scratch_shapes=[pltpu.VMEM((2, EXPERT_TOPK, tm * ROW_TILES, LANES), F32),
                        pltpu.SemaphoreType.DMA((2,))],
    )
    return pl.pallas_call(
        functools.partial(_combine_kernel, normalize=normalize),
        grid_spec=grid_spec,
        out_shape=jax.ShapeDtypeStruct((T, D_MODEL), F32),
        compiler_params=_params("arbitrary"),
        name="moe_combine",
    )(dest0, dest1, h, route, final_g, ys)


def _moe(h, g, w_group, b_group, w_expert, b_expert, w1, w3, w2, layer, final_g, normalize):
    T = h.shape[0]
    wr = jnp.zeros((D_MODEL, LANES), F32)
    wr = wr.at[:, :N_GROUPS].set(w_group).at[:, EXPERT_LANE0:EXPERT_LANE0 + N_EXPERTS].set(w_expert)
    br = jnp.zeros((1, LANES), F32)
    br = br.at[0, :N_GROUPS].set(b_group).at[0, EXPERT_LANE0:EXPERT_LANE0 + N_EXPERTS].set(b_expert)
    u, route, route_t, count = _router(h, g, wr, br)

    n_blocks = (T * EXPERT_TOPK) // MOE_BLOCK + N_EXPERTS
    counts = count[0, EXPERT_LANE0:EXPERT_LANE0 + N_EXPERTS].astype(I32)
    padded = ((counts + MOE_BLOCK - 1) // MOE_BLOCK) * MOE_BLOCK
    pend = jnp.cumsum(padded)
    pstart = pend - padded
    expert_ids = jnp.arange(N_EXPERTS, dtype=F32)[:, None]

    def slot_of(expert_row, rank_row):
        segment_start = jnp.sum(jnp.where(expert_row[None, :] == expert_ids, pstart[:, None], 0), axis=0)
        return segment_start + rank_row.astype(I32)

    dest0 = slot_of(route_t[0], route_t[4])
    dest1 = slot_of(route_t[1], route_t[5])
    block_row0 = jnp.arange(n_blocks, dtype=I32) * MOE_BLOCK
    blk_e = jnp.minimum(jnp.sum((pend[None, :] <= block_row0[:, None]).astype(I32), axis=1), N_EXPERTS - 1)
    n_used = (pend[-1:] // MOE_BLOCK).astype(I32)

    xs = _dispatch(dest0, dest1, pstart, pend, u, n_blocks * MOE_BLOCK)
    ys = _experts(blk_e, n_used, xs, w1, w3, w2, layer)
    return _combine(dest0, dest1, h, route, ys, final_g, normalize)


def _split_w_in(w_in):
    gla_n = GLA_HEADS * (2 * GLA_DK + 2 * GLA_DV) + GLA_LOWRANK
    swa_n = (SWA_HEADS + 2 * SWA_KV_HEADS) * SWA_HD
    moba_n = 3 * MOBA_HEADS * MOBA_HD
    ret_n = RET_HEADS * (2 * RET_DK + 2 * RET_DV)
    offs = np.cumsum([0, gla_n, swa_n, moba_n, ret_n, GATE_W])
    parts = [w_in[:, offs[i]:offs[i + 1]].astype(BF16) for i in range(5)]
    parts[0] = jnp.pad(parts[0], ((0, 0), (0, GLA_W - gla_n)))
    return parts


def _token_mixer(h, seq, ln1_g, w_in, gla_w_a2, gla_b_a, gla_norm_g, swa_sinks, ret_norm_g, w_branch, w_out,
                 rope):
    z_gla, z_swa, z_moba, z_ret, z_gates = _in_proj(h, ln1_g.reshape(1, D_MODEL), _split_w_in(w_in))
    wa = jnp.zeros((128, 128), F32).at[:GLA_LOWRANK].set(gla_w_a2)
    o_gla = _gla(z_gla, wa, gla_b_a.reshape(1, 128), jnp.tile(gla_norm_g, GLA_HEADS).reshape(1, 256), seq)
    o_swa = _swa(z_swa, swa_sinks, seq)
    o_moba = _moba(z_moba, seq)
    o_ret = _retention(z_ret, *rope, jnp.tile(ret_norm_g, RET_HEADS).reshape(1, 256), seq)
    return _merge(h, (o_gla, o_swa, o_moba, o_ret), z_gates, w_branch.astype(BF16), w_out.astype(BF16))


def kernel(x, ln1_g, w_in, gla_w_a2, gla_b_a, gla_norm_g, swa_sinks, ret_norm_g, w_branch, w_out, ln2_g,
           w_group, b_group, w_expert, b_expert, w1, w3, w2, final_g):
    batch, seq, _ = x.shape
    depth = w_in.shape[0]
    assert depth >= 1
    rope = _rope_tables(seq)
    h = x.reshape(batch * seq, D_MODEL)
    for l in range(depth):
        h = _token_mixer(h, seq, ln1_g[l], w_in[l], gla_w_a2[l], gla_b_a[l], gla_norm_g[l], swa_sinks[l],
                         ret_norm_g[l], w_branch[l], w_out[l], rope)
        h = _moe(h, ln2_g[l].reshape(1, D_MODEL), w_group[l], b_group[l], w_expert[l], b_expert[l],
                 w1, w3, w2, l, final_g.reshape(1, D_MODEL), normalize=(l == depth - 1))
    return h.reshape(batch, seq, D_MODEL)
```

```python
import functools
import math

import numpy as np
import jax
import jax.numpy as jnp
from jax import lax
from jax.experimental import pallas as pl
from jax.experimental.pallas import tpu as pltpu

F32 = jnp.float32
BF16 = jnp.bfloat16
I32 = jnp.int32
HIGHEST = lax.Precision.HIGHEST

D_MODEL = 1024
N_BRANCH = 4
NORM_EPS = 1e-6
NEG_INF = -1e30
BELOW_NEG_INF = -3e38

GLA_HEADS, GLA_DK, GLA_DV, GLA_LOWRANK, GLA_TAU, GLA_CHUNK = 4, 32, 64, 16, 16.0, 64
SWA_HEADS, SWA_KV_HEADS, SWA_HD, SWA_WINDOW = 4, 2, 64, 128
MOBA_HEADS, MOBA_HD, MOBA_BLOCK, MOBA_TOPK = 4, 64, 256, 3
RET_HEADS, RET_DK, RET_DV, RET_CHUNK, RET_ROPE_BASE = 4, 32, 64, 128, 10000.0
N_GROUPS, EXPERTS_PER_GROUP, EXPERT_TOPK, EXPERT_HIDDEN = 4, 8, 2, 256
MOE_BLOCK = 512
N_EXPERTS = N_GROUPS * EXPERTS_PER_GROUP

GLA_W = 896
SWA_W = 512
MOBA_W = 768
RET_W = 768
GATE_W = N_BRANCH * D_MODEL

LANES = 128
EXPERT_LANE0 = 32

VMEM_LIMIT = 56 * 1024 * 1024

TOK_TILE = 512
ROW_TILE = 256
DISPATCH_TILE = 512


def _params(*sem):
    return pltpu.CompilerParams(dimension_semantics=sem, vmem_limit_bytes=VMEM_LIMIT)


def _dot(a, b, precision=None):
    return jnp.dot(a, b, preferred_element_type=F32, precision=precision)


def _dot_nt(a, b, precision=None):
    return lax.dot_general(a, b, (((1,), (1,)), ((), ())), preferred_element_type=F32, precision=precision)


def _dot_tn(a, b, precision=None):
    return lax.dot_general(a, b, (((0,), (0,)), ((), ())), preferred_element_type=F32, precision=precision)


def _split_bf16(x, parts):
    out = []
    for _ in range(parts - 1):
        piece = x.astype(BF16)
        out.append(piece)
        x = x - piece.astype(F32)
    out.append(x.astype(BF16))
    return out


def _dot_f32_by_bf16(x, m):
    return sum(_dot(piece, m) for piece in _split_bf16(x, 3))


def _dot_bf16_by_f32(m, x):
    return sum(_dot(m, piece) for piece in _split_bf16(x, 3))


def _iota(shape, dim):
    return lax.broadcasted_iota(I32, shape, dim)


def _sigmoid(x):
    return 0.5 * jnp.tanh(0.5 * x) + 0.5


def _rms(x, g):
    ms = jnp.mean(x * x, axis=-1, keepdims=True)
    return x * lax.rsqrt(ms + NORM_EPS) * g


def _full(shape):
    return pl.BlockSpec(shape, lambda *_: (0,) * len(shape))


def _in_proj_kernel(h_ref, g_ref, w0, w1, w2, w3, w4, o0, o1, o2, o3, o4):
    u = _rms(h_ref[...], g_ref[...]).astype(BF16)
    for w, o in ((w0, o0), (w1, o1), (w2, o2), (w3, o3), (w4, o4)):
        o[...] = _dot(u, w[...]).astype(BF16)


def _in_proj(h, g, ws):
    T = h.shape[0]
    tm = TOK_TILE
    widths = [w.shape[1] for w in ws]
    return pl.pallas_call(
        _in_proj_kernel,
        grid=(T // tm,),
        in_specs=[pl.BlockSpec((tm, D_MODEL), lambda i: (i, 0)), _full((1, D_MODEL))]
        + [pl.BlockSpec((D_MODEL, n), lambda i: (0, 0), pipeline_mode=pl.Buffered(1)) for n in widths],
        out_specs=[pl.BlockSpec((tm, n), lambda i: (i, 0)) for n in widths],
        out_shape=[jax.ShapeDtypeStruct((T, n), BF16) for n in widths],
        compiler_params=_params("parallel"),
        name="in_proj",
    )(h, g, *ws)


def _head_select(x, lane_head, n_heads, rows):
    out = None
    for hh in range(n_heads):
        term = jnp.where(lane_head == hh, x[hh * rows:(hh + 1) * rows], 0.0)
        out = term if out is None else out + term
    return out


def _head_norm_gate(o, gate_in, g):
    n = o.shape[1]
    same_head = (_iota((n, n), 0) >> 6) == (_iota((n, n), 1) >> 6)
    ms = _dot_f32_by_bf16(o * o, same_head.astype(BF16)) * (1.0 / 64.0)
    return o * lax.rsqrt(ms + NORM_EPS) * g * (gate_in * _sigmoid(gate_in))


def _gla_kernel(z_ref, wa_ref, ba_ref, g_ref, o_ref, st_ref, oacc_ref, *, tiles_per_seq):
    C = GLA_CHUNK
    tt = z_ref.shape[0]

    @pl.when(pl.program_id(0) % tiles_per_seq == 0)
    def _():
        st_ref[...] = jnp.zeros_like(st_ref)

    q = z_ref[:, 0:128].astype(F32)
    k = z_ref[:, 128:256].astype(F32)
    r = z_ref[:, 512:768].astype(F32)
    pre = _dot_bf16_by_f32(z_ref[:, 768:896], wa_ref[...]) + ba_ref[...]
    log_a = (jnp.minimum(pre, 0.0) - jnp.log(1.0 + jnp.exp(-jnp.abs(pre)))) * (1.0 / GLA_TAU)

    lower = (_iota((C, C), 1) <= _iota((C, C), 0)).astype(BF16)
    b = jnp.concatenate([_dot_bf16_by_f32(lower, log_a[c * C:(c + 1) * C]) for c in range(tt // C)], axis=0)
    b_last = jnp.broadcast_to(b.reshape(tt // C, C, 128)[:, C - 1:C, :], (tt // C, C, 128)).reshape(tt, 128)
    q_dec = (q * (GLA_DK ** -0.5) * jnp.exp(b)).astype(BF16)
    k_dec = (k * jnp.exp(-b)).astype(BF16)
    k_end = (k * jnp.exp(b_last - b)).astype(BF16)
    decay = jnp.exp(b_last)

    qk_head = _iota((C, 128), 1) >> 5
    v_head = _iota((C, 256), 1) >> 6
    causal = _iota((4 * C, C), 1) <= (_iota((4 * C, C), 0) & (C - 1))
    state_mask = (_iota((256, 128), 0) >> 6) == (_iota((256, 128), 1) >> 5)

    for c in range(tt // C):
        sl = slice(c * C, (c + 1) * C)
        qd = q_dec[sl]
        v = z_ref[sl, 256:512]
        q_stack = jnp.concatenate([jnp.where(qk_head == hh, qd, 0) for hh in range(GLA_HEADS)], axis=0)
        att = jnp.where(causal, _dot_nt(q_stack, k_dec[sl]), 0.0)
        intra = _head_select(_dot(att.astype(BF16), v), v_head, GLA_HEADS, C)
        state = st_ref[...]
        inter = _dot_nt(qd, state.astype(BF16))
        oacc_ref[sl, :] = intra + inter
        kv = _dot_tn(v, k_end[sl])
        st_ref[...] = state * decay[c * C:c * C + 1] + jnp.where(state_mask, kv, 0.0)

    o_ref[...] = _head_norm_gate(oacc_ref[...], r, g_ref[...]).astype(BF16)


def _gla(z, wa, ba, g, seq):
    T = z.shape[0]
    tt = TOK_TILE
    return pl.pallas_call(
        functools.partial(_gla_kernel, tiles_per_seq=seq // tt),
        grid=(T // tt,),
        in_specs=[pl.BlockSpec((tt, GLA_W), lambda i: (i, 0)), _full((128, 128)), _full((1, 128)),
                  _full((1, 256))],
        out_specs=pl.BlockSpec((tt, 256), lambda i: (i, 0)),
        out_shape=jax.ShapeDtypeStruct((T, 256), BF16),
        scratch_shapes=[pltpu.VMEM((256, 128), F32), pltpu.VMEM((tt, 256), F32)],
        compiler_params=_params("arbitrary"),
        name="gla_mixer",
    )(z, wa, ba, g)


_RET_LOG_GAMMA = [math.log1p(-(2.0 ** (-5.0 - hh))) for hh in range(RET_HEADS)]


def _by_head(head, values):
    out = jnp.full(head.shape, values[-1], F32)
    for hh in range(len(values) - 2, -1, -1):
        out = jnp.where(head == hh, values[hh], out)
    return out


def _ret_kernel(z_ref, cos_ref, sina_ref, sinb_ref, g_ref, o_ref, st_ref, oacc_ref, *, tiles_per_seq):
    C = RET_CHUNK
    tt = z_ref.shape[0]

    @pl.when(pl.program_id(0) % tiles_per_seq == 0)
    def _():
        st_ref[...] = jnp.zeros_like(st_ref)

    cos, sina, sinb = cos_ref[...], sina_ref[...], sinb_ref[...]

    def rotate(t):
        return t * cos + pltpu.roll(t, 112, 1) * sina + pltpu.roll(t, 16, 1) * sinb

    q_rot = rotate(z_ref[:, 0:128].astype(F32))
    k_rot = rotate(z_ref[:, 128:256].astype(F32)) * (RET_DK ** -0.5)
    gate_in = z_ref[:, 512:768].astype(F32)

    qk_head = _iota((C, 128), 1) >> 5
    v_head = _iota((C, 256), 1) >> 6
    lg_qk = _by_head(_iota((1, 128), 1) >> 5, _RET_LOG_GAMMA)
    lg_v = _by_head(_iota((1, 256), 1) >> 6, _RET_LOG_GAMMA)
    pos_qk = _iota((C, 128), 0).astype(F32)
    pos_v = _iota((C, 256), 0).astype(F32)
    key_decay = jnp.exp((C - 1.0 - pos_qk) * lg_qk)
    query_decay = jnp.exp((pos_v + 1.0) * lg_v)
    chunk_decay = jnp.exp(float(C) * lg_qk)
    srow = _iota((4 * C, C), 0)
    rel = ((srow & (C - 1)) - _iota((4 * C, C), 1)).astype(F32)
    decay_mat = jnp.where(rel >= 0, jnp.exp(jnp.maximum(rel, 0.0) * _by_head(srow >> 7, _RET_LOG_GAMMA)), 0.0)
    state_mask = (_iota((256, 128), 0) >> 6) == (_iota((256, 128), 1) >> 5)

    q_bf = q_rot.astype(BF16)
    k_bf = k_rot.astype(BF16)
    for c in range(tt // C):
        sl = slice(c * C, (c + 1) * C)
        qc = q_bf[sl]
        v = z_ref[sl, 256:512]
        q_stack = jnp.concatenate([jnp.where(qk_head == hh, qc, 0) for hh in range(RET_HEADS)], axis=0)
        att = _dot_nt(q_stack, k_bf[sl]) * decay_mat
        intra = _head_select(_dot(att.astype(BF16), v), v_head, RET_HEADS, C)
        state = st_ref[...]
        inter = _dot_nt(qc, state.astype(BF16)) * query_decay
        oacc_ref[sl, :] = intra + inter
        kv = _dot_tn(v, (k_rot[sl] * key_decay).astype(BF16))
        st_ref[...] = state * chunk_decay + jnp.where(state_mask, kv, 0.0)

    o_ref[...] = _head_norm_gate(oacc_ref[...], gate_in, g_ref[...]).astype(BF16)


def _retention(z, cos, sina, sinb, g, seq):
    T = z.shape[0]
    tt = TOK_TILE
    tps = seq // tt
    pos_spec = pl.BlockSpec((tt, 128), lambda i: (i % tps, 0))
    return pl.pallas_call(
        functools.partial(_ret_kernel, tiles_per_seq=tps),
        grid=(T // tt,),
        in_specs=[pl.BlockSpec((tt, RET_W), lambda i: (i, 0)), pos_spec, pos_spec, pos_spec, _full((1, 256))],
        out_specs=pl.BlockSpec((tt, 256), lambda i: (i, 0)),
        out_shape=jax.ShapeDtypeStruct((T, 256), BF16),
        scratch_shapes=[pltpu.VMEM((256, 128), F32), pltpu.VMEM((tt, 256), F32)],
        compiler_params=_params("arbitrary"),
        name="retention_mixer",
    )(z, cos, sina, sinb, g)


def _rope_tables(seq):
    half = RET_DK // 2
    pos = jnp.arange(seq, dtype=F32)
    inv_freq = RET_ROPE_BASE ** (-jnp.arange(half, dtype=F32) * 2.0 / RET_DK)
    ang = pos[:, None] * inv_freq[None, :]
    cos = jnp.tile(jnp.cos(ang), (1, LANES // half))
    sin = jnp.tile(jnp.sin(ang), (1, LANES // half))
    first_half = (jnp.arange(LANES) % RET_DK) < half
    return cos, jnp.where(first_half, -sin, 0.0), jnp.where(first_half, 0.0, sin)


def _swa_kernel(sink_ref, cur_ref, prev_ref, o_ref, *, tiles_per_seq):
    W = SWA_WINDOW
    HD = SWA_HD
    tt = cur_ref.shape[0]
    first_key = jnp.where(pl.program_id(0) % tiles_per_seq == 0, W, 0)
    qi = _iota((W, 2 * W), 0)
    sj = _iota((W, 2 * W), 1)
    in_window = (sj > qi) & (sj <= qi + W)
    kv0 = SWA_HEADS * HD
    for c in range(tt // W):
        rows = slice(c * W, (c + 1) * W)
        if c == 0:
            k_prev, v_prev = prev_ref[:, kv0:kv0 + 128], prev_ref[:, kv0 + 128:kv0 + 256]
            mask = in_window & (sj >= first_key)
        else:
            prows = slice((c - 1) * W, c * W)
            k_prev, v_prev = cur_ref[prows, kv0:kv0 + 128], cur_ref[prows, kv0 + 128:kv0 + 256]
            mask = in_window
        k_band = jnp.concatenate([k_prev, cur_ref[rows, kv0:kv0 + 128]], axis=0)
        v_band = jnp.concatenate([v_prev, cur_ref[rows, kv0 + 128:kv0 + 256]], axis=0)
        outs = []
        for hh in range(SWA_HEADS):
            kk = hh // (SWA_HEADS // SWA_KV_HEADS)
            s = _dot_nt(cur_ref[rows, hh * HD:(hh + 1) * HD], k_band[:, kk * HD:(kk + 1) * HD]) * (HD ** -0.5)
            s = jnp.where(mask, s, NEG_INF)
            sink = sink_ref[hh]
            m = jnp.maximum(jnp.max(s, axis=-1, keepdims=True), sink)
            p = jnp.exp(s - m)
            denom = jnp.sum(p, axis=-1, keepdims=True) + jnp.exp(sink - m)
            outs.append(_dot(p.astype(BF16), v_band[:, kk * HD:(kk + 1) * HD]) / denom)
        o_ref[rows, :] = jnp.concatenate(outs, axis=1).astype(BF16)


def _swa(z, sinks, seq):
    T = z.shape[0]
    tt = TOK_TILE
    per = tt // SWA_WINDOW
    return pl.pallas_call(
        functools.partial(_swa_kernel, tiles_per_seq=seq // tt),
        grid=(T // tt,),
        in_specs=[pl.BlockSpec(memory_space=pltpu.SMEM),
                  pl.BlockSpec((tt, SWA_W), lambda i: (i, 0)),
                  pl.BlockSpec((SWA_WINDOW, SWA_W), lambda i: (jnp.maximum(i * per - 1, 0), 0))],
        out_specs=pl.BlockSpec((tt, 256), lambda i: (i, 0)),
        out_shape=jax.ShapeDtypeStruct((T, 256), BF16),
        compiler_params=_params("parallel"),
        name="swa_mixer",
    )(sinks, z, z)


_KMEAN_BLOCKS = 8


def _kmean_kernel(k_ref, o_ref):
    k = k_ref[...].astype(F32).reshape(_KMEAN_BLOCKS, MOBA_BLOCK, 256)
    o_ref[...] = jnp.mean(k, axis=1)


def _moba_kmean(z):
    T = z.shape[0]
    rows = _KMEAN_BLOCKS * MOBA_BLOCK
    return pl.pallas_call(
        _kmean_kernel,
        grid=(T // rows,),
        in_specs=[pl.BlockSpec((rows, 256), lambda i: (i, 1))],
        out_specs=pl.BlockSpec((_KMEAN_BLOCKS, 256), lambda i: (i, 0)),
        out_shape=jax.ShapeDtypeStruct((T // MOBA_BLOCK, 256), F32),
        compiler_params=_params("parallel"),
        name="moba_kmean",
    )(z)


def _moba_prep_kernel(z_ref, km_ref, qt_out, k_out, vt_out):
    HD = MOBA_HD
    tq = z_ref.shape[0]
    nb = km_ref.shape[0]
    qb = pl.program_id(0) % nb
    blk_i = _iota((nb, tq), 0)
    blk = blk_i.astype(F32)
    pad = LANES - HD - nb
    this_block_lanes = (_iota((tq, nb), 1) == qb).astype(BF16)
    ones_row = (_iota((LANES - HD, tq), 0) == 0).astype(BF16)
    for hh in range(MOBA_HEADS):
        cols = slice(hh * HD, (hh + 1) * HD)
        q_t = z_ref[:, cols].astype(F32).T
        gate = _dot(km_ref[:, cols], q_t, precision=HIGHEST)
        gate = jnp.where(blk_i < qb, gate, NEG_INF)
        keep = jnp.where(blk_i == qb, 1.0, 0.0)
        for _ in range(MOBA_TOPK):
            best = jnp.max(gate, axis=0, keepdims=True)
            first = jnp.min(jnp.where(gate == best, blk, float(nb)), axis=0, keepdims=True)
            hit = blk == first
            keep = keep + jnp.where(hit, jnp.where(best > 0.5 * NEG_INF, 1.0, 0.0), 0.0)
            gate = jnp.where(hit, BELOW_NEG_INF, gate)
        bias = ((keep - 1.0) * (-NEG_INF)).astype(BF16)
        qt_out[hh, 0] = jnp.concatenate([(q_t * (HD ** -0.5)).astype(BF16), bias, jnp.zeros((pad, tq), BF16)],
                                        axis=0)
        k_out[hh] = jnp.concatenate([z_ref[:, 256 + hh * HD:256 + (hh + 1) * HD], this_block_lanes,
                                     jnp.zeros((tq, pad), BF16)], axis=1)
        v_t = z_ref[:, 512 + hh * HD:512 + (hh + 1) * HD].astype(F32).T.astype(BF16)
        vt_out[hh, 0] = jnp.concatenate([v_t, ones_row], axis=0)


def _moba_prep(z, kmean, seq):
    T = z.shape[0]
    nb = seq // MOBA_BLOCK
    assert MOBA_HD + nb <= LANES
    tq = MOBA_BLOCK
    H = MOBA_HEADS
    per_block_t = pl.BlockSpec((H, 1, LANES, tq), lambda i: (0, i, 0, 0))
    return pl.pallas_call(
        _moba_prep_kernel,
        grid=(T // tq,),
        in_specs=[pl.BlockSpec((tq, MOBA_W), lambda i: (i, 0)), pl.BlockSpec((nb, 256), lambda i: (i // nb, 0))],
        out_specs=[per_block_t, pl.BlockSpec((H, tq, LANES), lambda i: (0, i, 0)), per_block_t],
        out_shape=[jax.ShapeDtypeStruct((H, T // tq, LANES, tq), BF16), jax.ShapeDtypeStruct((H, T, LANES), BF16),
                   jax.ShapeDtypeStruct((H, T // tq, LANES, tq), BF16)],
        compiler_params=_params("parallel"),
        name="moba_prep",
    )(z, kmean)


MOBA_PAIR = 2


def _moba_kernel(qidx_ref, kidx_ref, qt_ref, k_ref, vt_ref, o_ref, m_ref, acc_ref, s_ref):
    HD = MOBA_HD
    B = MOBA_BLOCK
    step = pl.program_id(1)
    qt = qidx_ref[step]
    kt = kidx_ref[step]
    diagonal_step = kt == qt
    last = (kt == qt - 1) | (qt == 0)

    def attend(diagonal):
        causal = _iota((B, B), 0) <= _iota((B, B), 1)
        key_blocks_of = [range(qh + 1) if diagonal else range(MOBA_PAIR) for qh in range(MOBA_PAIR)]
        for hh in range(MOBA_HEADS):
            for qh in range(MOBA_PAIR):
                for c in key_blocks_of[qh]:
                    s = _dot(k_ref[hh, c * B:(c + 1) * B, :], qt_ref[hh, qh])
                    if diagonal and c == qh:
                        s = jnp.where(causal, s, NEG_INF)
                    s_ref[hh, qh, c] = s
        for hh in range(MOBA_HEADS):
            for qh in range(MOBA_PAIR):
                key_blocks = key_blocks_of[qh]
                scores = [s_ref[hh, qh, c] for c in key_blocks]
                m_new = functools.reduce(jnp.maximum, [jnp.max(s, axis=0, keepdims=True) for s in scores])
                if not diagonal:
                    m_old = m_ref[hh, qh]
                    m_new = jnp.maximum(m_old, m_new)
                pv = None
                for c, s in zip(key_blocks, scores):
                    term = _dot(vt_ref[hh, c], jnp.exp(s - m_new).astype(BF16))
                    pv = term if pv is None else pv + term
                acc_ref[hh, qh] = pv if diagonal else jnp.exp(m_old - m_new) * acc_ref[hh, qh] + pv
                m_ref[hh, qh] = m_new

    @pl.when(diagonal_step)
    def _():
        attend(True)

    @pl.when(jnp.logical_not(diagonal_step))
    def _():
        attend(False)

    @pl.when(last)
    def _():
        for qh in range(MOBA_PAIR):
            outs = []
            for hh in range(MOBA_HEADS):
                a = acc_ref[hh, qh].T
                outs.append(a[:, 0:HD] / a[:, HD:HD + 1])
            o_ref[qh * B:(qh + 1) * B, :] = jnp.concatenate(outs, axis=1).astype(BF16)


def _moba(z, seq):
    T = z.shape[0]
    nb = seq // MOBA_BLOCK
    assert nb % MOBA_PAIR == 0
    npair = nb // MOBA_PAIR
    batch = T // seq
    qt_aug, k_aug, vt_aug = _moba_prep(z, _moba_kmean(z), seq)
    qidx = np.concatenate([np.full(t + 1, t) for t in range(npair)]).astype(np.int32)
    kidx = np.concatenate([np.concatenate([[t], np.arange(t)]) for t in range(npair)]).astype(np.int32)
    tq = MOBA_PAIR * MOBA_BLOCK
    H = MOBA_HEADS
    grid_spec = pltpu.PrefetchScalarGridSpec(
        num_scalar_prefetch=2,
        grid=(batch, len(qidx)),
        in_specs=[pl.BlockSpec((H, MOBA_PAIR, LANES, MOBA_BLOCK),
                               lambda b, s, qi, ki: (0, b * npair + qi[s], 0, 0)),
                  pl.BlockSpec((H, tq, LANES), lambda b, s, qi, ki: (0, b * npair + ki[s], 0)),
                  pl.BlockSpec((H, MOBA_PAIR, LANES, MOBA_BLOCK),
                               lambda b, s, qi, ki: (0, b * npair + ki[s], 0, 0))],
        out_specs=pl.BlockSpec((tq, 256), lambda b, s, qi, ki: (b * npair + qi[s], 0)),
        scratch_shapes=[pltpu.VMEM((H, MOBA_PAIR, 1, MOBA_BLOCK), F32),
                        pltpu.VMEM((H, MOBA_PAIR, LANES, MOBA_BLOCK), F32),
                        pltpu.VMEM((H, MOBA_PAIR, MOBA_PAIR, MOBA_BLOCK, MOBA_BLOCK), F32)],
    )
    return pl.pallas_call(
        _moba_kernel,
        grid_spec=grid_spec,
        out_shape=jax.ShapeDtypeStruct((T, 256), BF16),
        compiler_params=_params("parallel", "arbitrary"),
        name="moba_mixer",
    )(jnp.asarray(qidx), jnp.asarray(kidx), qt_aug, k_aug, vt_aug)


def _merge_kernel(h_ref, o0, o1, o2, o3, gates_ref, wb_ref, wout_ref, out_ref):
    merged = None
    for i, o in enumerate((o0, o1, o2, o3)):
        gate = _sigmoid(gates_ref[:, i * D_MODEL:(i + 1) * D_MODEL].astype(F32))
        term = gate * _dot(o[...], wb_ref[i])
        merged = term if merged is None else merged + term
    out_ref[...] = h_ref[...] + _dot(merged.astype(BF16), wout_ref[...])


def _merge(h, branches, gates, wb, wout):
    T = h.shape[0]
    tm = TOK_TILE
    return pl.pallas_call(
        _merge_kernel,
        grid=(T // tm,),
        in_specs=[pl.BlockSpec((tm, D_MODEL), lambda i: (i, 0))]
        + [pl.BlockSpec((tm, 256), lambda i: (i, 0))] * 4
        + [pl.BlockSpec((tm, GATE_W), lambda i: (i, 0)),
           pl.BlockSpec((N_BRANCH, 256, D_MODEL), lambda i: (0, 0, 0), pipeline_mode=pl.Buffered(1)),
           pl.BlockSpec((D_MODEL, D_MODEL), lambda i: (0, 0), pipeline_mode=pl.Buffered(1))],
        out_specs=pl.BlockSpec((tm, D_MODEL), lambda i: (i, 0)),
        out_shape=jax.ShapeDtypeStruct((T, D_MODEL), F32),
        compiler_params=_params("parallel"),
        name="merge_out_proj",
    )(h, *branches, gates, wb, wout)


ROW_TILES = D_MODEL // LANES


def _row_tile_spec(rows, index_map):
    return pl.BlockSpec((rows * ROW_TILES, LANES), index_map)


def _row_slab(ref, j):
    return ref[pl.ds(j, ref.shape[0] // ROW_TILES, stride=ROW_TILES), :]


def _store_row_tiles(ref, x):
    for j in range(ROW_TILES):
        ref[pl.ds(j, x.shape[0], stride=ROW_TILES), :] = x[:, j * LANES:(j + 1) * LANES]


def _load_row_tiles(ref):
    return jnp.concatenate([_row_slab(ref, j) for j in range(ROW_TILES)], axis=1)


def _lane_first(mask, lane):
    return jnp.min(jnp.where(mask, lane, float(LANES)), axis=-1, keepdims=True)


def _router_kernel(h_ref, g_ref, wr_ref, br_ref, u_ref, route_ref, route_t_ref, count_ref, carry_ref):
    tm = h_ref.shape[0]

    @pl.when(pl.program_id(0) == 0)
    def _():
        carry_ref[...] = jnp.zeros_like(carry_ref)

    u = _rms(h_ref[...], g_ref[...])
    _store_row_tiles(u_ref, u)
    u_hi, u_lo = _split_bf16(u, 2)
    w_hi, w_lo = _split_bf16(wr_ref[...], 2)
    logits = _dot(u_hi, w_hi) + (_dot(u_lo, w_hi) + _dot(u_hi, w_lo)) + br_ref[...]
    lane_i = _iota((tm, LANES), 1)
    lane = lane_i.astype(F32)
    lane_group = ((lane_i - EXPERT_LANE0) >> 3).astype(F32)

    grp = jnp.where(lane_i < N_GROUPS, logits, BELOW_NEG_INF)
    grp_e = jnp.exp(grp - jnp.max(grp, axis=-1, keepdims=True))
    p_group = grp_e / jnp.sum(grp_e, axis=-1, keepdims=True)
    g_w = jnp.max(p_group, axis=-1, keepdims=True)
    g_sel = _lane_first(p_group == g_w, lane)

    in_group = (lane_i >= EXPERT_LANE0) & (lane_group == g_sel)
    el = jnp.where(in_group, logits, BELOW_NEG_INF)
    top1 = jnp.max(el, axis=-1, keepdims=True)
    lane1 = _lane_first(el == top1, lane)
    el = jnp.where(lane == lane1, BELOW_NEG_INF, el)
    top2 = jnp.max(el, axis=-1, keepdims=True)
    lane2 = _lane_first(el == top2, lane)
    e2 = jnp.exp(top2 - top1)
    w1 = g_w / (1.0 + e2)
    w2 = g_w * e2 / (1.0 + e2)

    onehot = ((lane == lane1) | (lane == lane2)).astype(BF16)
    before = (_iota((tm, tm), 1) < _iota((tm, tm), 0)).astype(BF16)
    seen = _dot(before, onehot) + carry_ref[...]
    rank1 = jnp.sum(jnp.where(lane == lane1, seen, 0.0), axis=-1, keepdims=True)
    rank2 = jnp.sum(jnp.where(lane == lane2, seen, 0.0), axis=-1, keepdims=True)
    carry_ref[...] += jnp.sum(onehot.astype(F32), axis=0, keepdims=True)
    count_ref[...] = carry_ref[...]

    out = jnp.zeros((tm, LANES), F32)
    for idx, val in enumerate((lane1 - EXPERT_LANE0, lane2 - EXPERT_LANE0, w1, w2, rank1, rank2)):
        out = jnp.where(lane_i == idx, val, out)
    route_ref[...] = out
    route_t_ref[...] = out.T[0:8, :]


def _router(h, g, wr, br):
    T = h.shape[0]
    tm = TOK_TILE
    return pl.pallas_call(
        _router_kernel,
        grid=(T // tm,),
        in_specs=[pl.BlockSpec((tm, D_MODEL), lambda i: (i, 0)), _full((1, D_MODEL)),
                  _full((D_MODEL, LANES)), _full((1, LANES))],
        out_specs=[_row_tile_spec(tm, lambda i: (i, 0)), pl.BlockSpec((tm, LANES), lambda i: (i, 0)),
                   pl.BlockSpec((8, tm), lambda i: (0, i)), _full((1, LANES))],
        out_shape=[jax.ShapeDtypeStruct((T * ROW_TILES, LANES), F32), jax.ShapeDtypeStruct((T, LANES), F32),
                   jax.ShapeDtypeStruct((8, T), F32), jax.ShapeDtypeStruct((1, LANES), F32)],
        scratch_shapes=[pltpu.VMEM((1, LANES), F32)],
        compiler_params=_params("arbitrary"),
        name="moe_router",
    )(h, g, wr, br)


ROW_UNROLL = 8


def _row_copy(src_ref, src_row, dst_ref, dst_row, sem):
    src = src_ref.at[pl.ds(pl.multiple_of(src_row * ROW_TILES, ROW_TILES), ROW_TILES)]
    dst = dst_ref.at[pl.ds(pl.multiple_of(dst_row * ROW_TILES, ROW_TILES), ROW_TILES)]
    return pltpu.make_async_copy(src, dst, sem)


def _for_each_row_pair(rows, body):
    def group(g, carry):
        for i in range(ROW_UNROLL):
            for k in range(EXPERT_TOPK):
                body(g * ROW_UNROLL + i, k)
        return carry

    lax.fori_loop(0, rows // ROW_UNROLL, group, 0)


def _dispatch_kernel(dest0_ref, dest1_ref, pstart_ref, pend_ref, u_ref, xs_ref, zero_ref, sem):
    tm = u_ref.shape[0] // ROW_TILES
    tile = pl.program_id(0)
    dests = (dest0_ref, dest1_ref)

    @pl.when(tile == 0)
    def _():
        zero_ref[...] = jnp.zeros_like(zero_ref)

        def zero_block(first_row):
            start = pl.multiple_of(first_row * ROW_TILES, ROW_TILES)
            fill = pltpu.make_async_copy(zero_ref, xs_ref.at[pl.ds(start, MOE_BLOCK * ROW_TILES)], sem)
            fill.start()
            fill.wait()

        def zero_tail(e, carry):
            @pl.when(pend_ref[e] > pstart_ref[e])
            def _():
                zero_block(pend_ref[e] - MOE_BLOCK)
            return carry

        def zero_unused(b, carry):
            zero_block(b * MOE_BLOCK)
            return carry

        lax.fori_loop(0, N_EXPERTS, zero_tail, 0)
        lax.fori_loop(pend_ref[N_EXPERTS - 1] // MOE_BLOCK, xs_ref.shape[0] // (MOE_BLOCK * ROW_TILES),
                      zero_unused, 0)

    def issue(r, k):
        _row_copy(u_ref, r, xs_ref, dests[k][tile * tm + r], sem).start(priority=k)

    _for_each_row_pair(tm, issue)
    for _ in range(EXPERT_TOPK):
        pltpu.make_async_copy(u_ref, xs_ref.at[pl.ds(0, tm * ROW_TILES)], sem).wait()


def _dispatch(dest0, dest1, pstart, pend, u, n_slots):
    T = u.shape[0] // ROW_TILES
    tm = DISPATCH_TILE
    grid_spec = pltpu.PrefetchScalarGridSpec(
        num_scalar_prefetch=4,
        grid=(T // tm,),
        in_specs=[_row_tile_spec(tm, lambda i, *_: (i, 0))],
        out_specs=pl.BlockSpec(memory_space=pl.ANY),
        scratch_shapes=[pltpu.VMEM((MOE_BLOCK * ROW_TILES, LANES), F32), pltpu.SemaphoreType.DMA(())],
    )
    return pl.pallas_call(
        _dispatch_kernel,
        grid_spec=grid_spec,
        out_shape=jax.ShapeDtypeStruct((n_slots * ROW_TILES, LANES), F32),
        compiler_params=_params("arbitrary"),
        name="moe_dispatch",
    )(dest0, dest1, pstart, pend, u)


def _expert_kernel(blk_e_ref, n_used_ref, x_ref, w1_ref, w3_ref, w2_ref, y_ref, w1b_ref, w3b_ref, w2b_ref):
    i = pl.program_id(0)
    used = i < n_used_ref[0]

    @pl.when((i == 0) | (blk_e_ref[i] != blk_e_ref[jnp.maximum(i - 1, 0)]))
    def _():
        w1b_ref[...] = w1_ref[0, 0].astype(BF16)
        w3b_ref[...] = w3_ref[0, 0].astype(BF16)
        w2b_ref[...] = w2_ref[0, 0].astype(BF16)

    @pl.when(used)
    def _():
        x = _load_row_tiles(x_ref).astype(BF16)
        a = _dot(x, w1b_ref[...])
        hidden = (a * _sigmoid(a)) * _dot(x, w3b_ref[...])
        _store_row_tiles(y_ref, _dot(hidden.astype(BF16), w2b_ref[...]))

    @pl.when(jnp.logical_not(used))
    def _():
        y_ref[...] = jnp.zeros_like(y_ref)


def _experts(blk_e, n_used, xs, w1, w3, w2, layer):
    n_slots = xs.shape[0] // ROW_TILES
    bm = MOE_BLOCK
    grid_spec = pltpu.PrefetchScalarGridSpec(
        num_scalar_prefetch=2,
        grid=(n_slots // bm,),
        in_specs=[_row_tile_spec(bm, lambda i, be, nu: (jnp.minimum(i, nu[0] - 1), 0)),
                  pl.BlockSpec((1, 1, D_MODEL, EXPERT_HIDDEN), lambda i, be, nu: (layer, be[i], 0, 0)),
                  pl.BlockSpec((1, 1, D_MODEL, EXPERT_HIDDEN), lambda i, be, nu: (layer, be[i], 0, 0)),
                  pl.BlockSpec((1, 1, EXPERT_HIDDEN, D_MODEL), lambda i, be, nu: (layer, be[i], 0, 0))],
        out_specs=_row_tile_spec(bm, lambda i, be, nu: (i, 0)),
        scratch_shapes=[pltpu.VMEM((D_MODEL, EXPERT_HIDDEN), BF16), pltpu.VMEM((D_MODEL, EXPERT_HIDDEN), BF16),
                        pltpu.VMEM((EXPERT_HIDDEN, D_MODEL), BF16)],
    )
    return pl.pallas_call(
        _expert_kernel,
        grid_spec=grid_spec,
        out_shape=jax.ShapeDtypeStruct((n_slots * ROW_TILES, LANES), F32),
        compiler_params=_params("arbitrary"),
        name="moe_experts",
    )(blk_e, n_used, xs, w1, w3, w2)


def _combine_kernel(dest0_ref, dest1_ref, h_ref, route_ref, g_ref, ys_ref, out_ref, y_ref, sem, *, normalize):
    tm = h_ref.shape[0]
    step = pl.program_id(0)
    slot = step % 2

    dests = (dest0_ref, dest1_ref)

    def gather_tile(tile, buf):
        def issue(r, k):
            _row_copy(ys_ref, dests[k][tile * tm + r], y_ref.at[buf, k], r, sem.at[buf]).start(priority=k)

        _for_each_row_pair(tm, issue)

    @pl.when(step == 0)
    def _():
        gather_tile(0, 0)

    @pl.when(step + 1 < pl.num_programs(0))
    def _():
        gather_tile(step + 1, 1 - slot)

    for k in range(EXPERT_TOPK):
        pltpu.make_async_copy(ys_ref.at[pl.ds(0, tm * ROW_TILES)], y_ref.at[slot, k], sem.at[slot]).wait()
    w0 = route_ref[:, 2:3]
    w1 = route_ref[:, 3:4]
    for j in range(ROW_TILES):
        cols = slice(j * LANES, (j + 1) * LANES)
        out_ref[:, cols] = (h_ref[:, cols] + w0 * _row_slab(y_ref.at[slot, 0], j)
                            + w1 * _row_slab(y_ref.at[slot, 1], j))
    if normalize:
        out_ref[...] = _rms(out_ref[...], g_ref[...])


def _combine(dest0, dest1, h, route, ys, final_g, normalize):
    T = h.shape[0]
    tm = ROW_TILE
    grid_spec = pltpu.PrefetchScalarGridSpec(
        num_scalar_prefetch=2,
        grid=(T // tm,),
        in_specs=[pl.BlockSpec((tm, D_MODEL), lambda i, *_: (i, 0)),
                  pl.BlockSpec((tm, LANES), lambda i, *_: (i, 0)),
                  _full((1, D_MODEL)),
                  pl.BlockSpec(memory_space=pl.ANY)],
        out_specs=pl.BlockSpec((tm, D_MODEL), lambda i, *_: (i, 0)),
        scratch_shapes=[pltpu.VMEM((2, EXPERT_TOPK, tm * ROW_TILES, LANES), F32),
                        pltpu.SemaphoreType.DMA((2,))],
    )
    return pl.pallas_call(
        functools.partial(_combine_kernel, normalize=normalize),
        grid_spec=grid_spec,
        out_shape=jax.ShapeDtypeStruct((T, D_MODEL), F32),
        compiler_params=_params("arbitrary"),
        name="moe_combine",
    )(dest0, dest1, h, route, final_g, ys)


def _moe(h, g, w_group, b_group, w_expert, b_expert, w1, w3, w2, layer, final_g, normalize):
    T = h.shape[0]
    wr = jnp.zeros((D_MODEL, LANES), F32)
    wr = wr.at[:, :N_GROUPS].set(w_group).at[:, EXPERT_LANE0:EXPERT_LANE0 + N_EXPERTS].set(w_expert)
    br = jnp.zeros((1, LANES), F32)
    br = br.at[0, :N_GROUPS].set(b_group).at[0, EXPERT_LANE0:EXPERT_LANE0 + N_EXPERTS].set(b_expert)
    u, route, route_t, count = _router(h, g, wr, br)

    n_blocks = (T * EXPERT_TOPK) // MOE_BLOCK + N_EXPERTS
    counts = count[0, EXPERT_LANE0:EXPERT_LANE0 + N_EXPERTS].astype(I32)
    padded = ((counts + MOE_BLOCK - 1) // MOE_BLOCK) * MOE_BLOCK
    pend = jnp.cumsum(padded)
    pstart = pend - padded
    expert_ids = jnp.arange(N_EXPERTS, dtype=F32)[:, None]

    def slot_of(expert_row, rank_row):
        segment_start = jnp.sum(jnp.where(expert_row[None, :] == expert_ids, pstart[:, None], 0), axis=0)
        return segment_start + rank_row.astype(I32)

    dest0 = slot_of(route_t[0], route_t[4])
    dest1 = slot_of(route_t[1], route_t[5])
    block_row0 = jnp.arange(n_blocks, dtype=I32) * MOE_BLOCK
    blk_e = jnp.minimum(jnp.sum((pend[None, :] <= block_row0[:, None]).astype(I32), axis=1), N_EXPERTS - 1)
    n_used = (pend[-1:] // MOE_BLOCK).astype(I32)

    xs = _dispatch(dest0, dest1, pstart, pend, u, n_blocks * MOE_BLOCK)
    ys = _experts(blk_e, n_used, xs, w1, w3, w2, layer)
    return _combine(dest0, dest1, h, route, ys, final_g, normalize)


def _split_w_in(w_in):
    gla_n = GLA_HEADS * (2 * GLA_DK + 2 * GLA_DV) + GLA_LOWRANK
    swa_n = (SWA_HEADS + 2 * SWA_KV_HEADS) * SWA_HD
    moba_n = 3 * MOBA_HEADS * MOBA_HD
    ret_n = RET_HEADS * (2 * RET_DK + 2 * RET_DV)
    offs = np.cumsum([0, gla_n, swa_n, moba_n, ret_n, GATE_W])
    parts = [w_in[:, offs[i]:offs[i + 1]].astype(BF16) for i in range(5)]
    parts[0] = jnp.pad(parts[0], ((0, 0), (0, GLA_W - gla_n)))
    return parts


def _token_mixer(h, seq, ln1_g, w_in, gla_w_a2, gla_b_a, gla_norm_g, swa_sinks, ret_norm_g, w_branch, w_out,
                 rope):
    z_gla, z_swa, z_moba, z_ret, z_gates = _in_proj(h, ln1_g.reshape(1, D_MODEL), _split_w_in(w_in))
    wa = jnp.zeros((128, 128), F32).at[:GLA_LOWRANK].set(gla_w_a2)
    o_gla = _gla(z_gla, wa, gla_b_a.reshape(1, 128), jnp.tile(gla_norm_g, GLA_HEADS).reshape(1, 256), seq)
    o_swa = _swa(z_swa, swa_sinks, seq)
    o_moba = _moba(z_moba, seq)
    o_ret = _retention(z_ret, *rope, jnp.tile(ret_norm_g, RET_HEADS).reshape(1, 256), seq)
    return _merge(h, (o_gla, o_swa, o_moba, o_ret), z_gates, w_branch.astype(BF16), w_out.astype(BF16))


def kernel(x, ln1_g, w_in, gla_w_a2, gla_b_a, gla_norm_g, swa_sinks, ret_norm_g, w_branch, w_out, ln2_g,
           w_group, b_group, w_expert, b_expert, w1, w3, w2, final_g):
    batch, seq, _ = x.shape
    depth = w_in.shape[0]
    assert depth >= 1
    rope = _rope_tables(seq)
    h = x.reshape(batch * seq, D_MODEL)
    for l in range(depth):
        h = _token_mixer(h, seq, ln1_g[l], w_in[l], gla_w_a2[l], gla_b_a[l], gla_norm_g[l], swa_sinks[l],
                         ret_norm_g[l], w_branch[l], w_out[l], rope)
        h = _moe(h, ln2_g[l].reshape(1, D_MODEL), w_group[l], b_group[l], w_expert[l], b_expert[l],
                 w1, w3, w2, l, final_g.reshape(1, D_MODEL), normalize=(l == depth - 1))
    return h.reshape(batch, seq, D_MODEL)
```

```python
import functools
import math

import numpy as np
import jax
import jax.numpy as jnp
from jax import lax
from jax.experimental import pallas as pl
from jax.experimental.pallas import tpu as pltpu

F32 = jnp.float32
BF16 = jnp.bfloat16
I32 = jnp.int32
HIGHEST = lax.Precision.HIGHEST

D_MODEL = 1024
N_BRANCH = 4
NORM_EPS = 1e-6
NEG_INF = -1e30
BELOW_NEG_INF = -3e38

GLA_HEADS, GLA_DK, GLA_DV, GLA_LOWRANK, GLA_TAU, GLA_CHUNK = 4, 32, 64, 16, 16.0, 64
SWA_HEADS, SWA_KV_HEADS, SWA_HD, SWA_WINDOW = 4, 2, 64, 128
MOBA_HEADS, MOBA_HD, MOBA_BLOCK, MOBA_TOPK = 4, 64, 256, 3
RET_HEADS, RET_DK, RET_DV, RET_CHUNK, RET_ROPE_BASE = 4, 32, 64, 128, 10000.0
N_GROUPS, EXPERTS_PER_GROUP, EXPERT_TOPK, EXPERT_HIDDEN = 4, 8, 2, 256
MOE_BLOCK = 512
N_EXPERTS = N_GROUPS * EXPERTS_PER_GROUP

GLA_W = 896
SWA_W = 512
MOBA_W = 768
RET_W = 768
GATE_W = N_BRANCH * D_MODEL

LANES = 128
EXPERT_LANE0 = 32

VMEM_LIMIT = 56 * 1024 * 1024

TOK_TILE = 512
ROW_TILE = 256
DISPATCH_TILE = 512


def _params(*sem):
    return pltpu.CompilerParams(dimension_semantics=sem, vmem_limit_bytes=VMEM_LIMIT)


def _dot(a, b, precision=None):
    return jnp.dot(a, b, preferred_element_type=F32, precision=precision)


def _dot_nt(a, b, precision=None):
    return lax.dot_general(a, b, (((1,), (1,)), ((), ())), preferred_element_type=F32, precision=precision)


def _dot_tn(a, b, precision=None):
    return lax.dot_general(a, b, (((0,), (0,)), ((), ())), preferred_element_type=F32, precision=precision)


def _split_bf16(x, parts):
    out = []
    for _ in range(parts - 1):
        piece = x.astype(BF16)
        out.append(piece)
        x = x - piece.astype(F32)
    out.append(x.astype(BF16))
    return out


def _dot_f32_by_bf16(x, m):
    return sum(_dot(piece, m) for piece in _split_bf16(x, 3))


def _dot_bf16_by_f32(m, x):
    return sum(_dot(m, piece) for piece in _split_bf16(x, 3))


def _iota(shape, dim):
    return lax.broadcasted_iota(I32, shape, dim)


def _sigmoid(x):
    return 0.5 * jnp.tanh(0.5 * x) + 0.5


def _rms(x, g):
    ms = jnp.mean(x * x, axis=-1, keepdims=True)
    return x * lax.rsqrt(ms + NORM_EPS) * g


def _full(shape):
    return pl.BlockSpec(shape, lambda *_: (0,) * len(shape))


def _in_proj_kernel(h_ref, g_ref, w0, w1, w2, w3, w4, o0, o1, o2, o3, o4):
    u = _rms(h_ref[...], g_ref[...]).astype(BF16)
    for w, o in ((w0, o0), (w1, o1), (w2, o2), (w3, o3), (w4, o4)):
        o[...] = _dot(u, w[...]).astype(BF16)


def _in_proj(h, g, ws):
    T = h.shape[0]
    tm = TOK_TILE
    widths = [w.shape[1] for w in ws]
    return pl.pallas_call(
        _in_proj_kernel,
        grid=(T // tm,),
        in_specs=[pl.BlockSpec((tm, D_MODEL), lambda i: (i, 0)), _full((1, D_MODEL))]
        + [pl.BlockSpec((D_MODEL, n), lambda i: (0, 0), pipeline_mode=pl.Buffered(1)) for n in widths],
        out_specs=[pl.BlockSpec((tm, n), lambda i: (i, 0)) for n in widths],
        out_shape=[jax.ShapeDtypeStruct((T, n), BF16) for n in widths],
        compiler_params=_params("parallel"),
        name="in_proj",
    )(h, g, *ws)


def _head_select(x, lane_head, n_heads, rows):
    out = None
    for hh in range(n_heads):
        term = jnp.where(lane_head == hh, x[hh * rows:(hh + 1) * rows], 0.0)
        out = term if out is None else out + term
    return out


def _head_norm_gate(o, gate_in, g):
    n = o.shape[1]
    same_head = (_iota((n, n), 0) >> 6) == (_iota((n, n), 1) >> 6)
    ms = _dot_f32_by_bf16(o * o, same_head.astype(BF16)) * (1.0 / 64.0)
    return o * lax.rsqrt(ms + NORM_EPS) * g * (gate_in * _sigmoid(gate_in))


def _gla_kernel(z_ref, wa_ref, ba_ref, g_ref, o_ref, st_ref, oacc_ref, *, tiles_per_seq):
    C = GLA_CHUNK
    tt = z_ref.shape[0]

    @pl.when(pl.program_id(0) % tiles_per_seq == 0)
    def _():
        st_ref[...] = jnp.zeros_like(st_ref)

    q = z_ref[:, 0:128].astype(F32)
    k = z_ref[:, 128:256].astype(F32)
    r = z_ref[:, 512:768].astype(F32)
    pre = _dot_bf16_by_f32(z_ref[:, 768:896], wa_ref[...]) + ba_ref[...]
    log_a = (jnp.minimum(pre, 0.0) - jnp.log(1.0 + jnp.exp(-jnp.abs(pre)))) * (1.0 / GLA_TAU)

    lower = (_iota((C, C), 1) <= _iota((C, C), 0)).astype(BF16)
    b = jnp.concatenate([_dot_bf16_by_f32(lower, log_a[c * C:(c + 1) * C]) for c in range(tt // C)], axis=0)
    b_last = jnp.broadcast_to(b.reshape(tt // C, C, 128)[:, C - 1:C, :], (tt // C, C, 128)).reshape(tt, 128)
    q_dec = (q * (GLA_DK ** -0.5) * jnp.exp(b)).astype(BF16)
    k_dec = (k * jnp.exp(-b)).astype(BF16)
    k_end = (k * jnp.exp(b_last - b)).astype(BF16)
    decay = jnp.exp(b_last)

    qk_head = _iota((C, 128), 1) >> 5
    v_head = _iota((C, 256), 1) >> 6
    causal = _iota((4 * C, C), 1) <= (_iota((4 * C, C), 0) & (C - 1))
    state_mask = (_iota((256, 128), 0) >> 6) == (_iota((256, 128), 1) >> 5)

    for c in range(tt // C):
        sl = slice(c * C, (c + 1) * C)
        qd = q_dec[sl]
        v = z_ref[sl, 256:512]
        q_stack = jnp.concatenate([jnp.where(qk_head == hh, qd, 0) for hh in range(GLA_HEADS)], axis=0)
        att = jnp.where(causal, _dot_nt(q_stack, k_dec[sl]), 0.0)
        intra = _head_select(_dot(att.astype(BF16), v), v_head, GLA_HEADS, C)
        state = st_ref[...]
        inter = _dot_nt(qd, state.astype(BF16))
        oacc_ref[sl, :] = intra + inter
        kv = _dot_tn(v, k_end[sl])
        st_ref[...] = state * decay[c * C:c * C + 1] + jnp.where(state_mask, kv, 0.0)

    o_ref[...] = _head_norm_gate(oacc_ref[...], r, g_ref[...]).astype(BF16)


def _gla(z, wa, ba, g, seq):
    T = z.shape[0]
    tt = TOK_TILE
    return pl.pallas_call(
        functools.partial(_gla_kernel, tiles_per_seq=seq // tt),
        grid=(T // tt,),
        in_specs=[pl.BlockSpec((tt, GLA_W), lambda i: (i, 0)), _full((128, 128)), _full((1, 128)),
                  _full((1, 256))],
        out_specs=pl.BlockSpec((tt, 256), lambda i: (i, 0)),
        out_shape=jax.ShapeDtypeStruct((T, 256), BF16),
        scratch_shapes=[pltpu.VMEM((256, 128), F32), pltpu.VMEM((tt, 256), F32)],
        compiler_params=_params("arbitrary"),
        name="gla_mixer",
    )(z, wa, ba, g)


_RET_LOG_GAMMA = [math.log1p(-(2.0 ** (-5.0 - hh))) for hh in range(RET_HEADS)]


def _by_head(head, values):
    out = jnp.full(head.shape, values[-1], F32)
    for hh in range(len(values) - 2, -1, -1):
        out = jnp.where(head == hh, values[hh], out)
    return out


def _ret_kernel(z_ref, cos_ref, sina_ref, sinb_ref, g_ref, o_ref, st_ref, oacc_ref, *, tiles_per_seq):
    C = RET_CHUNK
    tt = z_ref.shape[0]

    @pl.when(pl.program_id(0) % tiles_per_seq == 0)
    def _():
        st_ref[...] = jnp.zeros_like(st_ref)

    cos, sina, sinb = cos_ref[...], sina_ref[...], sinb_ref[...]

    def rotate(t):
        return t * cos + pltpu.roll(t, 112, 1) * sina + pltpu.roll(t, 16, 1) * sinb

    q_rot = rotate(z_ref[:, 0:128].astype(F32))
    k_rot = rotate(z_ref[:, 128:256].astype(F32)) * (RET_DK ** -0.5)
    gate_in = z_ref[:, 512:768].astype(F32)

    qk_head = _iota((C, 128), 1) >> 5
    v_head = _iota((C, 256), 1) >> 6
    lg_qk = _by_head(_iota((1, 128), 1) >> 5, _RET_LOG_GAMMA)
    lg_v = _by_head(_iota((1, 256), 1) >> 6, _RET_LOG_GAMMA)
    pos_qk = _iota((C, 128), 0).astype(F32)
    pos_v = _iota((C, 256), 0).astype(F32)
    key_decay = jnp.exp((C - 1.0 - pos_qk) * lg_qk)
    query_decay = jnp.exp((pos_v + 1.0) * lg_v)
    chunk_decay = jnp.exp(float(C) * lg_qk)
    srow = _iota((4 * C, C), 0)
    rel = ((srow & (C - 1)) - _iota((4 * C, C), 1)).astype(F32)
    decay_mat = jnp.where(rel >= 0, jnp.exp(jnp.maximum(rel, 0.0) * _by_head(srow >> 7, _RET_LOG_GAMMA)), 0.0)
    state_mask = (_iota((256, 128), 0) >> 6) == (_iota((256, 128), 1) >> 5)

    q_bf = q_rot.astype(BF16)
    k_bf = k_rot.astype(BF16)
    for c in range(tt // C):
        sl = slice(c * C, (c + 1) * C)
        qc = q_bf[sl]
        v = z_ref[sl, 256:512]
        q_stack = jnp.concatenate([jnp.where(qk_head == hh, qc, 0) for hh in range(RET_HEADS)], axis=0)
        att = _dot_nt(q_stack, k_bf[sl]) * decay_mat
        intra = _head_select(_dot(att.astype(BF16), v), v_head, RET_HEADS, C)
        state = st_ref[...]
        inter = _dot_nt(qc, state.astype(BF16)) * query_decay
        oacc_ref[sl, :] = intra + inter
        kv = _dot_tn(v, (k_rot[sl] * key_decay).astype(BF16))
        st_ref[...] = state * chunk_decay + jnp.where(state_mask, kv, 0.0)

    o_ref[...] = _head_norm_gate(oacc_ref[...], gate_in, g_ref[...]).astype(BF16)


def _retention(z, cos, sina, sinb, g, seq):
    T = z.shape[0]
    tt = TOK_TILE
    tps = seq // tt
    pos_spec = pl.BlockSpec((tt, 128), lambda i: (i % tps, 0))
    return pl.pallas_call(
        functools.partial(_ret_kernel, tiles_per_seq=tps),
        grid=(T // tt,),
        in_specs=[pl.BlockSpec((tt, RET_W), lambda i: (i, 0)), pos_spec, pos_spec, pos_spec, _full((1, 256))],
        out_specs=pl.BlockSpec((tt, 256), lambda i: (i, 0)),
        out_shape=jax.ShapeDtypeStruct((T, 256), BF16),
        scratch_shapes=[pltpu.VMEM((256, 128), F32), pltpu.VMEM((tt, 256), F32)],
        compiler_params=_params("arbitrary"),
        name="retention_mixer",
    )(z, cos, sina, sinb, g)


def _rope_tables(seq):
    half = RET_DK // 2
    pos = jnp.arange(seq, dtype=F32)
    inv_freq = RET_ROPE_BASE ** (-jnp.arange(half, dtype=F32) * 2.0 / RET_DK)
    ang = pos[:, None] * inv_freq[None, :]
    cos = jnp.tile(jnp.cos(ang), (1, LANES // half))
    sin = jnp.tile(jnp.sin(ang), (1, LANES // half))
    first_half = (jnp.arange(LANES) % RET_DK) < half
    return cos, jnp.where(first_half, -sin, 0.0), jnp.where(first_half, 0.0, sin)


def _swa_kernel(sink_ref, cur_ref, prev_ref, o_ref, *, tiles_per_seq):
    W = SWA_WINDOW
    HD = SWA_HD
    tt = cur_ref.shape[0]
    first_key = jnp.where(pl.program_id(0) % tiles_per_seq == 0, W, 0)
    qi = _iota((W, 2 * W), 0)
    sj = _iota((W, 2 * W), 1)
    in_window = (sj > qi) & (sj <= qi + W)
    kv0 = SWA_HEADS * HD
    for c in range(tt // W):
        rows = slice(c * W, (c + 1) * W)
        if c == 0:
            k_prev, v_prev = prev_ref[:, kv0:kv0 + 128], prev_ref[:, kv0 + 128:kv0 + 256]
            mask = in_window & (sj >= first_key)
        else:
            prows = slice((c - 1) * W, c * W)
            k_prev, v_prev = cur_ref[prows, kv0:kv0 + 128], cur_ref[prows, kv0 + 128:kv0 + 256]
            mask = in_window
        k_band = jnp.concatenate([k_prev, cur_ref[rows, kv0:kv0 + 128]], axis=0)
        v_band = jnp.concatenate([v_prev, cur_ref[rows, kv0 + 128:kv0 + 256]], axis=0)
        outs = []
        for hh in range(SWA_HEADS):
            kk = hh // (SWA_HEADS // SWA_KV_HEADS)
            s = _dot_nt(cur_ref[rows, hh * HD:(hh + 1) * HD], k_band[:, kk * HD:(kk + 1) * HD]) * (HD ** -0.5)
            s = jnp.where(mask, s, NEG_INF)
            sink = sink_ref[hh]
            m = jnp.maximum(jnp.max(s, axis=-1, keepdims=True), sink)
            p = jnp.exp(s - m)
            denom = jnp.sum(p, axis=-1, keepdims=True) + jnp.exp(sink - m)
            outs.append(_dot(p.astype(BF16), v_band[:, kk * HD:(kk + 1) * HD]) / denom)
        o_ref[rows, :] = jnp.concatenate(outs, axis=1).astype(BF16)


def _swa(z, sinks, seq):
    T = z.shape[0]
    tt = TOK_TILE
    per = tt // SWA_WINDOW
    return pl.pallas_call(
        functools.partial(_swa_kernel, tiles_per_seq=seq // tt),
        grid=(T // tt,),
        in_specs=[pl.BlockSpec(memory_space=pltpu.SMEM),
                  pl.BlockSpec((tt, SWA_W), lambda i: (i, 0)),
                  pl.BlockSpec((SWA_WINDOW, SWA_W), lambda i: (jnp.maximum(i * per - 1, 0), 0))],
        out_specs=pl.BlockSpec((tt, 256), lambda i: (i, 0)),
        out_shape=jax.ShapeDtypeStruct((T, 256), BF16),
        compiler_params=_params("parallel"),
        name="swa_mixer",
    )(sinks, z, z)


_KMEAN_BLOCKS = 8


def _kmean_kernel(k_ref, o_ref):
    k = k_ref[...].astype(F32).reshape(_KMEAN_BLOCKS, MOBA_BLOCK, 256)
    o_ref[...] = jnp.mean(k, axis=1)


def _moba_kmean(z):
    T = z.shape[0]
    rows = _KMEAN_BLOCKS * MOBA_BLOCK
    return pl.pallas_call(
        _kmean_kernel,
        grid=(T // rows,),
        in_specs=[pl.BlockSpec((rows, 256), lambda i: (i, 1))],
        out_specs=pl.BlockSpec((_KMEAN_BLOCKS, 256), lambda i: (i, 0)),
        out_shape=jax.ShapeDtypeStruct((T // MOBA_BLOCK, 256), F32),
        compiler_params=_params("parallel"),
        name="moba_kmean",
    )(z)


MOBA_VROWS = 80


def _moba_prep_kernel(z_ref, km_ref, qt_out, k_out, vt_out):
    HD = MOBA_HD
    tq = z_ref.shape[0]
    nb = km_ref.shape[0]
    qb = pl.program_id(0) % nb
    blk_i = _iota((nb, tq), 0)
    blk = blk_i.astype(F32)
    pad = LANES - HD - nb
    this_block_lanes = (_iota((tq, nb), 1) == qb).astype(BF16)
    ones_row = (_iota((MOBA_VROWS - HD, tq), 0) == 0).astype(BF16)
    for hh in range(MOBA_HEADS):
        cols = slice(hh * HD, (hh + 1) * HD)
        q_t = z_ref[:, cols].astype(F32).T
        gate = _dot(km_ref[:, cols], q_t, precision=HIGHEST)
        gate = jnp.where(blk_i < qb, gate, NEG_INF)
        keep = jnp.where(blk_i == qb, 1.0, 0.0)
        for _ in range(MOBA_TOPK):
            best = jnp.max(gate, axis=0, keepdims=True)
            first = jnp.min(jnp.where(gate == best, blk, float(nb)), axis=0, keepdims=True)
            hit = blk == first
            keep = keep + jnp.where(hit, jnp.where(best > 0.5 * NEG_INF, 1.0, 0.0), 0.0)
            gate = jnp.where(hit, BELOW_NEG_INF, gate)
        bias = ((keep - 1.0) * (-NEG_INF)).astype(BF16)
        qt_out[hh, 0] = jnp.concatenate([(q_t * (HD ** -0.5)).astype(BF16), bias, jnp.zeros((pad, tq), BF16)],
                                        axis=0)
        k_out[hh] = jnp.concatenate([z_ref[:, 256 + hh * HD:256 + (hh + 1) * HD], this_block_lanes,
                                     jnp.zeros((tq, pad), BF16)], axis=1)
        v_t = z_ref[:, 512 + hh * HD:512 + (hh + 1) * HD].astype(F32).T.astype(BF16)
        vt_out[hh, 0] = jnp.concatenate([v_t, ones_row], axis=0)


def _moba_prep(z, kmean, seq):
    T = z.shape[0]
    nb = seq // MOBA_BLOCK
    assert MOBA_HD + nb <= LANES
    tq = MOBA_BLOCK
    H = MOBA_HEADS
    def per_block_t(rows):
        return pl.BlockSpec((H, 1, rows, tq), lambda i: (0, i, 0, 0))

    return pl.pallas_call(
        _moba_prep_kernel,
        grid=(T // tq,),
        in_specs=[pl.BlockSpec((tq, MOBA_W), lambda i: (i, 0)), pl.BlockSpec((nb, 256), lambda i: (i // nb, 0))],
        out_specs=[per_block_t(LANES), pl.BlockSpec((H, tq, LANES), lambda i: (0, i, 0)),
                   per_block_t(MOBA_VROWS)],
        out_shape=[jax.ShapeDtypeStruct((H, T // tq, LANES, tq), BF16), jax.ShapeDtypeStruct((H, T, LANES), BF16),
                   jax.ShapeDtypeStruct((H, T // tq, MOBA_VROWS, tq), BF16)],
        compiler_params=_params("parallel"),
        name="moba_prep",
    )(z, kmean)


MOBA_QBLOCKS = 4
MOBA_KBLOCKS = 2
MOBA_DIAGONALS = MOBA_QBLOCKS // MOBA_KBLOCKS


def _moba_kernel(qidx_ref, kidx_ref, kind_ref, qt_ref, k_ref, vt_ref, o_ref, m_ref, acc_ref, s_ref):
    HD = MOBA_HD
    B = MOBA_BLOCK
    step = pl.program_id(1)
    kind = kind_ref[step]
    is_last = (step + 1 == pl.num_programs(1)) | (qidx_ref[jnp.minimum(step + 1, pl.num_programs(1) - 1)]
                                                  != qidx_ref[step])

    def attend(diagonal):
        causal = _iota((B, B), 0) <= _iota((B, B), 1)

        def key_blocks_of(qi):
            if diagonal is None:
                return list(range(MOBA_KBLOCKS))
            return [c for c in range(MOBA_KBLOCKS) if MOBA_KBLOCKS * diagonal + c <= qi]

        for hh in range(MOBA_HEADS):
            for qi in range(MOBA_QBLOCKS):
                for c in key_blocks_of(qi):
                    s = _dot(k_ref[hh, c * B:(c + 1) * B, :], qt_ref[hh, qi])
                    if diagonal is not None and MOBA_KBLOCKS * diagonal + c == qi:
                        s = jnp.where(causal, s, NEG_INF)
                    s_ref[hh, qi, c] = s
        for hh in range(MOBA_HEADS):
            for qi in range(MOBA_QBLOCKS):
                key_blocks = key_blocks_of(qi)
                if not key_blocks:
                    continue
                scores = [s_ref[hh, qi, c] for c in key_blocks]
                m_new = functools.reduce(jnp.maximum, [jnp.max(s, axis=0, keepdims=True) for s in scores])
                first_visit = diagonal == 0
                if not first_visit:
                    m_old = m_ref[hh, qi]
                    m_new = jnp.maximum(m_old, m_new)
                pv = None
                for c, s in zip(key_blocks, scores):
                    term = _dot(vt_ref[hh, c], jnp.exp(s - m_new).astype(BF16))
                    pv = term if pv is None else pv + term
                acc_ref[hh, qi] = pv if first_visit else jnp.exp(m_old - m_new) * acc_ref[hh, qi] + pv
                m_ref[hh, qi] = m_new

    @pl.when(kind == 0)
    def _():
        attend(None)

    for d in range(MOBA_DIAGONALS):
        @pl.when(kind == 1 + d)
        def _(d=d):
            attend(d)

    @pl.when(is_last)
    def _():
        pad = jnp.zeros((LANES - MOBA_VROWS, B), F32)
        for qi in range(MOBA_QBLOCKS):
            outs = []
            for hh in range(MOBA_HEADS):
                a = jnp.concatenate([acc_ref[hh, qi], pad], axis=0).T
                outs.append(a[:, 0:HD] / a[:, HD:HD + 1])
            o_ref[qi * B:(qi + 1) * B, :] = jnp.concatenate(outs, axis=1).astype(BF16)


def _moba(z, seq):
    T = z.shape[0]
    nb = seq // MOBA_BLOCK
    assert nb % MOBA_QBLOCKS == 0
    q_tiles = nb // MOBA_QBLOCKS
    k_steps = nb // MOBA_KBLOCKS
    D = MOBA_DIAGONALS
    batch = T // seq
    qt_aug, k_aug, vt_aug = _moba_prep(z, _moba_kmean(z), seq)
    qidx = np.concatenate([np.full(D * t + D, t) for t in range(q_tiles)]).astype(np.int32)
    kidx = np.concatenate([np.concatenate([D * t + np.arange(D), np.arange(D * t)])
                           for t in range(q_tiles)]).astype(np.int32)
    kind = np.concatenate([np.concatenate([1 + np.arange(D), np.zeros(D * t, np.int64)])
                           for t in range(q_tiles)]).astype(np.int32)
    tq = MOBA_QBLOCKS * MOBA_BLOCK
    tk = MOBA_KBLOCKS * MOBA_BLOCK
    H = MOBA_HEADS
    grid_spec = pltpu.PrefetchScalarGridSpec(
        num_scalar_prefetch=3,
        grid=(batch, len(qidx)),
        in_specs=[pl.BlockSpec((H, MOBA_QBLOCKS, LANES, MOBA_BLOCK),
                               lambda b, s, qi, ki, kd: (0, b * q_tiles + qi[s], 0, 0)),
                  pl.BlockSpec((H, tk, LANES), lambda b, s, qi, ki, kd: (0, b * k_steps + ki[s], 0)),
                  pl.BlockSpec((H, MOBA_KBLOCKS, MOBA_VROWS, MOBA_BLOCK),
                               lambda b, s, qi, ki, kd: (0, b * k_steps + ki[s], 0, 0))],
        out_specs=pl.BlockSpec((tq, 256), lambda b, s, qi, ki, kd: (b * q_tiles + qi[s], 0)),
        scratch_shapes=[pltpu.VMEM((H, MOBA_QBLOCKS, 1, MOBA_BLOCK), F32),
                        pltpu.VMEM((H, MOBA_QBLOCKS, MOBA_VROWS, MOBA_BLOCK), F32),
                        pltpu.VMEM((H, MOBA_QBLOCKS, MOBA_KBLOCKS, MOBA_BLOCK, MOBA_BLOCK), F32)],
    )
    return pl.pallas_call(
        _moba_kernel,
        grid_spec=grid_spec,
        out_shape=jax.ShapeDtypeStruct((T, 256), BF16),
        compiler_params=_params("parallel", "arbitrary"),
        name="moba_mixer",
    )(jnp.asarray(qidx), jnp.asarray(kidx), jnp.asarray(kind), qt_aug, k_aug, vt_aug)


def _merge_kernel(h_ref, o0, o1, o2, o3, gates_ref, wb_ref, wout_ref, out_ref):
    merged = None
    for i, o in enumerate((o0, o1, o2, o3)):
        gate = _sigmoid(gates_ref[:, i * D_MODEL:(i + 1) * D_MODEL].astype(F32))
        term = gate * _dot(o[...], wb_ref[i])
        merged = term if merged is None else merged + term
    out_ref[...] = h_ref[...] + _dot(merged.astype(BF16), wout_ref[...])


def _merge(h, branches, gates, wb, wout):
    T = h.shape[0]
    tm = TOK_TILE
    return pl.pallas_call(
        _merge_kernel,
        grid=(T // tm,),
        in_specs=[pl.BlockSpec((tm, D_MODEL), lambda i: (i, 0))]
        + [pl.BlockSpec((tm, 256), lambda i: (i, 0))] * 4
        + [pl.BlockSpec((tm, GATE_W), lambda i: (i, 0)),
           pl.BlockSpec((N_BRANCH, 256, D_MODEL), lambda i: (0, 0, 0), pipeline_mode=pl.Buffered(1)),
           pl.BlockSpec((D_MODEL, D_MODEL), lambda i: (0, 0), pipeline_mode=pl.Buffered(1))],
        out_specs=pl.BlockSpec((tm, D_MODEL), lambda i: (i, 0)),
        out_shape=jax.ShapeDtypeStruct((T, D_MODEL), F32),
        compiler_params=_params("parallel"),
        name="merge_out_proj",
    )(h, *branches, gates, wb, wout)


ROW_TILES = D_MODEL // LANES


def _row_tile_spec(rows, index_map):
    return pl.BlockSpec((rows * ROW_TILES, LANES), index_map)


def _row_slab(ref, j):
    return ref[pl.ds(j, ref.shape[0] // ROW_TILES, stride=ROW_TILES), :]


def _store_row_tiles(ref, x):
    for j in range(ROW_TILES):
        ref[pl.ds(j, x.shape[0], stride=ROW_TILES), :] = x[:, j * LANES:(j + 1) * LANES]


def _load_row_tiles(ref):
    return jnp.concatenate([_row_slab(ref, j) for j in range(ROW_TILES)], axis=1)


def _lane_first(mask, lane):
    return jnp.min(jnp.where(mask, lane, float(LANES)), axis=-1, keepdims=True)


def _router_kernel(h_ref, g_ref, wr_ref, br_ref, u_ref, route_ref, route_t_ref, count_ref, carry_ref):
    tm = h_ref.shape[0]

    @pl.when(pl.program_id(0) == 0)
    def _():
        carry_ref[...] = jnp.zeros_like(carry_ref)

    u = _rms(h_ref[...], g_ref[...])
    _store_row_tiles(u_ref, u)
    u_hi, u_lo = _split_bf16(u, 2)
    w_hi, w_lo = _split_bf16(wr_ref[...], 2)
    logits = _dot(u_hi, w_hi) + (_dot(u_lo, w_hi) + _dot(u_hi, w_lo)) + br_ref[...]
    lane_i = _iota((tm, LANES), 1)
    lane = lane_i.astype(F32)
    lane_group = ((lane_i - EXPERT_LANE0) >> 3).astype(F32)

    grp = jnp.where(lane_i < N_GROUPS, logits, BELOW_NEG_INF)
    grp_e = jnp.exp(grp - jnp.max(grp, axis=-1, keepdims=True))
    p_group = grp_e / jnp.sum(grp_e, axis=-1, keepdims=True)
    g_w = jnp.max(p_group, axis=-1, keepdims=True)
    g_sel = _lane_first(p_group == g_w, lane)

    in_group = (lane_i >= EXPERT_LANE0) & (lane_group == g_sel)
    el = jnp.where(in_group, logits, BELOW_NEG_INF)
    top1 = jnp.max(el, axis=-1, keepdims=True)
    lane1 = _lane_first(el == top1, lane)
    el = jnp.where(lane == lane1, BELOW_NEG_INF, el)
    top2 = jnp.max(el, axis=-1, keepdims=True)
    lane2 = _lane_first(el == top2, lane)
    e2 = jnp.exp(top2 - top1)
    w1 = g_w / (1.0 + e2)
    w2 = g_w * e2 / (1.0 + e2)

    onehot = ((lane == lane1) | (lane == lane2)).astype(BF16)
    before = (_iota((tm, tm), 1) < _iota((tm, tm), 0)).astype(BF16)
    seen = _dot(before, onehot) + carry_ref[...]
    rank1 = jnp.sum(jnp.where(lane == lane1, seen, 0.0), axis=-1, keepdims=True)
    rank2 = jnp.sum(jnp.where(lane == lane2, seen, 0.0), axis=-1, keepdims=True)
    carry_ref[...] += jnp.sum(onehot.astype(F32), axis=0, keepdims=True)
    count_ref[...] = carry_ref[...]

    out = jnp.zeros((tm, LANES), F32)
    for idx, val in enumerate((lane1 - EXPERT_LANE0, lane2 - EXPERT_LANE0, w1, w2, rank1, rank2)):
        out = jnp.where(lane_i == idx, val, out)
    route_ref[...] = out
    route_t_ref[...] = out.T[0:8, :]


def _router(h, g, wr, br):
    T = h.shape[0]
    tm = TOK_TILE
    return pl.pallas_call(
        _router_kernel,
        grid=(T // tm,),
        in_specs=[pl.BlockSpec((tm, D_MODEL), lambda i: (i, 0)), _full((1, D_MODEL)),
                  _full((D_MODEL, LANES)), _full((1, LANES))],
        out_specs=[_row_tile_spec(tm, lambda i: (i, 0)), pl.BlockSpec((tm, LANES), lambda i: (i, 0)),
                   pl.BlockSpec((8, tm), lambda i: (0, i)), _full((1, LANES))],
        out_shape=[jax.ShapeDtypeStruct((T * ROW_TILES, LANES), F32), jax.ShapeDtypeStruct((T, LANES), F32),
                   jax.ShapeDtypeStruct((8, T), F32), jax.ShapeDtypeStruct((1, LANES), F32)],
        scratch_shapes=[pltpu.VMEM((1, LANES), F32)],
        compiler_params=_params("arbitrary"),
        name="moe_router",
    )(h, g, wr, br)


ROW_UNROLL = 8


def _row_copy(src_ref, src_row, dst_ref, dst_row, sem):
    src = src_ref.at[pl.ds(pl.multiple_of(src_row * ROW_TILES, ROW_TILES), ROW_TILES)]
    dst = dst_ref.at[pl.ds(pl.multiple_of(dst_row * ROW_TILES, ROW_TILES), ROW_TILES)]
    return pltpu.make_async_copy(src, dst, sem)


def _for_each_row_pair(rows, body):
    def group(g, carry):
        for i in range(ROW_UNROLL):
            for k in range(EXPERT_TOPK):
                body(g * ROW_UNROLL + i, k)
        return carry

    lax.fori_loop(0, rows // ROW_UNROLL, group, 0)


def _dispatch_kernel(dest0_ref, dest1_ref, pstart_ref, pend_ref, u_ref, xs_ref, zero_ref, sem):
    tm = u_ref.shape[0] // ROW_TILES
    tile = pl.program_id(0)
    dests = (dest0_ref, dest1_ref)

    @pl.when(tile == 0)
    def _():
        zero_ref[...] = jnp.zeros_like(zero_ref)

        def zero_block(first_row):
            start = pl.multiple_of(first_row * ROW_TILES, ROW_TILES)
            fill = pltpu.make_async_copy(zero_ref, xs_ref.at[pl.ds(start, MOE_BLOCK * ROW_TILES)], sem)
            fill.start()
            fill.wait()

        def zero_tail(e, carry):
            @pl.when(pend_ref[e] > pstart_ref[e])
            def _():
                zero_block(pend_ref[e] - MOE_BLOCK)
            return carry

        def zero_unused(b, carry):
            zero_block(b * MOE_BLOCK)
            return carry

        lax.fori_loop(0, N_EXPERTS, zero_tail, 0)
        lax.fori_loop(pend_ref[N_EXPERTS - 1] // MOE_BLOCK, xs_ref.shape[0] // (MOE_BLOCK * ROW_TILES),
                      zero_unused, 0)

    def issue(r, k):
        _row_copy(u_ref, r, xs_ref, dests[k][tile * tm + r], sem).start(priority=k)

    _for_each_row_pair(tm, issue)
    for _ in range(EXPERT_TOPK):
        pltpu.make_async_copy(u_ref, xs_ref.at[pl.ds(0, tm * ROW_TILES)], sem).wait()


def _dispatch(dest0, dest1, pstart, pend, u, n_slots):
    T = u.shape[0] // ROW_TILES
    tm = DISPATCH_TILE
    grid_spec = pltpu.PrefetchScalarGridSpec(
        num_scalar_prefetch=4,
        grid=(T // tm,),
        in_specs=[_row_tile_spec(tm, lambda i, *_: (i, 0))],
        out_specs=pl.BlockSpec(memory_space=pl.ANY),
        scratch_shapes=[pltpu.VMEM((MOE_BLOCK * ROW_TILES, LANES), F32), pltpu.SemaphoreType.DMA(())],
    )
    return pl.pallas_call(
        _dispatch_kernel,
        grid_spec=grid_spec,
        out_shape=jax.ShapeDtypeStruct((n_slots * ROW_TILES, LANES), F32),
        compiler_params=_params("arbitrary"),
        name="moe_dispatch",
    )(dest0, dest1, pstart, pend, u)


def _expert_kernel(blk_e_ref, n_used_ref, x_ref, w1_ref, w3_ref, w2_ref, y_ref, w1b_ref, w3b_ref, w2b_ref):
    i = pl.program_id(0)
    used = i < n_used_ref[0]

    @pl.when((i == 0) | (blk_e_ref[i] != blk_e_ref[jnp.maximum(i - 1, 0)]))
    def _():
        w1b_ref[...] = w1_ref[0, 0].astype(BF16)
        w3b_ref[...] = w3_ref[0, 0].astype(BF16)
        w2b_ref[...] = w2_ref[0, 0].astype(BF16)

    @pl.when(used)
    def _():
        x = _load_row_tiles(x_ref).astype(BF16)
        a = _dot(x, w1b_ref[...])
        hidden = (a * _sigmoid(a)) * _dot(x, w3b_ref[...])
        _store_row_tiles(y_ref, _dot(hidden.astype(BF16), w2b_ref[...]))

    @pl.when(jnp.logical_not(used))
    def _():
        y_ref[...] = jnp.zeros_like(y_ref)


def _experts(blk_e, n_used, xs, w1, w3, w2, layer):
    n_slots = xs.shape[0] // ROW_TILES
    bm = MOE_BLOCK
    grid_spec = pltpu.PrefetchScalarGridSpec(
        num_scalar_prefetch=2,
        grid=(n_slots // bm,),
        in_specs=[_row_tile_spec(bm, lambda i, be, nu: (jnp.minimum(i, nu[0] - 1), 0)),
                  pl.BlockSpec((1, 1, D_MODEL, EXPERT_HIDDEN), lambda i, be, nu: (layer, be[i], 0, 0)),
                  pl.BlockSpec((1, 1, D_MODEL, EXPERT_HIDDEN), lambda i, be, nu: (layer, be[i], 0, 0)),
                  pl.BlockSpec((1, 1, EXPERT_HIDDEN, D_MODEL), lambda i, be, nu: (layer, be[i], 0, 0))],
        out_specs=_row_tile_spec(bm, lambda i, be, nu: (i, 0)),
        scratch_shapes=[pltpu.VMEM((D_MODEL, EXPERT_HIDDEN), BF16), pltpu.VMEM((D_MODEL, EXPERT_HIDDEN), BF16),
                        pltpu.VMEM((EXPERT_HIDDEN, D_MODEL), BF16)],
    )
    return pl.pallas_call(
        _expert_kernel,
        grid_spec=grid_spec,
        out_shape=jax.ShapeDtypeStruct((n_slots * ROW_TILES, LANES), F32),
        compiler_params=_params("arbitrary"),
        name="moe_experts",
    )(blk_e, n_used, xs, w1, w3, w2)


def _combine_kernel(dest0_ref, dest1_ref, h_ref, route_ref, g_ref, ys_ref, out_ref, y_ref, sem, *, normalize):
    tm = h_ref.shape[0]
    step = pl.program_id(0)
    slot = step % 2

    dests = (dest0_ref, dest1_ref)

    def gather_tile(tile, buf):
        def issue(r, k):
            _row_copy(ys_ref, dests[k][tile * tm + r], y_ref.at[buf, k], r, sem.at[buf]).start(priority=k)

        _for_each_row_pair(tm, issue)

    @pl.when(step == 0)
    def _():
        gather_tile(0, 0)

    @pl.when(step + 1 < pl.num_programs(0))
    def _():
        gather_tile(step + 1, 1 - slot)

    for k in range(EXPERT_TOPK):
        pltpu.make_async_copy(ys_ref.at[pl.ds(0, tm * ROW_TILES)], y_ref.at[slot, k], sem.at[slot]).wait()
    w0 = route_ref[:, 2:3]
    w1 = route_ref[:, 3:4]
    for j in range(ROW_TILES):
        cols = slice(j * LANES, (j + 1) * LANES)
        out_ref[:, cols] = (h_ref[:, cols] + w0 * _row_slab(y_ref.at[slot, 0], j)
                            + w1 * _row_slab(y_ref.at[slot, 1], j))
    if normalize:
        out_ref[...] = _rms(out_ref[...], g_ref[...])


def _combine(dest0, dest1, h, route, ys, final_g, normalize):
    T = h.shape[0]
    tm = ROW_TILE
    grid_spec = pltpu.PrefetchScalarGridSpec(
        num_scalar_prefetch=2,
        grid=(T // tm,),
        in_specs=[pl.BlockSpec((tm, D_MODEL), lambda i, *_: (i, 0)),
                  pl.BlockSpec((tm, LANES), lambda i, *_: (i, 0)),
                  _full((1, D_MODEL)),
                  pl.BlockSpec(memory_space=pl.ANY)],
        out_specs=pl.BlockSpec((tm, D_MODEL), lambda i, *_: (i, 0)),
        scratch_shapes=[pltpu.VMEM((2, EXPERT_TOPK, tm * ROW_TILES, LANES), F32),
                        pltpu.SemaphoreType.DMA((2,))],
    )
    return pl.pallas_call(
        functools.partial(_combine_kernel, normalize=normalize),
        grid_spec=grid_spec,
        out_shape=jax.ShapeDtypeStruct((T, D_MODEL), F32),
        compiler_params=_params("arbitrary"),
        name="moe_combine",
    )(dest0, dest1, h, route, final_g, ys)


def _moe(h, g, w_group, b_group, w_expert, b_expert, w1, w3, w2, layer, final_g, normalize):
    T = h.shape[0]
    wr = jnp.zeros((D_MODEL, LANES), F32)
    wr = wr.at[:, :N_GROUPS].set(w_group).at[:, EXPERT_LANE0:EXPERT_LANE0 + N_EXPERTS].set(w_expert)
    br = jnp.zeros((1, LANES), F32)
    br = br.at[0, :N_GROUPS].set(b_group).at[0, EXPERT_LANE0:EXPERT_LANE0 + N_EXPERTS].set(b_expert)
    u, route, route_t, count = _router(h, g, wr, br)

    n_blocks = (T * EXPERT_TOPK) // MOE_BLOCK + N_EXPERTS
    counts = count[0, EXPERT_LANE0:EXPERT_LANE0 + N_EXPERTS].astype(I32)
    padded = ((counts + MOE_BLOCK - 1) // MOE_BLOCK) * MOE_BLOCK
    pend = jnp.cumsum(padded)
    pstart = pend - padded
    expert_ids = jnp.arange(N_EXPERTS, dtype=F32)[:, None]

    def slot_of(expert_row, rank_row):
        segment_start = jnp.sum(jnp.where(expert_row[None, :] == expert_ids, pstart[:, None], 0), axis=0)
        return segment_start + rank_row.astype(I32)

    dest0 = slot_of(route_t[0], route_t[4])
    dest1 = slot_of(route_t[1], route_t[5])
    block_row0 = jnp.arange(n_blocks, dtype=I32) * MOE_BLOCK
    blk_e = jnp.minimum(jnp.sum((pend[None, :] <= block_row0[:, None]).astype(I32), axis=1), N_EXPERTS - 1)
    n_used = (pend[-1:] // MOE_BLOCK).astype(I32)

    xs = _dispatch(dest0, dest1, pstart, pend, u, n_blocks * MOE_BLOCK)
    ys = _experts(blk_e, n_used, xs, w1, w3, w2, layer)
    return _combine(dest0, dest1, h, route, ys, final_g, normalize)


def _split_w_in(w_in):
    gla_n = GLA_HEADS * (2 * GLA_DK + 2 * GLA_DV) + GLA_LOWRANK
    swa_n = (SWA_HEADS + 2 * SWA_KV_HEADS) * SWA_HD
    moba_n = 3 * MOBA_HEADS * MOBA_HD
    ret_n = RET_HEADS * (2 * RET_DK + 2 * RET_DV)
    offs = np.cumsum([0, gla_n, swa_n, moba_n, ret_n, GATE_W])
    parts = [w_in[:, offs[i]:offs[i + 1]].astype(BF16) for i in range(5)]
    parts[0] = jnp.pad(parts[0], ((0, 0), (0, GLA_W - gla_n)))
    return parts


def _token_mixer(h, seq, ln1_g, w_in, gla_w_a2, gla_b_a, gla_norm_g, swa_sinks, ret_norm_g, w_branch, w_out,
                 rope):
    z_gla, z_swa, z_moba, z_ret, z_gates = _in_proj(h, ln1_g.reshape(1, D_MODEL), _split_w_in(w_in))
    wa = jnp.zeros((128, 128), F32).at[:GLA_LOWRANK].set(gla_w_a2)
    o_gla = _gla(z_gla, wa, gla_b_a.reshape(1, 128), jnp.tile(gla_norm_g, GLA_HEADS).reshape(1, 256), seq)
    o_swa = _swa(z_swa, swa_sinks, seq)
    o_moba = _moba(z_moba, seq)
    o_ret = _retention(z_ret, *rope, jnp.tile(ret_norm_g, RET_HEADS).reshape(1, 256), seq)
    return _merge(h, (o_gla, o_swa, o_moba, o_ret), z_gates, w_branch.astype(BF16), w_out.astype(BF16))


def kernel(x, ln1_g, w_in, gla_w_a2, gla_b_a, gla_norm_g, swa_sinks, ret_norm_g, w_branch, w_out, ln2_g,
           w_group, b_group, w_expert, b_expert, w1, w3, w2, final_g):
    batch, seq, _ = x.shape
    depth = w_in.shape[0]
    assert depth >= 1
    rope = _rope_tables(seq)
    h = x.reshape(batch * seq, D_MODEL)
    for l in range(depth):
        h = _token_mixer(h, seq, ln1_g[l], w_in[l], gla_w_a2[l], gla_b_a[l], gla_norm_g[l], swa_sinks[l],
                         ret_norm_g[l], w_branch[l], w_out[l], rope)
        h = _moe(h, ln2_g[l].reshape(1, D_MODEL), w_group[l], b_group[l], w_expert[l], b_expert[l],
                 w1, w3, w2, l, final_g.reshape(1, D_MODEL), normalize=(l == depth - 1))
    return h.reshape(batch, seq, D_MODEL)
```

```python
import functools
import math

import numpy as np
import jax
import jax.numpy as jnp
from jax import lax
from jax.experimental import pallas as pl
from jax.experimental.pallas import tpu as pltpu

F32 = jnp.float32
BF16 = jnp.bfloat16
I32 = jnp.int32
HIGHEST = lax.Precision.HIGHEST

D_MODEL = 1024
N_BRANCH = 4
NORM_EPS = 1e-6
NEG_INF = -1e30
BELOW_NEG_INF = -3e38

GLA_HEADS, GLA_DK, GLA_DV, GLA_LOWRANK, GLA_TAU, GLA_CHUNK = 4, 32, 64, 16, 16.0, 64
SWA_HEADS, SWA_KV_HEADS, SWA_HD, SWA_WINDOW = 4, 2, 64, 128
MOBA_HEADS, MOBA_HD, MOBA_BLOCK, MOBA_TOPK = 4, 64, 256, 3
RET_HEADS, RET_DK, RET_DV, RET_CHUNK, RET_ROPE_BASE = 4, 32, 64, 128, 10000.0
N_GROUPS, EXPERTS_PER_GROUP, EXPERT_TOPK, EXPERT_HIDDEN = 4, 8, 2, 256
MOE_BLOCK = 512
N_EXPERTS = N_GROUPS * EXPERTS_PER_GROUP

GLA_W = 896
SWA_W = 512
MOBA_W = 768
RET_W = 768
GATE_W = N_BRANCH * D_MODEL

LANES = 128
EXPERT_LANE0 = 32

VMEM_LIMIT = 56 * 1024 * 1024

TOK_TILE = 512
ROW_TILE = 256
DISPATCH_TILE = 512


def _params(*sem):
    return pltpu.CompilerParams(dimension_semantics=sem, vmem_limit_bytes=VMEM_LIMIT)


def _dot(a, b, precision=None):
    return jnp.dot(a, b, preferred_element_type=F32, precision=precision)


def _dot_nt(a, b, precision=None):
    return lax.dot_general(a, b, (((1,), (1,)), ((), ())), preferred_element_type=F32, precision=precision)


def _dot_tn(a, b, precision=None):
    return lax.dot_general(a, b, (((0,), (0,)), ((), ())), preferred_element_type=F32, precision=precision)


def _split_bf16(x, parts):
    out = []
    for _ in range(parts - 1):
        piece = x.astype(BF16)
        out.append(piece)
        x = x - piece.astype(F32)
    out.append(x.astype(BF16))
    return out


def _dot_f32_by_bf16(x, m):
    return sum(_dot(piece, m) for piece in _split_bf16(x, 3))


def _dot_bf16_by_f32(m, x):
    return sum(_dot(m, piece) for piece in _split_bf16(x, 3))


def _iota(shape, dim):
    return lax.broadcasted_iota(I32, shape, dim)


def _sigmoid(x):
    return 0.5 * jnp.tanh(0.5 * x) + 0.5


def _rms(x, g):
    ms = jnp.mean(x * x, axis=-1, keepdims=True)
    return x * lax.rsqrt(ms + NORM_EPS) * g


def _full(shape):
    return pl.BlockSpec(shape, lambda *_: (0,) * len(shape))


def _in_proj_kernel(h_ref, g_ref, w0, w1, w2, w3, w4, o0, o1, o2, o3, o4):
    u = _rms(h_ref[...], g_ref[...]).astype(BF16)
    for w, o in ((w0, o0), (w1, o1), (w2, o2), (w3, o3), (w4, o4)):
        o[...] = _dot(u, w[...]).astype(BF16)


def _in_proj(h, g, ws):
    T = h.shape[0]
    tm = TOK_TILE
    widths = [w.shape[1] for w in ws]
    return pl.pallas_call(
        _in_proj_kernel,
        grid=(T // tm,),
        in_specs=[pl.BlockSpec((tm, D_MODEL), lambda i: (i, 0)), _full((1, D_MODEL))]
        + [pl.BlockSpec((D_MODEL, n), lambda i: (0, 0), pipeline_mode=pl.Buffered(1)) for n in widths],
        out_specs=[pl.BlockSpec((tm, n), lambda i: (i, 0)) for n in widths],
        out_shape=[jax.ShapeDtypeStruct((T, n), BF16) for n in widths],
        compiler_params=_params("parallel"),
        name="in_proj",
    )(h, g, *ws)


def _head_select(x, lane_head, n_heads, rows):
    out = None
    for hh in range(n_heads):
        term = jnp.where(lane_head == hh, x[hh * rows:(hh + 1) * rows], 0.0)
        out = term if out is None else out + term
    return out


def _head_norm_gate(o, gate_in, g):
    n = o.shape[1]
    same_head = (_iota((n, n), 0) >> 6) == (_iota((n, n), 1) >> 6)
    ms = _dot_f32_by_bf16(o * o, same_head.astype(BF16)) * (1.0 / 64.0)
    return o * lax.rsqrt(ms + NORM_EPS) * g * (gate_in * _sigmoid(gate_in))


def _gla_kernel(z_ref, wa_ref, ba_ref, g_ref, o_ref, st_ref, oacc_ref, att_ref, kv_ref, *, tiles_per_seq):
    C = GLA_CHUNK
    tt = z_ref.shape[0]

    @pl.when(pl.program_id(0) % tiles_per_seq == 0)
    def _():
        st_ref[...] = jnp.zeros_like(st_ref)

    q = z_ref[:, 0:128].astype(F32)
    k = z_ref[:, 128:256].astype(F32)
    r = z_ref[:, 512:768].astype(F32)
    pre = _dot_bf16_by_f32(z_ref[:, 768:896], wa_ref[...]) + ba_ref[...]
    log_a = (jnp.minimum(pre, 0.0) - jnp.log(1.0 + jnp.exp(-jnp.abs(pre)))) * (1.0 / GLA_TAU)

    lower = (_iota((C, C), 1) <= _iota((C, C), 0)).astype(BF16)
    b = jnp.concatenate([_dot_bf16_by_f32(lower, log_a[c * C:(c + 1) * C]) for c in range(tt // C)], axis=0)
    b_last = jnp.broadcast_to(b.reshape(tt // C, C, 128)[:, C - 1:C, :], (tt // C, C, 128)).reshape(tt, 128)
    q_dec = (q * (GLA_DK ** -0.5) * jnp.exp(b)).astype(BF16)
    k_dec = (k * jnp.exp(-b)).astype(BF16)
    k_end = (k * jnp.exp(b_last - b)).astype(BF16)
    decay = jnp.exp(b_last)

    qk_head = _iota((C, 128), 1) >> 5
    v_head = _iota((C, 256), 1) >> 6
    causal = _iota((4 * C, C), 1) <= (_iota((4 * C, C), 0) & (C - 1))
    state_mask = (_iota((256, 128), 0) >> 6) == (_iota((256, 128), 1) >> 5)

    for c in range(tt // C):
        sl = slice(c * C, (c + 1) * C)
        q_stack = jnp.concatenate([jnp.where(qk_head == hh, q_dec[sl], 0) for hh in range(GLA_HEADS)], axis=0)
        att_ref[c] = jnp.where(causal, _dot_nt(q_stack, k_dec[sl]), 0.0).astype(BF16)
        kv_ref[c] = jnp.where(state_mask, _dot_tn(z_ref[sl, 256:512], k_end[sl]), 0.0)
    state = st_ref[...]
    for c in range(tt // C):
        sl = slice(c * C, (c + 1) * C)
        intra = _head_select(_dot(att_ref[c], z_ref[sl, 256:512]), v_head, GLA_HEADS, C)
        inter = _dot_nt(q_dec[sl], state.astype(BF16))
        oacc_ref[sl, :] = intra + inter
        state = state * decay[c * C:c * C + 1] + kv_ref[c]
    st_ref[...] = state

    o_ref[...] = _head_norm_gate(oacc_ref[...], r, g_ref[...]).astype(BF16)


def _gla(z, wa, ba, g, seq):
    T = z.shape[0]
    tt = TOK_TILE
    return pl.pallas_call(
        functools.partial(_gla_kernel, tiles_per_seq=seq // tt),
        grid=(T // tt,),
        in_specs=[pl.BlockSpec((tt, GLA_W), lambda i: (i, 0)), _full((128, 128)), _full((1, 128)),
                  _full((1, 256))],
        out_specs=pl.BlockSpec((tt, 256), lambda i: (i, 0)),
        out_shape=jax.ShapeDtypeStruct((T, 256), BF16),
        scratch_shapes=[pltpu.VMEM((256, 128), F32), pltpu.VMEM((tt, 256), F32),
                        pltpu.VMEM((tt // GLA_CHUNK, GLA_HEADS * GLA_CHUNK, GLA_CHUNK), BF16),
                        pltpu.VMEM((tt // GLA_CHUNK, 256, 128), F32)],
        compiler_params=_params("arbitrary"),
        name="gla_mixer",
    )(z, wa, ba, g)


_RET_LOG_GAMMA = [math.log1p(-(2.0 ** (-5.0 - hh))) for hh in range(RET_HEADS)]


def _by_head(head, values):
    out = jnp.full(head.shape, values[-1], F32)
    for hh in range(len(values) - 2, -1, -1):
        out = jnp.where(head == hh, values[hh], out)
    return out


def _ret_kernel(z_ref, cos_ref, sina_ref, sinb_ref, g_ref, o_ref, st_ref, oacc_ref, *, tiles_per_seq):
    C = RET_CHUNK
    tt = z_ref.shape[0]

    @pl.when(pl.program_id(0) % tiles_per_seq == 0)
    def _():
        st_ref[...] = jnp.zeros_like(st_ref)

    cos, sina, sinb = cos_ref[...], sina_ref[...], sinb_ref[...]

    def rotate(t):
        return t * cos + pltpu.roll(t, 112, 1) * sina + pltpu.roll(t, 16, 1) * sinb

    q_rot = rotate(z_ref[:, 0:128].astype(F32))
    k_rot = rotate(z_ref[:, 128:256].astype(F32)) * (RET_DK ** -0.5)
    gate_in = z_ref[:, 512:768].astype(F32)

    qk_head = _iota((C, 128), 1) >> 5
    v_head = _iota((C, 256), 1) >> 6
    lg_qk = _by_head(_iota((1, 128), 1) >> 5, _RET_LOG_GAMMA)
    lg_v = _by_head(_iota((1, 256), 1) >> 6, _RET_LOG_GAMMA)
    pos_qk = _iota((C, 128), 0).astype(F32)
    pos_v = _iota((C, 256), 0).astype(F32)
    key_decay = jnp.exp((C - 1.0 - pos_qk) * lg_qk)
    query_decay = jnp.exp((pos_v + 1.0) * lg_v)
    chunk_decay = jnp.exp(float(C) * lg_qk)
    srow = _iota((4 * C, C), 0)
    rel = ((srow & (C - 1)) - _iota((4 * C, C), 1)).astype(F32)
    decay_mat = jnp.where(rel >= 0, jnp.exp(jnp.maximum(rel, 0.0) * _by_head(srow >> 7, _RET_LOG_GAMMA)), 0.0)
    state_mask = (_iota((256, 128), 0) >> 6) == (_iota((256, 128), 1) >> 5)

    q_bf = q_rot.astype(BF16)
    k_bf = k_rot.astype(BF16)
    state = st_ref[...]
    for c in range(tt // C):
        sl = slice(c * C, (c + 1) * C)
        qc = q_bf[sl]
        v = z_ref[sl, 256:512]
        q_stack = jnp.concatenate([jnp.where(qk_head == hh, qc, 0) for hh in range(RET_HEADS)], axis=0)
        att = _dot_nt(q_stack, k_bf[sl]) * decay_mat
        intra = _head_select(_dot(att.astype(BF16), v), v_head, RET_HEADS, C)
        inter = _dot_nt(qc, state.astype(BF16)) * query_decay
        oacc_ref[sl, :] = intra + inter
        kv = _dot_tn(v, (k_rot[sl] * key_decay).astype(BF16))
        state = state * chunk_decay + jnp.where(state_mask, kv, 0.0)
    st_ref[...] = state

    o_ref[...] = _head_norm_gate(oacc_ref[...], gate_in, g_ref[...]).astype(BF16)


def _retention(z, cos, sina, sinb, g, seq):
    T = z.shape[0]
    tt = TOK_TILE
    tps = seq // tt
    pos_spec = pl.BlockSpec((tt, 128), lambda i: (i % tps, 0))
    return pl.pallas_call(
        functools.partial(_ret_kernel, tiles_per_seq=tps),
        grid=(T // tt,),
        in_specs=[pl.BlockSpec((tt, RET_W), lambda i: (i, 0)), pos_spec, pos_spec, pos_spec, _full((1, 256))],
        out_specs=pl.BlockSpec((tt, 256), lambda i: (i, 0)),
        out_shape=jax.ShapeDtypeStruct((T, 256), BF16),
        scratch_shapes=[pltpu.VMEM((256, 128), F32), pltpu.VMEM((tt, 256), F32)],
        compiler_params=_params("arbitrary"),
        name="retention_mixer",
    )(z, cos, sina, sinb, g)


def _rope_tables(seq):
    half = RET_DK // 2
    pos = jnp.arange(seq, dtype=F32)
    inv_freq = RET_ROPE_BASE ** (-jnp.arange(half, dtype=F32) * 2.0 / RET_DK)
    ang = pos[:, None] * inv_freq[None, :]
    cos = jnp.tile(jnp.cos(ang), (1, LANES // half))
    sin = jnp.tile(jnp.sin(ang), (1, LANES // half))
    first_half = (jnp.arange(LANES) % RET_DK) < half
    return cos, jnp.where(first_half, -sin, 0.0), jnp.where(first_half, 0.0, sin)


def _swa_kernel(sink_ref, cur_ref, prev_ref, o_ref, s_ref, *, tiles_per_seq):
    W = SWA_WINDOW
    HD = SWA_HD
    tt = cur_ref.shape[0]
    first_key = jnp.where(pl.program_id(0) % tiles_per_seq == 0, W, 0)
    sj = _iota((2 * W, W), 0)
    qi = _iota((2 * W, W), 1)
    in_window = (sj > qi) & (sj <= qi + W)
    kv0 = SWA_HEADS * HD
    no_head = jnp.zeros((HD, W), BF16)
    n_blocks = tt // W

    def band(c, col0):
        prev = prev_ref[:, col0:col0 + 128] if c == 0 else cur_ref[(c - 1) * W:c * W, col0:col0 + 128]
        return jnp.concatenate([prev, cur_ref[c * W:(c + 1) * W, col0:col0 + 128]], axis=0)

    for c in range(n_blocks):
        mask = in_window & (sj >= first_key) if c == 0 else in_window
        k_band = band(c, kv0)
        q_t = cur_ref[c * W:(c + 1) * W, 0:kv0].astype(F32).T.astype(BF16)
        for hh in range(SWA_HEADS):
            kk = hh // (SWA_HEADS // SWA_KV_HEADS)
            qh_t = q_t[hh * HD:(hh + 1) * HD]
            q_on_kv = jnp.concatenate([qh_t, no_head] if kk == 0 else [no_head, qh_t], axis=0)
            s = _dot(k_band, q_on_kv) * (HD ** -0.5)
            s_ref[c, hh] = jnp.where(mask, s, NEG_INF)
    for c in range(n_blocks):
        v_t = band(c, kv0 + 128).astype(F32).T.astype(BF16)
        outs = []
        for hh in range(SWA_HEADS):
            kk = hh // (SWA_HEADS // SWA_KV_HEADS)
            s = s_ref[c, hh]
            sink = sink_ref[hh]
            m = jnp.maximum(jnp.max(s, axis=0, keepdims=True), sink)
            p = jnp.exp(s - m)
            denom = jnp.sum(p, axis=0, keepdims=True) + jnp.exp(sink - m)
            outs.append(_dot(v_t[kk * HD:(kk + 1) * HD], p.astype(BF16)) / denom)
        o_ref[c * W:(c + 1) * W, :] = jnp.concatenate(outs, axis=0).T.astype(BF16)


def _swa(z, sinks, seq):
    T = z.shape[0]
    tt = TOK_TILE
    per = tt // SWA_WINDOW
    return pl.pallas_call(
        functools.partial(_swa_kernel, tiles_per_seq=seq // tt),
        grid=(T // tt,),
        in_specs=[pl.BlockSpec(memory_space=pltpu.SMEM),
                  pl.BlockSpec((tt, SWA_W), lambda i: (i, 0)),
                  pl.BlockSpec((SWA_WINDOW, SWA_W), lambda i: (jnp.maximum(i * per - 1, 0), 0))],
        out_specs=pl.BlockSpec((tt, 256), lambda i: (i, 0)),
        out_shape=jax.ShapeDtypeStruct((T, 256), BF16),
        scratch_shapes=[pltpu.VMEM((per, SWA_HEADS, 2 * SWA_WINDOW, SWA_WINDOW), F32)],
        compiler_params=_params("parallel"),
        name="swa_mixer",
    )(sinks, z, z)


_KMEAN_BLOCKS = 8


def _kmean_kernel(k_ref, o_ref):
    k = k_ref[...].astype(F32).reshape(_KMEAN_BLOCKS, MOBA_BLOCK, 256)
    o_ref[...] = jnp.mean(k, axis=1)


def _moba_kmean(z):
    T = z.shape[0]
    rows = _KMEAN_BLOCKS * MOBA_BLOCK
    return pl.pallas_call(
        _kmean_kernel,
        grid=(T // rows,),
        in_specs=[pl.BlockSpec((rows, 256), lambda i: (i, 1))],
        out_specs=pl.BlockSpec((_KMEAN_BLOCKS, 256), lambda i: (i, 0)),
        out_shape=jax.ShapeDtypeStruct((T // MOBA_BLOCK, 256), F32),
        compiler_params=_params("parallel"),
        name="moba_kmean",
    )(z)


MOBA_VROWS = 80


def _moba_prep_kernel(z_ref, km_ref, qt_out, k_out, vt_out):
    HD = MOBA_HD
    tq = z_ref.shape[0]
    nb = km_ref.shape[0]
    qb = pl.program_id(0) % nb
    blk_i = _iota((nb, tq), 0)
    blk = blk_i.astype(F32)
    pad = LANES - HD - nb
    this_block_lanes = (_iota((tq, nb), 1) == qb).astype(BF16)
    ones_row = (_iota((MOBA_VROWS - HD, tq), 0) == 0).astype(BF16)
    for hh in range(MOBA_HEADS):
        cols = slice(hh * HD, (hh + 1) * HD)
        q_t = z_ref[:, cols].astype(F32).T
        gate = _dot(km_ref[:, cols], q_t, precision=HIGHEST)
        gate = jnp.where(blk_i < qb, gate, NEG_INF)
        keep = jnp.where(blk_i == qb, 1.0, 0.0)
        for _ in range(MOBA_TOPK):
            best = jnp.max(gate, axis=0, keepdims=True)
            first = jnp.min(jnp.where(gate == best, blk, float(nb)), axis=0, keepdims=True)
            hit = blk == first
            keep = keep + jnp.where(hit, jnp.where(best > 0.5 * NEG_INF, 1.0, 0.0), 0.0)
            gate = jnp.where(hit, BELOW_NEG_INF, gate)
        bias = ((keep - 1.0) * (-NEG_INF)).astype(BF16)
        qt_out[hh, 0] = jnp.concatenate([(q_t * (HD ** -0.5)).astype(BF16), bias, jnp.zeros((pad, tq), BF16)],
                                        axis=0)
        k_out[hh] = jnp.concatenate([z_ref[:, 256 + hh * HD:256 + (hh + 1) * HD], this_block_lanes,
                                     jnp.zeros((tq, pad), BF16)], axis=1)
        v_t = z_ref[:, 512 + hh * HD:512 + (hh + 1) * HD].astype(F32).T.astype(BF16)
        vt_out[hh, 0] = jnp.concatenate([v_t, ones_row], axis=0)


def _moba_prep(z, kmean, seq):
    T = z.shape[0]
    nb = seq // MOBA_BLOCK
    assert MOBA_HD + nb <= LANES
    tq = MOBA_BLOCK
    H = MOBA_HEADS
    def per_block_t(rows):
        return pl.BlockSpec((H, 1, rows, tq), lambda i: (0, i, 0, 0))

    return pl.pallas_call(
        _moba_prep_kernel,
        grid=(T // tq,),
        in_specs=[pl.BlockSpec((tq, MOBA_W), lambda i: (i, 0)), pl.BlockSpec((nb, 256), lambda i: (i // nb, 0))],
        out_specs=[per_block_t(LANES), pl.BlockSpec((H, tq, LANES), lambda i: (0, i, 0)),
                   per_block_t(MOBA_VROWS)],
        out_shape=[jax.ShapeDtypeStruct((H, T // tq, LANES, tq), BF16), jax.ShapeDtypeStruct((H, T, LANES), BF16),
                   jax.ShapeDtypeStruct((H, T // tq, MOBA_VROWS, tq), BF16)],
        compiler_params=_params("parallel"),
        name="moba_prep",
    )(z, kmean)


MOBA_QBLOCKS = 4
MOBA_KBLOCKS = 2
MOBA_DIAGONALS = MOBA_QBLOCKS // MOBA_KBLOCKS


def _moba_kernel(qidx_ref, kidx_ref, kind_ref, qt_ref, k_ref, vt_ref, o_ref, m_ref, acc_ref, s_ref):
    HD = MOBA_HD
    B = MOBA_BLOCK
    step = pl.program_id(1)
    kind = kind_ref[step]
    is_last = (step + 1 == pl.num_programs(1)) | (qidx_ref[jnp.minimum(step + 1, pl.num_programs(1) - 1)]
                                                  != qidx_ref[step])

    def attend(diagonal):
        causal = _iota((B, B), 0) <= _iota((B, B), 1)

        def key_blocks_of(qi):
            if diagonal is None:
                return list(range(MOBA_KBLOCKS))
            return [c for c in range(MOBA_KBLOCKS) if MOBA_KBLOCKS * diagonal + c <= qi]

        for hh in range(MOBA_HEADS):
            for qi in range(MOBA_QBLOCKS):
                for c in key_blocks_of(qi):
                    s = _dot(k_ref[hh, c * B:(c + 1) * B, :], qt_ref[hh, qi])
                    if diagonal is not None and MOBA_KBLOCKS * diagonal + c == qi:
                        s = jnp.where(causal, s, NEG_INF)
                    s_ref[hh, qi, c] = s
        for hh in range(MOBA_HEADS):
            for qi in range(MOBA_QBLOCKS):
                key_blocks = key_blocks_of(qi)
                if not key_blocks:
                    continue
                scores = [s_ref[hh, qi, c] for c in key_blocks]
                m_new = functools.reduce(jnp.maximum, [jnp.max(s, axis=0, keepdims=True) for s in scores])
                first_visit = diagonal == 0
                if not first_visit:
                    m_old = m_ref[hh, qi]
                    m_new = jnp.maximum(m_old, m_new)
                pv = None
                for c, s in zip(key_blocks, scores):
                    term = _dot(vt_ref[hh, c], jnp.exp(s - m_new).astype(BF16))
                    pv = term if pv is None else pv + term
                acc_ref[hh, qi] = pv if first_visit else jnp.exp(m_old - m_new) * acc_ref[hh, qi] + pv
                m_ref[hh, qi] = m_new

    @pl.when(kind == 0)
    def _():
        attend(None)

    for d in range(MOBA_DIAGONALS):
        @pl.when(kind == 1 + d)
        def _(d=d):
            attend(d)

    @pl.when(is_last)
    def _():
        pad = jnp.zeros((LANES - MOBA_VROWS, B), F32)
        for qi in range(MOBA_QBLOCKS):
            outs = []
            for hh in range(MOBA_HEADS):
                a = jnp.concatenate([acc_ref[hh, qi], pad], axis=0).T
                outs.append(a[:, 0:HD] / a[:, HD:HD + 1])
            o_ref[qi * B:(qi + 1) * B, :] = jnp.concatenate(outs, axis=1).astype(BF16)


def _moba(z, seq):
    T = z.shape[0]
    nb = seq // MOBA_BLOCK
    assert nb % MOBA_QBLOCKS == 0
    q_tiles = nb // MOBA_QBLOCKS
    k_steps = nb // MOBA_KBLOCKS
    D = MOBA_DIAGONALS
    batch = T // seq
    qt_aug, k_aug, vt_aug = _moba_prep(z, _moba_kmean(z), seq)
    qidx = np.concatenate([np.full(D * t + D, t) for t in range(q_tiles)]).astype(np.int32)
    kidx = np.concatenate([np.concatenate([D * t + np.arange(D), np.arange(D * t)])
                           for t in range(q_tiles)]).astype(np.int32)
    kind = np.concatenate([np.concatenate([1 + np.arange(D), np.zeros(D * t, np.int64)])
                           for t in range(q_tiles)]).astype(np.int32)
    tq = MOBA_QBLOCKS * MOBA_BLOCK
    tk = MOBA_KBLOCKS * MOBA_BLOCK
    H = MOBA_HEADS
    grid_spec = pltpu.PrefetchScalarGridSpec(
        num_scalar_prefetch=3,
        grid=(batch, len(qidx)),
        in_specs=[pl.BlockSpec((H, MOBA_QBLOCKS, LANES, MOBA_BLOCK),
                               lambda b, s, qi, ki, kd: (0, b * q_tiles + qi[s], 0, 0)),
                  pl.BlockSpec((H, tk, LANES), lambda b, s, qi, ki, kd: (0, b * k_steps + ki[s], 0)),
                  pl.BlockSpec((H, MOBA_KBLOCKS, MOBA_VROWS, MOBA_BLOCK),
                               lambda b, s, qi, ki, kd: (0, b * k_steps + ki[s], 0, 0))],
        out_specs=pl.BlockSpec((tq, 256), lambda b, s, qi, ki, kd: (b * q_tiles + qi[s], 0)),
        scratch_shapes=[pltpu.VMEM((H, MOBA_QBLOCKS, 1, MOBA_BLOCK), F32),
                        pltpu.VMEM((H, MOBA_QBLOCKS, MOBA_VROWS, MOBA_BLOCK), F32),
                        pltpu.VMEM((H, MOBA_QBLOCKS, MOBA_KBLOCKS, MOBA_BLOCK, MOBA_BLOCK), F32)],
    )
    return pl.pallas_call(
        _moba_kernel,
        grid_spec=grid_spec,
        out_shape=jax.ShapeDtypeStruct((T, 256), BF16),
        compiler_params=_params("parallel", "arbitrary"),
        name="moba_mixer",
    )(jnp.asarray(qidx), jnp.asarray(kidx), jnp.asarray(kind), qt_aug, k_aug, vt_aug)


def _merge_kernel(h_ref, o0, o1, o2, o3, gates_ref, wb_ref, wout_ref, out_ref):
    merged = None
    for i, o in enumerate((o0, o1, o2, o3)):
        gate = _sigmoid(gates_ref[:, i * D_MODEL:(i + 1) * D_MODEL].astype(F32))
        term = gate * _dot(o[...], wb_ref[i])
        merged = term if merged is None else merged + term
    out_ref[...] = h_ref[...] + _dot(merged.astype(BF16), wout_ref[...])


def _merge(h, branches, gates, wb, wout):
    T = h.shape[0]
    tm = TOK_TILE
    return pl.pallas_call(
        _merge_kernel,
        grid=(T // tm,),
        in_specs=[pl.BlockSpec((tm, D_MODEL), lambda i: (i, 0))]
        + [pl.BlockSpec((tm, 256), lambda i: (i, 0))] * 4
        + [pl.BlockSpec((tm, GATE_W), lambda i: (i, 0)),
           pl.BlockSpec((N_BRANCH, 256, D_MODEL), lambda i: (0, 0, 0), pipeline_mode=pl.Buffered(1)),
           pl.BlockSpec((D_MODEL, D_MODEL), lambda i: (0, 0), pipeline_mode=pl.Buffered(1))],
        out_specs=pl.BlockSpec((tm, D_MODEL), lambda i: (i, 0)),
        out_shape=jax.ShapeDtypeStruct((T, D_MODEL), F32),
        compiler_params=_params("parallel"),
        name="merge_out_proj",
    )(h, *branches, gates, wb, wout)


ROW_TILES = D_MODEL // LANES


def _row_tile_spec(rows, index_map):
    return pl.BlockSpec((rows * ROW_TILES, LANES), index_map)


def _row_slab(ref, j):
    return ref[pl.ds(j, ref.shape[0] // ROW_TILES, stride=ROW_TILES), :]


def _store_row_tiles(ref, x):
    for j in range(ROW_TILES):
        ref[pl.ds(j, x.shape[0], stride=ROW_TILES), :] = x[:, j * LANES:(j + 1) * LANES]


def _load_row_tiles(ref):
    return jnp.concatenate([_row_slab(ref, j) for j in range(ROW_TILES)], axis=1)


def _lane_first(mask, lane):
    return jnp.min(jnp.where(mask, lane, float(LANES)), axis=-1, keepdims=True)


def _router_kernel(h_ref, g_ref, wr_ref, br_ref, u_ref, route_ref, route_t_ref, count_ref, carry_ref):
    tm = h_ref.shape[0]

    @pl.when(pl.program_id(0) == 0)
    def _():
        carry_ref[...] = jnp.zeros_like(carry_ref)

    u = _rms(h_ref[...], g_ref[...])
    _store_row_tiles(u_ref, u)
    u_hi, u_lo = _split_bf16(u, 2)
    w_hi, w_lo = _split_bf16(wr_ref[...], 2)
    logits = _dot(u_hi, w_hi) + (_dot(u_lo, w_hi) + _dot(u_hi, w_lo)) + br_ref[...]
    lane_i = _iota((tm, LANES), 1)
    lane = lane_i.astype(F32)
    lane_group = ((lane_i - EXPERT_LANE0) >> 3).astype(F32)

    grp = jnp.where(lane_i < N_GROUPS, logits, BELOW_NEG_INF)
    grp_e = jnp.exp(grp - jnp.max(grp, axis=-1, keepdims=True))
    p_group = grp_e / jnp.sum(grp_e, axis=-1, keepdims=True)
    g_w = jnp.max(p_group, axis=-1, keepdims=True)
    g_sel = _lane_first(p_group == g_w, lane)

    in_group = (lane_i >= EXPERT_LANE0) & (lane_group == g_sel)
    el = jnp.where(in_group, logits, BELOW_NEG_INF)
    top1 = jnp.max(el, axis=-1, keepdims=True)
    lane1 = _lane_first(el == top1, lane)
    el = jnp.where(lane == lane1, BELOW_NEG_INF, el)
    top2 = jnp.max(el, axis=-1, keepdims=True)
    lane2 = _lane_first(el == top2, lane)
    e2 = jnp.exp(top2 - top1)
    w1 = g_w / (1.0 + e2)
    w2 = g_w * e2 / (1.0 + e2)

    onehot = ((lane == lane1) | (lane == lane2)).astype(BF16)
    before = (_iota((tm, tm), 1) < _iota((tm, tm), 0)).astype(BF16)
    seen = _dot(before, onehot) + carry_ref[...]
    rank1 = jnp.sum(jnp.where(lane == lane1, seen, 0.0), axis=-1, keepdims=True)
    rank2 = jnp.sum(jnp.where(lane == lane2, seen, 0.0), axis=-1, keepdims=True)
    carry_ref[...] += jnp.sum(onehot.astype(F32), axis=0, keepdims=True)
    count_ref[...] = carry_ref[...]

    out = jnp.zeros((tm, LANES), F32)
    for idx, val in enumerate((lane1 - EXPERT_LANE0, lane2 - EXPERT_LANE0, w1, w2, rank1, rank2)):
        out = jnp.where(lane_i == idx, val, out)
    route_ref[...] = out
    route_t_ref[...] = out.T[0:8, :]


def _router(h, g, wr, br):
    T = h.shape[0]
    tm = TOK_TILE
    return pl.pallas_call(
        _router_kernel,
        grid=(T // tm,),
        in_specs=[pl.BlockSpec((tm, D_MODEL), lambda i: (i, 0)), _full((1, D_MODEL)),
                  _full((D_MODEL, LANES)), _full((1, LANES))],
        out_specs=[_row_tile_spec(tm, lambda i: (i, 0)), pl.BlockSpec((tm, LANES), lambda i: (i, 0)),
                   pl.BlockSpec((8, tm), lambda i: (0, i)), _full((1, LANES))],
        out_shape=[jax.ShapeDtypeStruct((T * ROW_TILES, LANES), F32), jax.ShapeDtypeStruct((T, LANES), F32),
                   jax.ShapeDtypeStruct((8, T), F32), jax.ShapeDtypeStruct((1, LANES), F32)],
        scratch_shapes=[pltpu.VMEM((1, LANES), F32)],
        compiler_params=_params("arbitrary"),
        name="moe_router",
    )(h, g, wr, br)


ROW_UNROLL = 8


def _row_copy(src_ref, src_row, dst_ref, dst_row, sem):
    src = src_ref.at[pl.ds(pl.multiple_of(src_row * ROW_TILES, ROW_TILES), ROW_TILES)]
    dst = dst_ref.at[pl.ds(pl.multiple_of(dst_row * ROW_TILES, ROW_TILES), ROW_TILES)]
    return pltpu.make_async_copy(src, dst, sem)


def _for_each_row_pair(rows, body):
    def group(g, carry):
        for i in range(ROW_UNROLL):
            for k in range(EXPERT_TOPK):
                body(g * ROW_UNROLL + i, k)
        return carry

    lax.fori_loop(0, rows // ROW_UNROLL, group, 0)


def _dispatch_kernel(dest0_ref, dest1_ref, pstart_ref, pend_ref, u_ref, xs_ref, zero_ref, sem):
    tm = u_ref.shape[0] // ROW_TILES
    tile = pl.program_id(0)
    dests = (dest0_ref, dest1_ref)

    @pl.when(tile == 0)
    def _():
        zero_ref[...] = jnp.zeros_like(zero_ref)

        def zero_block(first_row):
            start = pl.multiple_of(first_row * ROW_TILES, ROW_TILES)
            fill = pltpu.make_async_copy(zero_ref, xs_ref.at[pl.ds(start, MOE_BLOCK * ROW_TILES)], sem)
            fill.start()
            fill.wait()

        def zero_tail(e, carry):
            @pl.when(pend_ref[e] > pstart_ref[e])
            def _():
                zero_block(pend_ref[e] - MOE_BLOCK)
            return carry

        def zero_unused(b, carry):
            zero_block(b * MOE_BLOCK)
            return carry

        lax.fori_loop(0, N_EXPERTS, zero_tail, 0)
        lax.fori_loop(pend_ref[N_EXPERTS - 1] // MOE_BLOCK, xs_ref.shape[0] // (MOE_BLOCK * ROW_TILES),
                      zero_unused, 0)

    def issue(r, k):
        _row_copy(u_ref, r, xs_ref, dests[k][tile * tm + r], sem).start(priority=k)

    _for_each_row_pair(tm, issue)
    for _ in range(EXPERT_TOPK):
        pltpu.make_async_copy(u_ref, xs_ref.at[pl.ds(0, tm * ROW_TILES)], sem).wait()


def _dispatch(dest0, dest1, pstart, pend, u, n_slots):
    T = u.shape[0] // ROW_TILES
    tm = DISPATCH_TILE
    grid_spec = pltpu.PrefetchScalarGridSpec(
        num_scalar_prefetch=4,
        grid=(T // tm,),
        in_specs=[_row_tile_spec(tm, lambda i, *_: (i, 0))],
        out_specs=pl.BlockSpec(memory_space=pl.ANY),
        scratch_shapes=[pltpu.VMEM((MOE_BLOCK * ROW_TILES, LANES), F32), pltpu.SemaphoreType.DMA(())],
    )
    return pl.pallas_call(
        _dispatch_kernel,
        grid_spec=grid_spec,
        out_shape=jax.ShapeDtypeStruct((n_slots * ROW_TILES, LANES), F32),
        compiler_params=_params("arbitrary"),
        name="moe_dispatch",
    )(dest0, dest1, pstart, pend, u)


def _expert_kernel(blk_e_ref, n_used_ref, x_ref, w1_ref, w3_ref, w2_ref, y_ref, w1b_ref, w3b_ref, w2b_ref):
    i = pl.program_id(0)
    used = i < n_used_ref[0]

    @pl.when((i == 0) | (blk_e_ref[i] != blk_e_ref[jnp.maximum(i - 1, 0)]))
    def _():
        w1b_ref[...] = w1_ref[0, 0].astype(BF16)
        w3b_ref[...] = w3_ref[0, 0].astype(BF16)
        w2b_ref[...] = w2_ref[0, 0].astype(BF16)

    @pl.when(used)
    def _():
        x = _load_row_tiles(x_ref).astype(BF16)
        a = _dot(x, w1b_ref[...])
        hidden = (a * _sigmoid(a)) * _dot(x, w3b_ref[...])
        _store_row_tiles(y_ref, _dot(hidden.astype(BF16), w2b_ref[...]))

    @pl.when(jnp.logical_not(used))
    def _():
        y_ref[...] = jnp.zeros_like(y_ref)


def _experts(blk_e, n_used, xs, w1, w3, w2, layer):
    n_slots = xs.shape[0] // ROW_TILES
    bm = MOE_BLOCK
    grid_spec = pltpu.PrefetchScalarGridSpec(
        num_scalar_prefetch=2,
        grid=(n_slots // bm,),
        in_specs=[_row_tile_spec(bm, lambda i, be, nu: (jnp.minimum(i, nu[0] - 1), 0)),
                  pl.BlockSpec((1, 1, D_MODEL, EXPERT_HIDDEN), lambda i, be, nu: (layer, be[i], 0, 0)),
                  pl.BlockSpec((1, 1, D_MODEL, EXPERT_HIDDEN), lambda i, be, nu: (layer, be[i], 0, 0)),
                  pl.BlockSpec((1, 1, EXPERT_HIDDEN, D_MODEL), lambda i, be, nu: (layer, be[i], 0, 0))],
        out_specs=_row_tile_spec(bm, lambda i, be, nu: (i, 0)),
        scratch_shapes=[pltpu.VMEM((D_MODEL, EXPERT_HIDDEN), BF16), pltpu.VMEM((D_MODEL, EXPERT_HIDDEN), BF16),
                        pltpu.VMEM((EXPERT_HIDDEN, D_MODEL), BF16)],
    )
    return pl.pallas_call(
        _expert_kernel,
        grid_spec=grid_spec,
        out_shape=jax.ShapeDtypeStruct((n_slots * ROW_TILES, LANES), F32),
        compiler_params=_params("arbitrary"),
        name="moe_experts",
    )(blk_e, n_used, xs, w1, w3, w2)


def _combine_kernel(dest0_ref, dest1_ref, h_ref, route_ref, g_ref, ys_ref, out_ref, y_ref, sem, *, normalize):
    tm = h_ref.shape[0]
    step = pl.program_id(0)
    slot = step % 2

    dests = (dest0_ref, dest1_ref)

    def gather_tile(tile, buf):
        def issue(r, k):
            _row_copy(ys_ref, dests[k][tile * tm + r], y_ref.at[buf, k], r, sem.at[buf]).start(priority=k)

        _for_each_row_pair(tm, issue)

    @pl.when(step == 0)
    def _():
        gather_tile(0, 0)

    @pl.when(step + 1 < pl.num_programs(0))
    def _():
        gather_tile(step + 1, 1 - slot)

    for k in range(EXPERT_TOPK):
        pltpu.make_async_copy(ys_ref.at[pl.ds(0, tm * ROW_TILES)], y_ref.at[slot, k], sem.at[slot]).wait()
    w0 = route_ref[:, 2:3]
    w1 = route_ref[:, 3:4]
    for j in range(ROW_TILES):
        cols = slice(j * LANES, (j + 1) * LANES)
        out_ref[:, cols] = (h_ref[:, cols] + w0 * _row_slab(y_ref.at[slot, 0], j)
                            + w1 * _row_slab(y_ref.at[slot, 1], j))
    if normalize:
        out_ref[...] = _rms(out_ref[...], g_ref[...])


def _combine(dest0, dest1, h, route, ys, final_g, normalize):
    T = h.shape[0]
    tm = ROW_TILE
    grid_spec = pltpu.PrefetchScalarGridSpec(
        num_scalar_prefetch=2,
        grid=(T // tm,),
        in_specs=[pl.BlockSpec((tm, D_MODEL), lambda i, *_: (i, 0)),
                  pl.BlockSpec((tm, LANES), lambda i, *_: (i, 0)),
                  _full((1, D_MODEL)),
                  pl.BlockSpec(memory_space=pl.ANY)],
        out_specs=pl.BlockSpec((tm, D_MODEL), lambda i, *_: (i, 0)),
        scratch_shapes=[pltpu.VMEM((2, EXPERT_TOPK, tm * ROW_TILES, LANES), F32),
                        pltpu.SemaphoreType.DMA((2,))],
    )
    return pl.pallas_call(
        functools.partial(_combine_kernel, normalize=normalize),
        grid_spec=grid_spec,
        out_shape=jax.ShapeDtypeStruct((T, D_MODEL), F32),
        compiler_params=_params("arbitrary"),
        name="moe_combine",
    )(dest0, dest1, h, route, final_g, ys)


def _moe(h, g, w_group, b_group, w_expert, b_expert, w1, w3, w2, layer, final_g, normalize):
    T = h.shape[0]
    wr = jnp.zeros((D_MODEL, LANES), F32)
    wr = wr.at[:, :N_GROUPS].set(w_group).at[:, EXPERT_LANE0:EXPERT_LANE0 + N_EXPERTS].set(w_expert)
    br = jnp.zeros((1, LANES), F32)
    br = br.at[0, :N_GROUPS].set(b_group).at[0, EXPERT_LANE0:EXPERT_LANE0 + N_EXPERTS].set(b_expert)
    u, route, route_t, count = _router(h, g, wr, br)

    n_blocks = (T * EXPERT_TOPK) // MOE_BLOCK + N_EXPERTS
    counts = count[0, EXPERT_LANE0:EXPERT_LANE0 + N_EXPERTS].astype(I32)
    padded = ((counts + MOE_BLOCK - 1) // MOE_BLOCK) * MOE_BLOCK
    pend = jnp.cumsum(padded)
    pstart = pend - padded
    expert_ids = jnp.arange(N_EXPERTS, dtype=F32)[:, None]

    def slot_of(expert_row, rank_row):
        segment_start = jnp.sum(jnp.where(expert_row[None, :] == expert_ids, pstart[:, None], 0), axis=0)
        return segment_start + rank_row.astype(I32)

    dest0 = slot_of(route_t[0], route_t[4])
    dest1 = slot_of(route_t[1], route_t[5])
    block_row0 = jnp.arange(n_blocks, dtype=I32) * MOE_BLOCK
    blk_e = jnp.minimum(jnp.sum((pend[None, :] <= block_row0[:, None]).astype(I32), axis=1), N_EXPERTS - 1)
    n_used = (pend[-1:] // MOE_BLOCK).astype(I32)

    xs = _dispatch(dest0, dest1, pstart, pend, u, n_blocks * MOE_BLOCK)
    ys = _experts(blk_e, n_used, xs, w1, w3, w2, layer)
    return _combine(dest0, dest1, h, route, ys, final_g, normalize)


def _split_w_in(w_in):
    gla_n = GLA_HEADS * (2 * GLA_DK + 2 * GLA_DV) + GLA_LOWRANK
    swa_n = (SWA_HEADS + 2 * SWA_KV_HEADS) * SWA_HD
    moba_n = 3 * MOBA_HEADS * MOBA_HD
    ret_n = RET_HEADS * (2 * RET_DK + 2 * RET_DV)
    offs = np.cumsum([0, gla_n, swa_n, moba_n, ret_n, GATE_W])
    parts = [w_in[:, offs[i]:offs[i + 1]].astype(BF16) for i in range(5)]
    parts[0] = jnp.pad(parts[0], ((0, 0), (0, GLA_W - gla_n)))
    return parts


def _token_mixer(h, seq, ln1_g, w_in, gla_w_a2, gla_b_a, gla_norm_g, swa_sinks, ret_norm_g, w_branch, w_out,
                 rope):
    z_gla, z_swa, z_moba, z_ret, z_gates = _in_proj(h, ln1_g.reshape(1, D_MODEL), _split_w_in(w_in))
    wa = jnp.zeros((128, 128), F32).at[:GLA_LOWRANK].set(gla_w_a2)
    o_gla = _gla(z_gla, wa, gla_b_a.reshape(1, 128), jnp.tile(gla_norm_g, GLA_HEADS).reshape(1, 256), seq)
    o_swa = _swa(z_swa, swa_sinks, seq)
    o_moba = _moba(z_moba, seq)
    o_ret = _retention(z_ret, *rope, jnp.tile(ret_norm_g, RET_HEADS).reshape(1, 256), seq)
    return _merge(h, (o_gla, o_swa, o_moba, o_ret), z_gates, w_branch.astype(BF16), w_out.astype(BF16))


def kernel(x, ln1_g, w_in, gla_w_a2, gla_b_a, gla_norm_g, swa_sinks, ret_norm_g, w_branch, w_out, ln2_g,
           w_group, b_group, w_expert, b_expert, w1, w3, w2, final_g):
    batch, seq, _ = x.shape
    depth = w_in.shape[0]
    assert depth >= 1
    rope = _rope_tables(seq)
    h = x.reshape(batch * seq, D_MODEL)
    for l in range(depth):
        h = _token_mixer(h, seq, ln1_g[l], w_in[l], gla_w_a2[l], gla_b_a[l], gla_norm_g[l], swa_sinks[l],
                         ret_norm_g[l], w_branch[l], w_out[l], rope)
        h = _moe(h, ln2_g[l].reshape(1, D_MODEL), w_group[l], b_group[l], w_expert[l], b_expert[l],
                 w1, w3, w2, l, final_g.reshape(1, D_MODEL), normalize=(l == depth - 1))
    return h.reshape(batch, seq, D_MODEL)
```

```python
import functools
import math

import numpy as np
import jax
import jax.numpy as jnp
from jax import lax
from jax.experimental import pallas as pl
from jax.experimental.pallas import tpu as pltpu

F32 = jnp.float32
BF16 = jnp.bfloat16
I32 = jnp.int32
HIGHEST = lax.Precision.HIGHEST

D_MODEL = 1024
N_BRANCH = 4
NORM_EPS = 1e-6
NEG_INF = -1e30
BELOW_NEG_INF = -3e38

GLA_HEADS, GLA_DK, GLA_DV, GLA_LOWRANK, GLA_TAU, GLA_CHUNK = 4, 32, 64, 16, 16.0, 64
SWA_HEADS, SWA_KV_HEADS, SWA_HD, SWA_WINDOW = 4, 2, 64, 128
MOBA_HEADS, MOBA_HD, MOBA_BLOCK, MOBA_TOPK = 4, 64, 256, 3
RET_HEADS, RET_DK, RET_DV, RET_CHUNK, RET_ROPE_BASE = 4, 32, 64, 128, 10000.0
N_GROUPS, EXPERTS_PER_GROUP, EXPERT_TOPK, EXPERT_HIDDEN = 4, 8, 2, 256
MOE_BLOCK = 512
N_EXPERTS = N_GROUPS * EXPERTS_PER_GROUP

GLA_W = 896
SWA_W = 512
MOBA_W = 768
RET_W = 768
GATE_W = N_BRANCH * D_MODEL

LANES = 128
EXPERT_LANE0 = 32

VMEM_LIMIT = 56 * 1024 * 1024

TOK_TILE = 512
ROW_TILE = 256
DISPATCH_TILE = 512


def _params(*sem):
    return pltpu.CompilerParams(dimension_semantics=sem, vmem_limit_bytes=VMEM_LIMIT)


def _dot(a, b, precision=None):
    return jnp.dot(a, b, preferred_element_type=F32, precision=precision)


def _dot_nt(a, b, precision=None):
    return lax.dot_general(a, b, (((1,), (1,)), ((), ())), preferred_element_type=F32, precision=precision)


def _dot_tn(a, b, precision=None):
    return lax.dot_general(a, b, (((0,), (0,)), ((), ())), preferred_element_type=F32, precision=precision)


def _split_bf16(x, parts):
    out = []
    for _ in range(parts - 1):
        piece = x.astype(BF16)
        out.append(piece)
        x = x - piece.astype(F32)
    out.append(x.astype(BF16))
    return out


def _dot_f32_by_bf16(x, m):
    return sum(_dot(piece, m) for piece in _split_bf16(x, 3))


def _dot_bf16_by_f32(m, x):
    return sum(_dot(m, piece) for piece in _split_bf16(x, 3))


def _iota(shape, dim):
    return lax.broadcasted_iota(I32, shape, dim)


def _sigmoid(x):
    return 0.5 * jnp.tanh(0.5 * x) + 0.5


def _rms(x, g):
    ms = jnp.mean(x * x, axis=-1, keepdims=True)
    return x * lax.rsqrt(ms + NORM_EPS) * g


def _full(shape):
    return pl.BlockSpec(shape, lambda *_: (0,) * len(shape))


def _in_proj_kernel(h_ref, g_ref, w0, w1, w2, w3, w4, o0, o1, o2, o3, o4):
    u = _rms(h_ref[...], g_ref[...]).astype(BF16)
    for w, o in ((w0, o0), (w1, o1), (w2, o2), (w3, o3), (w4, o4)):
        o[...] = _dot(u, w[...]).astype(BF16)


def _in_proj(h, g, ws):
    T = h.shape[0]
    tm = TOK_TILE
    widths = [w.shape[1] for w in ws]
    return pl.pallas_call(
        _in_proj_kernel,
        grid=(T // tm,),
        in_specs=[pl.BlockSpec((tm, D_MODEL), lambda i: (i, 0)), _full((1, D_MODEL))]
        + [pl.BlockSpec((D_MODEL, n), lambda i: (0, 0), pipeline_mode=pl.Buffered(1)) for n in widths],
        out_specs=[pl.BlockSpec((tm, n), lambda i: (i, 0)) for n in widths],
        out_shape=[jax.ShapeDtypeStruct((T, n), BF16) for n in widths],
        compiler_params=_params("parallel"),
        name="in_proj",
    )(h, g, *ws)


def _head_select(x, lane_head, n_heads, rows):
    out = None
    for hh in range(n_heads):
        term = jnp.where(lane_head == hh, x[hh * rows:(hh + 1) * rows], 0.0)
        out = term if out is None else out + term
    return out


def _head_norm_gate(o, gate_in, g):
    n = o.shape[1]
    same_head = (_iota((n, n), 0) >> 6) == (_iota((n, n), 1) >> 6)
    ms = _dot_f32_by_bf16(o * o, same_head.astype(BF16)) * (1.0 / 64.0)
    return o * lax.rsqrt(ms + NORM_EPS) * g * (gate_in * _sigmoid(gate_in))


def _gla_kernel(z_ref, wa_ref, ba_ref, g_ref, o_ref, st_ref, oacc_ref, att_ref, kv_ref, *, tiles_per_seq):
    C = GLA_CHUNK
    tt = z_ref.shape[0]

    @pl.when(pl.program_id(0) % tiles_per_seq == 0)
    def _():
        st_ref[...] = jnp.zeros_like(st_ref)

    q = z_ref[:, 0:128].astype(F32)
    k = z_ref[:, 128:256].astype(F32)
    r = z_ref[:, 512:768].astype(F32)
    pre = _dot_bf16_by_f32(z_ref[:, 768:896], wa_ref[...]) + ba_ref[...]
    log_a = (jnp.minimum(pre, 0.0) - jnp.log(1.0 + jnp.exp(-jnp.abs(pre)))) * (1.0 / GLA_TAU)

    lower = (_iota((C, C), 1) <= _iota((C, C), 0)).astype(BF16)
    b = jnp.concatenate([_dot_bf16_by_f32(lower, log_a[c * C:(c + 1) * C]) for c in range(tt // C)], axis=0)
    b_last = jnp.broadcast_to(b.reshape(tt // C, C, 128)[:, C - 1:C, :], (tt // C, C, 128)).reshape(tt, 128)
    q_dec = (q * (GLA_DK ** -0.5) * jnp.exp(b)).astype(BF16)
    k_dec = (k * jnp.exp(-b)).astype(BF16)
    k_end = (k * jnp.exp(b_last - b)).astype(BF16)
    decay = jnp.exp(b_last)

    qk_head = _iota((C, 128), 1) >> 5
    v_head = _iota((C, 256), 1) >> 6
    causal = _iota((4 * C, C), 1) <= (_iota((4 * C, C), 0) & (C - 1))
    state_mask = (_iota((256, 128), 0) >> 6) == (_iota((256, 128), 1) >> 5)

    for c in range(tt // C):
        sl = slice(c * C, (c + 1) * C)
        q_stack = jnp.concatenate([jnp.where(qk_head == hh, q_dec[sl], 0) for hh in range(GLA_HEADS)], axis=0)
        att_ref[c] = jnp.where(causal, _dot_nt(q_stack, k_dec[sl]), 0.0).astype(BF16)
        kv_ref[c] = jnp.where(state_mask, _dot_tn(z_ref[sl, 256:512], k_end[sl]), 0.0)
    state = st_ref[...]
    for c in range(tt // C):
        sl = slice(c * C, (c + 1) * C)
        intra = _head_select(_dot(att_ref[c], z_ref[sl, 256:512]), v_head, GLA_HEADS, C)
        inter = _dot_nt(q_dec[sl], state.astype(BF16))
        oacc_ref[sl, :] = intra + inter
        state = state * decay[c * C:c * C + 1] + kv_ref[c]
    st_ref[...] = state

    o_ref[...] = _head_norm_gate(oacc_ref[...], r, g_ref[...]).astype(BF16)


def _gla(z, wa, ba, g, seq):
    T = z.shape[0]
    tt = TOK_TILE
    return pl.pallas_call(
        functools.partial(_gla_kernel, tiles_per_seq=seq // tt),
        grid=(T // tt,),
        in_specs=[pl.BlockSpec((tt, GLA_W), lambda i: (i, 0)), _full((128, 128)), _full((1, 128)),
                  _full((1, 256))],
        out_specs=pl.BlockSpec((tt, 256), lambda i: (i, 0)),
        out_shape=jax.ShapeDtypeStruct((T, 256), BF16),
        scratch_shapes=[pltpu.VMEM((256, 128), F32), pltpu.VMEM((tt, 256), F32),
                        pltpu.VMEM((tt // GLA_CHUNK, GLA_HEADS * GLA_CHUNK, GLA_CHUNK), BF16),
                        pltpu.VMEM((tt // GLA_CHUNK, 256, 128), F32)],
        compiler_params=_params("arbitrary"),
        name="gla_mixer",
    )(z, wa, ba, g)


_RET_LOG_GAMMA = [math.log1p(-(2.0 ** (-5.0 - hh))) for hh in range(RET_HEADS)]


def _by_head(head, values):
    out = jnp.full(head.shape, values[-1], F32)
    for hh in range(len(values) - 2, -1, -1):
        out = jnp.where(head == hh, values[hh], out)
    return out


def _ret_kernel(z_ref, cos_ref, sina_ref, sinb_ref, g_ref, o_ref, st_ref, oacc_ref, *, tiles_per_seq):
    C = RET_CHUNK
    tt = z_ref.shape[0]

    @pl.when(pl.program_id(0) % tiles_per_seq == 0)
    def _():
        st_ref[...] = jnp.zeros_like(st_ref)

    cos, sina, sinb = cos_ref[...], sina_ref[...], sinb_ref[...]

    def rotate(t):
        return t * cos + pltpu.roll(t, 112, 1) * sina + pltpu.roll(t, 16, 1) * sinb

    q_rot = rotate(z_ref[:, 0:128].astype(F32))
    k_rot = rotate(z_ref[:, 128:256].astype(F32)) * (RET_DK ** -0.5)
    gate_in = z_ref[:, 512:768].astype(F32)

    qk_head = _iota((C, 128), 1) >> 5
    v_head = _iota((C, 256), 1) >> 6
    lg_qk = _by_head(_iota((1, 128), 1) >> 5, _RET_LOG_GAMMA)
    lg_v = _by_head(_iota((1, 256), 1) >> 6, _RET_LOG_GAMMA)
    pos_qk = _iota((C, 128), 0).astype(F32)
    pos_v = _iota((C, 256), 0).astype(F32)
    key_decay = jnp.exp((C - 1.0 - pos_qk) * lg_qk)
    query_decay = jnp.exp((pos_v + 1.0) * lg_v)
    chunk_decay = jnp.exp(float(C) * lg_qk)
    srow = _iota((4 * C, C), 0)
    rel = ((srow & (C - 1)) - _iota((4 * C, C), 1)).astype(F32)
    decay_mat = jnp.where(rel >= 0, jnp.exp(jnp.maximum(rel, 0.0) * _by_head(srow >> 7, _RET_LOG_GAMMA)), 0.0)
    state_mask = (_iota((256, 128), 0) >> 6) == (_iota((256, 128), 1) >> 5)

    q_bf = q_rot.astype(BF16)
    k_bf = k_rot.astype(BF16)
    state = st_ref[...]
    for c in range(tt // C):
        sl = slice(c * C, (c + 1) * C)
        qc = q_bf[sl]
        v = z_ref[sl, 256:512]
        q_stack = jnp.concatenate([jnp.where(qk_head == hh, qc, 0) for hh in range(RET_HEADS)], axis=0)
        att = _dot_nt(q_stack, k_bf[sl]) * decay_mat
        intra = _head_select(_dot(att.astype(BF16), v), v_head, RET_HEADS, C)
        inter = _dot_nt(qc, state.astype(BF16)) * query_decay
        oacc_ref[sl, :] = intra + inter
        kv = _dot_tn(v, (k_rot[sl] * key_decay).astype(BF16))
        state = state * chunk_decay + jnp.where(state_mask, kv, 0.0)
    st_ref[...] = state

    o_ref[...] = _head_norm_gate(oacc_ref[...], gate_in, g_ref[...]).astype(BF16)


def _retention(z, cos, sina, sinb, g, seq):
    T = z.shape[0]
    tt = TOK_TILE
    tps = seq // tt
    pos_spec = pl.BlockSpec((tt, 128), lambda i: (i % tps, 0))
    return pl.pallas_call(
        functools.partial(_ret_kernel, tiles_per_seq=tps),
        grid=(T // tt,),
        in_specs=[pl.BlockSpec((tt, RET_W), lambda i: (i, 0)), pos_spec, pos_spec, pos_spec, _full((1, 256))],
        out_specs=pl.BlockSpec((tt, 256), lambda i: (i, 0)),
        out_shape=jax.ShapeDtypeStruct((T, 256), BF16),
        scratch_shapes=[pltpu.VMEM((256, 128), F32), pltpu.VMEM((tt, 256), F32)],
        compiler_params=_params("arbitrary"),
        name="retention_mixer",
    )(z, cos, sina, sinb, g)


def _rope_tables(seq):
    half = RET_DK // 2
    pos = jnp.arange(seq, dtype=F32)
    inv_freq = RET_ROPE_BASE ** (-jnp.arange(half, dtype=F32) * 2.0 / RET_DK)
    ang = pos[:, None] * inv_freq[None, :]
    cos = jnp.tile(jnp.cos(ang), (1, LANES // half))
    sin = jnp.tile(jnp.sin(ang), (1, LANES // half))
    first_half = (jnp.arange(LANES) % RET_DK) < half
    return cos, jnp.where(first_half, -sin, 0.0), jnp.where(first_half, 0.0, sin)


def _swa_kernel(sink_ref, cur_ref, prev_ref, o_ref, s_ref, *, tiles_per_seq):
    W = SWA_WINDOW
    HD = SWA_HD
    tt = cur_ref.shape[0]
    first_key = jnp.where(pl.program_id(0) % tiles_per_seq == 0, W, 0)
    sj = _iota((2 * W, W), 0)
    qi = _iota((2 * W, W), 1)
    in_window = (sj > qi) & (sj <= qi + W)
    kv0 = SWA_HEADS * HD
    no_head = jnp.zeros((HD, W), BF16)
    n_blocks = tt // W

    def band(c, col0):
        prev = prev_ref[:, col0:col0 + 128] if c == 0 else cur_ref[(c - 1) * W:c * W, col0:col0 + 128]
        return jnp.concatenate([prev, cur_ref[c * W:(c + 1) * W, col0:col0 + 128]], axis=0)

    for c in range(n_blocks):
        mask = in_window & (sj >= first_key) if c == 0 else in_window
        k_band = band(c, kv0)
        q_t = cur_ref[c * W:(c + 1) * W, 0:kv0].astype(F32).T.astype(BF16)
        for hh in range(SWA_HEADS):
            kk = hh // (SWA_HEADS // SWA_KV_HEADS)
            qh_t = q_t[hh * HD:(hh + 1) * HD]
            q_on_kv = jnp.concatenate([qh_t, no_head] if kk == 0 else [no_head, qh_t], axis=0)
            s = _dot(k_band, q_on_kv) * (HD ** -0.5)
            s_ref[c, hh] = jnp.where(mask, s, NEG_INF)
    for c in range(n_blocks):
        v_t = band(c, kv0 + 128).astype(F32).T.astype(BF16)
        outs = []
        for hh in range(SWA_HEADS):
            kk = hh // (SWA_HEADS // SWA_KV_HEADS)
            s = s_ref[c, hh]
            sink = sink_ref[hh]
            m = jnp.maximum(jnp.max(s, axis=0, keepdims=True), sink)
            p = jnp.exp(s - m)
            denom = jnp.sum(p, axis=0, keepdims=True) + jnp.exp(sink - m)
            outs.append(_dot(v_t[kk * HD:(kk + 1) * HD], p.astype(BF16)) / denom)
        o_ref[c * W:(c + 1) * W, :] = jnp.concatenate(outs, axis=0).T.astype(BF16)


def _swa(z, sinks, seq):
    T = z.shape[0]
    tt = TOK_TILE
    per = tt // SWA_WINDOW
    return pl.pallas_call(
        functools.partial(_swa_kernel, tiles_per_seq=seq // tt),
        grid=(T // tt,),
        in_specs=[pl.BlockSpec(memory_space=pltpu.SMEM),
                  pl.BlockSpec((tt, SWA_W), lambda i: (i, 0)),
                  pl.BlockSpec((SWA_WINDOW, SWA_W), lambda i: (jnp.maximum(i * per - 1, 0), 0))],
        out_specs=pl.BlockSpec((tt, 256), lambda i: (i, 0)),
        out_shape=jax.ShapeDtypeStruct((T, 256), BF16),
        scratch_shapes=[pltpu.VMEM((per, SWA_HEADS, 2 * SWA_WINDOW, SWA_WINDOW), F32)],
        compiler_params=_params("parallel"),
        name="swa_mixer",
    )(sinks, z, z)


_KMEAN_BLOCKS = 8


def _kmean_kernel(k_ref, o_ref):
    k = k_ref[...].astype(F32).reshape(_KMEAN_BLOCKS, MOBA_BLOCK, 256)
    o_ref[...] = jnp.mean(k, axis=1)


def _moba_kmean(z):
    T = z.shape[0]
    rows = _KMEAN_BLOCKS * MOBA_BLOCK
    return pl.pallas_call(
        _kmean_kernel,
        grid=(T // rows,),
        in_specs=[pl.BlockSpec((rows, 256), lambda i: (i, 1))],
        out_specs=pl.BlockSpec((_KMEAN_BLOCKS, 256), lambda i: (i, 0)),
        out_shape=jax.ShapeDtypeStruct((T // MOBA_BLOCK, 256), F32),
        compiler_params=_params("parallel"),
        name="moba_kmean",
    )(z)


MOBA_VROWS = 80


def _moba_prep_kernel(z_ref, km_ref, qt_out, k_out, vt_out):
    HD = MOBA_HD
    tq = z_ref.shape[0]
    nb = km_ref.shape[0]
    qb = pl.program_id(0) % nb
    blk_i = _iota((nb, tq), 0)
    blk = blk_i.astype(F32)
    pad = LANES - HD - nb
    this_block_lanes = (_iota((tq, nb), 1) == qb).astype(BF16)
    ones_row = (_iota((MOBA_VROWS - HD, tq), 0) == 0).astype(BF16)
    for hh in range(MOBA_HEADS):
        cols = slice(hh * HD, (hh + 1) * HD)
        q_t = z_ref[:, cols].astype(F32).T
        gate = _dot(km_ref[:, cols], q_t, precision=HIGHEST)
        gate = jnp.where(blk_i < qb, gate, NEG_INF)
        keep = jnp.where(blk_i == qb, 1.0, 0.0)
        for _ in range(MOBA_TOPK):
            best = jnp.max(gate, axis=0, keepdims=True)
            first = jnp.min(jnp.where(gate == best, blk, float(nb)), axis=0, keepdims=True)
            hit = blk == first
            keep = keep + jnp.where(hit, jnp.where(best > 0.5 * NEG_INF, 1.0, 0.0), 0.0)
            gate = jnp.where(hit, BELOW_NEG_INF, gate)
        bias = ((keep - 1.0) * (-NEG_INF)).astype(BF16)
        qt_out[hh, 0] = jnp.concatenate([(q_t * (HD ** -0.5)).astype(BF16), bias, jnp.zeros((pad, tq), BF16)],
                                        axis=0)
        k_out[hh] = jnp.concatenate([z_ref[:, 256 + hh * HD:256 + (hh + 1) * HD], this_block_lanes,
                                     jnp.zeros((tq, pad), BF16)], axis=1)
        v_t = z_ref[:, 512 + hh * HD:512 + (hh + 1) * HD].astype(F32).T.astype(BF16)
        vt_out[hh, 0] = jnp.concatenate([v_t, ones_row], axis=0)


def _moba_prep(z, kmean, seq):
    T = z.shape[0]
    nb = seq // MOBA_BLOCK
    assert MOBA_HD + nb <= LANES
    tq = MOBA_BLOCK
    H = MOBA_HEADS
    def per_block_t(rows):
        return pl.BlockSpec((H, 1, rows, tq), lambda i: (0, i, 0, 0))

    return pl.pallas_call(
        _moba_prep_kernel,
        grid=(T // tq,),
        in_specs=[pl.BlockSpec((tq, MOBA_W), lambda i: (i, 0)), pl.BlockSpec((nb, 256), lambda i: (i // nb, 0))],
        out_specs=[per_block_t(LANES), pl.BlockSpec((H, tq, LANES), lambda i: (0, i, 0)),
                   per_block_t(MOBA_VROWS)],
        out_shape=[jax.ShapeDtypeStruct((H, T // tq, LANES, tq), BF16), jax.ShapeDtypeStruct((H, T, LANES), BF16),
                   jax.ShapeDtypeStruct((H, T // tq, MOBA_VROWS, tq), BF16)],
        compiler_params=_params("parallel"),
        name="moba_prep",
    )(z, kmean)


MOBA_QBLOCKS = 4
MOBA_KBLOCKS = 2
MOBA_DIAGONALS = MOBA_QBLOCKS // MOBA_KBLOCKS


def _moba_kernel(qidx_ref, kidx_ref, kind_ref, qt_ref, k_ref, vt_ref, o_ref, m_ref, acc_ref, s_ref):
    HD = MOBA_HD
    B = MOBA_BLOCK
    step = pl.program_id(1)
    kind = kind_ref[step]
    is_last = (step + 1 == pl.num_programs(1)) | (qidx_ref[jnp.minimum(step + 1, pl.num_programs(1) - 1)]
                                                  != qidx_ref[step])

    def attend(diagonal):
        causal = _iota((B, B), 0) <= _iota((B, B), 1)

        def key_blocks_of(qi):
            if diagonal is None:
                return list(range(MOBA_KBLOCKS))
            return [c for c in range(MOBA_KBLOCKS) if MOBA_KBLOCKS * diagonal + c <= qi]

        for hh in range(MOBA_HEADS):
            for qi in range(MOBA_QBLOCKS):
                for c in key_blocks_of(qi):
                    s = _dot(k_ref[hh, c * B:(c + 1) * B, :], qt_ref[hh, qi])
                    if diagonal is not None and MOBA_KBLOCKS * diagonal + c == qi:
                        s = jnp.where(causal, s, NEG_INF)
                    s_ref[hh, qi, c] = s
        for hh in range(MOBA_HEADS):
            for qi in range(MOBA_QBLOCKS):
                key_blocks = key_blocks_of(qi)
                if not key_blocks:
                    continue
                scores = [s_ref[hh, qi, c] for c in key_blocks]
                m_new = functools.reduce(jnp.maximum, [jnp.max(s, axis=0, keepdims=True) for s in scores])
                first_visit = diagonal == 0
                if not first_visit:
                    m_old = m_ref[hh, qi]
                    m_new = jnp.maximum(m_old, m_new)
                pv = None
                for c, s in zip(key_blocks, scores):
                    term = _dot(vt_ref[hh, c], jnp.exp(s - m_new).astype(BF16))
                    pv = term if pv is None else pv + term
                acc_ref[hh, qi] = pv if first_visit else jnp.exp(m_old - m_new) * acc_ref[hh, qi] + pv
                m_ref[hh, qi] = m_new

    @pl.when(kind == 0)
    def _():
        attend(None)

    for d in range(MOBA_DIAGONALS):
        @pl.when(kind == 1 + d)
        def _(d=d):
            attend(d)

    @pl.when(is_last)
    def _():
        pad = jnp.zeros((LANES - MOBA_VROWS, B), F32)
        for qi in range(MOBA_QBLOCKS):
            outs = []
            for hh in range(MOBA_HEADS):
                a = jnp.concatenate([acc_ref[hh, qi], pad], axis=0).T
                outs.append(a[:, 0:HD] / a[:, HD:HD + 1])
            o_ref[qi * B:(qi + 1) * B, :] = jnp.concatenate(outs, axis=1).astype(BF16)


def _moba(z, seq):
    T = z.shape[0]
    nb = seq // MOBA_BLOCK
    assert nb % MOBA_QBLOCKS == 0
    q_tiles = nb // MOBA_QBLOCKS
    k_steps = nb // MOBA_KBLOCKS
    D = MOBA_DIAGONALS
    batch = T // seq
    qt_aug, k_aug, vt_aug = _moba_prep(z, _moba_kmean(z), seq)
    qidx = np.concatenate([np.full(D * t + D, t) for t in range(q_tiles)]).astype(np.int32)
    kidx = np.concatenate([np.concatenate([D * t + np.arange(D), np.arange(D * t)])
                           for t in range(q_tiles)]).astype(np.int32)
    kind = np.concatenate([np.concatenate([1 + np.arange(D), np.zeros(D * t, np.int64)])
                           for t in range(q_tiles)]).astype(np.int32)
    tq = MOBA_QBLOCKS * MOBA_BLOCK
    tk = MOBA_KBLOCKS * MOBA_BLOCK
    H = MOBA_HEADS
    grid_spec = pltpu.PrefetchScalarGridSpec(
        num_scalar_prefetch=3,
        grid=(batch, len(qidx)),
        in_specs=[pl.BlockSpec((H, MOBA_QBLOCKS, LANES, MOBA_BLOCK),
                               lambda b, s, qi, ki, kd: (0, b * q_tiles + qi[s], 0, 0)),
                  pl.BlockSpec((H, tk, LANES), lambda b, s, qi, ki, kd: (0, b * k_steps + ki[s], 0)),
                  pl.BlockSpec((H, MOBA_KBLOCKS, MOBA_VROWS, MOBA_BLOCK),
                               lambda b, s, qi, ki, kd: (0, b * k_steps + ki[s], 0, 0))],
        out_specs=pl.BlockSpec((tq, 256), lambda b, s, qi, ki, kd: (b * q_tiles + qi[s], 0)),
        scratch_shapes=[pltpu.VMEM((H, MOBA_QBLOCKS, 1, MOBA_BLOCK), F32),
                        pltpu.VMEM((H, MOBA_QBLOCKS, MOBA_VROWS, MOBA_BLOCK), F32),
                        pltpu.VMEM((H, MOBA_QBLOCKS, MOBA_KBLOCKS, MOBA_BLOCK, MOBA_BLOCK), F32)],
    )
    return pl.pallas_call(
        _moba_kernel,
        grid_spec=grid_spec,
        out_shape=jax.ShapeDtypeStruct((T, 256), BF16),
        compiler_params=_params("parallel", "arbitrary"),
        name="moba_mixer",
    )(jnp.asarray(qidx), jnp.asarray(kidx), jnp.asarray(kind), qt_aug, k_aug, vt_aug)


def _merge_kernel(h_ref, o0, o1, o2, o3, gates_ref, wb_ref, wout_ref, out_ref):
    merged = None
    for i, o in enumerate((o0, o1, o2, o3)):
        gate = _sigmoid(gates_ref[:, i * D_MODEL:(i + 1) * D_MODEL].astype(F32))
        term = gate * _dot(o[...], wb_ref[i])
        merged = term if merged is None else merged + term
    out_ref[...] = h_ref[...] + _dot(merged.astype(BF16), wout_ref[...])


def _merge(h, branches, gates, wb, wout):
    T = h.shape[0]
    tm = TOK_TILE
    return pl.pallas_call(
        _merge_kernel,
        grid=(T // tm,),
        in_specs=[pl.BlockSpec((tm, D_MODEL), lambda i: (i, 0))]
        + [pl.BlockSpec((tm, 256), lambda i: (i, 0))] * 4
        + [pl.BlockSpec((tm, GATE_W), lambda i: (i, 0)),
           pl.BlockSpec((N_BRANCH, 256, D_MODEL), lambda i: (0, 0, 0), pipeline_mode=pl.Buffered(1)),
           pl.BlockSpec((D_MODEL, D_MODEL), lambda i: (0, 0), pipeline_mode=pl.Buffered(1))],
        out_specs=pl.BlockSpec((tm, D_MODEL), lambda i: (i, 0)),
        out_shape=jax.ShapeDtypeStruct((T, D_MODEL), F32),
        compiler_params=_params("parallel"),
        name="merge_out_proj",
    )(h, *branches, gates, wb, wout)


ROW_HALF = D_MODEL // 2
ROW_TILES = ROW_HALF // LANES
HIGH_HALF_MASK = -65536


def _row_tile_spec(rows, index_map):
    return pl.BlockSpec((rows * ROW_TILES, LANES), index_map)


def _bf16_bits(x):
    return lax.bitcast_convert_type(x.astype(BF16).astype(F32), I32)


def _pack_rows(ref, x):
    for j in range(ROW_TILES):
        low = _bf16_bits(x[:, j * LANES:(j + 1) * LANES])
        high = _bf16_bits(x[:, ROW_HALF + j * LANES:ROW_HALF + (j + 1) * LANES])
        ref[pl.ds(j, x.shape[0], stride=ROW_TILES), :] = high | lax.shift_right_logical(low, 16)


def _unpack_slab(ref, j):
    words = ref[pl.ds(j, ref.shape[0] // ROW_TILES, stride=ROW_TILES), :]
    low = lax.bitcast_convert_type(lax.shift_left(words, 16), F32)
    high = lax.bitcast_convert_type(words & HIGH_HALF_MASK, F32)
    return low, high


def _unpack_rows(ref):
    halves = [_unpack_slab(ref, j) for j in range(ROW_TILES)]
    return jnp.concatenate([lo for lo, _ in halves] + [hi for _, hi in halves], axis=1).astype(BF16)


def _lane_first(mask, lane):
    return jnp.min(jnp.where(mask, lane, float(LANES)), axis=-1, keepdims=True)


def _router_kernel(h_ref, g_ref, wr_ref, br_ref, u_ref, route_ref, route_t_ref, count_ref, carry_ref):
    tm = h_ref.shape[0]

    @pl.when(pl.program_id(0) == 0)
    def _():
        carry_ref[...] = jnp.zeros_like(carry_ref)

    u = _rms(h_ref[...], g_ref[...])
    _pack_rows(u_ref, u)
    u_hi, u_lo = _split_bf16(u, 2)
    w_hi, w_lo = _split_bf16(wr_ref[...], 2)
    logits = _dot(u_hi, w_hi) + (_dot(u_lo, w_hi) + _dot(u_hi, w_lo)) + br_ref[...]
    lane_i = _iota((tm, LANES), 1)
    lane = lane_i.astype(F32)
    lane_group = ((lane_i - EXPERT_LANE0) >> 3).astype(F32)

    grp = jnp.where(lane_i < N_GROUPS, logits, BELOW_NEG_INF)
    grp_e = jnp.exp(grp - jnp.max(grp, axis=-1, keepdims=True))
    p_group = grp_e / jnp.sum(grp_e, axis=-1, keepdims=True)
    g_w = jnp.max(p_group, axis=-1, keepdims=True)
    g_sel = _lane_first(p_group == g_w, lane)

    in_group = (lane_i >= EXPERT_LANE0) & (lane_group == g_sel)
    el = jnp.where(in_group, logits, BELOW_NEG_INF)
    top1 = jnp.max(el, axis=-1, keepdims=True)
    lane1 = _lane_first(el == top1, lane)
    el = jnp.where(lane == lane1, BELOW_NEG_INF, el)
    top2 = jnp.max(el, axis=-1, keepdims=True)
    lane2 = _lane_first(el == top2, lane)
    e2 = jnp.exp(top2 - top1)
    w1 = g_w / (1.0 + e2)
    w2 = g_w * e2 / (1.0 + e2)

    onehot = ((lane == lane1) | (lane == lane2)).astype(BF16)
    before = (_iota((tm, tm), 1) < _iota((tm, tm), 0)).astype(BF16)
    seen = _dot(before, onehot) + carry_ref[...]
    rank1 = jnp.sum(jnp.where(lane == lane1, seen, 0.0), axis=-1, keepdims=True)
    rank2 = jnp.sum(jnp.where(lane == lane2, seen, 0.0), axis=-1, keepdims=True)
    carry_ref[...] += jnp.sum(onehot.astype(F32), axis=0, keepdims=True)
    count_ref[...] = carry_ref[...]

    out = jnp.zeros((tm, LANES), F32)
    for idx, val in enumerate((lane1 - EXPERT_LANE0, lane2 - EXPERT_LANE0, w1, w2, rank1, rank2)):
        out = jnp.where(lane_i == idx, val, out)
    route_ref[...] = out
    route_t_ref[...] = out.T[0:8, :]


def _router(h, g, wr, br):
    T = h.shape[0]
    tm = TOK_TILE
    return pl.pallas_call(
        _router_kernel,
        grid=(T // tm,),
        in_specs=[pl.BlockSpec((tm, D_MODEL), lambda i: (i, 0)), _full((1, D_MODEL)),
                  _full((D_MODEL, LANES)), _full((1, LANES))],
        out_specs=[_row_tile_spec(tm, lambda i: (i, 0)), pl.BlockSpec((tm, LANES), lambda i: (i, 0)),
                   pl.BlockSpec((8, tm), lambda i: (0, i)), _full((1, LANES))],
        out_shape=[jax.ShapeDtypeStruct((T * ROW_TILES, LANES), I32), jax.ShapeDtypeStruct((T, LANES), F32),
                   jax.ShapeDtypeStruct((8, T), F32), jax.ShapeDtypeStruct((1, LANES), F32)],
        scratch_shapes=[pltpu.VMEM((1, LANES), F32)],
        compiler_params=_params("arbitrary"),
        name="moe_router",
    )(h, g, wr, br)


ROW_UNROLL = 8


def _row_copy(src_ref, src_row, dst_ref, dst_row, sem):
    src = src_ref.at[pl.ds(pl.multiple_of(src_row * ROW_TILES, ROW_TILES), ROW_TILES)]
    dst = dst_ref.at[pl.ds(pl.multiple_of(dst_row * ROW_TILES, ROW_TILES), ROW_TILES)]
    return pltpu.make_async_copy(src, dst, sem)


def _for_each_row_pair(rows, body):
    def group(g, carry):
        for i in range(ROW_UNROLL):
            for k in range(EXPERT_TOPK):
                body(g * ROW_UNROLL + i, k)
        return carry

    lax.fori_loop(0, rows // ROW_UNROLL, group, 0)


def _dispatch_kernel(dest0_ref, dest1_ref, pstart_ref, pend_ref, u_ref, xs_ref, zero_ref, sem):
    tm = u_ref.shape[0] // ROW_TILES
    tile = pl.program_id(0)
    dests = (dest0_ref, dest1_ref)

    @pl.when(tile == 0)
    def _():
        zero_ref[...] = jnp.zeros_like(zero_ref)

        def zero_block(first_row):
            start = pl.multiple_of(first_row * ROW_TILES, ROW_TILES)
            fill = pltpu.make_async_copy(zero_ref, xs_ref.at[pl.ds(start, MOE_BLOCK * ROW_TILES)], sem)
            fill.start()
            fill.wait()

        def zero_tail(e, carry):
            @pl.when(pend_ref[e] > pstart_ref[e])
            def _():
                zero_block(pend_ref[e] - MOE_BLOCK)
            return carry

        def zero_unused(b, carry):
            zero_block(b * MOE_BLOCK)
            return carry

        lax.fori_loop(0, N_EXPERTS, zero_tail, 0)
        lax.fori_loop(pend_ref[N_EXPERTS - 1] // MOE_BLOCK, xs_ref.shape[0] // (MOE_BLOCK * ROW_TILES),
                      zero_unused, 0)

    def issue(r, k):
        _row_copy(u_ref, r, xs_ref, dests[k][tile * tm + r], sem).start(priority=k)

    _for_each_row_pair(tm, issue)
    for _ in range(EXPERT_TOPK):
        pltpu.make_async_copy(u_ref, xs_ref.at[pl.ds(0, tm * ROW_TILES)], sem).wait()


def _dispatch(dest0, dest1, pstart, pend, u, n_slots):
    T = u.shape[0] // ROW_TILES
    tm = DISPATCH_TILE
    grid_spec = pltpu.PrefetchScalarGridSpec(
        num_scalar_prefetch=4,
        grid=(T // tm,),
        in_specs=[_row_tile_spec(tm, lambda i, *_: (i, 0))],
        out_specs=pl.BlockSpec(memory_space=pl.ANY),
        scratch_shapes=[pltpu.VMEM((MOE_BLOCK * ROW_TILES, LANES), I32), pltpu.SemaphoreType.DMA(())],
    )
    return pl.pallas_call(
        _dispatch_kernel,
        grid_spec=grid_spec,
        out_shape=jax.ShapeDtypeStruct((n_slots * ROW_TILES, LANES), I32),
        compiler_params=_params("arbitrary"),
        name="moe_dispatch",
    )(dest0, dest1, pstart, pend, u)


def _expert_kernel(blk_e_ref, n_used_ref, x_ref, w1_ref, w3_ref, w2_ref, y_ref, w1b_ref, w3b_ref, w2b_ref):
    i = pl.program_id(0)
    used = i < n_used_ref[0]

    @pl.when((i == 0) | (blk_e_ref[i] != blk_e_ref[jnp.maximum(i - 1, 0)]))
    def _():
        w1b_ref[...] = w1_ref[0, 0].astype(BF16)
        w3b_ref[...] = w3_ref[0, 0].astype(BF16)
        w2b_ref[...] = w2_ref[0, 0].astype(BF16)

    @pl.when(used)
    def _():
        x = _unpack_rows(x_ref)
        a = _dot(x, w1b_ref[...])
        hidden = (a * _sigmoid(a)) * _dot(x, w3b_ref[...])
        _pack_rows(y_ref, _dot(hidden.astype(BF16), w2b_ref[...]))

    @pl.when(jnp.logical_not(used))
    def _():
        y_ref[...] = jnp.zeros_like(y_ref)


def _experts(blk_e, n_used, xs, w1, w3, w2, layer):
    n_slots = xs.shape[0] // ROW_TILES
    bm = MOE_BLOCK
    grid_spec = pltpu.PrefetchScalarGridSpec(
        num_scalar_prefetch=2,
        grid=(n_slots // bm,),
        in_specs=[_row_tile_spec(bm, lambda i, be, nu: (jnp.minimum(i, nu[0] - 1), 0)),
                  pl.BlockSpec((1, 1, D_MODEL, EXPERT_HIDDEN), lambda i, be, nu: (layer, be[i], 0, 0)),
                  pl.BlockSpec((1, 1, D_MODEL, EXPERT_HIDDEN), lambda i, be, nu: (layer, be[i], 0, 0)),
                  pl.BlockSpec((1, 1, EXPERT_HIDDEN, D_MODEL), lambda i, be, nu: (layer, be[i], 0, 0))],
        out_specs=_row_tile_spec(bm, lambda i, be, nu: (i, 0)),
        scratch_shapes=[pltpu.VMEM((D_MODEL, EXPERT_HIDDEN), BF16), pltpu.VMEM((D_MODEL, EXPERT_HIDDEN), BF16),
                        pltpu.VMEM((EXPERT_HIDDEN, D_MODEL), BF16)],
    )
    return pl.pallas_call(
        _expert_kernel,
        grid_spec=grid_spec,
        out_shape=jax.ShapeDtypeStruct((n_slots * ROW_TILES, LANES), I32),
        compiler_params=_params("arbitrary"),
        name="moe_experts",
    )(blk_e, n_used, xs, w1, w3, w2)


def _combine_kernel(dest0_ref, dest1_ref, h_ref, route_ref, g_ref, ys_ref, out_ref, y_ref, sem, *, normalize):
    tm = h_ref.shape[0]
    step = pl.program_id(0)
    slot = step % 2

    dests = (dest0_ref, dest1_ref)

    def gather_tile(tile, buf):
        def issue(r, k):
            _row_copy(ys_ref, dests[k][tile * tm + r], y_ref.at[buf, k], r, sem.at[buf]).start(priority=k)

        _for_each_row_pair(tm, issue)

    @pl.when(step == 0)
    def _():
        gather_tile(0, 0)

    @pl.when(step + 1 < pl.num_programs(0))
    def _():
        gather_tile(step + 1, 1 - slot)

    for k in range(EXPERT_TOPK):
        pltpu.make_async_copy(ys_ref.at[pl.ds(0, tm * ROW_TILES)], y_ref.at[slot, k], sem.at[slot]).wait()
    w0 = route_ref[:, 2:3]
    w1 = route_ref[:, 3:4]
    for j in range(ROW_TILES):
        low0, high0 = _unpack_slab(y_ref.at[slot, 0], j)
        low1, high1 = _unpack_slab(y_ref.at[slot, 1], j)
        cols = slice(j * LANES, (j + 1) * LANES)
        out_ref[:, cols] = h_ref[:, cols] + w0 * low0 + w1 * low1
        cols = slice(ROW_HALF + j * LANES, ROW_HALF + (j + 1) * LANES)
        out_ref[:, cols] = h_ref[:, cols] + w0 * high0 + w1 * high1
    if normalize:
        out_ref[...] = _rms(out_ref[...], g_ref[...])


def _combine(dest0, dest1, h, route, ys, final_g, normalize):
    T = h.shape[0]
    tm = ROW_TILE
    grid_spec = pltpu.PrefetchScalarGridSpec(
        num_scalar_prefetch=2,
        grid=(T // tm,),
        in_specs=[pl.BlockSpec((tm, D_MODEL), lambda i, *_: (i, 0)),
                  pl.BlockSpec((tm, LANES), lambda i, *_: (i, 0)),
                  _full((1, D_MODEL)),
                  pl.BlockSpec(memory_space=pl.ANY)],
        out_specs=pl.BlockSpec((tm, D_MODEL), lambda i, *_: (i, 0)),
        scratch_shapes=[pltpu.VMEM((2, EXPERT_TOPK, tm * ROW_TILES, LANES), I32),
                        pltpu.SemaphoreType.DMA((2,))],
    )
    return pl.pallas_call(
        functools.partial(_combine_kernel, normalize=normalize),
        grid_spec=grid_spec,
        out_shape=jax.ShapeDtypeStruct((T, D_MODEL), F32),
        compiler_params=_params("arbitrary"),
        name="moe_combine",
    )(dest0, dest1, h, route, final_g, ys)


def _moe(h, g, w_group, b_group, w_expert, b_expert, w1, w3, w2, layer, final_g, normalize):
    T = h.shape[0]
    wr = jnp.zeros((D_MODEL, LANES), F32)
    wr = wr.at[:, :N_GROUPS].set(w_group).at[:, EXPERT_LANE0:EXPERT_LANE0 + N_EXPERTS].set(w_expert)
    br = jnp.zeros((1, LANES), F32)
    br = br.at[0, :N_GROUPS].set(b_group).at[0, EXPERT_LANE0:EXPERT_LANE0 + N_EXPERTS].set(b_expert)
    u, route, route_t, count = _router(h, g, wr, br)

    n_blocks = (T * EXPERT_TOPK) // MOE_BLOCK + N_EXPERTS
    counts = count[0, EXPERT_LANE0:EXPERT_LANE0 + N_EXPERTS].astype(I32)
    padded = ((counts + MOE_BLOCK - 1) // MOE_BLOCK) * MOE_BLOCK
    pend = jnp.cumsum(padded)
    pstart = pend - padded
    expert_ids = jnp.arange(N_EXPERTS, dtype=F32)[:, None]

    def slot_of(expert_row, rank_row):
        segment_start = jnp.sum(jnp.where(expert_row[None, :] == expert_ids, pstart[:, None], 0), axis=0)
        return segment_start + rank_row.astype(I32)

    dest0 = slot_of(route_t[0], route_t[4])
    dest1 = slot_of(route_t[1], route_t[5])
    block_row0 = jnp.arange(n_blocks, dtype=I32) * MOE_BLOCK
    blk_e = jnp.minimum(jnp.sum((pend[None, :] <= block_row0[:, None]).astype(I32), axis=1), N_EXPERTS - 1)
    n_used = (pend[-1:] // MOE_BLOCK).astype(I32)

    xs = _dispatch(dest0, dest1, pstart, pend, u, n_blocks * MOE_BLOCK)
    ys = _experts(blk_e, n_used, xs, w1, w3, w2, layer)
    return _combine(dest0, dest1, h, route, ys, final_g, normalize)


def _split_w_in(w_in):
    gla_n = GLA_HEADS * (2 * GLA_DK + 2 * GLA_DV) + GLA_LOWRANK
    swa_n = (SWA_HEADS + 2 * SWA_KV_HEADS) * SWA_HD
    moba_n = 3 * MOBA_HEADS * MOBA_HD
    ret_n = RET_HEADS * (2 * RET_DK + 2 * RET_DV)
    offs = np.cumsum([0, gla_n, swa_n, moba_n, ret_n, GATE_W])
    parts = [w_in[:, offs[i]:offs[i + 1]].astype(BF16) for i in range(5)]
    parts[0] = jnp.pad(parts[0], ((0, 0), (0, GLA_W - gla_n)))
    return parts


def _token_mixer(h, seq, ln1_g, w_in, gla_w_a2, gla_b_a, gla_norm_g, swa_sinks, ret_norm_g, w_branch, w_out,
                 rope):
    z_gla, z_swa, z_moba, z_ret, z_gates = _in_proj(h, ln1_g.reshape(1, D_MODEL), _split_w_in(w_in))
    wa = jnp.zeros((128, 128), F32).at[:GLA_LOWRANK].set(gla_w_a2)
    o_gla = _gla(z_gla, wa, gla_b_a.reshape(1, 128), jnp.tile(gla_norm_g, GLA_HEADS).reshape(1, 256), seq)
    o_swa = _swa(z_swa, swa_sinks, seq)
    o_moba = _moba(z_moba, seq)
    o_ret = _retention(z_ret, *rope, jnp.tile(ret_norm_g, RET_HEADS).reshape(1, 256), seq)
    return _merge(h, (o_gla, o_swa, o_moba, o_ret), z_gates, w_branch.astype(BF16), w_out.astype(BF16))


def kernel(x, ln1_g, w_in, gla_w_a2, gla_b_a, gla_norm_g, swa_sinks, ret_norm_g, w_branch, w_out, ln2_g,
           w_group, b_group, w_expert, b_expert, w1, w3, w2, final_g):
    batch, seq, _ = x.shape
    depth = w_in.shape[0]
    assert depth >= 1
    rope = _rope_tables(seq)
    h = x.reshape(batch * seq, D_MODEL)
    for l in range(depth):
        h = _token_mixer(h, seq, ln1_g[l], w_in[l], gla_w_a2[l], gla_b_a[l], gla_norm_g[l], swa_sinks[l],
                         ret_norm_g[l], w_branch[l], w_out[l], rope)
        h = _moe(h, ln2_g[l].reshape(1, D_MODEL), w_group[l], b_group[l], w_expert[l], b_expert[l],
                 w1, w3, w2, l, final_g.reshape(1, D_MODEL), normalize=(l == depth - 1))
    return h.reshape(batch, seq, D_MODEL)
```

```python
import functools
import math

import numpy as np
import jax
import jax.numpy as jnp
from jax import lax
from jax.experimental import pallas as pl
from jax.experimental.pallas import tpu as pltpu

F32 = jnp.float32
BF16 = jnp.bfloat16
I32 = jnp.int32
HIGHEST = lax.Precision.HIGHEST

D_MODEL = 1024
N_BRANCH = 4
NORM_EPS = 1e-6
NEG_INF = -1e30
BELOW_NEG_INF = -3e38

GLA_HEADS, GLA_DK, GLA_DV, GLA_LOWRANK, GLA_TAU, GLA_CHUNK = 4, 32, 64, 16, 16.0, 64
SWA_HEADS, SWA_KV_HEADS, SWA_HD, SWA_WINDOW = 4, 2, 64, 128
MOBA_HEADS, MOBA_HD, MOBA_BLOCK, MOBA_TOPK = 4, 64, 256, 3
RET_HEADS, RET_DK, RET_DV, RET_CHUNK, RET_ROPE_BASE = 4, 32, 64, 128, 10000.0
N_GROUPS, EXPERTS_PER_GROUP, EXPERT_TOPK, EXPERT_HIDDEN = 4, 8, 2, 256
MOE_BLOCK = 512
N_EXPERTS = N_GROUPS * EXPERTS_PER_GROUP

GLA_W = 896
SWA_W = 512
MOBA_W = 768
RET_W = 768
GATE_W = N_BRANCH * D_MODEL

LANES = 128
EXPERT_LANE0 = 32

VMEM_LIMIT = 56 * 1024 * 1024

TOK_TILE = 512
ROW_TILE = 256
DISPATCH_TILE = 512


def _params(*sem):
    return pltpu.CompilerParams(dimension_semantics=sem, vmem_limit_bytes=VMEM_LIMIT)


def _dot(a, b, precision=None):
    return jnp.dot(a, b, preferred_element_type=F32, precision=precision)


def _dot_nt(a, b, precision=None):
    return lax.dot_general(a, b, (((1,), (1,)), ((), ())), preferred_element_type=F32, precision=precision)


def _dot_tn(a, b, precision=None):
    return lax.dot_general(a, b, (((0,), (0,)), ((), ())), preferred_element_type=F32, precision=precision)


def _split_bf16(x, parts):
    out = []
    for _ in range(parts - 1):
        piece = x.astype(BF16)
        out.append(piece)
        x = x - piece.astype(F32)
    out.append(x.astype(BF16))
    return out


def _dot_f32_by_bf16(x, m):
    return sum(_dot(piece, m) for piece in _split_bf16(x, 3))


def _dot_bf16_by_f32(m, x):
    return sum(_dot(m, piece) for piece in _split_bf16(x, 3))


def _iota(shape, dim):
    return lax.broadcasted_iota(I32, shape, dim)


def _sigmoid(x):
    return 0.5 * jnp.tanh(0.5 * x) + 0.5


def _rms(x, g):
    ms = jnp.mean(x * x, axis=-1, keepdims=True)
    return x * lax.rsqrt(ms + NORM_EPS) * g


def _full(shape):
    return pl.BlockSpec(shape, lambda *_: (0,) * len(shape))


def _in_proj_kernel(h_ref, g_ref, w0, w1, w2, w3, w4, o0, o1, o2, o3, o4):
    u = _rms(h_ref[...], g_ref[...]).astype(BF16)
    for w, o in ((w0, o0), (w1, o1), (w2, o2), (w3, o3), (w4, o4)):
        o[...] = _dot(u, w[...]).astype(BF16)


def _in_proj(h, g, ws):
    T = h.shape[0]
    tm = TOK_TILE
    widths = [w.shape[1] for w in ws]
    return pl.pallas_call(
        _in_proj_kernel,
        grid=(T // tm,),
        in_specs=[pl.BlockSpec((tm, D_MODEL), lambda i: (i, 0)), _full((1, D_MODEL))]
        + [pl.BlockSpec((D_MODEL, n), lambda i: (0, 0), pipeline_mode=pl.Buffered(1)) for n in widths],
        out_specs=[pl.BlockSpec((tm, n), lambda i: (i, 0)) for n in widths],
        out_shape=[jax.ShapeDtypeStruct((T, n), BF16) for n in widths],
        compiler_params=_params("parallel"),
        name="in_proj",
    )(h, g, *ws)


def _head_select(x, lane_head, n_heads, rows):
    out = None
    for hh in range(n_heads):
        term = jnp.where(lane_head == hh, x[hh * rows:(hh + 1) * rows], 0.0)
        out = term if out is None else out + term
    return out


def _head_norm_gate(o, gate_in, g):
    n = o.shape[1]
    same_head = (_iota((n, n), 0) >> 6) == (_iota((n, n), 1) >> 6)
    ms = _dot_f32_by_bf16(o * o, same_head.astype(BF16)) * (1.0 / 64.0)
    return o * lax.rsqrt(ms + NORM_EPS) * g * (gate_in * _sigmoid(gate_in))


def _gla_kernel(z_ref, wa_ref, ba_ref, g_ref, o_ref, st_ref, oacc_ref, att_ref, kv_ref, *, tiles_per_seq):
    C = GLA_CHUNK
    tt = z_ref.shape[0]

    @pl.when(pl.program_id(0) % tiles_per_seq == 0)
    def _():
        st_ref[...] = jnp.zeros_like(st_ref)

    q = z_ref[:, 0:128].astype(F32)
    k = z_ref[:, 128:256].astype(F32)
    r = z_ref[:, 512:768].astype(F32)
    pre = _dot_bf16_by_f32(z_ref[:, 768:896], wa_ref[...]) + ba_ref[...]
    log_a = (jnp.minimum(pre, 0.0) - jnp.log(1.0 + jnp.exp(-jnp.abs(pre)))) * (1.0 / GLA_TAU)

    lower = (_iota((C, C), 1) <= _iota((C, C), 0)).astype(BF16)
    b = jnp.concatenate([_dot_bf16_by_f32(lower, log_a[c * C:(c + 1) * C]) for c in range(tt // C)], axis=0)
    b_last = jnp.broadcast_to(b.reshape(tt // C, C, 128)[:, C - 1:C, :], (tt // C, C, 128)).reshape(tt, 128)
    q_dec = (q * (GLA_DK ** -0.5) * jnp.exp(b)).astype(BF16)
    k_dec = (k * jnp.exp(-b)).astype(BF16)
    k_end = (k * jnp.exp(b_last - b)).astype(BF16)
    decay = jnp.exp(b_last)

    qk_head = _iota((C, 128), 1) >> 5
    v_head = _iota((C, 256), 1) >> 6
    causal = _iota((4 * C, C), 1) <= (_iota((4 * C, C), 0) & (C - 1))
    state_mask = (_iota((256, 128), 0) >> 6) == (_iota((256, 128), 1) >> 5)

    for c in range(tt // C):
        sl = slice(c * C, (c + 1) * C)
        q_stack = jnp.concatenate([jnp.where(qk_head == hh, q_dec[sl], 0) for hh in range(GLA_HEADS)], axis=0)
        att_ref[c] = jnp.where(causal, _dot_nt(q_stack, k_dec[sl]), 0.0).astype(BF16)
        kv_ref[c] = jnp.where(state_mask, _dot_tn(z_ref[sl, 256:512], k_end[sl]), 0.0)
    state = st_ref[...]
    for c in range(tt // C):
        sl = slice(c * C, (c + 1) * C)
        intra = _head_select(_dot(att_ref[c], z_ref[sl, 256:512]), v_head, GLA_HEADS, C)
        inter = _dot_nt(q_dec[sl], state.astype(BF16))
        oacc_ref[sl, :] = intra + inter
        state = state * decay[c * C:c * C + 1] + kv_ref[c]
    st_ref[...] = state

    o_ref[...] = _head_norm_gate(oacc_ref[...], r, g_ref[...]).astype(BF16)


def _gla(z, wa, ba, g, seq):
    T = z.shape[0]
    tt = TOK_TILE
    return pl.pallas_call(
        functools.partial(_gla_kernel, tiles_per_seq=seq // tt),
        grid=(T // tt,),
        in_specs=[pl.BlockSpec((tt, GLA_W), lambda i: (i, 0)), _full((128, 128)), _full((1, 128)),
                  _full((1, 256))],
        out_specs=pl.BlockSpec((tt, 256), lambda i: (i, 0)),
        out_shape=jax.ShapeDtypeStruct((T, 256), BF16),
        scratch_shapes=[pltpu.VMEM((256, 128), F32), pltpu.VMEM((tt, 256), F32),
                        pltpu.VMEM((tt // GLA_CHUNK, GLA_HEADS * GLA_CHUNK, GLA_CHUNK), BF16),
                        pltpu.VMEM((tt // GLA_CHUNK, 256, 128), F32)],
        compiler_params=_params("arbitrary"),
        name="gla_mixer",
    )(z, wa, ba, g)


_RET_LOG_GAMMA = [math.log1p(-(2.0 ** (-5.0 - hh))) for hh in range(RET_HEADS)]


def _by_head(head, values):
    out = jnp.full(head.shape, values[-1], F32)
    for hh in range(len(values) - 2, -1, -1):
        out = jnp.where(head == hh, values[hh], out)
    return out


def _ret_kernel(z_ref, cos_ref, sina_ref, sinb_ref, g_ref, o_ref, st_ref, oacc_ref, *, tiles_per_seq):
    C = RET_CHUNK
    tt = z_ref.shape[0]

    @pl.when(pl.program_id(0) % tiles_per_seq == 0)
    def _():
        st_ref[...] = jnp.zeros_like(st_ref)

    cos, sina, sinb = cos_ref[...], sina_ref[...], sinb_ref[...]

    def rotate(t):
        return t * cos + pltpu.roll(t, 112, 1) * sina + pltpu.roll(t, 16, 1) * sinb

    q_rot = rotate(z_ref[:, 0:128].astype(F32))
    k_rot = rotate(z_ref[:, 128:256].astype(F32)) * (RET_DK ** -0.5)
    gate_in = z_ref[:, 512:768].astype(F32)

    qk_head = _iota((C, 128), 1) >> 5
    v_head = _iota((C, 256), 1) >> 6
    lg_qk = _by_head(_iota((1, 128), 1) >> 5, _RET_LOG_GAMMA)
    lg_v = _by_head(_iota((1, 256), 1) >> 6, _RET_LOG_GAMMA)
    pos_qk = _iota((C, 128), 0).astype(F32)
    pos_v = _iota((C, 256), 0).astype(F32)
    key_decay = jnp.exp((C - 1.0 - pos_qk) * lg_qk)
    query_decay = jnp.exp((pos_v + 1.0) * lg_v)
    chunk_decay = jnp.exp(float(C) * lg_qk)
    srow = _iota((4 * C, C), 0)
    rel = ((srow & (C - 1)) - _iota((4 * C, C), 1)).astype(F32)
    decay_mat = jnp.where(rel >= 0, jnp.exp(jnp.maximum(rel, 0.0) * _by_head(srow >> 7, _RET_LOG_GAMMA)), 0.0)
    state_mask = (_iota((256, 128), 0) >> 6) == (_iota((256, 128), 1) >> 5)

    q_bf = q_rot.astype(BF16)
    k_bf = k_rot.astype(BF16)
    state = st_ref[...]
    for c in range(tt // C):
        sl = slice(c * C, (c + 1) * C)
        qc = q_bf[sl]
        v = z_ref[sl, 256:512]
        q_stack = jnp.concatenate([jnp.where(qk_head == hh, qc, 0) for hh in range(RET_HEADS)], axis=0)
        att = _dot_nt(q_stack, k_bf[sl]) * decay_mat
        intra = _head_select(_dot(att.astype(BF16), v), v_head, RET_HEADS, C)
        inter = _dot_nt(qc, state.astype(BF16)) * query_decay
        oacc_ref[sl, :] = intra + inter
        kv = _dot_tn(v, (k_rot[sl] * key_decay).astype(BF16))
        state = state * chunk_decay + jnp.where(state_mask, kv, 0.0)
    st_ref[...] = state

    o_ref[...] = _head_norm_gate(oacc_ref[...], gate_in, g_ref[...]).astype(BF16)


def _retention(z, cos, sina, sinb, g, seq):
    T = z.shape[0]
    tt = TOK_TILE
    tps = seq // tt
    pos_spec = pl.BlockSpec((tt, 128), lambda i: (i % tps, 0))
    return pl.pallas_call(
        functools.partial(_ret_kernel, tiles_per_seq=tps),
        grid=(T // tt,),
        in_specs=[pl.BlockSpec((tt, RET_W), lambda i: (i, 0)), pos_spec, pos_spec, pos_spec, _full((1, 256))],
        out_specs=pl.BlockSpec((tt, 256), lambda i: (i, 0)),
        out_shape=jax.ShapeDtypeStruct((T, 256), BF16),
        scratch_shapes=[pltpu.VMEM((256, 128), F32), pltpu.VMEM((tt, 256), F32)],
        compiler_params=_params("arbitrary"),
        name="retention_mixer",
    )(z, cos, sina, sinb, g)


def _rope_tables(seq):
    half = RET_DK // 2
    pos = jnp.arange(seq, dtype=F32)
    inv_freq = RET_ROPE_BASE ** (-jnp.arange(half, dtype=F32) * 2.0 / RET_DK)
    ang = pos[:, None] * inv_freq[None, :]
    cos = jnp.tile(jnp.cos(ang), (1, LANES // half))
    sin = jnp.tile(jnp.sin(ang), (1, LANES // half))
    first_half = (jnp.arange(LANES) % RET_DK) < half
    return cos, jnp.where(first_half, -sin, 0.0), jnp.where(first_half, 0.0, sin)


def _swa_kernel(sink_ref, cur_ref, prev_ref, o_ref, s_ref, *, tiles_per_seq):
    W = SWA_WINDOW
    HD = SWA_HD
    tt = cur_ref.shape[0]
    first_key = jnp.where(pl.program_id(0) % tiles_per_seq == 0, W, 0)
    sj = _iota((2 * W, W), 0)
    qi = _iota((2 * W, W), 1)
    in_window = (sj > qi) & (sj <= qi + W)
    kv0 = SWA_HEADS * HD
    no_head = jnp.zeros((HD, W), BF16)
    n_blocks = tt // W

    def band(c, col0):
        prev = prev_ref[:, col0:col0 + 128] if c == 0 else cur_ref[(c - 1) * W:c * W, col0:col0 + 128]
        return jnp.concatenate([prev, cur_ref[c * W:(c + 1) * W, col0:col0 + 128]], axis=0)

    for c in range(n_blocks):
        mask = in_window & (sj >= first_key) if c == 0 else in_window
        k_band = band(c, kv0)
        q_t = cur_ref[c * W:(c + 1) * W, 0:kv0].astype(F32).T.astype(BF16)
        for hh in range(SWA_HEADS):
            kk = hh // (SWA_HEADS // SWA_KV_HEADS)
            qh_t = q_t[hh * HD:(hh + 1) * HD]
            q_on_kv = jnp.concatenate([qh_t, no_head] if kk == 0 else [no_head, qh_t], axis=0)
            s = _dot(k_band, q_on_kv) * (HD ** -0.5)
            s_ref[c, hh] = jnp.where(mask, s, NEG_INF)
    for c in range(n_blocks):
        v_t = band(c, kv0 + 128).astype(F32).T.astype(BF16)
        outs = []
        for hh in range(SWA_HEADS):
            kk = hh // (SWA_HEADS // SWA_KV_HEADS)
            s = s_ref[c, hh]
            sink = sink_ref[hh]
            m = jnp.maximum(jnp.max(s, axis=0, keepdims=True), sink)
            p = jnp.exp(s - m)
            denom = jnp.sum(p, axis=0, keepdims=True) + jnp.exp(sink - m)
            outs.append(_dot(v_t[kk * HD:(kk + 1) * HD], p.astype(BF16)) / denom)
        o_ref[c * W:(c + 1) * W, :] = jnp.concatenate(outs, axis=0).T.astype(BF16)


def _swa(z, sinks, seq):
    T = z.shape[0]
    tt = TOK_TILE
    per = tt // SWA_WINDOW
    return pl.pallas_call(
        functools.partial(_swa_kernel, tiles_per_seq=seq // tt),
        grid=(T // tt,),
        in_specs=[pl.BlockSpec(memory_space=pltpu.SMEM),
                  pl.BlockSpec((tt, SWA_W), lambda i: (i, 0)),
                  pl.BlockSpec((SWA_WINDOW, SWA_W), lambda i: (jnp.maximum(i * per - 1, 0), 0))],
        out_specs=pl.BlockSpec((tt, 256), lambda i: (i, 0)),
        out_shape=jax.ShapeDtypeStruct((T, 256), BF16),
        scratch_shapes=[pltpu.VMEM((per, SWA_HEADS, 2 * SWA_WINDOW, SWA_WINDOW), F32)],
        compiler_params=_params("parallel"),
        name="swa_mixer",
    )(sinks, z, z)


_KMEAN_BLOCKS = 8


def _kmean_kernel(k_ref, o_ref):
    k = k_ref[...].astype(F32).reshape(_KMEAN_BLOCKS, MOBA_BLOCK, 256)
    o_ref[...] = jnp.mean(k, axis=1)


def _moba_kmean(z):
    T = z.shape[0]
    rows = _KMEAN_BLOCKS * MOBA_BLOCK
    return pl.pallas_call(
        _kmean_kernel,
        grid=(T // rows,),
        in_specs=[pl.BlockSpec((rows, 256), lambda i: (i, 1))],
        out_specs=pl.BlockSpec((_KMEAN_BLOCKS, 256), lambda i: (i, 0)),
        out_shape=jax.ShapeDtypeStruct((T // MOBA_BLOCK, 256), F32),
        compiler_params=_params("parallel"),
        name="moba_kmean",
    )(z)


MOBA_VROWS = 80


def _moba_prep_kernel(z_ref, km_ref, qt_out, k_out, vt_out):
    HD = MOBA_HD
    tq = z_ref.shape[0]
    nb = km_ref.shape[0]
    qb = pl.program_id(0) % nb
    blk_i = _iota((nb, tq), 0)
    blk = blk_i.astype(F32)
    pad = LANES - HD - nb
    this_block_lanes = (_iota((tq, nb), 1) == qb).astype(BF16)
    ones_row = (_iota((MOBA_VROWS - HD, tq), 0) == 0).astype(BF16)
    for hh in range(MOBA_HEADS):
        cols = slice(hh * HD, (hh + 1) * HD)
        q_t = z_ref[:, cols].astype(F32).T
        gate = _dot(km_ref[:, cols], q_t, precision=HIGHEST)
        gate = jnp.where(blk_i < qb, gate, NEG_INF)
        keep = jnp.where(blk_i == qb, 1.0, 0.0)
        for _ in range(MOBA_TOPK):
            best = jnp.max(gate, axis=0, keepdims=True)
            first = jnp.min(jnp.where(gate == best, blk, float(nb)), axis=0, keepdims=True)
            hit = blk == first
            keep = keep + jnp.where(hit, jnp.where(best > 0.5 * NEG_INF, 1.0, 0.0), 0.0)
            gate = jnp.where(hit, BELOW_NEG_INF, gate)
        bias = ((keep - 1.0) * (-NEG_INF)).astype(BF16)
        qt_out[hh, 0] = jnp.concatenate([(q_t * (HD ** -0.5)).astype(BF16), bias, jnp.zeros((pad, tq), BF16)],
                                        axis=0)
        k_out[hh] = jnp.concatenate([z_ref[:, 256 + hh * HD:256 + (hh + 1) * HD], this_block_lanes,
                                     jnp.zeros((tq, pad), BF16)], axis=1)
        v_t = z_ref[:, 512 + hh * HD:512 + (hh + 1) * HD].astype(F32).T.astype(BF16)
        vt_out[hh, 0] = jnp.concatenate([v_t, ones_row], axis=0)


def _moba_prep(z, kmean, seq):
    T = z.shape[0]
    nb = seq // MOBA_BLOCK
    assert MOBA_HD + nb <= LANES
    tq = MOBA_BLOCK
    H = MOBA_HEADS
    def per_block_t(rows):
        return pl.BlockSpec((H, 1, rows, tq), lambda i: (0, i, 0, 0))

    return pl.pallas_call(
        _moba_prep_kernel,
        grid=(T // tq,),
        in_specs=[pl.BlockSpec((tq, MOBA_W), lambda i: (i, 0)), pl.BlockSpec((nb, 256), lambda i: (i // nb, 0))],
        out_specs=[per_block_t(LANES), pl.BlockSpec((H, tq, LANES), lambda i: (0, i, 0)),
                   per_block_t(MOBA_VROWS)],
        out_shape=[jax.ShapeDtypeStruct((H, T // tq, LANES, tq), BF16), jax.ShapeDtypeStruct((H, T, LANES), BF16),
                   jax.ShapeDtypeStruct((H, T // tq, MOBA_VROWS, tq), BF16)],
        compiler_params=_params("parallel"),
        name="moba_prep",
    )(z, kmean)


MOBA_QBLOCKS = 4
MOBA_KBLOCKS = 2
MOBA_DIAGONALS = MOBA_QBLOCKS // MOBA_KBLOCKS


def _moba_kernel(qidx_ref, kidx_ref, kind_ref, qt_ref, k_ref, vt_ref, o_ref, m_ref, acc_ref, s_ref):
    HD = MOBA_HD
    B = MOBA_BLOCK
    step = pl.program_id(1)
    kind = kind_ref[step]
    is_last = (step + 1 == pl.num_programs(1)) | (qidx_ref[jnp.minimum(step + 1, pl.num_programs(1) - 1)]
                                                  != qidx_ref[step])

    def attend(diagonal):
        causal = _iota((B, B), 0) <= _iota((B, B), 1)

        def key_blocks_of(qi):
            if diagonal is None:
                return list(range(MOBA_KBLOCKS))
            return [c for c in range(MOBA_KBLOCKS) if MOBA_KBLOCKS * diagonal + c <= qi]

        for hh in range(MOBA_HEADS):
            for qi in range(MOBA_QBLOCKS):
                for c in key_blocks_of(qi):
                    s = _dot(k_ref[hh, c * B:(c + 1) * B, :], qt_ref[hh, qi])
                    if diagonal is not None and MOBA_KBLOCKS * diagonal + c == qi:
                        s = jnp.where(causal, s, NEG_INF)
                    s_ref[hh, qi, c] = s
        for hh in range(MOBA_HEADS):
            for qi in range(MOBA_QBLOCKS):
                key_blocks = key_blocks_of(qi)
                if not key_blocks:
                    continue
                scores = [s_ref[hh, qi, c] for c in key_blocks]
                m_new = functools.reduce(jnp.maximum, [jnp.max(s, axis=0, keepdims=True) for s in scores])
                first_visit = diagonal == 0
                if not first_visit:
                    m_old = m_ref[hh, qi]
                    m_new = jnp.maximum(m_old, m_new)
                pv = None
                for c, s in zip(key_blocks, scores):
                    term = _dot(vt_ref[hh, c], jnp.exp(s - m_new).astype(BF16))
                    pv = term if pv is None else pv + term
                acc_ref[hh, qi] = pv if first_visit else jnp.exp(m_old - m_new) * acc_ref[hh, qi] + pv
                m_ref[hh, qi] = m_new

    @pl.when(kind == 0)
    def _():
        attend(None)

    for d in range(MOBA_DIAGONALS):
        @pl.when(kind == 1 + d)
        def _(d=d):
            attend(d)

    @pl.when(is_last)
    def _():
        pad = jnp.zeros((LANES - MOBA_VROWS, B), F32)
        for qi in range(MOBA_QBLOCKS):
            outs = []
            for hh in range(MOBA_HEADS):
                a = jnp.concatenate([acc_ref[hh, qi], pad], axis=0).T
                outs.append(a[:, 0:HD] / a[:, HD:HD + 1])
            o_ref[qi * B:(qi + 1) * B, :] = jnp.concatenate(outs, axis=1).astype(BF16)


def _moba(z, seq):
    T = z.shape[0]
    nb = seq // MOBA_BLOCK
    assert nb % MOBA_QBLOCKS == 0
    q_tiles = nb // MOBA_QBLOCKS
    k_steps = nb // MOBA_KBLOCKS
    D = MOBA_DIAGONALS
    batch = T // seq
    qt_aug, k_aug, vt_aug = _moba_prep(z, _moba_kmean(z), seq)
    qidx = np.concatenate([np.full(D * t + D, t) for t in range(q_tiles)]).astype(np.int32)
    kidx = np.concatenate([np.concatenate([D * t + np.arange(D), np.arange(D * t)])
                           for t in range(q_tiles)]).astype(np.int32)
    kind = np.concatenate([np.concatenate([1 + np.arange(D), np.zeros(D * t, np.int64)])
                           for t in range(q_tiles)]).astype(np.int32)
    tq = MOBA_QBLOCKS * MOBA_BLOCK
    tk = MOBA_KBLOCKS * MOBA_BLOCK
    H = MOBA_HEADS
    grid_spec = pltpu.PrefetchScalarGridSpec(
        num_scalar_prefetch=3,
        grid=(batch, len(qidx)),
        in_specs=[pl.BlockSpec((H, MOBA_QBLOCKS, LANES, MOBA_BLOCK),
                               lambda b, s, qi, ki, kd: (0, b * q_tiles + qi[s], 0, 0)),
                  pl.BlockSpec((H, tk, LANES), lambda b, s, qi, ki, kd: (0, b * k_steps + ki[s], 0)),
                  pl.BlockSpec((H, MOBA_KBLOCKS, MOBA_VROWS, MOBA_BLOCK),
                               lambda b, s, qi, ki, kd: (0, b * k_steps + ki[s], 0, 0))],
        out_specs=pl.BlockSpec((tq, 256), lambda b, s, qi, ki, kd: (b * q_tiles + qi[s], 0)),
        scratch_shapes=[pltpu.VMEM((H, MOBA_QBLOCKS, 1, MOBA_BLOCK), F32),
                        pltpu.VMEM((H, MOBA_QBLOCKS, MOBA_VROWS, MOBA_BLOCK), F32),
                        pltpu.VMEM((H, MOBA_QBLOCKS, MOBA_KBLOCKS, MOBA_BLOCK, MOBA_BLOCK), F32)],
    )
    return pl.pallas_call(
        _moba_kernel,
        grid_spec=grid_spec,
        out_shape=jax.ShapeDtypeStruct((T, 256), BF16),
        compiler_params=_params("parallel", "arbitrary"),
        name="moba_mixer",
    )(jnp.asarray(qidx), jnp.asarray(kidx), jnp.asarray(kind), qt_aug, k_aug, vt_aug)


def _merge_kernel(h_ref, o0, o1, o2, o3, gates_ref, wb_ref, wout_ref, out_ref):
    merged = None
    for i, o in enumerate((o0, o1, o2, o3)):
        gate = _sigmoid(gates_ref[:, i * D_MODEL:(i + 1) * D_MODEL].astype(F32))
        term = gate * _dot(o[...], wb_ref[i])
        merged = term if merged is None else merged + term
    out_ref[...] = h_ref[...] + _dot(merged.astype(BF16), wout_ref[...])


def _merge(h, branches, gates, wb, wout):
    T = h.shape[0]
    tm = TOK_TILE
    return pl.pallas_call(
        _merge_kernel,
        grid=(T // tm,),
        in_specs=[pl.BlockSpec((tm, D_MODEL), lambda i: (i, 0))]
        + [pl.BlockSpec((tm, 256), lambda i: (i, 0))] * 4
        + [pl.BlockSpec((tm, GATE_W), lambda i: (i, 0)),
           pl.BlockSpec((N_BRANCH, 256, D_MODEL), lambda i: (0, 0, 0), pipeline_mode=pl.Buffered(1)),
           pl.BlockSpec((D_MODEL, D_MODEL), lambda i: (0, 0), pipeline_mode=pl.Buffered(1))],
        out_specs=pl.BlockSpec((tm, D_MODEL), lambda i: (i, 0)),
        out_shape=jax.ShapeDtypeStruct((T, D_MODEL), F32),
        compiler_params=_params("parallel"),
        name="merge_out_proj",
    )(h, *branches, gates, wb, wout)


ROW_HALF = D_MODEL // 2
ROW_TILES = ROW_HALF // LANES
HIGH_HALF_MASK = -65536


def _row_tile_spec(rows, index_map):
    return pl.BlockSpec((rows * ROW_TILES, LANES), index_map)


def _bf16_bits(x):
    return lax.bitcast_convert_type(x.astype(BF16).astype(F32), I32)


def _pack_rows(ref, x):
    for j in range(ROW_TILES):
        low = _bf16_bits(x[:, j * LANES:(j + 1) * LANES])
        high = _bf16_bits(x[:, ROW_HALF + j * LANES:ROW_HALF + (j + 1) * LANES])
        ref[pl.ds(j, x.shape[0], stride=ROW_TILES), :] = high | lax.shift_right_logical(low, 16)


def _unpack_slab(ref, j):
    words = ref[pl.ds(j, ref.shape[0] // ROW_TILES, stride=ROW_TILES), :]
    low = lax.bitcast_convert_type(lax.shift_left(words, 16), F32)
    high = lax.bitcast_convert_type(words & HIGH_HALF_MASK, F32)
    return low, high


def _unpack_rows(ref):
    halves = [_unpack_slab(ref, j) for j in range(ROW_TILES)]
    return jnp.concatenate([lo for lo, _ in halves] + [hi for _, hi in halves], axis=1).astype(BF16)


def _router_kernel(h_ref, g_ref, wr_ref, br_ref, u_ref, route_ref, route_t_ref, count_ref, carry_ref):
    tm = h_ref.shape[0]

    @pl.when(pl.program_id(0) == 0)
    def _():
        carry_ref[...] = jnp.zeros_like(carry_ref)

    u = _rms(h_ref[...], g_ref[...])
    _pack_rows(u_ref, u)
    u_hi, u_lo = _split_bf16(u, 2)
    w_hi, w_lo = _split_bf16(wr_ref[...], 2)
    logits = _dot(u_hi, w_hi) + (_dot(u_lo, w_hi) + _dot(u_hi, w_lo)) + br_ref[...]
    lt = logits.T
    row_i = _iota((LANES, tm), 0)
    row = row_i.astype(F32)
    row_group = ((row_i - EXPERT_LANE0) >> 3).astype(F32)

    def first_row(mask):
        return jnp.min(jnp.where(mask, row, float(LANES)), axis=0, keepdims=True)

    grp = jnp.where(row_i < N_GROUPS, lt, BELOW_NEG_INF)
    grp_e = jnp.exp(grp - jnp.max(grp, axis=0, keepdims=True))
    p_group = grp_e / jnp.sum(grp_e, axis=0, keepdims=True)
    g_w = jnp.max(p_group, axis=0, keepdims=True)
    g_sel = first_row(p_group == g_w)

    in_group = (row_i >= EXPERT_LANE0) & (row_group == g_sel)
    el = jnp.where(in_group, lt, BELOW_NEG_INF)
    top1 = jnp.max(el, axis=0, keepdims=True)
    row1 = first_row(el == top1)
    el = jnp.where(row == row1, BELOW_NEG_INF, el)
    top2 = jnp.max(el, axis=0, keepdims=True)
    row2 = first_row(el == top2)
    e2 = jnp.exp(top2 - top1)
    w1 = g_w / (1.0 + e2)
    w2 = g_w * e2 / (1.0 + e2)

    onehot = ((row == row1) | (row == row2)).astype(BF16)
    earlier = (_iota((tm, tm), 0) < _iota((tm, tm), 1)).astype(BF16)
    seen = _dot(onehot, earlier) + carry_ref[:, 0:1]
    rank1 = jnp.sum(jnp.where(row == row1, seen, 0.0), axis=0, keepdims=True)
    rank2 = jnp.sum(jnp.where(row == row2, seen, 0.0), axis=0, keepdims=True)
    carry_ref[...] += jnp.sum(onehot.astype(F32), axis=1, keepdims=True)
    count_ref[...] = carry_ref[...]

    out_t = jnp.zeros((LANES, tm), F32)
    for idx, val in enumerate((row1 - EXPERT_LANE0, row2 - EXPERT_LANE0, w1, w2, rank1, rank2)):
        out_t = jnp.where(row_i == idx, val, out_t)
    route_t_ref[...] = out_t[0:8, :]
    route_ref[...] = out_t.T


def _router(h, g, wr, br):
    T = h.shape[0]
    tm = TOK_TILE
    return pl.pallas_call(
        _router_kernel,
        grid=(T // tm,),
        in_specs=[pl.BlockSpec((tm, D_MODEL), lambda i: (i, 0)), _full((1, D_MODEL)),
                  _full((D_MODEL, LANES)), _full((1, LANES))],
        out_specs=[_row_tile_spec(tm, lambda i: (i, 0)), pl.BlockSpec((tm, LANES), lambda i: (i, 0)),
                   pl.BlockSpec((8, tm), lambda i: (0, i)), _full((LANES, LANES))],
        out_shape=[jax.ShapeDtypeStruct((T * ROW_TILES, LANES), I32), jax.ShapeDtypeStruct((T, LANES), F32),
                   jax.ShapeDtypeStruct((8, T), F32), jax.ShapeDtypeStruct((LANES, LANES), F32)],
        scratch_shapes=[pltpu.VMEM((LANES, LANES), F32)],
        compiler_params=_params("arbitrary"),
        name="moe_router",
    )(h, g, wr, br)


ROW_UNROLL = 8


def _row_copy(src_ref, src_row, dst_ref, dst_row, sem):
    src = src_ref.at[pl.ds(pl.multiple_of(src_row * ROW_TILES, ROW_TILES), ROW_TILES)]
    dst = dst_ref.at[pl.ds(pl.multiple_of(dst_row * ROW_TILES, ROW_TILES), ROW_TILES)]
    return pltpu.make_async_copy(src, dst, sem)


def _for_each_row_pair(rows, body):
    def group(g, carry):
        for i in range(ROW_UNROLL):
            for k in range(EXPERT_TOPK):
                body(g * ROW_UNROLL + i, k)
        return carry

    lax.fori_loop(0, rows // ROW_UNROLL, group, 0)


def _dispatch_kernel(dest0_ref, dest1_ref, pstart_ref, pend_ref, u_ref, xs_ref, zero_ref, sem):
    tm = u_ref.shape[0] // ROW_TILES
    tile = pl.program_id(0)
    dests = (dest0_ref, dest1_ref)

    @pl.when(tile == 0)
    def _():
        zero_ref[...] = jnp.zeros_like(zero_ref)

        def zero_block(first_row):
            start = pl.multiple_of(first_row * ROW_TILES, ROW_TILES)
            fill = pltpu.make_async_copy(zero_ref, xs_ref.at[pl.ds(start, MOE_BLOCK * ROW_TILES)], sem)
            fill.start()
            fill.wait()

        def zero_tail(e, carry):
            @pl.when(pend_ref[e] > pstart_ref[e])
            def _():
                zero_block(pend_ref[e] - MOE_BLOCK)
            return carry

        def zero_unused(b, carry):
            zero_block(b * MOE_BLOCK)
            return carry

        lax.fori_loop(0, N_EXPERTS, zero_tail, 0)
        lax.fori_loop(pend_ref[N_EXPERTS - 1] // MOE_BLOCK, xs_ref.shape[0] // (MOE_BLOCK * ROW_TILES),
                      zero_unused, 0)

    def issue(r, k):
        _row_copy(u_ref, r, xs_ref, dests[k][tile * tm + r], sem).start(priority=k)

    _for_each_row_pair(tm, issue)
    for _ in range(EXPERT_TOPK):
        pltpu.make_async_copy(u_ref, xs_ref.at[pl.ds(0, tm * ROW_TILES)], sem).wait()


def _dispatch(dest0, dest1, pstart, pend, u, n_slots):
    T = u.shape[0] // ROW_TILES
    tm = DISPATCH_TILE
    grid_spec = pltpu.PrefetchScalarGridSpec(
        num_scalar_prefetch=4,
        grid=(T // tm,),
        in_specs=[_row_tile_spec(tm, lambda i, *_: (i, 0))],
        out_specs=pl.BlockSpec(memory_space=pl.ANY),
        scratch_shapes=[pltpu.VMEM((MOE_BLOCK * ROW_TILES, LANES), I32), pltpu.SemaphoreType.DMA(())],
    )
    return pl.pallas_call(
        _dispatch_kernel,
        grid_spec=grid_spec,
        out_shape=jax.ShapeDtypeStruct((n_slots * ROW_TILES, LANES), I32),
        compiler_params=_params("arbitrary"),
        name="moe_dispatch",
    )(dest0, dest1, pstart, pend, u)


def _expert_kernel(blk_e_ref, n_used_ref, x_ref, w1_ref, w3_ref, w2_ref, y_ref, w1b_ref, w3b_ref, w2b_ref):
    i = pl.program_id(0)
    used = i < n_used_ref[0]

    @pl.when((i == 0) | (blk_e_ref[i] != blk_e_ref[jnp.maximum(i - 1, 0)]))
    def _():
        w1b_ref[...] = w1_ref[0, 0].astype(BF16)
        w3b_ref[...] = w3_ref[0, 0].astype(BF16)
        w2b_ref[...] = w2_ref[0, 0].astype(BF16)

    @pl.when(used)
    def _():
        x = _unpack_rows(x_ref)
        a = _dot(x, w1b_ref[...])
        hidden = (a * _sigmoid(a)) * _dot(x, w3b_ref[...])
        _pack_rows(y_ref, _dot(hidden.astype(BF16), w2b_ref[...]))

    @pl.when(jnp.logical_not(used))
    def _():
        y_ref[...] = jnp.zeros_like(y_ref)


def _experts(blk_e, n_used, xs, w1, w3, w2, layer):
    n_slots = xs.shape[0] // ROW_TILES
    bm = MOE_BLOCK
    grid_spec = pltpu.PrefetchScalarGridSpec(
        num_scalar_prefetch=2,
        grid=(n_slots // bm,),
        in_specs=[_row_tile_spec(bm, lambda i, be, nu: (jnp.minimum(i, nu[0] - 1), 0)),
                  pl.BlockSpec((1, 1, D_MODEL, EXPERT_HIDDEN), lambda i, be, nu: (layer, be[i], 0, 0)),
                  pl.BlockSpec((1, 1, D_MODEL, EXPERT_HIDDEN), lambda i, be, nu: (layer, be[i], 0, 0)),
                  pl.BlockSpec((1, 1, EXPERT_HIDDEN, D_MODEL), lambda i, be, nu: (layer, be[i], 0, 0))],
        out_specs=_row_tile_spec(bm, lambda i, be, nu: (i, 0)),
        scratch_shapes=[pltpu.VMEM((D_MODEL, EXPERT_HIDDEN), BF16), pltpu.VMEM((D_MODEL, EXPERT_HIDDEN), BF16),
                        pltpu.VMEM((EXPERT_HIDDEN, D_MODEL), BF16)],
    )
    return pl.pallas_call(
        _expert_kernel,
        grid_spec=grid_spec,
        out_shape=jax.ShapeDtypeStruct((n_slots * ROW_TILES, LANES), I32),
        compiler_params=_params("arbitrary"),
        name="moe_experts",
    )(blk_e, n_used, xs, w1, w3, w2)


def _combine_kernel(dest0_ref, dest1_ref, h_ref, route_ref, g_ref, ys_ref, out_ref, y_ref, sem, *, normalize):
    tm = h_ref.shape[0]
    step = pl.program_id(0)
    slot = step % 2

    dests = (dest0_ref, dest1_ref)

    def gather_tile(tile, buf):
        def issue(r, k):
            _row_copy(ys_ref, dests[k][tile * tm + r], y_ref.at[buf, k], r, sem.at[buf]).start(priority=k)

        _for_each_row_pair(tm, issue)

    @pl.when(step == 0)
    def _():
        gather_tile(0, 0)

    @pl.when(step + 1 < pl.num_programs(0))
    def _():
        gather_tile(step + 1, 1 - slot)

    for k in range(EXPERT_TOPK):
        pltpu.make_async_copy(ys_ref.at[pl.ds(0, tm * ROW_TILES)], y_ref.at[slot, k], sem.at[slot]).wait()
    w0 = route_ref[:, 2:3]
    w1 = route_ref[:, 3:4]
    for j in range(ROW_TILES):
        low0, high0 = _unpack_slab(y_ref.at[slot, 0], j)
        low1, high1 = _unpack_slab(y_ref.at[slot, 1], j)
        cols = slice(j * LANES, (j + 1) * LANES)
        out_ref[:, cols] = h_ref[:, cols] + w0 * low0 + w1 * low1
        cols = slice(ROW_HALF + j * LANES, ROW_HALF + (j + 1) * LANES)
        out_ref[:, cols] = h_ref[:, cols] + w0 * high0 + w1 * high1
    if normalize:
        out_ref[...] = _rms(out_ref[...], g_ref[...])


def _combine(dest0, dest1, h, route, ys, final_g, normalize):
    T = h.shape[0]
    tm = ROW_TILE
    grid_spec = pltpu.PrefetchScalarGridSpec(
        num_scalar_prefetch=2,
        grid=(T // tm,),
        in_specs=[pl.BlockSpec((tm, D_MODEL), lambda i, *_: (i, 0)),
                  pl.BlockSpec((tm, LANES), lambda i, *_: (i, 0)),
                  _full((1, D_MODEL)),
                  pl.BlockSpec(memory_space=pl.ANY)],
        out_specs=pl.BlockSpec((tm, D_MODEL), lambda i, *_: (i, 0)),
        scratch_shapes=[pltpu.VMEM((2, EXPERT_TOPK, tm * ROW_TILES, LANES), I32),
                        pltpu.SemaphoreType.DMA((2,))],
    )
    return pl.pallas_call(
        functools.partial(_combine_kernel, normalize=normalize),
        grid_spec=grid_spec,
        out_shape=jax.ShapeDtypeStruct((T, D_MODEL), F32),
        compiler_params=_params("arbitrary"),
        name="moe_combine",
    )(dest0, dest1, h, route, final_g, ys)


def _moe(h, g, w_group, b_group, w_expert, b_expert, w1, w3, w2, layer, final_g, normalize):
    T = h.shape[0]
    wr = jnp.zeros((D_MODEL, LANES), F32)
    wr = wr.at[:, :N_GROUPS].set(w_group).at[:, EXPERT_LANE0:EXPERT_LANE0 + N_EXPERTS].set(w_expert)
    br = jnp.zeros((1, LANES), F32)
    br = br.at[0, :N_GROUPS].set(b_group).at[0, EXPERT_LANE0:EXPERT_LANE0 + N_EXPERTS].set(b_expert)
    u, route, route_t, count = _router(h, g, wr, br)

    n_blocks = (T * EXPERT_TOPK) // MOE_BLOCK + N_EXPERTS
    counts = count[EXPERT_LANE0:EXPERT_LANE0 + N_EXPERTS, 0].astype(I32)
    padded = ((counts + MOE_BLOCK - 1) // MOE_BLOCK) * MOE_BLOCK
    pend = jnp.cumsum(padded)
    pstart = pend - padded
    expert_ids = jnp.arange(N_EXPERTS, dtype=F32)[:, None]

    def slot_of(expert_row, rank_row):
        segment_start = jnp.sum(jnp.where(expert_row[None, :] == expert_ids, pstart[:, None], 0), axis=0)
        return segment_start + rank_row.astype(I32)

    dest0 = slot_of(route_t[0], route_t[4])
    dest1 = slot_of(route_t[1], route_t[5])
    block_row0 = jnp.arange(n_blocks, dtype=I32) * MOE_BLOCK
    blk_e = jnp.minimum(jnp.sum((pend[None, :] <= block_row0[:, None]).astype(I32), axis=1), N_EXPERTS - 1)
    n_used = (pend[-1:] // MOE_BLOCK).astype(I32)

    xs = _dispatch(dest0, dest1, pstart, pend, u, n_blocks * MOE_BLOCK)
    ys = _experts(blk_e, n_used, xs, w1, w3, w2, layer)
    return _combine(dest0, dest1, h, route, ys, final_g, normalize)


def _split_w_in(w_in):
    gla_n = GLA_HEADS * (2 * GLA_DK + 2 * GLA_DV) + GLA_LOWRANK
    swa_n = (SWA_HEADS + 2 * SWA_KV_HEADS) * SWA_HD
    moba_n = 3 * MOBA_HEADS * MOBA_HD
    ret_n = RET_HEADS * (2 * RET_DK + 2 * RET_DV)
    offs = np.cumsum([0, gla_n, swa_n, moba_n, ret_n, GATE_W])
    parts = [w_in[:, offs[i]:offs[i + 1]].astype(BF16) for i in range(5)]
    parts[0] = jnp.pad(parts[0], ((0, 0), (0, GLA_W - gla_n)))
    return parts


def _token_mixer(h, seq, ln1_g, w_in, gla_w_a2, gla_b_a, gla_norm_g, swa_sinks, ret_norm_g, w_branch, w_out,
                 rope):
    z_gla, z_swa, z_moba, z_ret, z_gates = _in_proj(h, ln1_g.reshape(1, D_MODEL), _split_w_in(w_in))
    wa = jnp.zeros((128, 128), F32).at[:GLA_LOWRANK].set(gla_w_a2)
    o_gla = _gla(z_gla, wa, gla_b_a.reshape(1, 128), jnp.tile(gla_norm_g, GLA_HEADS).reshape(1, 256), seq)
    o_swa = _swa(z_swa, swa_sinks, seq)
    o_moba = _moba(z_moba, seq)
    o_ret = _retention(z_ret, *rope, jnp.tile(ret_norm_g, RET_HEADS).reshape(1, 256), seq)
    return _merge(h, (o_gla, o_swa, o_moba, o_ret), z_gates, w_branch.astype(BF16), w_out.astype(BF16))


def kernel(x, ln1_g, w_in, gla_w_a2, gla_b_a, gla_norm_g, swa_sinks, ret_norm_g, w_branch, w_out, ln2_g,
           w_group, b_group, w_expert, b_expert, w1, w3, w2, final_g):
    batch, seq, _ = x.shape
    depth = w_in.shape[0]
    assert depth >= 1
    rope = _rope_tables(seq)
    h = x.reshape(batch * seq, D_MODEL)
    for l in range(depth):
        h = _token_mixer(h, seq, ln1_g[l], w_in[l], gla_w_a2[l], gla_b_a[l], gla_norm_g[l], swa_sinks[l],
                         ret_norm_g[l], w_branch[l], w_out[l], rope)
        h = _moe(h, ln2_g[l].reshape(1, D_MODEL), w_group[l], b_group[l], w_expert[l], b_expert[l],
                 w1, w3, w2, l, final_g.reshape(1, D_MODEL), normalize=(l == depth - 1))
    return h.reshape(batch, seq, D_MODEL)
```

```python
import functools
import math

import numpy as np
import jax
import jax.numpy as jnp
from jax import lax
from jax.experimental import pallas as pl
from jax.experimental.pallas import tpu as pltpu

F32 = jnp.float32
BF16 = jnp.bfloat16
I32 = jnp.int32
HIGHEST = lax.Precision.HIGHEST

D_MODEL = 1024
N_BRANCH = 4
NORM_EPS = 1e-6
NEG_INF = -1e30
BELOW_NEG_INF = -3e38

GLA_HEADS, GLA_DK, GLA_DV, GLA_LOWRANK, GLA_TAU, GLA_CHUNK = 4, 32, 64, 16, 16.0, 64
SWA_HEADS, SWA_KV_HEADS, SWA_HD, SWA_WINDOW = 4, 2, 64, 128
MOBA_HEADS, MOBA_HD, MOBA_BLOCK, MOBA_TOPK = 4, 64, 256, 3
RET_HEADS, RET_DK, RET_DV, RET_CHUNK, RET_ROPE_BASE = 4, 32, 64, 128, 10000.0
N_GROUPS, EXPERTS_PER_GROUP, EXPERT_TOPK, EXPERT_HIDDEN = 4, 8, 2, 256
MOE_BLOCK = 512
N_EXPERTS = N_GROUPS * EXPERTS_PER_GROUP

GLA_W = 896
SWA_W = 512
MOBA_W = 768
RET_W = 768
GATE_W = N_BRANCH * D_MODEL

LANES = 128
EXPERT_LANE0 = 32

VMEM_LIMIT = 56 * 1024 * 1024

TOK_TILE = 512
ROW_TILE = 512
DISPATCH_TILE = 1024


def _params(*sem):
    return pltpu.CompilerParams(dimension_semantics=sem, vmem_limit_bytes=VMEM_LIMIT)


def _dot(a, b, precision=None):
    return jnp.dot(a, b, preferred_element_type=F32, precision=precision)


def _dot_nt(a, b, precision=None):
    return lax.dot_general(a, b, (((1,), (1,)), ((), ())), preferred_element_type=F32, precision=precision)


def _dot_tn(a, b, precision=None):
    return lax.dot_general(a, b, (((0,), (0,)), ((), ())), preferred_element_type=F32, precision=precision)


def _split_bf16(x, parts):
    out = []
    for _ in range(parts - 1):
        piece = x.astype(BF16)
        out.append(piece)
        x = x - piece.astype(F32)
    out.append(x.astype(BF16))
    return out


def _dot_f32_by_bf16(x, m):
    return sum(_dot(piece, m) for piece in _split_bf16(x, 3))


def _dot_bf16_by_f32(m, x):
    return sum(_dot(m, piece) for piece in _split_bf16(x, 3))


def _iota(shape, dim):
    return lax.broadcasted_iota(I32, shape, dim)


def _sigmoid(x):
    return 0.5 * jnp.tanh(0.5 * x) + 0.5


def _rms(x, g):
    ms = jnp.mean(x * x, axis=-1, keepdims=True)
    return x * lax.rsqrt(ms + NORM_EPS) * g


def _full(shape):
    return pl.BlockSpec(shape, lambda *_: (0,) * len(shape))


def _in_proj_kernel(h_ref, g_ref, w0, w1, w2, w3, w4, o0, o1, o2, o3, o4):
    u = _rms(h_ref[...], g_ref[...]).astype(BF16)
    for w, o in ((w0, o0), (w1, o1), (w2, o2), (w3, o3), (w4, o4)):
        o[...] = _dot(u, w[...]).astype(BF16)


def _in_proj(h, g, ws):
    T = h.shape[0]
    tm = TOK_TILE
    widths = [w.shape[1] for w in ws]
    return pl.pallas_call(
        _in_proj_kernel,
        grid=(T // tm,),
        in_specs=[pl.BlockSpec((tm, D_MODEL), lambda i: (i, 0)), _full((1, D_MODEL))]
        + [pl.BlockSpec((D_MODEL, n), lambda i: (0, 0), pipeline_mode=pl.Buffered(1)) for n in widths],
        out_specs=[pl.BlockSpec((tm, n), lambda i: (i, 0)) for n in widths],
        out_shape=[jax.ShapeDtypeStruct((T, n), BF16) for n in widths],
        compiler_params=_params("parallel"),
        name="in_proj",
    )(h, g, *ws)


def _head_select(x, lane_head, n_heads, rows):
    out = None
    for hh in range(n_heads):
        term = jnp.where(lane_head == hh, x[hh * rows:(hh + 1) * rows], 0.0)
        out = term if out is None else out + term
    return out


def _head_norm_gate(o, gate_in, g):
    n = o.shape[1]
    same_head = (_iota((n, n), 0) >> 6) == (_iota((n, n), 1) >> 6)
    ms = _dot_f32_by_bf16(o * o, same_head.astype(BF16)) * (1.0 / 64.0)
    return o * lax.rsqrt(ms + NORM_EPS) * g * (gate_in * _sigmoid(gate_in))


def _gla_kernel(z_ref, wa_ref, ba_ref, g_ref, o_ref, st_ref, oacc_ref, att_ref, kv_ref, *, tiles_per_seq):
    C = GLA_CHUNK
    tt = z_ref.shape[0]

    @pl.when(pl.program_id(0) % tiles_per_seq == 0)
    def _():
        st_ref[...] = jnp.zeros_like(st_ref)

    q = z_ref[:, 0:128].astype(F32)
    k = z_ref[:, 128:256].astype(F32)
    r = z_ref[:, 512:768].astype(F32)
    pre = _dot_bf16_by_f32(z_ref[:, 768:896], wa_ref[...]) + ba_ref[...]
    log_a = (jnp.minimum(pre, 0.0) - jnp.log(1.0 + jnp.exp(-jnp.abs(pre)))) * (1.0 / GLA_TAU)

    lower = (_iota((C, C), 1) <= _iota((C, C), 0)).astype(BF16)
    b = jnp.concatenate([_dot_bf16_by_f32(lower, log_a[c * C:(c + 1) * C]) for c in range(tt // C)], axis=0)
    b_last = jnp.broadcast_to(b.reshape(tt // C, C, 128)[:, C - 1:C, :], (tt // C, C, 128)).reshape(tt, 128)
    q_dec = (q * (GLA_DK ** -0.5) * jnp.exp(b)).astype(BF16)
    k_dec = (k * jnp.exp(-b)).astype(BF16)
    k_end = (k * jnp.exp(b_last - b)).astype(BF16)
    decay = jnp.exp(b_last)

    qk_head = _iota((C, 128), 1) >> 5
    v_head = _iota((C, 256), 1) >> 6
    causal = _iota((4 * C, C), 1) <= (_iota((4 * C, C), 0) & (C - 1))
    state_mask = (_iota((256, 128), 0) >> 6) == (_iota((256, 128), 1) >> 5)

    for c in range(tt // C):
        sl = slice(c * C, (c + 1) * C)
        q_stack = jnp.concatenate([jnp.where(qk_head == hh, q_dec[sl], 0) for hh in range(GLA_HEADS)], axis=0)
        att_ref[c] = jnp.where(causal, _dot_nt(q_stack, k_dec[sl]), 0.0).astype(BF16)
        kv_ref[c] = jnp.where(state_mask, _dot_tn(z_ref[sl, 256:512], k_end[sl]), 0.0)
    state = st_ref[...]
    for c in range(tt // C):
        sl = slice(c * C, (c + 1) * C)
        intra = _head_select(_dot(att_ref[c], z_ref[sl, 256:512]), v_head, GLA_HEADS, C)
        inter = _dot_nt(q_dec[sl], state.astype(BF16))
        oacc_ref[sl, :] = intra + inter
        state = state * decay[c * C:c * C + 1] + kv_ref[c]
    st_ref[...] = state

    o_ref[...] = _head_norm_gate(oacc_ref[...], r, g_ref[...]).astype(BF16)


def _gla(z, wa, ba, g, seq):
    T = z.shape[0]
    tt = TOK_TILE
    return pl.pallas_call(
        functools.partial(_gla_kernel, tiles_per_seq=seq // tt),
        grid=(T // tt,),
        in_specs=[pl.BlockSpec((tt, GLA_W), lambda i: (i, 0)), _full((128, 128)), _full((1, 128)),
                  _full((1, 256))],
        out_specs=pl.BlockSpec((tt, 256), lambda i: (i, 0)),
        out_shape=jax.ShapeDtypeStruct((T, 256), BF16),
        scratch_shapes=[pltpu.VMEM((256, 128), F32), pltpu.VMEM((tt, 256), F32),
                        pltpu.VMEM((tt // GLA_CHUNK, GLA_HEADS * GLA_CHUNK, GLA_CHUNK), BF16),
                        pltpu.VMEM((tt // GLA_CHUNK, 256, 128), F32)],
        compiler_params=_params("arbitrary"),
        name="gla_mixer",
    )(z, wa, ba, g)


_RET_LOG_GAMMA = [math.log1p(-(2.0 ** (-5.0 - hh))) for hh in range(RET_HEADS)]


def _by_head(head, values):
    out = jnp.full(head.shape, values[-1], F32)
    for hh in range(len(values) - 2, -1, -1):
        out = jnp.where(head == hh, values[hh], out)
    return out


def _ret_kernel(z_ref, cos_ref, sina_ref, sinb_ref, g_ref, o_ref, st_ref, oacc_ref, *, tiles_per_seq):
    C = RET_CHUNK
    tt = z_ref.shape[0]

    @pl.when(pl.program_id(0) % tiles_per_seq == 0)
    def _():
        st_ref[...] = jnp.zeros_like(st_ref)

    cos, sina, sinb = cos_ref[...], sina_ref[...], sinb_ref[...]

    def rotate(t):
        return t * cos + pltpu.roll(t, 112, 1) * sina + pltpu.roll(t, 16, 1) * sinb

    q_rot = rotate(z_ref[:, 0:128].astype(F32))
    k_rot = rotate(z_ref[:, 128:256].astype(F32)) * (RET_DK ** -0.5)
    gate_in = z_ref[:, 512:768].astype(F32)

    qk_head = _iota((C, 128), 1) >> 5
    v_head = _iota((C, 256), 1) >> 6
    lg_qk = _by_head(_iota((1, 128), 1) >> 5, _RET_LOG_GAMMA)
    lg_v = _by_head(_iota((1, 256), 1) >> 6, _RET_LOG_GAMMA)
    pos_qk = _iota((C, 128), 0).astype(F32)
    pos_v = _iota((C, 256), 0).astype(F32)
    key_decay = jnp.exp((C - 1.0 - pos_qk) * lg_qk)
    query_decay = jnp.exp((pos_v + 1.0) * lg_v)
    chunk_decay = jnp.exp(float(C) * lg_qk)
    srow = _iota((4 * C, C), 0)
    rel = ((srow & (C - 1)) - _iota((4 * C, C), 1)).astype(F32)
    decay_mat = jnp.where(rel >= 0, jnp.exp(jnp.maximum(rel, 0.0) * _by_head(srow >> 7, _RET_LOG_GAMMA)), 0.0)
    state_mask = (_iota((256, 128), 0) >> 6) == (_iota((256, 128), 1) >> 5)

    q_bf = q_rot.astype(BF16)
    k_bf = k_rot.astype(BF16)
    state = st_ref[...]
    for c in range(tt // C):
        sl = slice(c * C, (c + 1) * C)
        qc = q_bf[sl]
        v = z_ref[sl, 256:512]
        q_stack = jnp.concatenate([jnp.where(qk_head == hh, qc, 0) for hh in range(RET_HEADS)], axis=0)
        att = _dot_nt(q_stack, k_bf[sl]) * decay_mat
        intra = _head_select(_dot(att.astype(BF16), v), v_head, RET_HEADS, C)
        inter = _dot_nt(qc, state.astype(BF16)) * query_decay
        oacc_ref[sl, :] = intra + inter
        kv = _dot_tn(v, (k_rot[sl] * key_decay).astype(BF16))
        state = state * chunk_decay + jnp.where(state_mask, kv, 0.0)
    st_ref[...] = state

    o_ref[...] = _head_norm_gate(oacc_ref[...], gate_in, g_ref[...]).astype(BF16)


def _retention(z, cos, sina, sinb, g, seq):
    T = z.shape[0]
    tt = TOK_TILE
    tps = seq // tt
    pos_spec = pl.BlockSpec((tt, 128), lambda i: (i % tps, 0))
    return pl.pallas_call(
        functools.partial(_ret_kernel, tiles_per_seq=tps),
        grid=(T // tt,),
        in_specs=[pl.BlockSpec((tt, RET_W), lambda i: (i, 0)), pos_spec, pos_spec, pos_spec, _full((1, 256))],
        out_specs=pl.BlockSpec((tt, 256), lambda i: (i, 0)),
        out_shape=jax.ShapeDtypeStruct((T, 256), BF16),
        scratch_shapes=[pltpu.VMEM((256, 128), F32), pltpu.VMEM((tt, 256), F32)],
        compiler_params=_params("arbitrary"),
        name="retention_mixer",
    )(z, cos, sina, sinb, g)


def _rope_tables(seq):
    half = RET_DK // 2
    pos = jnp.arange(seq, dtype=F32)
    inv_freq = RET_ROPE_BASE ** (-jnp.arange(half, dtype=F32) * 2.0 / RET_DK)
    ang = pos[:, None] * inv_freq[None, :]
    cos = jnp.tile(jnp.cos(ang), (1, LANES // half))
    sin = jnp.tile(jnp.sin(ang), (1, LANES // half))
    first_half = (jnp.arange(LANES) % RET_DK) < half
    return cos, jnp.where(first_half, -sin, 0.0), jnp.where(first_half, 0.0, sin)


def _swa_kernel(sink_ref, cur_ref, prev_ref, o_ref, s_ref, *, tiles_per_seq):
    W = SWA_WINDOW
    HD = SWA_HD
    tt = cur_ref.shape[0]
    first_key = jnp.where(pl.program_id(0) % tiles_per_seq == 0, W, 0)
    sj = _iota((2 * W, W), 0)
    qi = _iota((2 * W, W), 1)
    in_window = (sj > qi) & (sj <= qi + W)
    kv0 = SWA_HEADS * HD
    no_head = jnp.zeros((HD, W), BF16)
    n_blocks = tt // W

    def band(c, col0):
        prev = prev_ref[:, col0:col0 + 128] if c == 0 else cur_ref[(c - 1) * W:c * W, col0:col0 + 128]
        return jnp.concatenate([prev, cur_ref[c * W:(c + 1) * W, col0:col0 + 128]], axis=0)

    for c in range(n_blocks):
        mask = in_window & (sj >= first_key) if c == 0 else in_window
        k_band = band(c, kv0)
        q_t = cur_ref[c * W:(c + 1) * W, 0:kv0].astype(F32).T.astype(BF16)
        for hh in range(SWA_HEADS):
            kk = hh // (SWA_HEADS // SWA_KV_HEADS)
            qh_t = q_t[hh * HD:(hh + 1) * HD]
            q_on_kv = jnp.concatenate([qh_t, no_head] if kk == 0 else [no_head, qh_t], axis=0)
            s = _dot(k_band, q_on_kv) * (HD ** -0.5)
            s_ref[c, hh] = jnp.where(mask, s, NEG_INF)
    for c in range(n_blocks):
        v_t = band(c, kv0 + 128).astype(F32).T.astype(BF16)
        outs = []
        for hh in range(SWA_HEADS):
            kk = hh // (SWA_HEADS // SWA_KV_HEADS)
            s = s_ref[c, hh]
            sink = sink_ref[hh]
            m = jnp.maximum(jnp.max(s, axis=0, keepdims=True), sink)
            p = jnp.exp(s - m)
            denom = jnp.sum(p, axis=0, keepdims=True) + jnp.exp(sink - m)
            outs.append(_dot(v_t[kk * HD:(kk + 1) * HD], p.astype(BF16)) / denom)
        o_ref[c * W:(c + 1) * W, :] = jnp.concatenate(outs, axis=0).T.astype(BF16)


def _swa(z, sinks, seq):
    T = z.shape[0]
    tt = TOK_TILE
    per = tt // SWA_WINDOW
    return pl.pallas_call(
        functools.partial(_swa_kernel, tiles_per_seq=seq // tt),
        grid=(T // tt,),
        in_specs=[pl.BlockSpec(memory_space=pltpu.SMEM),
                  pl.BlockSpec((tt, SWA_W), lambda i: (i, 0)),
                  pl.BlockSpec((SWA_WINDOW, SWA_W), lambda i: (jnp.maximum(i * per - 1, 0), 0))],
        out_specs=pl.BlockSpec((tt, 256), lambda i: (i, 0)),
        out_shape=jax.ShapeDtypeStruct((T, 256), BF16),
        scratch_shapes=[pltpu.VMEM((per, SWA_HEADS, 2 * SWA_WINDOW, SWA_WINDOW), F32)],
        compiler_params=_params("parallel"),
        name="swa_mixer",
    )(sinks, z, z)


_KMEAN_BLOCKS = 8


def _kmean_kernel(k_ref, o_ref):
    k = k_ref[...].astype(F32).reshape(_KMEAN_BLOCKS, MOBA_BLOCK, 256)
    o_ref[...] = jnp.mean(k, axis=1)


def _moba_kmean(z):
    T = z.shape[0]
    rows = _KMEAN_BLOCKS * MOBA_BLOCK
    return pl.pallas_call(
        _kmean_kernel,
        grid=(T // rows,),
        in_specs=[pl.BlockSpec((rows, 256), lambda i: (i, 1))],
        out_specs=pl.BlockSpec((_KMEAN_BLOCKS, 256), lambda i: (i, 0)),
        out_shape=jax.ShapeDtypeStruct((T // MOBA_BLOCK, 256), F32),
        compiler_params=_params("parallel"),
        name="moba_kmean",
    )(z)


MOBA_VROWS = 80
MOBA_PREP_BLOCKS = 2


def _moba_prep_kernel(z_ref, km_ref, qt_out, k_out, vt_out):
    HD = MOBA_HD
    tq = MOBA_BLOCK
    nb = km_ref.shape[0]
    blk_i = _iota((nb, tq), 0)
    blk = blk_i.astype(F32)
    pad = LANES - HD - nb
    ones_row = (_iota((MOBA_VROWS - HD, tq), 0) == 0).astype(BF16)
    for sb in range(MOBA_PREP_BLOCKS):
        rows = slice(sb * tq, (sb + 1) * tq)
        qb = (pl.program_id(0) * MOBA_PREP_BLOCKS + sb) % nb
        this_block_lanes = (_iota((tq, nb), 1) == qb).astype(BF16)
        for hh in range(MOBA_HEADS):
            cols = slice(hh * HD, (hh + 1) * HD)
            q_t = z_ref[rows, cols].astype(F32).T
            gate = _dot(km_ref[:, cols], q_t, precision=HIGHEST)
            gate = jnp.where(blk_i < qb, gate, NEG_INF)
            keep = jnp.where(blk_i == qb, 1.0, 0.0)
            for _ in range(MOBA_TOPK):
                best = jnp.max(gate, axis=0, keepdims=True)
                first = jnp.min(jnp.where(gate == best, blk, float(nb)), axis=0, keepdims=True)
                hit = blk == first
                keep = keep + jnp.where(hit, jnp.where(best > 0.5 * NEG_INF, 1.0, 0.0), 0.0)
                gate = jnp.where(hit, BELOW_NEG_INF, gate)
            bias = ((keep - 1.0) * (-NEG_INF)).astype(BF16)
            qt_out[hh, sb] = jnp.concatenate(
                [(q_t * (HD ** -0.5)).astype(BF16), bias, jnp.zeros((pad, tq), BF16)], axis=0)
            k_out[hh, rows, :] = jnp.concatenate(
                [z_ref[rows, 256 + hh * HD:256 + (hh + 1) * HD], this_block_lanes, jnp.zeros((tq, pad), BF16)],
                axis=1)
            v_t = z_ref[rows, 512 + hh * HD:512 + (hh + 1) * HD].astype(F32).T.astype(BF16)
            vt_out[hh, sb] = jnp.concatenate([v_t, ones_row], axis=0)


def _moba_prep(z, kmean, seq):
    T = z.shape[0]
    nb = seq // MOBA_BLOCK
    assert MOBA_HD + nb <= LANES and nb % MOBA_PREP_BLOCKS == 0
    tq = MOBA_BLOCK
    tile = MOBA_PREP_BLOCKS * tq
    H = MOBA_HEADS

    def per_block_t(rows):
        return pl.BlockSpec((H, MOBA_PREP_BLOCKS, rows, tq), lambda i: (0, i, 0, 0))

    return pl.pallas_call(
        _moba_prep_kernel,
        grid=(T // tile,),
        in_specs=[pl.BlockSpec((tile, MOBA_W), lambda i: (i, 0)),
                  pl.BlockSpec((nb, 256), lambda i: (i * MOBA_PREP_BLOCKS // nb, 0))],
        out_specs=[per_block_t(LANES), pl.BlockSpec((H, tile, LANES), lambda i: (0, i, 0)),
                   per_block_t(MOBA_VROWS)],
        out_shape=[jax.ShapeDtypeStruct((H, T // tq, LANES, tq), BF16), jax.ShapeDtypeStruct((H, T, LANES), BF16),
                   jax.ShapeDtypeStruct((H, T // tq, MOBA_VROWS, tq), BF16)],
        compiler_params=_params("parallel"),
        name="moba_prep",
    )(z, kmean)


MOBA_QBLOCKS = 4
MOBA_KBLOCKS = 2
MOBA_DIAGONALS = MOBA_QBLOCKS // MOBA_KBLOCKS


def _moba_kernel(qidx_ref, kidx_ref, kind_ref, qt_ref, k_ref, vt_ref, o_ref, m_ref, acc_ref, s_ref):
    HD = MOBA_HD
    B = MOBA_BLOCK
    step = pl.program_id(1)
    kind = kind_ref[step]
    is_last = (step + 1 == pl.num_programs(1)) | (qidx_ref[jnp.minimum(step + 1, pl.num_programs(1) - 1)]
                                                  != qidx_ref[step])

    def attend(diagonal):
        causal = _iota((B, B), 0) <= _iota((B, B), 1)

        def key_blocks_of(qi):
            if diagonal is None:
                return list(range(MOBA_KBLOCKS))
            return [c for c in range(MOBA_KBLOCKS) if MOBA_KBLOCKS * diagonal + c <= qi]

        for hh in range(MOBA_HEADS):
            for qi in range(MOBA_QBLOCKS):
                for c in key_blocks_of(qi):
                    s = _dot(k_ref[hh, c * B:(c + 1) * B, :], qt_ref[hh, qi])
                    if diagonal is not None and MOBA_KBLOCKS * diagonal + c == qi:
                        s = jnp.where(causal, s, NEG_INF)
                    s_ref[hh, qi, c] = s
        for hh in range(MOBA_HEADS):
            for qi in range(MOBA_QBLOCKS):
                key_blocks = key_blocks_of(qi)
                if not key_blocks:
                    continue
                scores = [s_ref[hh, qi, c] for c in key_blocks]
                m_new = functools.reduce(jnp.maximum, [jnp.max(s, axis=0, keepdims=True) for s in scores])
                first_visit = diagonal == 0
                if not first_visit:
                    m_old = m_ref[hh, qi]
                    m_new = jnp.maximum(m_old, m_new)
                pv = None
                for c, s in zip(key_blocks, scores):
                    term = _dot(vt_ref[hh, c], jnp.exp(s - m_new).astype(BF16))
                    pv = term if pv is None else pv + term
                acc_ref[hh, qi] = pv if first_visit else jnp.exp(m_old - m_new) * acc_ref[hh, qi] + pv
                m_ref[hh, qi] = m_new

    @pl.when(kind == 0)
    def _():
        attend(None)

    for d in range(MOBA_DIAGONALS):
        @pl.when(kind == 1 + d)
        def _(d=d):
            attend(d)

    @pl.when(is_last)
    def _():
        pad = jnp.zeros((LANES - MOBA_VROWS, B), F32)
        for qi in range(MOBA_QBLOCKS):
            outs = []
            for hh in range(MOBA_HEADS):
                a = jnp.concatenate([acc_ref[hh, qi], pad], axis=0).T
                outs.append(a[:, 0:HD] / a[:, HD:HD + 1])
            o_ref[qi * B:(qi + 1) * B, :] = jnp.concatenate(outs, axis=1).astype(BF16)


def _moba(z, seq):
    T = z.shape[0]
    nb = seq // MOBA_BLOCK
    assert nb % MOBA_QBLOCKS == 0
    q_tiles = nb // MOBA_QBLOCKS
    k_steps = nb // MOBA_KBLOCKS
    D = MOBA_DIAGONALS
    batch = T // seq
    qt_aug, k_aug, vt_aug = _moba_prep(z, _moba_kmean(z), seq)
    qidx = np.concatenate([np.full(D * t + D, t) for t in range(q_tiles)]).astype(np.int32)
    kidx = np.concatenate([np.concatenate([D * t + np.arange(D), np.arange(D * t)])
                           for t in range(q_tiles)]).astype(np.int32)
    kind = np.concatenate([np.concatenate([1 + np.arange(D), np.zeros(D * t, np.int64)])
                           for t in range(q_tiles)]).astype(np.int32)
    tq = MOBA_QBLOCKS * MOBA_BLOCK
    tk = MOBA_KBLOCKS * MOBA_BLOCK
    H = MOBA_HEADS
    grid_spec = pltpu.PrefetchScalarGridSpec(
        num_scalar_prefetch=3,
        grid=(batch, len(qidx)),
        in_specs=[pl.BlockSpec((H, MOBA_QBLOCKS, LANES, MOBA_BLOCK),
                               lambda b, s, qi, ki, kd: (0, b * q_tiles + qi[s], 0, 0)),
                  pl.BlockSpec((H, tk, LANES), lambda b, s, qi, ki, kd: (0, b * k_steps + ki[s], 0)),
                  pl.BlockSpec((H, MOBA_KBLOCKS, MOBA_VROWS, MOBA_BLOCK),
                               lambda b, s, qi, ki, kd: (0, b * k_steps + ki[s], 0, 0))],
        out_specs=pl.BlockSpec((tq, 256), lambda b, s, qi, ki, kd: (b * q_tiles + qi[s], 0)),
        scratch_shapes=[pltpu.VMEM((H, MOBA_QBLOCKS, 1, MOBA_BLOCK), F32),
                        pltpu.VMEM((H, MOBA_QBLOCKS, MOBA_VROWS, MOBA_BLOCK), F32),
                        pltpu.VMEM((H, MOBA_QBLOCKS, MOBA_KBLOCKS, MOBA_BLOCK, MOBA_BLOCK), F32)],
    )
    return pl.pallas_call(
        _moba_kernel,
        grid_spec=grid_spec,
        out_shape=jax.ShapeDtypeStruct((T, 256), BF16),
        compiler_params=_params("parallel", "arbitrary"),
        name="moba_mixer",
    )(jnp.asarray(qidx), jnp.asarray(kidx), jnp.asarray(kind), qt_aug, k_aug, vt_aug)


def _merge_kernel(h_ref, o0, o1, o2, o3, gates_ref, wb_ref, wout_ref, out_ref):
    merged = None
    for i, o in enumerate((o0, o1, o2, o3)):
        gate = _sigmoid(gates_ref[:, i * D_MODEL:(i + 1) * D_MODEL].astype(F32))
        term = gate * _dot(o[...], wb_ref[i])
        merged = term if merged is None else merged + term
    out_ref[...] = h_ref[...] + _dot(merged.astype(BF16), wout_ref[...])


def _merge(h, branches, gates, wb, wout):
    T = h.shape[0]
    tm = TOK_TILE
    return pl.pallas_call(
        _merge_kernel,
        grid=(T // tm,),
        in_specs=[pl.BlockSpec((tm, D_MODEL), lambda i: (i, 0))]
        + [pl.BlockSpec((tm, 256), lambda i: (i, 0))] * 4
        + [pl.BlockSpec((tm, GATE_W), lambda i: (i, 0)),
           pl.BlockSpec((N_BRANCH, 256, D_MODEL), lambda i: (0, 0, 0), pipeline_mode=pl.Buffered(1)),
           pl.BlockSpec((D_MODEL, D_MODEL), lambda i: (0, 0), pipeline_mode=pl.Buffered(1))],
        out_specs=pl.BlockSpec((tm, D_MODEL), lambda i: (i, 0)),
        out_shape=jax.ShapeDtypeStruct((T, D_MODEL), F32),
        compiler_params=_params("parallel"),
        name="merge_out_proj",
    )(h, *branches, gates, wb, wout)


ROW_HALF = D_MODEL // 2
ROW_TILES = ROW_HALF // LANES
HIGH_HALF_MASK = -65536


def _row_tile_spec(rows, index_map):
    return pl.BlockSpec((rows * ROW_TILES, LANES), index_map)


def _bf16_bits(x):
    return lax.bitcast_convert_type(x.astype(BF16).astype(F32), I32)


def _pack_rows(ref, x):
    for j in range(ROW_TILES):
        low = _bf16_bits(x[:, j * LANES:(j + 1) * LANES])
        high = _bf16_bits(x[:, ROW_HALF + j * LANES:ROW_HALF + (j + 1) * LANES])
        ref[pl.ds(j, x.shape[0], stride=ROW_TILES), :] = high | lax.shift_right_logical(low, 16)


def _unpack_slab(ref, j):
    words = ref[pl.ds(j, ref.shape[0] // ROW_TILES, stride=ROW_TILES), :]
    low = lax.bitcast_convert_type(lax.shift_left(words, 16), F32)
    high = lax.bitcast_convert_type(words & HIGH_HALF_MASK, F32)
    return low, high


def _unpack_rows(ref):
    halves = [_unpack_slab(ref, j) for j in range(ROW_TILES)]
    return jnp.concatenate([lo for lo, _ in halves] + [hi for _, hi in halves], axis=1).astype(BF16)


def _router_kernel(h_ref, g_ref, wr_ref, br_ref, u_ref, route_ref, route_t_ref, count_ref, carry_ref):
    tm = h_ref.shape[0]

    @pl.when(pl.program_id(0) == 0)
    def _():
        carry_ref[...] = jnp.zeros_like(carry_ref)

    u = _rms(h_ref[...], g_ref[...])
    _pack_rows(u_ref, u)
    u_hi, u_lo = _split_bf16(u, 2)
    w_hi, w_lo = _split_bf16(wr_ref[...], 2)
    logits = _dot(u_hi, w_hi) + (_dot(u_lo, w_hi) + _dot(u_hi, w_lo)) + br_ref[...]
    lt = logits.T
    row_i = _iota((LANES, tm), 0)
    row = row_i.astype(F32)
    row_group = ((row_i - EXPERT_LANE0) >> 3).astype(F32)

    def first_row(mask):
        return jnp.min(jnp.where(mask, row, float(LANES)), axis=0, keepdims=True)

    grp = jnp.where(row_i < N_GROUPS, lt, BELOW_NEG_INF)
    grp_e = jnp.exp(grp - jnp.max(grp, axis=0, keepdims=True))
    p_group = grp_e / jnp.sum(grp_e, axis=0, keepdims=True)
    g_w = jnp.max(p_group, axis=0, keepdims=True)
    g_sel = first_row(p_group == g_w)

    in_group = (row_i >= EXPERT_LANE0) & (row_group == g_sel)
    el = jnp.where(in_group, lt, BELOW_NEG_INF)
    top1 = jnp.max(el, axis=0, keepdims=True)
    row1 = first_row(el == top1)
    el = jnp.where(row == row1, BELOW_NEG_INF, el)
    top2 = jnp.max(el, axis=0, keepdims=True)
    row2 = first_row(el == top2)
    e2 = jnp.exp(top2 - top1)
    w1 = g_w / (1.0 + e2)
    w2 = g_w * e2 / (1.0 + e2)

    onehot = ((row == row1) | (row == row2)).astype(BF16)
    earlier = (_iota((tm, tm), 0) < _iota((tm, tm), 1)).astype(BF16)
    seen = _dot(onehot, earlier) + carry_ref[:, 0:1]
    rank1 = jnp.sum(jnp.where(row == row1, seen, 0.0), axis=0, keepdims=True)
    rank2 = jnp.sum(jnp.where(row == row2, seen, 0.0), axis=0, keepdims=True)
    carry_ref[...] += jnp.sum(onehot.astype(F32), axis=1, keepdims=True)
    count_ref[...] = carry_ref[...]

    out_t = jnp.zeros((LANES, tm), F32)
    for idx, val in enumerate((row1 - EXPERT_LANE0, row2 - EXPERT_LANE0, w1, w2, rank1, rank2)):
        out_t = jnp.where(row_i == idx, val, out_t)
    route_t_ref[...] = out_t[0:8, :]
    route_ref[...] = out_t.T


def _router(h, g, wr, br):
    T = h.shape[0]
    tm = TOK_TILE
    return pl.pallas_call(
        _router_kernel,
        grid=(T // tm,),
        in_specs=[pl.BlockSpec((tm, D_MODEL), lambda i: (i, 0)), _full((1, D_MODEL)),
                  _full((D_MODEL, LANES)), _full((1, LANES))],
        out_specs=[_row_tile_spec(tm, lambda i: (i, 0)), pl.BlockSpec((tm, LANES), lambda i: (i, 0)),
                   pl.BlockSpec((8, tm), lambda i: (0, i)), _full((LANES, LANES))],
        out_shape=[jax.ShapeDtypeStruct((T * ROW_TILES, LANES), I32), jax.ShapeDtypeStruct((T, LANES), F32),
                   jax.ShapeDtypeStruct((8, T), F32), jax.ShapeDtypeStruct((LANES, LANES), F32)],
        scratch_shapes=[pltpu.VMEM((LANES, LANES), F32)],
        compiler_params=_params("arbitrary"),
        name="moe_router",
    )(h, g, wr, br)


ROW_UNROLL = 8


def _row_copy(src_ref, src_row, dst_ref, dst_row, sem):
    src = src_ref.at[pl.ds(pl.multiple_of(src_row * ROW_TILES, ROW_TILES), ROW_TILES)]
    dst = dst_ref.at[pl.ds(pl.multiple_of(dst_row * ROW_TILES, ROW_TILES), ROW_TILES)]
    return pltpu.make_async_copy(src, dst, sem)


def _for_each_row_pair(rows, body):
    def group(g, carry):
        for i in range(ROW_UNROLL):
            for k in range(EXPERT_TOPK):
                body(g * ROW_UNROLL + i, k)
        return carry

    lax.fori_loop(0, rows // ROW_UNROLL, group, 0)


def _dispatch_kernel(dest0_ref, dest1_ref, pstart_ref, pend_ref, u_ref, xs_ref, zero_ref, sem):
    tm = u_ref.shape[0] // ROW_TILES
    tile = pl.program_id(0)
    dests = (dest0_ref, dest1_ref)

    @pl.when(tile == 0)
    def _():
        zero_ref[...] = jnp.zeros_like(zero_ref)

        def zero_block(first_row):
            start = pl.multiple_of(first_row * ROW_TILES, ROW_TILES)
            fill = pltpu.make_async_copy(zero_ref, xs_ref.at[pl.ds(start, MOE_BLOCK * ROW_TILES)], sem)
            fill.start()
            fill.wait()

        def zero_tail(e, carry):
            @pl.when(pend_ref[e] > pstart_ref[e])
            def _():
                zero_block(pend_ref[e] - MOE_BLOCK)
            return carry

        def zero_unused(b, carry):
            zero_block(b * MOE_BLOCK)
            return carry

        lax.fori_loop(0, N_EXPERTS, zero_tail, 0)
        lax.fori_loop(pend_ref[N_EXPERTS - 1] // MOE_BLOCK, xs_ref.shape[0] // (MOE_BLOCK * ROW_TILES),
                      zero_unused, 0)

    def issue(r, k):
        _row_copy(u_ref, r, xs_ref, dests[k][tile * tm + r], sem).start(priority=k)

    _for_each_row_pair(tm, issue)
    for _ in range(EXPERT_TOPK):
        pltpu.make_async_copy(u_ref, xs_ref.at[pl.ds(0, tm * ROW_TILES)], sem).wait()


def _dispatch(dest0, dest1, pstart, pend, u, n_slots):
    T = u.shape[0] // ROW_TILES
    tm = DISPATCH_TILE
    grid_spec = pltpu.PrefetchScalarGridSpec(
        num_scalar_prefetch=4,
        grid=(T // tm,),
        in_specs=[_row_tile_spec(tm, lambda i, *_: (i, 0))],
        out_specs=pl.BlockSpec(memory_space=pl.ANY),
        scratch_shapes=[pltpu.VMEM((MOE_BLOCK * ROW_TILES, LANES), I32), pltpu.SemaphoreType.DMA(())],
    )
    return pl.pallas_call(
        _dispatch_kernel,
        grid_spec=grid_spec,
        out_shape=jax.ShapeDtypeStruct((n_slots * ROW_TILES, LANES), I32),
        compiler_params=_params("arbitrary"),
        name="moe_dispatch",
    )(dest0, dest1, pstart, pend, u)


def _expert_kernel(blk_e_ref, n_used_ref, x_ref, w1_ref, w3_ref, w2_ref, y_ref, w1b_ref, w3b_ref, w2b_ref):
    i = pl.program_id(0)
    used = i < n_used_ref[0]

    @pl.when((i == 0) | (blk_e_ref[i] != blk_e_ref[jnp.maximum(i - 1, 0)]))
    def _():
        w1b_ref[...] = w1_ref[0, 0].astype(BF16)
        w3b_ref[...] = w3_ref[0, 0].astype(BF16)
        w2b_ref[...] = w2_ref[0, 0].astype(BF16)

    @pl.when(used)
    def _():
        x = _unpack_rows(x_ref)
        a = _dot(x, w1b_ref[...])
        hidden = (a * _sigmoid(a)) * _dot(x, w3b_ref[...])
        _pack_rows(y_ref, _dot(hidden.astype(BF16), w2b_ref[...]))

    @pl.when(jnp.logical_not(used))
    def _():
        y_ref[...] = jnp.zeros_like(y_ref)


def _experts(blk_e, n_used, xs, w1, w3, w2, layer):
    n_slots = xs.shape[0] // ROW_TILES
    bm = MOE_BLOCK
    grid_spec = pltpu.PrefetchScalarGridSpec(
        num_scalar_prefetch=2,
        grid=(n_slots // bm,),
        in_specs=[_row_tile_spec(bm, lambda i, be, nu: (jnp.minimum(i, nu[0] - 1), 0)),
                  pl.BlockSpec((1, 1, D_MODEL, EXPERT_HIDDEN), lambda i, be, nu: (layer, be[i], 0, 0)),
                  pl.BlockSpec((1, 1, D_MODEL, EXPERT_HIDDEN), lambda i, be, nu: (layer, be[i], 0, 0)),
                  pl.BlockSpec((1, 1, EXPERT_HIDDEN, D_MODEL), lambda i, be, nu: (layer, be[i], 0, 0))],
        out_specs=_row_tile_spec(bm, lambda i, be, nu: (i, 0)),
        scratch_shapes=[pltpu.VMEM((D_MODEL, EXPERT_HIDDEN), BF16), pltpu.VMEM((D_MODEL, EXPERT_HIDDEN), BF16),
                        pltpu.VMEM((EXPERT_HIDDEN, D_MODEL), BF16)],
    )
    return pl.pallas_call(
        _expert_kernel,
        grid_spec=grid_spec,
        out_shape=jax.ShapeDtypeStruct((n_slots * ROW_TILES, LANES), I32),
        compiler_params=_params("arbitrary"),
        name="moe_experts",
    )(blk_e, n_used, xs, w1, w3, w2)


def _combine_kernel(dest0_ref, dest1_ref, h_ref, route_ref, g_ref, ys_ref, out_ref, y_ref, sem, *, normalize):
    tm = h_ref.shape[0]
    step = pl.program_id(0)
    slot = step % 2

    dests = (dest0_ref, dest1_ref)

    def gather_tile(tile, buf):
        def issue(r, k):
            _row_copy(ys_ref, dests[k][tile * tm + r], y_ref.at[buf, k], r, sem.at[buf]).start(priority=k)

        _for_each_row_pair(tm, issue)

    @pl.when(step == 0)
    def _():
        gather_tile(0, 0)

    @pl.when(step + 1 < pl.num_programs(0))
    def _():
        gather_tile(step + 1, 1 - slot)

    for k in range(EXPERT_TOPK):
        pltpu.make_async_copy(ys_ref.at[pl.ds(0, tm * ROW_TILES)], y_ref.at[slot, k], sem.at[slot]).wait()
    w0 = route_ref[:, 2:3]
    w1 = route_ref[:, 3:4]
    for j in range(ROW_TILES):
        low0, high0 = _unpack_slab(y_ref.at[slot, 0], j)
        low1, high1 = _unpack_slab(y_ref.at[slot, 1], j)
        cols = slice(j * LANES, (j + 1) * LANES)
        out_ref[:, cols] = h_ref[:, cols] + w0 * low0 + w1 * low1
        cols = slice(ROW_HALF + j * LANES, ROW_HALF + (j + 1) * LANES)
        out_ref[:, cols] = h_ref[:, cols] + w0 * high0 + w1 * high1
    if normalize:
        out_ref[...] = _rms(out_ref[...], g_ref[...])


def _combine(dest0, dest1, h, route, ys, final_g, normalize):
    T = h.shape[0]
    tm = ROW_TILE
    grid_spec = pltpu.PrefetchScalarGridSpec(
        num_scalar_prefetch=2,
        grid=(T // tm,),
        in_specs=[pl.BlockSpec((tm, D_MODEL), lambda i, *_: (i, 0)),
                  pl.BlockSpec((tm, LANES), lambda i, *_: (i, 0)),
                  _full((1, D_MODEL)),
                  pl.BlockSpec(memory_space=pl.ANY)],
        out_specs=pl.BlockSpec((tm, D_MODEL), lambda i, *_: (i, 0)),
        scratch_shapes=[pltpu.VMEM((2, EXPERT_TOPK, tm * ROW_TILES, LANES), I32),
                        pltpu.SemaphoreType.DMA((2,))],
    )
    return pl.pallas_call(
        functools.partial(_combine_kernel, normalize=normalize),
        grid_spec=grid_spec,
        out_shape=jax.ShapeDtypeStruct((T, D_MODEL), F32),
        compiler_params=_params("arbitrary"),
        name="moe_combine",
    )(dest0, dest1, h, route, final_g, ys)


def _moe(h, g, w_group, b_group, w_expert, b_expert, w1, w3, w2, layer, final_g, normalize):
    T = h.shape[0]
    wr = jnp.zeros((D_MODEL, LANES), F32)
    wr = wr.at[:, :N_GROUPS].set(w_group).at[:, EXPERT_LANE0:EXPERT_LANE0 + N_EXPERTS].set(w_expert)
    br = jnp.zeros((1, LANES), F32)
    br = br.at[0, :N_GROUPS].set(b_group).at[0, EXPERT_LANE0:EXPERT_LANE0 + N_EXPERTS].set(b_expert)
    u, route, route_t, count = _router(h, g, wr, br)

    n_blocks = (T * EXPERT_TOPK) // MOE_BLOCK + N_EXPERTS
    counts = count[EXPERT_LANE0:EXPERT_LANE0 + N_EXPERTS, 0].astype(I32)
    padded = ((counts + MOE_BLOCK - 1) // MOE_BLOCK) * MOE_BLOCK
    pend = jnp.cumsum(padded)
    pstart = pend - padded
    expert_ids = jnp.arange(N_EXPERTS, dtype=F32)[:, None]

    def slot_of(expert_row, rank_row):
        segment_start = jnp.sum(jnp.where(expert_row[None, :] == expert_ids, pstart[:, None], 0), axis=0)
        return segment_start + rank_row.astype(I32)

    dest0 = slot_of(route_t[0], route_t[4])
    dest1 = slot_of(route_t[1], route_t[5])
    block_row0 = jnp.arange(n_blocks, dtype=I32) * MOE_BLOCK
    blk_e = jnp.minimum(jnp.sum((pend[None, :] <= block_row0[:, None]).astype(I32), axis=1), N_EXPERTS - 1)
    n_used = (pend[-1:] // MOE_BLOCK).astype(I32)

    xs = _dispatch(dest0, dest1, pstart, pend, u, n_blocks * MOE_BLOCK)
    ys = _experts(blk_e, n_used, xs, w1, w3, w2, layer)
    return _combine(dest0, dest1, h, route, ys, final_g, normalize)


def _split_w_in(w_in):
    gla_n = GLA_HEADS * (2 * GLA_DK + 2 * GLA_DV) + GLA_LOWRANK
    swa_n = (SWA_HEADS + 2 * SWA_KV_HEADS) * SWA_HD
    moba_n = 3 * MOBA_HEADS * MOBA_HD
    ret_n = RET_HEADS * (2 * RET_DK + 2 * RET_DV)
    offs = np.cumsum([0, gla_n, swa_n, moba_n, ret_n, GATE_W])
    parts = [w_in[:, offs[i]:offs[i + 1]].astype(BF16) for i in range(5)]
    parts[0] = jnp.pad(parts[0], ((0, 0), (0, GLA_W - gla_n)))
    return parts


def _token_mixer(h, seq, ln1_g, w_in, gla_w_a2, gla_b_a, gla_norm_g, swa_sinks, ret_norm_g, w_branch, w_out,
                 rope):
    z_gla, z_swa, z_moba, z_ret, z_gates = _in_proj(h, ln1_g.reshape(1, D_MODEL), _split_w_in(w_in))
    wa = jnp.zeros((128, 128), F32).at[:GLA_LOWRANK].set(gla_w_a2)
    o_gla = _gla(z_gla, wa, gla_b_a.reshape(1, 128), jnp.tile(gla_norm_g, GLA_HEADS).reshape(1, 256), seq)
    o_swa = _swa(z_swa, swa_sinks, seq)
    o_moba = _moba(z_moba, seq)
    o_ret = _retention(z_ret, *rope, jnp.tile(ret_norm_g, RET_HEADS).reshape(1, 256), seq)
    return _merge(h, (o_gla, o_swa, o_moba, o_ret), z_gates, w_branch.astype(BF16), w_out.astype(BF16))


def kernel(x, ln1_g, w_in, gla_w_a2, gla_b_a, gla_norm_g, swa_sinks, ret_norm_g, w_branch, w_out, ln2_g,
           w_group, b_group, w_expert, b_expert, w1, w3, w2, final_g):
    batch, seq, _ = x.shape
    depth = w_in.shape[0]
    assert depth >= 1
    rope = _rope_tables(seq)
    h = x.reshape(batch * seq, D_MODEL)
    for l in range(depth):
        h = _token_mixer(h, seq, ln1_g[l], w_in[l], gla_w_a2[l], gla_b_a[l], gla_norm_g[l], swa_sinks[l],
                         ret_norm_g[l], w_branch[l], w_out[l], rope)
        h = _moe(h, ln2_g[l].reshape(1, D_MODEL), w_group[l], b_group[l], w_expert[l], b_expert[l],
                 w1, w3, w2, l, final_g.reshape(1, D_MODEL), normalize=(l == depth - 1))
    return h.reshape(batch, seq, D_MODEL)
```

```python
import functools
import math

import numpy as np
import jax
import jax.numpy as jnp
from jax import lax
from jax.experimental import pallas as pl
from jax.experimental.pallas import tpu as pltpu

F32 = jnp.float32
BF16 = jnp.bfloat16
I32 = jnp.int32
HIGHEST = lax.Precision.HIGHEST

D_MODEL = 1024
N_BRANCH = 4
NORM_EPS = 1e-6
NEG_INF = -1e30
BELOW_NEG_INF = -3e38
LOG2_E = math.log2(math.e)

GLA_HEADS, GLA_DK, GLA_DV, GLA_LOWRANK, GLA_TAU, GLA_CHUNK = 4, 32, 64, 16, 16.0, 64
SWA_HEADS, SWA_KV_HEADS, SWA_HD, SWA_WINDOW = 4, 2, 64, 128
MOBA_HEADS, MOBA_HD, MOBA_BLOCK, MOBA_TOPK = 4, 64, 256, 3
RET_HEADS, RET_DK, RET_DV, RET_CHUNK, RET_ROPE_BASE = 4, 32, 64, 128, 10000.0
N_GROUPS, EXPERTS_PER_GROUP, EXPERT_TOPK, EXPERT_HIDDEN = 4, 8, 2, 256
MOE_BLOCK = 512
N_EXPERTS = N_GROUPS * EXPERTS_PER_GROUP

GLA_W = 896
SWA_W = 512
MOBA_W = 768
RET_W = 768
GATE_W = N_BRANCH * D_MODEL

LANES = 128
EXPERT_LANE0 = 32

VMEM_LIMIT = 56 * 1024 * 1024

TOK_TILE = 512
ROW_TILE = 512
DISPATCH_TILE = 1024


def _params(*sem):
    return pltpu.CompilerParams(dimension_semantics=sem, vmem_limit_bytes=VMEM_LIMIT)


def _dot(a, b, precision=None):
    return jnp.dot(a, b, preferred_element_type=F32, precision=precision)


def _dot_nt(a, b, precision=None):
    return lax.dot_general(a, b, (((1,), (1,)), ((), ())), preferred_element_type=F32, precision=precision)


def _dot_tn(a, b, precision=None):
    return lax.dot_general(a, b, (((0,), (0,)), ((), ())), preferred_element_type=F32, precision=precision)


def _split_bf16(x, parts):
    out = []
    for _ in range(parts - 1):
        piece = x.astype(BF16)
        out.append(piece)
        x = x - piece.astype(F32)
    out.append(x.astype(BF16))
    return out


def _dot_f32_by_bf16(x, m):
    return sum(_dot(piece, m) for piece in _split_bf16(x, 3))


def _dot_bf16_by_f32(m, x):
    return sum(_dot(m, piece) for piece in _split_bf16(x, 3))


def _iota(shape, dim):
    return lax.broadcasted_iota(I32, shape, dim)


def _sigmoid(x):
    return 0.5 * jnp.tanh(0.5 * x) + 0.5


def _rms(x, g):
    ms = jnp.mean(x * x, axis=-1, keepdims=True)
    return x * lax.rsqrt(ms + NORM_EPS) * g


def _full(shape):
    return pl.BlockSpec(shape, lambda *_: (0,) * len(shape))


def _in_proj_kernel(h_ref, g_ref, w0, w1, w2, w3, w4, o0, o1, o2, o3, o4):
    u = _rms(h_ref[...], g_ref[...]).astype(BF16)
    for w, o in ((w0, o0), (w1, o1), (w2, o2), (w3, o3), (w4, o4)):
        o[...] = _dot(u, w[...]).astype(BF16)


def _in_proj(h, g, ws):
    T = h.shape[0]
    tm = TOK_TILE
    widths = [w.shape[1] for w in ws]
    return pl.pallas_call(
        _in_proj_kernel,
        grid=(T // tm,),
        in_specs=[pl.BlockSpec((tm, D_MODEL), lambda i: (i, 0)), _full((1, D_MODEL))]
        + [pl.BlockSpec((D_MODEL, n), lambda i: (0, 0), pipeline_mode=pl.Buffered(1)) for n in widths],
        out_specs=[pl.BlockSpec((tm, n), lambda i: (i, 0)) for n in widths],
        out_shape=[jax.ShapeDtypeStruct((T, n), BF16) for n in widths],
        compiler_params=_params("parallel"),
        name="in_proj",
    )(h, g, *ws)


def _head_select(x, lane_head, n_heads, rows):
    out = None
    for hh in range(n_heads):
        term = jnp.where(lane_head == hh, x[hh * rows:(hh + 1) * rows], 0.0)
        out = term if out is None else out + term
    return out


def _head_norm_gate(o, gate_in, g):
    n = o.shape[1]
    same_head = (_iota((n, n), 0) >> 6) == (_iota((n, n), 1) >> 6)
    ms = _dot_f32_by_bf16(o * o, same_head.astype(BF16)) * (1.0 / 64.0)
    return o * lax.rsqrt(ms + NORM_EPS) * g * (gate_in * _sigmoid(gate_in))


def _gla_kernel(z_ref, wa_ref, ba_ref, g_ref, o_ref, st_ref, oacc_ref, att_ref, kv_ref, *, tiles_per_seq):
    C = GLA_CHUNK
    tt = z_ref.shape[0]

    @pl.when(pl.program_id(0) % tiles_per_seq == 0)
    def _():
        st_ref[...] = jnp.zeros_like(st_ref)

    q = z_ref[:, 0:128].astype(F32)
    k = z_ref[:, 128:256].astype(F32)
    r = z_ref[:, 512:768].astype(F32)
    pre = _dot_bf16_by_f32(z_ref[:, 768:896], wa_ref[...]) + ba_ref[...]
    log_a = (jnp.minimum(pre, 0.0) - jnp.log(1.0 + jnp.exp(-jnp.abs(pre)))) * (1.0 / GLA_TAU)

    lower = (_iota((C, C), 1) <= _iota((C, C), 0)).astype(BF16)
    b = jnp.concatenate([_dot_bf16_by_f32(lower, log_a[c * C:(c + 1) * C]) for c in range(tt // C)], axis=0)
    b_last = jnp.broadcast_to(b.reshape(tt // C, C, 128)[:, C - 1:C, :], (tt // C, C, 128)).reshape(tt, 128)
    q_dec = (q * (GLA_DK ** -0.5) * jnp.exp(b)).astype(BF16)
    k_dec = (k * jnp.exp(-b)).astype(BF16)
    k_end = (k * jnp.exp(b_last - b)).astype(BF16)
    decay = jnp.exp(b_last)

    qk_head = _iota((C, 128), 1) >> 5
    v_head = _iota((C, 256), 1) >> 6
    causal = _iota((4 * C, C), 1) <= (_iota((4 * C, C), 0) & (C - 1))
    state_mask = (_iota((256, 128), 0) >> 6) == (_iota((256, 128), 1) >> 5)

    for c in range(tt // C):
        sl = slice(c * C, (c + 1) * C)
        q_stack = jnp.concatenate([jnp.where(qk_head == hh, q_dec[sl], 0) for hh in range(GLA_HEADS)], axis=0)
        att_ref[c] = jnp.where(causal, _dot_nt(q_stack, k_dec[sl]), 0.0).astype(BF16)
        kv_ref[c] = jnp.where(state_mask, _dot_tn(z_ref[sl, 256:512], k_end[sl]), 0.0)
    state = st_ref[...]
    for c in range(tt // C):
        sl = slice(c * C, (c + 1) * C)
        intra = _head_select(_dot(att_ref[c], z_ref[sl, 256:512]), v_head, GLA_HEADS, C)
        inter = _dot_nt(q_dec[sl], state.astype(BF16))
        oacc_ref[sl, :] = intra + inter
        state = state * decay[c * C:c * C + 1] + kv_ref[c]
    st_ref[...] = state

    o_ref[...] = _head_norm_gate(oacc_ref[...], r, g_ref[...]).astype(BF16)


def _gla(z, wa, ba, g, seq):
    T = z.shape[0]
    tt = TOK_TILE
    return pl.pallas_call(
        functools.partial(_gla_kernel, tiles_per_seq=seq // tt),
        grid=(T // tt,),
        in_specs=[pl.BlockSpec((tt, GLA_W), lambda i: (i, 0)), _full((128, 128)), _full((1, 128)),
                  _full((1, 256))],
        out_specs=pl.BlockSpec((tt, 256), lambda i: (i, 0)),
        out_shape=jax.ShapeDtypeStruct((T, 256), BF16),
        scratch_shapes=[pltpu.VMEM((256, 128), F32), pltpu.VMEM((tt, 256), F32),
                        pltpu.VMEM((tt // GLA_CHUNK, GLA_HEADS * GLA_CHUNK, GLA_CHUNK), BF16),
                        pltpu.VMEM((tt // GLA_CHUNK, 256, 128), F32)],
        compiler_params=_params("arbitrary"),
        name="gla_mixer",
    )(z, wa, ba, g)


_RET_LOG_GAMMA = [math.log1p(-(2.0 ** (-5.0 - hh))) for hh in range(RET_HEADS)]


def _by_head(head, values):
    out = jnp.full(head.shape, values[-1], F32)
    for hh in range(len(values) - 2, -1, -1):
        out = jnp.where(head == hh, values[hh], out)
    return out


def _ret_kernel(z_ref, cos_ref, sina_ref, sinb_ref, g_ref, o_ref, st_ref, oacc_ref, *, tiles_per_seq):
    C = RET_CHUNK
    tt = z_ref.shape[0]

    @pl.when(pl.program_id(0) % tiles_per_seq == 0)
    def _():
        st_ref[...] = jnp.zeros_like(st_ref)

    cos, sina, sinb = cos_ref[...], sina_ref[...], sinb_ref[...]

    def rotate(t):
        return t * cos + pltpu.roll(t, 112, 1) * sina + pltpu.roll(t, 16, 1) * sinb

    q_rot = rotate(z_ref[:, 0:128].astype(F32))
    k_rot = rotate(z_ref[:, 128:256].astype(F32)) * (RET_DK ** -0.5)
    gate_in = z_ref[:, 512:768].astype(F32)

    qk_head = _iota((C, 128), 1) >> 5
    v_head = _iota((C, 256), 1) >> 6
    lg_qk = _by_head(_iota((1, 128), 1) >> 5, _RET_LOG_GAMMA)
    lg_v = _by_head(_iota((1, 256), 1) >> 6, _RET_LOG_GAMMA)
    pos_qk = _iota((C, 128), 0).astype(F32)
    pos_v = _iota((C, 256), 0).astype(F32)
    key_decay = jnp.exp((C - 1.0 - pos_qk) * lg_qk)
    query_decay = jnp.exp((pos_v + 1.0) * lg_v)
    chunk_decay = jnp.exp(float(C) * lg_qk)
    srow = _iota((4 * C, C), 0)
    rel = ((srow & (C - 1)) - _iota((4 * C, C), 1)).astype(F32)
    decay_mat = jnp.where(rel >= 0, jnp.exp(jnp.maximum(rel, 0.0) * _by_head(srow >> 7, _RET_LOG_GAMMA)), 0.0)
    state_mask = (_iota((256, 128), 0) >> 6) == (_iota((256, 128), 1) >> 5)

    q_bf = q_rot.astype(BF16)
    k_bf = k_rot.astype(BF16)
    state = st_ref[...]
    for c in range(tt // C):
        sl = slice(c * C, (c + 1) * C)
        qc = q_bf[sl]
        v = z_ref[sl, 256:512]
        q_stack = jnp.concatenate([jnp.where(qk_head == hh, qc, 0) for hh in range(RET_HEADS)], axis=0)
        att = _dot_nt(q_stack, k_bf[sl]) * decay_mat
        intra = _head_select(_dot(att.astype(BF16), v), v_head, RET_HEADS, C)
        inter = _dot_nt(qc, state.astype(BF16)) * query_decay
        oacc_ref[sl, :] = intra + inter
        kv = _dot_tn(v, (k_rot[sl] * key_decay).astype(BF16))
        state = state * chunk_decay + jnp.where(state_mask, kv, 0.0)
    st_ref[...] = state

    o_ref[...] = _head_norm_gate(oacc_ref[...], gate_in, g_ref[...]).astype(BF16)


def _retention(z, cos, sina, sinb, g, seq):
    T = z.shape[0]
    tt = TOK_TILE
    tps = seq // tt
    pos_spec = pl.BlockSpec((tt, 128), lambda i: (i % tps, 0))
    return pl.pallas_call(
        functools.partial(_ret_kernel, tiles_per_seq=tps),
        grid=(T // tt,),
        in_specs=[pl.BlockSpec((tt, RET_W), lambda i: (i, 0)), pos_spec, pos_spec, pos_spec, _full((1, 256))],
        out_specs=pl.BlockSpec((tt, 256), lambda i: (i, 0)),
        out_shape=jax.ShapeDtypeStruct((T, 256), BF16),
        scratch_shapes=[pltpu.VMEM((256, 128), F32), pltpu.VMEM((tt, 256), F32)],
        compiler_params=_params("arbitrary"),
        name="retention_mixer",
    )(z, cos, sina, sinb, g)


def _rope_tables(seq):
    half = RET_DK // 2
    pos = jnp.arange(seq, dtype=F32)
    inv_freq = RET_ROPE_BASE ** (-jnp.arange(half, dtype=F32) * 2.0 / RET_DK)
    ang = pos[:, None] * inv_freq[None, :]
    cos = jnp.tile(jnp.cos(ang), (1, LANES // half))
    sin = jnp.tile(jnp.sin(ang), (1, LANES // half))
    first_half = (jnp.arange(LANES) % RET_DK) < half
    return cos, jnp.where(first_half, -sin, 0.0), jnp.where(first_half, 0.0, sin)


def _swa_kernel(sink_ref, cur_ref, prev_ref, o_ref, s_ref, *, tiles_per_seq):
    W = SWA_WINDOW
    HD = SWA_HD
    tt = cur_ref.shape[0]
    first_key = jnp.where(pl.program_id(0) % tiles_per_seq == 0, W, 0)
    sj = _iota((2 * W, W), 0)
    qi = _iota((2 * W, W), 1)
    in_window = (sj > qi) & (sj <= qi + W)
    kv0 = SWA_HEADS * HD
    no_head = jnp.zeros((HD, W), BF16)
    n_blocks = tt // W

    def band(c, col0):
        prev = prev_ref[:, col0:col0 + 128] if c == 0 else cur_ref[(c - 1) * W:c * W, col0:col0 + 128]
        return jnp.concatenate([prev, cur_ref[c * W:(c + 1) * W, col0:col0 + 128]], axis=0)

    for c in range(n_blocks):
        mask = in_window & (sj >= first_key) if c == 0 else in_window
        k_band = band(c, kv0)
        q_t = cur_ref[c * W:(c + 1) * W, 0:kv0].astype(F32).T.astype(BF16)
        for hh in range(SWA_HEADS):
            kk = hh // (SWA_HEADS // SWA_KV_HEADS)
            qh_t = q_t[hh * HD:(hh + 1) * HD]
            q_on_kv = jnp.concatenate([qh_t, no_head] if kk == 0 else [no_head, qh_t], axis=0)
            s = _dot(k_band, q_on_kv) * (HD ** -0.5 * LOG2_E)
            s_ref[c, hh] = jnp.where(mask, s, NEG_INF)
    for c in range(n_blocks):
        v_t = band(c, kv0 + 128).astype(F32).T.astype(BF16)
        outs = []
        for hh in range(SWA_HEADS):
            kk = hh // (SWA_HEADS // SWA_KV_HEADS)
            s = s_ref[c, hh]
            sink = sink_ref[hh] * LOG2_E
            m = jnp.maximum(jnp.max(s, axis=0, keepdims=True), sink)
            p = jnp.exp2(s - m)
            denom = jnp.sum(p, axis=0, keepdims=True) + jnp.exp2(sink - m)
            outs.append(_dot(v_t[kk * HD:(kk + 1) * HD], p.astype(BF16)) / denom)
        o_ref[c * W:(c + 1) * W, :] = jnp.concatenate(outs, axis=0).T.astype(BF16)


def _swa(z, sinks, seq):
    T = z.shape[0]
    tt = TOK_TILE
    per = tt // SWA_WINDOW
    return pl.pallas_call(
        functools.partial(_swa_kernel, tiles_per_seq=seq // tt),
        grid=(T // tt,),
        in_specs=[pl.BlockSpec(memory_space=pltpu.SMEM),
                  pl.BlockSpec((tt, SWA_W), lambda i: (i, 0)),
                  pl.BlockSpec((SWA_WINDOW, SWA_W), lambda i: (jnp.maximum(i * per - 1, 0), 0))],
        out_specs=pl.BlockSpec((tt, 256), lambda i: (i, 0)),
        out_shape=jax.ShapeDtypeStruct((T, 256), BF16),
        scratch_shapes=[pltpu.VMEM((per, SWA_HEADS, 2 * SWA_WINDOW, SWA_WINDOW), F32)],
        compiler_params=_params("parallel"),
        name="swa_mixer",
    )(sinks, z, z)


_KMEAN_BLOCKS = 8


def _kmean_kernel(k_ref, o_ref):
    k = k_ref[...].astype(F32).reshape(_KMEAN_BLOCKS, MOBA_BLOCK, 256)
    o_ref[...] = jnp.mean(k, axis=1)


def _moba_kmean(z):
    T = z.shape[0]
    rows = _KMEAN_BLOCKS * MOBA_BLOCK
    return pl.pallas_call(
        _kmean_kernel,
        grid=(T // rows,),
        in_specs=[pl.BlockSpec((rows, 256), lambda i: (i, 1))],
        out_specs=pl.BlockSpec((_KMEAN_BLOCKS, 256), lambda i: (i, 0)),
        out_shape=jax.ShapeDtypeStruct((T // MOBA_BLOCK, 256), F32),
        compiler_params=_params("parallel"),
        name="moba_kmean",
    )(z)


MOBA_VROWS = 80
MOBA_PREP_BLOCKS = 2


def _moba_prep_kernel(z_ref, km_ref, qt_out, k_out, vt_out):
    HD = MOBA_HD
    tq = MOBA_BLOCK
    nb = km_ref.shape[0]
    blk_i = _iota((nb, tq), 0)
    blk = blk_i.astype(F32)
    pad = LANES - HD - nb
    ones_row = (_iota((MOBA_VROWS - HD, tq), 0) == 0).astype(BF16)
    for sb in range(MOBA_PREP_BLOCKS):
        rows = slice(sb * tq, (sb + 1) * tq)
        qb = (pl.program_id(0) * MOBA_PREP_BLOCKS + sb) % nb
        this_block_lanes = (_iota((tq, nb), 1) == qb).astype(BF16)
        for hh in range(MOBA_HEADS):
            cols = slice(hh * HD, (hh + 1) * HD)
            q_t = z_ref[rows, cols].astype(F32).T
            gate = _dot(km_ref[:, cols], q_t, precision=HIGHEST)
            gate = jnp.where(blk_i < qb, gate, NEG_INF)
            keep = jnp.where(blk_i == qb, 1.0, 0.0)
            for _ in range(MOBA_TOPK):
                best = jnp.max(gate, axis=0, keepdims=True)
                first = jnp.min(jnp.where(gate == best, blk, float(nb)), axis=0, keepdims=True)
                hit = blk == first
                keep = keep + jnp.where(hit, jnp.where(best > 0.5 * NEG_INF, 1.0, 0.0), 0.0)
                gate = jnp.where(hit, BELOW_NEG_INF, gate)
            bias = ((keep - 1.0) * (-NEG_INF)).astype(BF16)
            qt_out[hh, sb] = jnp.concatenate(
                [(q_t * (HD ** -0.5 * LOG2_E)).astype(BF16), bias, jnp.zeros((pad, tq), BF16)], axis=0)
            k_out[hh, rows, :] = jnp.concatenate(
                [z_ref[rows, 256 + hh * HD:256 + (hh + 1) * HD], this_block_lanes, jnp.zeros((tq, pad), BF16)],
                axis=1)
            v_t = z_ref[rows, 512 + hh * HD:512 + (hh + 1) * HD].astype(F32).T.astype(BF16)
            vt_out[hh, sb] = jnp.concatenate([v_t, ones_row], axis=0)


def _moba_prep(z, kmean, seq):
    T = z.shape[0]
    nb = seq // MOBA_BLOCK
    assert MOBA_HD + nb <= LANES and nb % MOBA_PREP_BLOCKS == 0
    tq = MOBA_BLOCK
    tile = MOBA_PREP_BLOCKS * tq
    H = MOBA_HEADS

    def per_block_t(rows):
        return pl.BlockSpec((H, MOBA_PREP_BLOCKS, rows, tq), lambda i: (0, i, 0, 0))

    return pl.pallas_call(
        _moba_prep_kernel,
        grid=(T // tile,),
        in_specs=[pl.BlockSpec((tile, MOBA_W), lambda i: (i, 0)),
                  pl.BlockSpec((nb, 256), lambda i: (i * MOBA_PREP_BLOCKS // nb, 0))],
        out_specs=[per_block_t(LANES), pl.BlockSpec((H, tile, LANES), lambda i: (0, i, 0)),
                   per_block_t(MOBA_VROWS)],
        out_shape=[jax.ShapeDtypeStruct((H, T // tq, LANES, tq), BF16), jax.ShapeDtypeStruct((H, T, LANES), BF16),
                   jax.ShapeDtypeStruct((H, T // tq, MOBA_VROWS, tq), BF16)],
        compiler_params=_params("parallel"),
        name="moba_prep",
    )(z, kmean)


MOBA_QBLOCKS = 4
MOBA_KBLOCKS = 2
MOBA_DIAGONALS = MOBA_QBLOCKS // MOBA_KBLOCKS


def _moba_kernel(qidx_ref, kidx_ref, kind_ref, qt_ref, k_ref, vt_ref, o_ref, m_ref, acc_ref, s_ref):
    HD = MOBA_HD
    B = MOBA_BLOCK
    step = pl.program_id(1)
    kind = kind_ref[step]
    is_last = (step + 1 == pl.num_programs(1)) | (qidx_ref[jnp.minimum(step + 1, pl.num_programs(1) - 1)]
                                                  != qidx_ref[step])

    def attend(diagonal):
        causal = _iota((B, B), 0) <= _iota((B, B), 1)

        def key_blocks_of(qi):
            if diagonal is None:
                return list(range(MOBA_KBLOCKS))
            return [c for c in range(MOBA_KBLOCKS) if MOBA_KBLOCKS * diagonal + c <= qi]

        for hh in range(MOBA_HEADS):
            for qi in range(MOBA_QBLOCKS):
                for c in key_blocks_of(qi):
                    s = _dot(k_ref[hh, c * B:(c + 1) * B, :], qt_ref[hh, qi])
                    if diagonal is not None and MOBA_KBLOCKS * diagonal + c == qi:
                        s = jnp.where(causal, s, NEG_INF)
                    s_ref[hh, qi, c] = s
        for hh in range(MOBA_HEADS):
            for qi in range(MOBA_QBLOCKS):
                key_blocks = key_blocks_of(qi)
                if not key_blocks:
                    continue
                scores = [s_ref[hh, qi, c] for c in key_blocks]
                m_new = functools.reduce(jnp.maximum, [jnp.max(s, axis=0, keepdims=True) for s in scores])
                first_visit = diagonal == 0
                if not first_visit:
                    m_old = m_ref[hh, qi]
                    m_new = jnp.maximum(m_old, m_new)
                pv = None
                for c, s in zip(key_blocks, scores):
                    term = _dot(vt_ref[hh, c], jnp.exp2(s - m_new).astype(BF16))
                    pv = term if pv is None else pv + term
                acc_ref[hh, qi] = pv if first_visit else jnp.exp2(m_old - m_new) * acc_ref[hh, qi] + pv
                m_ref[hh, qi] = m_new

    @pl.when(kind == 0)
    def _():
        attend(None)

    for d in range(MOBA_DIAGONALS):
        @pl.when(kind == 1 + d)
        def _(d=d):
            attend(d)

    @pl.when(is_last)
    def _():
        pad = jnp.zeros((LANES - MOBA_VROWS, B), F32)
        for qi in range(MOBA_QBLOCKS):
            outs = []
            for hh in range(MOBA_HEADS):
                a = jnp.concatenate([acc_ref[hh, qi], pad], axis=0).T
                outs.append(a[:, 0:HD] / a[:, HD:HD + 1])
            o_ref[qi * B:(qi + 1) * B, :] = jnp.concatenate(outs, axis=1).astype(BF16)


def _moba(z, seq):
    T = z.shape[0]
    nb = seq // MOBA_BLOCK
    assert nb % MOBA_QBLOCKS == 0
    q_tiles = nb // MOBA_QBLOCKS
    k_steps = nb // MOBA_KBLOCKS
    D = MOBA_DIAGONALS
    batch = T // seq
    qt_aug, k_aug, vt_aug = _moba_prep(z, _moba_kmean(z), seq)
    qidx = np.concatenate([np.full(D * t + D, t) for t in range(q_tiles)]).astype(np.int32)
    kidx = np.concatenate([np.concatenate([D * t + np.arange(D), np.arange(D * t)])
                           for t in range(q_tiles)]).astype(np.int32)
    kind = np.concatenate([np.concatenate([1 + np.arange(D), np.zeros(D * t, np.int64)])
                           for t in range(q_tiles)]).astype(np.int32)
    tq = MOBA_QBLOCKS * MOBA_BLOCK
    tk = MOBA_KBLOCKS * MOBA_BLOCK
    H = MOBA_HEADS
    grid_spec = pltpu.PrefetchScalarGridSpec(
        num_scalar_prefetch=3,
        grid=(batch, len(qidx)),
        in_specs=[pl.BlockSpec((H, MOBA_QBLOCKS, LANES, MOBA_BLOCK),
                               lambda b, s, qi, ki, kd: (0, b * q_tiles + qi[s], 0, 0)),
                  pl.BlockSpec((H, tk, LANES), lambda b, s, qi, ki, kd: (0, b * k_steps + ki[s], 0)),
                  pl.BlockSpec((H, MOBA_KBLOCKS, MOBA_VROWS, MOBA_BLOCK),
                               lambda b, s, qi, ki, kd: (0, b * k_steps + ki[s], 0, 0))],
        out_specs=pl.BlockSpec((tq, 256), lambda b, s, qi, ki, kd: (b * q_tiles + qi[s], 0)),
        scratch_shapes=[pltpu.VMEM((H, MOBA_QBLOCKS, 1, MOBA_BLOCK), F32),
                        pltpu.VMEM((H, MOBA_QBLOCKS, MOBA_VROWS, MOBA_BLOCK), F32),
                        pltpu.VMEM((H, MOBA_QBLOCKS, MOBA_KBLOCKS, MOBA_BLOCK, MOBA_BLOCK), F32)],
    )
    return pl.pallas_call(
        _moba_kernel,
        grid_spec=grid_spec,
        out_shape=jax.ShapeDtypeStruct((T, 256), BF16),
        compiler_params=_params("parallel", "arbitrary"),
        name="moba_mixer",
    )(jnp.asarray(qidx), jnp.asarray(kidx), jnp.asarray(kind), qt_aug, k_aug, vt_aug)


def _merge_kernel(h_ref, o0, o1, o2, o3, half_gates_ref, half_wb_ref, wout_ref, out_ref):
    merged = None
    for i, o in enumerate((o0, o1, o2, o3)):
        half_y = _dot(o[...], half_wb_ref[i])
        term = half_y * jnp.tanh(half_gates_ref[:, i * D_MODEL:(i + 1) * D_MODEL].astype(F32)) + half_y
        merged = term if merged is None else merged + term
    out_ref[...] = h_ref[...] + _dot(merged.astype(BF16), wout_ref[...])


def _merge(h, branches, gates, wb, wout):
    T = h.shape[0]
    tm = TOK_TILE
    return pl.pallas_call(
        _merge_kernel,
        grid=(T // tm,),
        in_specs=[pl.BlockSpec((tm, D_MODEL), lambda i: (i, 0))]
        + [pl.BlockSpec((tm, 256), lambda i: (i, 0))] * 4
        + [pl.BlockSpec((tm, GATE_W), lambda i: (i, 0)),
           pl.BlockSpec((N_BRANCH, 256, D_MODEL), lambda i: (0, 0, 0), pipeline_mode=pl.Buffered(1)),
           pl.BlockSpec((D_MODEL, D_MODEL), lambda i: (0, 0), pipeline_mode=pl.Buffered(1))],
        out_specs=pl.BlockSpec((tm, D_MODEL), lambda i: (i, 0)),
        out_shape=jax.ShapeDtypeStruct((T, D_MODEL), F32),
        compiler_params=_params("parallel"),
        name="merge_out_proj",
    )(h, *branches, gates, wb, wout)


ROW_HALF = D_MODEL // 2
ROW_TILES = ROW_HALF // LANES
HIGH_HALF_MASK = -65536


def _row_tile_spec(rows, index_map):
    return pl.BlockSpec((rows * ROW_TILES, LANES), index_map)


def _bf16_bits(x):
    return lax.bitcast_convert_type(x.astype(BF16).astype(F32), I32)


def _pack_rows(ref, x):
    for j in range(ROW_TILES):
        low = _bf16_bits(x[:, j * LANES:(j + 1) * LANES])
        high = _bf16_bits(x[:, ROW_HALF + j * LANES:ROW_HALF + (j + 1) * LANES])
        ref[pl.ds(j, x.shape[0], stride=ROW_TILES), :] = high | lax.shift_right_logical(low, 16)


def _unpack_slab(ref, j):
    words = ref[pl.ds(j, ref.shape[0] // ROW_TILES, stride=ROW_TILES), :]
    low = lax.bitcast_convert_type(lax.shift_left(words, 16), F32)
    high = lax.bitcast_convert_type(words & HIGH_HALF_MASK, F32)
    return low, high


def _unpack_rows(ref):
    halves = [_unpack_slab(ref, j) for j in range(ROW_TILES)]
    return jnp.concatenate([lo for lo, _ in halves] + [hi for _, hi in halves], axis=1).astype(BF16)


def _router_kernel(h_ref, g_ref, wr_ref, br_ref, u_ref, route_ref, route_t_ref, count_ref, carry_ref):
    tm = h_ref.shape[0]

    @pl.when(pl.program_id(0) == 0)
    def _():
        carry_ref[...] = jnp.zeros_like(carry_ref)

    u = _rms(h_ref[...], g_ref[...])
    _pack_rows(u_ref, u)
    u_hi, u_lo = _split_bf16(u, 2)
    w_hi, w_lo = _split_bf16(wr_ref[...], 2)
    logits = _dot(u_hi, w_hi) + (_dot(u_lo, w_hi) + _dot(u_hi, w_lo)) + br_ref[...]
    lt = logits.T
    row_i = _iota((LANES, tm), 0)
    row = row_i.astype(F32)
    row_group = ((row_i - EXPERT_LANE0) >> 3).astype(F32)

    def first_row(mask):
        return jnp.min(jnp.where(mask, row, float(LANES)), axis=0, keepdims=True)

    grp = jnp.where(row_i < N_GROUPS, lt, BELOW_NEG_INF)
    grp_e = jnp.exp(grp - jnp.max(grp, axis=0, keepdims=True))
    p_group = grp_e / jnp.sum(grp_e, axis=0, keepdims=True)
    g_w = jnp.max(p_group, axis=0, keepdims=True)
    g_sel = first_row(p_group == g_w)

    in_group = (row_i >= EXPERT_LANE0) & (row_group == g_sel)
    el = jnp.where(in_group, lt, BELOW_NEG_INF)
    top1 = jnp.max(el, axis=0, keepdims=True)
    row1 = first_row(el == top1)
    el = jnp.where(row == row1, BELOW_NEG_INF, el)
    top2 = jnp.max(el, axis=0, keepdims=True)
    row2 = first_row(el == top2)
    e2 = jnp.exp(top2 - top1)
    w1 = g_w / (1.0 + e2)
    w2 = g_w * e2 / (1.0 + e2)

    onehot = ((row == row1) | (row == row2)).astype(BF16)
    earlier = (_iota((tm, tm), 0) < _iota((tm, tm), 1)).astype(BF16)
    seen = _dot(onehot, earlier) + carry_ref[:, 0:1]
    rank1 = jnp.sum(jnp.where(row == row1, seen, 0.0), axis=0, keepdims=True)
    rank2 = jnp.sum(jnp.where(row == row2, seen, 0.0), axis=0, keepdims=True)
    carry_ref[...] += jnp.sum(onehot.astype(F32), axis=1, keepdims=True)
    count_ref[...] = carry_ref[...]

    out_t = jnp.zeros((LANES, tm), F32)
    for idx, val in enumerate((row1 - EXPERT_LANE0, row2 - EXPERT_LANE0, w1, w2, rank1, rank2)):
        out_t = jnp.where(row_i == idx, val, out_t)
    route_t_ref[...] = out_t[0:8, :]
    route_ref[...] = out_t.T


def _router(h, g, wr, br):
    T = h.shape[0]
    tm = TOK_TILE
    return pl.pallas_call(
        _router_kernel,
        grid=(T // tm,),
        in_specs=[pl.BlockSpec((tm, D_MODEL), lambda i: (i, 0)), _full((1, D_MODEL)),
                  _full((D_MODEL, LANES)), _full((1, LANES))],
        out_specs=[_row_tile_spec(tm, lambda i: (i, 0)), pl.BlockSpec((tm, LANES), lambda i: (i, 0)),
                   pl.BlockSpec((8, tm), lambda i: (0, i)), _full((LANES, LANES))],
        out_shape=[jax.ShapeDtypeStruct((T * ROW_TILES, LANES), I32), jax.ShapeDtypeStruct((T, LANES), F32),
                   jax.ShapeDtypeStruct((8, T), F32), jax.ShapeDtypeStruct((LANES, LANES), F32)],
        scratch_shapes=[pltpu.VMEM((LANES, LANES), F32)],
        compiler_params=_params("arbitrary"),
        name="moe_router",
    )(h, g, wr, br)


ROW_UNROLL = 8


def _row_copy(src_ref, src_row, dst_ref, dst_row, sem):
    src = src_ref.at[pl.ds(pl.multiple_of(src_row * ROW_TILES, ROW_TILES), ROW_TILES)]
    dst = dst_ref.at[pl.ds(pl.multiple_of(dst_row * ROW_TILES, ROW_TILES), ROW_TILES)]
    return pltpu.make_async_copy(src, dst, sem)


def _for_each_row_pair(rows, body):
    def group(g, carry):
        for i in range(ROW_UNROLL):
            for k in range(EXPERT_TOPK):
                body(g * ROW_UNROLL + i, k)
        return carry

    lax.fori_loop(0, rows // ROW_UNROLL, group, 0)


def _dispatch_kernel(dest0_ref, dest1_ref, pstart_ref, pend_ref, u_ref, xs_ref, zero_ref, sem):
    tm = u_ref.shape[0] // ROW_TILES
    tile = pl.program_id(0)
    dests = (dest0_ref, dest1_ref)

    @pl.when(tile == 0)
    def _():
        zero_ref[...] = jnp.zeros_like(zero_ref)

        def zero_block(first_row):
            start = pl.multiple_of(first_row * ROW_TILES, ROW_TILES)
            fill = pltpu.make_async_copy(zero_ref, xs_ref.at[pl.ds(start, MOE_BLOCK * ROW_TILES)], sem)
            fill.start()
            fill.wait()

        def zero_tail(e, carry):
            @pl.when(pend_ref[e] > pstart_ref[e])
            def _():
                zero_block(pend_ref[e] - MOE_BLOCK)
            return carry

        def zero_unused(b, carry):
            zero_block(b * MOE_BLOCK)
            return carry

        lax.fori_loop(0, N_EXPERTS, zero_tail, 0)
        lax.fori_loop(pend_ref[N_EXPERTS - 1] // MOE_BLOCK, xs_ref.shape[0] // (MOE_BLOCK * ROW_TILES),
                      zero_unused, 0)

    def issue(r, k):
        _row_copy(u_ref, r, xs_ref, dests[k][tile * tm + r], sem).start(priority=k)

    _for_each_row_pair(tm, issue)
    for _ in range(EXPERT_TOPK):
        pltpu.make_async_copy(u_ref, xs_ref.at[pl.ds(0, tm * ROW_TILES)], sem).wait()


def _dispatch(dest0, dest1, pstart, pend, u, n_slots):
    T = u.shape[0] // ROW_TILES
    tm = DISPATCH_TILE
    grid_spec = pltpu.PrefetchScalarGridSpec(
        num_scalar_prefetch=4,
        grid=(T // tm,),
        in_specs=[_row_tile_spec(tm, lambda i, *_: (i, 0))],
        out_specs=pl.BlockSpec(memory_space=pl.ANY),
        scratch_shapes=[pltpu.VMEM((MOE_BLOCK * ROW_TILES, LANES), I32), pltpu.SemaphoreType.DMA(())],
    )
    return pl.pallas_call(
        _dispatch_kernel,
        grid_spec=grid_spec,
        out_shape=jax.ShapeDtypeStruct((n_slots * ROW_TILES, LANES), I32),
        compiler_params=_params("arbitrary"),
        name="moe_dispatch",
    )(dest0, dest1, pstart, pend, u)


def _expert_kernel(blk_e_ref, n_used_ref, x_ref, w1_ref, w3_ref, w2_ref, y_ref, w1b_ref, w3b_ref, w2b_ref):
    i = pl.program_id(0)
    used = i < n_used_ref[0]

    @pl.when((i == 0) | (blk_e_ref[i] != blk_e_ref[jnp.maximum(i - 1, 0)]))
    def _():
        w1b_ref[...] = w1_ref[0, 0].astype(BF16)
        w3b_ref[...] = w3_ref[0, 0].astype(BF16)
        w2b_ref[...] = w2_ref[0, 0].astype(BF16)

    @pl.when(used)
    def _():
        x = _unpack_rows(x_ref)
        a = _dot(x, w1b_ref[...])
        hidden = (a * _sigmoid(a)) * _dot(x, w3b_ref[...])
        _pack_rows(y_ref, _dot(hidden.astype(BF16), w2b_ref[...]))

    @pl.when(jnp.logical_not(used))
    def _():
        y_ref[...] = jnp.zeros_like(y_ref)


def _experts(blk_e, n_used, xs, w1, w3, w2, layer):
    n_slots = xs.shape[0] // ROW_TILES
    bm = MOE_BLOCK
    grid_spec = pltpu.PrefetchScalarGridSpec(
        num_scalar_prefetch=2,
        grid=(n_slots // bm,),
        in_specs=[_row_tile_spec(bm, lambda i, be, nu: (jnp.minimum(i, nu[0] - 1), 0)),
                  pl.BlockSpec((1, 1, D_MODEL, EXPERT_HIDDEN), lambda i, be, nu: (layer, be[i], 0, 0)),
                  pl.BlockSpec((1, 1, D_MODEL, EXPERT_HIDDEN), lambda i, be, nu: (layer, be[i], 0, 0)),
                  pl.BlockSpec((1, 1, EXPERT_HIDDEN, D_MODEL), lambda i, be, nu: (layer, be[i], 0, 0))],
        out_specs=_row_tile_spec(bm, lambda i, be, nu: (i, 0)),
        scratch_shapes=[pltpu.VMEM((D_MODEL, EXPERT_HIDDEN), BF16), pltpu.VMEM((D_MODEL, EXPERT_HIDDEN), BF16),
                        pltpu.VMEM((EXPERT_HIDDEN, D_MODEL), BF16)],
    )
    return pl.pallas_call(
        _expert_kernel,
        grid_spec=grid_spec,
        out_shape=jax.ShapeDtypeStruct((n_slots * ROW_TILES, LANES), I32),
        compiler_params=_params("arbitrary"),
        name="moe_experts",
    )(blk_e, n_used, xs, w1, w3, w2)


def _combine_kernel(dest0_ref, dest1_ref, h_ref, route_ref, g_ref, ys_ref, out_ref, y_ref, sem, *, normalize):
    tm = h_ref.shape[0]
    step = pl.program_id(0)
    slot = step % 2

    dests = (dest0_ref, dest1_ref)

    def gather_tile(tile, buf):
        def issue(r, k):
            _row_copy(ys_ref, dests[k][tile * tm + r], y_ref.at[buf, k], r, sem.at[buf]).start(priority=k)

        _for_each_row_pair(tm, issue)

    @pl.when(step == 0)
    def _():
        gather_tile(0, 0)

    @pl.when(step + 1 < pl.num_programs(0))
    def _():
        gather_tile(step + 1, 1 - slot)

    for k in range(EXPERT_TOPK):
        pltpu.make_async_copy(ys_ref.at[pl.ds(0, tm * ROW_TILES)], y_ref.at[slot, k], sem.at[slot]).wait()
    w0 = route_ref[:, 2:3]
    w1 = route_ref[:, 3:4]
    for j in range(ROW_TILES):
        low0, high0 = _unpack_slab(y_ref.at[slot, 0], j)
        low1, high1 = _unpack_slab(y_ref.at[slot, 1], j)
        cols = slice(j * LANES, (j + 1) * LANES)
        out_ref[:, cols] = h_ref[:, cols] + w0 * low0 + w1 * low1
        cols = slice(ROW_HALF + j * LANES, ROW_HALF + (j + 1) * LANES)
        out_ref[:, cols] = h_ref[:, cols] + w0 * high0 + w1 * high1
    if normalize:
        out_ref[...] = _rms(out_ref[...], g_ref[...])


def _combine(dest0, dest1, h, route, ys, final_g, normalize):
    T = h.shape[0]
    tm = ROW_TILE
    grid_spec = pltpu.PrefetchScalarGridSpec(
        num_scalar_prefetch=2,
        grid=(T // tm,),
        in_specs=[pl.BlockSpec((tm, D_MODEL), lambda i, *_: (i, 0)),
                  pl.BlockSpec((tm, LANES), lambda i, *_: (i, 0)),
                  _full((1, D_MODEL)),
                  pl.BlockSpec(memory_space=pl.ANY)],
        out_specs=pl.BlockSpec((tm, D_MODEL), lambda i, *_: (i, 0)),
        scratch_shapes=[pltpu.VMEM((2, EXPERT_TOPK, tm * ROW_TILES, LANES), I32),
                        pltpu.SemaphoreType.DMA((2,))],
    )
    return pl.pallas_call(
        functools.partial(_combine_kernel, normalize=normalize),
        grid_spec=grid_spec,
        out_shape=jax.ShapeDtypeStruct((T, D_MODEL), F32),
        compiler_params=_params("arbitrary"),
        name="moe_combine",
    )(dest0, dest1, h, route, final_g, ys)


def _moe(h, g, w_group, b_group, w_expert, b_expert, w1, w3, w2, layer, final_g, normalize):
    T = h.shape[0]
    wr = jnp.zeros((D_MODEL, LANES), F32)
    wr = wr.at[:, :N_GROUPS].set(w_group).at[:, EXPERT_LANE0:EXPERT_LANE0 + N_EXPERTS].set(w_expert)
    br = jnp.zeros((1, LANES), F32)
    br = br.at[0, :N_GROUPS].set(b_group).at[0, EXPERT_LANE0:EXPERT_LANE0 + N_EXPERTS].set(b_expert)
    u, route, route_t, count = _router(h, g, wr, br)

    n_blocks = (T * EXPERT_TOPK) // MOE_BLOCK + N_EXPERTS
    counts = count[EXPERT_LANE0:EXPERT_LANE0 + N_EXPERTS, 0].astype(I32)
    padded = ((counts + MOE_BLOCK - 1) // MOE_BLOCK) * MOE_BLOCK
    pend = jnp.cumsum(padded)
    pstart = pend - padded
    expert_ids = jnp.arange(N_EXPERTS, dtype=F32)[:, None]

    def slot_of(expert_row, rank_row):
        segment_start = jnp.sum(jnp.where(expert_row[None, :] == expert_ids, pstart[:, None], 0), axis=0)
        return segment_start + rank_row.astype(I32)

    dest0 = slot_of(route_t[0], route_t[4])
    dest1 = slot_of(route_t[1], route_t[5])
    block_row0 = jnp.arange(n_blocks, dtype=I32) * MOE_BLOCK
    blk_e = jnp.minimum(jnp.sum((pend[None, :] <= block_row0[:, None]).astype(I32), axis=1), N_EXPERTS - 1)
    n_used = (pend[-1:] // MOE_BLOCK).astype(I32)

    xs = _dispatch(dest0, dest1, pstart, pend, u, n_blocks * MOE_BLOCK)
    ys = _experts(blk_e, n_used, xs, w1, w3, w2, layer)
    return _combine(dest0, dest1, h, route, ys, final_g, normalize)


def _split_w_in(w_in):
    gla_n = GLA_HEADS * (2 * GLA_DK + 2 * GLA_DV) + GLA_LOWRANK
    swa_n = (SWA_HEADS + 2 * SWA_KV_HEADS) * SWA_HD
    moba_n = 3 * MOBA_HEADS * MOBA_HD
    ret_n = RET_HEADS * (2 * RET_DK + 2 * RET_DV)
    offs = np.cumsum([0, gla_n, swa_n, moba_n, ret_n, GATE_W])
    parts = [w_in[:, offs[i]:offs[i + 1]].astype(BF16) for i in range(5)]
    parts[0] = jnp.pad(parts[0], ((0, 0), (0, GLA_W - gla_n)))
    parts[4] = parts[4] * 0.5
    return parts


def _token_mixer(h, seq, ln1_g, w_in, gla_w_a2, gla_b_a, gla_norm_g, swa_sinks, ret_norm_g, w_branch, w_out,
                 rope):
    z_gla, z_swa, z_moba, z_ret, z_gates = _in_proj(h, ln1_g.reshape(1, D_MODEL), _split_w_in(w_in))
    wa = jnp.zeros((128, 128), F32).at[:GLA_LOWRANK].set(gla_w_a2)
    o_gla = _gla(z_gla, wa, gla_b_a.reshape(1, 128), jnp.tile(gla_norm_g, GLA_HEADS).reshape(1, 256), seq)
    o_swa = _swa(z_swa, swa_sinks, seq)
    o_moba = _moba(z_moba, seq)
    o_ret = _retention(z_ret, *rope, jnp.tile(ret_norm_g, RET_HEADS).reshape(1, 256), seq)
    return _merge(h, (o_gla, o_swa, o_moba, o_ret), z_gates, w_branch.astype(BF16) * 0.5, w_out.astype(BF16))


def kernel(x, ln1_g, w_in, gla_w_a2, gla_b_a, gla_norm_g, swa_sinks, ret_norm_g, w_branch, w_out, ln2_g,
           w_group, b_group, w_expert, b_expert, w1, w3, w2, final_g):
    batch, seq, _ = x.shape
    depth = w_in.shape[0]
    assert depth >= 1
    rope = _rope_tables(seq)
    h = x.reshape(batch * seq, D_MODEL)
    for l in range(depth):
        h = _token_mixer(h, seq, ln1_g[l], w_in[l], gla_w_a2[l], gla_b_a[l], gla_norm_g[l], swa_sinks[l],
                         ret_norm_g[l], w_branch[l], w_out[l], rope)
        h = _moe(h, ln2_g[l].reshape(1, D_MODEL), w_group[l], b_group[l], w_expert[l], b_expert[l],
                 w1, w3, w2, l, final_g.reshape(1, D_MODEL), normalize=(l == depth - 1))
    return h.reshape(batch, seq, D_MODEL)
```

```python
import functools
import math

import numpy as np
import jax
import jax.numpy as jnp
from jax import lax
from jax.experimental import pallas as pl
from jax.experimental.pallas import tpu as pltpu

F32 = jnp.float32
BF16 = jnp.bfloat16
I32 = jnp.int32
HIGHEST = lax.Precision.HIGHEST

D_MODEL = 1024
N_BRANCH = 4
NORM_EPS = 1e-6
NEG_INF = -1e30
BELOW_NEG_INF = -3e38
LOG2_E = math.log2(math.e)

GLA_HEADS, GLA_DK, GLA_DV, GLA_LOWRANK, GLA_TAU, GLA_CHUNK = 4, 32, 64, 16, 16.0, 64
SWA_HEADS, SWA_KV_HEADS, SWA_HD, SWA_WINDOW = 4, 2, 64, 128
MOBA_HEADS, MOBA_HD, MOBA_BLOCK, MOBA_TOPK = 4, 64, 256, 3
RET_HEADS, RET_DK, RET_DV, RET_CHUNK, RET_ROPE_BASE = 4, 32, 64, 128, 10000.0
N_GROUPS, EXPERTS_PER_GROUP, EXPERT_TOPK, EXPERT_HIDDEN = 4, 8, 2, 256
MOE_BLOCK = 512
N_EXPERTS = N_GROUPS * EXPERTS_PER_GROUP

GLA_W = 896
SWA_W = 512
MOBA_W = 768
RET_W = 768
GATE_W = N_BRANCH * D_MODEL

LANES = 128
EXPERT_LANE0 = 32

VMEM_LIMIT = 56 * 1024 * 1024

TOK_TILE = 512
ROW_TILE = 512
DISPATCH_TILE = 1024


def _params(*sem):
    return pltpu.CompilerParams(dimension_semantics=sem, vmem_limit_bytes=VMEM_LIMIT)


def _dot(a, b, precision=None):
    return jnp.dot(a, b, preferred_element_type=F32, precision=precision)


def _dot_nt(a, b, precision=None):
    return lax.dot_general(a, b, (((1,), (1,)), ((), ())), preferred_element_type=F32, precision=precision)


def _dot_tn(a, b, precision=None):
    return lax.dot_general(a, b, (((0,), (0,)), ((), ())), preferred_element_type=F32, precision=precision)


def _split_bf16(x, parts):
    out = []
    for _ in range(parts - 1):
        piece = x.astype(BF16)
        out.append(piece)
        x = x - piece.astype(F32)
    out.append(x.astype(BF16))
    return out


def _dot_f32_by_bf16(x, m):
    return sum(_dot(piece, m) for piece in _split_bf16(x, 3))


def _dot_bf16_by_f32(m, x):
    return sum(_dot(m, piece) for piece in _split_bf16(x, 3))


def _iota(shape, dim):
    return lax.broadcasted_iota(I32, shape, dim)


def _sigmoid(x):
    return 0.5 * jnp.tanh(0.5 * x) + 0.5


def _rms(x, g):
    ms = jnp.mean(x * x, axis=-1, keepdims=True)
    return x * lax.rsqrt(ms + NORM_EPS) * g


def _full(shape):
    return pl.BlockSpec(shape, lambda *_: (0,) * len(shape))


def _in_proj_kernel(h_ref, g_ref, w0, w1, w2, w3, w4, o0, o1, o2, o3, o4):
    u = _rms(h_ref[...], g_ref[...]).astype(BF16)
    for w, o in ((w0, o0), (w1, o1), (w2, o2), (w3, o3), (w4, o4)):
        o[...] = _dot(u, w[...]).astype(BF16)


def _in_proj(h, g, ws):
    T = h.shape[0]
    tm = TOK_TILE
    widths = [w.shape[1] for w in ws]
    return pl.pallas_call(
        _in_proj_kernel,
        grid=(T // tm,),
        in_specs=[pl.BlockSpec((tm, D_MODEL), lambda i: (i, 0)), _full((1, D_MODEL))]
        + [pl.BlockSpec((D_MODEL, n), lambda i: (0, 0), pipeline_mode=pl.Buffered(1)) for n in widths],
        out_specs=[pl.BlockSpec((tm, n), lambda i: (i, 0)) for n in widths],
        out_shape=[jax.ShapeDtypeStruct((T, n), BF16) for n in widths],
        compiler_params=_params("parallel"),
        name="in_proj",
    )(h, g, *ws)


def _head_select(x, lane_head, n_heads, rows):
    out = None
    for hh in range(n_heads):
        term = jnp.where(lane_head == hh, x[hh * rows:(hh + 1) * rows], 0.0)
        out = term if out is None else out + term
    return out


def _head_norm_gate(o, gate_in, g):
    n = o.shape[1]
    same_head = (_iota((n, n), 0) >> 6) == (_iota((n, n), 1) >> 6)
    ms = _dot_f32_by_bf16(o * o, same_head.astype(BF16)) * (1.0 / 64.0)
    return o * lax.rsqrt(ms + NORM_EPS) * g * (gate_in * _sigmoid(gate_in))


def _gla_kernel(z_ref, wa_ref, ba_ref, g_ref, o_ref, st_ref, oacc_ref, att_ref, kv_ref, *, tiles_per_seq):
    C = GLA_CHUNK
    tt = z_ref.shape[0]

    @pl.when(pl.program_id(0) % tiles_per_seq == 0)
    def _():
        st_ref[...] = jnp.zeros_like(st_ref)

    q = z_ref[:, 0:128].astype(F32)
    k = z_ref[:, 128:256].astype(F32)
    r = z_ref[:, 512:768].astype(F32)
    pre = _dot_bf16_by_f32(z_ref[:, 768:896], wa_ref[...]) + ba_ref[...]
    log_a = (jnp.minimum(pre, 0.0) - jnp.log(1.0 + jnp.exp(-jnp.abs(pre)))) * (1.0 / GLA_TAU)

    lower = (_iota((C, C), 1) <= _iota((C, C), 0)).astype(BF16)
    b = jnp.concatenate([_dot_bf16_by_f32(lower, log_a[c * C:(c + 1) * C]) for c in range(tt // C)], axis=0)
    b_last = jnp.broadcast_to(b.reshape(tt // C, C, 128)[:, C - 1:C, :], (tt // C, C, 128)).reshape(tt, 128)
    q_dec = (q * (GLA_DK ** -0.5) * jnp.exp(b)).astype(BF16)
    k_dec = (k * jnp.exp(-b)).astype(BF16)
    k_end = (k * jnp.exp(b_last - b)).astype(BF16)
    decay = jnp.exp(b_last)

    qk_head = _iota((C, 128), 1) >> 5
    v_head = _iota((C, 256), 1) >> 6
    causal = _iota((4 * C, C), 1) <= (_iota((4 * C, C), 0) & (C - 1))
    state_mask = (_iota((256, 128), 0) >> 6) == (_iota((256, 128), 1) >> 5)

    for c in range(tt // C):
        sl = slice(c * C, (c + 1) * C)
        q_stack = jnp.concatenate([jnp.where(qk_head == hh, q_dec[sl], 0) for hh in range(GLA_HEADS)], axis=0)
        att_ref[c] = jnp.where(causal, _dot_nt(q_stack, k_dec[sl]), 0.0).astype(BF16)
        kv_ref[c] = jnp.where(state_mask, _dot_tn(z_ref[sl, 256:512], k_end[sl]), 0.0)
    state = st_ref[...]
    for c in range(tt // C):
        sl = slice(c * C, (c + 1) * C)
        intra = _head_select(_dot(att_ref[c], z_ref[sl, 256:512]), v_head, GLA_HEADS, C)
        inter = _dot_nt(q_dec[sl], state.astype(BF16))
        oacc_ref[sl, :] = intra + inter
        state = state * decay[c * C:c * C + 1] + kv_ref[c]
    st_ref[...] = state

    o_ref[...] = _head_norm_gate(oacc_ref[...], r, g_ref[...]).astype(BF16)


def _gla(z, wa, ba, g, seq):
    T = z.shape[0]
    tt = TOK_TILE
    return pl.pallas_call(
        functools.partial(_gla_kernel, tiles_per_seq=seq // tt),
        grid=(T // tt,),
        in_specs=[pl.BlockSpec((tt, GLA_W), lambda i: (i, 0)), _full((128, 128)), _full((1, 128)),
                  _full((1, 256))],
        out_specs=pl.BlockSpec((tt, 256), lambda i: (i, 0)),
        out_shape=jax.ShapeDtypeStruct((T, 256), BF16),
        scratch_shapes=[pltpu.VMEM((256, 128), F32), pltpu.VMEM((tt, 256), F32),
                        pltpu.VMEM((tt // GLA_CHUNK, GLA_HEADS * GLA_CHUNK, GLA_CHUNK), BF16),
                        pltpu.VMEM((tt // GLA_CHUNK, 256, 128), F32)],
        compiler_params=_params("arbitrary"),
        name="gla_mixer",
    )(z, wa, ba, g)


_RET_LOG_GAMMA = [math.log1p(-(2.0 ** (-5.0 - hh))) for hh in range(RET_HEADS)]


def _by_head(head, values):
    out = jnp.full(head.shape, values[-1], F32)
    for hh in range(len(values) - 2, -1, -1):
        out = jnp.where(head == hh, values[hh], out)
    return out


def _ret_kernel(z_ref, cos_ref, sina_ref, sinb_ref, g_ref, o_ref, st_ref, oacc_ref, *, tiles_per_seq):
    C = RET_CHUNK
    tt = z_ref.shape[0]

    @pl.when(pl.program_id(0) % tiles_per_seq == 0)
    def _():
        st_ref[...] = jnp.zeros_like(st_ref)

    cos, sina, sinb = cos_ref[...], sina_ref[...], sinb_ref[...]

    def rotate(t):
        return t * cos + pltpu.roll(t, 112, 1) * sina + pltpu.roll(t, 16, 1) * sinb

    q_rot = rotate(z_ref[:, 0:128].astype(F32))
    k_rot = rotate(z_ref[:, 128:256].astype(F32)) * (RET_DK ** -0.5)
    gate_in = z_ref[:, 512:768].astype(F32)

    qk_head = _iota((C, 128), 1) >> 5
    v_head = _iota((C, 256), 1) >> 6
    lg_qk = _by_head(_iota((1, 128), 1) >> 5, _RET_LOG_GAMMA)
    lg_v = _by_head(_iota((1, 256), 1) >> 6, _RET_LOG_GAMMA)
    pos_qk = _iota((C, 128), 0).astype(F32)
    pos_v = _iota((C, 256), 0).astype(F32)
    key_decay = jnp.exp((C - 1.0 - pos_qk) * lg_qk)
    query_decay = jnp.exp((pos_v + 1.0) * lg_v)
    chunk_decay = jnp.exp(float(C) * lg_qk)
    srow = _iota((4 * C, C), 0)
    rel = ((srow & (C - 1)) - _iota((4 * C, C), 1)).astype(F32)
    decay_mat = jnp.where(rel >= 0, jnp.exp(jnp.maximum(rel, 0.0) * _by_head(srow >> 7, _RET_LOG_GAMMA)), 0.0)
    state_mask = (_iota((256, 128), 0) >> 6) == (_iota((256, 128), 1) >> 5)

    q_bf = q_rot.astype(BF16)
    k_bf = k_rot.astype(BF16)
    state = st_ref[...]
    for c in range(tt // C):
        sl = slice(c * C, (c + 1) * C)
        qc = q_bf[sl]
        v = z_ref[sl, 256:512]
        q_stack = jnp.concatenate([jnp.where(qk_head == hh, qc, 0) for hh in range(RET_HEADS)], axis=0)
        att = _dot_nt(q_stack, k_bf[sl]) * decay_mat
        intra = _head_select(_dot(att.astype(BF16), v), v_head, RET_HEADS, C)
        inter = _dot_nt(qc, state.astype(BF16)) * query_decay
        oacc_ref[sl, :] = intra + inter
        kv = _dot_tn(v, (k_rot[sl] * key_decay).astype(BF16))
        state = state * chunk_decay + jnp.where(state_mask, kv, 0.0)
    st_ref[...] = state

    o_ref[...] = _head_norm_gate(oacc_ref[...], gate_in, g_ref[...]).astype(BF16)


def _retention(z, cos, sina, sinb, g, seq):
    T = z.shape[0]
    tt = TOK_TILE
    tps = seq // tt
    pos_spec = pl.BlockSpec((tt, 128), lambda i: (i % tps, 0))
    return pl.pallas_call(
        functools.partial(_ret_kernel, tiles_per_seq=tps),
        grid=(T // tt,),
        in_specs=[pl.BlockSpec((tt, RET_W), lambda i: (i, 0)), pos_spec, pos_spec, pos_spec, _full((1, 256))],
        out_specs=pl.BlockSpec((tt, 256), lambda i: (i, 0)),
        out_shape=jax.ShapeDtypeStruct((T, 256), BF16),
        scratch_shapes=[pltpu.VMEM((256, 128), F32), pltpu.VMEM((tt, 256), F32)],
        compiler_params=_params("arbitrary"),
        name="retention_mixer",
    )(z, cos, sina, sinb, g)


def _rope_tables(seq):
    half = RET_DK // 2
    pos = jnp.arange(seq, dtype=F32)
    inv_freq = RET_ROPE_BASE ** (-jnp.arange(half, dtype=F32) * 2.0 / RET_DK)
    ang = pos[:, None] * inv_freq[None, :]
    cos = jnp.tile(jnp.cos(ang), (1, LANES // half))
    sin = jnp.tile(jnp.sin(ang), (1, LANES // half))
    first_half = (jnp.arange(LANES) % RET_DK) < half
    return cos, jnp.where(first_half, -sin, 0.0), jnp.where(first_half, 0.0, sin)


def _swa_kernel(sink_ref, cur_ref, prev_ref, o_ref, s_ref, *, tiles_per_seq):
    W = SWA_WINDOW
    HD = SWA_HD
    tt = cur_ref.shape[0]
    first_key = jnp.where(pl.program_id(0) % tiles_per_seq == 0, W, 0)
    sj = _iota((2 * W, W), 0)
    qi = _iota((2 * W, W), 1)
    in_window = (sj > qi) & (sj <= qi + W)
    kv0 = SWA_HEADS * HD
    no_head = jnp.zeros((HD, W), BF16)
    n_blocks = tt // W

    def band(c, col0):
        prev = prev_ref[:, col0:col0 + 128] if c == 0 else cur_ref[(c - 1) * W:c * W, col0:col0 + 128]
        return jnp.concatenate([prev, cur_ref[c * W:(c + 1) * W, col0:col0 + 128]], axis=0)

    for c in range(n_blocks):
        mask = in_window & (sj >= first_key) if c == 0 else in_window
        k_band = band(c, kv0)
        q_t = cur_ref[c * W:(c + 1) * W, 0:kv0].astype(F32).T.astype(BF16)
        for hh in range(SWA_HEADS):
            kk = hh // (SWA_HEADS // SWA_KV_HEADS)
            qh_t = q_t[hh * HD:(hh + 1) * HD]
            q_on_kv = jnp.concatenate([qh_t, no_head] if kk == 0 else [no_head, qh_t], axis=0)
            s = _dot(k_band, q_on_kv) * (HD ** -0.5 * LOG2_E)
            s_ref[c, hh] = jnp.where(mask, s, NEG_INF)
    for c in range(n_blocks):
        v_t = band(c, kv0 + 128).astype(F32).T.astype(BF16)
        outs = []
        for hh in range(SWA_HEADS):
            kk = hh // (SWA_HEADS // SWA_KV_HEADS)
            s = s_ref[c, hh]
            sink = sink_ref[hh] * LOG2_E
            m = jnp.maximum(jnp.max(s, axis=0, keepdims=True), sink)
            p = jnp.exp2(s - m)
            denom = jnp.sum(p, axis=0, keepdims=True) + jnp.exp2(sink - m)
            outs.append(_dot(v_t[kk * HD:(kk + 1) * HD], p.astype(BF16)) / denom)
        o_ref[c * W:(c + 1) * W, :] = jnp.concatenate(outs, axis=0).T.astype(BF16)


def _swa(z, sinks, seq):
    T = z.shape[0]
    tt = TOK_TILE
    per = tt // SWA_WINDOW
    return pl.pallas_call(
        functools.partial(_swa_kernel, tiles_per_seq=seq // tt),
        grid=(T // tt,),
        in_specs=[pl.BlockSpec(memory_space=pltpu.SMEM),
                  pl.BlockSpec((tt, SWA_W), lambda i: (i, 0)),
                  pl.BlockSpec((SWA_WINDOW, SWA_W), lambda i: (jnp.maximum(i * per - 1, 0), 0))],
        out_specs=pl.BlockSpec((tt, 256), lambda i: (i, 0)),
        out_shape=jax.ShapeDtypeStruct((T, 256), BF16),
        scratch_shapes=[pltpu.VMEM((per, SWA_HEADS, 2 * SWA_WINDOW, SWA_WINDOW), F32)],
        compiler_params=_params("parallel"),
        name="swa_mixer",
    )(sinks, z, z)


_KMEAN_BLOCKS = 8


def _kmean_kernel(k_ref, o_ref):
    k = k_ref[...].astype(F32).reshape(_KMEAN_BLOCKS, MOBA_BLOCK, 256)
    o_ref[...] = jnp.mean(k, axis=1)


def _moba_kmean(z):
    T = z.shape[0]
    rows = _KMEAN_BLOCKS * MOBA_BLOCK
    return pl.pallas_call(
        _kmean_kernel,
        grid=(T // rows,),
        in_specs=[pl.BlockSpec((rows, 256), lambda i: (i, 1))],
        out_specs=pl.BlockSpec((_KMEAN_BLOCKS, 256), lambda i: (i, 0)),
        out_shape=jax.ShapeDtypeStruct((T // MOBA_BLOCK, 256), F32),
        compiler_params=_params("parallel"),
        name="moba_kmean",
    )(z)


MOBA_VROWS = 80
MOBA_PREP_BLOCKS = 2


def _moba_prep_kernel(z_ref, km_ref, qt_out, k_out, vt_out):
    HD = MOBA_HD
    tq = MOBA_BLOCK
    nb = km_ref.shape[0]
    blk_i = _iota((nb, tq), 0)
    blk = blk_i.astype(F32)
    pad = LANES - HD - nb
    ones_row = (_iota((MOBA_VROWS - HD, tq), 0) == 0).astype(BF16)
    for sb in range(MOBA_PREP_BLOCKS):
        rows = slice(sb * tq, (sb + 1) * tq)
        qb = (pl.program_id(0) * MOBA_PREP_BLOCKS + sb) % nb
        this_block_lanes = (_iota((tq, nb), 1) == qb).astype(BF16)
        for hh in range(MOBA_HEADS):
            cols = slice(hh * HD, (hh + 1) * HD)
            q_t = z_ref[rows, cols].astype(F32).T
            gate = _dot(km_ref[:, cols], q_t, precision=HIGHEST)
            gate = jnp.where(blk_i < qb, gate, NEG_INF)
            keep = jnp.where(blk_i == qb, 1.0, 0.0)
            for _ in range(MOBA_TOPK):
                best = jnp.max(gate, axis=0, keepdims=True)
                first = jnp.min(jnp.where(gate == best, blk, float(nb)), axis=0, keepdims=True)
                hit = blk == first
                keep = keep + jnp.where(hit, jnp.where(best > 0.5 * NEG_INF, 1.0, 0.0), 0.0)
                gate = jnp.where(hit, BELOW_NEG_INF, gate)
            bias = ((keep - 1.0) * (-NEG_INF)).astype(BF16)
            qt_out[hh, sb] = jnp.concatenate(
                [(q_t * (HD ** -0.5 * LOG2_E)).astype(BF16), bias, jnp.zeros((pad, tq), BF16)], axis=0)
            k_out[hh, rows, :] = jnp.concatenate(
                [z_ref[rows, 256 + hh * HD:256 + (hh + 1) * HD], this_block_lanes, jnp.zeros((tq, pad), BF16)],
                axis=1)
            v_t = z_ref[rows, 512 + hh * HD:512 + (hh + 1) * HD].astype(F32).T.astype(BF16)
            vt_out[hh, sb] = jnp.concatenate([v_t, ones_row], axis=0)


def _moba_prep(z, kmean, seq):
    T = z.shape[0]
    nb = seq // MOBA_BLOCK
    assert MOBA_HD + nb <= LANES and nb % MOBA_PREP_BLOCKS == 0
    tq = MOBA_BLOCK
    tile = MOBA_PREP_BLOCKS * tq
    H = MOBA_HEADS

    def per_block_t(rows):
        return pl.BlockSpec((H, MOBA_PREP_BLOCKS, rows, tq), lambda i: (0, i, 0, 0))

    return pl.pallas_call(
        _moba_prep_kernel,
        grid=(T // tile,),
        in_specs=[pl.BlockSpec((tile, MOBA_W), lambda i: (i, 0)),
                  pl.BlockSpec((nb, 256), lambda i: (i * MOBA_PREP_BLOCKS // nb, 0))],
        out_specs=[per_block_t(LANES), pl.BlockSpec((H, tile, LANES), lambda i: (0, i, 0)),
                   per_block_t(MOBA_VROWS)],
        out_shape=[jax.ShapeDtypeStruct((H, T // tq, LANES, tq), BF16), jax.ShapeDtypeStruct((H, T, LANES), BF16),
                   jax.ShapeDtypeStruct((H, T // tq, MOBA_VROWS, tq), BF16)],
        compiler_params=_params("parallel"),
        name="moba_prep",
    )(z, kmean)


MOBA_QBLOCKS = 4
MOBA_KBLOCKS = 4
MOBA_DIAGONALS = MOBA_QBLOCKS // MOBA_KBLOCKS


def _moba_kernel(qidx_ref, kidx_ref, kind_ref, qt_ref, k_ref, vt_ref, o_ref, m_ref, acc_ref, s_ref, smax_ref):
    HD = MOBA_HD
    B = MOBA_BLOCK
    step = pl.program_id(1)
    kind = kind_ref[step]
    is_last = (step + 1 == pl.num_programs(1)) | (qidx_ref[jnp.minimum(step + 1, pl.num_programs(1) - 1)]
                                                  != qidx_ref[step])

    def attend(diagonal):
        causal = _iota((B, B), 0) <= _iota((B, B), 1)

        def key_blocks_of(qi):
            if diagonal is None:
                return list(range(MOBA_KBLOCKS))
            return [c for c in range(MOBA_KBLOCKS) if MOBA_KBLOCKS * diagonal + c <= qi]

        for hh in range(MOBA_HEADS):
            for qi in range(MOBA_QBLOCKS):
                for c in key_blocks_of(qi):
                    s = _dot(k_ref[hh, c * B:(c + 1) * B, :], qt_ref[hh, qi])
                    if diagonal is not None and MOBA_KBLOCKS * diagonal + c == qi:
                        s = jnp.where(causal, s, NEG_INF)
                    s_ref[hh, qi, c] = s
                    smax_ref[hh, qi, c] = jnp.max(s, axis=0, keepdims=True)
        first_visit = diagonal == 0
        for hh in range(MOBA_HEADS):
            for qi in range(MOBA_QBLOCKS):
                key_blocks = key_blocks_of(qi)
                if not key_blocks:
                    continue
                m_new = functools.reduce(jnp.maximum, [smax_ref[hh, qi, c] for c in key_blocks])
                if not first_visit:
                    m_old = m_ref[hh, qi]
                    m_new = jnp.maximum(m_old, m_new)
                pv = None
                for c in key_blocks:
                    p = jnp.exp2(s_ref[hh, qi, c] - m_new).astype(BF16)
                    term = _dot(vt_ref[hh, c], p)
                    pv = term if pv is None else pv + term
                acc_ref[hh, qi] = pv if first_visit else jnp.exp2(m_old - m_new) * acc_ref[hh, qi] + pv
                m_ref[hh, qi] = m_new

    @pl.when(kind == 0)
    def _():
        attend(None)

    for d in range(MOBA_DIAGONALS):
        @pl.when(kind == 1 + d)
        def _(d=d):
            attend(d)

    @pl.when(is_last)
    def _():
        pad = jnp.zeros((LANES - MOBA_VROWS, B), F32)
        for qi in range(MOBA_QBLOCKS):
            outs = []
            for hh in range(MOBA_HEADS):
                a = jnp.concatenate([acc_ref[hh, qi], pad], axis=0).T
                outs.append(a[:, 0:HD] / a[:, HD:HD + 1])
            o_ref[qi * B:(qi + 1) * B, :] = jnp.concatenate(outs, axis=1).astype(BF16)


def _moba(z, seq):
    T = z.shape[0]
    nb = seq // MOBA_BLOCK
    assert nb % MOBA_QBLOCKS == 0
    q_tiles = nb // MOBA_QBLOCKS
    k_steps = nb // MOBA_KBLOCKS
    D = MOBA_DIAGONALS
    batch = T // seq
    qt_aug, k_aug, vt_aug = _moba_prep(z, _moba_kmean(z), seq)
    qidx = np.concatenate([np.full(D * t + D, t) for t in range(q_tiles)]).astype(np.int32)
    kidx = np.concatenate([np.concatenate([D * t + np.arange(D), np.arange(D * t)])
                           for t in range(q_tiles)]).astype(np.int32)
    kind = np.concatenate([np.concatenate([1 + np.arange(D), np.zeros(D * t, np.int64)])
                           for t in range(q_tiles)]).astype(np.int32)
    tq = MOBA_QBLOCKS * MOBA_BLOCK
    tk = MOBA_KBLOCKS * MOBA_BLOCK
    H = MOBA_HEADS
    grid_spec = pltpu.PrefetchScalarGridSpec(
        num_scalar_prefetch=3,
        grid=(batch, len(qidx)),
        in_specs=[pl.BlockSpec((H, MOBA_QBLOCKS, LANES, MOBA_BLOCK),
                               lambda b, s, qi, ki, kd: (0, b * q_tiles + qi[s], 0, 0)),
                  pl.BlockSpec((H, tk, LANES), lambda b, s, qi, ki, kd: (0, b * k_steps + ki[s], 0)),
                  pl.BlockSpec((H, MOBA_KBLOCKS, MOBA_VROWS, MOBA_BLOCK),
                               lambda b, s, qi, ki, kd: (0, b * k_steps + ki[s], 0, 0))],
        out_specs=pl.BlockSpec((tq, 256), lambda b, s, qi, ki, kd: (b * q_tiles + qi[s], 0)),
        scratch_shapes=[pltpu.VMEM((H, MOBA_QBLOCKS, 1, MOBA_BLOCK), F32),
                        pltpu.VMEM((H, MOBA_QBLOCKS, MOBA_VROWS, MOBA_BLOCK), F32),
                        pltpu.VMEM((H, MOBA_QBLOCKS, MOBA_KBLOCKS, MOBA_BLOCK, MOBA_BLOCK), F32),
                        pltpu.VMEM((H, MOBA_QBLOCKS, MOBA_KBLOCKS, 1, MOBA_BLOCK), F32)],
    )
    return pl.pallas_call(
        _moba_kernel,
        grid_spec=grid_spec,
        out_shape=jax.ShapeDtypeStruct((T, 256), BF16),
        compiler_params=_params("parallel", "arbitrary"),
        name="moba_mixer",
    )(jnp.asarray(qidx), jnp.asarray(kidx), jnp.asarray(kind), qt_aug, k_aug, vt_aug)


def _merge_kernel(h_ref, o0, o1, o2, o3, half_gates_ref, half_wb_ref, wout_ref, out_ref):
    merged = None
    for i, o in enumerate((o0, o1, o2, o3)):
        half_y = _dot(o[...], half_wb_ref[i])
        term = half_y * jnp.tanh(half_gates_ref[:, i * D_MODEL:(i + 1) * D_MODEL].astype(F32)) + half_y
        merged = term if merged is None else merged + term
    out_ref[...] = h_ref[...] + _dot(merged.astype(BF16), wout_ref[...])


def _merge(h, branches, gates, wb, wout):
    T = h.shape[0]
    tm = TOK_TILE
    return pl.pallas_call(
        _merge_kernel,
        grid=(T // tm,),
        in_specs=[pl.BlockSpec((tm, D_MODEL), lambda i: (i, 0))]
        + [pl.BlockSpec((tm, 256), lambda i: (i, 0))] * 4
        + [pl.BlockSpec((tm, GATE_W), lambda i: (i, 0)),
           pl.BlockSpec((N_BRANCH, 256, D_MODEL), lambda i: (0, 0, 0), pipeline_mode=pl.Buffered(1)),
           pl.BlockSpec((D_MODEL, D_MODEL), lambda i: (0, 0), pipeline_mode=pl.Buffered(1))],
        out_specs=pl.BlockSpec((tm, D_MODEL), lambda i: (i, 0)),
        out_shape=jax.ShapeDtypeStruct((T, D_MODEL), F32),
        compiler_params=_params("parallel"),
        name="merge_out_proj",
    )(h, *branches, gates, wb, wout)


ROW_HALF = D_MODEL // 2
ROW_TILES = ROW_HALF // LANES
HIGH_HALF_MASK = -65536


def _row_tile_spec(rows, index_map):
    return pl.BlockSpec((rows * ROW_TILES, LANES), index_map)


def _bf16_bits(x):
    return lax.bitcast_convert_type(x.astype(BF16).astype(F32), I32)


def _pack_rows(ref, x):
    for j in range(ROW_TILES):
        low = _bf16_bits(x[:, j * LANES:(j + 1) * LANES])
        high = _bf16_bits(x[:, ROW_HALF + j * LANES:ROW_HALF + (j + 1) * LANES])
        ref[pl.ds(j, x.shape[0], stride=ROW_TILES), :] = high | lax.shift_right_logical(low, 16)


def _unpack_slab(ref, j):
    words = ref[pl.ds(j, ref.shape[0] // ROW_TILES, stride=ROW_TILES), :]
    low = lax.bitcast_convert_type(lax.shift_left(words, 16), F32)
    high = lax.bitcast_convert_type(words & HIGH_HALF_MASK, F32)
    return low, high


def _unpack_rows(ref):
    halves = [_unpack_slab(ref, j) for j in range(ROW_TILES)]
    return jnp.concatenate([lo for lo, _ in halves] + [hi for _, hi in halves], axis=1).astype(BF16)


def _router_kernel(h_ref, g_ref, wr_ref, br_ref, u_ref, route_ref, route_t_ref, count_ref, carry_ref):
    tm = h_ref.shape[0]

    @pl.when(pl.program_id(0) == 0)
    def _():
        carry_ref[...] = jnp.zeros_like(carry_ref)

    u = _rms(h_ref[...], g_ref[...])
    _pack_rows(u_ref, u)
    u_hi, u_lo = _split_bf16(u, 2)
    w_hi, w_lo = _split_bf16(wr_ref[...], 2)
    logits = _dot(u_hi, w_hi) + (_dot(u_lo, w_hi) + _dot(u_hi, w_lo)) + br_ref[...]
    lt = logits.T
    row_i = _iota((LANES, tm), 0)
    row = row_i.astype(F32)
    row_group = ((row_i - EXPERT_LANE0) >> 3).astype(F32)

    def first_row(mask):
        return jnp.min(jnp.where(mask, row, float(LANES)), axis=0, keepdims=True)

    grp = jnp.where(row_i < N_GROUPS, lt, BELOW_NEG_INF)
    grp_e = jnp.exp(grp - jnp.max(grp, axis=0, keepdims=True))
    p_group = grp_e / jnp.sum(grp_e, axis=0, keepdims=True)
    g_w = jnp.max(p_group, axis=0, keepdims=True)
    g_sel = first_row(p_group == g_w)

    in_group = (row_i >= EXPERT_LANE0) & (row_group == g_sel)
    el = jnp.where(in_group, lt, BELOW_NEG_INF)
    top1 = jnp.max(el, axis=0, keepdims=True)
    row1 = first_row(el == top1)
    el = jnp.where(row == row1, BELOW_NEG_INF, el)
    top2 = jnp.max(el, axis=0, keepdims=True)
    row2 = first_row(el == top2)
    e2 = jnp.exp(top2 - top1)
    w1 = g_w / (1.0 + e2)
    w2 = g_w * e2 / (1.0 + e2)

    onehot = ((row == row1) | (row == row2)).astype(BF16)
    earlier = (_iota((tm, tm), 0) < _iota((tm, tm), 1)).astype(BF16)
    seen = _dot(onehot, earlier) + carry_ref[:, 0:1]
    rank1 = jnp.sum(jnp.where(row == row1, seen, 0.0), axis=0, keepdims=True)
    rank2 = jnp.sum(jnp.where(row == row2, seen, 0.0), axis=0, keepdims=True)
    carry_ref[...] += jnp.sum(onehot.astype(F32), axis=1, keepdims=True)
    count_ref[...] = carry_ref[...]

    out_t = jnp.zeros((LANES, tm), F32)
    for idx, val in enumerate((row1 - EXPERT_LANE0, row2 - EXPERT_LANE0, w1, w2, rank1, rank2)):
        out_t = jnp.where(row_i == idx, val, out_t)
    route_t_ref[...] = out_t[0:8, :]
    route_ref[...] = out_t.T


def _router(h, g, wr, br):
    T = h.shape[0]
    tm = TOK_TILE
    return pl.pallas_call(
        _router_kernel,
        grid=(T // tm,),
        in_specs=[pl.BlockSpec((tm, D_MODEL), lambda i: (i, 0)), _full((1, D_MODEL)),
                  _full((D_MODEL, LANES)), _full((1, LANES))],
        out_specs=[_row_tile_spec(tm, lambda i: (i, 0)), pl.BlockSpec((tm, LANES), lambda i: (i, 0)),
                   pl.BlockSpec((8, tm), lambda i: (0, i)), _full((LANES, LANES))],
        out_shape=[jax.ShapeDtypeStruct((T * ROW_TILES, LANES), I32), jax.ShapeDtypeStruct((T, LANES), F32),
                   jax.ShapeDtypeStruct((8, T), F32), jax.ShapeDtypeStruct((LANES, LANES), F32)],
        scratch_shapes=[pltpu.VMEM((LANES, LANES), F32)],
        compiler_params=_params("arbitrary"),
        name="moe_router",
    )(h, g, wr, br)


ROW_UNROLL = 8


def _row_copy(src_ref, src_row, dst_ref, dst_row, sem):
    src = src_ref.at[pl.ds(pl.multiple_of(src_row * ROW_TILES, ROW_TILES), ROW_TILES)]
    dst = dst_ref.at[pl.ds(pl.multiple_of(dst_row * ROW_TILES, ROW_TILES), ROW_TILES)]
    return pltpu.make_async_copy(src, dst, sem)


def _for_each_row_pair(rows, body):
    def group(g, carry):
        for i in range(ROW_UNROLL):
            for k in range(EXPERT_TOPK):
                body(g * ROW_UNROLL + i, k)
        return carry

    lax.fori_loop(0, rows // ROW_UNROLL, group, 0)


def _dispatch_kernel(dest0_ref, dest1_ref, pstart_ref, pend_ref, u_ref, xs_ref, zero_ref, sem):
    tm = u_ref.shape[0] // ROW_TILES
    tile = pl.program_id(0)
    dests = (dest0_ref, dest1_ref)

    @pl.when(tile == 0)
    def _():
        zero_ref[...] = jnp.zeros_like(zero_ref)

        def zero_block(first_row):
            start = pl.multiple_of(first_row * ROW_TILES, ROW_TILES)
            fill = pltpu.make_async_copy(zero_ref, xs_ref.at[pl.ds(start, MOE_BLOCK * ROW_TILES)], sem)
            fill.start()
            fill.wait()

        def zero_tail(e, carry):
            @pl.when(pend_ref[e] > pstart_ref[e])
            def _():
                zero_block(pend_ref[e] - MOE_BLOCK)
            return carry

        def zero_unused(b, carry):
            zero_block(b * MOE_BLOCK)
            return carry

        lax.fori_loop(0, N_EXPERTS, zero_tail, 0)
        lax.fori_loop(pend_ref[N_EXPERTS - 1] // MOE_BLOCK, xs_ref.shape[0] // (MOE_BLOCK * ROW_TILES),
                      zero_unused, 0)

    def issue(r, k):
        _row_copy(u_ref, r, xs_ref, dests[k][tile * tm + r], sem).start(priority=k)

    _for_each_row_pair(tm, issue)
    for _ in range(EXPERT_TOPK):
        pltpu.make_async_copy(u_ref, xs_ref.at[pl.ds(0, tm * ROW_TILES)], sem).wait()


def _dispatch(dest0, dest1, pstart, pend, u, n_slots):
    T = u.shape[0] // ROW_TILES
    tm = DISPATCH_TILE
    grid_spec = pltpu.PrefetchScalarGridSpec(
        num_scalar_prefetch=4,
        grid=(T // tm,),
        in_specs=[_row_tile_spec(tm, lambda i, *_: (i, 0))],
        out_specs=pl.BlockSpec(memory_space=pl.ANY),
        scratch_shapes=[pltpu.VMEM((MOE_BLOCK * ROW_TILES, LANES), I32), pltpu.SemaphoreType.DMA(())],
    )
    return pl.pallas_call(
        _dispatch_kernel,
        grid_spec=grid_spec,
        out_shape=jax.ShapeDtypeStruct((n_slots * ROW_TILES, LANES), I32),
        compiler_params=_params("arbitrary"),
        name="moe_dispatch",
    )(dest0, dest1, pstart, pend, u)


def _expert_kernel(blk_e_ref, n_used_ref, x_ref, w1_ref, w3_ref, w2_ref, y_ref, w1b_ref, w3b_ref, w2b_ref):
    i = pl.program_id(0)
    used = i < n_used_ref[0]

    @pl.when((i == 0) | (blk_e_ref[i] != blk_e_ref[jnp.maximum(i - 1, 0)]))
    def _():
        w1b_ref[...] = w1_ref[0, 0].astype(BF16)
        w3b_ref[...] = w3_ref[0, 0].astype(BF16)
        w2b_ref[...] = w2_ref[0, 0].astype(BF16)

    @pl.when(used)
    def _():
        x = _unpack_rows(x_ref)
        a = _dot(x, w1b_ref[...])
        hidden = (a * _sigmoid(a)) * _dot(x, w3b_ref[...])
        _pack_rows(y_ref, _dot(hidden.astype(BF16), w2b_ref[...]))

    @pl.when(jnp.logical_not(used))
    def _():
        y_ref[...] = jnp.zeros_like(y_ref)


def _experts(blk_e, n_used, xs, w1, w3, w2, layer):
    n_slots = xs.shape[0] // ROW_TILES
    bm = MOE_BLOCK
    grid_spec = pltpu.PrefetchScalarGridSpec(
        num_scalar_prefetch=2,
        grid=(n_slots // bm,),
        in_specs=[_row_tile_spec(bm, lambda i, be, nu: (jnp.minimum(i, nu[0] - 1), 0)),
                  pl.BlockSpec((1, 1, D_MODEL, EXPERT_HIDDEN), lambda i, be, nu: (layer, be[i], 0, 0)),
                  pl.BlockSpec((1, 1, D_MODEL, EXPERT_HIDDEN), lambda i, be, nu: (layer, be[i], 0, 0)),
                  pl.BlockSpec((1, 1, EXPERT_HIDDEN, D_MODEL), lambda i, be, nu: (layer, be[i], 0, 0))],
        out_specs=_row_tile_spec(bm, lambda i, be, nu: (i, 0)),
        scratch_shapes=[pltpu.VMEM((D_MODEL, EXPERT_HIDDEN), BF16), pltpu.VMEM((D_MODEL, EXPERT_HIDDEN), BF16),
                        pltpu.VMEM((EXPERT_HIDDEN, D_MODEL), BF16)],
    )
    return pl.pallas_call(
        _expert_kernel,
        grid_spec=grid_spec,
        out_shape=jax.ShapeDtypeStruct((n_slots * ROW_TILES, LANES), I32),
        compiler_params=_params("arbitrary"),
        name="moe_experts",
    )(blk_e, n_used, xs, w1, w3, w2)


def _combine_kernel(dest0_ref, dest1_ref, h_ref, route_ref, g_ref, ys_ref, out_ref, y_ref, sem, *, normalize):
    tm = h_ref.shape[0]
    step = pl.program_id(0)
    slot = step % 2

    dests = (dest0_ref, dest1_ref)

    def gather_tile(tile, buf):
        def issue(r, k):
            _row_copy(ys_ref, dests[k][tile * tm + r], y_ref.at[buf, k], r, sem.at[buf]).start(priority=k)

        _for_each_row_pair(tm, issue)

    @pl.when(step == 0)
    def _():
        gather_tile(0, 0)

    @pl.when(step + 1 < pl.num_programs(0))
    def _():
        gather_tile(step + 1, 1 - slot)

    for k in range(EXPERT_TOPK):
        pltpu.make_async_copy(ys_ref.at[pl.ds(0, tm * ROW_TILES)], y_ref.at[slot, k], sem.at[slot]).wait()
    w0 = route_ref[:, 2:3]
    w1 = route_ref[:, 3:4]
    for j in range(ROW_TILES):
        low0, high0 = _unpack_slab(y_ref.at[slot, 0], j)
        low1, high1 = _unpack_slab(y_ref.at[slot, 1], j)
        cols = slice(j * LANES, (j + 1) * LANES)
        out_ref[:, cols] = h_ref[:, cols] + w0 * low0 + w1 * low1
        cols = slice(ROW_HALF + j * LANES, ROW_HALF + (j + 1) * LANES)
        out_ref[:, cols] = h_ref[:, cols] + w0 * high0 + w1 * high1
    if normalize:
        out_ref[...] = _rms(out_ref[...], g_ref[...])


def _combine(dest0, dest1, h, route, ys, final_g, normalize):
    T = h.shape[0]
    tm = ROW_TILE
    grid_spec = pltpu.PrefetchScalarGridSpec(
        num_scalar_prefetch=2,
        grid=(T // tm,),
        in_specs=[pl.BlockSpec((tm, D_MODEL), lambda i, *_: (i, 0)),
                  pl.BlockSpec((tm, LANES), lambda i, *_: (i, 0)),
                  _full((1, D_MODEL)),
                  pl.BlockSpec(memory_space=pl.ANY)],
        out_specs=pl.BlockSpec((tm, D_MODEL), lambda i, *_: (i, 0)),
        scratch_shapes=[pltpu.VMEM((2, EXPERT_TOPK, tm * ROW_TILES, LANES), I32),
                        pltpu.SemaphoreType.DMA((2,))],
    )
    return pl.pallas_call(
        functools.partial(_combine_kernel, normalize=normalize),
        grid_spec=grid_spec,
        out_shape=jax.ShapeDtypeStruct((T, D_MODEL), F32),
        compiler_params=_params("arbitrary"),
        name="moe_combine",
    )(dest0, dest1, h, route, final_g, ys)


def _moe(h, g, w_group, b_group, w_expert, b_expert, w1, w3, w2, layer, final_g, normalize):
    T = h.shape[0]
    wr = jnp.zeros((D_MODEL, LANES), F32)
    wr = wr.at[:, :N_GROUPS].set(w_group).at[:, EXPERT_LANE0:EXPERT_LANE0 + N_EXPERTS].set(w_expert)
    br = jnp.zeros((1, LANES), F32)
    br = br.at[0, :N_GROUPS].set(b_group).at[0, EXPERT_LANE0:EXPERT_LANE0 + N_EXPERTS].set(b_expert)
    u, route, route_t, count = _router(h, g, wr, br)

    n_blocks = (T * EXPERT_TOPK) // MOE_BLOCK + N_EXPERTS
    counts = count[EXPERT_LANE0:EXPERT_LANE0 + N_EXPERTS, 0].astype(I32)
    padded = ((counts + MOE_BLOCK - 1) // MOE_BLOCK) * MOE_BLOCK
    pend = jnp.cumsum(padded)
    pstart = pend - padded
    expert_ids = jnp.arange(N_EXPERTS, dtype=F32)[:, None]

    def slot_of(expert_row, rank_row):
        segment_start = jnp.sum(jnp.where(expert_row[None, :] == expert_ids, pstart[:, None], 0), axis=0)
        return segment_start + rank_row.astype(I32)

    dest0 = slot_of(route_t[0], route_t[4])
    dest1 = slot_of(route_t[1], route_t[5])
    block_row0 = jnp.arange(n_blocks, dtype=I32) * MOE_BLOCK
    blk_e = jnp.minimum(jnp.sum((pend[None, :] <= block_row0[:, None]).astype(I32), axis=1), N_EXPERTS - 1)
    n_used = (pend[-1:] // MOE_BLOCK).astype(I32)

    xs = _dispatch(dest0, dest1, pstart, pend, u, n_blocks * MOE_BLOCK)
    ys = _experts(blk_e, n_used, xs, w1, w3, w2, layer)
    return _combine(dest0, dest1, h, route, ys, final_g, normalize)


def _split_w_in(w_in):
    gla_n = GLA_HEADS * (2 * GLA_DK + 2 * GLA_DV) + GLA_LOWRANK
    swa_n = (SWA_HEADS + 2 * SWA_KV_HEADS) * SWA_HD
    moba_n = 3 * MOBA_HEADS * MOBA_HD
    ret_n = RET_HEADS * (2 * RET_DK + 2 * RET_DV)
    offs = np.cumsum([0, gla_n, swa_n, moba_n, ret_n, GATE_W])
    parts = [w_in[:, offs[i]:offs[i + 1]].astype(BF16) for i in range(5)]
    parts[0] = jnp.pad(parts[0], ((0, 0), (0, GLA_W - gla_n)))
    parts[4] = parts[4] * 0.5
    return parts


def _token_mixer(h, seq, ln1_g, w_in, gla_w_a2, gla_b_a, gla_norm_g, swa_sinks, ret_norm_g, w_branch, w_out,
                 rope):
    z_gla, z_swa, z_moba, z_ret, half_gates = _in_proj(h, ln1_g.reshape(1, D_MODEL), _split_w_in(w_in))
    wa = jnp.zeros((128, 128), F32).at[:GLA_LOWRANK].set(gla_w_a2)
    o_gla = _gla(z_gla, wa, gla_b_a.reshape(1, 128), jnp.tile(gla_norm_g, GLA_HEADS).reshape(1, 256), seq)
    o_swa = _swa(z_swa, swa_sinks, seq)
    o_moba = _moba(z_moba, seq)
    o_ret = _retention(z_ret, *rope, jnp.tile(ret_norm_g, RET_HEADS).reshape(1, 256), seq)
    half_w_branch = w_branch.astype(BF16) * 0.5
    return _merge(h, (o_gla, o_swa, o_moba, o_ret), half_gates, half_w_branch, w_out.astype(BF16))


def kernel(x, ln1_g, w_in, gla_w_a2, gla_b_a, gla_norm_g, swa_sinks, ret_norm_g, w_branch, w_out, ln2_g,
           w_group, b_group, w_expert, b_expert, w1, w3, w2, final_g):
    batch, seq, _ = x.shape
    depth = w_in.shape[0]
    assert depth >= 1
    assert x.shape[2] == D_MODEL and seq % (MOBA_QBLOCKS * MOBA_BLOCK) == 0 and seq % TOK_TILE == 0
    assert (batch * seq) % DISPATCH_TILE == 0 and (batch * seq * EXPERT_TOPK) % MOE_BLOCK == 0
    rope = _rope_tables(seq)
    h = x.reshape(batch * seq, D_MODEL)
    for l in range(depth):
        h = _token_mixer(h, seq, ln1_g[l], w_in[l], gla_w_a2[l], gla_b_a[l], gla_norm_g[l], swa_sinks[l],
                         ret_norm_g[l], w_branch[l], w_out[l], rope)
        h = _moe(h, ln2_g[l].reshape(1, D_MODEL), w_group[l], b_group[l], w_expert[l], b_expert[l],
                 w1, w3, w2, l, final_g.reshape(1, D_MODEL), normalize=(l == depth - 1))
    return h.reshape(batch, seq, D_MODEL)
```

```python
import functools
import math

import numpy as np
import jax
import jax.numpy as jnp
from jax import lax
from jax.experimental import pallas as pl
from jax.experimental.pallas import tpu as pltpu

F32 = jnp.float32
BF16 = jnp.bfloat16
I32 = jnp.int32
HIGHEST = lax.Precision.HIGHEST

D_MODEL = 1024
N_BRANCH = 4
NORM_EPS = 1e-6
NEG_INF = -1e30
BELOW_NEG_INF = -3e38
LOG2_E = math.log2(math.e)

GLA_HEADS, GLA_DK, GLA_DV, GLA_LOWRANK, GLA_TAU, GLA_CHUNK = 4, 32, 64, 16, 16.0, 64
SWA_HEADS, SWA_KV_HEADS, SWA_HD, SWA_WINDOW = 4, 2, 64, 128
MOBA_HEADS, MOBA_HD, MOBA_BLOCK, MOBA_TOPK = 4, 64, 256, 3
RET_HEADS, RET_DK, RET_DV, RET_CHUNK, RET_ROPE_BASE = 4, 32, 64, 128, 10000.0
N_GROUPS, EXPERTS_PER_GROUP, EXPERT_TOPK, EXPERT_HIDDEN = 4, 8, 2, 256
MOE_BLOCK = 512
N_EXPERTS = N_GROUPS * EXPERTS_PER_GROUP

GLA_W = 896
SWA_W = 512
MOBA_W = 768
RET_W = 768
GATE_W = N_BRANCH * D_MODEL

LANES = 128
EXPERT_LANE0 = 32

VMEM_LIMIT = 56 * 1024 * 1024

TOK_TILE = 512
ROW_TILE = 1024
DISPATCH_TILE = 2048


def _params(*sem):
    return pltpu.CompilerParams(dimension_semantics=sem, vmem_limit_bytes=VMEM_LIMIT)


def _dot(a, b, precision=None):
    return jnp.dot(a, b, preferred_element_type=F32, precision=precision)


def _dot_nt(a, b, precision=None):
    return lax.dot_general(a, b, (((1,), (1,)), ((), ())), preferred_element_type=F32, precision=precision)


def _dot_tn(a, b, precision=None):
    return lax.dot_general(a, b, (((0,), (0,)), ((), ())), preferred_element_type=F32, precision=precision)


def _split_bf16(x, parts):
    out = []
    for _ in range(parts - 1):
        piece = x.astype(BF16)
        out.append(piece)
        x = x - piece.astype(F32)
    out.append(x.astype(BF16))
    return out


def _dot_f32_by_bf16(x, m):
    return sum(_dot(piece, m) for piece in _split_bf16(x, 3))


def _dot_bf16_by_f32(m, x):
    return sum(_dot(m, piece) for piece in _split_bf16(x, 3))


def _iota(shape, dim):
    return lax.broadcasted_iota(I32, shape, dim)


def _sigmoid(x):
    return 0.5 * jnp.tanh(0.5 * x) + 0.5


def _rms(x, g):
    ms = jnp.mean(x * x, axis=-1, keepdims=True)
    return x * lax.rsqrt(ms + NORM_EPS) * g


def _full(shape):
    return pl.BlockSpec(shape, lambda *_: (0,) * len(shape))


def _in_proj_kernel(h_ref, g_ref, w0, w1, w2, w3, w4, o0, o1, o2, o3, o4):
    u = _rms(h_ref[...], g_ref[...]).astype(BF16)
    for w, o in ((w0, o0), (w1, o1), (w2, o2), (w3, o3), (w4, o4)):
        o[...] = _dot(u, w[...]).astype(BF16)


def _in_proj(h, g, ws):
    T = h.shape[0]
    tm = TOK_TILE
    widths = [w.shape[1] for w in ws]
    return pl.pallas_call(
        _in_proj_kernel,
        grid=(T // tm,),
        in_specs=[pl.BlockSpec((tm, D_MODEL), lambda i: (i, 0)), _full((1, D_MODEL))]
        + [pl.BlockSpec((D_MODEL, n), lambda i: (0, 0), pipeline_mode=pl.Buffered(1)) for n in widths],
        out_specs=[pl.BlockSpec((tm, n), lambda i: (i, 0)) for n in widths],
        out_shape=[jax.ShapeDtypeStruct((T, n), BF16) for n in widths],
        compiler_params=_params("parallel"),
        name="in_proj",
    )(h, g, *ws)


def _head_select(x, lane_head, n_heads, rows):
    out = None
    for hh in range(n_heads):
        term = jnp.where(lane_head == hh, x[hh * rows:(hh + 1) * rows], 0.0)
        out = term if out is None else out + term
    return out


def _head_norm_gate(o, gate_in, g):
    n = o.shape[1]
    same_head = (_iota((n, n), 0) >> 6) == (_iota((n, n), 1) >> 6)
    ms = _dot_f32_by_bf16(o * o, same_head.astype(BF16)) * (1.0 / 64.0)
    return o * lax.rsqrt(ms + NORM_EPS) * g * (gate_in * _sigmoid(gate_in))


def _gla_kernel(z_ref, wa_ref, ba_ref, g_ref, o_ref, st_ref, oacc_ref, att_ref, kv_ref, *, tiles_per_seq):
    C = GLA_CHUNK
    tt = z_ref.shape[0]

    @pl.when(pl.program_id(0) % tiles_per_seq == 0)
    def _():
        st_ref[...] = jnp.zeros_like(st_ref)

    q = z_ref[:, 0:128].astype(F32)
    k = z_ref[:, 128:256].astype(F32)
    r = z_ref[:, 512:768].astype(F32)
    pre = _dot_bf16_by_f32(z_ref[:, 768:896], wa_ref[...]) + ba_ref[...]
    log_a = (jnp.minimum(pre, 0.0) - jnp.log(1.0 + jnp.exp(-jnp.abs(pre)))) * (1.0 / GLA_TAU)

    lower = (_iota((C, C), 1) <= _iota((C, C), 0)).astype(BF16)
    b = jnp.concatenate([_dot_bf16_by_f32(lower, log_a[c * C:(c + 1) * C]) for c in range(tt // C)], axis=0)
    b_last = jnp.broadcast_to(b.reshape(tt // C, C, 128)[:, C - 1:C, :], (tt // C, C, 128)).reshape(tt, 128)
    q_dec = (q * (GLA_DK ** -0.5) * jnp.exp(b)).astype(BF16)
    k_dec = (k * jnp.exp(-b)).astype(BF16)
    k_end = (k * jnp.exp(b_last - b)).astype(BF16)
    decay = jnp.exp(b_last)

    qk_head = _iota((C, 128), 1) >> 5
    v_head = _iota((C, 256), 1) >> 6
    causal = _iota((4 * C, C), 1) <= (_iota((4 * C, C), 0) & (C - 1))
    state_mask = (_iota((256, 128), 0) >> 6) == (_iota((256, 128), 1) >> 5)

    for c in range(tt // C):
        sl = slice(c * C, (c + 1) * C)
        q_stack = jnp.concatenate([jnp.where(qk_head == hh, q_dec[sl], 0) for hh in range(GLA_HEADS)], axis=0)
        att_ref[c] = jnp.where(causal, _dot_nt(q_stack, k_dec[sl]), 0.0).astype(BF16)
        kv_ref[c] = jnp.where(state_mask, _dot_tn(z_ref[sl, 256:512], k_end[sl]), 0.0)
    state = st_ref[...]
    for c in range(tt // C):
        sl = slice(c * C, (c + 1) * C)
        intra = _head_select(_dot(att_ref[c], z_ref[sl, 256:512]), v_head, GLA_HEADS, C)
        inter = _dot_nt(q_dec[sl], state.astype(BF16))
        oacc_ref[sl, :] = intra + inter
        state = state * decay[c * C:c * C + 1] + kv_ref[c]
    st_ref[...] = state

    o_ref[...] = _head_norm_gate(oacc_ref[...], r, g_ref[...]).astype(BF16)


def _gla(z, wa, ba, g, seq):
    T = z.shape[0]
    tt = TOK_TILE
    return pl.pallas_call(
        functools.partial(_gla_kernel, tiles_per_seq=seq // tt),
        grid=(T // tt,),
        in_specs=[pl.BlockSpec((tt, GLA_W), lambda i: (i, 0)), _full((128, 128)), _full((1, 128)),
                  _full((1, 256))],
        out_specs=pl.BlockSpec((tt, 256), lambda i: (i, 0)),
        out_shape=jax.ShapeDtypeStruct((T, 256), BF16),
        scratch_shapes=[pltpu.VMEM((256, 128), F32), pltpu.VMEM((tt, 256), F32),
                        pltpu.VMEM((tt // GLA_CHUNK, GLA_HEADS * GLA_CHUNK, GLA_CHUNK), BF16),
                        pltpu.VMEM((tt // GLA_CHUNK, 256, 128), F32)],
        compiler_params=_params("arbitrary"),
        name="gla_mixer",
    )(z, wa, ba, g)


_RET_LOG_GAMMA = [math.log1p(-(2.0 ** (-5.0 - hh))) for hh in range(RET_HEADS)]


def _by_head(head, values):
    out = jnp.full(head.shape, values[-1], F32)
    for hh in range(len(values) - 2, -1, -1):
        out = jnp.where(head == hh, values[hh], out)
    return out


def _ret_kernel(z_ref, cos_ref, sina_ref, sinb_ref, g_ref, o_ref, st_ref, oacc_ref, *, tiles_per_seq):
    C = RET_CHUNK
    tt = z_ref.shape[0]

    @pl.when(pl.program_id(0) % tiles_per_seq == 0)
    def _():
        st_ref[...] = jnp.zeros_like(st_ref)

    cos, sina, sinb = cos_ref[...], sina_ref[...], sinb_ref[...]

    def rotate(t):
        return t * cos + pltpu.roll(t, 112, 1) * sina + pltpu.roll(t, 16, 1) * sinb

    q_rot = rotate(z_ref[:, 0:128].astype(F32))
    k_rot = rotate(z_ref[:, 128:256].astype(F32)) * (RET_DK ** -0.5)
    gate_in = z_ref[:, 512:768].astype(F32)

    qk_head = _iota((C, 128), 1) >> 5
    v_head = _iota((C, 256), 1) >> 6
    lg_qk = _by_head(_iota((1, 128), 1) >> 5, _RET_LOG_GAMMA)
    lg_v = _by_head(_iota((1, 256), 1) >> 6, _RET_LOG_GAMMA)
    pos_qk = _iota((C, 128), 0).astype(F32)
    pos_v = _iota((C, 256), 0).astype(F32)
    key_decay = jnp.exp((C - 1.0 - pos_qk) * lg_qk)
    query_decay = jnp.exp((pos_v + 1.0) * lg_v)
    chunk_decay = jnp.exp(float(C) * lg_qk)
    srow = _iota((4 * C, C), 0)
    rel = ((srow & (C - 1)) - _iota((4 * C, C), 1)).astype(F32)
    decay_mat = jnp.where(rel >= 0, jnp.exp(jnp.maximum(rel, 0.0) * _by_head(srow >> 7, _RET_LOG_GAMMA)), 0.0)
    state_mask = (_iota((256, 128), 0) >> 6) == (_iota((256, 128), 1) >> 5)

    q_bf = q_rot.astype(BF16)
    k_bf = k_rot.astype(BF16)
    state = st_ref[...]
    for c in range(tt // C):
        sl = slice(c * C, (c + 1) * C)
        qc = q_bf[sl]
        v = z_ref[sl, 256:512]
        q_stack = jnp.concatenate([jnp.where(qk_head == hh, qc, 0) for hh in range(RET_HEADS)], axis=0)
        att = _dot_nt(q_stack, k_bf[sl]) * decay_mat
        intra = _head_select(_dot(att.astype(BF16), v), v_head, RET_HEADS, C)
        inter = _dot_nt(qc, state.astype(BF16)) * query_decay
        oacc_ref[sl, :] = intra + inter
        kv = _dot_tn(v, (k_rot[sl] * key_decay).astype(BF16))
        state = state * chunk_decay + jnp.where(state_mask, kv, 0.0)
    st_ref[...] = state

    o_ref[...] = _head_norm_gate(oacc_ref[...], gate_in, g_ref[...]).astype(BF16)


def _retention(z, cos, sina, sinb, g, seq):
    T = z.shape[0]
    tt = TOK_TILE
    tps = seq // tt
    pos_spec = pl.BlockSpec((tt, 128), lambda i: (i % tps, 0))
    return pl.pallas_call(
        functools.partial(_ret_kernel, tiles_per_seq=tps),
        grid=(T // tt,),
        in_specs=[pl.BlockSpec((tt, RET_W), lambda i: (i, 0)), pos_spec, pos_spec, pos_spec, _full((1, 256))],
        out_specs=pl.BlockSpec((tt, 256), lambda i: (i, 0)),
        out_shape=jax.ShapeDtypeStruct((T, 256), BF16),
        scratch_shapes=[pltpu.VMEM((256, 128), F32), pltpu.VMEM((tt, 256), F32)],
        compiler_params=_params("arbitrary"),
        name="retention_mixer",
    )(z, cos, sina, sinb, g)


def _rope_tables(seq):
    half = RET_DK // 2
    pos = jnp.arange(seq, dtype=F32)
    inv_freq = RET_ROPE_BASE ** (-jnp.arange(half, dtype=F32) * 2.0 / RET_DK)
    ang = pos[:, None] * inv_freq[None, :]
    cos = jnp.tile(jnp.cos(ang), (1, LANES // half))
    sin = jnp.tile(jnp.sin(ang), (1, LANES // half))
    first_half = (jnp.arange(LANES) % RET_DK) < half
    return cos, jnp.where(first_half, -sin, 0.0), jnp.where(first_half, 0.0, sin)


def _swa_kernel(sink_ref, cur_ref, prev_ref, o_ref, s_ref, *, tiles_per_seq):
    W = SWA_WINDOW
    HD = SWA_HD
    tt = cur_ref.shape[0]
    first_key = jnp.where(pl.program_id(0) % tiles_per_seq == 0, W, 0)
    sj = _iota((2 * W, W), 0)
    qi = _iota((2 * W, W), 1)
    in_window = (sj > qi) & (sj <= qi + W)
    kv0 = SWA_HEADS * HD
    no_head = jnp.zeros((HD, W), BF16)
    n_blocks = tt // W

    def band(c, col0):
        prev = prev_ref[:, col0:col0 + 128] if c == 0 else cur_ref[(c - 1) * W:c * W, col0:col0 + 128]
        return jnp.concatenate([prev, cur_ref[c * W:(c + 1) * W, col0:col0 + 128]], axis=0)

    for c in range(n_blocks):
        mask = in_window & (sj >= first_key) if c == 0 else in_window
        k_band = band(c, kv0)
        q_t = cur_ref[c * W:(c + 1) * W, 0:kv0].astype(F32).T.astype(BF16)
        for hh in range(SWA_HEADS):
            kk = hh // (SWA_HEADS // SWA_KV_HEADS)
            qh_t = q_t[hh * HD:(hh + 1) * HD]
            q_on_kv = jnp.concatenate([qh_t, no_head] if kk == 0 else [no_head, qh_t], axis=0)
            s = _dot(k_band, q_on_kv) * (HD ** -0.5 * LOG2_E)
            s_ref[c, hh] = jnp.where(mask, s, NEG_INF)
    for c in range(n_blocks):
        v_t = band(c, kv0 + 128).astype(F32).T.astype(BF16)
        outs = []
        for hh in range(SWA_HEADS):
            kk = hh // (SWA_HEADS // SWA_KV_HEADS)
            s = s_ref[c, hh]
            sink = sink_ref[hh] * LOG2_E
            m = jnp.maximum(jnp.max(s, axis=0, keepdims=True), sink)
            p = jnp.exp2(s - m)
            denom = jnp.sum(p, axis=0, keepdims=True) + jnp.exp2(sink - m)
            outs.append(_dot(v_t[kk * HD:(kk + 1) * HD], p.astype(BF16)) / denom)
        o_ref[c * W:(c + 1) * W, :] = jnp.concatenate(outs, axis=0).T.astype(BF16)


def _swa(z, sinks, seq):
    T = z.shape[0]
    tt = TOK_TILE
    per = tt // SWA_WINDOW
    return pl.pallas_call(
        functools.partial(_swa_kernel, tiles_per_seq=seq // tt),
        grid=(T // tt,),
        in_specs=[pl.BlockSpec(memory_space=pltpu.SMEM),
                  pl.BlockSpec((tt, SWA_W), lambda i: (i, 0)),
                  pl.BlockSpec((SWA_WINDOW, SWA_W), lambda i: (jnp.maximum(i * per - 1, 0), 0))],
        out_specs=pl.BlockSpec((tt, 256), lambda i: (i, 0)),
        out_shape=jax.ShapeDtypeStruct((T, 256), BF16),
        scratch_shapes=[pltpu.VMEM((per, SWA_HEADS, 2 * SWA_WINDOW, SWA_WINDOW), F32)],
        compiler_params=_params("parallel"),
        name="swa_mixer",
    )(sinks, z, z)


_KMEAN_BLOCKS = 8


def _kmean_kernel(k_ref, o_ref):
    k = k_ref[...].astype(F32).reshape(_KMEAN_BLOCKS, MOBA_BLOCK, 256)
    o_ref[...] = jnp.mean(k, axis=1)


def _moba_kmean(z):
    T = z.shape[0]
    rows = _KMEAN_BLOCKS * MOBA_BLOCK
    return pl.pallas_call(
        _kmean_kernel,
        grid=(T // rows,),
        in_specs=[pl.BlockSpec((rows, 256), lambda i: (i, 1))],
        out_specs=pl.BlockSpec((_KMEAN_BLOCKS, 256), lambda i: (i, 0)),
        out_shape=jax.ShapeDtypeStruct((T // MOBA_BLOCK, 256), F32),
        compiler_params=_params("parallel"),
        name="moba_kmean",
    )(z)


MOBA_VROWS = 80
MOBA_PREP_BLOCKS = 2


def _moba_prep_kernel(z_ref, km_ref, qt_out, k_out, vt_out):
    HD = MOBA_HD
    tq = MOBA_BLOCK
    nb = km_ref.shape[0]
    blk_i = _iota((nb, tq), 0)
    blk = blk_i.astype(F32)
    pad = LANES - HD - nb
    ones_row = (_iota((MOBA_VROWS - HD, tq), 0) == 0).astype(BF16)
    for sb in range(MOBA_PREP_BLOCKS):
        rows = slice(sb * tq, (sb + 1) * tq)
        qb = (pl.program_id(0) * MOBA_PREP_BLOCKS + sb) % nb
        this_block_lanes = (_iota((tq, nb), 1) == qb).astype(BF16)
        for hh in range(MOBA_HEADS):
            cols = slice(hh * HD, (hh + 1) * HD)
            q_t = z_ref[rows, cols].astype(F32).T
            gate = _dot(km_ref[:, cols], q_t, precision=HIGHEST)
            gate = jnp.where(blk_i < qb, gate, NEG_INF)
            keep = jnp.where(blk_i == qb, 1.0, 0.0)
            for _ in range(MOBA_TOPK):
                best = jnp.max(gate, axis=0, keepdims=True)
                first = jnp.min(jnp.where(gate == best, blk, float(nb)), axis=0, keepdims=True)
                hit = blk == first
                keep = keep + jnp.where(hit, jnp.where(best > 0.5 * NEG_INF, 1.0, 0.0), 0.0)
                gate = jnp.where(hit, BELOW_NEG_INF, gate)
            bias = ((keep - 1.0) * (-NEG_INF)).astype(BF16)
            qt_out[hh, sb] = jnp.concatenate(
                [(q_t * (HD ** -0.5 * LOG2_E)).astype(BF16), bias, jnp.zeros((pad, tq), BF16)], axis=0)
            k_out[hh, rows, :] = jnp.concatenate(
                [z_ref[rows, 256 + hh * HD:256 + (hh + 1) * HD], this_block_lanes, jnp.zeros((tq, pad), BF16)],
                axis=1)
            v_t = z_ref[rows, 512 + hh * HD:512 + (hh + 1) * HD].astype(F32).T.astype(BF16)
            vt_out[hh, sb] = jnp.concatenate([v_t, ones_row], axis=0)


def _moba_prep(z, kmean, seq):
    T = z.shape[0]
    nb = seq // MOBA_BLOCK
    assert MOBA_HD + nb <= LANES and nb % MOBA_PREP_BLOCKS == 0
    tq = MOBA_BLOCK
    tile = MOBA_PREP_BLOCKS * tq
    H = MOBA_HEADS

    def per_block_t(rows):
        return pl.BlockSpec((H, MOBA_PREP_BLOCKS, rows, tq), lambda i: (0, i, 0, 0))

    return pl.pallas_call(
        _moba_prep_kernel,
        grid=(T // tile,),
        in_specs=[pl.BlockSpec((tile, MOBA_W), lambda i: (i, 0)),
                  pl.BlockSpec((nb, 256), lambda i: (i * MOBA_PREP_BLOCKS // nb, 0))],
        out_specs=[per_block_t(LANES), pl.BlockSpec((H, tile, LANES), lambda i: (0, i, 0)),
                   per_block_t(MOBA_VROWS)],
        out_shape=[jax.ShapeDtypeStruct((H, T // tq, LANES, tq), BF16), jax.ShapeDtypeStruct((H, T, LANES), BF16),
                   jax.ShapeDtypeStruct((H, T // tq, MOBA_VROWS, tq), BF16)],
        compiler_params=_params("parallel"),
        name="moba_prep",
    )(z, kmean)


MOBA_QBLOCKS = 4
MOBA_KBLOCKS = 4
MOBA_DIAGONALS = MOBA_QBLOCKS // MOBA_KBLOCKS


def _moba_kernel(qidx_ref, kidx_ref, kind_ref, qt_ref, k_ref, vt_ref, o_ref, m_ref, acc_ref, s_ref, smax_ref):
    HD = MOBA_HD
    B = MOBA_BLOCK
    step = pl.program_id(1)
    kind = kind_ref[step]
    is_last = (step + 1 == pl.num_programs(1)) | (qidx_ref[jnp.minimum(step + 1, pl.num_programs(1) - 1)]
                                                  != qidx_ref[step])

    def attend(diagonal):
        causal = _iota((B, B), 0) <= _iota((B, B), 1)

        def key_blocks_of(qi):
            if diagonal is None:
                return list(range(MOBA_KBLOCKS))
            return [c for c in range(MOBA_KBLOCKS) if MOBA_KBLOCKS * diagonal + c <= qi]

        for hh in range(MOBA_HEADS):
            for qi in range(MOBA_QBLOCKS):
                for c in key_blocks_of(qi):
                    s = _dot(k_ref[hh, c * B:(c + 1) * B, :], qt_ref[hh, qi])
                    if diagonal is not None and MOBA_KBLOCKS * diagonal + c == qi:
                        s = jnp.where(causal, s, NEG_INF)
                    s_ref[hh, qi, c] = s
                    smax_ref[hh, qi, c] = jnp.max(s, axis=0, keepdims=True)
        first_visit = diagonal == 0
        for hh in range(MOBA_HEADS):
            for qi in range(MOBA_QBLOCKS):
                key_blocks = key_blocks_of(qi)
                if not key_blocks:
                    continue
                m_new = functools.reduce(jnp.maximum, [smax_ref[hh, qi, c] for c in key_blocks])
                if not first_visit:
                    m_old = m_ref[hh, qi]
                    m_new = jnp.maximum(m_old, m_new)
                pv = None
                for c in key_blocks:
                    p = jnp.exp2(s_ref[hh, qi, c] - m_new).astype(BF16)
                    term = _dot(vt_ref[hh, c], p)
                    pv = term if pv is None else pv + term
                acc_ref[hh, qi] = pv if first_visit else jnp.exp2(m_old - m_new) * acc_ref[hh, qi] + pv
                m_ref[hh, qi] = m_new

    @pl.when(kind == 0)
    def _():
        attend(None)

    for d in range(MOBA_DIAGONALS):
        @pl.when(kind == 1 + d)
        def _(d=d):
            attend(d)

    @pl.when(is_last)
    def _():
        for qi in range(MOBA_QBLOCKS):
            heads = [acc_ref[hh, qi, 0:HD, :] / acc_ref[hh, qi, HD:HD + 1, :] for hh in range(MOBA_HEADS)]
            o_ref[qi * B:(qi + 1) * B, :] = jnp.concatenate(heads, axis=0).T.astype(BF16)


def _moba(z, seq):
    T = z.shape[0]
    nb = seq // MOBA_BLOCK
    assert nb % MOBA_QBLOCKS == 0
    q_tiles = nb // MOBA_QBLOCKS
    k_steps = nb // MOBA_KBLOCKS
    D = MOBA_DIAGONALS
    batch = T // seq
    qt_aug, k_aug, vt_aug = _moba_prep(z, _moba_kmean(z), seq)
    qidx = np.concatenate([np.full(D * t + D, t) for t in range(q_tiles)]).astype(np.int32)
    kidx = np.concatenate([np.concatenate([D * t + np.arange(D), np.arange(D * t)])
                           for t in range(q_tiles)]).astype(np.int32)
    kind = np.concatenate([np.concatenate([1 + np.arange(D), np.zeros(D * t, np.int64)])
                           for t in range(q_tiles)]).astype(np.int32)
    tq = MOBA_QBLOCKS * MOBA_BLOCK
    tk = MOBA_KBLOCKS * MOBA_BLOCK
    H = MOBA_HEADS
    grid_spec = pltpu.PrefetchScalarGridSpec(
        num_scalar_prefetch=3,
        grid=(batch, len(qidx)),
        in_specs=[pl.BlockSpec((H, MOBA_QBLOCKS, LANES, MOBA_BLOCK),
                               lambda b, s, qi, ki, kd: (0, b * q_tiles + qi[s], 0, 0)),
                  pl.BlockSpec((H, tk, LANES), lambda b, s, qi, ki, kd: (0, b * k_steps + ki[s], 0)),
                  pl.BlockSpec((H, MOBA_KBLOCKS, MOBA_VROWS, MOBA_BLOCK),
                               lambda b, s, qi, ki, kd: (0, b * k_steps + ki[s], 0, 0))],
        out_specs=pl.BlockSpec((tq, 256), lambda b, s, qi, ki, kd: (b * q_tiles + qi[s], 0)),
        scratch_shapes=[pltpu.VMEM((H, MOBA_QBLOCKS, 1, MOBA_BLOCK), F32),
                        pltpu.VMEM((H, MOBA_QBLOCKS, MOBA_VROWS, MOBA_BLOCK), F32),
                        pltpu.VMEM((H, MOBA_QBLOCKS, MOBA_KBLOCKS, MOBA_BLOCK, MOBA_BLOCK), F32),
                        pltpu.VMEM((H, MOBA_QBLOCKS, MOBA_KBLOCKS, 1, MOBA_BLOCK), F32)],
    )
    return pl.pallas_call(
        _moba_kernel,
        grid_spec=grid_spec,
        out_shape=jax.ShapeDtypeStruct((T, 256), BF16),
        compiler_params=_params("parallel", "arbitrary"),
        name="moba_mixer",
    )(jnp.asarray(qidx), jnp.asarray(kidx), jnp.asarray(kind), qt_aug, k_aug, vt_aug)


def _merge_kernel(h_ref, o0, o1, o2, o3, half_gates_ref, half_wb_ref, wout_ref, out_ref):
    merged = None
    for i, o in enumerate((o0, o1, o2, o3)):
        half_y = _dot(o[...], half_wb_ref[i])
        term = half_y * jnp.tanh(half_gates_ref[:, i * D_MODEL:(i + 1) * D_MODEL].astype(F32)) + half_y
        merged = term if merged is None else merged + term
    out_ref[...] = h_ref[...] + _dot(merged.astype(BF16), wout_ref[...])


def _merge(h, branches, gates, wb, wout):
    T = h.shape[0]
    tm = TOK_TILE
    return pl.pallas_call(
        _merge_kernel,
        grid=(T // tm,),
        in_specs=[pl.BlockSpec((tm, D_MODEL), lambda i: (i, 0))]
        + [pl.BlockSpec((tm, 256), lambda i: (i, 0))] * 4
        + [pl.BlockSpec((tm, GATE_W), lambda i: (i, 0)),
           pl.BlockSpec((N_BRANCH, 256, D_MODEL), lambda i: (0, 0, 0), pipeline_mode=pl.Buffered(1)),
           pl.BlockSpec((D_MODEL, D_MODEL), lambda i: (0, 0), pipeline_mode=pl.Buffered(1))],
        out_specs=pl.BlockSpec((tm, D_MODEL), lambda i: (i, 0)),
        out_shape=jax.ShapeDtypeStruct((T, D_MODEL), F32),
        compiler_params=_params("parallel"),
        name="merge_out_proj",
    )(h, *branches, gates, wb, wout)


ROW_HALF = D_MODEL // 2
ROW_TILES = ROW_HALF // LANES
HIGH_HALF_MASK = -65536


def _row_tile_spec(rows, index_map):
    return pl.BlockSpec((rows * ROW_TILES, LANES), index_map)


def _bf16_bits(x):
    return lax.bitcast_convert_type(x.astype(BF16).astype(F32), I32)


def _pack_rows(ref, x):
    for j in range(ROW_TILES):
        low = _bf16_bits(x[:, j * LANES:(j + 1) * LANES])
        high = _bf16_bits(x[:, ROW_HALF + j * LANES:ROW_HALF + (j + 1) * LANES])
        ref[pl.ds(j, x.shape[0], stride=ROW_TILES), :] = high | lax.shift_right_logical(low, 16)


def _unpack_slab(ref, j):
    words = ref[pl.ds(j, ref.shape[0] // ROW_TILES, stride=ROW_TILES), :]
    low = lax.bitcast_convert_type(lax.shift_left(words, 16), F32)
    high = lax.bitcast_convert_type(words & HIGH_HALF_MASK, F32)
    return low, high


def _unpack_rows(ref):
    halves = [_unpack_slab(ref, j) for j in range(ROW_TILES)]
    return jnp.concatenate([lo for lo, _ in halves] + [hi for _, hi in halves], axis=1).astype(BF16)


def _router_kernel(h_ref, g_ref, wr_ref, br_ref, u_ref, route_ref, route_t_ref, count_ref, carry_ref):
    tm = h_ref.shape[0]

    @pl.when(pl.program_id(0) == 0)
    def _():
        carry_ref[...] = jnp.zeros_like(carry_ref)

    u = _rms(h_ref[...], g_ref[...])
    _pack_rows(u_ref, u)
    u_hi, u_lo = _split_bf16(u, 2)
    w_hi, w_lo = _split_bf16(wr_ref[...], 2)
    logits = _dot(u_hi, w_hi) + (_dot(u_lo, w_hi) + _dot(u_hi, w_lo)) + br_ref[...]
    lt = logits.T
    row_i = _iota((LANES, tm), 0)
    row = row_i.astype(F32)
    row_group = ((row_i - EXPERT_LANE0) >> 3).astype(F32)

    def first_row(mask):
        return jnp.min(jnp.where(mask, row, float(LANES)), axis=0, keepdims=True)

    grp = jnp.where(row_i < N_GROUPS, lt, BELOW_NEG_INF)
    grp_e = jnp.exp(grp - jnp.max(grp, axis=0, keepdims=True))
    p_group = grp_e / jnp.sum(grp_e, axis=0, keepdims=True)
    g_w = jnp.max(p_group, axis=0, keepdims=True)
    g_sel = first_row(p_group == g_w)

    in_group = (row_i >= EXPERT_LANE0) & (row_group == g_sel)
    el = jnp.where(in_group, lt, BELOW_NEG_INF)
    top1 = jnp.max(el, axis=0, keepdims=True)
    row1 = first_row(el == top1)
    el = jnp.where(row == row1, BELOW_NEG_INF, el)
    top2 = jnp.max(el, axis=0, keepdims=True)
    row2 = first_row(el == top2)
    e2 = jnp.exp(top2 - top1)
    w1 = g_w / (1.0 + e2)
    w2 = g_w * e2 / (1.0 + e2)

    onehot = ((row == row1) | (row == row2)).astype(BF16)
    earlier = (_iota((tm, tm), 0) < _iota((tm, tm), 1)).astype(BF16)
    seen = _dot(onehot, earlier) + carry_ref[:, 0:1]
    rank1 = jnp.sum(jnp.where(row == row1, seen, 0.0), axis=0, keepdims=True)
    rank2 = jnp.sum(jnp.where(row == row2, seen, 0.0), axis=0, keepdims=True)
    carry_ref[...] += jnp.sum(onehot.astype(F32), axis=1, keepdims=True)
    count_ref[...] = carry_ref[...]

    out_t = jnp.zeros((LANES, tm), F32)
    for idx, val in enumerate((row1 - EXPERT_LANE0, row2 - EXPERT_LANE0, w1, w2, rank1, rank2)):
        out_t = jnp.where(row_i == idx, val, out_t)
    route_t_ref[...] = out_t[0:8, :]
    route_ref[...] = out_t.T


def _router(h, g, wr, br):
    T = h.shape[0]
    tm = TOK_TILE
    return pl.pallas_call(
        _router_kernel,
        grid=(T // tm,),
        in_specs=[pl.BlockSpec((tm, D_MODEL), lambda i: (i, 0)), _full((1, D_MODEL)),
                  _full((D_MODEL, LANES)), _full((1, LANES))],
        out_specs=[_row_tile_spec(tm, lambda i: (i, 0)), pl.BlockSpec((tm, LANES), lambda i: (i, 0)),
                   pl.BlockSpec((8, tm), lambda i: (0, i)), _full((LANES, LANES))],
        out_shape=[jax.ShapeDtypeStruct((T * ROW_TILES, LANES), I32), jax.ShapeDtypeStruct((T, LANES), F32),
                   jax.ShapeDtypeStruct((8, T), F32), jax.ShapeDtypeStruct((LANES, LANES), F32)],
        scratch_shapes=[pltpu.VMEM((LANES, LANES), F32)],
        compiler_params=_params("arbitrary"),
        name="moe_router",
    )(h, g, wr, br)


ROW_UNROLL = 8


def _row_copy(src_ref, src_row, dst_ref, dst_row, sem):
    src = src_ref.at[pl.ds(pl.multiple_of(src_row * ROW_TILES, ROW_TILES), ROW_TILES)]
    dst = dst_ref.at[pl.ds(pl.multiple_of(dst_row * ROW_TILES, ROW_TILES), ROW_TILES)]
    return pltpu.make_async_copy(src, dst, sem)


def _for_each_row_pair(rows, body):
    def group(g, carry):
        for i in range(ROW_UNROLL):
            for k in range(EXPERT_TOPK):
                body(g * ROW_UNROLL + i, k)
        return carry

    lax.fori_loop(0, rows // ROW_UNROLL, group, 0)


def _dispatch_kernel(dest0_ref, dest1_ref, pstart_ref, pend_ref, u_ref, xs_ref, zero_ref, sem):
    tm = u_ref.shape[0] // ROW_TILES
    tile = pl.program_id(0)
    dests = (dest0_ref, dest1_ref)

    @pl.when(tile == 0)
    def _():
        zero_ref[...] = jnp.zeros_like(zero_ref)

        def zero_block(first_row):
            start = pl.multiple_of(first_row * ROW_TILES, ROW_TILES)
            fill = pltpu.make_async_copy(zero_ref, xs_ref.at[pl.ds(start, MOE_BLOCK * ROW_TILES)], sem)
            fill.start()
            fill.wait()

        def zero_tail(e, carry):
            @pl.when(pend_ref[e] > pstart_ref[e])
            def _():
                zero_block(pend_ref[e] - MOE_BLOCK)
            return carry

        def zero_unused(b, carry):
            zero_block(b * MOE_BLOCK)
            return carry

        lax.fori_loop(0, N_EXPERTS, zero_tail, 0)
        lax.fori_loop(pend_ref[N_EXPERTS - 1] // MOE_BLOCK, xs_ref.shape[0] // (MOE_BLOCK * ROW_TILES),
                      zero_unused, 0)

    def issue(r, k):
        _row_copy(u_ref, r, xs_ref, dests[k][tile * tm + r], sem).start(priority=k)

    _for_each_row_pair(tm, issue)
    for _ in range(EXPERT_TOPK):
        pltpu.make_async_copy(u_ref, xs_ref.at[pl.ds(0, tm * ROW_TILES)], sem).wait()


def _dispatch(dest0, dest1, pstart, pend, u, n_slots):
    T = u.shape[0] // ROW_TILES
    tm = DISPATCH_TILE
    grid_spec = pltpu.PrefetchScalarGridSpec(
        num_scalar_prefetch=4,
        grid=(T // tm,),
        in_specs=[_row_tile_spec(tm, lambda i, *_: (i, 0))],
        out_specs=pl.BlockSpec(memory_space=pl.ANY),
        scratch_shapes=[pltpu.VMEM((MOE_BLOCK * ROW_TILES, LANES), I32), pltpu.SemaphoreType.DMA(())],
    )
    return pl.pallas_call(
        _dispatch_kernel,
        grid_spec=grid_spec,
        out_shape=jax.ShapeDtypeStruct((n_slots * ROW_TILES, LANES), I32),
        compiler_params=_params("arbitrary"),
        name="moe_dispatch",
    )(dest0, dest1, pstart, pend, u)


def _expert_kernel(blk_e_ref, n_used_ref, x_ref, w1_ref, w3_ref, w2_ref, y_ref, w1b_ref, w3b_ref, w2b_ref):
    i = pl.program_id(0)
    used = i < n_used_ref[0]

    @pl.when((i == 0) | (blk_e_ref[i] != blk_e_ref[jnp.maximum(i - 1, 0)]))
    def _():
        w1b_ref[...] = w1_ref[0, 0].astype(BF16)
        w3b_ref[...] = w3_ref[0, 0].astype(BF16)
        w2b_ref[...] = w2_ref[0, 0].astype(BF16)

    @pl.when(used)
    def _():
        x = _unpack_rows(x_ref)
        a = _dot(x, w1b_ref[...])
        hidden = (a * _sigmoid(a)) * _dot(x, w3b_ref[...])
        _pack_rows(y_ref, _dot(hidden.astype(BF16), w2b_ref[...]))

    @pl.when(jnp.logical_not(used))
    def _():
        y_ref[...] = jnp.zeros_like(y_ref)


def _experts(blk_e, n_used, xs, w1, w3, w2, layer):
    n_slots = xs.shape[0] // ROW_TILES
    bm = MOE_BLOCK
    grid_spec = pltpu.PrefetchScalarGridSpec(
        num_scalar_prefetch=2,
        grid=(n_slots // bm,),
        in_specs=[_row_tile_spec(bm, lambda i, be, nu: (jnp.minimum(i, nu[0] - 1), 0)),
                  pl.BlockSpec((1, 1, D_MODEL, EXPERT_HIDDEN), lambda i, be, nu: (layer, be[i], 0, 0)),
                  pl.BlockSpec((1, 1, D_MODEL, EXPERT_HIDDEN), lambda i, be, nu: (layer, be[i], 0, 0)),
                  pl.BlockSpec((1, 1, EXPERT_HIDDEN, D_MODEL), lambda i, be, nu: (layer, be[i], 0, 0))],
        out_specs=_row_tile_spec(bm, lambda i, be, nu: (i, 0)),
        scratch_shapes=[pltpu.VMEM((D_MODEL, EXPERT_HIDDEN), BF16), pltpu.VMEM((D_MODEL, EXPERT_HIDDEN), BF16),
                        pltpu.VMEM((EXPERT_HIDDEN, D_MODEL), BF16)],
    )
    return pl.pallas_call(
        _expert_kernel,
        grid_spec=grid_spec,
        out_shape=jax.ShapeDtypeStruct((n_slots * ROW_TILES, LANES), I32),
        compiler_params=_params("arbitrary"),
        name="moe_experts",
    )(blk_e, n_used, xs, w1, w3, w2)


def _combine_kernel(dest0_ref, dest1_ref, h_ref, route_ref, g_ref, ys_ref, out_ref, y_ref, sem, *, normalize):
    tm = h_ref.shape[0]
    step = pl.program_id(0)
    slot = step % 2

    dests = (dest0_ref, dest1_ref)

    def gather_tile(tile, buf):
        def issue(r, k):
            _row_copy(ys_ref, dests[k][tile * tm + r], y_ref.at[buf, k], r, sem.at[buf]).start(priority=k)

        _for_each_row_pair(tm, issue)

    @pl.when(step == 0)
    def _():
        gather_tile(0, 0)

    @pl.when(step + 1 < pl.num_programs(0))
    def _():
        gather_tile(step + 1, 1 - slot)

    for k in range(EXPERT_TOPK):
        pltpu.make_async_copy(ys_ref.at[pl.ds(0, tm * ROW_TILES)], y_ref.at[slot, k], sem.at[slot]).wait()
    w0 = route_ref[:, 2:3]
    w1 = route_ref[:, 3:4]
    for j in range(ROW_TILES):
        low0, high0 = _unpack_slab(y_ref.at[slot, 0], j)
        low1, high1 = _unpack_slab(y_ref.at[slot, 1], j)
        cols = slice(j * LANES, (j + 1) * LANES)
        out_ref[:, cols] = h_ref[:, cols] + w0 * low0 + w1 * low1
        cols = slice(ROW_HALF + j * LANES, ROW_HALF + (j + 1) * LANES)
        out_ref[:, cols] = h_ref[:, cols] + w0 * high0 + w1 * high1
    if normalize:
        out_ref[...] = _rms(out_ref[...], g_ref[...])


def _combine(dest0, dest1, h, route, ys, final_g, normalize):
    T = h.shape[0]
    tm = ROW_TILE
    grid_spec = pltpu.PrefetchScalarGridSpec(
        num_scalar_prefetch=2,
        grid=(T // tm,),
        in_specs=[pl.BlockSpec((tm, D_MODEL), lambda i, *_: (i, 0)),
                  pl.BlockSpec((tm, LANES), lambda i, *_: (i, 0)),
                  _full((1, D_MODEL)),
                  pl.BlockSpec(memory_space=pl.ANY)],
        out_specs=pl.BlockSpec((tm, D_MODEL), lambda i, *_: (i, 0)),
        scratch_shapes=[pltpu.VMEM((2, EXPERT_TOPK, tm * ROW_TILES, LANES), I32),
                        pltpu.SemaphoreType.DMA((2,))],
    )
    return pl.pallas_call(
        functools.partial(_combine_kernel, normalize=normalize),
        grid_spec=grid_spec,
        out_shape=jax.ShapeDtypeStruct((T, D_MODEL), F32),
        compiler_params=_params("arbitrary"),
        name="moe_combine",
    )(dest0, dest1, h, route, final_g, ys)


def _moe(h, g, w_group, b_group, w_expert, b_expert, w1, w3, w2, layer, final_g, normalize):
    T = h.shape[0]
    wr = jnp.zeros((D_MODEL, LANES), F32)
    wr = wr.at[:, :N_GROUPS].set(w_group).at[:, EXPERT_LANE0:EXPERT_LANE0 + N_EXPERTS].set(w_expert)
    br = jnp.zeros((1, LANES), F32)
    br = br.at[0, :N_GROUPS].set(b_group).at[0, EXPERT_LANE0:EXPERT_LANE0 + N_EXPERTS].set(b_expert)
    u, route, route_t, count = _router(h, g, wr, br)

    n_blocks = (T * EXPERT_TOPK) // MOE_BLOCK + N_EXPERTS
    counts = count[EXPERT_LANE0:EXPERT_LANE0 + N_EXPERTS, 0].astype(I32)
    padded = ((counts + MOE_BLOCK - 1) // MOE_BLOCK) * MOE_BLOCK
    pend = jnp.cumsum(padded)
    pstart = pend - padded
    expert_ids = jnp.arange(N_EXPERTS, dtype=F32)[:, None]

    def slot_of(expert_row, rank_row):
        segment_start = jnp.sum(jnp.where(expert_row[None, :] == expert_ids, pstart[:, None], 0), axis=0)
        return segment_start + rank_row.astype(I32)

    dest0 = slot_of(route_t[0], route_t[4])
    dest1 = slot_of(route_t[1], route_t[5])
    block_row0 = jnp.arange(n_blocks, dtype=I32) * MOE_BLOCK
    blk_e = jnp.minimum(jnp.sum((pend[None, :] <= block_row0[:, None]).astype(I32), axis=1), N_EXPERTS - 1)
    n_used = (pend[-1:] // MOE_BLOCK).astype(I32)

    xs = _dispatch(dest0, dest1, pstart, pend, u, n_blocks * MOE_BLOCK)
    ys = _experts(blk_e, n_used, xs, w1, w3, w2, layer)
    return _combine(dest0, dest1, h, route, ys, final_g, normalize)


def _split_w_in(w_in):
    gla_n = GLA_HEADS * (2 * GLA_DK + 2 * GLA_DV) + GLA_LOWRANK
    swa_n = (SWA_HEADS + 2 * SWA_KV_HEADS) * SWA_HD
    moba_n = 3 * MOBA_HEADS * MOBA_HD
    ret_n = RET_HEADS * (2 * RET_DK + 2 * RET_DV)
    offs = np.cumsum([0, gla_n, swa_n, moba_n, ret_n, GATE_W])
    parts = [w_in[:, offs[i]:offs[i + 1]].astype(BF16) for i in range(5)]
    parts[0] = jnp.pad(parts[0], ((0, 0), (0, GLA_W - gla_n)))
    parts[4] = parts[4] * 0.5
    return parts


def _token_mixer(h, seq, ln1_g, w_in, gla_w_a2, gla_b_a, gla_norm_g, swa_sinks, ret_norm_g, w_branch, w_out,
                 rope):
    z_gla, z_swa, z_moba, z_ret, half_gates = _in_proj(h, ln1_g.reshape(1, D_MODEL), _split_w_in(w_in))
    wa = jnp.zeros((128, 128), F32).at[:GLA_LOWRANK].set(gla_w_a2)
    o_gla = _gla(z_gla, wa, gla_b_a.reshape(1, 128), jnp.tile(gla_norm_g, GLA_HEADS).reshape(1, 256), seq)
    o_swa = _swa(z_swa, swa_sinks, seq)
    o_moba = _moba(z_moba, seq)
    o_ret = _retention(z_ret, *rope, jnp.tile(ret_norm_g, RET_HEADS).reshape(1, 256), seq)
    half_w_branch = w_branch.astype(BF16) * 0.5
    return _merge(h, (o_gla, o_swa, o_moba, o_ret), half_gates, half_w_branch, w_out.astype(BF16))


def kernel(x, ln1_g, w_in, gla_w_a2, gla_b_a, gla_norm_g, swa_sinks, ret_norm_g, w_branch, w_out, ln2_g,
           w_group, b_group, w_expert, b_expert, w1, w3, w2, final_g):
    batch, seq, _ = x.shape
    depth = w_in.shape[0]
    assert depth >= 1
    assert x.shape[2] == D_MODEL and seq % (MOBA_QBLOCKS * MOBA_BLOCK) == 0 and seq % TOK_TILE == 0
    assert (batch * seq) % DISPATCH_TILE == 0 and (batch * seq * EXPERT_TOPK) % MOE_BLOCK == 0
    rope = _rope_tables(seq)
    h = x.reshape(batch * seq, D_MODEL)
    for l in range(depth):
        h = _token_mixer(h, seq, ln1_g[l], w_in[l], gla_w_a2[l], gla_b_a[l], gla_norm_g[l], swa_sinks[l],
                         ret_norm_g[l], w_branch[l], w_out[l], rope)
        h = _moe(h, ln2_g[l].reshape(1, D_MODEL), w_group[l], b_group[l], w_expert[l], b_expert[l],
                 w1, w3, w2, l, final_g.reshape(1, D_MODEL), normalize=(l == depth - 1))
    return h.reshape(batch, seq, D_MODEL)
```

```python
import functools
import math

import numpy as np
import jax
import jax.numpy as jnp
from jax import lax
from jax.experimental import pallas as pl
from jax.experimental.pallas import tpu as pltpu

F32 = jnp.float32
BF16 = jnp.bfloat16
I32 = jnp.int32
HIGHEST = lax.Precision.HIGHEST

D_MODEL = 1024
N_BRANCH = 4
NORM_EPS = 1e-6
NEG_INF = -1e30
BELOW_NEG_INF = -3e38
LOG2_E = math.log2(math.e)

GLA_HEADS, GLA_DK, GLA_DV, GLA_LOWRANK, GLA_TAU, GLA_CHUNK = 4, 32, 64, 16, 16.0, 64
SWA_HEADS, SWA_KV_HEADS, SWA_HD, SWA_WINDOW = 4, 2, 64, 128
MOBA_HEADS, MOBA_HD, MOBA_BLOCK, MOBA_TOPK = 4, 64, 256, 3
RET_HEADS, RET_DK, RET_DV, RET_CHUNK, RET_ROPE_BASE = 4, 32, 64, 128, 10000.0
N_GROUPS, EXPERTS_PER_GROUP, EXPERT_TOPK, EXPERT_HIDDEN = 4, 8, 2, 256
MOE_BLOCK = 512
N_EXPERTS = N_GROUPS * EXPERTS_PER_GROUP

GLA_W = 896
SWA_W = 512
MOBA_W = 768
RET_W = 768
GATE_W = N_BRANCH * D_MODEL

LANES = 128
EXPERT_LANE0 = 32

VMEM_LIMIT = 56 * 1024 * 1024

TOK_TILE = 512
ROW_TILE = 512
DISPATCH_TILE = 2048


def _params(*sem):
    return pltpu.CompilerParams(dimension_semantics=sem, vmem_limit_bytes=VMEM_LIMIT)


def _dot(a, b, precision=None):
    return jnp.dot(a, b, preferred_element_type=F32, precision=precision)


def _dot_nt(a, b, precision=None):
    return lax.dot_general(a, b, (((1,), (1,)), ((), ())), preferred_element_type=F32, precision=precision)


def _dot_tn(a, b, precision=None):
    return lax.dot_general(a, b, (((0,), (0,)), ((), ())), preferred_element_type=F32, precision=precision)


def _split_bf16(x, parts):
    out = []
    for _ in range(parts - 1):
        piece = x.astype(BF16)
        out.append(piece)
        x = x - piece.astype(F32)
    out.append(x.astype(BF16))
    return out


def _dot_f32_by_bf16(x, m):
    return sum(_dot(piece, m) for piece in _split_bf16(x, 3))


def _dot_bf16_by_f32(m, x):
    return sum(_dot(m, piece) for piece in _split_bf16(x, 3))


def _iota(shape, dim):
    return lax.broadcasted_iota(I32, shape, dim)


def _sigmoid(x):
    return 0.5 * jnp.tanh(0.5 * x) + 0.5


def _rms(x, g):
    ms = jnp.mean(x * x, axis=-1, keepdims=True)
    return x * lax.rsqrt(ms + NORM_EPS) * g


def _full(shape):
    return pl.BlockSpec(shape, lambda *_: (0,) * len(shape))


def _in_proj_kernel(h_ref, g_ref, w0, w1, w2, w3, w4, o0, o1, o2, o3, o4):
    u = _rms(h_ref[...], g_ref[...]).astype(BF16)
    for w, o in ((w0, o0), (w1, o1), (w2, o2), (w3, o3), (w4, o4)):
        o[...] = _dot(u, w[...]).astype(BF16)


def _in_proj(h, g, ws):
    T = h.shape[0]
    tm = TOK_TILE
    widths = [w.shape[1] for w in ws]
    return pl.pallas_call(
        _in_proj_kernel,
        grid=(T // tm,),
        in_specs=[pl.BlockSpec((tm, D_MODEL), lambda i: (i, 0)), _full((1, D_MODEL))]
        + [pl.BlockSpec((D_MODEL, n), lambda i: (0, 0), pipeline_mode=pl.Buffered(1)) for n in widths],
        out_specs=[pl.BlockSpec((tm, n), lambda i: (i, 0)) for n in widths],
        out_shape=[jax.ShapeDtypeStruct((T, n), BF16) for n in widths],
        compiler_params=_params("parallel"),
        name="in_proj",
    )(h, g, *ws)


def _head_select(x, lane_head, n_heads, rows):
    out = None
    for hh in range(n_heads):
        term = jnp.where(lane_head == hh, x[hh * rows:(hh + 1) * rows], 0.0)
        out = term if out is None else out + term
    return out


def _head_norm_gate(o, gate_in, g):
    n = o.shape[1]
    same_head = (_iota((n, n), 0) >> 6) == (_iota((n, n), 1) >> 6)
    ms = _dot_f32_by_bf16(o * o, same_head.astype(BF16)) * (1.0 / 64.0)
    return o * lax.rsqrt(ms + NORM_EPS) * g * (gate_in * _sigmoid(gate_in))


def _gla_kernel(z_ref, wa_ref, ba_ref, g_ref, o_ref, st_ref, oacc_ref, att_ref, kv_ref, *, tiles_per_seq):
    C = GLA_CHUNK
    tt = z_ref.shape[0]

    @pl.when(pl.program_id(0) % tiles_per_seq == 0)
    def _():
        st_ref[...] = jnp.zeros_like(st_ref)

    q = z_ref[:, 0:128].astype(F32)
    k = z_ref[:, 128:256].astype(F32)
    r = z_ref[:, 512:768].astype(F32)
    pre = _dot_bf16_by_f32(z_ref[:, 768:896], wa_ref[...]) + ba_ref[...]
    log_a = (jnp.minimum(pre, 0.0) - jnp.log(1.0 + jnp.exp(-jnp.abs(pre)))) * (1.0 / GLA_TAU)

    lower = (_iota((C, C), 1) <= _iota((C, C), 0)).astype(BF16)
    b = jnp.concatenate([_dot_bf16_by_f32(lower, log_a[c * C:(c + 1) * C]) for c in range(tt // C)], axis=0)
    b_last = jnp.broadcast_to(b.reshape(tt // C, C, 128)[:, C - 1:C, :], (tt // C, C, 128)).reshape(tt, 128)
    q_dec = (q * (GLA_DK ** -0.5) * jnp.exp(b)).astype(BF16)
    k_dec = (k * jnp.exp(-b)).astype(BF16)
    k_end = (k * jnp.exp(b_last - b)).astype(BF16)
    decay = jnp.exp(b_last)

    qk_head = _iota((C, 128), 1) >> 5
    v_head = _iota((C, 256), 1) >> 6
    causal = _iota((4 * C, C), 1) <= (_iota((4 * C, C), 0) & (C - 1))
    state_mask = (_iota((256, 128), 0) >> 6) == (_iota((256, 128), 1) >> 5)

    for c in range(tt // C):
        sl = slice(c * C, (c + 1) * C)
        q_stack = jnp.concatenate([jnp.where(qk_head == hh, q_dec[sl], 0) for hh in range(GLA_HEADS)], axis=0)
        att_ref[c] = jnp.where(causal, _dot_nt(q_stack, k_dec[sl]), 0.0).astype(BF16)
        kv_ref[c] = jnp.where(state_mask, _dot_tn(z_ref[sl, 256:512], k_end[sl]), 0.0)
    state = st_ref[...]
    for c in range(tt // C):
        sl = slice(c * C, (c + 1) * C)
        intra = _head_select(_dot(att_ref[c], z_ref[sl, 256:512]), v_head, GLA_HEADS, C)
        inter = _dot_nt(q_dec[sl], state.astype(BF16))
        oacc_ref[sl, :] = intra + inter
        state = state * decay[c * C:c * C + 1] + kv_ref[c]
    st_ref[...] = state

    o_ref[...] = _head_norm_gate(oacc_ref[...], r, g_ref[...]).astype(BF16)


def _gla(z, wa, ba, g, seq):
    T = z.shape[0]
    tt = TOK_TILE
    return pl.pallas_call(
        functools.partial(_gla_kernel, tiles_per_seq=seq // tt),
        grid=(T // tt,),
        in_specs=[pl.BlockSpec((tt, GLA_W), lambda i: (i, 0)), _full((128, 128)), _full((1, 128)),
                  _full((1, 256))],
        out_specs=pl.BlockSpec((tt, 256), lambda i: (i, 0)),
        out_shape=jax.ShapeDtypeStruct((T, 256), BF16),
        scratch_shapes=[pltpu.VMEM((256, 128), F32), pltpu.VMEM((tt, 256), F32),
                        pltpu.VMEM((tt // GLA_CHUNK, GLA_HEADS * GLA_CHUNK, GLA_CHUNK), BF16),
                        pltpu.VMEM((tt // GLA_CHUNK, 256, 128), F32)],
        compiler_params=_params("arbitrary"),
        name="gla_mixer",
    )(z, wa, ba, g)


_RET_LOG_GAMMA = [math.log1p(-(2.0 ** (-5.0 - hh))) for hh in range(RET_HEADS)]


def _by_head(head, values):
    out = jnp.full(head.shape, values[-1], F32)
    for hh in range(len(values) - 2, -1, -1):
        out = jnp.where(head == hh, values[hh], out)
    return out


def _ret_kernel(z_ref, cos_ref, sina_ref, sinb_ref, g_ref, o_ref, st_ref, oacc_ref, *, tiles_per_seq):
    C = RET_CHUNK
    tt = z_ref.shape[0]

    @pl.when(pl.program_id(0) % tiles_per_seq == 0)
    def _():
        st_ref[...] = jnp.zeros_like(st_ref)

    cos, sina, sinb = cos_ref[...], sina_ref[...], sinb_ref[...]

    def rotate(t):
        return t * cos + pltpu.roll(t, 112, 1) * sina + pltpu.roll(t, 16, 1) * sinb

    q_rot = rotate(z_ref[:, 0:128].astype(F32))
    k_rot = rotate(z_ref[:, 128:256].astype(F32)) * (RET_DK ** -0.5)
    gate_in = z_ref[:, 512:768].astype(F32)

    qk_head = _iota((C, 128), 1) >> 5
    v_head = _iota((C, 256), 1) >> 6
    lg_qk = _by_head(_iota((1, 128), 1) >> 5, _RET_LOG_GAMMA)
    lg_v = _by_head(_iota((1, 256), 1) >> 6, _RET_LOG_GAMMA)
    pos_qk = _iota((C, 128), 0).astype(F32)
    pos_v = _iota((C, 256), 0).astype(F32)
    key_decay = jnp.exp((C - 1.0 - pos_qk) * lg_qk)
    query_decay = jnp.exp((pos_v + 1.0) * lg_v)
    chunk_decay = jnp.exp(float(C) * lg_qk)
    srow = _iota((4 * C, C), 0)
    rel = ((srow & (C - 1)) - _iota((4 * C, C), 1)).astype(F32)
    decay_mat = jnp.where(rel >= 0, jnp.exp(jnp.maximum(rel, 0.0) * _by_head(srow >> 7, _RET_LOG_GAMMA)), 0.0)
    state_mask = (_iota((256, 128), 0) >> 6) == (_iota((256, 128), 1) >> 5)

    q_bf = q_rot.astype(BF16)
    k_bf = k_rot.astype(BF16)
    state = st_ref[...]
    for c in range(tt // C):
        sl = slice(c * C, (c + 1) * C)
        qc = q_bf[sl]
        v = z_ref[sl, 256:512]
        q_stack = jnp.concatenate([jnp.where(qk_head == hh, qc, 0) for hh in range(RET_HEADS)], axis=0)
        att = _dot_nt(q_stack, k_bf[sl]) * decay_mat
        intra = _head_select(_dot(att.astype(BF16), v), v_head, RET_HEADS, C)
        inter = _dot_nt(qc, state.astype(BF16)) * query_decay
        oacc_ref[sl, :] = intra + inter
        kv = _dot_tn(v, (k_rot[sl] * key_decay).astype(BF16))
        state = state * chunk_decay + jnp.where(state_mask, kv, 0.0)
    st_ref[...] = state

    o_ref[...] = _head_norm_gate(oacc_ref[...], gate_in, g_ref[...]).astype(BF16)


def _retention(z, cos, sina, sinb, g, seq):
    T = z.shape[0]
    tt = TOK_TILE
    tps = seq // tt
    pos_spec = pl.BlockSpec((tt, 128), lambda i: (i % tps, 0))
    return pl.pallas_call(
        functools.partial(_ret_kernel, tiles_per_seq=tps),
        grid=(T // tt,),
        in_specs=[pl.BlockSpec((tt, RET_W), lambda i: (i, 0)), pos_spec, pos_spec, pos_spec, _full((1, 256))],
        out_specs=pl.BlockSpec((tt, 256), lambda i: (i, 0)),
        out_shape=jax.ShapeDtypeStruct((T, 256), BF16),
        scratch_shapes=[pltpu.VMEM((256, 128), F32), pltpu.VMEM((tt, 256), F32)],
        compiler_params=_params("arbitrary"),
        name="retention_mixer",
    )(z, cos, sina, sinb, g)


def _rope_tables(seq):
    half = RET_DK // 2
    pos = jnp.arange(seq, dtype=F32)
    inv_freq = RET_ROPE_BASE ** (-jnp.arange(half, dtype=F32) * 2.0 / RET_DK)
    ang = pos[:, None] * inv_freq[None, :]
    cos = jnp.tile(jnp.cos(ang), (1, LANES // half))
    sin = jnp.tile(jnp.sin(ang), (1, LANES // half))
    first_half = (jnp.arange(LANES) % RET_DK) < half
    return cos, jnp.where(first_half, -sin, 0.0), jnp.where(first_half, 0.0, sin)


def _swa_kernel(sink_ref, cur_ref, prev_ref, o_ref, s_ref, *, tiles_per_seq):
    W = SWA_WINDOW
    HD = SWA_HD
    tt = cur_ref.shape[0]
    first_key = jnp.where(pl.program_id(0) % tiles_per_seq == 0, W, 0)
    sj = _iota((2 * W, W), 0)
    qi = _iota((2 * W, W), 1)
    in_window = (sj > qi) & (sj <= qi + W)
    kv0 = SWA_HEADS * HD
    no_head = jnp.zeros((HD, W), BF16)
    n_blocks = tt // W

    def band(c, col0):
        prev = prev_ref[:, col0:col0 + 128] if c == 0 else cur_ref[(c - 1) * W:c * W, col0:col0 + 128]
        return jnp.concatenate([prev, cur_ref[c * W:(c + 1) * W, col0:col0 + 128]], axis=0)

    for c in range(n_blocks):
        mask = in_window & (sj >= first_key) if c == 0 else in_window
        k_band = band(c, kv0)
        q_t = cur_ref[c * W:(c + 1) * W, 0:kv0].astype(F32).T.astype(BF16)
        for hh in range(SWA_HEADS):
            kk = hh // (SWA_HEADS // SWA_KV_HEADS)
            qh_t = q_t[hh * HD:(hh + 1) * HD]
            q_on_kv = jnp.concatenate([qh_t, no_head] if kk == 0 else [no_head, qh_t], axis=0)
            s = _dot(k_band, q_on_kv) * (HD ** -0.5 * LOG2_E)
            s_ref[c, hh] = jnp.where(mask, s, NEG_INF)
    for c in range(n_blocks):
        v_t = band(c, kv0 + 128).astype(F32).T.astype(BF16)
        outs = []
        for hh in range(SWA_HEADS):
            kk = hh // (SWA_HEADS // SWA_KV_HEADS)
            s = s_ref[c, hh]
            sink = sink_ref[hh] * LOG2_E
            m = jnp.maximum(jnp.max(s, axis=0, keepdims=True), sink)
            p = jnp.exp2(s - m)
            denom = jnp.sum(p, axis=0, keepdims=True) + jnp.exp2(sink - m)
            outs.append(_dot(v_t[kk * HD:(kk + 1) * HD], p.astype(BF16)) / denom)
        o_ref[c * W:(c + 1) * W, :] = jnp.concatenate(outs, axis=0).T.astype(BF16)


def _swa(z, sinks, seq):
    T = z.shape[0]
    tt = TOK_TILE
    per = tt // SWA_WINDOW
    return pl.pallas_call(
        functools.partial(_swa_kernel, tiles_per_seq=seq // tt),
        grid=(T // tt,),
        in_specs=[pl.BlockSpec(memory_space=pltpu.SMEM),
                  pl.BlockSpec((tt, SWA_W), lambda i: (i, 0)),
                  pl.BlockSpec((SWA_WINDOW, SWA_W), lambda i: (jnp.maximum(i * per - 1, 0), 0))],
        out_specs=pl.BlockSpec((tt, 256), lambda i: (i, 0)),
        out_shape=jax.ShapeDtypeStruct((T, 256), BF16),
        scratch_shapes=[pltpu.VMEM((per, SWA_HEADS, 2 * SWA_WINDOW, SWA_WINDOW), F32)],
        compiler_params=_params("parallel"),
        name="swa_mixer",
    )(sinks, z, z)


_KMEAN_BLOCKS = 8


def _kmean_kernel(k_ref, o_ref):
    k = k_ref[...].astype(F32).reshape(_KMEAN_BLOCKS, MOBA_BLOCK, 256)
    o_ref[...] = jnp.mean(k, axis=1)


def _moba_kmean(z):
    T = z.shape[0]
    rows = _KMEAN_BLOCKS * MOBA_BLOCK
    return pl.pallas_call(
        _kmean_kernel,
        grid=(T // rows,),
        in_specs=[pl.BlockSpec((rows, 256), lambda i: (i, 1))],
        out_specs=pl.BlockSpec((_KMEAN_BLOCKS, 256), lambda i: (i, 0)),
        out_shape=jax.ShapeDtypeStruct((T // MOBA_BLOCK, 256), F32),
        compiler_params=_params("parallel"),
        name="moba_kmean",
    )(z)


MOBA_VROWS = 80
MOBA_PREP_BLOCKS = 2


def _moba_prep_kernel(z_ref, km_ref, qt_out, k_out, vt_out):
    HD = MOBA_HD
    tq = MOBA_BLOCK
    nb = km_ref.shape[0]
    blk_i = _iota((nb, tq), 0)
    blk = blk_i.astype(F32)
    pad = LANES - HD - nb
    ones_row = (_iota((MOBA_VROWS - HD, tq), 0) == 0).astype(BF16)
    for sb in range(MOBA_PREP_BLOCKS):
        rows = slice(sb * tq, (sb + 1) * tq)
        qb = (pl.program_id(0) * MOBA_PREP_BLOCKS + sb) % nb
        this_block_lanes = (_iota((tq, nb), 1) == qb).astype(BF16)
        for hh in range(MOBA_HEADS):
            cols = slice(hh * HD, (hh + 1) * HD)
            q_t = z_ref[rows, cols].astype(F32).T
            gate = _dot(km_ref[:, cols], q_t, precision=HIGHEST)
            gate = jnp.where(blk_i < qb, gate, NEG_INF)
            keep = jnp.where(blk_i == qb, 1.0, 0.0)
            for _ in range(MOBA_TOPK):
                best = jnp.max(gate, axis=0, keepdims=True)
                first = jnp.min(jnp.where(gate == best, blk, float(nb)), axis=0, keepdims=True)
                hit = blk == first
                keep = keep + jnp.where(hit, jnp.where(best > 0.5 * NEG_INF, 1.0, 0.0), 0.0)
                gate = jnp.where(hit, BELOW_NEG_INF, gate)
            bias = ((keep - 1.0) * (-NEG_INF)).astype(BF16)
            qt_out[hh, sb] = jnp.concatenate(
                [(q_t * (HD ** -0.5 * LOG2_E)).astype(BF16), bias, jnp.zeros((pad, tq), BF16)], axis=0)
            k_out[hh, rows, :] = jnp.concatenate(
                [z_ref[rows, 256 + hh * HD:256 + (hh + 1) * HD], this_block_lanes, jnp.zeros((tq, pad), BF16)],
                axis=1)
            v_t = z_ref[rows, 512 + hh * HD:512 + (hh + 1) * HD].astype(F32).T.astype(BF16)
            vt_out[hh, sb] = jnp.concatenate([v_t, ones_row], axis=0)


def _moba_prep(z, kmean, seq):
    T = z.shape[0]
    nb = seq // MOBA_BLOCK
    assert MOBA_HD + nb <= LANES and nb % MOBA_PREP_BLOCKS == 0
    tq = MOBA_BLOCK
    tile = MOBA_PREP_BLOCKS * tq
    H = MOBA_HEADS

    def per_block_t(rows):
        return pl.BlockSpec((H, MOBA_PREP_BLOCKS, rows, tq), lambda i: (0, i, 0, 0))

    return pl.pallas_call(
        _moba_prep_kernel,
        grid=(T // tile,),
        in_specs=[pl.BlockSpec((tile, MOBA_W), lambda i: (i, 0)),
                  pl.BlockSpec((nb, 256), lambda i: (i * MOBA_PREP_BLOCKS // nb, 0))],
        out_specs=[per_block_t(LANES), pl.BlockSpec((H, tile, LANES), lambda i: (0, i, 0)),
                   per_block_t(MOBA_VROWS)],
        out_shape=[jax.ShapeDtypeStruct((H, T // tq, LANES, tq), BF16), jax.ShapeDtypeStruct((H, T, LANES), BF16),
                   jax.ShapeDtypeStruct((H, T // tq, MOBA_VROWS, tq), BF16)],
        compiler_params=_params("parallel"),
        name="moba_prep",
    )(z, kmean)


MOBA_QBLOCKS = 4
MOBA_KBLOCKS = 4
MOBA_DIAGONALS = MOBA_QBLOCKS // MOBA_KBLOCKS


def _moba_kernel(qidx_ref, kidx_ref, kind_ref, qt_ref, k_ref, vt_ref, o_ref, m_ref, acc_ref, s_ref, smax_ref):
    HD = MOBA_HD
    B = MOBA_BLOCK
    step = pl.program_id(1)
    kind = kind_ref[step]
    is_last = (step + 1 == pl.num_programs(1)) | (qidx_ref[jnp.minimum(step + 1, pl.num_programs(1) - 1)]
                                                  != qidx_ref[step])

    def attend(diagonal):
        causal = _iota((B, B), 0) <= _iota((B, B), 1)

        def key_blocks_of(qi):
            if diagonal is None:
                return list(range(MOBA_KBLOCKS))
            return [c for c in range(MOBA_KBLOCKS) if MOBA_KBLOCKS * diagonal + c <= qi]

        for hh in range(MOBA_HEADS):
            for qi in range(MOBA_QBLOCKS):
                for c in key_blocks_of(qi):
                    s = _dot(k_ref[hh, c * B:(c + 1) * B, :], qt_ref[hh, qi])
                    if diagonal is not None and MOBA_KBLOCKS * diagonal + c == qi:
                        s = jnp.where(causal, s, NEG_INF)
                    s_ref[hh, qi, c] = s
                    smax_ref[hh, qi, c] = jnp.max(s, axis=0, keepdims=True)
        first_visit = diagonal == 0
        for hh in range(MOBA_HEADS):
            for qi in range(MOBA_QBLOCKS):
                key_blocks = key_blocks_of(qi)
                if not key_blocks:
                    continue
                m_new = functools.reduce(jnp.maximum, [smax_ref[hh, qi, c] for c in key_blocks])
                if not first_visit:
                    m_old = m_ref[hh, qi]
                    m_new = jnp.maximum(m_old, m_new)
                pv = None
                for c in key_blocks:
                    p = jnp.exp2(s_ref[hh, qi, c] - m_new).astype(BF16)
                    term = _dot(vt_ref[hh, c], p)
                    pv = term if pv is None else pv + term
                acc_ref[hh, qi] = pv if first_visit else jnp.exp2(m_old - m_new) * acc_ref[hh, qi] + pv
                m_ref[hh, qi] = m_new

    @pl.when(kind == 0)
    def _():
        attend(None)

    for d in range(MOBA_DIAGONALS):
        @pl.when(kind == 1 + d)
        def _(d=d):
            attend(d)

    @pl.when(is_last)
    def _():
        for qi in range(MOBA_QBLOCKS):
            heads = [acc_ref[hh, qi, 0:HD, :] / acc_ref[hh, qi, HD:HD + 1, :] for hh in range(MOBA_HEADS)]
            o_ref[qi * B:(qi + 1) * B, :] = jnp.concatenate(heads, axis=0).T.astype(BF16)


def _moba(z, seq):
    T = z.shape[0]
    nb = seq // MOBA_BLOCK
    assert nb % MOBA_QBLOCKS == 0
    q_tiles = nb // MOBA_QBLOCKS
    k_steps = nb // MOBA_KBLOCKS
    D = MOBA_DIAGONALS
    batch = T // seq
    qt_aug, k_aug, vt_aug = _moba_prep(z, _moba_kmean(z), seq)
    qidx = np.concatenate([np.full(D * t + D, t) for t in range(q_tiles)]).astype(np.int32)
    kidx = np.concatenate([np.concatenate([D * t + np.arange(D), np.arange(D * t)])
                           for t in range(q_tiles)]).astype(np.int32)
    kind = np.concatenate([np.concatenate([1 + np.arange(D), np.zeros(D * t, np.int64)])
                           for t in range(q_tiles)]).astype(np.int32)
    tq = MOBA_QBLOCKS * MOBA_BLOCK
    tk = MOBA_KBLOCKS * MOBA_BLOCK
    H = MOBA_HEADS
    grid_spec = pltpu.PrefetchScalarGridSpec(
        num_scalar_prefetch=3,
        grid=(batch, len(qidx)),
        in_specs=[pl.BlockSpec((H, MOBA_QBLOCKS, LANES, MOBA_BLOCK),
                               lambda b, s, qi, ki, kd: (0, b * q_tiles + qi[s], 0, 0)),
                  pl.BlockSpec((H, tk, LANES), lambda b, s, qi, ki, kd: (0, b * k_steps + ki[s], 0)),
                  pl.BlockSpec((H, MOBA_KBLOCKS, MOBA_VROWS, MOBA_BLOCK),
                               lambda b, s, qi, ki, kd: (0, b * k_steps + ki[s], 0, 0))],
        out_specs=pl.BlockSpec((tq, 256), lambda b, s, qi, ki, kd: (b * q_tiles + qi[s], 0)),
        scratch_shapes=[pltpu.VMEM((H, MOBA_QBLOCKS, 1, MOBA_BLOCK), F32),
                        pltpu.VMEM((H, MOBA_QBLOCKS, MOBA_VROWS, MOBA_BLOCK), F32),
                        pltpu.VMEM((H, MOBA_QBLOCKS, MOBA_KBLOCKS, MOBA_BLOCK, MOBA_BLOCK), F32),
                        pltpu.VMEM((H, MOBA_QBLOCKS, MOBA_KBLOCKS, 1, MOBA_BLOCK), F32)],
    )
    return pl.pallas_call(
        _moba_kernel,
        grid_spec=grid_spec,
        out_shape=jax.ShapeDtypeStruct((T, 256), BF16),
        compiler_params=_params("parallel", "arbitrary"),
        name="moba_mixer",
    )(jnp.asarray(qidx), jnp.asarray(kidx), jnp.asarray(kind), qt_aug, k_aug, vt_aug)


def _merge_kernel(h_ref, o0, o1, o2, o3, half_gates_ref, half_wb_ref, wout_ref, out_ref):
    merged = None
    for i, o in enumerate((o0, o1, o2, o3)):
        half_y = _dot(o[...], half_wb_ref[i])
        term = half_y * jnp.tanh(half_gates_ref[:, i * D_MODEL:(i + 1) * D_MODEL].astype(F32)) + half_y
        merged = term if merged is None else merged + term
    out_ref[...] = h_ref[...] + _dot(merged.astype(BF16), wout_ref[...])


def _merge(h, branches, gates, wb, wout):
    T = h.shape[0]
    tm = TOK_TILE
    return pl.pallas_call(
        _merge_kernel,
        grid=(T // tm,),
        in_specs=[pl.BlockSpec((tm, D_MODEL), lambda i: (i, 0))]
        + [pl.BlockSpec((tm, 256), lambda i: (i, 0))] * 4
        + [pl.BlockSpec((tm, GATE_W), lambda i: (i, 0)),
           pl.BlockSpec((N_BRANCH, 256, D_MODEL), lambda i: (0, 0, 0), pipeline_mode=pl.Buffered(1)),
           pl.BlockSpec((D_MODEL, D_MODEL), lambda i: (0, 0), pipeline_mode=pl.Buffered(1))],
        out_specs=pl.BlockSpec((tm, D_MODEL), lambda i: (i, 0)),
        out_shape=jax.ShapeDtypeStruct((T, D_MODEL), F32),
        compiler_params=_params("parallel"),
        name="merge_out_proj",
    )(h, *branches, gates, wb, wout)


ROW_TILES = D_MODEL // (2 * LANES)
HIGH_HALF_MASK = -65536


def _slab_columns(j):
    first = 2 * j * LANES
    return slice(first, first + LANES), slice(first + LANES, first + 2 * LANES)


def _row_tile_spec(rows, index_map):
    return pl.BlockSpec((rows * ROW_TILES, LANES), index_map)


def _bf16_bits(x):
    return lax.bitcast_convert_type(x.astype(BF16).astype(F32), I32)


def _pack_rows(ref, x):
    for j in range(ROW_TILES):
        low_cols, high_cols = _slab_columns(j)
        low = _bf16_bits(x[:, low_cols])
        high = _bf16_bits(x[:, high_cols])
        ref[pl.ds(j, x.shape[0], stride=ROW_TILES), :] = high | lax.shift_right_logical(low, 16)


def _unpack_slab(ref, j):
    words = ref[pl.ds(j, ref.shape[0] // ROW_TILES, stride=ROW_TILES), :]
    low = lax.bitcast_convert_type(lax.shift_left(words, 16), F32)
    high = lax.bitcast_convert_type(words & HIGH_HALF_MASK, F32)
    return low, high


def _unpack_rows(ref):
    pieces = [piece for j in range(ROW_TILES) for piece in _unpack_slab(ref, j)]
    return jnp.concatenate(pieces, axis=1).astype(BF16)


def _router_kernel(h_ref, g_ref, wr_ref, br_ref, u_ref, route_ref, route_t_ref, count_ref, carry_ref):
    tm = h_ref.shape[0]

    @pl.when(pl.program_id(0) == 0)
    def _():
        carry_ref[...] = jnp.zeros_like(carry_ref)

    u = _rms(h_ref[...], g_ref[...])
    _pack_rows(u_ref, u)
    u_hi, u_lo = _split_bf16(u, 2)
    w_hi, w_lo = _split_bf16(wr_ref[...], 2)
    logits = _dot(u_hi, w_hi) + (_dot(u_lo, w_hi) + _dot(u_hi, w_lo)) + br_ref[...]
    lt = logits.T
    row_i = _iota((LANES, tm), 0)
    row = row_i.astype(F32)
    row_group = ((row_i - EXPERT_LANE0) >> 3).astype(F32)

    def first_row(mask):
        return jnp.min(jnp.where(mask, row, float(LANES)), axis=0, keepdims=True)

    grp = jnp.where(row_i < N_GROUPS, lt, BELOW_NEG_INF)
    grp_e = jnp.exp(grp - jnp.max(grp, axis=0, keepdims=True))
    p_group = grp_e / jnp.sum(grp_e, axis=0, keepdims=True)
    g_w = jnp.max(p_group, axis=0, keepdims=True)
    g_sel = first_row(p_group == g_w)

    in_group = (row_i >= EXPERT_LANE0) & (row_group == g_sel)
    el = jnp.where(in_group, lt, BELOW_NEG_INF)
    top1 = jnp.max(el, axis=0, keepdims=True)
    row1 = first_row(el == top1)
    el = jnp.where(row == row1, BELOW_NEG_INF, el)
    top2 = jnp.max(el, axis=0, keepdims=True)
    row2 = first_row(el == top2)
    e2 = jnp.exp(top2 - top1)
    w1 = g_w / (1.0 + e2)
    w2 = g_w * e2 / (1.0 + e2)

    onehot = ((row == row1) | (row == row2)).astype(BF16)
    earlier = (_iota((tm, tm), 0) < _iota((tm, tm), 1)).astype(BF16)
    seen = _dot(onehot, earlier) + carry_ref[:, 0:1]
    rank1 = jnp.sum(jnp.where(row == row1, seen, 0.0), axis=0, keepdims=True)
    rank2 = jnp.sum(jnp.where(row == row2, seen, 0.0), axis=0, keepdims=True)
    carry_ref[...] += jnp.sum(onehot.astype(F32), axis=1, keepdims=True)
    count_ref[...] = carry_ref[...]

    out_t = jnp.zeros((LANES, tm), F32)
    for idx, val in enumerate((row1 - EXPERT_LANE0, row2 - EXPERT_LANE0, w1, w2, rank1, rank2)):
        out_t = jnp.where(row_i == idx, val, out_t)
    route_t_ref[...] = out_t[0:8, :]
    route_ref[...] = out_t.T


def _router(h, g, wr, br):
    T = h.shape[0]
    tm = TOK_TILE
    return pl.pallas_call(
        _router_kernel,
        grid=(T // tm,),
        in_specs=[pl.BlockSpec((tm, D_MODEL), lambda i: (i, 0)), _full((1, D_MODEL)),
                  _full((D_MODEL, LANES)), _full((1, LANES))],
        out_specs=[_row_tile_spec(tm, lambda i: (i, 0)), pl.BlockSpec((tm, LANES), lambda i: (i, 0)),
                   pl.BlockSpec((8, tm), lambda i: (0, i)), _full((LANES, LANES))],
        out_shape=[jax.ShapeDtypeStruct((T * ROW_TILES, LANES), I32), jax.ShapeDtypeStruct((T, LANES), F32),
                   jax.ShapeDtypeStruct((8, T), F32), jax.ShapeDtypeStruct((LANES, LANES), F32)],
        scratch_shapes=[pltpu.VMEM((LANES, LANES), F32)],
        compiler_params=_params("arbitrary"),
        name="moe_router",
    )(h, g, wr, br)


ROW_UNROLL = 8


def _row_copy(src_ref, src_row, dst_ref, dst_row, sem):
    src = src_ref.at[pl.ds(pl.multiple_of(src_row * ROW_TILES, ROW_TILES), ROW_TILES)]
    dst = dst_ref.at[pl.ds(pl.multiple_of(dst_row * ROW_TILES, ROW_TILES), ROW_TILES)]
    return pltpu.make_async_copy(src, dst, sem)


def _for_each_row_pair(rows, body):
    def group(g, carry):
        for i in range(ROW_UNROLL):
            for k in range(EXPERT_TOPK):
                body(g * ROW_UNROLL + i, k)
        return carry

    lax.fori_loop(0, rows // ROW_UNROLL, group, 0)


def _dispatch_kernel(dest0_ref, dest1_ref, pstart_ref, pend_ref, u_ref, xs_ref, zero_ref, sem):
    tm = u_ref.shape[0] // ROW_TILES
    tile = pl.program_id(0)
    dests = (dest0_ref, dest1_ref)

    @pl.when(tile == 0)
    def _():
        zero_ref[...] = jnp.zeros_like(zero_ref)

        def zero_block(first_row):
            start = pl.multiple_of(first_row * ROW_TILES, ROW_TILES)
            fill = pltpu.make_async_copy(zero_ref, xs_ref.at[pl.ds(start, MOE_BLOCK * ROW_TILES)], sem)
            fill.start()
            fill.wait()

        def zero_tail(e, carry):
            @pl.when(pend_ref[e] > pstart_ref[e])
            def _():
                zero_block(pend_ref[e] - MOE_BLOCK)
            return carry

        def zero_unused(b, carry):
            zero_block(b * MOE_BLOCK)
            return carry

        lax.fori_loop(0, N_EXPERTS, zero_tail, 0)
        lax.fori_loop(pend_ref[N_EXPERTS - 1] // MOE_BLOCK, xs_ref.shape[0] // (MOE_BLOCK * ROW_TILES),
                      zero_unused, 0)

    def issue(r, k):
        _row_copy(u_ref, r, xs_ref, dests[k][tile * tm + r], sem).start(priority=k)

    _for_each_row_pair(tm, issue)
    for _ in range(EXPERT_TOPK):
        pltpu.make_async_copy(u_ref, xs_ref.at[pl.ds(0, tm * ROW_TILES)], sem).wait()


def _dispatch(dest0, dest1, pstart, pend, u, n_slots):
    T = u.shape[0] // ROW_TILES
    tm = DISPATCH_TILE
    grid_spec = pltpu.PrefetchScalarGridSpec(
        num_scalar_prefetch=4,
        grid=(T // tm,),
        in_specs=[_row_tile_spec(tm, lambda i, *_: (i, 0))],
        out_specs=pl.BlockSpec(memory_space=pl.ANY),
        scratch_shapes=[pltpu.VMEM((MOE_BLOCK * ROW_TILES, LANES), I32), pltpu.SemaphoreType.DMA(())],
    )
    return pl.pallas_call(
        _dispatch_kernel,
        grid_spec=grid_spec,
        out_shape=jax.ShapeDtypeStruct((n_slots * ROW_TILES, LANES), I32),
        compiler_params=_params("arbitrary"),
        name="moe_dispatch",
    )(dest0, dest1, pstart, pend, u)


def _expert_kernel(blk_e_ref, n_used_ref, x_ref, w1_ref, w3_ref, w2_ref, y_ref, w1b_ref, w3b_ref, w2b_ref):
    i = pl.program_id(0)
    used = i < n_used_ref[0]

    @pl.when((i == 0) | (blk_e_ref[i] != blk_e_ref[jnp.maximum(i - 1, 0)]))
    def _():
        w1b_ref[...] = w1_ref[0, 0].astype(BF16)
        w3b_ref[...] = w3_ref[0, 0].astype(BF16)
        w2b_ref[...] = w2_ref[0, 0].astype(BF16)

    @pl.when(used)
    def _():
        x = _unpack_rows(x_ref)
        a = _dot(x, w1b_ref[...])
        hidden = (a * _sigmoid(a)) * _dot(x, w3b_ref[...])
        _pack_rows(y_ref, _dot(hidden.astype(BF16), w2b_ref[...]))

    @pl.when(jnp.logical_not(used))
    def _():
        y_ref[...] = jnp.zeros_like(y_ref)


def _experts(blk_e, n_used, xs, w1, w3, w2, layer):
    n_slots = xs.shape[0] // ROW_TILES
    bm = MOE_BLOCK
    grid_spec = pltpu.PrefetchScalarGridSpec(
        num_scalar_prefetch=2,
        grid=(n_slots // bm,),
        in_specs=[_row_tile_spec(bm, lambda i, be, nu: (jnp.minimum(i, nu[0] - 1), 0)),
                  pl.BlockSpec((1, 1, D_MODEL, EXPERT_HIDDEN), lambda i, be, nu: (layer, be[i], 0, 0)),
                  pl.BlockSpec((1, 1, D_MODEL, EXPERT_HIDDEN), lambda i, be, nu: (layer, be[i], 0, 0)),
                  pl.BlockSpec((1, 1, EXPERT_HIDDEN, D_MODEL), lambda i, be, nu: (layer, be[i], 0, 0))],
        out_specs=_row_tile_spec(bm, lambda i, be, nu: (i, 0)),
        scratch_shapes=[pltpu.VMEM((D_MODEL, EXPERT_HIDDEN), BF16), pltpu.VMEM((D_MODEL, EXPERT_HIDDEN), BF16),
                        pltpu.VMEM((EXPERT_HIDDEN, D_MODEL), BF16)],
    )
    return pl.pallas_call(
        _expert_kernel,
        grid_spec=grid_spec,
        out_shape=jax.ShapeDtypeStruct((n_slots * ROW_TILES, LANES), I32),
        compiler_params=_params("arbitrary"),
        name="moe_experts",
    )(blk_e, n_used, xs, w1, w3, w2)


def _combine_kernel(dest0_ref, dest1_ref, h_ref, route_ref, g_ref, ys_ref, out_ref, y_ref, sem, *, normalize):
    tm = h_ref.shape[0]
    step = pl.program_id(0)
    slot = step % 2

    dests = (dest0_ref, dest1_ref)

    def gather_tile(tile, buf):
        def issue(r, k):
            _row_copy(ys_ref, dests[k][tile * tm + r], y_ref.at[buf, k], r, sem.at[buf]).start(priority=k)

        _for_each_row_pair(tm, issue)

    @pl.when(step == 0)
    def _():
        gather_tile(0, 0)

    @pl.when(step + 1 < pl.num_programs(0))
    def _():
        gather_tile(step + 1, 1 - slot)

    for k in range(EXPERT_TOPK):
        pltpu.make_async_copy(ys_ref.at[pl.ds(0, tm * ROW_TILES)], y_ref.at[slot, k], sem.at[slot]).wait()
    w0 = route_ref[:, 2:3]
    w1 = route_ref[:, 3:4]
    for j in range(ROW_TILES):
        low0, high0 = _unpack_slab(y_ref.at[slot, 0], j)
        low1, high1 = _unpack_slab(y_ref.at[slot, 1], j)
        low_cols, high_cols = _slab_columns(j)
        out_ref[:, low_cols] = h_ref[:, low_cols] + w0 * low0 + w1 * low1
        out_ref[:, high_cols] = h_ref[:, high_cols] + w0 * high0 + w1 * high1
    if normalize:
        out_ref[...] = _rms(out_ref[...], g_ref[...])


def _combine(dest0, dest1, h, route, ys, final_g, normalize):
    T = h.shape[0]
    tm = ROW_TILE
    grid_spec = pltpu.PrefetchScalarGridSpec(
        num_scalar_prefetch=2,
        grid=(T // tm,),
        in_specs=[pl.BlockSpec((tm, D_MODEL), lambda i, *_: (i, 0)),
                  pl.BlockSpec((tm, LANES), lambda i, *_: (i, 0)),
                  _full((1, D_MODEL)),
                  pl.BlockSpec(memory_space=pl.ANY)],
        out_specs=pl.BlockSpec((tm, D_MODEL), lambda i, *_: (i, 0)),
        scratch_shapes=[pltpu.VMEM((2, EXPERT_TOPK, tm * ROW_TILES, LANES), I32),
                        pltpu.SemaphoreType.DMA((2,))],
    )
    return pl.pallas_call(
        functools.partial(_combine_kernel, normalize=normalize),
        grid_spec=grid_spec,
        out_shape=jax.ShapeDtypeStruct((T, D_MODEL), F32),
        compiler_params=_params("arbitrary"),
        name="moe_combine",
    )(dest0, dest1, h, route, final_g, ys)


def _moe(h, g, w_group, b_group, w_expert, b_expert, w1, w3, w2, layer, final_g, normalize):
    T = h.shape[0]
    wr = jnp.zeros((D_MODEL, LANES), F32)
    wr = wr.at[:, :N_GROUPS].set(w_group).at[:, EXPERT_LANE0:EXPERT_LANE0 + N_EXPERTS].set(w_expert)
    br = jnp.zeros((1, LANES), F32)
    br = br.at[0, :N_GROUPS].set(b_group).at[0, EXPERT_LANE0:EXPERT_LANE0 + N_EXPERTS].set(b_expert)
    u, route, route_t, count = _router(h, g, wr, br)

    n_blocks = (T * EXPERT_TOPK) // MOE_BLOCK + N_EXPERTS
    counts = count[EXPERT_LANE0:EXPERT_LANE0 + N_EXPERTS, 0].astype(I32)
    padded = ((counts + MOE_BLOCK - 1) // MOE_BLOCK) * MOE_BLOCK
    pend = jnp.cumsum(padded)
    pstart = pend - padded
    expert_ids = jnp.arange(N_EXPERTS, dtype=F32)[:, None]

    def slot_of(expert_row, rank_row):
        segment_start = jnp.sum(jnp.where(expert_row[None, :] == expert_ids, pstart[:, None], 0), axis=0)
        return segment_start + rank_row.astype(I32)

    dest0 = slot_of(route_t[0], route_t[4])
    dest1 = slot_of(route_t[1], route_t[5])
    block_row0 = jnp.arange(n_blocks, dtype=I32) * MOE_BLOCK
    blk_e = jnp.minimum(jnp.sum((pend[None, :] <= block_row0[:, None]).astype(I32), axis=1), N_EXPERTS - 1)
    n_used = (pend[-1:] // MOE_BLOCK).astype(I32)

    xs = _dispatch(dest0, dest1, pstart, pend, u, n_blocks * MOE_BLOCK)
    ys = _experts(blk_e, n_used, xs, w1, w3, w2, layer)
    return _combine(dest0, dest1, h, route, ys, final_g, normalize)


def _split_w_in(w_in):
    gla_n = GLA_HEADS * (2 * GLA_DK + 2 * GLA_DV) + GLA_LOWRANK
    swa_n = (SWA_HEADS + 2 * SWA_KV_HEADS) * SWA_HD
    moba_n = 3 * MOBA_HEADS * MOBA_HD
    ret_n = RET_HEADS * (2 * RET_DK + 2 * RET_DV)
    offs = np.cumsum([0, gla_n, swa_n, moba_n, ret_n, GATE_W])
    parts = [w_in[:, offs[i]:offs[i + 1]].astype(BF16) for i in range(5)]
    parts[0] = jnp.pad(parts[0], ((0, 0), (0, GLA_W - gla_n)))
    parts[4] = parts[4] * 0.5
    return parts


def _token_mixer(h, seq, ln1_g, w_in, gla_w_a2, gla_b_a, gla_norm_g, swa_sinks, ret_norm_g, w_branch, w_out,
                 rope):
    z_gla, z_swa, z_moba, z_ret, half_gates = _in_proj(h, ln1_g.reshape(1, D_MODEL), _split_w_in(w_in))
    wa = jnp.zeros((128, 128), F32).at[:GLA_LOWRANK].set(gla_w_a2)
    o_gla = _gla(z_gla, wa, gla_b_a.reshape(1, 128), jnp.tile(gla_norm_g, GLA_HEADS).reshape(1, 256), seq)
    o_swa = _swa(z_swa, swa_sinks, seq)
    o_moba = _moba(z_moba, seq)
    o_ret = _retention(z_ret, *rope, jnp.tile(ret_norm_g, RET_HEADS).reshape(1, 256), seq)
    half_w_branch = w_branch.astype(BF16) * 0.5
    return _merge(h, (o_gla, o_swa, o_moba, o_ret), half_gates, half_w_branch, w_out.astype(BF16))


def kernel(x, ln1_g, w_in, gla_w_a2, gla_b_a, gla_norm_g, swa_sinks, ret_norm_g, w_branch, w_out, ln2_g,
           w_group, b_group, w_expert, b_expert, w1, w3, w2, final_g):
    batch, seq, _ = x.shape
    depth = w_in.shape[0]
    assert depth >= 1
    assert x.shape[2] == D_MODEL and seq % (MOBA_QBLOCKS * MOBA_BLOCK) == 0 and seq % TOK_TILE == 0
    assert (batch * seq) % DISPATCH_TILE == 0 and (batch * seq * EXPERT_TOPK) % MOE_BLOCK == 0
    rope = _rope_tables(seq)
    h = x.reshape(batch * seq, D_MODEL)
    for l in range(depth):
        h = _token_mixer(h, seq, ln1_g[l], w_in[l], gla_w_a2[l], gla_b_a[l], gla_norm_g[l], swa_sinks[l],
                         ret_norm_g[l], w_branch[l], w_out[l], rope)
        h = _moe(h, ln2_g[l].reshape(1, D_MODEL), w_group[l], b_group[l], w_expert[l], b_expert[l],
                 w1, w3, w2, l, final_g.reshape(1, D_MODEL), normalize=(l == depth - 1))
    return h.reshape(batch, seq, D_MODEL)
```

```python
import functools
import math

import numpy as np
import jax
import jax.numpy as jnp
from jax import lax
from jax.experimental import pallas as pl
from jax.experimental.pallas import tpu as pltpu

F32 = jnp.float32
BF16 = jnp.bfloat16
I32 = jnp.int32
HIGHEST = lax.Precision.HIGHEST

D_MODEL = 1024
N_BRANCH = 4
NORM_EPS = 1e-6
NEG_INF = -1e30
BELOW_NEG_INF = -3e38
LOG2_E = math.log2(math.e)

GLA_HEADS, GLA_DK, GLA_DV, GLA_LOWRANK, GLA_TAU, GLA_CHUNK = 4, 32, 64, 16, 16.0, 64
SWA_HEADS, SWA_KV_HEADS, SWA_HD, SWA_WINDOW = 4, 2, 64, 128
MOBA_HEADS, MOBA_HD, MOBA_BLOCK, MOBA_TOPK = 4, 64, 256, 3
RET_HEADS, RET_DK, RET_DV, RET_CHUNK, RET_ROPE_BASE = 4, 32, 64, 128, 10000.0
N_GROUPS, EXPERTS_PER_GROUP, EXPERT_TOPK, EXPERT_HIDDEN = 4, 8, 2, 256
MOE_BLOCK = 512
N_EXPERTS = N_GROUPS * EXPERTS_PER_GROUP

GLA_W = 896
SWA_W = 512
MOBA_W = 768
RET_W = 768
GATE_W = N_BRANCH * D_MODEL

LANES = 128
EXPERT_LANE0 = 32

VMEM_LIMIT = 56 * 1024 * 1024

TOK_TILE = 512
MIXER_TILE = 1024
ROW_TILE = 512
DISPATCH_TILE = 2048


def _params(*sem):
    return pltpu.CompilerParams(dimension_semantics=sem, vmem_limit_bytes=VMEM_LIMIT)


def _dot(a, b, precision=None):
    return jnp.dot(a, b, preferred_element_type=F32, precision=precision)


def _dot_nt(a, b, precision=None):
    return lax.dot_general(a, b, (((1,), (1,)), ((), ())), preferred_element_type=F32, precision=precision)


def _dot_tn(a, b, precision=None):
    return lax.dot_general(a, b, (((0,), (0,)), ((), ())), preferred_element_type=F32, precision=precision)


def _split_bf16(x, parts):
    out = []
    for _ in range(parts - 1):
        piece = x.astype(BF16)
        out.append(piece)
        x = x - piece.astype(F32)
    out.append(x.astype(BF16))
    return out


def _dot_f32_by_bf16(x, m):
    return sum(_dot(piece, m) for piece in _split_bf16(x, 3))


def _dot_bf16_by_f32(m, x):
    return sum(_dot(m, piece) for piece in _split_bf16(x, 3))


def _iota(shape, dim):
    return lax.broadcasted_iota(I32, shape, dim)


def _sigmoid(x):
    return 0.5 * jnp.tanh(0.5 * x) + 0.5


def _rms(x, g):
    ms = jnp.mean(x * x, axis=-1, keepdims=True)
    return x * lax.rsqrt(ms + NORM_EPS) * g


def _full(shape):
    return pl.BlockSpec(shape, lambda *_: (0,) * len(shape))


def _in_proj_kernel(h_ref, g_ref, w0, w1, w2, w3, w4, o0, o1, o2, o3, o4):
    u = _rms(h_ref[...], g_ref[...]).astype(BF16)
    for w, o in ((w0, o0), (w1, o1), (w2, o2), (w3, o3), (w4, o4)):
        o[...] = _dot(u, w[...]).astype(BF16)


def _in_proj(h, g, ws):
    T = h.shape[0]
    tm = TOK_TILE
    widths = [w.shape[1] for w in ws]
    return pl.pallas_call(
        _in_proj_kernel,
        grid=(T // tm,),
        in_specs=[pl.BlockSpec((tm, D_MODEL), lambda i: (i, 0)), _full((1, D_MODEL))]
        + [pl.BlockSpec((D_MODEL, n), lambda i: (0, 0), pipeline_mode=pl.Buffered(1)) for n in widths],
        out_specs=[pl.BlockSpec((tm, n), lambda i: (i, 0)) for n in widths],
        out_shape=[jax.ShapeDtypeStruct((T, n), BF16) for n in widths],
        compiler_params=_params("parallel"),
        name="in_proj",
    )(h, g, *ws)


def _head_select(x, lane_head, n_heads, rows):
    out = None
    for hh in range(n_heads):
        term = jnp.where(lane_head == hh, x[hh * rows:(hh + 1) * rows], 0.0)
        out = term if out is None else out + term
    return out


def _head_norm_gate(o, gate_in, g):
    n = o.shape[1]
    same_head = (_iota((n, n), 0) >> 6) == (_iota((n, n), 1) >> 6)
    ms = _dot_f32_by_bf16(o * o, same_head.astype(BF16)) * (1.0 / 64.0)
    return o * lax.rsqrt(ms + NORM_EPS) * g * (gate_in * _sigmoid(gate_in))


def _gla_kernel(z_ref, wa_ref, ba_ref, g_ref, o_ref, st_ref, oacc_ref, att_ref, kv_ref, *, tiles_per_seq):
    C = GLA_CHUNK
    tt = z_ref.shape[0]

    @pl.when(pl.program_id(0) % tiles_per_seq == 0)
    def _():
        st_ref[...] = jnp.zeros_like(st_ref)

    q = z_ref[:, 0:128].astype(F32)
    k = z_ref[:, 128:256].astype(F32)
    r = z_ref[:, 512:768].astype(F32)
    pre = _dot_bf16_by_f32(z_ref[:, 768:896], wa_ref[...]) + ba_ref[...]
    log_a = (jnp.minimum(pre, 0.0) - jnp.log(1.0 + jnp.exp(-jnp.abs(pre)))) * (1.0 / GLA_TAU)

    lower = (_iota((C, C), 1) <= _iota((C, C), 0)).astype(BF16)
    b = jnp.concatenate([_dot_bf16_by_f32(lower, log_a[c * C:(c + 1) * C]) for c in range(tt // C)], axis=0)
    b_last = jnp.broadcast_to(b.reshape(tt // C, C, 128)[:, C - 1:C, :], (tt // C, C, 128)).reshape(tt, 128)
    q_dec = (q * (GLA_DK ** -0.5) * jnp.exp(b)).astype(BF16)
    k_dec = (k * jnp.exp(-b)).astype(BF16)
    k_end = (k * jnp.exp(b_last - b)).astype(BF16)
    decay = jnp.exp(b_last)

    qk_head = _iota((C, 128), 1) >> 5
    v_head = _iota((C, 256), 1) >> 6
    causal = _iota((4 * C, C), 1) <= (_iota((4 * C, C), 0) & (C - 1))
    state_mask = (_iota((256, 128), 0) >> 6) == (_iota((256, 128), 1) >> 5)

    for c in range(tt // C):
        sl = slice(c * C, (c + 1) * C)
        q_stack = jnp.concatenate([jnp.where(qk_head == hh, q_dec[sl], 0) for hh in range(GLA_HEADS)], axis=0)
        att_ref[c] = jnp.where(causal, _dot_nt(q_stack, k_dec[sl]), 0.0).astype(BF16)
        kv_ref[c] = jnp.where(state_mask, _dot_tn(z_ref[sl, 256:512], k_end[sl]), 0.0)
    state = st_ref[...]
    for c in range(tt // C):
        sl = slice(c * C, (c + 1) * C)
        intra = _head_select(_dot(att_ref[c], z_ref[sl, 256:512]), v_head, GLA_HEADS, C)
        inter = _dot_nt(q_dec[sl], state.astype(BF16))
        oacc_ref[sl, :] = intra + inter
        state = state * decay[c * C:c * C + 1] + kv_ref[c]
    st_ref[...] = state

    o_ref[...] = _head_norm_gate(oacc_ref[...], r, g_ref[...]).astype(BF16)


def _gla(z, wa, ba, g, seq):
    T = z.shape[0]
    tt = MIXER_TILE
    return pl.pallas_call(
        functools.partial(_gla_kernel, tiles_per_seq=seq // tt),
        grid=(T // tt,),
        in_specs=[pl.BlockSpec((tt, GLA_W), lambda i: (i, 0)), _full((128, 128)), _full((1, 128)),
                  _full((1, 256))],
        out_specs=pl.BlockSpec((tt, 256), lambda i: (i, 0)),
        out_shape=jax.ShapeDtypeStruct((T, 256), BF16),
        scratch_shapes=[pltpu.VMEM((256, 128), F32), pltpu.VMEM((tt, 256), F32),
                        pltpu.VMEM((tt // GLA_CHUNK, GLA_HEADS * GLA_CHUNK, GLA_CHUNK), BF16),
                        pltpu.VMEM((tt // GLA_CHUNK, 256, 128), F32)],
        compiler_params=_params("arbitrary"),
        name="gla_mixer",
    )(z, wa, ba, g)


_RET_LOG_GAMMA = [math.log1p(-(2.0 ** (-5.0 - hh))) for hh in range(RET_HEADS)]


def _by_head(head, values):
    out = jnp.full(head.shape, values[-1], F32)
    for hh in range(len(values) - 2, -1, -1):
        out = jnp.where(head == hh, values[hh], out)
    return out


def _ret_kernel(z_ref, cos_ref, sina_ref, sinb_ref, g_ref, o_ref, st_ref, oacc_ref, *, tiles_per_seq):
    C = RET_CHUNK
    tt = z_ref.shape[0]

    @pl.when(pl.program_id(0) % tiles_per_seq == 0)
    def _():
        st_ref[...] = jnp.zeros_like(st_ref)

    cos, sina, sinb = cos_ref[...], sina_ref[...], sinb_ref[...]

    def rotate(t):
        return t * cos + pltpu.roll(t, 112, 1) * sina + pltpu.roll(t, 16, 1) * sinb

    q_rot = rotate(z_ref[:, 0:128].astype(F32))
    k_rot = rotate(z_ref[:, 128:256].astype(F32)) * (RET_DK ** -0.5)
    gate_in = z_ref[:, 512:768].astype(F32)

    qk_head = _iota((C, 128), 1) >> 5
    v_head = _iota((C, 256), 1) >> 6
    lg_qk = _by_head(_iota((1, 128), 1) >> 5, _RET_LOG_GAMMA)
    lg_v = _by_head(_iota((1, 256), 1) >> 6, _RET_LOG_GAMMA)
    pos_qk = _iota((C, 128), 0).astype(F32)
    pos_v = _iota((C, 256), 0).astype(F32)
    key_decay = jnp.exp((C - 1.0 - pos_qk) * lg_qk)
    query_decay = jnp.exp((pos_v + 1.0) * lg_v)
    chunk_decay = jnp.exp(float(C) * lg_qk)
    srow = _iota((4 * C, C), 0)
    rel = ((srow & (C - 1)) - _iota((4 * C, C), 1)).astype(F32)
    decay_mat = jnp.where(rel >= 0, jnp.exp(jnp.maximum(rel, 0.0) * _by_head(srow >> 7, _RET_LOG_GAMMA)), 0.0)
    state_mask = (_iota((256, 128), 0) >> 6) == (_iota((256, 128), 1) >> 5)

    q_bf = q_rot.astype(BF16)
    k_bf = k_rot.astype(BF16)
    state = st_ref[...]
    for c in range(tt // C):
        sl = slice(c * C, (c + 1) * C)
        qc = q_bf[sl]
        v = z_ref[sl, 256:512]
        q_stack = jnp.concatenate([jnp.where(qk_head == hh, qc, 0) for hh in range(RET_HEADS)], axis=0)
        att = _dot_nt(q_stack, k_bf[sl]) * decay_mat
        intra = _head_select(_dot(att.astype(BF16), v), v_head, RET_HEADS, C)
        inter = _dot_nt(qc, state.astype(BF16)) * query_decay
        oacc_ref[sl, :] = intra + inter
        kv = _dot_tn(v, (k_rot[sl] * key_decay).astype(BF16))
        state = state * chunk_decay + jnp.where(state_mask, kv, 0.0)
    st_ref[...] = state

    o_ref[...] = _head_norm_gate(oacc_ref[...], gate_in, g_ref[...]).astype(BF16)


def _retention(z, cos, sina, sinb, g, seq):
    T = z.shape[0]
    tt = MIXER_TILE
    tps = seq // tt
    pos_spec = pl.BlockSpec((tt, 128), lambda i: (i % tps, 0))
    return pl.pallas_call(
        functools.partial(_ret_kernel, tiles_per_seq=tps),
        grid=(T // tt,),
        in_specs=[pl.BlockSpec((tt, RET_W), lambda i: (i, 0)), pos_spec, pos_spec, pos_spec, _full((1, 256))],
        out_specs=pl.BlockSpec((tt, 256), lambda i: (i, 0)),
        out_shape=jax.ShapeDtypeStruct((T, 256), BF16),
        scratch_shapes=[pltpu.VMEM((256, 128), F32), pltpu.VMEM((tt, 256), F32)],
        compiler_params=_params("arbitrary"),
        name="retention_mixer",
    )(z, cos, sina, sinb, g)


def _rope_tables(seq):
    half = RET_DK // 2
    pos = jnp.arange(seq, dtype=F32)
    inv_freq = RET_ROPE_BASE ** (-jnp.arange(half, dtype=F32) * 2.0 / RET_DK)
    ang = pos[:, None] * inv_freq[None, :]
    cos = jnp.tile(jnp.cos(ang), (1, LANES // half))
    sin = jnp.tile(jnp.sin(ang), (1, LANES // half))
    first_half = (jnp.arange(LANES) % RET_DK) < half
    return cos, jnp.where(first_half, -sin, 0.0), jnp.where(first_half, 0.0, sin)


def _swa_kernel(sink_ref, cur_ref, prev_ref, o_ref, s_ref, *, tiles_per_seq):
    W = SWA_WINDOW
    HD = SWA_HD
    tt = cur_ref.shape[0]
    first_key = jnp.where(pl.program_id(0) % tiles_per_seq == 0, W, 0)
    sj = _iota((2 * W, W), 0)
    qi = _iota((2 * W, W), 1)
    in_window = (sj > qi) & (sj <= qi + W)
    kv0 = SWA_HEADS * HD
    no_head = jnp.zeros((HD, W), BF16)
    n_blocks = tt // W

    def band(c, col0):
        prev = prev_ref[:, col0:col0 + 128] if c == 0 else cur_ref[(c - 1) * W:c * W, col0:col0 + 128]
        return jnp.concatenate([prev, cur_ref[c * W:(c + 1) * W, col0:col0 + 128]], axis=0)

    for c in range(n_blocks):
        mask = in_window & (sj >= first_key) if c == 0 else in_window
        k_band = band(c, kv0)
        q_t = cur_ref[c * W:(c + 1) * W, 0:kv0].astype(F32).T.astype(BF16)
        for hh in range(SWA_HEADS):
            kk = hh // (SWA_HEADS // SWA_KV_HEADS)
            qh_t = q_t[hh * HD:(hh + 1) * HD]
            q_on_kv = jnp.concatenate([qh_t, no_head] if kk == 0 else [no_head, qh_t], axis=0)
            s = _dot(k_band, q_on_kv) * (HD ** -0.5 * LOG2_E)
            s_ref[c, hh] = jnp.where(mask, s, NEG_INF)
    for c in range(n_blocks):
        v_t = band(c, kv0 + 128).astype(F32).T.astype(BF16)
        outs = []
        for hh in range(SWA_HEADS):
            kk = hh // (SWA_HEADS // SWA_KV_HEADS)
            s = s_ref[c, hh]
            sink = sink_ref[hh] * LOG2_E
            m = jnp.maximum(jnp.max(s, axis=0, keepdims=True), sink)
            p = jnp.exp2(s - m)
            denom = jnp.sum(p, axis=0, keepdims=True) + jnp.exp2(sink - m)
            outs.append(_dot(v_t[kk * HD:(kk + 1) * HD], p.astype(BF16)) / denom)
        o_ref[c * W:(c + 1) * W, :] = jnp.concatenate(outs, axis=0).T.astype(BF16)


def _swa(z, sinks, seq):
    T = z.shape[0]
    tt = MIXER_TILE
    per = tt // SWA_WINDOW
    return pl.pallas_call(
        functools.partial(_swa_kernel, tiles_per_seq=seq // tt),
        grid=(T // tt,),
        in_specs=[pl.BlockSpec(memory_space=pltpu.SMEM),
                  pl.BlockSpec((tt, SWA_W), lambda i: (i, 0)),
                  pl.BlockSpec((SWA_WINDOW, SWA_W), lambda i: (jnp.maximum(i * per - 1, 0), 0))],
        out_specs=pl.BlockSpec((tt, 256), lambda i: (i, 0)),
        out_shape=jax.ShapeDtypeStruct((T, 256), BF16),
        scratch_shapes=[pltpu.VMEM((per, SWA_HEADS, 2 * SWA_WINDOW, SWA_WINDOW), F32)],
        compiler_params=_params("parallel"),
        name="swa_mixer",
    )(sinks, z, z)


_KMEAN_BLOCKS = 8


def _kmean_kernel(k_ref, o_ref):
    k = k_ref[...].astype(F32).reshape(_KMEAN_BLOCKS, MOBA_BLOCK, 256)
    o_ref[...] = jnp.mean(k, axis=1)


def _moba_kmean(z):
    T = z.shape[0]
    rows = _KMEAN_BLOCKS * MOBA_BLOCK
    return pl.pallas_call(
        _kmean_kernel,
        grid=(T // rows,),
        in_specs=[pl.BlockSpec((rows, 256), lambda i: (i, 1))],
        out_specs=pl.BlockSpec((_KMEAN_BLOCKS, 256), lambda i: (i, 0)),
        out_shape=jax.ShapeDtypeStruct((T // MOBA_BLOCK, 256), F32),
        compiler_params=_params("parallel"),
        name="moba_kmean",
    )(z)


MOBA_VROWS = 80
MOBA_PREP_BLOCKS = 2


def _moba_prep_kernel(z_ref, km_ref, qt_out, k_out, vt_out):
    HD = MOBA_HD
    tq = MOBA_BLOCK
    nb = km_ref.shape[0]
    blk_i = _iota((nb, tq), 0)
    blk = blk_i.astype(F32)
    pad = LANES - HD - nb
    ones_row = (_iota((MOBA_VROWS - HD, tq), 0) == 0).astype(BF16)
    for sb in range(MOBA_PREP_BLOCKS):
        rows = slice(sb * tq, (sb + 1) * tq)
        qb = (pl.program_id(0) * MOBA_PREP_BLOCKS + sb) % nb
        this_block_lanes = (_iota((tq, nb), 1) == qb).astype(BF16)
        for hh in range(MOBA_HEADS):
            cols = slice(hh * HD, (hh + 1) * HD)
            q_t = z_ref[rows, cols].astype(F32).T
            gate = _dot(km_ref[:, cols], q_t, precision=HIGHEST)
            gate = jnp.where(blk_i < qb, gate, NEG_INF)
            keep = jnp.where(blk_i == qb, 1.0, 0.0)
            for _ in range(MOBA_TOPK):
                best = jnp.max(gate, axis=0, keepdims=True)
                first = jnp.min(jnp.where(gate == best, blk, float(nb)), axis=0, keepdims=True)
                hit = blk == first
                keep = keep + jnp.where(hit, jnp.where(best > 0.5 * NEG_INF, 1.0, 0.0), 0.0)
                gate = jnp.where(hit, BELOW_NEG_INF, gate)
            bias = ((keep - 1.0) * (-NEG_INF)).astype(BF16)
            qt_out[hh, sb] = jnp.concatenate(
                [(q_t * (HD ** -0.5 * LOG2_E)).astype(BF16), bias, jnp.zeros((pad, tq), BF16)], axis=0)
            k_out[hh, rows, :] = jnp.concatenate(
                [z_ref[rows, 256 + hh * HD:256 + (hh + 1) * HD], this_block_lanes, jnp.zeros((tq, pad), BF16)],
                axis=1)
            v_t = z_ref[rows, 512 + hh * HD:512 + (hh + 1) * HD].astype(F32).T.astype(BF16)
            vt_out[hh, sb] = jnp.concatenate([v_t, ones_row], axis=0)


def _moba_prep(z, kmean, seq):
    T = z.shape[0]
    nb = seq // MOBA_BLOCK
    assert MOBA_HD + nb <= LANES and nb % MOBA_PREP_BLOCKS == 0
    tq = MOBA_BLOCK
    tile = MOBA_PREP_BLOCKS * tq
    H = MOBA_HEADS

    def per_block_t(rows):
        return pl.BlockSpec((H, MOBA_PREP_BLOCKS, rows, tq), lambda i: (0, i, 0, 0))

    return pl.pallas_call(
        _moba_prep_kernel,
        grid=(T // tile,),
        in_specs=[pl.BlockSpec((tile, MOBA_W), lambda i: (i, 0)),
                  pl.BlockSpec((nb, 256), lambda i: (i * MOBA_PREP_BLOCKS // nb, 0))],
        out_specs=[per_block_t(LANES), pl.BlockSpec((H, tile, LANES), lambda i: (0, i, 0)),
                   per_block_t(MOBA_VROWS)],
        out_shape=[jax.ShapeDtypeStruct((H, T // tq, LANES, tq), BF16), jax.ShapeDtypeStruct((H, T, LANES), BF16),
                   jax.ShapeDtypeStruct((H, T // tq, MOBA_VROWS, tq), BF16)],
        compiler_params=_params("parallel"),
        name="moba_prep",
    )(z, kmean)


MOBA_QBLOCKS = 4
MOBA_KBLOCKS = 4
MOBA_DIAGONALS = MOBA_QBLOCKS // MOBA_KBLOCKS


def _moba_kernel(qidx_ref, kidx_ref, kind_ref, qt_ref, k_ref, vt_ref, o_ref, m_ref, acc_ref, s_ref, smax_ref):
    HD = MOBA_HD
    B = MOBA_BLOCK
    step = pl.program_id(1)
    kind = kind_ref[step]
    is_last = (step + 1 == pl.num_programs(1)) | (qidx_ref[jnp.minimum(step + 1, pl.num_programs(1) - 1)]
                                                  != qidx_ref[step])

    def attend(diagonal):
        causal = _iota((B, B), 0) <= _iota((B, B), 1)

        def key_blocks_of(qi):
            if diagonal is None:
                return list(range(MOBA_KBLOCKS))
            return [c for c in range(MOBA_KBLOCKS) if MOBA_KBLOCKS * diagonal + c <= qi]

        for hh in range(MOBA_HEADS):
            for qi in range(MOBA_QBLOCKS):
                for c in key_blocks_of(qi):
                    s = _dot(k_ref[hh, c * B:(c + 1) * B, :], qt_ref[hh, qi])
                    if diagonal is not None and MOBA_KBLOCKS * diagonal + c == qi:
                        s = jnp.where(causal, s, NEG_INF)
                    s_ref[hh, qi, c] = s
                    smax_ref[hh, qi, c] = jnp.max(s, axis=0, keepdims=True)
        first_visit = diagonal == 0
        for hh in range(MOBA_HEADS):
            for qi in range(MOBA_QBLOCKS):
                key_blocks = key_blocks_of(qi)
                if not key_blocks:
                    continue
                m_new = functools.reduce(jnp.maximum, [smax_ref[hh, qi, c] for c in key_blocks])
                if not first_visit:
                    m_old = m_ref[hh, qi]
                    m_new = jnp.maximum(m_old, m_new)
                pv = None
                for c in key_blocks:
                    p = jnp.exp2(s_ref[hh, qi, c] - m_new).astype(BF16)
                    term = _dot(vt_ref[hh, c], p)
                    pv = term if pv is None else pv + term
                acc_ref[hh, qi] = pv if first_visit else jnp.exp2(m_old - m_new) * acc_ref[hh, qi] + pv
                m_ref[hh, qi] = m_new

    @pl.when(kind == 0)
    def _():
        attend(None)

    for d in range(MOBA_DIAGONALS):
        @pl.when(kind == 1 + d)
        def _(d=d):
            attend(d)

    @pl.when(is_last)
    def _():
        for qi in range(MOBA_QBLOCKS):
            heads = [acc_ref[hh, qi, 0:HD, :] / acc_ref[hh, qi, HD:HD + 1, :] for hh in range(MOBA_HEADS)]
            o_ref[qi * B:(qi + 1) * B, :] = jnp.concatenate(heads, axis=0).T.astype(BF16)


def _moba(z, seq):
    T = z.shape[0]
    nb = seq // MOBA_BLOCK
    assert nb % MOBA_QBLOCKS == 0
    q_tiles = nb // MOBA_QBLOCKS
    k_steps = nb // MOBA_KBLOCKS
    D = MOBA_DIAGONALS
    batch = T // seq
    qt_aug, k_aug, vt_aug = _moba_prep(z, _moba_kmean(z), seq)
    qidx = np.concatenate([np.full(D * t + D, t) for t in range(q_tiles)]).astype(np.int32)
    kidx = np.concatenate([np.concatenate([D * t + np.arange(D), np.arange(D * t)])
                           for t in range(q_tiles)]).astype(np.int32)
    kind = np.concatenate([np.concatenate([1 + np.arange(D), np.zeros(D * t, np.int64)])
                           for t in range(q_tiles)]).astype(np.int32)
    tq = MOBA_QBLOCKS * MOBA_BLOCK
    tk = MOBA_KBLOCKS * MOBA_BLOCK
    H = MOBA_HEADS
    grid_spec = pltpu.PrefetchScalarGridSpec(
        num_scalar_prefetch=3,
        grid=(batch, len(qidx)),
        in_specs=[pl.BlockSpec((H, MOBA_QBLOCKS, LANES, MOBA_BLOCK),
                               lambda b, s, qi, ki, kd: (0, b * q_tiles + qi[s], 0, 0)),
                  pl.BlockSpec((H, tk, LANES), lambda b, s, qi, ki, kd: (0, b * k_steps + ki[s], 0)),
                  pl.BlockSpec((H, MOBA_KBLOCKS, MOBA_VROWS, MOBA_BLOCK),
                               lambda b, s, qi, ki, kd: (0, b * k_steps + ki[s], 0, 0))],
        out_specs=pl.BlockSpec((tq, 256), lambda b, s, qi, ki, kd: (b * q_tiles + qi[s], 0)),
        scratch_shapes=[pltpu.VMEM((H, MOBA_QBLOCKS, 1, MOBA_BLOCK), F32),
                        pltpu.VMEM((H, MOBA_QBLOCKS, MOBA_VROWS, MOBA_BLOCK), F32),
                        pltpu.VMEM((H, MOBA_QBLOCKS, MOBA_KBLOCKS, MOBA_BLOCK, MOBA_BLOCK), F32),
                        pltpu.VMEM((H, MOBA_QBLOCKS, MOBA_KBLOCKS, 1, MOBA_BLOCK), F32)],
    )
    return pl.pallas_call(
        _moba_kernel,
        grid_spec=grid_spec,
        out_shape=jax.ShapeDtypeStruct((T, 256), BF16),
        compiler_params=_params("parallel", "arbitrary"),
        name="moba_mixer",
    )(jnp.asarray(qidx), jnp.asarray(kidx), jnp.asarray(kind), qt_aug, k_aug, vt_aug)


def _merge_kernel(h_ref, o0, o1, o2, o3, half_gates_ref, half_wb_ref, wout_ref, out_ref):
    merged = None
    for i, o in enumerate((o0, o1, o2, o3)):
        half_y = _dot(o[...], half_wb_ref[i])
        term = half_y * jnp.tanh(half_gates_ref[:, i * D_MODEL:(i + 1) * D_MODEL].astype(F32)) + half_y
        merged = term if merged is None else merged + term
    out_ref[...] = h_ref[...] + _dot(merged.astype(BF16), wout_ref[...])


def _merge(h, branches, gates, wb, wout):
    T = h.shape[0]
    tm = TOK_TILE
    return pl.pallas_call(
        _merge_kernel,
        grid=(T // tm,),
        in_specs=[pl.BlockSpec((tm, D_MODEL), lambda i: (i, 0))]
        + [pl.BlockSpec((tm, 256), lambda i: (i, 0))] * 4
        + [pl.BlockSpec((tm, GATE_W), lambda i: (i, 0)),
           pl.BlockSpec((N_BRANCH, 256, D_MODEL), lambda i: (0, 0, 0), pipeline_mode=pl.Buffered(1)),
           pl.BlockSpec((D_MODEL, D_MODEL), lambda i: (0, 0), pipeline_mode=pl.Buffered(1))],
        out_specs=pl.BlockSpec((tm, D_MODEL), lambda i: (i, 0)),
        out_shape=jax.ShapeDtypeStruct((T, D_MODEL), F32),
        compiler_params=_params("parallel"),
        name="merge_out_proj",
    )(h, *branches, gates, wb, wout)


ROW_TILES = D_MODEL // (2 * LANES)
HIGH_HALF_MASK = -65536


def _slab_columns(j):
    first = 2 * j * LANES
    return slice(first, first + LANES), slice(first + LANES, first + 2 * LANES)


def _row_tile_spec(rows, index_map):
    return pl.BlockSpec((rows * ROW_TILES, LANES), index_map)


def _bf16_bits(x):
    return lax.bitcast_convert_type(x.astype(BF16).astype(F32), I32)


def _pack_rows(ref, x):
    for j in range(ROW_TILES):
        low_cols, high_cols = _slab_columns(j)
        low = _bf16_bits(x[:, low_cols])
        high = _bf16_bits(x[:, high_cols])
        ref[pl.ds(j, x.shape[0], stride=ROW_TILES), :] = high | lax.shift_right_logical(low, 16)


def _unpack_slab(ref, j):
    words = ref[pl.ds(j, ref.shape[0] // ROW_TILES, stride=ROW_TILES), :]
    low = lax.bitcast_convert_type(lax.shift_left(words, 16), F32)
    high = lax.bitcast_convert_type(words & HIGH_HALF_MASK, F32)
    return low, high


def _unpack_rows(ref):
    pieces = [piece for j in range(ROW_TILES) for piece in _unpack_slab(ref, j)]
    return jnp.concatenate(pieces, axis=1).astype(BF16)


def _router_kernel(h_ref, g_ref, wr_ref, br_ref, u_ref, route_ref, route_t_ref, count_ref, carry_ref):
    tm = h_ref.shape[0]

    @pl.when(pl.program_id(0) == 0)
    def _():
        carry_ref[...] = jnp.zeros_like(carry_ref)

    u = _rms(h_ref[...], g_ref[...])
    _pack_rows(u_ref, u)
    u_hi, u_lo = _split_bf16(u, 2)
    w_hi, w_lo = _split_bf16(wr_ref[...], 2)
    logits = _dot(u_hi, w_hi) + (_dot(u_lo, w_hi) + _dot(u_hi, w_lo)) + br_ref[...]
    lt = logits.T
    row_i = _iota((LANES, tm), 0)
    row = row_i.astype(F32)
    row_group = ((row_i - EXPERT_LANE0) >> 3).astype(F32)

    def first_row(mask):
        return jnp.min(jnp.where(mask, row, float(LANES)), axis=0, keepdims=True)

    grp = jnp.where(row_i < N_GROUPS, lt, BELOW_NEG_INF)
    grp_e = jnp.exp(grp - jnp.max(grp, axis=0, keepdims=True))
    p_group = grp_e / jnp.sum(grp_e, axis=0, keepdims=True)
    g_w = jnp.max(p_group, axis=0, keepdims=True)
    g_sel = first_row(p_group == g_w)

    in_group = (row_i >= EXPERT_LANE0) & (row_group == g_sel)
    el = jnp.where(in_group, lt, BELOW_NEG_INF)
    top1 = jnp.max(el, axis=0, keepdims=True)
    row1 = first_row(el == top1)
    el = jnp.where(row == row1, BELOW_NEG_INF, el)
    top2 = jnp.max(el, axis=0, keepdims=True)
    row2 = first_row(el == top2)
    e2 = jnp.exp(top2 - top1)
    w1 = g_w / (1.0 + e2)
    w2 = g_w * e2 / (1.0 + e2)

    onehot = ((row == row1) | (row == row2)).astype(BF16)
    earlier = (_iota((tm, tm), 0) < _iota((tm, tm), 1)).astype(BF16)
    seen = _dot(onehot, earlier) + carry_ref[:, 0:1]
    rank1 = jnp.sum(jnp.where(row == row1, seen, 0.0), axis=0, keepdims=True)
    rank2 = jnp.sum(jnp.where(row == row2, seen, 0.0), axis=0, keepdims=True)
    carry_ref[...] += jnp.sum(onehot.astype(F32), axis=1, keepdims=True)
    count_ref[...] = carry_ref[...]

    out_t = jnp.zeros((LANES, tm), F32)
    for idx, val in enumerate((row1 - EXPERT_LANE0, row2 - EXPERT_LANE0, w1, w2, rank1, rank2)):
        out_t = jnp.where(row_i == idx, val, out_t)
    route_t_ref[...] = out_t[0:8, :]
    route_ref[...] = out_t.T


def _router(h, g, wr, br):
    T = h.shape[0]
    tm = TOK_TILE
    return pl.pallas_call(
        _router_kernel,
        grid=(T // tm,),
        in_specs=[pl.BlockSpec((tm, D_MODEL), lambda i: (i, 0)), _full((1, D_MODEL)),
                  _full((D_MODEL, LANES)), _full((1, LANES))],
        out_specs=[_row_tile_spec(tm, lambda i: (i, 0)), pl.BlockSpec((tm, LANES), lambda i: (i, 0)),
                   pl.BlockSpec((8, tm), lambda i: (0, i)), _full((LANES, LANES))],
        out_shape=[jax.ShapeDtypeStruct((T * ROW_TILES, LANES), I32), jax.ShapeDtypeStruct((T, LANES), F32),
                   jax.ShapeDtypeStruct((8, T), F32), jax.ShapeDtypeStruct((LANES, LANES), F32)],
        scratch_shapes=[pltpu.VMEM((LANES, LANES), F32)],
        compiler_params=_params("arbitrary"),
        name="moe_router",
    )(h, g, wr, br)


ROW_UNROLL = 8


def _row_copy(src_ref, src_row, dst_ref, dst_row, sem):
    src = src_ref.at[pl.ds(pl.multiple_of(src_row * ROW_TILES, ROW_TILES), ROW_TILES)]
    dst = dst_ref.at[pl.ds(pl.multiple_of(dst_row * ROW_TILES, ROW_TILES), ROW_TILES)]
    return pltpu.make_async_copy(src, dst, sem)


def _for_each_row_pair(rows, body):
    def group(g, carry):
        for i in range(ROW_UNROLL):
            for k in range(EXPERT_TOPK):
                body(g * ROW_UNROLL + i, k)
        return carry

    lax.fori_loop(0, rows // ROW_UNROLL, group, 0)


def _dispatch_kernel(dest0_ref, dest1_ref, pstart_ref, pend_ref, u_ref, xs_ref, zero_ref, sem):
    tm = u_ref.shape[0] // ROW_TILES
    tile = pl.program_id(0)
    dests = (dest0_ref, dest1_ref)

    @pl.when(tile == 0)
    def _():
        zero_ref[...] = jnp.zeros_like(zero_ref)

        def zero_block(first_row):
            start = pl.multiple_of(first_row * ROW_TILES, ROW_TILES)
            fill = pltpu.make_async_copy(zero_ref, xs_ref.at[pl.ds(start, MOE_BLOCK * ROW_TILES)], sem)
            fill.start()
            fill.wait()

        def zero_tail(e, carry):
            @pl.when(pend_ref[e] > pstart_ref[e])
            def _():
                zero_block(pend_ref[e] - MOE_BLOCK)
            return carry

        def zero_unused(b, carry):
            zero_block(b * MOE_BLOCK)
            return carry

        lax.fori_loop(0, N_EXPERTS, zero_tail, 0)
        lax.fori_loop(pend_ref[N_EXPERTS - 1] // MOE_BLOCK, xs_ref.shape[0] // (MOE_BLOCK * ROW_TILES),
                      zero_unused, 0)

    def issue(r, k):
        _row_copy(u_ref, r, xs_ref, dests[k][tile * tm + r], sem).start(priority=k)

    _for_each_row_pair(tm, issue)
    for _ in range(EXPERT_TOPK):
        pltpu.make_async_copy(u_ref, xs_ref.at[pl.ds(0, tm * ROW_TILES)], sem).wait()


def _dispatch(dest0, dest1, pstart, pend, u, n_slots):
    T = u.shape[0] // ROW_TILES
    tm = DISPATCH_TILE
    grid_spec = pltpu.PrefetchScalarGridSpec(
        num_scalar_prefetch=4,
        grid=(T // tm,),
        in_specs=[_row_tile_spec(tm, lambda i, *_: (i, 0))],
        out_specs=pl.BlockSpec(memory_space=pl.ANY),
        scratch_shapes=[pltpu.VMEM((MOE_BLOCK * ROW_TILES, LANES), I32), pltpu.SemaphoreType.DMA(())],
    )
    return pl.pallas_call(
        _dispatch_kernel,
        grid_spec=grid_spec,
        out_shape=jax.ShapeDtypeStruct((n_slots * ROW_TILES, LANES), I32),
        compiler_params=_params("arbitrary"),
        name="moe_dispatch",
    )(dest0, dest1, pstart, pend, u)


def _expert_kernel(blk_e_ref, n_used_ref, x_ref, w1_ref, w3_ref, w2_ref, y_ref, w1b_ref, w3b_ref, w2b_ref):
    i = pl.program_id(0)
    used = i < n_used_ref[0]

    @pl.when((i == 0) | (blk_e_ref[i] != blk_e_ref[jnp.maximum(i - 1, 0)]))
    def _():
        w1b_ref[...] = w1_ref[0, 0].astype(BF16)
        w3b_ref[...] = w3_ref[0, 0].astype(BF16)
        w2b_ref[...] = w2_ref[0, 0].astype(BF16)

    @pl.when(used)
    def _():
        x = _unpack_rows(x_ref)
        a = _dot(x, w1b_ref[...])
        hidden = (a * _sigmoid(a)) * _dot(x, w3b_ref[...])
        _pack_rows(y_ref, _dot(hidden.astype(BF16), w2b_ref[...]))

    @pl.when(jnp.logical_not(used))
    def _():
        y_ref[...] = jnp.zeros_like(y_ref)


def _experts(blk_e, n_used, xs, w1, w3, w2, layer):
    n_slots = xs.shape[0] // ROW_TILES
    bm = MOE_BLOCK
    grid_spec = pltpu.PrefetchScalarGridSpec(
        num_scalar_prefetch=2,
        grid=(n_slots // bm,),
        in_specs=[_row_tile_spec(bm, lambda i, be, nu: (jnp.minimum(i, nu[0] - 1), 0)),
                  pl.BlockSpec((1, 1, D_MODEL, EXPERT_HIDDEN), lambda i, be, nu: (layer, be[i], 0, 0)),
                  pl.BlockSpec((1, 1, D_MODEL, EXPERT_HIDDEN), lambda i, be, nu: (layer, be[i], 0, 0)),
                  pl.BlockSpec((1, 1, EXPERT_HIDDEN, D_MODEL), lambda i, be, nu: (layer, be[i], 0, 0))],
        out_specs=_row_tile_spec(bm, lambda i, be, nu: (i, 0)),
        scratch_shapes=[pltpu.VMEM((D_MODEL, EXPERT_HIDDEN), BF16), pltpu.VMEM((D_MODEL, EXPERT_HIDDEN), BF16),
                        pltpu.VMEM((EXPERT_HIDDEN, D_MODEL), BF16)],
    )
    return pl.pallas_call(
        _expert_kernel,
        grid_spec=grid_spec,
        out_shape=jax.ShapeDtypeStruct((n_slots * ROW_TILES, LANES), I32),
        compiler_params=_params("arbitrary"),
        name="moe_experts",
    )(blk_e, n_used, xs, w1, w3, w2)


def _combine_kernel(dest0_ref, dest1_ref, h_ref, route_ref, g_ref, ys_ref, out_ref, y_ref, sem, *, normalize):
    tm = h_ref.shape[0]
    step = pl.program_id(0)
    slot = step % 2

    dests = (dest0_ref, dest1_ref)

    def gather_tile(tile, buf):
        def issue(r, k):
            _row_copy(ys_ref, dests[k][tile * tm + r], y_ref.at[buf, k], r, sem.at[buf]).start(priority=k)

        _for_each_row_pair(tm, issue)

    @pl.when(step == 0)
    def _():
        gather_tile(0, 0)

    @pl.when(step + 1 < pl.num_programs(0))
    def _():
        gather_tile(step + 1, 1 - slot)

    for k in range(EXPERT_TOPK):
        pltpu.make_async_copy(ys_ref.at[pl.ds(0, tm * ROW_TILES)], y_ref.at[slot, k], sem.at[slot]).wait()
    w0 = route_ref[:, 2:3]
    w1 = route_ref[:, 3:4]
    for j in range(ROW_TILES):
        low0, high0 = _unpack_slab(y_ref.at[slot, 0], j)
        low1, high1 = _unpack_slab(y_ref.at[slot, 1], j)
        low_cols, high_cols = _slab_columns(j)
        out_ref[:, low_cols] = h_ref[:, low_cols] + w0 * low0 + w1 * low1
        out_ref[:, high_cols] = h_ref[:, high_cols] + w0 * high0 + w1 * high1
    if normalize:
        out_ref[...] = _rms(out_ref[...], g_ref[...])


def _combine(dest0, dest1, h, route, ys, final_g, normalize):
    T = h.shape[0]
    tm = ROW_TILE
    grid_spec = pltpu.PrefetchScalarGridSpec(
        num_scalar_prefetch=2,
        grid=(T // tm,),
        in_specs=[pl.BlockSpec((tm, D_MODEL), lambda i, *_: (i, 0)),
                  pl.BlockSpec((tm, LANES), lambda i, *_: (i, 0)),
                  _full((1, D_MODEL)),
                  pl.BlockSpec(memory_space=pl.ANY)],
        out_specs=pl.BlockSpec((tm, D_MODEL), lambda i, *_: (i, 0)),
        scratch_shapes=[pltpu.VMEM((2, EXPERT_TOPK, tm * ROW_TILES, LANES), I32),
                        pltpu.SemaphoreType.DMA((2,))],
    )
    return pl.pallas_call(
        functools.partial(_combine_kernel, normalize=normalize),
        grid_spec=grid_spec,
        out_shape=jax.ShapeDtypeStruct((T, D_MODEL), F32),
        compiler_params=_params("arbitrary"),
        name="moe_combine",
    )(dest0, dest1, h, route, final_g, ys)


def _moe(h, g, w_group, b_group, w_expert, b_expert, w1, w3, w2, layer, final_g, normalize):
    T = h.shape[0]
    wr = jnp.zeros((D_MODEL, LANES), F32)
    wr = wr.at[:, :N_GROUPS].set(w_group).at[:, EXPERT_LANE0:EXPERT_LANE0 + N_EXPERTS].set(w_expert)
    br = jnp.zeros((1, LANES), F32)
    br = br.at[0, :N_GROUPS].set(b_group).at[0, EXPERT_LANE0:EXPERT_LANE0 + N_EXPERTS].set(b_expert)
    u, route, route_t, count = _router(h, g, wr, br)

    n_blocks = (T * EXPERT_TOPK) // MOE_BLOCK + N_EXPERTS
    counts = count[EXPERT_LANE0:EXPERT_LANE0 + N_EXPERTS, 0].astype(I32)
    padded = ((counts + MOE_BLOCK - 1) // MOE_BLOCK) * MOE_BLOCK
    pend = jnp.cumsum(padded)
    pstart = pend - padded
    expert_ids = jnp.arange(N_EXPERTS, dtype=F32)[:, None]

    def slot_of(expert_row, rank_row):
        segment_start = jnp.sum(jnp.where(expert_row[None, :] == expert_ids, pstart[:, None], 0), axis=0)
        return segment_start + rank_row.astype(I32)

    dest0 = slot_of(route_t[0], route_t[4])
    dest1 = slot_of(route_t[1], route_t[5])
    block_row0 = jnp.arange(n_blocks, dtype=I32) * MOE_BLOCK
    blk_e = jnp.minimum(jnp.sum((pend[None, :] <= block_row0[:, None]).astype(I32), axis=1), N_EXPERTS - 1)
    n_used = (pend[-1:] // MOE_BLOCK).astype(I32)

    xs = _dispatch(dest0, dest1, pstart, pend, u, n_blocks * MOE_BLOCK)
    ys = _experts(blk_e, n_used, xs, w1, w3, w2, layer)
    return _combine(dest0, dest1, h, route, ys, final_g, normalize)


def _split_w_in(w_in):
    gla_n = GLA_HEADS * (2 * GLA_DK + 2 * GLA_DV) + GLA_LOWRANK
    swa_n = (SWA_HEADS + 2 * SWA_KV_HEADS) * SWA_HD
    moba_n = 3 * MOBA_HEADS * MOBA_HD
    ret_n = RET_HEADS * (2 * RET_DK + 2 * RET_DV)
    offs = np.cumsum([0, gla_n, swa_n, moba_n, ret_n, GATE_W])
    parts = [w_in[:, offs[i]:offs[i + 1]].astype(BF16) for i in range(5)]
    parts[0] = jnp.pad(parts[0], ((0, 0), (0, GLA_W - gla_n)))
    parts[4] = parts[4] * 0.5
    return parts


def _token_mixer(h, seq, ln1_g, w_in, gla_w_a2, gla_b_a, gla_norm_g, swa_sinks, ret_norm_g, w_branch, w_out,
                 rope):
    z_gla, z_swa, z_moba, z_ret, half_gates = _in_proj(h, ln1_g.reshape(1, D_MODEL), _split_w_in(w_in))
    wa = jnp.zeros((128, 128), F32).at[:GLA_LOWRANK].set(gla_w_a2)
    o_gla = _gla(z_gla, wa, gla_b_a.reshape(1, 128), jnp.tile(gla_norm_g, GLA_HEADS).reshape(1, 256), seq)
    o_swa = _swa(z_swa, swa_sinks, seq)
    o_moba = _moba(z_moba, seq)
    o_ret = _retention(z_ret, *rope, jnp.tile(ret_norm_g, RET_HEADS).reshape(1, 256), seq)
    half_w_branch = w_branch.astype(BF16) * 0.5
    return _merge(h, (o_gla, o_swa, o_moba, o_ret), half_gates, half_w_branch, w_out.astype(BF16))


def kernel(x, ln1_g, w_in, gla_w_a2, gla_b_a, gla_norm_g, swa_sinks, ret_norm_g, w_branch, w_out, ln2_g,
           w_group, b_group, w_expert, b_expert, w1, w3, w2, final_g):
    batch, seq, _ = x.shape
    depth = w_in.shape[0]
    assert depth >= 1
    assert x.shape[2] == D_MODEL and seq % (MOBA_QBLOCKS * MOBA_BLOCK) == 0 and seq % MIXER_TILE == 0
    assert (batch * seq) % DISPATCH_TILE == 0 and (batch * seq * EXPERT_TOPK) % MOE_BLOCK == 0
    rope = _rope_tables(seq)
    h = x.reshape(batch * seq, D_MODEL)
    for l in range(depth):
        h = _token_mixer(h, seq, ln1_g[l], w_in[l], gla_w_a2[l], gla_b_a[l], gla_norm_g[l], swa_sinks[l],
                         ret_norm_g[l], w_branch[l], w_out[l], rope)
        h = _moe(h, ln2_g[l].reshape(1, D_MODEL), w_group[l], b_group[l], w_expert[l], b_expert[l],
                 w1, w3, w2, l, final_g.reshape(1, D_MODEL), normalize=(l == depth - 1))
    return h.reshape(batch, seq, D_MODEL)
```

```python
import functools
import math

import numpy as np
import jax
import jax.numpy as jnp
from jax import lax
from jax.experimental import pallas as pl
from jax.experimental.pallas import tpu as pltpu

F32 = jnp.float32
BF16 = jnp.bfloat16
I32 = jnp.int32
HIGHEST = lax.Precision.HIGHEST

D_MODEL = 1024
N_BRANCH = 4
NORM_EPS = 1e-6
NEG_INF = -1e30
BELOW_NEG_INF = -3e38
LOG2_E = math.log2(math.e)

GLA_HEADS, GLA_DK, GLA_DV, GLA_LOWRANK, GLA_TAU, GLA_CHUNK = 4, 32, 64, 16, 16.0, 64
SWA_HEADS, SWA_KV_HEADS, SWA_HD, SWA_WINDOW = 4, 2, 64, 128
MOBA_HEADS, MOBA_HD, MOBA_BLOCK, MOBA_TOPK = 4, 64, 256, 3
RET_HEADS, RET_DK, RET_DV, RET_CHUNK, RET_ROPE_BASE = 4, 32, 64, 128, 10000.0
N_GROUPS, EXPERTS_PER_GROUP, EXPERT_TOPK, EXPERT_HIDDEN = 4, 8, 2, 256
MOE_BLOCK = 512
N_EXPERTS = N_GROUPS * EXPERTS_PER_GROUP

GLA_W = 896
SWA_W = 512
MOBA_W = 768
RET_W = 768
GATE_W = N_BRANCH * D_MODEL

LANES = 128
EXPERT_LANE0 = 32

VMEM_LIMIT = 56 * 1024 * 1024

TOK_TILE = 512
MIXER_TILE = 2048
ROW_TILE = 512
DISPATCH_TILE = 2048


def _params(*sem):
    return pltpu.CompilerParams(dimension_semantics=sem, vmem_limit_bytes=VMEM_LIMIT)


def _dot(a, b, precision=None):
    return jnp.dot(a, b, preferred_element_type=F32, precision=precision)


def _dot_nt(a, b, precision=None):
    return lax.dot_general(a, b, (((1,), (1,)), ((), ())), preferred_element_type=F32, precision=precision)


def _dot_tn(a, b, precision=None):
    return lax.dot_general(a, b, (((0,), (0,)), ((), ())), preferred_element_type=F32, precision=precision)


def _split_bf16(x, parts):
    out = []
    for _ in range(parts - 1):
        piece = x.astype(BF16)
        out.append(piece)
        x = x - piece.astype(F32)
    out.append(x.astype(BF16))
    return out


def _dot_f32_by_bf16(x, m):
    return sum(_dot(piece, m) for piece in _split_bf16(x, 3))


def _dot_bf16_by_f32(m, x):
    return sum(_dot(m, piece) for piece in _split_bf16(x, 3))


def _iota(shape, dim):
    return lax.broadcasted_iota(I32, shape, dim)


def _sigmoid(x):
    return 0.5 * jnp.tanh(0.5 * x) + 0.5


def _rms(x, g):
    ms = jnp.mean(x * x, axis=-1, keepdims=True)
    return x * lax.rsqrt(ms + NORM_EPS) * g


def _full(shape):
    return pl.BlockSpec(shape, lambda *_: (0,) * len(shape))


def _in_proj_kernel(h_ref, g_ref, w0, w1, w2, w3, w4, o0, o1, o2, o3, o4):
    u = _rms(h_ref[...], g_ref[...]).astype(BF16)
    for w, o in ((w0, o0), (w1, o1), (w2, o2), (w3, o3), (w4, o4)):
        o[...] = _dot(u, w[...]).astype(BF16)


def _in_proj(h, g, ws):
    T = h.shape[0]
    tm = TOK_TILE
    widths = [w.shape[1] for w in ws]
    return pl.pallas_call(
        _in_proj_kernel,
        grid=(T // tm,),
        in_specs=[pl.BlockSpec((tm, D_MODEL), lambda i: (i, 0)), _full((1, D_MODEL))]
        + [pl.BlockSpec((D_MODEL, n), lambda i: (0, 0), pipeline_mode=pl.Buffered(1)) for n in widths],
        out_specs=[pl.BlockSpec((tm, n), lambda i: (i, 0)) for n in widths],
        out_shape=[jax.ShapeDtypeStruct((T, n), BF16) for n in widths],
        compiler_params=_params("parallel"),
        name="in_proj",
    )(h, g, *ws)


def _head_select(x, lane_head, n_heads, rows):
    out = None
    for hh in range(n_heads):
        term = jnp.where(lane_head == hh, x[hh * rows:(hh + 1) * rows], 0.0)
        out = term if out is None else out + term
    return out


def _head_norm_gate(o, gate_in, g):
    n = o.shape[1]
    same_head = (_iota((n, n), 0) >> 6) == (_iota((n, n), 1) >> 6)
    ms = _dot_f32_by_bf16(o * o, same_head.astype(BF16)) * (1.0 / 64.0)
    return o * lax.rsqrt(ms + NORM_EPS) * g * (gate_in * _sigmoid(gate_in))


def _gla_kernel(z_ref, wa_ref, ba_ref, g_ref, o_ref, st_ref, oacc_ref, att_ref, kv_ref, *, tiles_per_seq):
    C = GLA_CHUNK
    tt = z_ref.shape[0]

    @pl.when(pl.program_id(0) % tiles_per_seq == 0)
    def _():
        st_ref[...] = jnp.zeros_like(st_ref)

    q = z_ref[:, 0:128].astype(F32)
    k = z_ref[:, 128:256].astype(F32)
    r = z_ref[:, 512:768].astype(F32)
    pre = _dot_bf16_by_f32(z_ref[:, 768:896], wa_ref[...]) + ba_ref[...]
    log_a = (jnp.minimum(pre, 0.0) - jnp.log(1.0 + jnp.exp(-jnp.abs(pre)))) * (1.0 / GLA_TAU)

    lower = (_iota((C, C), 1) <= _iota((C, C), 0)).astype(BF16)
    b = jnp.concatenate([_dot_bf16_by_f32(lower, log_a[c * C:(c + 1) * C]) for c in range(tt // C)], axis=0)
    b_last = jnp.broadcast_to(b.reshape(tt // C, C, 128)[:, C - 1:C, :], (tt // C, C, 128)).reshape(tt, 128)
    q_dec = (q * (GLA_DK ** -0.5) * jnp.exp(b)).astype(BF16)
    k_dec = (k * jnp.exp(-b)).astype(BF16)
    k_end = (k * jnp.exp(b_last - b)).astype(BF16)
    decay = jnp.exp(b_last)

    qk_head = _iota((C, 128), 1) >> 5
    v_head = _iota((C, 256), 1) >> 6
    causal = _iota((4 * C, C), 1) <= (_iota((4 * C, C), 0) & (C - 1))
    state_mask = (_iota((256, 128), 0) >> 6) == (_iota((256, 128), 1) >> 5)

    for c in range(tt // C):
        sl = slice(c * C, (c + 1) * C)
        q_stack = jnp.concatenate([jnp.where(qk_head == hh, q_dec[sl], 0) for hh in range(GLA_HEADS)], axis=0)
        att_ref[c] = jnp.where(causal, _dot_nt(q_stack, k_dec[sl]), 0.0).astype(BF16)
        kv_ref[c] = jnp.where(state_mask, _dot_tn(z_ref[sl, 256:512], k_end[sl]), 0.0)
    state = st_ref[...]
    for c in range(tt // C):
        sl = slice(c * C, (c + 1) * C)
        intra = _head_select(_dot(att_ref[c], z_ref[sl, 256:512]), v_head, GLA_HEADS, C)
        inter = _dot_nt(q_dec[sl], state.astype(BF16))
        oacc_ref[sl, :] = intra + inter
        state = state * decay[c * C:c * C + 1] + kv_ref[c]
    st_ref[...] = state

    o_ref[...] = _head_norm_gate(oacc_ref[...], r, g_ref[...]).astype(BF16)


def _gla(z, wa, ba, g, seq):
    T = z.shape[0]
    tt = MIXER_TILE
    return pl.pallas_call(
        functools.partial(_gla_kernel, tiles_per_seq=seq // tt),
        grid=(T // tt,),
        in_specs=[pl.BlockSpec((tt, GLA_W), lambda i: (i, 0)), _full((128, 128)), _full((1, 128)),
                  _full((1, 256))],
        out_specs=pl.BlockSpec((tt, 256), lambda i: (i, 0)),
        out_shape=jax.ShapeDtypeStruct((T, 256), BF16),
        scratch_shapes=[pltpu.VMEM((256, 128), F32), pltpu.VMEM((tt, 256), F32),
                        pltpu.VMEM((tt // GLA_CHUNK, GLA_HEADS * GLA_CHUNK, GLA_CHUNK), BF16),
                        pltpu.VMEM((tt // GLA_CHUNK, 256, 128), F32)],
        compiler_params=_params("arbitrary"),
        name="gla_mixer",
    )(z, wa, ba, g)


_RET_LOG_GAMMA = [math.log1p(-(2.0 ** (-5.0 - hh))) for hh in range(RET_HEADS)]


def _by_head(head, values):
    out = jnp.full(head.shape, values[-1], F32)
    for hh in range(len(values) - 2, -1, -1):
        out = jnp.where(head == hh, values[hh], out)
    return out


def _ret_kernel(z_ref, cos_ref, sina_ref, sinb_ref, g_ref, o_ref, st_ref, oacc_ref, *, tiles_per_seq):
    C = RET_CHUNK
    tt = z_ref.shape[0]

    @pl.when(pl.program_id(0) % tiles_per_seq == 0)
    def _():
        st_ref[...] = jnp.zeros_like(st_ref)

    cos, sina, sinb = cos_ref[...], sina_ref[...], sinb_ref[...]

    def rotate(t):
        return t * cos + pltpu.roll(t, 112, 1) * sina + pltpu.roll(t, 16, 1) * sinb

    q_rot = rotate(z_ref[:, 0:128].astype(F32))
    k_rot = rotate(z_ref[:, 128:256].astype(F32)) * (RET_DK ** -0.5)
    gate_in = z_ref[:, 512:768].astype(F32)

    qk_head = _iota((C, 128), 1) >> 5
    v_head = _iota((C, 256), 1) >> 6
    lg_qk = _by_head(_iota((1, 128), 1) >> 5, _RET_LOG_GAMMA)
    lg_v = _by_head(_iota((1, 256), 1) >> 6, _RET_LOG_GAMMA)
    pos_qk = _iota((C, 128), 0).astype(F32)
    pos_v = _iota((C, 256), 0).astype(F32)
    key_decay = jnp.exp((C - 1.0 - pos_qk) * lg_qk)
    query_decay = jnp.exp((pos_v + 1.0) * lg_v)
    chunk_decay = jnp.exp(float(C) * lg_qk)
    srow = _iota((4 * C, C), 0)
    rel = ((srow & (C - 1)) - _iota((4 * C, C), 1)).astype(F32)
    decay_mat = jnp.where(rel >= 0, jnp.exp(jnp.maximum(rel, 0.0) * _by_head(srow >> 7, _RET_LOG_GAMMA)), 0.0)
    state_mask = (_iota((256, 128), 0) >> 6) == (_iota((256, 128), 1) >> 5)

    q_bf = q_rot.astype(BF16)
    k_bf = k_rot.astype(BF16)
    state = st_ref[...]
    for c in range(tt // C):
        sl = slice(c * C, (c + 1) * C)
        qc = q_bf[sl]
        v = z_ref[sl, 256:512]
        q_stack = jnp.concatenate([jnp.where(qk_head == hh, qc, 0) for hh in range(RET_HEADS)], axis=0)
        att = _dot_nt(q_stack, k_bf[sl]) * decay_mat
        intra = _head_select(_dot(att.astype(BF16), v), v_head, RET_HEADS, C)
        inter = _dot_nt(qc, state.astype(BF16)) * query_decay
        oacc_ref[sl, :] = intra + inter
        kv = _dot_tn(v, (k_rot[sl] * key_decay).astype(BF16))
        state = state * chunk_decay + jnp.where(state_mask, kv, 0.0)
    st_ref[...] = state

    o_ref[...] = _head_norm_gate(oacc_ref[...], gate_in, g_ref[...]).astype(BF16)


def _retention(z, cos, sina, sinb, g, seq):
    T = z.shape[0]
    tt = MIXER_TILE
    tps = seq // tt
    pos_spec = pl.BlockSpec((tt, 128), lambda i: (i % tps, 0))
    return pl.pallas_call(
        functools.partial(_ret_kernel, tiles_per_seq=tps),
        grid=(T // tt,),
        in_specs=[pl.BlockSpec((tt, RET_W), lambda i: (i, 0)), pos_spec, pos_spec, pos_spec, _full((1, 256))],
        out_specs=pl.BlockSpec((tt, 256), lambda i: (i, 0)),
        out_shape=jax.ShapeDtypeStruct((T, 256), BF16),
        scratch_shapes=[pltpu.VMEM((256, 128), F32), pltpu.VMEM((tt, 256), F32)],
        compiler_params=_params("arbitrary"),
        name="retention_mixer",
    )(z, cos, sina, sinb, g)


def _rope_tables(seq):
    half = RET_DK // 2
    pos = jnp.arange(seq, dtype=F32)
    inv_freq = RET_ROPE_BASE ** (-jnp.arange(half, dtype=F32) * 2.0 / RET_DK)
    ang = pos[:, None] * inv_freq[None, :]
    cos = jnp.tile(jnp.cos(ang), (1, LANES // half))
    sin = jnp.tile(jnp.sin(ang), (1, LANES // half))
    first_half = (jnp.arange(LANES) % RET_DK) < half
    return cos, jnp.where(first_half, -sin, 0.0), jnp.where(first_half, 0.0, sin)


def _swa_kernel(sink_ref, cur_ref, prev_ref, o_ref, s_ref, *, tiles_per_seq):
    W = SWA_WINDOW
    HD = SWA_HD
    tt = cur_ref.shape[0]
    first_key = jnp.where(pl.program_id(0) % tiles_per_seq == 0, W, 0)
    sj = _iota((2 * W, W), 0)
    qi = _iota((2 * W, W), 1)
    in_window = (sj > qi) & (sj <= qi + W)
    kv0 = SWA_HEADS * HD
    no_head = jnp.zeros((HD, W), BF16)
    n_blocks = tt // W

    def band(c, col0):
        prev = prev_ref[:, col0:col0 + 128] if c == 0 else cur_ref[(c - 1) * W:c * W, col0:col0 + 128]
        return jnp.concatenate([prev, cur_ref[c * W:(c + 1) * W, col0:col0 + 128]], axis=0)

    for c in range(n_blocks):
        mask = in_window & (sj >= first_key) if c == 0 else in_window
        k_band = band(c, kv0)
        q_t = cur_ref[c * W:(c + 1) * W, 0:kv0].astype(F32).T.astype(BF16)
        for hh in range(SWA_HEADS):
            kk = hh // (SWA_HEADS // SWA_KV_HEADS)
            qh_t = q_t[hh * HD:(hh + 1) * HD]
            q_on_kv = jnp.concatenate([qh_t, no_head] if kk == 0 else [no_head, qh_t], axis=0)
            s = _dot(k_band, q_on_kv) * (HD ** -0.5 * LOG2_E)
            s_ref[c, hh] = jnp.where(mask, s, NEG_INF)
    for c in range(n_blocks):
        v_t = band(c, kv0 + 128).astype(F32).T.astype(BF16)
        outs = []
        for hh in range(SWA_HEADS):
            kk = hh // (SWA_HEADS // SWA_KV_HEADS)
            s = s_ref[c, hh]
            sink = sink_ref[hh] * LOG2_E
            m = jnp.maximum(jnp.max(s, axis=0, keepdims=True), sink)
            p = jnp.exp2(s - m)
            denom = jnp.sum(p, axis=0, keepdims=True) + jnp.exp2(sink - m)
            outs.append(_dot(v_t[kk * HD:(kk + 1) * HD], p.astype(BF16)) / denom)
        o_ref[c * W:(c + 1) * W, :] = jnp.concatenate(outs, axis=0).T.astype(BF16)


def _swa(z, sinks, seq):
    T = z.shape[0]
    tt = MIXER_TILE
    per = tt // SWA_WINDOW
    return pl.pallas_call(
        functools.partial(_swa_kernel, tiles_per_seq=seq // tt),
        grid=(T // tt,),
        in_specs=[pl.BlockSpec(memory_space=pltpu.SMEM),
                  pl.BlockSpec((tt, SWA_W), lambda i: (i, 0)),
                  pl.BlockSpec((SWA_WINDOW, SWA_W), lambda i: (jnp.maximum(i * per - 1, 0), 0))],
        out_specs=pl.BlockSpec((tt, 256), lambda i: (i, 0)),
        out_shape=jax.ShapeDtypeStruct((T, 256), BF16),
        scratch_shapes=[pltpu.VMEM((per, SWA_HEADS, 2 * SWA_WINDOW, SWA_WINDOW), F32)],
        compiler_params=_params("parallel"),
        name="swa_mixer",
    )(sinks, z, z)


_KMEAN_BLOCKS = 8


def _kmean_kernel(k_ref, o_ref):
    k = k_ref[...].astype(F32).reshape(_KMEAN_BLOCKS, MOBA_BLOCK, 256)
    o_ref[...] = jnp.mean(k, axis=1)


def _moba_kmean(z):
    T = z.shape[0]
    rows = _KMEAN_BLOCKS * MOBA_BLOCK
    return pl.pallas_call(
        _kmean_kernel,
        grid=(T // rows,),
        in_specs=[pl.BlockSpec((rows, 256), lambda i: (i, 1))],
        out_specs=pl.BlockSpec((_KMEAN_BLOCKS, 256), lambda i: (i, 0)),
        out_shape=jax.ShapeDtypeStruct((T // MOBA_BLOCK, 256), F32),
        compiler_params=_params("parallel"),
        name="moba_kmean",
    )(z)


MOBA_VROWS = 80
MOBA_PREP_BLOCKS = 4


def _moba_prep_kernel(z_ref, km_ref, qt_out, k_out, vt_out):
    HD = MOBA_HD
    tq = MOBA_BLOCK
    nb = km_ref.shape[0]
    blk_i = _iota((nb, tq), 0)
    blk = blk_i.astype(F32)
    pad = LANES - HD - nb
    ones_row = (_iota((MOBA_VROWS - HD, tq), 0) == 0).astype(BF16)
    for sb in range(MOBA_PREP_BLOCKS):
        rows = slice(sb * tq, (sb + 1) * tq)
        qb = (pl.program_id(0) * MOBA_PREP_BLOCKS + sb) % nb
        this_block_lanes = (_iota((tq, nb), 1) == qb).astype(BF16)
        for hh in range(MOBA_HEADS):
            cols = slice(hh * HD, (hh + 1) * HD)
            q_t = z_ref[rows, cols].astype(F32).T
            gate = _dot(km_ref[:, cols], q_t, precision=HIGHEST)
            gate = jnp.where(blk_i < qb, gate, NEG_INF)
            keep = jnp.where(blk_i == qb, 1.0, 0.0)
            for _ in range(MOBA_TOPK):
                best = jnp.max(gate, axis=0, keepdims=True)
                first = jnp.min(jnp.where(gate == best, blk, float(nb)), axis=0, keepdims=True)
                hit = blk == first
                keep = keep + jnp.where(hit, jnp.where(best > 0.5 * NEG_INF, 1.0, 0.0), 0.0)
                gate = jnp.where(hit, BELOW_NEG_INF, gate)
            bias = ((keep - 1.0) * (-NEG_INF)).astype(BF16)
            qt_out[hh, sb] = jnp.concatenate(
                [(q_t * (HD ** -0.5 * LOG2_E)).astype(BF16), bias, jnp.zeros((pad, tq), BF16)], axis=0)
            k_out[hh, rows, :] = jnp.concatenate(
                [z_ref[rows, 256 + hh * HD:256 + (hh + 1) * HD], this_block_lanes, jnp.zeros((tq, pad), BF16)],
                axis=1)
            v_t = z_ref[rows, 512 + hh * HD:512 + (hh + 1) * HD].astype(F32).T.astype(BF16)
            vt_out[hh, sb] = jnp.concatenate([v_t, ones_row], axis=0)


def _moba_prep(z, kmean, seq):
    T = z.shape[0]
    nb = seq // MOBA_BLOCK
    assert MOBA_HD + nb <= LANES and nb % MOBA_PREP_BLOCKS == 0
    tq = MOBA_BLOCK
    tile = MOBA_PREP_BLOCKS * tq
    H = MOBA_HEADS

    def per_block_t(rows):
        return pl.BlockSpec((H, MOBA_PREP_BLOCKS, rows, tq), lambda i: (0, i, 0, 0))

    return pl.pallas_call(
        _moba_prep_kernel,
        grid=(T // tile,),
        in_specs=[pl.BlockSpec((tile, MOBA_W), lambda i: (i, 0)),
                  pl.BlockSpec((nb, 256), lambda i: (i * MOBA_PREP_BLOCKS // nb, 0))],
        out_specs=[per_block_t(LANES), pl.BlockSpec((H, tile, LANES), lambda i: (0, i, 0)),
                   per_block_t(MOBA_VROWS)],
        out_shape=[jax.ShapeDtypeStruct((H, T // tq, LANES, tq), BF16), jax.ShapeDtypeStruct((H, T, LANES), BF16),
                   jax.ShapeDtypeStruct((H, T // tq, MOBA_VROWS, tq), BF16)],
        compiler_params=_params("parallel"),
        name="moba_prep",
    )(z, kmean)


MOBA_QBLOCKS = 4
MOBA_KBLOCKS = 4
MOBA_DIAGONALS = MOBA_QBLOCKS // MOBA_KBLOCKS


def _moba_kernel(qidx_ref, kidx_ref, kind_ref, qt_ref, k_ref, vt_ref, o_ref, m_ref, acc_ref, s_ref, smax_ref):
    HD = MOBA_HD
    B = MOBA_BLOCK
    step = pl.program_id(1)
    kind = kind_ref[step]
    is_last = (step + 1 == pl.num_programs(1)) | (qidx_ref[jnp.minimum(step + 1, pl.num_programs(1) - 1)]
                                                  != qidx_ref[step])

    def attend(diagonal):
        causal = _iota((B, B), 0) <= _iota((B, B), 1)

        def key_blocks_of(qi):
            if diagonal is None:
                return list(range(MOBA_KBLOCKS))
            return [c for c in range(MOBA_KBLOCKS) if MOBA_KBLOCKS * diagonal + c <= qi]

        for hh in range(MOBA_HEADS):
            for qi in range(MOBA_QBLOCKS):
                for c in key_blocks_of(qi):
                    s = _dot(k_ref[hh, c * B:(c + 1) * B, :], qt_ref[hh, qi])
                    if diagonal is not None and MOBA_KBLOCKS * diagonal + c == qi:
                        s = jnp.where(causal, s, NEG_INF)
                    s_ref[hh, qi, c] = s
                    smax_ref[hh, qi, c] = jnp.max(s, axis=0, keepdims=True)
        first_visit = diagonal == 0
        for hh in range(MOBA_HEADS):
            for qi in range(MOBA_QBLOCKS):
                key_blocks = key_blocks_of(qi)
                if not key_blocks:
                    continue
                m_new = functools.reduce(jnp.maximum, [smax_ref[hh, qi, c] for c in key_blocks])
                if not first_visit:
                    m_old = m_ref[hh, qi]
                    m_new = jnp.maximum(m_old, m_new)
                pv = None
                for c in key_blocks:
                    p = jnp.exp2(s_ref[hh, qi, c] - m_new).astype(BF16)
                    term = _dot(vt_ref[hh, c], p)
                    pv = term if pv is None else pv + term
                acc_ref[hh, qi] = pv if first_visit else jnp.exp2(m_old - m_new) * acc_ref[hh, qi] + pv
                m_ref[hh, qi] = m_new

    @pl.when(kind == 0)
    def _():
        attend(None)

    for d in range(MOBA_DIAGONALS):
        @pl.when(kind == 1 + d)
        def _(d=d):
            attend(d)

    @pl.when(is_last)
    def _():
        for qi in range(MOBA_QBLOCKS):
            heads = [acc_ref[hh, qi, 0:HD, :] / acc_ref[hh, qi, HD:HD + 1, :] for hh in range(MOBA_HEADS)]
            o_ref[qi * B:(qi + 1) * B, :] = jnp.concatenate(heads, axis=0).T.astype(BF16)


def _moba(z, seq):
    T = z.shape[0]
    nb = seq // MOBA_BLOCK
    assert nb % MOBA_QBLOCKS == 0
    q_tiles = nb // MOBA_QBLOCKS
    k_steps = nb // MOBA_KBLOCKS
    D = MOBA_DIAGONALS
    batch = T // seq
    qt_aug, k_aug, vt_aug = _moba_prep(z, _moba_kmean(z), seq)
    qidx = np.concatenate([np.full(D * t + D, t) for t in range(q_tiles)]).astype(np.int32)
    kidx = np.concatenate([np.concatenate([D * t + np.arange(D), np.arange(D * t)])
                           for t in range(q_tiles)]).astype(np.int32)
    kind = np.concatenate([np.concatenate([1 + np.arange(D), np.zeros(D * t, np.int64)])
                           for t in range(q_tiles)]).astype(np.int32)
    tq = MOBA_QBLOCKS * MOBA_BLOCK
    tk = MOBA_KBLOCKS * MOBA_BLOCK
    H = MOBA_HEADS
    grid_spec = pltpu.PrefetchScalarGridSpec(
        num_scalar_prefetch=3,
        grid=(batch, len(qidx)),
        in_specs=[pl.BlockSpec((H, MOBA_QBLOCKS, LANES, MOBA_BLOCK),
                               lambda b, s, qi, ki, kd: (0, b * q_tiles + qi[s], 0, 0)),
                  pl.BlockSpec((H, tk, LANES), lambda b, s, qi, ki, kd: (0, b * k_steps + ki[s], 0)),
                  pl.BlockSpec((H, MOBA_KBLOCKS, MOBA_VROWS, MOBA_BLOCK),
                               lambda b, s, qi, ki, kd: (0, b * k_steps + ki[s], 0, 0))],
        out_specs=pl.BlockSpec((tq, 256), lambda b, s, qi, ki, kd: (b * q_tiles + qi[s], 0)),
        scratch_shapes=[pltpu.VMEM((H, MOBA_QBLOCKS, 1, MOBA_BLOCK), F32),
                        pltpu.VMEM((H, MOBA_QBLOCKS, MOBA_VROWS, MOBA_BLOCK), F32),
                        pltpu.VMEM((H, MOBA_QBLOCKS, MOBA_KBLOCKS, MOBA_BLOCK, MOBA_BLOCK), F32),
                        pltpu.VMEM((H, MOBA_QBLOCKS, MOBA_KBLOCKS, 1, MOBA_BLOCK), F32)],
    )
    return pl.pallas_call(
        _moba_kernel,
        grid_spec=grid_spec,
        out_shape=jax.ShapeDtypeStruct((T, 256), BF16),
        compiler_params=_params("parallel", "arbitrary"),
        name="moba_mixer",
    )(jnp.asarray(qidx), jnp.asarray(kidx), jnp.asarray(kind), qt_aug, k_aug, vt_aug)


def _merge_kernel(h_ref, o0, o1, o2, o3, half_gates_ref, half_wb_ref, wout_ref, out_ref):
    merged = None
    for i, o in enumerate((o0, o1, o2, o3)):
        half_y = _dot(o[...], half_wb_ref[i])
        term = half_y * jnp.tanh(half_gates_ref[:, i * D_MODEL:(i + 1) * D_MODEL].astype(F32)) + half_y
        merged = term if merged is None else merged + term
    out_ref[...] = h_ref[...] + _dot(merged.astype(BF16), wout_ref[...])


def _merge(h, branches, gates, wb, wout):
    T = h.shape[0]
    tm = TOK_TILE
    return pl.pallas_call(
        _merge_kernel,
        grid=(T // tm,),
        in_specs=[pl.BlockSpec((tm, D_MODEL), lambda i: (i, 0))]
        + [pl.BlockSpec((tm, 256), lambda i: (i, 0))] * 4
        + [pl.BlockSpec((tm, GATE_W), lambda i: (i, 0)),
           pl.BlockSpec((N_BRANCH, 256, D_MODEL), lambda i: (0, 0, 0), pipeline_mode=pl.Buffered(1)),
           pl.BlockSpec((D_MODEL, D_MODEL), lambda i: (0, 0), pipeline_mode=pl.Buffered(1))],
        out_specs=pl.BlockSpec((tm, D_MODEL), lambda i: (i, 0)),
        out_shape=jax.ShapeDtypeStruct((T, D_MODEL), F32),
        compiler_params=_params("parallel"),
        name="merge_out_proj",
    )(h, *branches, gates, wb, wout)


ROW_TILES = D_MODEL // (2 * LANES)
HIGH_HALF_MASK = -65536


def _slab_columns(j):
    first = 2 * j * LANES
    return slice(first, first + LANES), slice(first + LANES, first + 2 * LANES)


def _row_tile_spec(rows, index_map):
    return pl.BlockSpec((rows * ROW_TILES, LANES), index_map)


def _bf16_bits(x):
    return lax.bitcast_convert_type(x.astype(BF16).astype(F32), I32)


def _pack_rows(ref, x):
    for j in range(ROW_TILES):
        low_cols, high_cols = _slab_columns(j)
        low = _bf16_bits(x[:, low_cols])
        high = _bf16_bits(x[:, high_cols])
        ref[pl.ds(j, x.shape[0], stride=ROW_TILES), :] = high | lax.shift_right_logical(low, 16)


def _unpack_slab(ref, j):
    words = ref[pl.ds(j, ref.shape[0] // ROW_TILES, stride=ROW_TILES), :]
    low = lax.bitcast_convert_type(lax.shift_left(words, 16), F32)
    high = lax.bitcast_convert_type(words & HIGH_HALF_MASK, F32)
    return low, high


def _unpack_rows(ref):
    pieces = [piece for j in range(ROW_TILES) for piece in _unpack_slab(ref, j)]
    return jnp.concatenate(pieces, axis=1).astype(BF16)


def _router_kernel(h_ref, g_ref, wr_ref, br_ref, u_ref, route_ref, route_t_ref, count_ref, carry_ref):
    tm = h_ref.shape[0]

    @pl.when(pl.program_id(0) == 0)
    def _():
        carry_ref[...] = jnp.zeros_like(carry_ref)

    u = _rms(h_ref[...], g_ref[...])
    _pack_rows(u_ref, u)
    u_hi, u_lo = _split_bf16(u, 2)
    w_hi, w_lo = _split_bf16(wr_ref[...], 2)
    logits = _dot(u_hi, w_hi) + (_dot(u_lo, w_hi) + _dot(u_hi, w_lo)) + br_ref[...]
    lt = logits.T
    row_i = _iota((LANES, tm), 0)
    row = row_i.astype(F32)
    row_group = ((row_i - EXPERT_LANE0) >> 3).astype(F32)

    def first_row(mask):
        return jnp.min(jnp.where(mask, row, float(LANES)), axis=0, keepdims=True)

    grp = jnp.where(row_i < N_GROUPS, lt, BELOW_NEG_INF)
    grp_e = jnp.exp(grp - jnp.max(grp, axis=0, keepdims=True))
    p_group = grp_e / jnp.sum(grp_e, axis=0, keepdims=True)
    g_w = jnp.max(p_group, axis=0, keepdims=True)
    g_sel = first_row(p_group == g_w)

    in_group = (row_i >= EXPERT_LANE0) & (row_group == g_sel)
    el = jnp.where(in_group, lt, BELOW_NEG_INF)
    top1 = jnp.max(el, axis=0, keepdims=True)
    row1 = first_row(el == top1)
    el = jnp.where(row == row1, BELOW_NEG_INF, el)
    top2 = jnp.max(el, axis=0, keepdims=True)
    row2 = first_row(el == top2)
    e2 = jnp.exp(top2 - top1)
    w1 = g_w / (1.0 + e2)
    w2 = g_w * e2 / (1.0 + e2)

    onehot = ((row == row1) | (row == row2)).astype(BF16)
    earlier = (_iota((tm, tm), 0) < _iota((tm, tm), 1)).astype(BF16)
    seen = _dot(onehot, earlier) + carry_ref[:, 0:1]
    rank1 = jnp.sum(jnp.where(row == row1, seen, 0.0), axis=0, keepdims=True)
    rank2 = jnp.sum(jnp.where(row == row2, seen, 0.0), axis=0, keepdims=True)
    carry_ref[...] += jnp.sum(onehot.astype(F32), axis=1, keepdims=True)
    count_ref[...] = carry_ref[...]

    out_t = jnp.zeros((LANES, tm), F32)
    for idx, val in enumerate((row1 - EXPERT_LANE0, row2 - EXPERT_LANE0, w1, w2, rank1, rank2)):
        out_t = jnp.where(row_i == idx, val, out_t)
    route_t_ref[...] = out_t[0:8, :]
    route_ref[...] = out_t.T


def _router(h, g, wr, br):
    T = h.shape[0]
    tm = TOK_TILE
    return pl.pallas_call(
        _router_kernel,
        grid=(T // tm,),
        in_specs=[pl.BlockSpec((tm, D_MODEL), lambda i: (i, 0)), _full((1, D_MODEL)),
                  _full((D_MODEL, LANES)), _full((1, LANES))],
        out_specs=[_row_tile_spec(tm, lambda i: (i, 0)), pl.BlockSpec((tm, LANES), lambda i: (i, 0)),
                   pl.BlockSpec((8, tm), lambda i: (0, i)), _full((LANES, LANES))],
        out_shape=[jax.ShapeDtypeStruct((T * ROW_TILES, LANES), I32), jax.ShapeDtypeStruct((T, LANES), F32),
                   jax.ShapeDtypeStruct((8, T), F32), jax.ShapeDtypeStruct((LANES, LANES), F32)],
        scratch_shapes=[pltpu.VMEM((LANES, LANES), F32)],
        compiler_params=_params("arbitrary"),
        name="moe_router",
    )(h, g, wr, br)


ROW_UNROLL = 8


def _row_copy(src_ref, src_row, dst_ref, dst_row, sem):
    src = src_ref.at[pl.ds(pl.multiple_of(src_row * ROW_TILES, ROW_TILES), ROW_TILES)]
    dst = dst_ref.at[pl.ds(pl.multiple_of(dst_row * ROW_TILES, ROW_TILES), ROW_TILES)]
    return pltpu.make_async_copy(src, dst, sem)


def _for_each_row_pair(rows, body):
    def group(g, carry):
        for i in range(ROW_UNROLL):
            for k in range(EXPERT_TOPK):
                body(g * ROW_UNROLL + i, k)
        return carry

    lax.fori_loop(0, rows // ROW_UNROLL, group, 0)


def _dispatch_kernel(dest0_ref, dest1_ref, pstart_ref, pend_ref, u_ref, xs_ref, zero_ref, sem):
    tm = u_ref.shape[0] // ROW_TILES
    tile = pl.program_id(0)
    dests = (dest0_ref, dest1_ref)

    @pl.when(tile == 0)
    def _():
        zero_ref[...] = jnp.zeros_like(zero_ref)

        def zero_block(first_row):
            start = pl.multiple_of(first_row * ROW_TILES, ROW_TILES)
            fill = pltpu.make_async_copy(zero_ref, xs_ref.at[pl.ds(start, MOE_BLOCK * ROW_TILES)], sem)
            fill.start()
            fill.wait()

        def zero_tail(e, carry):
            @pl.when(pend_ref[e] > pstart_ref[e])
            def _():
                zero_block(pend_ref[e] - MOE_BLOCK)
            return carry

        def zero_unused(b, carry):
            zero_block(b * MOE_BLOCK)
            return carry

        lax.fori_loop(0, N_EXPERTS, zero_tail, 0)
        lax.fori_loop(pend_ref[N_EXPERTS - 1] // MOE_BLOCK, xs_ref.shape[0] // (MOE_BLOCK * ROW_TILES),
                      zero_unused, 0)

    def issue(r, k):
        _row_copy(u_ref, r, xs_ref, dests[k][tile * tm + r], sem).start(priority=k)

    _for_each_row_pair(tm, issue)
    for _ in range(EXPERT_TOPK):
        pltpu.make_async_copy(u_ref, xs_ref.at[pl.ds(0, tm * ROW_TILES)], sem).wait()


def _dispatch(dest0, dest1, pstart, pend, u, n_slots):
    T = u.shape[0] // ROW_TILES
    tm = DISPATCH_TILE
    grid_spec = pltpu.PrefetchScalarGridSpec(
        num_scalar_prefetch=4,
        grid=(T // tm,),
        in_specs=[_row_tile_spec(tm, lambda i, *_: (i, 0))],
        out_specs=pl.BlockSpec(memory_space=pl.ANY),
        scratch_shapes=[pltpu.VMEM((MOE_BLOCK * ROW_TILES, LANES), I32), pltpu.SemaphoreType.DMA(())],
    )
    return pl.pallas_call(
        _dispatch_kernel,
        grid_spec=grid_spec,
        out_shape=jax.ShapeDtypeStruct((n_slots * ROW_TILES, LANES), I32),
        compiler_params=_params("arbitrary"),
        name="moe_dispatch",
    )(dest0, dest1, pstart, pend, u)


def _expert_kernel(blk_e_ref, n_used_ref, x_ref, w1_ref, w3_ref, w2_ref, y_ref, w1b_ref, w3b_ref, w2b_ref):
    i = pl.program_id(0)
    used = i < n_used_ref[0]

    @pl.when((i == 0) | (blk_e_ref[i] != blk_e_ref[jnp.maximum(i - 1, 0)]))
    def _():
        w1b_ref[...] = w1_ref[0, 0].astype(BF16)
        w3b_ref[...] = w3_ref[0, 0].astype(BF16)
        w2b_ref[...] = w2_ref[0, 0].astype(BF16)

    @pl.when(used)
    def _():
        x = _unpack_rows(x_ref)
        a = _dot(x, w1b_ref[...])
        hidden = (a * _sigmoid(a)) * _dot(x, w3b_ref[...])
        _pack_rows(y_ref, _dot(hidden.astype(BF16), w2b_ref[...]))

    @pl.when(jnp.logical_not(used))
    def _():
        y_ref[...] = jnp.zeros_like(y_ref)


def _experts(blk_e, n_used, xs, w1, w3, w2, layer):
    n_slots = xs.shape[0] // ROW_TILES
    bm = MOE_BLOCK
    grid_spec = pltpu.PrefetchScalarGridSpec(
        num_scalar_prefetch=2,
        grid=(n_slots // bm,),
        in_specs=[_row_tile_spec(bm, lambda i, be, nu: (jnp.minimum(i, nu[0] - 1), 0)),
                  pl.BlockSpec((1, 1, D_MODEL, EXPERT_HIDDEN), lambda i, be, nu: (layer, be[i], 0, 0)),
                  pl.BlockSpec((1, 1, D_MODEL, EXPERT_HIDDEN), lambda i, be, nu: (layer, be[i], 0, 0)),
                  pl.BlockSpec((1, 1, EXPERT_HIDDEN, D_MODEL), lambda i, be, nu: (layer, be[i], 0, 0))],
        out_specs=_row_tile_spec(bm, lambda i, be, nu: (i, 0)),
        scratch_shapes=[pltpu.VMEM((D_MODEL, EXPERT_HIDDEN), BF16), pltpu.VMEM((D_MODEL, EXPERT_HIDDEN), BF16),
                        pltpu.VMEM((EXPERT_HIDDEN, D_MODEL), BF16)],
    )
    return pl.pallas_call(
        _expert_kernel,
        grid_spec=grid_spec,
        out_shape=jax.ShapeDtypeStruct((n_slots * ROW_TILES, LANES), I32),
        compiler_params=_params("arbitrary"),
        name="moe_experts",
    )(blk_e, n_used, xs, w1, w3, w2)


def _combine_kernel(dest0_ref, dest1_ref, h_ref, route_ref, g_ref, ys_ref, out_ref, y_ref, sem, *, normalize):
    tm = h_ref.shape[0]
    step = pl.program_id(0)
    slot = step % 2

    dests = (dest0_ref, dest1_ref)

    def gather_tile(tile, buf):
        def issue(r, k):
            _row_copy(ys_ref, dests[k][tile * tm + r], y_ref.at[buf, k], r, sem.at[buf]).start(priority=k)

        _for_each_row_pair(tm, issue)

    @pl.when(step == 0)
    def _():
        gather_tile(0, 0)

    @pl.when(step + 1 < pl.num_programs(0))
    def _():
        gather_tile(step + 1, 1 - slot)

    for k in range(EXPERT_TOPK):
        pltpu.make_async_copy(ys_ref.at[pl.ds(0, tm * ROW_TILES)], y_ref.at[slot, k], sem.at[slot]).wait()
    w0 = route_ref[:, 2:3]
    w1 = route_ref[:, 3:4]
    for j in range(ROW_TILES):
        low0, high0 = _unpack_slab(y_ref.at[slot, 0], j)
        low1, high1 = _unpack_slab(y_ref.at[slot, 1], j)
        low_cols, high_cols = _slab_columns(j)
        out_ref[:, low_cols] = h_ref[:, low_cols] + w0 * low0 + w1 * low1
        out_ref[:, high_cols] = h_ref[:, high_cols] + w0 * high0 + w1 * high1
    if normalize:
        out_ref[...] = _rms(out_ref[...], g_ref[...])


def _combine(dest0, dest1, h, route, ys, final_g, normalize):
    T = h.shape[0]
    tm = ROW_TILE
    grid_spec = pltpu.PrefetchScalarGridSpec(
        num_scalar_prefetch=2,
        grid=(T // tm,),
        in_specs=[pl.BlockSpec((tm, D_MODEL), lambda i, *_: (i, 0)),
                  pl.BlockSpec((tm, LANES), lambda i, *_: (i, 0)),
                  _full((1, D_MODEL)),
                  pl.BlockSpec(memory_space=pl.ANY)],
        out_specs=pl.BlockSpec((tm, D_MODEL), lambda i, *_: (i, 0)),
        scratch_shapes=[pltpu.VMEM((2, EXPERT_TOPK, tm * ROW_TILES, LANES), I32),
                        pltpu.SemaphoreType.DMA((2,))],
    )
    return pl.pallas_call(
        functools.partial(_combine_kernel, normalize=normalize),
        grid_spec=grid_spec,
        out_shape=jax.ShapeDtypeStruct((T, D_MODEL), F32),
        compiler_params=_params("arbitrary"),
        name="moe_combine",
    )(dest0, dest1, h, route, final_g, ys)


def _moe(h, g, w_group, b_group, w_expert, b_expert, w1, w3, w2, layer, final_g, normalize):
    T = h.shape[0]
    wr = jnp.zeros((D_MODEL, LANES), F32)
    wr = wr.at[:, :N_GROUPS].set(w_group).at[:, EXPERT_LANE0:EXPERT_LANE0 + N_EXPERTS].set(w_expert)
    br = jnp.zeros((1, LANES), F32)
    br = br.at[0, :N_GROUPS].set(b_group).at[0, EXPERT_LANE0:EXPERT_LANE0 + N_EXPERTS].set(b_expert)
    u, route, route_t, count = _router(h, g, wr, br)

    n_blocks = (T * EXPERT_TOPK) // MOE_BLOCK + N_EXPERTS
    counts = count[EXPERT_LANE0:EXPERT_LANE0 + N_EXPERTS, 0].astype(I32)
    padded = ((counts + MOE_BLOCK - 1) // MOE_BLOCK) * MOE_BLOCK
    pend = jnp.cumsum(padded)
    pstart = pend - padded
    expert_ids = jnp.arange(N_EXPERTS, dtype=F32)[:, None]

    def slot_of(expert_row, rank_row):
        segment_start = jnp.sum(jnp.where(expert_row[None, :] == expert_ids, pstart[:, None], 0), axis=0)
        return segment_start + rank_row.astype(I32)

    dest0 = slot_of(route_t[0], route_t[4])
    dest1 = slot_of(route_t[1], route_t[5])
    block_row0 = jnp.arange(n_blocks, dtype=I32) * MOE_BLOCK
    blk_e = jnp.minimum(jnp.sum((pend[None, :] <= block_row0[:, None]).astype(I32), axis=1), N_EXPERTS - 1)
    n_used = (pend[-1:] // MOE_BLOCK).astype(I32)

    xs = _dispatch(dest0, dest1, pstart, pend, u, n_blocks * MOE_BLOCK)
    ys = _experts(blk_e, n_used, xs, w1, w3, w2, layer)
    return _combine(dest0, dest1, h, route, ys, final_g, normalize)


def _split_w_in(w_in):
    gla_n = GLA_HEADS * (2 * GLA_DK + 2 * GLA_DV) + GLA_LOWRANK
    swa_n = (SWA_HEADS + 2 * SWA_KV_HEADS) * SWA_HD
    moba_n = 3 * MOBA_HEADS * MOBA_HD
    ret_n = RET_HEADS * (2 * RET_DK + 2 * RET_DV)
    offs = np.cumsum([0, gla_n, swa_n, moba_n, ret_n, GATE_W])
    parts = [w_in[:, offs[i]:offs[i + 1]].astype(BF16) for i in range(5)]
    parts[0] = jnp.pad(parts[0], ((0, 0), (0, GLA_W - gla_n)))
    parts[4] = parts[4] * 0.5
    return parts


def _token_mixer(h, seq, ln1_g, w_in, gla_w_a2, gla_b_a, gla_norm_g, swa_sinks, ret_norm_g, w_branch, w_out,
                 rope):
    z_gla, z_swa, z_moba, z_ret, half_gates = _in_proj(h, ln1_g.reshape(1, D_MODEL), _split_w_in(w_in))
    wa = jnp.zeros((128, 128), F32).at[:GLA_LOWRANK].set(gla_w_a2)
    o_gla = _gla(z_gla, wa, gla_b_a.reshape(1, 128), jnp.tile(gla_norm_g, GLA_HEADS).reshape(1, 256), seq)
    o_swa = _swa(z_swa, swa_sinks, seq)
    o_moba = _moba(z_moba, seq)
    o_ret = _retention(z_ret, *rope, jnp.tile(ret_norm_g, RET_HEADS).reshape(1, 256), seq)
    half_w_branch = w_branch.astype(BF16) * 0.5
    return _merge(h, (o_gla, o_swa, o_moba, o_ret), half_gates, half_w_branch, w_out.astype(BF16))


def kernel(x, ln1_g, w_in, gla_w_a2, gla_b_a, gla_norm_g, swa_sinks, ret_norm_g, w_branch, w_out, ln2_g,
           w_group, b_group, w_expert, b_expert, w1, w3, w2, final_g):
    batch, seq, _ = x.shape
    depth = w_in.shape[0]
    assert depth >= 1
    assert x.shape[2] == D_MODEL and seq % (MOBA_QBLOCKS * MOBA_BLOCK) == 0 and seq % MIXER_TILE == 0
    assert (batch * seq) % DISPATCH_TILE == 0 and (batch * seq * EXPERT_TOPK) % MOE_BLOCK == 0
    rope = _rope_tables(seq)
    h = x.reshape(batch * seq, D_MODEL)
    for l in range(depth):
        h = _token_mixer(h, seq, ln1_g[l], w_in[l], gla_w_a2[l], gla_b_a[l], gla_norm_g[l], swa_sinks[l],
                         ret_norm_g[l], w_branch[l], w_out[l], rope)
        h = _moe(h, ln2_g[l].reshape(1, D_MODEL), w_group[l], b_group[l], w_expert[l], b_expert[l],
                 w1, w3, w2, l, final_g.reshape(1, D_MODEL), normalize=(l == depth - 1))
    return h.reshape(batch, seq, D_MODEL)
```

```python
import functools
import math

import numpy as np
import jax
import jax.numpy as jnp
from jax import lax
from jax.experimental import pallas as pl
from jax.experimental.pallas import tpu as pltpu

F32 = jnp.float32
BF16 = jnp.bfloat16
I32 = jnp.int32
HIGHEST = lax.Precision.HIGHEST

D_MODEL = 1024
N_BRANCH = 4
NORM_EPS = 1e-6
NEG_INF = -1e30
BELOW_NEG_INF = -3e38
LOG2_E = math.log2(math.e)

GLA_HEADS, GLA_DK, GLA_DV, GLA_LOWRANK, GLA_TAU, GLA_CHUNK = 4, 32, 64, 16, 16.0, 64
SWA_HEADS, SWA_KV_HEADS, SWA_HD, SWA_WINDOW = 4, 2, 64, 128
MOBA_HEADS, MOBA_HD, MOBA_BLOCK, MOBA_TOPK = 4, 64, 256, 3
RET_HEADS, RET_DK, RET_DV, RET_CHUNK, RET_ROPE_BASE = 4, 32, 64, 128, 10000.0
N_GROUPS, EXPERTS_PER_GROUP, EXPERT_TOPK, EXPERT_HIDDEN = 4, 8, 2, 256
MOE_BLOCK = 512
N_EXPERTS = N_GROUPS * EXPERTS_PER_GROUP

GLA_W = 896
SWA_W = 512
MOBA_W = 768
RET_W = 768
GATE_W = N_BRANCH * D_MODEL

LANES = 128
EXPERT_LANE0 = 32

VMEM_LIMIT = 56 * 1024 * 1024

TOK_TILE = 512
MIXER_TILE = 2048
ROW_TILE = 512
DISPATCH_TILE = 2048


def _params(*sem):
    return pltpu.CompilerParams(dimension_semantics=sem, vmem_limit_bytes=VMEM_LIMIT)


def _dot(a, b, precision=None):
    return jnp.dot(a, b, preferred_element_type=F32, precision=precision)


def _dot_nt(a, b, precision=None):
    return lax.dot_general(a, b, (((1,), (1,)), ((), ())), preferred_element_type=F32, precision=precision)


def _dot_tn(a, b, precision=None):
    return lax.dot_general(a, b, (((0,), (0,)), ((), ())), preferred_element_type=F32, precision=precision)


def _split_bf16(x, parts):
    out = []
    for _ in range(parts - 1):
        piece = x.astype(BF16)
        out.append(piece)
        x = x - piece.astype(F32)
    out.append(x.astype(BF16))
    return out


def _dot_f32_by_bf16(x, m):
    return sum(_dot(piece, m) for piece in _split_bf16(x, 3))


def _dot_bf16_by_f32(m, x):
    return sum(_dot(m, piece) for piece in _split_bf16(x, 3))


def _iota(shape, dim):
    return lax.broadcasted_iota(I32, shape, dim)


def _sigmoid(x):
    return 0.5 * jnp.tanh(0.5 * x) + 0.5


def _rms(x, g):
    ms = jnp.mean(x * x, axis=-1, keepdims=True)
    return x * lax.rsqrt(ms + NORM_EPS) * g


def _full(shape):
    return pl.BlockSpec(shape, lambda *_: (0,) * len(shape))


def _in_proj_kernel(h_ref, g_ref, w0, w1, w2, w3, w4, o0, o1, o2, o3, o4):
    u = _rms(h_ref[...], g_ref[...]).astype(BF16)
    for w, o in ((w0, o0), (w1, o1), (w2, o2), (w3, o3), (w4, o4)):
        o[...] = _dot(u, w[...]).astype(BF16)


def _in_proj(h, g, ws):
    T = h.shape[0]
    tm = TOK_TILE
    widths = [w.shape[1] for w in ws]
    return pl.pallas_call(
        _in_proj_kernel,
        grid=(T // tm,),
        in_specs=[pl.BlockSpec((tm, D_MODEL), lambda i: (i, 0)), _full((1, D_MODEL))]
        + [pl.BlockSpec((D_MODEL, n), lambda i: (0, 0), pipeline_mode=pl.Buffered(1)) for n in widths],
        out_specs=[pl.BlockSpec((tm, n), lambda i: (i, 0)) for n in widths],
        out_shape=[jax.ShapeDtypeStruct((T, n), BF16) for n in widths],
        compiler_params=_params("parallel"),
        name="in_proj",
    )(h, g, *ws)


def _head_select(x, lane_head, n_heads, rows):
    out = None
    for hh in range(n_heads):
        term = jnp.where(lane_head == hh, x[hh * rows:(hh + 1) * rows], 0.0)
        out = term if out is None else out + term
    return out


def _head_norm_gate(o, gate_in, g):
    n = o.shape[1]
    same_head = (_iota((n, n), 0) >> 6) == (_iota((n, n), 1) >> 6)
    ms = _dot_f32_by_bf16(o * o, same_head.astype(BF16)) * (1.0 / 64.0)
    return o * lax.rsqrt(ms + NORM_EPS) * g * (gate_in * _sigmoid(gate_in))


def _gla_kernel(z_ref, wa_ref, ba_ref, g_ref, o_ref, st_ref, oacc_ref, att_ref, kv_ref, *, tiles_per_seq):
    C = GLA_CHUNK
    tt = z_ref.shape[0]

    @pl.when(pl.program_id(0) % tiles_per_seq == 0)
    def _():
        st_ref[...] = jnp.zeros_like(st_ref)

    q = z_ref[:, 0:128].astype(F32)
    k = z_ref[:, 128:256].astype(F32)
    r = z_ref[:, 512:768].astype(F32)
    pre = _dot_bf16_by_f32(z_ref[:, 768:896], wa_ref[...]) + ba_ref[...]
    log_a = (jnp.minimum(pre, 0.0) - jnp.log(1.0 + jnp.exp(-jnp.abs(pre)))) * (1.0 / GLA_TAU)

    lower = (_iota((C, C), 1) <= _iota((C, C), 0)).astype(BF16)
    b = jnp.concatenate([_dot_bf16_by_f32(lower, log_a[c * C:(c + 1) * C]) for c in range(tt // C)], axis=0)
    b_last = jnp.broadcast_to(b.reshape(tt // C, C, 128)[:, C - 1:C, :], (tt // C, C, 128)).reshape(tt, 128)
    q_dec = (q * (GLA_DK ** -0.5) * jnp.exp(b)).astype(BF16)
    k_dec = (k * jnp.exp(-b)).astype(BF16)
    k_end = (k * jnp.exp(b_last - b)).astype(BF16)
    decay = jnp.exp(b_last)

    qk_head = _iota((C, 128), 1) >> 5
    v_head = _iota((C, 256), 1) >> 6
    causal = _iota((4 * C, C), 1) <= (_iota((4 * C, C), 0) & (C - 1))
    state_mask = (_iota((256, 128), 0) >> 6) == (_iota((256, 128), 1) >> 5)

    for c in range(tt // C):
        sl = slice(c * C, (c + 1) * C)
        q_stack = jnp.concatenate([jnp.where(qk_head == hh, q_dec[sl], 0) for hh in range(GLA_HEADS)], axis=0)
        att_ref[c] = jnp.where(causal, _dot_nt(q_stack, k_dec[sl]), 0.0).astype(BF16)
        kv_ref[c] = jnp.where(state_mask, _dot_tn(z_ref[sl, 256:512], k_end[sl]), 0.0)
    state = st_ref[...]
    for c in range(tt // C):
        sl = slice(c * C, (c + 1) * C)
        intra = _head_select(_dot(att_ref[c], z_ref[sl, 256:512]), v_head, GLA_HEADS, C)
        inter = _dot_nt(q_dec[sl], state.astype(BF16))
        oacc_ref[sl, :] = intra + inter
        state = state * decay[c * C:c * C + 1] + kv_ref[c]
    st_ref[...] = state

    o_ref[...] = _head_norm_gate(oacc_ref[...], r, g_ref[...]).astype(BF16)


def _gla(z, wa, ba, g, seq):
    T = z.shape[0]
    tt = MIXER_TILE
    return pl.pallas_call(
        functools.partial(_gla_kernel, tiles_per_seq=seq // tt),
        grid=(T // tt,),
        in_specs=[pl.BlockSpec((tt, GLA_W), lambda i: (i, 0)), _full((128, 128)), _full((1, 128)),
                  _full((1, 256))],
        out_specs=pl.BlockSpec((tt, 256), lambda i: (i, 0)),
        out_shape=jax.ShapeDtypeStruct((T, 256), BF16),
        scratch_shapes=[pltpu.VMEM((256, 128), F32), pltpu.VMEM((tt, 256), F32),
                        pltpu.VMEM((tt // GLA_CHUNK, GLA_HEADS * GLA_CHUNK, GLA_CHUNK), BF16),
                        pltpu.VMEM((tt // GLA_CHUNK, 256, 128), F32)],
        compiler_params=_params("arbitrary"),
        name="gla_mixer",
    )(z, wa, ba, g)


_RET_LOG_GAMMA = [math.log1p(-(2.0 ** (-5.0 - hh))) for hh in range(RET_HEADS)]


def _by_head(head, values):
    out = jnp.full(head.shape, values[-1], F32)
    for hh in range(len(values) - 2, -1, -1):
        out = jnp.where(head == hh, values[hh], out)
    return out


def _ret_kernel(z_ref, cos_ref, sina_ref, sinb_ref, g_ref, o_ref, st_ref, oacc_ref, *, tiles_per_seq):
    C = RET_CHUNK
    tt = z_ref.shape[0]

    @pl.when(pl.program_id(0) % tiles_per_seq == 0)
    def _():
        st_ref[...] = jnp.zeros_like(st_ref)

    cos, sina, sinb = cos_ref[...], sina_ref[...], sinb_ref[...]

    def rotate(t):
        return t * cos + pltpu.roll(t, 112, 1) * sina + pltpu.roll(t, 16, 1) * sinb

    q_rot = rotate(z_ref[:, 0:128].astype(F32))
    k_rot = rotate(z_ref[:, 128:256].astype(F32)) * (RET_DK ** -0.5)
    gate_in = z_ref[:, 512:768].astype(F32)

    qk_head = _iota((C, 128), 1) >> 5
    v_head = _iota((C, 256), 1) >> 6
    lg_qk = _by_head(_iota((1, 128), 1) >> 5, _RET_LOG_GAMMA)
    lg_v = _by_head(_iota((1, 256), 1) >> 6, _RET_LOG_GAMMA)
    pos_qk = _iota((C, 128), 0).astype(F32)
    pos_v = _iota((C, 256), 0).astype(F32)
    key_decay = jnp.exp((C - 1.0 - pos_qk) * lg_qk)
    query_decay = jnp.exp((pos_v + 1.0) * lg_v)
    chunk_decay = jnp.exp(float(C) * lg_qk)
    srow = _iota((4 * C, C), 0)
    rel = ((srow & (C - 1)) - _iota((4 * C, C), 1)).astype(F32)
    decay_mat = jnp.where(rel >= 0, jnp.exp(jnp.maximum(rel, 0.0) * _by_head(srow >> 7, _RET_LOG_GAMMA)), 0.0)
    state_mask = (_iota((256, 128), 0) >> 6) == (_iota((256, 128), 1) >> 5)

    q_bf = q_rot.astype(BF16)
    k_bf = k_rot.astype(BF16)
    state = st_ref[...]
    for c in range(tt // C):
        sl = slice(c * C, (c + 1) * C)
        qc = q_bf[sl]
        v = z_ref[sl, 256:512]
        q_stack = jnp.concatenate([jnp.where(qk_head == hh, qc, 0) for hh in range(RET_HEADS)], axis=0)
        att = _dot_nt(q_stack, k_bf[sl]) * decay_mat
        intra = _head_select(_dot(att.astype(BF16), v), v_head, RET_HEADS, C)
        inter = _dot_nt(qc, state.astype(BF16)) * query_decay
        oacc_ref[sl, :] = intra + inter
        kv = _dot_tn(v, (k_rot[sl] * key_decay).astype(BF16))
        state = state * chunk_decay + jnp.where(state_mask, kv, 0.0)
    st_ref[...] = state

    o_ref[...] = _head_norm_gate(oacc_ref[...], gate_in, g_ref[...]).astype(BF16)


def _retention(z, cos, sina, sinb, g, seq):
    T = z.shape[0]
    tt = MIXER_TILE
    tps = seq // tt
    pos_spec = pl.BlockSpec((tt, 128), lambda i: (i % tps, 0))
    return pl.pallas_call(
        functools.partial(_ret_kernel, tiles_per_seq=tps),
        grid=(T // tt,),
        in_specs=[pl.BlockSpec((tt, RET_W), lambda i: (i, 0)), pos_spec, pos_spec, pos_spec, _full((1, 256))],
        out_specs=pl.BlockSpec((tt, 256), lambda i: (i, 0)),
        out_shape=jax.ShapeDtypeStruct((T, 256), BF16),
        scratch_shapes=[pltpu.VMEM((256, 128), F32), pltpu.VMEM((tt, 256), F32)],
        compiler_params=_params("arbitrary"),
        name="retention_mixer",
    )(z, cos, sina, sinb, g)


def _rope_tables(seq):
    half = RET_DK // 2
    pos = jnp.arange(seq, dtype=F32)
    inv_freq = RET_ROPE_BASE ** (-jnp.arange(half, dtype=F32) * 2.0 / RET_DK)
    ang = pos[:, None] * inv_freq[None, :]
    cos = jnp.tile(jnp.cos(ang), (1, LANES // half))
    sin = jnp.tile(jnp.sin(ang), (1, LANES // half))
    first_half = (jnp.arange(LANES) % RET_DK) < half
    return cos, jnp.where(first_half, -sin, 0.0), jnp.where(first_half, 0.0, sin)


def _swa_kernel(sink_ref, cur_ref, prev_ref, o_ref, s_ref, *, tiles_per_seq):
    W = SWA_WINDOW
    HD = SWA_HD
    tt = cur_ref.shape[0]
    first_key = jnp.where(pl.program_id(0) % tiles_per_seq == 0, W, 0)
    sj = _iota((2 * W, W), 0)
    qi = _iota((2 * W, W), 1)
    in_window = (sj > qi) & (sj <= qi + W)
    kv0 = SWA_HEADS * HD
    no_head = jnp.zeros((HD, W), BF16)
    n_blocks = tt // W

    def band(c, col0):
        prev = prev_ref[:, col0:col0 + 128] if c == 0 else cur_ref[(c - 1) * W:c * W, col0:col0 + 128]
        return jnp.concatenate([prev, cur_ref[c * W:(c + 1) * W, col0:col0 + 128]], axis=0)

    for c in range(n_blocks):
        mask = in_window & (sj >= first_key) if c == 0 else in_window
        k_band = band(c, kv0)
        q_t = cur_ref[c * W:(c + 1) * W, 0:kv0].astype(F32).T.astype(BF16)
        for hh in range(SWA_HEADS):
            kk = hh // (SWA_HEADS // SWA_KV_HEADS)
            qh_t = q_t[hh * HD:(hh + 1) * HD]
            q_on_kv = jnp.concatenate([qh_t, no_head] if kk == 0 else [no_head, qh_t], axis=0)
            s = _dot(k_band, q_on_kv) * (HD ** -0.5 * LOG2_E)
            s_ref[c, hh] = jnp.where(mask, s, NEG_INF)
    for c in range(n_blocks):
        v_t = band(c, kv0 + 128).astype(F32).T.astype(BF16)
        outs = []
        for hh in range(SWA_HEADS):
            kk = hh // (SWA_HEADS // SWA_KV_HEADS)
            s = s_ref[c, hh]
            sink = sink_ref[hh] * LOG2_E
            m = jnp.maximum(jnp.max(s, axis=0, keepdims=True), sink)
            p = jnp.exp2(s - m)
            denom = jnp.sum(p, axis=0, keepdims=True) + jnp.exp2(sink - m)
            outs.append(_dot(v_t[kk * HD:(kk + 1) * HD], p.astype(BF16)) / denom)
        o_ref[c * W:(c + 1) * W, :] = jnp.concatenate(outs, axis=0).T.astype(BF16)


def _swa(z, sinks, seq):
    T = z.shape[0]
    tt = MIXER_TILE
    per = tt // SWA_WINDOW
    return pl.pallas_call(
        functools.partial(_swa_kernel, tiles_per_seq=seq // tt),
        grid=(T // tt,),
        in_specs=[pl.BlockSpec(memory_space=pltpu.SMEM),
                  pl.BlockSpec((tt, SWA_W), lambda i: (i, 0)),
                  pl.BlockSpec((SWA_WINDOW, SWA_W), lambda i: (jnp.maximum(i * per - 1, 0), 0))],
        out_specs=pl.BlockSpec((tt, 256), lambda i: (i, 0)),
        out_shape=jax.ShapeDtypeStruct((T, 256), BF16),
        scratch_shapes=[pltpu.VMEM((per, SWA_HEADS, 2 * SWA_WINDOW, SWA_WINDOW), F32)],
        compiler_params=_params("parallel"),
        name="swa_mixer",
    )(sinks, z, z)


_KMEAN_BLOCKS = 8


def _kmean_kernel(k_ref, o_ref):
    k = k_ref[...].astype(F32).reshape(_KMEAN_BLOCKS, MOBA_BLOCK, 256)
    o_ref[...] = jnp.mean(k, axis=1)


def _moba_kmean(z):
    T = z.shape[0]
    rows = _KMEAN_BLOCKS * MOBA_BLOCK
    return pl.pallas_call(
        _kmean_kernel,
        grid=(T // rows,),
        in_specs=[pl.BlockSpec((rows, 256), lambda i: (i, 1))],
        out_specs=pl.BlockSpec((_KMEAN_BLOCKS, 256), lambda i: (i, 0)),
        out_shape=jax.ShapeDtypeStruct((T // MOBA_BLOCK, 256), F32),
        compiler_params=_params("parallel"),
        name="moba_kmean",
    )(z)


MOBA_VROWS = 80
MOBA_PREP_BLOCKS = 4


def _moba_prep_kernel(z_ref, km_ref, qt_out, k_out, vt_out):
    HD = MOBA_HD
    tq = MOBA_BLOCK
    nb = km_ref.shape[0]
    blk_i = _iota((nb, tq), 0)
    blk = blk_i.astype(F32)
    pad = LANES - HD - nb
    ones_row = (_iota((MOBA_VROWS - HD, tq), 0) == 0).astype(BF16)
    for sb in range(MOBA_PREP_BLOCKS):
        rows = slice(sb * tq, (sb + 1) * tq)
        qb = (pl.program_id(0) * MOBA_PREP_BLOCKS + sb) % nb
        this_block_lanes = (_iota((tq, nb), 1) == qb).astype(BF16)
        for hh in range(MOBA_HEADS):
            cols = slice(hh * HD, (hh + 1) * HD)
            q_t = z_ref[rows, cols].astype(F32).T
            gate = _dot(km_ref[:, cols], q_t, precision=HIGHEST)
            gate = jnp.where(blk_i < qb, gate, NEG_INF)
            keep = jnp.where(blk_i == qb, 1.0, 0.0)
            for _ in range(MOBA_TOPK):
                best = jnp.max(gate, axis=0, keepdims=True)
                first = jnp.min(jnp.where(gate == best, blk, float(nb)), axis=0, keepdims=True)
                hit = blk == first
                keep = keep + jnp.where(hit, jnp.where(best > 0.5 * NEG_INF, 1.0, 0.0), 0.0)
                gate = jnp.where(hit, BELOW_NEG_INF, gate)
            bias = ((keep - 1.0) * (-NEG_INF)).astype(BF16)
            qt_out[hh, sb] = jnp.concatenate(
                [(q_t * (HD ** -0.5 * LOG2_E)).astype(BF16), bias, jnp.zeros((pad, tq), BF16)], axis=0)
            k_out[hh, rows, :] = jnp.concatenate(
                [z_ref[rows, 256 + hh * HD:256 + (hh + 1) * HD], this_block_lanes, jnp.zeros((tq, pad), BF16)],
                axis=1)
            v_t = z_ref[rows, 512 + hh * HD:512 + (hh + 1) * HD].astype(F32).T.astype(BF16)
            vt_out[hh, sb] = jnp.concatenate([v_t, ones_row], axis=0)


def _moba_prep(z, kmean, seq):
    T = z.shape[0]
    nb = seq // MOBA_BLOCK
    assert MOBA_HD + nb <= LANES and nb % MOBA_PREP_BLOCKS == 0
    tq = MOBA_BLOCK
    tile = MOBA_PREP_BLOCKS * tq
    H = MOBA_HEADS

    def per_block_t(rows):
        return pl.BlockSpec((H, MOBA_PREP_BLOCKS, rows, tq), lambda i: (0, i, 0, 0))

    return pl.pallas_call(
        _moba_prep_kernel,
        grid=(T // tile,),
        in_specs=[pl.BlockSpec((tile, MOBA_W), lambda i: (i, 0)),
                  pl.BlockSpec((nb, 256), lambda i: (i * MOBA_PREP_BLOCKS // nb, 0))],
        out_specs=[per_block_t(LANES), pl.BlockSpec((H, tile, LANES), lambda i: (0, i, 0)),
                   per_block_t(MOBA_VROWS)],
        out_shape=[jax.ShapeDtypeStruct((H, T // tq, LANES, tq), BF16), jax.ShapeDtypeStruct((H, T, LANES), BF16),
                   jax.ShapeDtypeStruct((H, T // tq, MOBA_VROWS, tq), BF16)],
        compiler_params=_params("parallel"),
        name="moba_prep",
    )(z, kmean)


MOBA_QBLOCKS = 8
MOBA_KBLOCKS = 4
MOBA_DIAGONALS = MOBA_QBLOCKS // MOBA_KBLOCKS


def _moba_kernel(qidx_ref, kidx_ref, kind_ref, qt_ref, k_ref, vt_ref, o_ref, m_ref, acc_ref, s_ref, smax_ref):
    HD = MOBA_HD
    B = MOBA_BLOCK
    step = pl.program_id(1)
    kind = kind_ref[step]
    is_last = (step + 1 == pl.num_programs(1)) | (qidx_ref[jnp.minimum(step + 1, pl.num_programs(1) - 1)]
                                                  != qidx_ref[step])

    def attend(diagonal):
        causal = _iota((B, B), 0) <= _iota((B, B), 1)

        def key_blocks_of(qi):
            if diagonal is None:
                return list(range(MOBA_KBLOCKS))
            return [c for c in range(MOBA_KBLOCKS) if MOBA_KBLOCKS * diagonal + c <= qi]

        for hh in range(MOBA_HEADS):
            for qi in range(MOBA_QBLOCKS):
                for c in key_blocks_of(qi):
                    s = _dot(k_ref[hh, c * B:(c + 1) * B, :], qt_ref[hh, qi])
                    if diagonal is not None and MOBA_KBLOCKS * diagonal + c == qi:
                        s = jnp.where(causal, s, NEG_INF)
                    s_ref[hh, qi, c] = s
                    smax_ref[hh, qi, c] = jnp.max(s, axis=0, keepdims=True)
        first_visit = diagonal == 0
        for hh in range(MOBA_HEADS):
            for qi in range(MOBA_QBLOCKS):
                key_blocks = key_blocks_of(qi)
                if not key_blocks:
                    continue
                m_new = functools.reduce(jnp.maximum, [smax_ref[hh, qi, c] for c in key_blocks])
                if not first_visit:
                    m_old = m_ref[hh, qi]
                    m_new = jnp.maximum(m_old, m_new)
                pv = None
                for c in key_blocks:
                    p = jnp.exp2(s_ref[hh, qi, c] - m_new).astype(BF16)
                    term = _dot(vt_ref[hh, c], p)
                    pv = term if pv is None else pv + term
                acc_ref[hh, qi] = pv if first_visit else jnp.exp2(m_old - m_new) * acc_ref[hh, qi] + pv
                m_ref[hh, qi] = m_new

    @pl.when(kind == 0)
    def _():
        attend(None)

    for d in range(MOBA_DIAGONALS):
        @pl.when(kind == 1 + d)
        def _(d=d):
            attend(d)

    @pl.when(is_last)
    def _():
        for qi in range(MOBA_QBLOCKS):
            heads = [acc_ref[hh, qi, 0:HD, :] / acc_ref[hh, qi, HD:HD + 1, :] for hh in range(MOBA_HEADS)]
            o_ref[qi * B:(qi + 1) * B, :] = jnp.concatenate(heads, axis=0).T.astype(BF16)


def _moba(z, seq):
    T = z.shape[0]
    nb = seq // MOBA_BLOCK
    assert nb % MOBA_QBLOCKS == 0
    q_tiles = nb // MOBA_QBLOCKS
    k_steps = nb // MOBA_KBLOCKS
    D = MOBA_DIAGONALS
    batch = T // seq
    qt_aug, k_aug, vt_aug = _moba_prep(z, _moba_kmean(z), seq)
    qidx = np.concatenate([np.full(D * t + D, t) for t in range(q_tiles)]).astype(np.int32)
    kidx = np.concatenate([np.concatenate([D * t + np.arange(D), np.arange(D * t)])
                           for t in range(q_tiles)]).astype(np.int32)
    kind = np.concatenate([np.concatenate([1 + np.arange(D), np.zeros(D * t, np.int64)])
                           for t in range(q_tiles)]).astype(np.int32)
    tq = MOBA_QBLOCKS * MOBA_BLOCK
    tk = MOBA_KBLOCKS * MOBA_BLOCK
    H = MOBA_HEADS
    grid_spec = pltpu.PrefetchScalarGridSpec(
        num_scalar_prefetch=3,
        grid=(batch, len(qidx)),
        in_specs=[pl.BlockSpec((H, MOBA_QBLOCKS, LANES, MOBA_BLOCK),
                               lambda b, s, qi, ki, kd: (0, b * q_tiles + qi[s], 0, 0)),
                  pl.BlockSpec((H, tk, LANES), lambda b, s, qi, ki, kd: (0, b * k_steps + ki[s], 0)),
                  pl.BlockSpec((H, MOBA_KBLOCKS, MOBA_VROWS, MOBA_BLOCK),
                               lambda b, s, qi, ki, kd: (0, b * k_steps + ki[s], 0, 0))],
        out_specs=pl.BlockSpec((tq, 256), lambda b, s, qi, ki, kd: (b * q_tiles + qi[s], 0)),
        scratch_shapes=[pltpu.VMEM((H, MOBA_QBLOCKS, 1, MOBA_BLOCK), F32),
                        pltpu.VMEM((H, MOBA_QBLOCKS, MOBA_VROWS, MOBA_BLOCK), F32),
                        pltpu.VMEM((H, MOBA_QBLOCKS, MOBA_KBLOCKS, MOBA_BLOCK, MOBA_BLOCK), F32),
                        pltpu.VMEM((H, MOBA_QBLOCKS, MOBA_KBLOCKS, 1, MOBA_BLOCK), F32)],
    )
    return pl.pallas_call(
        _moba_kernel,
        grid_spec=grid_spec,
        out_shape=jax.ShapeDtypeStruct((T, 256), BF16),
        compiler_params=_params("parallel", "arbitrary"),
        name="moba_mixer",
    )(jnp.asarray(qidx), jnp.asarray(kidx), jnp.asarray(kind), qt_aug, k_aug, vt_aug)


def _merge_kernel(h_ref, o0, o1, o2, o3, half_gates_ref, half_wb_ref, wout_ref, out_ref):
    merged = None
    for i, o in enumerate((o0, o1, o2, o3)):
        half_y = _dot(o[...], half_wb_ref[i])
        term = half_y * jnp.tanh(half_gates_ref[:, i * D_MODEL:(i + 1) * D_MODEL].astype(F32)) + half_y
        merged = term if merged is None else merged + term
    out_ref[...] = h_ref[...] + _dot(merged.astype(BF16), wout_ref[...])


def _merge(h, branches, gates, wb, wout):
    T = h.shape[0]
    tm = TOK_TILE
    return pl.pallas_call(
        _merge_kernel,
        grid=(T // tm,),
        in_specs=[pl.BlockSpec((tm, D_MODEL), lambda i: (i, 0))]
        + [pl.BlockSpec((tm, 256), lambda i: (i, 0))] * 4
        + [pl.BlockSpec((tm, GATE_W), lambda i: (i, 0)),
           pl.BlockSpec((N_BRANCH, 256, D_MODEL), lambda i: (0, 0, 0), pipeline_mode=pl.Buffered(1)),
           pl.BlockSpec((D_MODEL, D_MODEL), lambda i: (0, 0), pipeline_mode=pl.Buffered(1))],
        out_specs=pl.BlockSpec((tm, D_MODEL), lambda i: (i, 0)),
        out_shape=jax.ShapeDtypeStruct((T, D_MODEL), F32),
        compiler_params=_params("parallel"),
        name="merge_out_proj",
    )(h, *branches, gates, wb, wout)


ROW_TILES = D_MODEL // (2 * LANES)
HIGH_HALF_MASK = -65536


def _slab_columns(j):
    first = 2 * j * LANES
    return slice(first, first + LANES), slice(first + LANES, first + 2 * LANES)


def _row_tile_spec(rows, index_map):
    return pl.BlockSpec((rows * ROW_TILES, LANES), index_map)


def _bf16_bits(x):
    return lax.bitcast_convert_type(x.astype(BF16).astype(F32), I32)


def _pack_rows(ref, x):
    for j in range(ROW_TILES):
        low_cols, high_cols = _slab_columns(j)
        low = _bf16_bits(x[:, low_cols])
        high = _bf16_bits(x[:, high_cols])
        ref[pl.ds(j, x.shape[0], stride=ROW_TILES), :] = high | lax.shift_right_logical(low, 16)


def _unpack_slab(ref, j):
    words = ref[pl.ds(j, ref.shape[0] // ROW_TILES, stride=ROW_TILES), :]
    low = lax.bitcast_convert_type(lax.shift_left(words, 16), F32)
    high = lax.bitcast_convert_type(words & HIGH_HALF_MASK, F32)
    return low, high


def _unpack_rows(ref):
    pieces = [piece for j in range(ROW_TILES) for piece in _unpack_slab(ref, j)]
    return jnp.concatenate(pieces, axis=1).astype(BF16)


def _router_kernel(h_ref, g_ref, wr_ref, br_ref, u_ref, route_ref, route_t_ref, count_ref, carry_ref):
    tm = h_ref.shape[0]

    @pl.when(pl.program_id(0) == 0)
    def _():
        carry_ref[...] = jnp.zeros_like(carry_ref)

    u = _rms(h_ref[...], g_ref[...])
    _pack_rows(u_ref, u)
    u_hi, u_lo = _split_bf16(u, 2)
    w_hi, w_lo = _split_bf16(wr_ref[...], 2)
    logits = _dot(u_hi, w_hi) + (_dot(u_lo, w_hi) + _dot(u_hi, w_lo)) + br_ref[...]
    lt = logits.T
    row_i = _iota((LANES, tm), 0)
    row = row_i.astype(F32)
    row_group = ((row_i - EXPERT_LANE0) >> 3).astype(F32)

    def first_row(mask):
        return jnp.min(jnp.where(mask, row, float(LANES)), axis=0, keepdims=True)

    grp = jnp.where(row_i < N_GROUPS, lt, BELOW_NEG_INF)
    grp_e = jnp.exp(grp - jnp.max(grp, axis=0, keepdims=True))
    p_group = grp_e / jnp.sum(grp_e, axis=0, keepdims=True)
    g_w = jnp.max(p_group, axis=0, keepdims=True)
    g_sel = first_row(p_group == g_w)

    in_group = (row_i >= EXPERT_LANE0) & (row_group == g_sel)
    el = jnp.where(in_group, lt, BELOW_NEG_INF)
    top1 = jnp.max(el, axis=0, keepdims=True)
    row1 = first_row(el == top1)
    el = jnp.where(row == row1, BELOW_NEG_INF, el)
    top2 = jnp.max(el, axis=0, keepdims=True)
    row2 = first_row(el == top2)
    e2 = jnp.exp(top2 - top1)
    w1 = g_w / (1.0 + e2)
    w2 = g_w * e2 / (1.0 + e2)

    onehot = ((row == row1) | (row == row2)).astype(BF16)
    earlier = (_iota((tm, tm), 0) < _iota((tm, tm), 1)).astype(BF16)
    seen = _dot(onehot, earlier) + carry_ref[:, 0:1]
    rank1 = jnp.sum(jnp.where(row == row1, seen, 0.0), axis=0, keepdims=True)
    rank2 = jnp.sum(jnp.where(row == row2, seen, 0.0), axis=0, keepdims=True)
    carry_ref[...] += jnp.sum(onehot.astype(F32), axis=1, keepdims=True)
    count_ref[...] = carry_ref[...]

    out_t = jnp.zeros((LANES, tm), F32)
    for idx, val in enumerate((row1 - EXPERT_LANE0, row2 - EXPERT_LANE0, w1, w2, rank1, rank2)):
        out_t = jnp.where(row_i == idx, val, out_t)
    route_t_ref[...] = out_t[0:8, :]
    route_ref[...] = out_t.T


def _router(h, g, wr, br):
    T = h.shape[0]
    tm = TOK_TILE
    return pl.pallas_call(
        _router_kernel,
        grid=(T // tm,),
        in_specs=[pl.BlockSpec((tm, D_MODEL), lambda i: (i, 0)), _full((1, D_MODEL)),
                  _full((D_MODEL, LANES)), _full((1, LANES))],
        out_specs=[_row_tile_spec(tm, lambda i: (i, 0)), pl.BlockSpec((tm, LANES), lambda i: (i, 0)),
                   pl.BlockSpec((8, tm), lambda i: (0, i)), _full((LANES, LANES))],
        out_shape=[jax.ShapeDtypeStruct((T * ROW_TILES, LANES), I32), jax.ShapeDtypeStruct((T, LANES), F32),
                   jax.ShapeDtypeStruct((8, T), F32), jax.ShapeDtypeStruct((LANES, LANES), F32)],
        scratch_shapes=[pltpu.VMEM((LANES, LANES), F32)],
        compiler_params=_params("arbitrary"),
        name="moe_router",
    )(h, g, wr, br)


ROW_UNROLL = 8


def _row_copy(src_ref, src_row, dst_ref, dst_row, sem):
    src = src_ref.at[pl.ds(pl.multiple_of(src_row * ROW_TILES, ROW_TILES), ROW_TILES)]
    dst = dst_ref.at[pl.ds(pl.multiple_of(dst_row * ROW_TILES, ROW_TILES), ROW_TILES)]
    return pltpu.make_async_copy(src, dst, sem)


def _for_each_row_pair(rows, body):
    def group(g, carry):
        for i in range(ROW_UNROLL):
            for k in range(EXPERT_TOPK):
                body(g * ROW_UNROLL + i, k)
        return carry

    lax.fori_loop(0, rows // ROW_UNROLL, group, 0)


def _dispatch_kernel(dest0_ref, dest1_ref, pstart_ref, pend_ref, u_ref, xs_ref, zero_ref, sem):
    tm = u_ref.shape[0] // ROW_TILES
    tile = pl.program_id(0)
    dests = (dest0_ref, dest1_ref)

    @pl.when(tile == 0)
    def _():
        zero_ref[...] = jnp.zeros_like(zero_ref)

        def zero_block(first_row):
            start = pl.multiple_of(first_row * ROW_TILES, ROW_TILES)
            fill = pltpu.make_async_copy(zero_ref, xs_ref.at[pl.ds(start, MOE_BLOCK * ROW_TILES)], sem)
            fill.start()
            fill.wait()

        def zero_tail(e, carry):
            @pl.when(pend_ref[e] > pstart_ref[e])
            def _():
                zero_block(pend_ref[e] - MOE_BLOCK)
            return carry

        def zero_unused(b, carry):
            zero_block(b * MOE_BLOCK)
            return carry

        lax.fori_loop(0, N_EXPERTS, zero_tail, 0)
        lax.fori_loop(pend_ref[N_EXPERTS - 1] // MOE_BLOCK, xs_ref.shape[0] // (MOE_BLOCK * ROW_TILES),
                      zero_unused, 0)

    def issue(r, k):
        _row_copy(u_ref, r, xs_ref, dests[k][tile * tm + r], sem).start(priority=k)

    _for_each_row_pair(tm, issue)
    for _ in range(EXPERT_TOPK):
        pltpu.make_async_copy(u_ref, xs_ref.at[pl.ds(0, tm * ROW_TILES)], sem).wait()


def _dispatch(dest0, dest1, pstart, pend, u, n_slots):
    T = u.shape[0] // ROW_TILES
    tm = DISPATCH_TILE
    grid_spec = pltpu.PrefetchScalarGridSpec(
        num_scalar_prefetch=4,
        grid=(T // tm,),
        in_specs=[_row_tile_spec(tm, lambda i, *_: (i, 0))],
        out_specs=pl.BlockSpec(memory_space=pl.ANY),
        scratch_shapes=[pltpu.VMEM((MOE_BLOCK * ROW_TILES, LANES), I32), pltpu.SemaphoreType.DMA(())],
    )
    return pl.pallas_call(
        _dispatch_kernel,
        grid_spec=grid_spec,
        out_shape=jax.ShapeDtypeStruct((n_slots * ROW_TILES, LANES), I32),
        compiler_params=_params("arbitrary"),
        name="moe_dispatch",
    )(dest0, dest1, pstart, pend, u)


def _expert_kernel(blk_e_ref, n_used_ref, x_ref, w1_ref, w3_ref, w2_ref, y_ref, w1b_ref, w3b_ref, w2b_ref):
    i = pl.program_id(0)
    used = i < n_used_ref[0]

    @pl.when((i == 0) | (blk_e_ref[i] != blk_e_ref[jnp.maximum(i - 1, 0)]))
    def _():
        w1b_ref[...] = w1_ref[0, 0].astype(BF16)
        w3b_ref[...] = w3_ref[0, 0].astype(BF16)
        w2b_ref[...] = w2_ref[0, 0].astype(BF16)

    @pl.when(used)
    def _():
        x = _unpack_rows(x_ref)
        a = _dot(x, w1b_ref[...])
        hidden = (a * _sigmoid(a)) * _dot(x, w3b_ref[...])
        _pack_rows(y_ref, _dot(hidden.astype(BF16), w2b_ref[...]))

    @pl.when(jnp.logical_not(used))
    def _():
        y_ref[...] = jnp.zeros_like(y_ref)


def _experts(blk_e, n_used, xs, w1, w3, w2, layer):
    n_slots = xs.shape[0] // ROW_TILES
    bm = MOE_BLOCK
    grid_spec = pltpu.PrefetchScalarGridSpec(
        num_scalar_prefetch=2,
        grid=(n_slots // bm,),
        in_specs=[_row_tile_spec(bm, lambda i, be, nu: (jnp.minimum(i, nu[0] - 1), 0)),
                  pl.BlockSpec((1, 1, D_MODEL, EXPERT_HIDDEN), lambda i, be, nu: (layer, be[i], 0, 0)),
                  pl.BlockSpec((1, 1, D_MODEL, EXPERT_HIDDEN), lambda i, be, nu: (layer, be[i], 0, 0)),
                  pl.BlockSpec((1, 1, EXPERT_HIDDEN, D_MODEL), lambda i, be, nu: (layer, be[i], 0, 0))],
        out_specs=_row_tile_spec(bm, lambda i, be, nu: (i, 0)),
        scratch_shapes=[pltpu.VMEM((D_MODEL, EXPERT_HIDDEN), BF16), pltpu.VMEM((D_MODEL, EXPERT_HIDDEN), BF16),
                        pltpu.VMEM((EXPERT_HIDDEN, D_MODEL), BF16)],
    )
    return pl.pallas_call(
        _expert_kernel,
        grid_spec=grid_spec,
        out_shape=jax.ShapeDtypeStruct((n_slots * ROW_TILES, LANES), I32),
        compiler_params=_params("arbitrary"),
        name="moe_experts",
    )(blk_e, n_used, xs, w1, w3, w2)


def _combine_kernel(dest0_ref, dest1_ref, h_ref, route_ref, g_ref, ys_ref, out_ref, y_ref, sem, *, normalize):
    tm = h_ref.shape[0]
    step = pl.program_id(0)
    slot = step % 2

    dests = (dest0_ref, dest1_ref)

    def gather_tile(tile, buf):
        def issue(r, k):
            _row_copy(ys_ref, dests[k][tile * tm + r], y_ref.at[buf, k], r, sem.at[buf]).start(priority=k)

        _for_each_row_pair(tm, issue)

    @pl.when(step == 0)
    def _():
        gather_tile(0, 0)

    @pl.when(step + 1 < pl.num_programs(0))
    def _():
        gather_tile(step + 1, 1 - slot)

    for k in range(EXPERT_TOPK):
        pltpu.make_async_copy(ys_ref.at[pl.ds(0, tm * ROW_TILES)], y_ref.at[slot, k], sem.at[slot]).wait()
    w0 = route_ref[:, 2:3]
    w1 = route_ref[:, 3:4]
    for j in range(ROW_TILES):
        low0, high0 = _unpack_slab(y_ref.at[slot, 0], j)
        low1, high1 = _unpack_slab(y_ref.at[slot, 1], j)
        low_cols, high_cols = _slab_columns(j)
        out_ref[:, low_cols] = h_ref[:, low_cols] + w0 * low0 + w1 * low1
        out_ref[:, high_cols] = h_ref[:, high_cols] + w0 * high0 + w1 * high1
    if normalize:
        out_ref[...] = _rms(out_ref[...], g_ref[...])


def _combine(dest0, dest1, h, route, ys, final_g, normalize):
    T = h.shape[0]
    tm = ROW_TILE
    grid_spec = pltpu.PrefetchScalarGridSpec(
        num_scalar_prefetch=2,
        grid=(T // tm,),
        in_specs=[pl.BlockSpec((tm, D_MODEL), lambda i, *_: (i, 0)),
                  pl.BlockSpec((tm, LANES), lambda i, *_: (i, 0)),
                  _full((1, D_MODEL)),
                  pl.BlockSpec(memory_space=pl.ANY)],
        out_specs=pl.BlockSpec((tm, D_MODEL), lambda i, *_: (i, 0)),
        scratch_shapes=[pltpu.VMEM((2, EXPERT_TOPK, tm * ROW_TILES, LANES), I32),
                        pltpu.SemaphoreType.DMA((2,))],
    )
    return pl.pallas_call(
        functools.partial(_combine_kernel, normalize=normalize),
        grid_spec=grid_spec,
        out_shape=jax.ShapeDtypeStruct((T, D_MODEL), F32),
        compiler_params=_params("arbitrary"),
        name="moe_combine",
    )(dest0, dest1, h, route, final_g, ys)


def _moe(h, g, w_group, b_group, w_expert, b_expert, w1, w3, w2, layer, final_g, normalize):
    T = h.shape[0]
    wr = jnp.zeros((D_MODEL, LANES), F32)
    wr = wr.at[:, :N_GROUPS].set(w_group).at[:, EXPERT_LANE0:EXPERT_LANE0 + N_EXPERTS].set(w_expert)
    br = jnp.zeros((1, LANES), F32)
    br = br.at[0, :N_GROUPS].set(b_group).at[0, EXPERT_LANE0:EXPERT_LANE0 + N_EXPERTS].set(b_expert)
    u, route, route_t, count = _router(h, g, wr, br)

    n_blocks = (T * EXPERT_TOPK) // MOE_BLOCK + N_EXPERTS
    counts = count[EXPERT_LANE0:EXPERT_LANE0 + N_EXPERTS, 0].astype(I32)
    padded = ((counts + MOE_BLOCK - 1) // MOE_BLOCK) * MOE_BLOCK
    pend = jnp.cumsum(padded)
    pstart = pend - padded
    expert_ids = jnp.arange(N_EXPERTS, dtype=F32)[:, None]

    def slot_of(expert_row, rank_row):
        segment_start = jnp.sum(jnp.where(expert_row[None, :] == expert_ids, pstart[:, None], 0), axis=0)
        return segment_start + rank_row.astype(I32)

    dest0 = slot_of(route_t[0], route_t[4])
    dest1 = slot_of(route_t[1], route_t[5])
    block_row0 = jnp.arange(n_blocks, dtype=I32) * MOE_BLOCK
    blk_e = jnp.minimum(jnp.sum((pend[None, :] <= block_row0[:, None]).astype(I32), axis=1), N_EXPERTS - 1)
    n_used = (pend[-1:] // MOE_BLOCK).astype(I32)

    xs = _dispatch(dest0, dest1, pstart, pend, u, n_blocks * MOE_BLOCK)
    ys = _experts(blk_e, n_used, xs, w1, w3, w2, layer)
    return _combine(dest0, dest1, h, route, ys, final_g, normalize)


def _split_w_in(w_in):
    gla_n = GLA_HEADS * (2 * GLA_DK + 2 * GLA_DV) + GLA_LOWRANK
    swa_n = (SWA_HEADS + 2 * SWA_KV_HEADS) * SWA_HD
    moba_n = 3 * MOBA_HEADS * MOBA_HD
    ret_n = RET_HEADS * (2 * RET_DK + 2 * RET_DV)
    offs = np.cumsum([0, gla_n, swa_n, moba_n, ret_n, GATE_W])
    parts = [w_in[:, offs[i]:offs[i + 1]].astype(BF16) for i in range(5)]
    parts[0] = jnp.pad(parts[0], ((0, 0), (0, GLA_W - gla_n)))
    parts[4] = parts[4] * 0.5
    return parts


def _token_mixer(h, seq, ln1_g, w_in, gla_w_a2, gla_b_a, gla_norm_g, swa_sinks, ret_norm_g, w_branch, w_out,
                 rope):
    z_gla, z_swa, z_moba, z_ret, half_gates = _in_proj(h, ln1_g.reshape(1, D_MODEL), _split_w_in(w_in))
    wa = jnp.zeros((128, 128), F32).at[:GLA_LOWRANK].set(gla_w_a2)
    o_gla = _gla(z_gla, wa, gla_b_a.reshape(1, 128), jnp.tile(gla_norm_g, GLA_HEADS).reshape(1, 256), seq)
    o_swa = _swa(z_swa, swa_sinks, seq)
    o_moba = _moba(z_moba, seq)
    o_ret = _retention(z_ret, *rope, jnp.tile(ret_norm_g, RET_HEADS).reshape(1, 256), seq)
    half_w_branch = w_branch.astype(BF16) * 0.5
    return _merge(h, (o_gla, o_swa, o_moba, o_ret), half_gates, half_w_branch, w_out.astype(BF16))


def kernel(x, ln1_g, w_in, gla_w_a2, gla_b_a, gla_norm_g, swa_sinks, ret_norm_g, w_branch, w_out, ln2_g,
           w_group, b_group, w_expert, b_expert, w1, w3, w2, final_g):
    batch, seq, _ = x.shape
    depth = w_in.shape[0]
    assert depth >= 1
    assert x.shape[2] == D_MODEL and seq % (MOBA_QBLOCKS * MOBA_BLOCK) == 0 and seq % MIXER_TILE == 0
    assert (batch * seq) % DISPATCH_TILE == 0 and (batch * seq * EXPERT_TOPK) % MOE_BLOCK == 0
    rope = _rope_tables(seq)
    h = x.reshape(batch * seq, D_MODEL)
    for l in range(depth):
        h = _token_mixer(h, seq, ln1_g[l], w_in[l], gla_w_a2[l], gla_b_a[l], gla_norm_g[l], swa_sinks[l],
                         ret_norm_g[l], w_branch[l], w_out[l], rope)
        h = _moe(h, ln2_g[l].reshape(1, D_MODEL), w_group[l], b_group[l], w_expert[l], b_expert[l],
                 w1, w3, w2, l, final_g.reshape(1, D_MODEL), normalize=(l == depth - 1))
    return h.reshape(batch, seq, D_MODEL)
```

```python
import functools
import math

import numpy as np
import jax
import jax.numpy as jnp
from jax import lax
from jax.experimental import pallas as pl
from jax.experimental.pallas import tpu as pltpu

F32 = jnp.float32
BF16 = jnp.bfloat16
I32 = jnp.int32
HIGHEST = lax.Precision.HIGHEST

D_MODEL = 1024
N_BRANCH = 4
NORM_EPS = 1e-6
NEG_INF = -1e30
BELOW_NEG_INF = -3e38
LOG2_E = math.log2(math.e)

GLA_HEADS, GLA_DK, GLA_DV, GLA_LOWRANK, GLA_TAU, GLA_CHUNK = 4, 32, 64, 16, 16.0, 64
SWA_HEADS, SWA_KV_HEADS, SWA_HD, SWA_WINDOW = 4, 2, 64, 128
MOBA_HEADS, MOBA_HD, MOBA_BLOCK, MOBA_TOPK = 4, 64, 256, 3
RET_HEADS, RET_DK, RET_DV, RET_CHUNK, RET_ROPE_BASE = 4, 32, 64, 128, 10000.0
N_GROUPS, EXPERTS_PER_GROUP, EXPERT_TOPK, EXPERT_HIDDEN = 4, 8, 2, 256
MOE_BLOCK = 512
N_EXPERTS = N_GROUPS * EXPERTS_PER_GROUP

GLA_W = 896
SWA_W = 512
MOBA_W = 768
RET_W = 768
GATE_W = N_BRANCH * D_MODEL

LANES = 128
EXPERT_LANE0 = 32

VMEM_LIMIT = 56 * 1024 * 1024

TOK_TILE = 512
MIXER_TILE = 2048
ROW_TILE = 512
DISPATCH_TILE = 4096


def _params(*sem):
    return pltpu.CompilerParams(dimension_semantics=sem, vmem_limit_bytes=VMEM_LIMIT)


def _dot(a, b, precision=None):
    return jnp.dot(a, b, preferred_element_type=F32, precision=precision)


def _dot_nt(a, b, precision=None):
    return lax.dot_general(a, b, (((1,), (1,)), ((), ())), preferred_element_type=F32, precision=precision)


def _dot_tn(a, b, precision=None):
    return lax.dot_general(a, b, (((0,), (0,)), ((), ())), preferred_element_type=F32, precision=precision)


def _split_bf16(x, parts):
    out = []
    for _ in range(parts - 1):
        piece = x.astype(BF16)
        out.append(piece)
        x = x - piece.astype(F32)
    out.append(x.astype(BF16))
    return out


def _dot_f32_by_bf16(x, m):
    return sum(_dot(piece, m) for piece in _split_bf16(x, 3))


def _dot_bf16_by_f32(m, x):
    return sum(_dot(m, piece) for piece in _split_bf16(x, 3))


def _iota(shape, dim):
    return lax.broadcasted_iota(I32, shape, dim)


def _sigmoid(x):
    return 0.5 * jnp.tanh(0.5 * x) + 0.5


def _rms(x, g):
    ms = jnp.mean(x * x, axis=-1, keepdims=True)
    return x * lax.rsqrt(ms + NORM_EPS) * g


def _full(shape):
    return pl.BlockSpec(shape, lambda *_: (0,) * len(shape))


def _in_proj_kernel(h_ref, g_ref, w0, w1, w2, w3, w4, o0, o1, o2, o3, o4):
    u = _rms(h_ref[...], g_ref[...]).astype(BF16)
    for w, o in ((w0, o0), (w1, o1), (w2, o2), (w3, o3), (w4, o4)):
        o[...] = _dot(u, w[...]).astype(BF16)


def _in_proj(h, g, ws):
    T = h.shape[0]
    tm = TOK_TILE
    widths = [w.shape[1] for w in ws]
    return pl.pallas_call(
        _in_proj_kernel,
        grid=(T // tm,),
        in_specs=[pl.BlockSpec((tm, D_MODEL), lambda i: (i, 0)), _full((1, D_MODEL))]
        + [pl.BlockSpec((D_MODEL, n), lambda i: (0, 0), pipeline_mode=pl.Buffered(1)) for n in widths],
        out_specs=[pl.BlockSpec((tm, n), lambda i: (i, 0)) for n in widths],
        out_shape=[jax.ShapeDtypeStruct((T, n), BF16) for n in widths],
        compiler_params=_params("parallel"),
        name="in_proj",
    )(h, g, *ws)


def _head_select(x, lane_head, n_heads, rows):
    out = None
    for hh in range(n_heads):
        term = jnp.where(lane_head == hh, x[hh * rows:(hh + 1) * rows], 0.0)
        out = term if out is None else out + term
    return out


def _head_norm_gate(o, gate_in, g):
    n = o.shape[1]
    same_head = (_iota((n, n), 0) >> 6) == (_iota((n, n), 1) >> 6)
    ms = _dot_f32_by_bf16(o * o, same_head.astype(BF16)) * (1.0 / 64.0)
    return o * lax.rsqrt(ms + NORM_EPS) * g * (gate_in * _sigmoid(gate_in))


def _gla_kernel(z_ref, wa_ref, ba_ref, g_ref, o_ref, st_ref, oacc_ref, att_ref, kv_ref, *, tiles_per_seq):
    C = GLA_CHUNK
    tt = z_ref.shape[0]

    @pl.when(pl.program_id(0) % tiles_per_seq == 0)
    def _():
        st_ref[...] = jnp.zeros_like(st_ref)

    q = z_ref[:, 0:128].astype(F32)
    k = z_ref[:, 128:256].astype(F32)
    r = z_ref[:, 512:768].astype(F32)
    pre = _dot_bf16_by_f32(z_ref[:, 768:896], wa_ref[...]) + ba_ref[...]
    log_a = (jnp.minimum(pre, 0.0) - jnp.log(1.0 + jnp.exp(-jnp.abs(pre)))) * (1.0 / GLA_TAU)

    lower = (_iota((C, C), 1) <= _iota((C, C), 0)).astype(BF16)
    b = jnp.concatenate([_dot_bf16_by_f32(lower, log_a[c * C:(c + 1) * C]) for c in range(tt // C)], axis=0)
    b_last = jnp.broadcast_to(b.reshape(tt // C, C, 128)[:, C - 1:C, :], (tt // C, C, 128)).reshape(tt, 128)
    q_dec = (q * (GLA_DK ** -0.5) * jnp.exp(b)).astype(BF16)
    k_dec = (k * jnp.exp(-b)).astype(BF16)
    k_end = (k * jnp.exp(b_last - b)).astype(BF16)
    decay = jnp.exp(b_last)

    qk_head = _iota((C, 128), 1) >> 5
    v_head = _iota((C, 256), 1) >> 6
    causal = _iota((4 * C, C), 1) <= (_iota((4 * C, C), 0) & (C - 1))
    state_mask = (_iota((256, 128), 0) >> 6) == (_iota((256, 128), 1) >> 5)

    for c in range(tt // C):
        sl = slice(c * C, (c + 1) * C)
        q_stack = jnp.concatenate([jnp.where(qk_head == hh, q_dec[sl], 0) for hh in range(GLA_HEADS)], axis=0)
        att_ref[c] = jnp.where(causal, _dot_nt(q_stack, k_dec[sl]), 0.0).astype(BF16)
        kv_ref[c] = jnp.where(state_mask, _dot_tn(z_ref[sl, 256:512], k_end[sl]), 0.0)
    state = st_ref[...]
    for c in range(tt // C):
        sl = slice(c * C, (c + 1) * C)
        intra = _head_select(_dot(att_ref[c], z_ref[sl, 256:512]), v_head, GLA_HEADS, C)
        inter = _dot_nt(q_dec[sl], state.astype(BF16))
        oacc_ref[sl, :] = intra + inter
        state = state * decay[c * C:c * C + 1] + kv_ref[c]
    st_ref[...] = state

    o_ref[...] = _head_norm_gate(oacc_ref[...], r, g_ref[...]).astype(BF16)


def _gla(z, wa, ba, g, seq):
    T = z.shape[0]
    tt = MIXER_TILE
    return pl.pallas_call(
        functools.partial(_gla_kernel, tiles_per_seq=seq // tt),
        grid=(T // tt,),
        in_specs=[pl.BlockSpec((tt, GLA_W), lambda i: (i, 0)), _full((128, 128)), _full((1, 128)),
                  _full((1, 256))],
        out_specs=pl.BlockSpec((tt, 256), lambda i: (i, 0)),
        out_shape=jax.ShapeDtypeStruct((T, 256), BF16),
        scratch_shapes=[pltpu.VMEM((256, 128), F32), pltpu.VMEM((tt, 256), F32),
                        pltpu.VMEM((tt // GLA_CHUNK, GLA_HEADS * GLA_CHUNK, GLA_CHUNK), BF16),
                        pltpu.VMEM((tt // GLA_CHUNK, 256, 128), F32)],
        compiler_params=_params("arbitrary"),
        name="gla_mixer",
    )(z, wa, ba, g)


_RET_LOG_GAMMA = [math.log1p(-(2.0 ** (-5.0 - hh))) for hh in range(RET_HEADS)]


def _by_head(head, values):
    out = jnp.full(head.shape, values[-1], F32)
    for hh in range(len(values) - 2, -1, -1):
        out = jnp.where(head == hh, values[hh], out)
    return out


def _ret_kernel(z_ref, cos_ref, sina_ref, sinb_ref, g_ref, o_ref, st_ref, oacc_ref, *, tiles_per_seq):
    C = RET_CHUNK
    tt = z_ref.shape[0]

    @pl.when(pl.program_id(0) % tiles_per_seq == 0)
    def _():
        st_ref[...] = jnp.zeros_like(st_ref)

    cos, sina, sinb = cos_ref[...], sina_ref[...], sinb_ref[...]

    def rotate(t):
        return t * cos + pltpu.roll(t, 112, 1) * sina + pltpu.roll(t, 16, 1) * sinb

    q_rot = rotate(z_ref[:, 0:128].astype(F32))
    k_rot = rotate(z_ref[:, 128:256].astype(F32)) * (RET_DK ** -0.5)
    gate_in = z_ref[:, 512:768].astype(F32)

    qk_head = _iota((C, 128), 1) >> 5
    v_head = _iota((C, 256), 1) >> 6
    lg_qk = _by_head(_iota((1, 128), 1) >> 5, _RET_LOG_GAMMA)
    lg_v = _by_head(_iota((1, 256), 1) >> 6, _RET_LOG_GAMMA)
    pos_qk = _iota((C, 128), 0).astype(F32)
    pos_v = _iota((C, 256), 0).astype(F32)
    key_decay = jnp.exp((C - 1.0 - pos_qk) * lg_qk)
    query_decay = jnp.exp((pos_v + 1.0) * lg_v)
    chunk_decay = jnp.exp(float(C) * lg_qk)
    srow = _iota((4 * C, C), 0)
    rel = ((srow & (C - 1)) - _iota((4 * C, C), 1)).astype(F32)
    decay_mat = jnp.where(rel >= 0, jnp.exp(jnp.maximum(rel, 0.0) * _by_head(srow >> 7, _RET_LOG_GAMMA)), 0.0)
    state_mask = (_iota((256, 128), 0) >> 6) == (_iota((256, 128), 1) >> 5)

    q_bf = q_rot.astype(BF16)
    k_bf = k_rot.astype(BF16)
    state = st_ref[...]
    for c in range(tt // C):
        sl = slice(c * C, (c + 1) * C)
        qc = q_bf[sl]
        v = z_ref[sl, 256:512]
        q_stack = jnp.concatenate([jnp.where(qk_head == hh, qc, 0) for hh in range(RET_HEADS)], axis=0)
        att = _dot_nt(q_stack, k_bf[sl]) * decay_mat
        intra = _head_select(_dot(att.astype(BF16), v), v_head, RET_HEADS, C)
        inter = _dot_nt(qc, state.astype(BF16)) * query_decay
        oacc_ref[sl, :] = intra + inter
        kv = _dot_tn(v, (k_rot[sl] * key_decay).astype(BF16))
        state = state * chunk_decay + jnp.where(state_mask, kv, 0.0)
    st_ref[...] = state

    o_ref[...] = _head_norm_gate(oacc_ref[...], gate_in, g_ref[...]).astype(BF16)


def _retention(z, cos, sina, sinb, g, seq):
    T = z.shape[0]
    tt = MIXER_TILE
    tps = seq // tt
    pos_spec = pl.BlockSpec((tt, 128), lambda i: (i % tps, 0))
    return pl.pallas_call(
        functools.partial(_ret_kernel, tiles_per_seq=tps),
        grid=(T // tt,),
        in_specs=[pl.BlockSpec((tt, RET_W), lambda i: (i, 0)), pos_spec, pos_spec, pos_spec, _full((1, 256))],
        out_specs=pl.BlockSpec((tt, 256), lambda i: (i, 0)),
        out_shape=jax.ShapeDtypeStruct((T, 256), BF16),
        scratch_shapes=[pltpu.VMEM((256, 128), F32), pltpu.VMEM((tt, 256), F32)],
        compiler_params=_params("arbitrary"),
        name="retention_mixer",
    )(z, cos, sina, sinb, g)


def _rope_tables(seq):
    half = RET_DK // 2
    pos = jnp.arange(seq, dtype=F32)
    inv_freq = RET_ROPE_BASE ** (-jnp.arange(half, dtype=F32) * 2.0 / RET_DK)
    ang = pos[:, None] * inv_freq[None, :]
    cos = jnp.tile(jnp.cos(ang), (1, LANES // half))
    sin = jnp.tile(jnp.sin(ang), (1, LANES // half))
    first_half = (jnp.arange(LANES) % RET_DK) < half
    return cos, jnp.where(first_half, -sin, 0.0), jnp.where(first_half, 0.0, sin)


def _swa_kernel(sink_ref, cur_ref, prev_ref, o_ref, s_ref, *, tiles_per_seq):
    W = SWA_WINDOW
    HD = SWA_HD
    tt = cur_ref.shape[0]
    first_key = jnp.where(pl.program_id(0) % tiles_per_seq == 0, W, 0)
    sj = _iota((2 * W, W), 0)
    qi = _iota((2 * W, W), 1)
    in_window = (sj > qi) & (sj <= qi + W)
    kv0 = SWA_HEADS * HD
    no_head = jnp.zeros((HD, W), BF16)
    n_blocks = tt // W

    def band(c, col0):
        prev = prev_ref[:, col0:col0 + 128] if c == 0 else cur_ref[(c - 1) * W:c * W, col0:col0 + 128]
        return jnp.concatenate([prev, cur_ref[c * W:(c + 1) * W, col0:col0 + 128]], axis=0)

    for c in range(n_blocks):
        mask = in_window & (sj >= first_key) if c == 0 else in_window
        k_band = band(c, kv0)
        q_t = cur_ref[c * W:(c + 1) * W, 0:kv0].astype(F32).T.astype(BF16)
        for hh in range(SWA_HEADS):
            kk = hh // (SWA_HEADS // SWA_KV_HEADS)
            qh_t = q_t[hh * HD:(hh + 1) * HD]
            q_on_kv = jnp.concatenate([qh_t, no_head] if kk == 0 else [no_head, qh_t], axis=0)
            s = _dot(k_band, q_on_kv) * (HD ** -0.5 * LOG2_E)
            s_ref[c, hh] = jnp.where(mask, s, NEG_INF)
    for c in range(n_blocks):
        v_t = band(c, kv0 + 128).astype(F32).T.astype(BF16)
        outs = []
        for hh in range(SWA_HEADS):
            kk = hh // (SWA_HEADS // SWA_KV_HEADS)
            s = s_ref[c, hh]
            sink = sink_ref[hh] * LOG2_E
            m = jnp.maximum(jnp.max(s, axis=0, keepdims=True), sink)
            p = jnp.exp2(s - m)
            denom = jnp.sum(p, axis=0, keepdims=True) + jnp.exp2(sink - m)
            outs.append(_dot(v_t[kk * HD:(kk + 1) * HD], p.astype(BF16)) / denom)
        o_ref[c * W:(c + 1) * W, :] = jnp.concatenate(outs, axis=0).T.astype(BF16)


def _swa(z, sinks, seq):
    T = z.shape[0]
    tt = MIXER_TILE
    per = tt // SWA_WINDOW
    return pl.pallas_call(
        functools.partial(_swa_kernel, tiles_per_seq=seq // tt),
        grid=(T // tt,),
        in_specs=[pl.BlockSpec(memory_space=pltpu.SMEM),
                  pl.BlockSpec((tt, SWA_W), lambda i: (i, 0)),
                  pl.BlockSpec((SWA_WINDOW, SWA_W), lambda i: (jnp.maximum(i * per - 1, 0), 0))],
        out_specs=pl.BlockSpec((tt, 256), lambda i: (i, 0)),
        out_shape=jax.ShapeDtypeStruct((T, 256), BF16),
        scratch_shapes=[pltpu.VMEM((per, SWA_HEADS, 2 * SWA_WINDOW, SWA_WINDOW), F32)],
        compiler_params=_params("parallel"),
        name="swa_mixer",
    )(sinks, z, z)


_KMEAN_BLOCKS = 32


def _kmean_kernel(k_ref, o_ref):
    k = k_ref[...].astype(F32).reshape(_KMEAN_BLOCKS, MOBA_BLOCK, 256)
    o_ref[...] = jnp.mean(k, axis=1)


def _moba_kmean(z):
    T = z.shape[0]
    rows = _KMEAN_BLOCKS * MOBA_BLOCK
    return pl.pallas_call(
        _kmean_kernel,
        grid=(T // rows,),
        in_specs=[pl.BlockSpec((rows, 256), lambda i: (i, 1))],
        out_specs=pl.BlockSpec((_KMEAN_BLOCKS, 256), lambda i: (i, 0)),
        out_shape=jax.ShapeDtypeStruct((T // MOBA_BLOCK, 256), F32),
        compiler_params=_params("parallel"),
        name="moba_kmean",
    )(z)


MOBA_VROWS = 80
MOBA_PREP_BLOCKS = 8


def _moba_prep_kernel(z_ref, km_ref, qt_out, k_out, vt_out):
    HD = MOBA_HD
    tq = MOBA_BLOCK
    nb = km_ref.shape[0]
    blk_i = _iota((nb, tq), 0)
    blk = blk_i.astype(F32)
    pad = LANES - HD - nb
    ones_row = (_iota((MOBA_VROWS - HD, tq), 0) == 0).astype(BF16)
    for sb in range(MOBA_PREP_BLOCKS):
        rows = slice(sb * tq, (sb + 1) * tq)
        qb = (pl.program_id(0) * MOBA_PREP_BLOCKS + sb) % nb
        this_block_lanes = (_iota((tq, nb), 1) == qb).astype(BF16)
        for hh in range(MOBA_HEADS):
            cols = slice(hh * HD, (hh + 1) * HD)
            q_t = z_ref[rows, cols].astype(F32).T
            gate = _dot(km_ref[:, cols], q_t, precision=HIGHEST)
            gate = jnp.where(blk_i < qb, gate, NEG_INF)
            keep = jnp.where(blk_i == qb, 1.0, 0.0)
            for _ in range(MOBA_TOPK):
                best = jnp.max(gate, axis=0, keepdims=True)
                first = jnp.min(jnp.where(gate == best, blk, float(nb)), axis=0, keepdims=True)
                hit = blk == first
                keep = keep + jnp.where(hit, jnp.where(best > 0.5 * NEG_INF, 1.0, 0.0), 0.0)
                gate = jnp.where(hit, BELOW_NEG_INF, gate)
            bias = ((keep - 1.0) * (-NEG_INF)).astype(BF16)
            qt_out[hh, sb] = jnp.concatenate(
                [(q_t * (HD ** -0.5 * LOG2_E)).astype(BF16), bias, jnp.zeros((pad, tq), BF16)], axis=0)
            k_out[hh, rows, :] = jnp.concatenate(
                [z_ref[rows, 256 + hh * HD:256 + (hh + 1) * HD], this_block_lanes, jnp.zeros((tq, pad), BF16)],
                axis=1)
            v_t = z_ref[rows, 512 + hh * HD:512 + (hh + 1) * HD].astype(F32).T.astype(BF16)
            vt_out[hh, sb] = jnp.concatenate([v_t, ones_row], axis=0)


def _moba_prep(z, kmean, seq):
    T = z.shape[0]
    nb = seq // MOBA_BLOCK
    assert MOBA_HD + nb <= LANES and nb % MOBA_PREP_BLOCKS == 0
    tq = MOBA_BLOCK
    tile = MOBA_PREP_BLOCKS * tq
    H = MOBA_HEADS

    def per_block_t(rows):
        return pl.BlockSpec((H, MOBA_PREP_BLOCKS, rows, tq), lambda i: (0, i, 0, 0))

    return pl.pallas_call(
        _moba_prep_kernel,
        grid=(T // tile,),
        in_specs=[pl.BlockSpec((tile, MOBA_W), lambda i: (i, 0)),
                  pl.BlockSpec((nb, 256), lambda i: (i * MOBA_PREP_BLOCKS // nb, 0))],
        out_specs=[per_block_t(LANES), pl.BlockSpec((H, tile, LANES), lambda i: (0, i, 0)),
                   per_block_t(MOBA_VROWS)],
        out_shape=[jax.ShapeDtypeStruct((H, T // tq, LANES, tq), BF16), jax.ShapeDtypeStruct((H, T, LANES), BF16),
                   jax.ShapeDtypeStruct((H, T // tq, MOBA_VROWS, tq), BF16)],
        compiler_params=_params("parallel"),
        name="moba_prep",
    )(z, kmean)


MOBA_QBLOCKS = 8
MOBA_KBLOCKS = 4
MOBA_DIAGONALS = MOBA_QBLOCKS // MOBA_KBLOCKS


def _moba_kernel(qidx_ref, kidx_ref, kind_ref, qt_ref, k_ref, vt_ref, o_ref, m_ref, acc_ref, s_ref, smax_ref):
    HD = MOBA_HD
    B = MOBA_BLOCK
    step = pl.program_id(1)
    kind = kind_ref[step]
    is_last = (step + 1 == pl.num_programs(1)) | (qidx_ref[jnp.minimum(step + 1, pl.num_programs(1) - 1)]
                                                  != qidx_ref[step])

    def attend(diagonal):
        causal = _iota((B, B), 0) <= _iota((B, B), 1)

        def key_blocks_of(qi):
            if diagonal is None:
                return list(range(MOBA_KBLOCKS))
            return [c for c in range(MOBA_KBLOCKS) if MOBA_KBLOCKS * diagonal + c <= qi]

        for hh in range(MOBA_HEADS):
            for qi in range(MOBA_QBLOCKS):
                for c in key_blocks_of(qi):
                    s = _dot(k_ref[hh, c * B:(c + 1) * B, :], qt_ref[hh, qi])
                    if diagonal is not None and MOBA_KBLOCKS * diagonal + c == qi:
                        s = jnp.where(causal, s, NEG_INF)
                    s_ref[hh, qi, c] = s
                    smax_ref[hh, qi, c] = jnp.max(s, axis=0, keepdims=True)
        first_visit = diagonal == 0
        for hh in range(MOBA_HEADS):
            for qi in range(MOBA_QBLOCKS):
                key_blocks = key_blocks_of(qi)
                if not key_blocks:
                    continue
                m_new = functools.reduce(jnp.maximum, [smax_ref[hh, qi, c] for c in key_blocks])
                if not first_visit:
                    m_old = m_ref[hh, qi]
                    m_new = jnp.maximum(m_old, m_new)
                pv = None
                for c in key_blocks:
                    p = jnp.exp2(s_ref[hh, qi, c] - m_new).astype(BF16)
                    term = _dot(vt_ref[hh, c], p)
                    pv = term if pv is None else pv + term
                acc_ref[hh, qi] = pv if first_visit else jnp.exp2(m_old - m_new) * acc_ref[hh, qi] + pv
                m_ref[hh, qi] = m_new

    @pl.when(kind == 0)
    def _():
        attend(None)

    for d in range(MOBA_DIAGONALS):
        @pl.when(kind == 1 + d)
        def _(d=d):
            attend(d)

    @pl.when(is_last)
    def _():
        for qi in range(MOBA_QBLOCKS):
            heads = [acc_ref[hh, qi, 0:HD, :] / acc_ref[hh, qi, HD:HD + 1, :] for hh in range(MOBA_HEADS)]
            o_ref[qi * B:(qi + 1) * B, :] = jnp.concatenate(heads, axis=0).T.astype(BF16)


def _moba(z, seq):
    T = z.shape[0]
    nb = seq // MOBA_BLOCK
    assert nb % MOBA_QBLOCKS == 0
    q_tiles = nb // MOBA_QBLOCKS
    k_steps = nb // MOBA_KBLOCKS
    D = MOBA_DIAGONALS
    batch = T // seq
    qt_aug, k_aug, vt_aug = _moba_prep(z, _moba_kmean(z), seq)
    qidx = np.concatenate([np.full(D * t + D, t) for t in range(q_tiles)]).astype(np.int32)
    kidx = np.concatenate([np.concatenate([D * t + np.arange(D), np.arange(D * t)])
                           for t in range(q_tiles)]).astype(np.int32)
    kind = np.concatenate([np.concatenate([1 + np.arange(D), np.zeros(D * t, np.int64)])
                           for t in range(q_tiles)]).astype(np.int32)
    tq = MOBA_QBLOCKS * MOBA_BLOCK
    tk = MOBA_KBLOCKS * MOBA_BLOCK
    H = MOBA_HEADS
    grid_spec = pltpu.PrefetchScalarGridSpec(
        num_scalar_prefetch=3,
        grid=(batch, len(qidx)),
        in_specs=[pl.BlockSpec((H, MOBA_QBLOCKS, LANES, MOBA_BLOCK),
                               lambda b, s, qi, ki, kd: (0, b * q_tiles + qi[s], 0, 0)),
                  pl.BlockSpec((H, tk, LANES), lambda b, s, qi, ki, kd: (0, b * k_steps + ki[s], 0)),
                  pl.BlockSpec((H, MOBA_KBLOCKS, MOBA_VROWS, MOBA_BLOCK),
                               lambda b, s, qi, ki, kd: (0, b * k_steps + ki[s], 0, 0))],
        out_specs=pl.BlockSpec((tq, 256), lambda b, s, qi, ki, kd: (b * q_tiles + qi[s], 0)),
        scratch_shapes=[pltpu.VMEM((H, MOBA_QBLOCKS, 1, MOBA_BLOCK), F32),
                        pltpu.VMEM((H, MOBA_QBLOCKS, MOBA_VROWS, MOBA_BLOCK), F32),
                        pltpu.VMEM((H, MOBA_QBLOCKS, MOBA_KBLOCKS, MOBA_BLOCK, MOBA_BLOCK), F32),
                        pltpu.VMEM((H, MOBA_QBLOCKS, MOBA_KBLOCKS, 1, MOBA_BLOCK), F32)],
    )
    return pl.pallas_call(
        _moba_kernel,
        grid_spec=grid_spec,
        out_shape=jax.ShapeDtypeStruct((T, 256), BF16),
        compiler_params=_params("parallel", "arbitrary"),
        name="moba_mixer",
    )(jnp.asarray(qidx), jnp.asarray(kidx), jnp.asarray(kind), qt_aug, k_aug, vt_aug)


def _merge_kernel(h_ref, o0, o1, o2, o3, half_gates_ref, half_wb_ref, wout_ref, out_ref):
    merged = None
    for i, o in enumerate((o0, o1, o2, o3)):
        half_y = _dot(o[...], half_wb_ref[i])
        term = half_y * jnp.tanh(half_gates_ref[:, i * D_MODEL:(i + 1) * D_MODEL].astype(F32)) + half_y
        merged = term if merged is None else merged + term
    out_ref[...] = h_ref[...] + _dot(merged.astype(BF16), wout_ref[...])


def _merge(h, branches, gates, wb, wout):
    T = h.shape[0]
    tm = TOK_TILE
    return pl.pallas_call(
        _merge_kernel,
        grid=(T // tm,),
        in_specs=[pl.BlockSpec((tm, D_MODEL), lambda i: (i, 0))]
        + [pl.BlockSpec((tm, 256), lambda i: (i, 0))] * 4
        + [pl.BlockSpec((tm, GATE_W), lambda i: (i, 0)),
           pl.BlockSpec((N_BRANCH, 256, D_MODEL), lambda i: (0, 0, 0), pipeline_mode=pl.Buffered(1)),
           pl.BlockSpec((D_MODEL, D_MODEL), lambda i: (0, 0), pipeline_mode=pl.Buffered(1))],
        out_specs=pl.BlockSpec((tm, D_MODEL), lambda i: (i, 0)),
        out_shape=jax.ShapeDtypeStruct((T, D_MODEL), F32),
        compiler_params=_params("parallel"),
        name="merge_out_proj",
    )(h, *branches, gates, wb, wout)


ROW_TILES = D_MODEL // (2 * LANES)
HIGH_HALF_MASK = -65536


def _slab_columns(j):
    first = 2 * j * LANES
    return slice(first, first + LANES), slice(first + LANES, first + 2 * LANES)


def _row_tile_spec(rows, index_map):
    return pl.BlockSpec((rows * ROW_TILES, LANES), index_map)


def _bf16_bits(x):
    return lax.bitcast_convert_type(x.astype(BF16).astype(F32), I32)


def _pack_rows(ref, x):
    for j in range(ROW_TILES):
        low_cols, high_cols = _slab_columns(j)
        low = _bf16_bits(x[:, low_cols])
        high = _bf16_bits(x[:, high_cols])
        ref[pl.ds(j, x.shape[0], stride=ROW_TILES), :] = high | lax.shift_right_logical(low, 16)


def _unpack_slab(ref, j):
    words = ref[pl.ds(j, ref.shape[0] // ROW_TILES, stride=ROW_TILES), :]
    low = lax.bitcast_convert_type(lax.shift_left(words, 16), F32)
    high = lax.bitcast_convert_type(words & HIGH_HALF_MASK, F32)
    return low, high


def _unpack_rows(ref):
    pieces = [piece for j in range(ROW_TILES) for piece in _unpack_slab(ref, j)]
    return jnp.concatenate(pieces, axis=1).astype(BF16)


def _router_kernel(h_ref, g_ref, wr_ref, br_ref, u_ref, route_ref, route_t_ref, count_ref, carry_ref):
    tm = h_ref.shape[0]

    @pl.when(pl.program_id(0) == 0)
    def _():
        carry_ref[...] = jnp.zeros_like(carry_ref)

    u = _rms(h_ref[...], g_ref[...])
    _pack_rows(u_ref, u)
    u_hi, u_lo = _split_bf16(u, 2)
    w_hi, w_lo = _split_bf16(wr_ref[...], 2)
    logits = _dot(u_hi, w_hi) + (_dot(u_lo, w_hi) + _dot(u_hi, w_lo)) + br_ref[...]
    lt = logits.T
    row_i = _iota((LANES, tm), 0)
    row = row_i.astype(F32)
    row_group = ((row_i - EXPERT_LANE0) >> 3).astype(F32)

    def first_row(mask):
        return jnp.min(jnp.where(mask, row, float(LANES)), axis=0, keepdims=True)

    grp = jnp.where(row_i < N_GROUPS, lt, BELOW_NEG_INF)
    grp_e = jnp.exp(grp - jnp.max(grp, axis=0, keepdims=True))
    p_group = grp_e / jnp.sum(grp_e, axis=0, keepdims=True)
    g_w = jnp.max(p_group, axis=0, keepdims=True)
    g_sel = first_row(p_group == g_w)

    in_group = (row_i >= EXPERT_LANE0) & (row_group == g_sel)
    el = jnp.where(in_group, lt, BELOW_NEG_INF)
    top1 = jnp.max(el, axis=0, keepdims=True)
    row1 = first_row(el == top1)
    el = jnp.where(row == row1, BELOW_NEG_INF, el)
    top2 = jnp.max(el, axis=0, keepdims=True)
    row2 = first_row(el == top2)
    e2 = jnp.exp(top2 - top1)
    w1 = g_w / (1.0 + e2)
    w2 = g_w * e2 / (1.0 + e2)

    onehot = ((row == row1) | (row == row2)).astype(BF16)
    earlier = (_iota((tm, tm), 0) < _iota((tm, tm), 1)).astype(BF16)
    seen = _dot(onehot, earlier) + carry_ref[:, 0:1]
    rank1 = jnp.sum(jnp.where(row == row1, seen, 0.0), axis=0, keepdims=True)
    rank2 = jnp.sum(jnp.where(row == row2, seen, 0.0), axis=0, keepdims=True)
    carry_ref[...] += jnp.sum(onehot.astype(F32), axis=1, keepdims=True)
    count_ref[...] = carry_ref[...]

    out_t = jnp.zeros((LANES, tm), F32)
    for idx, val in enumerate((row1 - EXPERT_LANE0, row2 - EXPERT_LANE0, w1, w2, rank1, rank2)):
        out_t = jnp.where(row_i == idx, val, out_t)
    route_t_ref[...] = out_t[0:8, :]
    route_ref[...] = out_t.T


def _router(h, g, wr, br):
    T = h.shape[0]
    tm = TOK_TILE
    return pl.pallas_call(
        _router_kernel,
        grid=(T // tm,),
        in_specs=[pl.BlockSpec((tm, D_MODEL), lambda i: (i, 0)), _full((1, D_MODEL)),
                  _full((D_MODEL, LANES)), _full((1, LANES))],
        out_specs=[_row_tile_spec(tm, lambda i: (i, 0)), pl.BlockSpec((tm, LANES), lambda i: (i, 0)),
                   pl.BlockSpec((8, tm), lambda i: (0, i)), _full((LANES, LANES))],
        out_shape=[jax.ShapeDtypeStruct((T * ROW_TILES, LANES), I32), jax.ShapeDtypeStruct((T, LANES), F32),
                   jax.ShapeDtypeStruct((8, T), F32), jax.ShapeDtypeStruct((LANES, LANES), F32)],
        scratch_shapes=[pltpu.VMEM((LANES, LANES), F32)],
        compiler_params=_params("arbitrary"),
        name="moe_router",
    )(h, g, wr, br)


ROW_UNROLL = 8


def _row_copy(src_ref, src_row, dst_ref, dst_row, sem):
    src = src_ref.at[pl.ds(pl.multiple_of(src_row * ROW_TILES, ROW_TILES), ROW_TILES)]
    dst = dst_ref.at[pl.ds(pl.multiple_of(dst_row * ROW_TILES, ROW_TILES), ROW_TILES)]
    return pltpu.make_async_copy(src, dst, sem)


def _for_each_row_pair(rows, body):
    def group(g, carry):
        for i in range(ROW_UNROLL):
            for k in range(EXPERT_TOPK):
                body(g * ROW_UNROLL + i, k)
        return carry

    lax.fori_loop(0, rows // ROW_UNROLL, group, 0)


def _dispatch_kernel(dest0_ref, dest1_ref, pstart_ref, pend_ref, u_ref, xs_ref, zero_ref, sem):
    tm = u_ref.shape[0] // ROW_TILES
    tile = pl.program_id(0)
    dests = (dest0_ref, dest1_ref)

    @pl.when(tile == 0)
    def _():
        zero_ref[...] = jnp.zeros_like(zero_ref)

        def zero_block(first_row):
            start = pl.multiple_of(first_row * ROW_TILES, ROW_TILES)
            fill = pltpu.make_async_copy(zero_ref, xs_ref.at[pl.ds(start, MOE_BLOCK * ROW_TILES)], sem)
            fill.start()
            fill.wait()

        def zero_tail(e, carry):
            @pl.when(pend_ref[e] > pstart_ref[e])
            def _():
                zero_block(pend_ref[e] - MOE_BLOCK)
            return carry

        def zero_unused(b, carry):
            zero_block(b * MOE_BLOCK)
            return carry

        lax.fori_loop(0, N_EXPERTS, zero_tail, 0)
        lax.fori_loop(pend_ref[N_EXPERTS - 1] // MOE_BLOCK, xs_ref.shape[0] // (MOE_BLOCK * ROW_TILES),
                      zero_unused, 0)

    def issue(r, k):
        _row_copy(u_ref, r, xs_ref, dests[k][tile * tm + r], sem).start(priority=k)

    _for_each_row_pair(tm, issue)
    for _ in range(EXPERT_TOPK):
        pltpu.make_async_copy(u_ref, xs_ref.at[pl.ds(0, tm * ROW_TILES)], sem).wait()


def _dispatch(dest0, dest1, pstart, pend, u, n_slots):
    T = u.shape[0] // ROW_TILES
    tm = DISPATCH_TILE
    grid_spec = pltpu.PrefetchScalarGridSpec(
        num_scalar_prefetch=4,
        grid=(T // tm,),
        in_specs=[_row_tile_spec(tm, lambda i, *_: (i, 0))],
        out_specs=pl.BlockSpec(memory_space=pl.ANY),
        scratch_shapes=[pltpu.VMEM((MOE_BLOCK * ROW_TILES, LANES), I32), pltpu.SemaphoreType.DMA(())],
    )
    return pl.pallas_call(
        _dispatch_kernel,
        grid_spec=grid_spec,
        out_shape=jax.ShapeDtypeStruct((n_slots * ROW_TILES, LANES), I32),
        compiler_params=_params("arbitrary"),
        name="moe_dispatch",
    )(dest0, dest1, pstart, pend, u)


def _expert_kernel(blk_e_ref, n_used_ref, x_ref, w1_ref, w3_ref, w2_ref, y_ref, w1b_ref, w3b_ref, w2b_ref):
    i = pl.program_id(0)
    used = i < n_used_ref[0]

    @pl.when((i == 0) | (blk_e_ref[i] != blk_e_ref[jnp.maximum(i - 1, 0)]))
    def _():
        w1b_ref[...] = w1_ref[0, 0].astype(BF16)
        w3b_ref[...] = w3_ref[0, 0].astype(BF16)
        w2b_ref[...] = w2_ref[0, 0].astype(BF16)

    @pl.when(used)
    def _():
        x = _unpack_rows(x_ref)
        a = _dot(x, w1b_ref[...])
        hidden = (a * _sigmoid(a)) * _dot(x, w3b_ref[...])
        _pack_rows(y_ref, _dot(hidden.astype(BF16), w2b_ref[...]))

    @pl.when(jnp.logical_not(used))
    def _():
        y_ref[...] = jnp.zeros_like(y_ref)


def _experts(blk_e, n_used, xs, w1, w3, w2, layer):
    n_slots = xs.shape[0] // ROW_TILES
    bm = MOE_BLOCK
    grid_spec = pltpu.PrefetchScalarGridSpec(
        num_scalar_prefetch=2,
        grid=(n_slots // bm,),
        in_specs=[_row_tile_spec(bm, lambda i, be, nu: (jnp.minimum(i, nu[0] - 1), 0)),
                  pl.BlockSpec((1, 1, D_MODEL, EXPERT_HIDDEN), lambda i, be, nu: (layer, be[i], 0, 0)),
                  pl.BlockSpec((1, 1, D_MODEL, EXPERT_HIDDEN), lambda i, be, nu: (layer, be[i], 0, 0)),
                  pl.BlockSpec((1, 1, EXPERT_HIDDEN, D_MODEL), lambda i, be, nu: (layer, be[i], 0, 0))],
        out_specs=_row_tile_spec(bm, lambda i, be, nu: (i, 0)),
        scratch_shapes=[pltpu.VMEM((D_MODEL, EXPERT_HIDDEN), BF16), pltpu.VMEM((D_MODEL, EXPERT_HIDDEN), BF16),
                        pltpu.VMEM((EXPERT_HIDDEN, D_MODEL), BF16)],
    )
    return pl.pallas_call(
        _expert_kernel,
        grid_spec=grid_spec,
        out_shape=jax.ShapeDtypeStruct((n_slots * ROW_TILES, LANES), I32),
        compiler_params=_params("arbitrary"),
        name="moe_experts",
    )(blk_e, n_used, xs, w1, w3, w2)


def _combine_kernel(dest0_ref, dest1_ref, h_ref, route_ref, g_ref, ys_ref, out_ref, y_ref, sem, *, normalize):
    tm = h_ref.shape[0]
    step = pl.program_id(0)
    slot = step % 2

    dests = (dest0_ref, dest1_ref)

    def gather_tile(tile, buf):
        def issue(r, k):
            _row_copy(ys_ref, dests[k][tile * tm + r], y_ref.at[buf, k], r, sem.at[buf]).start(priority=k)

        _for_each_row_pair(tm, issue)

    @pl.when(step == 0)
    def _():
        gather_tile(0, 0)

    @pl.when(step + 1 < pl.num_programs(0))
    def _():
        gather_tile(step + 1, 1 - slot)

    for k in range(EXPERT_TOPK):
        pltpu.make_async_copy(ys_ref.at[pl.ds(0, tm * ROW_TILES)], y_ref.at[slot, k], sem.at[slot]).wait()
    w0 = route_ref[:, 2:3]
    w1 = route_ref[:, 3:4]
    for j in range(ROW_TILES):
        low0, high0 = _unpack_slab(y_ref.at[slot, 0], j)
        low1, high1 = _unpack_slab(y_ref.at[slot, 1], j)
        low_cols, high_cols = _slab_columns(j)
        out_ref[:, low_cols] = h_ref[:, low_cols] + w0 * low0 + w1 * low1
        out_ref[:, high_cols] = h_ref[:, high_cols] + w0 * high0 + w1 * high1
    if normalize:
        out_ref[...] = _rms(out_ref[...], g_ref[...])


def _combine(dest0, dest1, h, route, ys, final_g, normalize):
    T = h.shape[0]
    tm = ROW_TILE
    grid_spec = pltpu.PrefetchScalarGridSpec(
        num_scalar_prefetch=2,
        grid=(T // tm,),
        in_specs=[pl.BlockSpec((tm, D_MODEL), lambda i, *_: (i, 0)),
                  pl.BlockSpec((tm, LANES), lambda i, *_: (i, 0)),
                  _full((1, D_MODEL)),
                  pl.BlockSpec(memory_space=pl.ANY)],
        out_specs=pl.BlockSpec((tm, D_MODEL), lambda i, *_: (i, 0)),
        scratch_shapes=[pltpu.VMEM((2, EXPERT_TOPK, tm * ROW_TILES, LANES), I32),
                        pltpu.SemaphoreType.DMA((2,))],
    )
    return pl.pallas_call(
        functools.partial(_combine_kernel, normalize=normalize),
        grid_spec=grid_spec,
        out_shape=jax.ShapeDtypeStruct((T, D_MODEL), F32),
        compiler_params=_params("arbitrary"),
        name="moe_combine",
    )(dest0, dest1, h, route, final_g, ys)


def _moe(h, g, w_group, b_group, w_expert, b_expert, w1, w3, w2, layer, final_g, normalize):
    T = h.shape[0]
    wr = jnp.zeros((D_MODEL, LANES), F32)
    wr = wr.at[:, :N_GROUPS].set(w_group).at[:, EXPERT_LANE0:EXPERT_LANE0 + N_EXPERTS].set(w_expert)
    br = jnp.zeros((1, LANES), F32)
    br = br.at[0, :N_GROUPS].set(b_group).at[0, EXPERT_LANE0:EXPERT_LANE0 + N_EXPERTS].set(b_expert)
    u, route, route_t, count = _router(h, g, wr, br)

    n_blocks = (T * EXPERT_TOPK) // MOE_BLOCK + N_EXPERTS
    counts = count[EXPERT_LANE0:EXPERT_LANE0 + N_EXPERTS, 0].astype(I32)
    padded = ((counts + MOE_BLOCK - 1) // MOE_BLOCK) * MOE_BLOCK
    pend = jnp.cumsum(padded)
    pstart = pend - padded
    expert_ids = jnp.arange(N_EXPERTS, dtype=F32)[:, None]

    def slot_of(expert_row, rank_row):
        segment_start = jnp.sum(jnp.where(expert_row[None, :] == expert_ids, pstart[:, None], 0), axis=0)
        return segment_start + rank_row.astype(I32)

    dest0 = slot_of(route_t[0], route_t[4])
    dest1 = slot_of(route_t[1], route_t[5])
    block_row0 = jnp.arange(n_blocks, dtype=I32) * MOE_BLOCK
    blk_e = jnp.minimum(jnp.sum((pend[None, :] <= block_row0[:, None]).astype(I32), axis=1), N_EXPERTS - 1)
    n_used = (pend[-1:] // MOE_BLOCK).astype(I32)

    xs = _dispatch(dest0, dest1, pstart, pend, u, n_blocks * MOE_BLOCK)
    ys = _experts(blk_e, n_used, xs, w1, w3, w2, layer)
    return _combine(dest0, dest1, h, route, ys, final_g, normalize)


def _split_w_in(w_in):
    gla_n = GLA_HEADS * (2 * GLA_DK + 2 * GLA_DV) + GLA_LOWRANK
    swa_n = (SWA_HEADS + 2 * SWA_KV_HEADS) * SWA_HD
    moba_n = 3 * MOBA_HEADS * MOBA_HD
    ret_n = RET_HEADS * (2 * RET_DK + 2 * RET_DV)
    offs = np.cumsum([0, gla_n, swa_n, moba_n, ret_n, GATE_W])
    parts = [w_in[:, offs[i]:offs[i + 1]].astype(BF16) for i in range(5)]
    parts[0] = jnp.pad(parts[0], ((0, 0), (0, GLA_W - gla_n)))
    parts[4] = parts[4] * 0.5
    return parts


def _token_mixer(h, seq, ln1_g, w_in, gla_w_a2, gla_b_a, gla_norm_g, swa_sinks, ret_norm_g, w_branch, w_out,
                 rope):
    z_gla, z_swa, z_moba, z_ret, half_gates = _in_proj(h, ln1_g.reshape(1, D_MODEL), _split_w_in(w_in))
    wa = jnp.zeros((128, 128), F32).at[:GLA_LOWRANK].set(gla_w_a2)
    o_gla = _gla(z_gla, wa, gla_b_a.reshape(1, 128), jnp.tile(gla_norm_g, GLA_HEADS).reshape(1, 256), seq)
    o_swa = _swa(z_swa, swa_sinks, seq)
    o_moba = _moba(z_moba, seq)
    o_ret = _retention(z_ret, *rope, jnp.tile(ret_norm_g, RET_HEADS).reshape(1, 256), seq)
    half_w_branch = w_branch.astype(BF16) * 0.5
    return _merge(h, (o_gla, o_swa, o_moba, o_ret), half_gates, half_w_branch, w_out.astype(BF16))


def kernel(x, ln1_g, w_in, gla_w_a2, gla_b_a, gla_norm_g, swa_sinks, ret_norm_g, w_branch, w_out, ln2_g,
           w_group, b_group, w_expert, b_expert, w1, w3, w2, final_g):
    batch, seq, _ = x.shape
    depth = w_in.shape[0]
    assert depth >= 1
    assert x.shape[2] == D_MODEL and seq % (MOBA_QBLOCKS * MOBA_BLOCK) == 0 and seq % MIXER_TILE == 0
    assert (batch * seq) % DISPATCH_TILE == 0 and (batch * seq * EXPERT_TOPK) % MOE_BLOCK == 0
    assert (batch * seq) % (_KMEAN_BLOCKS * MOBA_BLOCK) == 0
    rope = _rope_tables(seq)
    h = x.reshape(batch * seq, D_MODEL)
    for l in range(depth):
        h = _token_mixer(h, seq, ln1_g[l], w_in[l], gla_w_a2[l], gla_b_a[l], gla_norm_g[l], swa_sinks[l],
                         ret_norm_g[l], w_branch[l], w_out[l], rope)
        h = _moe(h, ln2_g[l].reshape(1, D_MODEL), w_group[l], b_group[l], w_expert[l], b_expert[l],
                 w1, w3, w2, l, final_g.reshape(1, D_MODEL), normalize=(l == depth - 1))
    return h.reshape(batch, seq, D_MODEL)
```
